```python
import jax, jax.numpy as jnp
from jax import lax
import numpy as np

D_MODEL = 1024
BATCH = 8
SEQ = 8192
DEPTH = 1

D_RNN = 1024
RNN_HEADS = 16
RNN_HEAD_DIM = D_RNN // RNN_HEADS
CONV_WIDTH = 4
RG_C = 8.0
D_SGU = 1024
SGU_GROUPS = 8
SGU_GROUP_DIM = D_SGU // SGU_GROUPS
CHUNK = 128
IN_COLS = 2 * D_RNN + 2 * D_SGU + 2 * D_MODEL
SPLIT_POINTS = (D_RNN, 2 * D_RNN, 2 * D_RNN + D_SGU, 2 * D_RNN + 2 * D_SGU, 2 * D_RNN + 2 * D_SGU + D_MODEL)
D_FF = -(-8 * D_MODEL // (3 * 256)) * 256
EPS = 1e-6

kernel_name = "hybrid_rglru_chunked_sgu_gated_block"


def rms_norm(x, g):
    xf = x.astype(jnp.float32)
    y = xf * lax.rsqrt(jnp.mean(xf * xf, axis=-1, keepdims=True) + EPS)
    return (y * g.astype(jnp.float32)).astype(x.dtype)


def layer_norm(x, g, b):
    xf = x.astype(jnp.float32)
    mu = jnp.mean(xf, axis=-1, keepdims=True)
    var = jnp.mean(jnp.square(xf - mu), axis=-1, keepdims=True)
    y = (xf - mu) * lax.rsqrt(var + EPS)
    return (y * g.astype(jnp.float32) + b.astype(jnp.float32)).astype(x.dtype)


def causal_depthwise_conv(x, w, b):
    s = x.shape[1]
    xp = jnp.pad(x, ((0, 0), (CONV_WIDTH - 1, 0), (0, 0)))
    y = b
    for k in range(CONV_WIDTH):
        y = y + xp[:, k:k + s, :] * w[k]
    return y


def rg_lru(x, w_a, b_a, w_x, b_x, lam):
    bsz, s, _ = x.shape
    xh = x.reshape(bsz, s, RNN_HEADS, RNN_HEAD_DIM)
    r = jax.nn.sigmoid(jnp.einsum('bshd,hde->bshe', xh, w_a) + b_a).reshape(bsz, s, D_RNN)
    i = jax.nn.sigmoid(jnp.einsum('bshd,hde->bshe', xh, w_x) + b_x).reshape(bsz, s, D_RNN)
    log_a = -RG_C * r.astype(jnp.float32) * jax.nn.softplus(-lam.astype(jnp.float32))
    a = jnp.exp(log_a)
    gated_x = jnp.sqrt(-jnp.expm1(2.0 * log_a)) * (i * x).astype(jnp.float32)

    def combine(c1, c2):
        a1, b1 = c1
        a2, b2 = c2
        return a1 * a2, a2 * b1 + b2

    _, h = lax.associative_scan(combine, (a, gated_x), axis=1)
    return h.astype(x.dtype)


def chunked_sgu(u, v, ln_g, ln_b, w_s, b_s):
    bsz, s, _ = v.shape
    n_chunks = s // CHUNK
    v = layer_norm(v, ln_g, ln_b)
    vc = v.reshape(bsz, n_chunks, CHUNK, SGU_GROUPS, SGU_GROUP_DIM)
    causal = jnp.tril(jnp.ones((CHUNK, CHUNK), dtype=bool))
    w = jnp.where(causal[None], w_s, jnp.zeros_like(w_s))
    mixed = jnp.einsum('gts,bnsgd->bntgd', w, vc) + b_s.T[None, None, :, :, None]
    return u * mixed.reshape(bsz, s, D_SGU)


def _fwd_setup_inputs(seed: int = 0) -> dict:
    key = jax.random.key(seed)
    ks = jax.random.split(key, 24)
    f32 = jnp.float32

    def nrm(k, shape, scale):
        return jax.random.normal(k, shape, f32) * scale

    a_c = jax.random.uniform(ks[10], (DEPTH, D_RNN), f32, 0.9, 0.999)
    p = a_c ** (1.0 / RG_C)
    rg_lambda = jnp.log(p) - jnp.log1p(-p)
    return {
        "x": nrm(ks[0], (BATCH, SEQ, D_MODEL), 1.0),
        "norm_mix_g": 1.0 + nrm(ks[1], (DEPTH, D_MODEL), 0.02),
        "w_in": nrm(ks[2], (DEPTH, D_MODEL, IN_COLS), D_MODEL ** -0.5),
        "conv_w": nrm(ks[3], (DEPTH, CONV_WIDTH, D_RNN), CONV_WIDTH ** -0.5),
        "conv_b": nrm(ks[4], (DEPTH, D_RNN), 0.01),
        "rg_wa": nrm(ks[5], (DEPTH, RNN_HEADS, RNN_HEAD_DIM, RNN_HEAD_DIM), RNN_HEAD_DIM ** -0.5),
        "rg_ba": nrm(ks[6], (DEPTH, RNN_HEADS, RNN_HEAD_DIM), 0.01),
        "rg_wx": nrm(ks[7], (DEPTH, RNN_HEADS, RNN_HEAD_DIM, RNN_HEAD_DIM), RNN_HEAD_DIM ** -0.5),
        "rg_bx": nrm(ks[8], (DEPTH, RNN_HEADS, RNN_HEAD_DIM), 0.01),
        "rg_lambda": rg_lambda,
        "sgu_ln_g": 1.0 + nrm(ks[11], (DEPTH, D_SGU), 0.02),
        "sgu_ln_b": nrm(ks[12], (DEPTH, D_SGU), 0.01),
        "sgu_ws": nrm(ks[13], (DEPTH, SGU_GROUPS, CHUNK, CHUNK), CHUNK ** -0.5),
        "sgu_bs": 1.0 + nrm(ks[14], (DEPTH, SGU_GROUPS, CHUNK), 0.02),
        "w_proj_a": nrm(ks[15], (DEPTH, D_RNN, D_MODEL), D_RNN ** -0.5),
        "w_proj_b": nrm(ks[16], (DEPTH, D_SGU, D_MODEL), D_SGU ** -0.5),
        "w_out": nrm(ks[17], (DEPTH, D_MODEL, D_MODEL), D_MODEL ** -0.5),
        "norm_ffn_g": 1.0 + nrm(ks[18], (DEPTH, D_MODEL), 0.02),
        "w_gate_up": nrm(ks[19], (DEPTH, D_MODEL, 2 * D_FF), D_MODEL ** -0.5),
        "w_down": nrm(ks[20], (DEPTH, D_FF, D_MODEL), D_FF ** -0.5),
        "norm_final_g": 1.0 + nrm(ks[21], (D_MODEL,), 0.02),
    }


def _fwd_reference(x, norm_mix_g, w_in, conv_w, conv_b, rg_wa, rg_ba, rg_wx, rg_bx, rg_lambda,
              sgu_ln_g, sgu_ln_b, sgu_ws, sgu_bs, w_proj_a, w_proj_b, w_out,
              norm_ffn_g, w_gate_up, w_down, norm_final_g):
    for l in range(DEPTH):
        h = rms_norm(x, norm_mix_g[l])
        proj = h @ w_in[l]
        rnn_x, rnn_gate, sgu_u, sgu_v, gate_a, gate_b = jnp.split(proj, SPLIT_POINTS, axis=-1)
        rnn_x = causal_depthwise_conv(rnn_x, conv_w[l], conv_b[l])
        y_a = jax.nn.gelu(rnn_gate) * rg_lru(rnn_x, rg_wa[l], rg_ba[l], rg_wx[l], rg_bx[l], rg_lambda[l])
        y_b = chunked_sgu(jax.nn.gelu(sgu_u), jax.nn.gelu(sgu_v), sgu_ln_g[l], sgu_ln_b[l], sgu_ws[l], sgu_bs[l])
        merged = jax.nn.sigmoid(gate_a) * (y_a @ w_proj_a[l]) + jax.nn.sigmoid(gate_b) * (y_b @ w_proj_b[l])
        x = x + merged @ w_out[l]
        h = rms_norm(x, norm_ffn_g[l])
        g, u = jnp.split(h @ w_gate_up[l], 2, axis=-1)
        x = x + (jax.nn.silu(g) * u) @ w_down[l]
    return rms_norm(x, norm_final_g)


import jax as _jax
import jax.numpy as _jnp

TWIN_FORMAT = 'train_step'
FWD_PARAMS = ['x', 'norm_mix_g', 'w_in', 'conv_w', 'conv_b', 'rg_wa', 'rg_ba', 'rg_wx', 'rg_bx', 'rg_lambda', 'sgu_ln_g', 'sgu_ln_b', 'sgu_ws', 'sgu_bs', 'w_proj_a', 'w_proj_b', 'w_out', 'norm_ffn_g', 'w_gate_up', 'w_down', 'norm_final_g']
TWIN_WEIGHTS = ['norm_mix_g', 'w_in', 'conv_w', 'conv_b', 'rg_wa', 'rg_ba', 'rg_wx', 'rg_bx', 'rg_lambda', 'sgu_ln_g', 'sgu_ln_b', 'sgu_ws', 'sgu_bs', 'w_proj_a', 'w_proj_b', 'w_out', 'norm_ffn_g', 'w_gate_up', 'w_down', 'norm_final_g']
TWIN_DIFF_INPUT = 'x'
TWIN_INPUTS = ['x', 'norm_mix_g', 'w_in', 'conv_w', 'conv_b', 'rg_wa', 'rg_ba', 'rg_wx', 'rg_bx', 'rg_lambda', 'sgu_ln_g', 'sgu_ln_b', 'sgu_ws', 'sgu_bs', 'w_proj_a', 'w_proj_b', 'w_out', 'norm_ffn_g', 'w_gate_up', 'w_down', 'norm_final_g', 'loss_target', 'm_norm_mix_g', 'm_w_in', 'm_conv_w', 'm_conv_b', 'm_rg_wa', 'm_rg_ba', 'm_rg_wx', 'm_rg_bx', 'm_rg_lambda', 'm_sgu_ln_g', 'm_sgu_ln_b', 'm_sgu_ws', 'm_sgu_bs', 'm_w_proj_a', 'm_w_proj_b', 'm_w_out', 'm_norm_ffn_g', 'm_w_gate_up', 'm_w_down', 'm_norm_final_g', 'v_norm_mix_g', 'v_w_in', 'v_conv_w', 'v_conv_b', 'v_rg_wa', 'v_rg_ba', 'v_rg_wx', 'v_rg_bx', 'v_rg_lambda', 'v_sgu_ln_g', 'v_sgu_ln_b', 'v_sgu_ws', 'v_sgu_bs', 'v_w_proj_a', 'v_w_proj_b', 'v_w_out', 'v_norm_ffn_g', 'v_w_gate_up', 'v_w_down', 'v_norm_final_g']
TWIN_OUTPUTS = ['loss', 'grad_x', 'grad_norm_mix_g', 'grad_w_in', 'grad_conv_w', 'grad_conv_b', 'grad_rg_wa', 'grad_rg_ba', 'grad_rg_wx', 'grad_rg_bx', 'grad_rg_lambda', 'grad_sgu_ln_g', 'grad_sgu_ln_b', 'grad_sgu_ws', 'grad_sgu_bs', 'grad_w_proj_a', 'grad_w_proj_b', 'grad_w_out', 'grad_norm_ffn_g', 'grad_w_gate_up', 'grad_w_down', 'grad_norm_final_g', 'delta_norm_mix_g', 'delta_w_in', 'delta_conv_w', 'delta_conv_b', 'delta_rg_wa', 'delta_rg_ba', 'delta_rg_wx', 'delta_rg_bx', 'delta_rg_lambda', 'delta_sgu_ln_g', 'delta_sgu_ln_b', 'delta_sgu_ws', 'delta_sgu_bs', 'delta_w_proj_a', 'delta_w_proj_b', 'delta_w_out', 'delta_norm_ffn_g', 'delta_w_gate_up', 'delta_w_down', 'delta_norm_final_g', 'new_m_norm_mix_g', 'new_m_w_in', 'new_m_conv_w', 'new_m_conv_b', 'new_m_rg_wa', 'new_m_rg_ba', 'new_m_rg_wx', 'new_m_rg_bx', 'new_m_rg_lambda', 'new_m_sgu_ln_g', 'new_m_sgu_ln_b', 'new_m_sgu_ws', 'new_m_sgu_bs', 'new_m_w_proj_a', 'new_m_w_proj_b', 'new_m_w_out', 'new_m_norm_ffn_g', 'new_m_w_gate_up', 'new_m_w_down', 'new_m_norm_final_g', 'new_v_norm_mix_g', 'new_v_w_in', 'new_v_conv_w', 'new_v_conv_b', 'new_v_rg_wa', 'new_v_rg_ba', 'new_v_rg_wx', 'new_v_rg_bx', 'new_v_rg_lambda', 'new_v_sgu_ln_g', 'new_v_sgu_ln_b', 'new_v_sgu_ws', 'new_v_sgu_bs', 'new_v_w_proj_a', 'new_v_w_proj_b', 'new_v_w_out', 'new_v_norm_ffn_g', 'new_v_w_gate_up', 'new_v_w_down', 'new_v_norm_final_g']
TWIN_LEAF_KINDS = {'loss': 'loss', 'grad_x': 'grad_x', 'grad_norm_mix_g': 'grad_w', 'grad_w_in': 'grad_w', 'grad_conv_w': 'grad_w', 'grad_conv_b': 'grad_w', 'grad_rg_wa': 'grad_w', 'grad_rg_ba': 'grad_w', 'grad_rg_wx': 'grad_w', 'grad_rg_bx': 'grad_w', 'grad_rg_lambda': 'grad_w', 'grad_sgu_ln_g': 'grad_w', 'grad_sgu_ln_b': 'grad_w', 'grad_sgu_ws': 'grad_w', 'grad_sgu_bs': 'grad_w', 'grad_w_proj_a': 'grad_w', 'grad_w_proj_b': 'grad_w', 'grad_w_out': 'grad_w', 'grad_norm_ffn_g': 'grad_w', 'grad_w_gate_up': 'grad_w', 'grad_w_down': 'grad_w', 'grad_norm_final_g': 'grad_w', 'delta_norm_mix_g': 'delta_w', 'delta_w_in': 'delta_w', 'delta_conv_w': 'delta_w', 'delta_conv_b': 'delta_w', 'delta_rg_wa': 'delta_w', 'delta_rg_ba': 'delta_w', 'delta_rg_wx': 'delta_w', 'delta_rg_bx': 'delta_w', 'delta_rg_lambda': 'delta_w', 'delta_sgu_ln_g': 'delta_w', 'delta_sgu_ln_b': 'delta_w', 'delta_sgu_ws': 'delta_w', 'delta_sgu_bs': 'delta_w', 'delta_w_proj_a': 'delta_w', 'delta_w_proj_b': 'delta_w', 'delta_w_out': 'delta_w', 'delta_norm_ffn_g': 'delta_w', 'delta_w_gate_up': 'delta_w', 'delta_w_down': 'delta_w', 'delta_norm_final_g': 'delta_w', 'new_m_norm_mix_g': 'new_m', 'new_m_w_in': 'new_m', 'new_m_conv_w': 'new_m', 'new_m_conv_b': 'new_m', 'new_m_rg_wa': 'new_m', 'new_m_rg_ba': 'new_m', 'new_m_rg_wx': 'new_m', 'new_m_rg_bx': 'new_m', 'new_m_rg_lambda': 'new_m', 'new_m_sgu_ln_g': 'new_m', 'new_m_sgu_ln_b': 'new_m', 'new_m_sgu_ws': 'new_m', 'new_m_sgu_bs': 'new_m', 'new_m_w_proj_a': 'new_m', 'new_m_w_proj_b': 'new_m', 'new_m_w_out': 'new_m', 'new_m_norm_ffn_g': 'new_m', 'new_m_w_gate_up': 'new_m', 'new_m_w_down': 'new_m', 'new_m_norm_final_g': 'new_m', 'new_v_norm_mix_g': 'new_v', 'new_v_w_in': 'new_v', 'new_v_conv_w': 'new_v', 'new_v_conv_b': 'new_v', 'new_v_rg_wa': 'new_v', 'new_v_rg_ba': 'new_v', 'new_v_rg_wx': 'new_v', 'new_v_rg_bx': 'new_v', 'new_v_rg_lambda': 'new_v', 'new_v_sgu_ln_g': 'new_v', 'new_v_sgu_ln_b': 'new_v', 'new_v_sgu_ws': 'new_v', 'new_v_sgu_bs': 'new_v', 'new_v_w_proj_a': 'new_v', 'new_v_w_proj_b': 'new_v', 'new_v_w_out': 'new_v', 'new_v_norm_ffn_g': 'new_v', 'new_v_w_gate_up': 'new_v', 'new_v_w_down': 'new_v', 'new_v_norm_final_g': 'new_v'}


def _forward(args):
    return _fwd_reference(*[args[k] for k in FWD_PARAMS])


def _output_shape():
    def fwd():
        inp = _fwd_setup_inputs(0)
        return _fwd_reference(*[inp[k] for k in FWD_PARAMS])
    out = _jax.eval_shape(fwd)
    return out.shape, out.dtype

N_MICROBATCH = 1
ADAM_LR = 0.001
ADAM_B1 = 0.9
ADAM_B2 = 0.999
ADAM_EPS = 1e-08
ADAM_WD = 0.01
ADAM_STEP = 10
PER_EXAMPLE_BATCH_AXIS = {'x': 0, 'loss_target': 0}
SHARED_INPUTS = []
_WEIGHT_DTYPES = {'norm_mix_g': _jnp.float32, 'w_in': _jnp.float32, 'conv_w': _jnp.float32, 'conv_b': _jnp.float32, 'rg_wa': _jnp.float32, 'rg_ba': _jnp.float32, 'rg_wx': _jnp.float32, 'rg_bx': _jnp.float32, 'rg_lambda': _jnp.float32, 'sgu_ln_g': _jnp.float32, 'sgu_ln_b': _jnp.float32, 'sgu_ws': _jnp.float32, 'sgu_bs': _jnp.float32, 'w_proj_a': _jnp.float32, 'w_proj_b': _jnp.float32, 'w_out': _jnp.float32, 'norm_ffn_g': _jnp.float32, 'w_gate_up': _jnp.float32, 'w_down': _jnp.float32, 'norm_final_g': _jnp.float32}
MOMENT_SCALE = {'norm_mix_g': 1.797140e-01, 'w_in': 6.772386e-02, 'conv_w': 7.401759e-02, 'conv_b': 8.172570e-01, 'rg_wa': 3.057869e-02, 'rg_ba': 2.192893e-02, 'rg_wx': 5.434456e-02, 'rg_bx': 2.273989e-02, 'rg_lambda': 3.298721e-02, 'sgu_ln_g': 6.425725e-02, 'sgu_ln_b': 6.322325e-02, 'sgu_ws': 6.253346e-02, 'sgu_bs': 8.996302e-02, 'w_proj_a': 6.955954e-02, 'w_proj_b': 1.118114e-01, 'w_out': 1.292674e-01, 'norm_ffn_g': 1.719476e-01, 'w_gate_up': 7.281383e-02, 'w_down': 1.186069e-01, 'norm_final_g': 6.405701e+01}


def _to_microbatches(a, axis):
    t = _jnp.moveaxis(a, axis, 0)
    t = t.reshape((N_MICROBATCH, t.shape[0] // N_MICROBATCH) + t.shape[1:])
    return _jnp.moveaxis(t, 1, axis + 1)


def setup_inputs(seed: int = 0) -> dict:
    inp = _fwd_setup_inputs(seed)
    key = _jax.random.fold_in(_jax.random.key(seed), 7919)
    shape, _ = _output_shape()
    out = dict(inp)
    out["loss_target"] = _jax.random.normal(_jax.random.fold_in(key, 0), shape, _jnp.float32)
    for i, name in enumerate(TWIN_WEIGHTS):
        w = inp[name].astype(_jnp.float32)
        if MOMENT_SCALE is None:
            s = _jnp.sqrt(_jnp.mean(_jnp.square(w)) + 1e-30)
        else:
            s = MOMENT_SCALE[name]
        km, kv = _jax.random.split(_jax.random.fold_in(key, i + 1))
        out[name] = w
        out["m_" + name] = s * _jax.random.normal(km, w.shape, _jnp.float32)
        out["v_" + name] = (s * s) * _jax.random.uniform(kv, w.shape, _jnp.float32, 0.5, 1.5)
    if N_MICROBATCH > 1:
        for name, axis in PER_EXAMPLE_BATCH_AXIS.items():
            out[name] = _to_microbatches(out[name], axis)
    return {'x': out['x'], 'norm_mix_g': out['norm_mix_g'], 'w_in': out['w_in'], 'conv_w': out['conv_w'], 'conv_b': out['conv_b'], 'rg_wa': out['rg_wa'], 'rg_ba': out['rg_ba'], 'rg_wx': out['rg_wx'], 'rg_bx': out['rg_bx'], 'rg_lambda': out['rg_lambda'], 'sgu_ln_g': out['sgu_ln_g'], 'sgu_ln_b': out['sgu_ln_b'], 'sgu_ws': out['sgu_ws'], 'sgu_bs': out['sgu_bs'], 'w_proj_a': out['w_proj_a'], 'w_proj_b': out['w_proj_b'], 'w_out': out['w_out'], 'norm_ffn_g': out['norm_ffn_g'], 'w_gate_up': out['w_gate_up'], 'w_down': out['w_down'], 'norm_final_g': out['norm_final_g'], 'loss_target': out['loss_target'], 'm_norm_mix_g': out['m_norm_mix_g'], 'm_w_in': out['m_w_in'], 'm_conv_w': out['m_conv_w'], 'm_conv_b': out['m_conv_b'], 'm_rg_wa': out['m_rg_wa'], 'm_rg_ba': out['m_rg_ba'], 'm_rg_wx': out['m_rg_wx'], 'm_rg_bx': out['m_rg_bx'], 'm_rg_lambda': out['m_rg_lambda'], 'm_sgu_ln_g': out['m_sgu_ln_g'], 'm_sgu_ln_b': out['m_sgu_ln_b'], 'm_sgu_ws': out['m_sgu_ws'], 'm_sgu_bs': out['m_sgu_bs'], 'm_w_proj_a': out['m_w_proj_a'], 'm_w_proj_b': out['m_w_proj_b'], 'm_w_out': out['m_w_out'], 'm_norm_ffn_g': out['m_norm_ffn_g'], 'm_w_gate_up': out['m_w_gate_up'], 'm_w_down': out['m_w_down'], 'm_norm_final_g': out['m_norm_final_g'], 'v_norm_mix_g': out['v_norm_mix_g'], 'v_w_in': out['v_w_in'], 'v_conv_w': out['v_conv_w'], 'v_conv_b': out['v_conv_b'], 'v_rg_wa': out['v_rg_wa'], 'v_rg_ba': out['v_rg_ba'], 'v_rg_wx': out['v_rg_wx'], 'v_rg_bx': out['v_rg_bx'], 'v_rg_lambda': out['v_rg_lambda'], 'v_sgu_ln_g': out['v_sgu_ln_g'], 'v_sgu_ln_b': out['v_sgu_ln_b'], 'v_sgu_ws': out['v_sgu_ws'], 'v_sgu_bs': out['v_sgu_bs'], 'v_w_proj_a': out['v_w_proj_a'], 'v_w_proj_b': out['v_w_proj_b'], 'v_w_out': out['v_w_out'], 'v_norm_ffn_g': out['v_norm_ffn_g'], 'v_w_gate_up': out['v_w_gate_up'], 'v_w_down': out['v_w_down'], 'v_norm_final_g': out['v_norm_final_g']}


def _loss(weights, diff, rest, loss_target):
    with _jax.named_scope("forward"):
        args = {**rest, TWIN_DIFF_INPUT: diff, **{k: w.astype(_WEIGHT_DTYPES[k]) for k, w in weights.items()}}
        y = _forward(args)
    with _jax.named_scope("loss_head"):
        err = _jnp.square(y.astype(_jnp.float32) - loss_target)
        return 0.5 * _jnp.sum(_jnp.mean(err, axis=-1)) if err.ndim else 0.5 * err


def _adamw(w, g, m, v):
    m = ADAM_B1 * m + (1.0 - ADAM_B1) * g
    v = ADAM_B2 * v + (1.0 - ADAM_B2) * _jnp.square(g)
    m_hat = m / (1.0 - ADAM_B1 ** ADAM_STEP)
    v_hat = v / (1.0 - ADAM_B2 ** ADAM_STEP)
    delta = -ADAM_LR * (m_hat / (_jnp.sqrt(v_hat) + ADAM_EPS) + ADAM_WD * w)
    return delta, m, v


def reference(x, norm_mix_g, w_in, conv_w, conv_b, rg_wa, rg_ba, rg_wx, rg_bx, rg_lambda, sgu_ln_g, sgu_ln_b, sgu_ws, sgu_bs, w_proj_a, w_proj_b, w_out, norm_ffn_g, w_gate_up, w_down, norm_final_g, loss_target, m_norm_mix_g, m_w_in, m_conv_w, m_conv_b, m_rg_wa, m_rg_ba, m_rg_wx, m_rg_bx, m_rg_lambda, m_sgu_ln_g, m_sgu_ln_b, m_sgu_ws, m_sgu_bs, m_w_proj_a, m_w_proj_b, m_w_out, m_norm_ffn_g, m_w_gate_up, m_w_down, m_norm_final_g, v_norm_mix_g, v_w_in, v_conv_w, v_conv_b, v_rg_wa, v_rg_ba, v_rg_wx, v_rg_bx, v_rg_lambda, v_sgu_ln_g, v_sgu_ln_b, v_sgu_ws, v_sgu_bs, v_w_proj_a, v_w_proj_b, v_w_out, v_norm_ffn_g, v_w_gate_up, v_w_down, v_norm_final_g):
    given = dict(x=x, norm_mix_g=norm_mix_g, w_in=w_in, conv_w=conv_w, conv_b=conv_b, rg_wa=rg_wa, rg_ba=rg_ba, rg_wx=rg_wx, rg_bx=rg_bx, rg_lambda=rg_lambda, sgu_ln_g=sgu_ln_g, sgu_ln_b=sgu_ln_b, sgu_ws=sgu_ws, sgu_bs=sgu_bs, w_proj_a=w_proj_a, w_proj_b=w_proj_b, w_out=w_out, norm_ffn_g=norm_ffn_g, w_gate_up=w_gate_up, w_down=w_down, norm_final_g=norm_final_g, loss_target=loss_target, m_norm_mix_g=m_norm_mix_g, m_w_in=m_w_in, m_conv_w=m_conv_w, m_conv_b=m_conv_b, m_rg_wa=m_rg_wa, m_rg_ba=m_rg_ba, m_rg_wx=m_rg_wx, m_rg_bx=m_rg_bx, m_rg_lambda=m_rg_lambda, m_sgu_ln_g=m_sgu_ln_g, m_sgu_ln_b=m_sgu_ln_b, m_sgu_ws=m_sgu_ws, m_sgu_bs=m_sgu_bs, m_w_proj_a=m_w_proj_a, m_w_proj_b=m_w_proj_b, m_w_out=m_w_out, m_norm_ffn_g=m_norm_ffn_g, m_w_gate_up=m_w_gate_up, m_w_down=m_w_down, m_norm_final_g=m_norm_final_g, v_norm_mix_g=v_norm_mix_g, v_w_in=v_w_in, v_conv_w=v_conv_w, v_conv_b=v_conv_b, v_rg_wa=v_rg_wa, v_rg_ba=v_rg_ba, v_rg_wx=v_rg_wx, v_rg_bx=v_rg_bx, v_rg_lambda=v_rg_lambda, v_sgu_ln_g=v_sgu_ln_g, v_sgu_ln_b=v_sgu_ln_b, v_sgu_ws=v_sgu_ws, v_sgu_bs=v_sgu_bs, v_w_proj_a=v_w_proj_a, v_w_proj_b=v_w_proj_b, v_w_out=v_w_out, v_norm_ffn_g=v_norm_ffn_g, v_w_gate_up=v_w_gate_up, v_w_down=v_w_down, v_norm_final_g=v_norm_final_g)
    weights = {n: given[n] for n in TWIN_WEIGHTS}
    shared = {n: given[n] for n in SHARED_INPUTS}
    per_example = {n: given[n] for n in ['x']}
    grad_fn = _jax.value_and_grad(_loss, argnums=(0, 1))

    def one_microbatch(ex, loss_target):
        ex = dict(ex)
        diff = ex.pop(TWIN_DIFF_INPUT)
        return grad_fn(weights, diff, {**shared, **ex}, loss_target)

    if N_MICROBATCH == 1:
        loss, (grad_w, grad_x) = one_microbatch(per_example, given["loss_target"])
    else:
        def body(carry, xs):
            loss_sum, grad_sum = carry
            l_k, (gw_k, gx_k) = one_microbatch(xs[0], xs[1])
            with _jax.named_scope("update"):
                return (loss_sum + l_k, _jax.tree.map(_jnp.add, grad_sum, gw_k)), gx_k

        init = (_jnp.zeros((), _jnp.float32), _jax.tree.map(_jnp.zeros_like, weights))
        (loss, grad_w), grad_x = _jax.lax.scan(body, init, (per_example, given["loss_target"]))
    with _jax.named_scope("update"):
        delta_w, new_m, new_v = {}, {}, {}
        for n in TWIN_WEIGHTS:
            delta_w[n], new_m[n], new_v[n] = _adamw(weights[n], grad_w[n], given["m_" + n], given["v_" + n])
    return (loss, grad_x, *[grad_w[n] for n in TWIN_WEIGHTS], *[delta_w[n] for n in TWIN_WEIGHTS],
            *[new_m[n] for n in TWIN_WEIGHTS], *[new_v[n] for n in TWIN_WEIGHTS])
```

```python
import functools

import jax
import jax.numpy as jnp
from jax import lax
from jax.experimental import pallas as pl
from jax.experimental.pallas import tpu as pltpu

F32 = jnp.float32
BF = jnp.bfloat16

D = 1024
NDEV = 8
EPS = 1e-6
RG_C = 8.0
CHUNK = 128
NGRP = 8
GW = 128
NQ = 4
QW = 256
RC = 16
SMALL_ROWS = 320
SMALL_PER = SMALL_ROWS // NDEV

ADAM_LR = 0.001
ADAM_B1 = 0.9
ADAM_B2 = 0.999
ADAM_EPS = 1e-08
ADAM_WD = 0.01
ADAM_STEP = 10

VMEM_LIMIT = 60 * 1024 * 1024

MESH = pl.DeviceIdType.MESH


def _rows(n, fn, rc=RC):
    def body(i, c):
        fn(pl.multiple_of(i * rc, rc))
        return c
    lax.fori_loop(0, n // rc, body, 0)


def _dot(a, b):
    return jnp.dot(a, b, preferred_element_type=F32)


def _dot_nt(a, b):
    return lax.dot_general(a, b, (((1,), (1,)), ((), ())), preferred_element_type=F32)


def _dot_tn(a, b):
    return lax.dot_general(a, b, (((0,), (0,)), ((), ())), preferred_element_type=F32)


_GC = 0.7978845608028654
_GK = 0.044715


def _gelu(x):
    t = jnp.tanh(_GC * (x + _GK * (x * x * x)))
    return x * (0.5 * (1.0 + t))


def _gelu_grad(x):
    x2 = x * x
    t = jnp.tanh(_GC * (x + _GK * (x2 * x)))
    cdf = 0.5 * (1.0 + t)
    dg = cdf + (0.5 * x) * (1.0 - t * t) * (_GC * (1.0 + (3.0 * _GK) * x2))
    return x * cdf, dg


def _sigmoid(x):
    return jax.nn.sigmoid(x)


def _log1p(e):
    u = 1.0 + e
    d = u - 1.0
    return jnp.where(d == 0.0, e, jnp.log(u) * (e / jnp.where(d == 0.0, 1.0, d)))


def _softplus(z):
    return jnp.maximum(z, 0.0) + _log1p(jnp.exp(-jnp.abs(z)))


def _neg_expm1(z):
    u = jnp.exp(z)
    lu = jnp.log(u)
    k = (1.0 - u) * (z / jnp.where(lu == 0.0, 1.0, lu))
    small = jnp.where(lu == 0.0, -z, k)
    return jnp.where(z > -0.5, small, 1.0 - u)


def _shift_back(prev8, cur, j):
    cat = jnp.concatenate([prev8, cur], axis=0)
    return pltpu.roll(cat, j, 0)[8:8 + cur.shape[0]]


def _shift_fwd(cur, next8, j):
    cat = jnp.concatenate([cur, next8], axis=0)
    n = cat.shape[0]
    return pltpu.roll(cat, n - j, 0)[0:cur.shape[0]]


def _const_spec(shape):
    nd = len(shape)
    return pl.BlockSpec(shape, lambda *_: (0,) * nd, pipeline_mode=pl.Buffered(1))


def _params(sem):
    return pltpu.CompilerParams(dimension_semantics=sem, vmem_limit_bytes=VMEM_LIMIT)


def _gate_parts(z_ref, r0, q, ba, bx, sp):
    cs = slice(q * QW, (q + 1) * QW)
    za = z_ref[pl.ds(r0, RC), q * 2 * QW:q * 2 * QW + QW] + ba[:, cs]
    zx = z_ref[pl.ds(r0, RC), q * 2 * QW + QW:(q + 1) * 2 * QW] + bx[:, cs]
    r = _sigmoid(za)
    ig = _sigmoid(zx)
    la = (-RG_C * r) * sp[:, cs]
    return r, ig, la


TM_PROJ = 512
TM_MIX = 256
TM_FFN = 256
TM_DX = 512
TS_DW = 1024


def _proj_call(x, gmix, w_in_all):
    s = x.shape[0]
    tm = min(TM_PROJ, s)
    nb, _, wb = w_in_all.shape

    def body(x_ref, g_ref, w_ref, proj_ref, h1_ref):
        g = g_ref[...]

        def norm(r0):
            xx = x_ref[pl.ds(r0, RC), :]
            r = lax.rsqrt(jnp.mean(xx * xx, axis=-1, keepdims=True) + EPS)
            h1_ref[pl.ds(r0, RC), :] = ((xx * r) * g).astype(BF)
        _rows(tm, norm)
        h = h1_ref[...]
        for k in range(nb):
            proj_ref[:, k * wb:(k + 1) * wb] = _dot(h, w_ref[k])

    return pl.pallas_call(
        body, name="proj", grid=(s // tm,),
        in_specs=[pl.BlockSpec((tm, D), lambda i: (i, 0)), _const_spec((1, D)), _const_spec(w_in_all.shape)],
        out_specs=[pl.BlockSpec((tm, nb * wb), lambda i: (i, 0)), pl.BlockSpec((tm, D), lambda i: (i, 0))],
        out_shape=[jax.ShapeDtypeStruct((s, nb * wb), F32), jax.ShapeDtypeStruct((s, D), BF)],
        compiler_params=_params(("parallel",)),
    )(x, gmix, w_in_all)


def _conv_chunk(src_ref, ext, xc_s, xcb_s, cw_ref, cb, r0):
    cur = src_ref[pl.ds(r0, RC), 0:D]
    ext[pl.ds(r0 + 8, RC), :] = cur
    pv = ext[pl.ds(r0, 8), :]
    xc = cb + cw_ref[3:4, :] * cur
    for j in (1, 2, 3):
        xc = xc + cw_ref[3 - j:4 - j, :] * _shift_back(pv, cur, j)
    xc_s[pl.ds(r0, RC), :] = xc
    xcb_s[pl.ds(r0, RC), :] = xc.astype(BF)


def _mixer_fwd_call(proj, x, cw, cb, ba, bx, lam, lng, lnb, wax, wtr, bias, wpa, wpb, wo):
    s = x.shape[0]
    tm = min(TM_MIX, s)
    nt = s // tm
    pw = proj.shape[1]

    def body(proj_ref, x_ref, cw_ref, cb_ref, ba_ref, bx_ref, lam_ref, lng_ref, lnb_ref, wax_ref, wtr_ref,
             bias_ref, wpa_ref, wpb_ref, wo_ref,
             h_ref, pa_ref, pb_ref, x1_ref, ya_ref, yb_ref, mg_ref,
             ext, xc_s, xcb_s, z_s, a_s, b_s, hc_s, vn_s, mx_s):
        i = pl.program_id(0)

        @pl.when(i == 0)
        def _():
            ext[0:8, :] = jnp.zeros((8, D), F32)
            hc_s[...] = jnp.zeros((8, D), F32)

        @pl.when(i > 0)
        def _():
            ext[0:8, :] = ext[tm:tm + 8, :]

        cb = cb_ref[...]
        _rows(tm, lambda r0: _conv_chunk(proj_ref, ext, xc_s, xcb_s, cw_ref, cb, r0))

        for q in range(NQ):
            z_s[:, q * 2 * QW:(q + 1) * 2 * QW] = _dot(xcb_s[:, q * QW:(q + 1) * QW], wax_ref[q])

        sp = _softplus(-lam_ref[...])
        ba = ba_ref[...]
        bx = bx_ref[...]

        def gates(r0):
            for q in range(NQ):
                cs = slice(q * QW, (q + 1) * QW)
                _, ig, la = _gate_parts(z_s, r0, q, ba, bx, sp)
                a_s[pl.ds(r0, RC), cs] = jnp.exp(la)
                b_s[pl.ds(r0, RC), cs] = jnp.sqrt(_neg_expm1(2.0 * la)) * (ig * xc_s[pl.ds(r0, RC), cs])
        _rows(tm, gates)

        def scan(t, h):
            h = a_s[pl.ds(t, 1), :] * h + b_s[pl.ds(t, 1), :]
            h_ref[pl.ds(t, 1), :] = h
            return h
        hc_s[0:1, :] = lax.fori_loop(0, tm, scan, hc_s[0:1, :], unroll=8)

        def branch_a(r0):
            rg = proj_ref[pl.ds(r0, RC), D:2 * D]
            ya_ref[pl.ds(r0, RC), :] = (_gelu(rg) * h_ref[pl.ds(r0, RC), :]).astype(BF)
        _rows(tm, branch_a)

        lng = lng_ref[...]
        lnb = lnb_ref[...]

        def lnorm(r0):
            gv = _gelu(proj_ref[pl.ds(r0, RC), 3 * D:4 * D])
            mu = jnp.mean(gv, axis=-1, keepdims=True)
            dv = gv - mu
            var = jnp.mean(dv * dv, axis=-1, keepdims=True)
            vn_s[pl.ds(r0, RC), :] = ((dv * lax.rsqrt(var + EPS)) * lng + lnb).astype(BF)
        _rows(tm, lnorm)

        for c in range(tm // CHUNK):
            rs = slice(c * CHUNK, (c + 1) * CHUNK)
            for g in range(NGRP):
                cs = slice(g * GW, (g + 1) * GW)
                mx_s[rs, cs] = _dot(wtr_ref[g], vn_s[rs, cs])

        def branch_b(r0):
            u = proj_ref[pl.ds(r0, RC), 2 * D:3 * D]
            boff = pl.multiple_of(lax.rem(r0, CHUNK), RC)
            mixed = mx_s[pl.ds(r0, RC), :] + bias_ref[pl.ds(boff, RC), :]
            yb_ref[pl.ds(r0, RC), :] = (_gelu(u) * mixed).astype(BF)
        _rows(tm, branch_b)

        pa_ref[...] = _dot(ya_ref[...], wpa_ref[...])
        pb_ref[...] = _dot(yb_ref[...], wpb_ref[...])

        def merge(r0):
            ga = proj_ref[pl.ds(r0, RC), 4 * D:5 * D]
            gb = proj_ref[pl.ds(r0, RC), 5 * D:6 * D]
            mg = _sigmoid(ga) * pa_ref[pl.ds(r0, RC), :] + _sigmoid(gb) * pb_ref[pl.ds(r0, RC), :]
            mg_ref[pl.ds(r0, RC), :] = mg.astype(BF)
        _rows(tm, merge)

        x1_ref[...] = x_ref[...] + _dot(mg_ref[...], wo_ref[...])

    tile = lambda w: pl.BlockSpec((tm, w), lambda i: (i, 0))
    vec = _const_spec((1, D))
    return pl.pallas_call(
        body, name="mixer_fwd", grid=(nt,),
        in_specs=[tile(pw), tile(D), _const_spec((4, D)), vec, vec, vec, vec, vec, vec,
                  _const_spec(wax.shape), _const_spec(wtr.shape), _const_spec(bias.shape),
                  _const_spec((D, D)), _const_spec((D, D)), _const_spec((D, D))],
        out_specs=[tile(D)] * 7,
        out_shape=[jax.ShapeDtypeStruct((s, D), F32)] * 4 + [jax.ShapeDtypeStruct((s, D), BF)] * 3,
        scratch_shapes=[pltpu.VMEM((tm + 8, D), F32), pltpu.VMEM((tm, D), F32), pltpu.VMEM((tm, D), BF),
                        pltpu.VMEM((tm, 2 * D), F32), pltpu.VMEM((tm, D), F32), pltpu.VMEM((tm, D), F32),
                        pltpu.VMEM((8, D), F32), pltpu.VMEM((tm, D), BF), pltpu.VMEM((tm, D), F32)],
        compiler_params=_params(("arbitrary",)),
    )(proj, x, cw, cb, ba, bx, lam, lng, lnb, wax, wtr, bias, wpa, wpb, wo)


def _ffn_call(x1, tgt, gffn, gfin, wgu, wdn):
    s = x1.shape[0]
    tm = min(TM_FFN, s)
    nt = s // tm
    nh = wdn.shape[0]
    fb = wgu.shape[2]

    def body(x1_ref, tgt_ref, gffn_ref, gfin_ref, wgu_ref, wdn_ref,
             dx1_ref, dx1b_ref, h2_ref, act_ref, dgu_ref, dx2b_ref, acc_ref,
             g_s, u_s, x2_s, dx2_s, da_s, dh2_s, accs):
        i = pl.program_id(0)

        @pl.when(i == 0)
        def _():
            accs[...] = jnp.zeros(accs.shape, F32)

        gffn = gffn_ref[...]
        gfin = gfin_ref[...]

        def norm(r0):
            xx = x1_ref[pl.ds(r0, RC), :]
            r = lax.rsqrt(jnp.mean(xx * xx, axis=-1, keepdims=True) + EPS)
            h2_ref[pl.ds(r0, RC), :] = ((xx * r) * gffn).astype(BF)
        _rows(tm, norm)

        h2 = h2_ref[...]
        for k in range(nh):
            g_s[k] = _dot(h2, wgu_ref[k])
            u_s[k] = _dot(h2, wgu_ref[k + nh])

            def act(r0, k=k):
                g = g_s[k, pl.ds(r0, RC), :]
                act_ref[k, pl.ds(r0, RC), :] = ((g * _sigmoid(g)) * u_s[k, pl.ds(r0, RC), :]).astype(BF)
            _rows(tm, act)

        x2 = x1_ref[...]
        for k in range(nh):
            x2 = x2 + _dot(act_ref[k], wdn_ref[k])
        x2_s[...] = x2

        def head(r0):
            xx = x2_s[pl.ds(r0, RC), :]
            r = lax.rsqrt(jnp.mean(xx * xx, axis=-1, keepdims=True) + EPS)
            xh = xx * r
            err = xh * gfin - tgt_ref[pl.ds(r0, RC), :]
            accs[2] += err * err
            dy = err * (1.0 / D)
            accs[1] += dy * xh
            dxh = dy * gfin
            dx2 = r * (dxh - xh * jnp.mean(dxh * xh, axis=-1, keepdims=True))
            dx2_s[pl.ds(r0, RC), :] = dx2
            dx2b_ref[pl.ds(r0, RC), :] = dx2.astype(BF)
        _rows(tm, head)

        dx2b = dx2b_ref[...]
        for k in range(nh):
            da_s[...] = _dot_nt(dx2b, wdn_ref[k])

            def dact(r0, k=k):
                g = g_s[k, pl.ds(r0, RC), :]
                u = u_s[k, pl.ds(r0, RC), :]
                da = da_s[pl.ds(r0, RC), :]
                sg = _sigmoid(g)
                dgu_ref[k, pl.ds(r0, RC), :] = ((da * u) * (sg * (1.0 + g * (1.0 - sg)))).astype(BF)
                dgu_ref[k + nh, pl.ds(r0, RC), :] = (da * (g * sg)).astype(BF)
            _rows(tm, dact)

        dh2 = _dot_nt(dgu_ref[0], wgu_ref[0])
        for k in range(1, 2 * nh):
            dh2 = dh2 + _dot_nt(dgu_ref[k], wgu_ref[k])
        dh2_s[...] = dh2

        def tail(r0):
            xx = x1_ref[pl.ds(r0, RC), :]
            r = lax.rsqrt(jnp.mean(xx * xx, axis=-1, keepdims=True) + EPS)
            xh = xx * r
            dh = dh2_s[pl.ds(r0, RC), :]
            accs[0] += dh * xh
            dxh = dh * gffn
            dx1 = dx2_s[pl.ds(r0, RC), :] + r * (dxh - xh * jnp.mean(dxh * xh, axis=-1, keepdims=True))
            dx1_ref[pl.ds(r0, RC), :] = dx1
            dx1b_ref[pl.ds(r0, RC), :] = dx1.astype(BF)
        _rows(tm, tail)

        @pl.when(i == nt - 1)
        def _():
            acc_ref[...] = jnp.zeros((8, D), F32)
            for j in range(3):
                acc_ref[j:j + 1, :] = jnp.sum(accs[j], axis=0, keepdims=True)

    tile = lambda w: pl.BlockSpec((tm, w), lambda i: (i, 0))
    vec = _const_spec((1, D))
    return pl.pallas_call(
        body, name="ffn", grid=(nt,),
        in_specs=[tile(D), tile(D), vec, vec, _const_spec(wgu.shape), _const_spec(wdn.shape)],
        out_specs=[tile(D), tile(D), tile(D),
                   pl.BlockSpec((nh, tm, fb), lambda i: (0, i, 0)),
                   pl.BlockSpec((2 * nh, tm, fb), lambda i: (0, i, 0)),
                   tile(D), pl.BlockSpec((8, D), lambda i: (0, 0))],
        out_shape=[jax.ShapeDtypeStruct((s, D), F32), jax.ShapeDtypeStruct((s, D), BF),
                   jax.ShapeDtypeStruct((s, D), BF), jax.ShapeDtypeStruct((nh, s, fb), BF),
                   jax.ShapeDtypeStruct((2 * nh, s, fb), BF), jax.ShapeDtypeStruct((s, D), BF),
                   jax.ShapeDtypeStruct((8, D), F32)],
        scratch_shapes=[pltpu.VMEM((nh, tm, fb), F32), pltpu.VMEM((nh, tm, fb), F32), pltpu.VMEM((tm, D), F32),
                        pltpu.VMEM((tm, D), F32), pltpu.VMEM((tm, fb), F32), pltpu.VMEM((tm, D), F32),
                        pltpu.VMEM((3, RC, D), F32)],
        compiler_params=_params(("arbitrary",)),
    )(x1, tgt, gffn, gfin, wgu, wdn)


def _mixer_bwd_pre_call(dx1b, proj, pa, pb, lng, lnb, wtr, wtrt, bias, wpa, wpb, wo):
    s = dx1b.shape[0]
    tm = min(TM_MIX, s)
    nt = s // tm
    pw = proj.shape[1]

    def body(dx1b_ref, uv_ref, gg_ref, pa_ref, pb_ref, lng_ref, lnb_ref, wtr_ref, wtrt_ref, bias_ref,
             wpa_ref, wpb_ref, wo_ref,
             dproj_ref, dya_ref, dpa_ref, dpb_ref, vec_ref, dws_ref, dbs_ref,
             dm_s, dyb_s, vn_s, xh_s, rs_s, mx_s, dmx_s, dvn_s, accs, dbs_s):
        i = pl.program_id(0)

        @pl.when(i == 0)
        def _():
            accs[...] = jnp.zeros(accs.shape, F32)
            dbs_s[...] = jnp.zeros(dbs_s.shape, F32)
            dws_ref[...] = jnp.zeros(dws_ref.shape, F32)

        dm_s[...] = _dot_nt(dx1b_ref[...], wo_ref[...])

        def merge(r0):
            rows = pl.ds(r0, RC)
            dm = dm_s[rows, :]
            sa = _sigmoid(gg_ref[rows, 0:D])
            sb = _sigmoid(gg_ref[rows, D:2 * D])
            dpa = dm * sa
            dpb = dm * sb
            dpa_ref[rows, :] = dpa.astype(BF)
            dpb_ref[rows, :] = dpb.astype(BF)
            dproj_ref[rows, 4 * D:5 * D] = ((dpa * pa_ref[rows, :]) * (1.0 - sa)).astype(BF)
            dproj_ref[rows, 5 * D:6 * D] = ((dpb * pb_ref[rows, :]) * (1.0 - sb)).astype(BF)
            dproj_ref[rows, 0:2 * D] = jnp.zeros((RC, 2 * D), BF)
        _rows(tm, merge)

        dya_ref[...] = _dot_nt(dpa_ref[...], wpa_ref[...])
        dyb_s[...] = _dot_nt(dpb_ref[...], wpb_ref[...])

        lng = lng_ref[...]
        lnb = lnb_ref[...]

        def lnorm(r0):
            rows = pl.ds(r0, RC)
            gv = _gelu(uv_ref[rows, D:2 * D])
            mu = jnp.mean(gv, axis=-1, keepdims=True)
            dv = gv - mu
            var = jnp.mean(dv * dv, axis=-1, keepdims=True)
            rstd = lax.rsqrt(var + EPS)
            xh = dv * rstd
            xh_s[rows, :] = xh
            rs_s[rows, :] = jnp.broadcast_to(rstd, (RC, 128))
            vn_s[rows, :] = (xh * lng + lnb).astype(BF)
        _rows(tm, lnorm)

        for c in range(tm // CHUNK):
            rs = slice(c * CHUNK, (c + 1) * CHUNK)
            for g in range(NGRP):
                cs = slice(g * GW, (g + 1) * GW)
                mx_s[rs, cs] = _dot(wtr_ref[g], vn_s[rs, cs])

        def gate_u(r0):
            rows = pl.ds(r0, RC)
            boff = pl.multiple_of(lax.rem(r0, CHUNK), RC)
            gu, dgu = _gelu_grad(uv_ref[rows, 0:D])
            dyb = dyb_s[rows, :]
            mixed = mx_s[rows, :] + bias_ref[pl.ds(boff, RC), :]
            dproj_ref[rows, 2 * D:3 * D] = ((dyb * mixed) * dgu).astype(BF)
            dmx = dyb * gu
            dmx_s[rows, :] = dmx.astype(BF)
            dbs_s[pl.ds(boff, RC), :] += dmx
        _rows(tm, gate_u)

        for c in range(tm // CHUNK):
            rs = slice(c * CHUNK, (c + 1) * CHUNK)
            for g in range(NGRP):
                cs = slice(g * GW, (g + 1) * GW)
                dvn_s[rs, cs] = _dot(wtrt_ref[g], dmx_s[rs, cs])
                dws_ref[g] += _dot_nt(dmx_s[rs, cs], vn_s[rs, cs])

        def lnorm_bwd(r0):
            rows = pl.ds(r0, RC)
            dvn = dvn_s[rows, :]
            xh = xh_s[rows, :]
            accs[0] += dvn * xh
            accs[1] += dvn
            dxh = dvn * lng
            m1 = jnp.mean(dxh, axis=-1, keepdims=True)
            m2 = jnp.mean(dxh * xh, axis=-1, keepdims=True)
            dgv = rs_s[rows, 0:1] * (dxh - m1 - xh * m2)
            _, dg = _gelu_grad(uv_ref[rows, D:2 * D])
            dproj_ref[rows, 3 * D:4 * D] = (dgv * dg).astype(BF)
        _rows(tm, lnorm_bwd)

        @pl.when(i == nt - 1)
        def _():
            vec_ref[...] = jnp.zeros((8, D), F32)
            for j in range(2):
                vec_ref[j:j + 1, :] = jnp.sum(accs[j], axis=0, keepdims=True)
            row = lax.broadcasted_iota(jnp.int32, (CHUNK, CHUNK), 0)
            col = lax.broadcasted_iota(jnp.int32, (CHUNK, CHUNK), 1)
            for g in range(NGRP):
                dws_ref[g] = jnp.where(row >= col, dws_ref[g], 0.0)
                gs = jnp.sum(dbs_s[:, g * GW:(g + 1) * GW], axis=1, keepdims=True)
                dbs_ref[:, g * GW:(g + 1) * GW] = jnp.broadcast_to(gs, (CHUNK, GW))

    tile = lambda w: pl.BlockSpec((tm, w), lambda i: (i, 0))
    vec = _const_spec((1, D))
    return pl.pallas_call(
        body, name="mixer_bwd_pre", grid=(nt,),
        in_specs=[tile(D), pl.BlockSpec((tm, 2 * D), lambda i: (i, 1)), pl.BlockSpec((tm, 2 * D), lambda i: (i, 2)),
                  tile(D), tile(D), vec, vec, _const_spec(wtr.shape), _const_spec(wtrt.shape),
                  _const_spec(bias.shape), _const_spec((D, D)), _const_spec((D, D)), _const_spec((D, D))],
        out_specs=[tile(pw), tile(D), tile(D), tile(D), pl.BlockSpec((8, D), lambda i: (0, 0)),
                   pl.BlockSpec((NGRP, CHUNK, CHUNK), lambda i: (0, 0, 0)),
                   pl.BlockSpec((CHUNK, D), lambda i: (0, 0))],
        out_shape=[jax.ShapeDtypeStruct((s, pw), BF), jax.ShapeDtypeStruct((s, D), F32),
                   jax.ShapeDtypeStruct((s, D), BF), jax.ShapeDtypeStruct((s, D), BF),
                   jax.ShapeDtypeStruct((8, D), F32), jax.ShapeDtypeStruct((NGRP, CHUNK, CHUNK), F32),
                   jax.ShapeDtypeStruct((CHUNK, D), F32)],
        scratch_shapes=[pltpu.VMEM((tm, D), F32), pltpu.VMEM((tm, D), F32), pltpu.VMEM((tm, D), BF),
                        pltpu.VMEM((tm, D), F32), pltpu.VMEM((tm, 128), F32), pltpu.VMEM((tm, D), F32),
                        pltpu.VMEM((tm, D), BF), pltpu.VMEM((tm, D), F32), pltpu.VMEM((2, RC, D), F32),
                        pltpu.VMEM((CHUNK, D), F32)],
        compiler_params=_params(("arbitrary",)),
    )(dx1b, proj, proj, pa, pb, lng, lnb, wtr, wtrt, bias, wpa, wpb, wo)


def _mixer_bwd_seq_call(dproj, dya, proj, h, cw, cb, ba, bx, lam, wax):
    s = dya.shape[0]
    tm = min(TM_MIX, s)
    nt = s // tm
    tb = tm // 8

    def body(dproj_any, dya_ref, xg_ref, xh8_ref, h_ref, hh8_ref, cw_ref, cb_ref, ba_ref, bx_ref, lam_ref, wax_ref,
             dpa_ref, vec_ref, dwax_ref,
             ext, hext, xc_s, xcb_s, z_s, a_s, lm_s, dz_s, dxc_s, c_s, accs):
        del dproj_any
        i = pl.program_id(0)

        @pl.when(i == 0)
        def _():
            accs[...] = jnp.zeros(accs.shape, F32)
            dwax_ref[...] = jnp.zeros(dwax_ref.shape, F32)
            c_s[...] = jnp.zeros((8, D), F32)
            dxc_s[tm:tm + 8, :] = jnp.zeros((8, D), F32)

        @pl.when(i == nt - 1)
        def _():
            ext[0:8, :] = jnp.zeros((8, D), F32)
            hext[0:8, :] = jnp.zeros((8, D), F32)

        @pl.when(i < nt - 1)
        def _():
            ext[0:8, :] = xh8_ref[...]
            hext[0:8, :] = hh8_ref[...]

        cb = cb_ref[...]
        _rows(tm, lambda r0: _conv_chunk(xg_ref, ext, xc_s, xcb_s, cw_ref, cb, r0))

        for q in range(NQ):
            z_s[:, q * 2 * QW:(q + 1) * 2 * QW] = _dot(xcb_s[:, q * QW:(q + 1) * QW], wax_ref[q])

        lamv = lam_ref[...]
        sp = _softplus(-lamv)
        ba = ba_ref[...]
        bx = bx_ref[...]

        def prep(r0):
            rows = pl.ds(r0, RC)
            for q in range(NQ):
                cs = slice(q * QW, (q + 1) * QW)
                _, _, la = _gate_parts(z_s, r0, q, ba, bx, sp)
                a_s[rows, cs] = jnp.exp(la)
            hv = h_ref[rows, :]
            hext[pl.ds(r0 + 8, RC), :] = hv
            g, dg = _gelu_grad(xg_ref[rows, D:2 * D])
            dya = dya_ref[rows, :]
            lm_s[rows, :] = dya * g
            dpa_ref[rows, D:2 * D] = ((dya * hv) * dg).astype(BF)
        _rows(tm, prep)

        def scan(k, c):
            t = tm - 1 - k
            lm = lm_s[pl.ds(t, 1), :] + c
            lm_s[pl.ds(t, 1), :] = lm
            return a_s[pl.ds(t, 1), :] * lm
        c_s[0:1, :] = lax.fori_loop(0, tm, scan, c_s[0:1, :], unroll=8)

        def gate_bwd(r0):
            rows = pl.ds(r0, RC)
            hprev = _shift_back(hext[pl.ds(r0, 8), :], hext[pl.ds(r0 + 8, RC), :], 1)
            for q in range(NQ):
                cs = slice(q * QW, (q + 1) * QW)
                r, ig, la = _gate_parts(z_s, r0, q, ba, bx, sp)
                a = a_s[rows, cs]
                m = jnp.sqrt(_neg_expm1(2.0 * la))
                lm = lm_s[rows, cs]
                xc = xc_s[rows, cs]
                dixc = lm * m
                dig = dixc * xc
                dla = (lm * hprev[:, cs]) * a - ((lm * (ig * xc)) * (a * a)) / m
                accs[3, :, cs] += dla * r
                dza = (dla * (-RG_C * sp[:, cs])) * (r * (1.0 - r))
                dzx = dig * (ig * (1.0 - ig))
                accs[1, :, cs] += dza
                accs[2, :, cs] += dzx
                dz_s[rows, q * 2 * QW:q * 2 * QW + QW] = dza.astype(BF)
                dz_s[rows, q * 2 * QW + QW:(q + 1) * 2 * QW] = dzx.astype(BF)
                dxc_s[rows, cs] = dixc * ig
        _rows(tm, gate_bwd)

        for q in range(NQ):
            cs = slice(q * QW, (q + 1) * QW)
            dzq = dz_s[:, q * 2 * QW:(q + 1) * 2 * QW]
            dxc_s[0:tm, cs] += _dot_nt(dzq, wax_ref[q])
            dwax_ref[q] += _dot_tn(xcb_s[:, cs], dzq)

        def conv_bwd(r0):
            rows = pl.ds(r0, RC)
            cur = dxc_s[rows, :]
            nx = dxc_s[pl.ds(r0 + RC, 8), :]
            drx = cw_ref[3:4, :] * cur
            for j in (1, 2, 3):
                drx = drx + cw_ref[3 - j:4 - j, :] * _shift_fwd(cur, nx, j)
            dpa_ref[rows, 0:D] = drx.astype(BF)
            accs[0] += cur
            xcur = ext[pl.ds(r0 + 8, RC), :]
            xpv = ext[pl.ds(r0, 8), :]
            accs[7] += cur * xcur
            for j in (1, 2, 3):
                accs[7 - j] += cur * _shift_back(xpv, xcur, j)
        _rows(tm, conv_bwd)

        dxc_s[tm:tm + 8, :] = dxc_s[0:8, :]

        @pl.when(i == nt - 1)
        def _():
            vec_ref[...] = jnp.zeros((8, D), F32)
            for j in range(8):
                vec_ref[j:j + 1, :] = jnp.sum(accs[j], axis=0, keepdims=True)
            vec_ref[3:4, :] = vec_ref[3:4, :] * (RG_C * _sigmoid(-lamv))

    rev = lambda w: pl.BlockSpec((tm, w), lambda i: (nt - 1 - i, 0))
    halo = pl.BlockSpec((8, D), lambda i: (jnp.maximum((nt - 1 - i) * tb - 1, 0), 0))
    vec = _const_spec((1, D))
    return pl.pallas_call(
        body, name="mixer_bwd_seq", grid=(nt,),
        in_specs=[pl.BlockSpec(memory_space=pl.ANY), rev(D), rev(2 * D), halo, rev(D), halo,
                  _const_spec((4, D)), vec, vec, vec, vec, _const_spec(wax.shape)],
        out_specs=[rev(2 * D), pl.BlockSpec((8, D), lambda i: (0, 0)),
                   pl.BlockSpec((NQ, QW, 2 * QW), lambda i: (0, 0, 0))],
        out_shape=[jax.ShapeDtypeStruct(dproj.shape, BF), jax.ShapeDtypeStruct((8, D), F32),
                   jax.ShapeDtypeStruct((NQ, QW, 2 * QW), F32)],
        input_output_aliases={0: 0},
        scratch_shapes=[pltpu.VMEM((tm + 8, D), F32), pltpu.VMEM((tm + 8, D), F32), pltpu.VMEM((tm, D), F32),
                        pltpu.VMEM((tm, D), BF), pltpu.VMEM((tm, 2 * D), F32), pltpu.VMEM((tm, D), F32),
                        pltpu.VMEM((tm, D), F32), pltpu.VMEM((tm, 2 * D), BF), pltpu.VMEM((tm + 8, D), F32),
                        pltpu.VMEM((8, D), F32), pltpu.VMEM((8, RC, D), F32)],
        compiler_params=_params(("arbitrary",)),
    )(dproj, dya, proj, proj, h, h, cw, cb, ba, bx, lam, wax)


def _dx_call(dproj, dx1, x, gmix, w_in_all):
    s = x.shape[0]
    tm = min(TM_DX, s)
    nt = s // tm
    nb, _, wb = w_in_all.shape

    def body(dp_ref, dx1_ref, x_ref, g_ref, w_ref, dx_ref, acc_ref, dh_s, accs):
        i = pl.program_id(0)

        @pl.when(i == 0)
        def _():
            accs[...] = jnp.zeros(accs.shape, F32)

        dh = _dot_nt(dp_ref[:, 0:wb], w_ref[0])
        for k in range(1, nb):
            dh = dh + _dot_nt(dp_ref[:, k * wb:(k + 1) * wb], w_ref[k])
        dh_s[...] = dh
        g = g_ref[...]

        def tail(r0):
            rows = pl.ds(r0, RC)
            xx = x_ref[rows, :]
            r = lax.rsqrt(jnp.mean(xx * xx, axis=-1, keepdims=True) + EPS)
            xh = xx * r
            d = dh_s[rows, :]
            accs[...] += d * xh
            dxh = d * g
            dx_ref[rows, :] = dx1_ref[rows, :] + r * (dxh - xh * jnp.mean(dxh * xh, axis=-1, keepdims=True))
        _rows(tm, tail)

        @pl.when(i == nt - 1)
        def _():
            acc_ref[...] = jnp.zeros((8, D), F32)
            acc_ref[0:1, :] = jnp.sum(accs[...], axis=0, keepdims=True)

    tile = lambda w: pl.BlockSpec((tm, w), lambda i: (i, 0))
    return pl.pallas_call(
        body, name="dx", grid=(nt,),
        in_specs=[tile(nb * wb), tile(D), tile(D), _const_spec((1, D)), _const_spec(w_in_all.shape)],
        out_specs=[tile(D), pl.BlockSpec((8, D), lambda i: (0, 0))],
        out_shape=[jax.ShapeDtypeStruct((s, D), F32), jax.ShapeDtypeStruct((8, D), F32)],
        scratch_shapes=[pltpu.VMEM((tm, D), F32), pltpu.VMEM((RC, D), F32)],
        compiler_params=_params(("arbitrary",)),
    )(dproj, dx1, x, gmix, w_in_all)


def _dw_call(name, a, b, a_spec, b_spec, nb, k1, n1, s):
    ts = min(TS_DW, s)
    ns = s // ts

    def body(a_ref, b_ref, of_ref, ob_ref, acc):
        j = pl.program_id(1)
        p = _dot_tn(a_ref[...], b_ref[...])

        @pl.when(j == 0)
        def _():
            acc[...] = p

        @pl.when(j > 0)
        def _():
            acc[...] += p

        @pl.when(j == ns - 1)
        def _():
            of_ref[...] = acc[...]
            ob_ref[...] = acc[...].astype(BF)

    out_spec = pl.BlockSpec((None, k1, n1), lambda k, j: (k, 0, 0))
    return pl.pallas_call(
        body, name=name, grid=(nb, ns),
        in_specs=[a_spec(ts), b_spec(ts)],
        out_specs=[out_spec, out_spec],
        out_shape=[jax.ShapeDtypeStruct((nb, k1, n1), F32), jax.ShapeDtypeStruct((nb, k1, n1), BF)],
        scratch_shapes=[pltpu.VMEM((k1, n1), F32)],
        compiler_params=_params(("parallel", "arbitrary")),
    )(a, b)


def _rows2d(w):
    return lambda ts: pl.BlockSpec((ts, w), lambda k, j: (j, 0))


def _cols2d(w):
    return lambda ts: pl.BlockSpec((ts, w), lambda k, j: (j, k))


def _blk3d(w):
    return lambda ts: pl.BlockSpec((None, ts, w), lambda k, j: (k, j, 0))


_BC1 = 1.0 - ADAM_B1 ** ADAM_STEP
_BC2 = 1.0 - ADAM_B2 ** ADAM_STEP


def _adamw_math(w, g, m, v):
    m = ADAM_B1 * m + (1.0 - ADAM_B1) * g
    v = ADAM_B2 * v + (1.0 - ADAM_B2) * (g * g)
    m_hat = m / _BC1
    v_hat = v / _BC2
    delta = -ADAM_LR * (m_hat / (jnp.sqrt(v_hat) + ADAM_EPS) + ADAM_WD * w)
    return delta, m, v


def _row_tile(r):
    for t in (256, 176, 128, 64, 32, 16, 8):
        if r % t == 0:
            return t
    return r


def _reduce_adamw_call(name, me, own, recv, w, m, v):
    _, r, c = own.shape
    tr = _row_tile(r)

    def body(me_ref, own_ref, recv_ref, w_ref, m_ref, v_ref, g_ref, d_ref, nm_ref, nv_ref):
        mine = me_ref[0]
        g = jnp.zeros((tr, c), F32)
        for sdev in range(NDEV):
            g = g + jnp.where(mine == sdev, own_ref[...], recv_ref[sdev].astype(F32))
        g_ref[...] = g
        d_ref[...], nm_ref[...], nv_ref[...] = _adamw_math(w_ref[...], g, m_ref[...], v_ref[...])

    tile = pl.BlockSpec((tr, c), lambda i, me_ref: (i, 0))
    grid_spec = pltpu.PrefetchScalarGridSpec(
        num_scalar_prefetch=1, grid=(r // tr,),
        in_specs=[pl.BlockSpec((None, tr, c), lambda i, me_ref: (me_ref[0], i, 0)),
                  pl.BlockSpec((NDEV, tr, c), lambda i, me_ref: (0, i, 0)), tile, tile, tile],
        out_specs=[tile] * 4)
    return pl.pallas_call(
        body, name=name, grid_spec=grid_spec,
        out_shape=[jax.ShapeDtypeStruct((r, c), F32)] * 4,
        compiler_params=_params(("parallel",)),
    )(me, own, recv, w, m, v)


def _adamw_call(name, w, g, m, v):
    r, c = w.shape
    tr = _row_tile(r)

    def body(w_ref, g_ref, m_ref, v_ref, d_ref, nm_ref, nv_ref):
        d_ref[...], nm_ref[...], nv_ref[...] = _adamw_math(w_ref[...], g_ref[...], m_ref[...], v_ref[...])

    tile = pl.BlockSpec((tr, c), lambda i: (i, 0))
    return pl.pallas_call(
        body, name=name, grid=(r // tr,), in_specs=[tile] * 4, out_specs=[tile] * 3,
        out_shape=[jax.ShapeDtypeStruct((r, c), F32)] * 3,
        compiler_params=_params(("parallel",)),
    )(w, g, m, v)


def _me():
    return lax.axis_index("x"), lax.axis_index("y"), lax.axis_index("c")


def _flip(pos, r):
    x, y, c = pos
    return (1 - x if r & 4 else x, 1 - y if r & 2 else y, 1 - c if r & 1 else c)


def _lin(pos):
    return pos[0] * 4 + pos[1] * 2 + pos[2]


def _all_gather_call(shards):
    n = len(shards)
    chips = (4, 2, 6)

    def body(*refs):
        ins, outs = refs[:n], refs[n:2 * n]
        send_sems, recv_sems, local_sems = refs[2 * n:]
        me = _me()

        def copy(a, k, block, to, src=None):
            dst = outs[a].at[_lin(block)]
            return pltpu.make_async_remote_copy(
                src_ref=dst if src is None else src, dst_ref=dst,
                send_sem=send_sems.at[a, k], recv_sem=recv_sems.at[a, k], device_id=to, device_id_type=MESH)

        sibling = _flip(me, 1)
        mine, first, passed = [], [], []
        for a in range(n):
            cp = pltpu.make_async_copy(ins[a], outs[a].at[_lin(me)], local_sems.at[a])
            cp.start()
            mine.append(cp)
            first.append(copy(a, 0, me, sibling, src=ins[a]))
            for j, f in enumerate(chips):
                first.append(copy(a, 1 + j, me, _flip(me, f), src=ins[a]))
        for cp in first:
            cp.start()
        for j, f in enumerate(chips):
            for a in range(n):
                copy(a, 1 + j, _flip(me, f), me).wait_recv()
                cp = copy(a, 4 + j, _flip(me, f), sibling)
                cp.start()
                passed.append(cp)
        for a in range(n):
            copy(a, 0, sibling, me).wait_recv()
            for j, f in enumerate(chips):
                copy(a, 4 + j, _flip(me, f | 1), me).wait_recv()
        for cp in first + passed:
            cp.wait_send()
        for cp in mine:
            cp.wait()

    any_spec = pl.BlockSpec(memory_space=pl.ANY)
    return pl.pallas_call(
        body, name="all_gather",
        in_specs=[any_spec] * n, out_specs=[any_spec] * n,
        out_shape=[jax.ShapeDtypeStruct((NDEV,) + sh.shape, sh.dtype) for sh in shards],
        scratch_shapes=[pltpu.SemaphoreType.DMA((n, 7)), pltpu.SemaphoreType.DMA((n, 7)),
                        pltpu.SemaphoreType.DMA((n,))],
    )(*shards)


def _exchange_call(parts):
    n = len(parts)

    def body(*refs):
        ins, outs = refs[:n], refs[n:2 * n]
        send_sems, recv_sems, local_sems = refs[2 * n:]
        me = _me()
        mine, sent = [], []
        for a in range(n):
            cp = pltpu.make_async_copy(ins[a].at[_lin(me)], outs[a].at[_lin(me)], local_sems.at[a])
            cp.start()
            mine.append(cp)
        for r in range(1, NDEV):
            peer = _flip(me, r)
            for a in range(n):
                cp = pltpu.make_async_remote_copy(
                    src_ref=ins[a].at[_lin(peer)], dst_ref=outs[a].at[_lin(me)],
                    send_sem=send_sems.at[a, r - 1], recv_sem=recv_sems.at[a, r - 1],
                    device_id=peer, device_id_type=MESH)
                cp.start()
                sent.append(cp)
        for r in range(1, NDEV):
            peer = _flip(me, r)
            for a in range(n):
                pltpu.make_async_remote_copy(
                    src_ref=ins[a].at[_lin(me)], dst_ref=outs[a].at[_lin(peer)],
                    send_sem=send_sems.at[a, r - 1], recv_sem=recv_sems.at[a, r - 1],
                    device_id=peer, device_id_type=MESH).wait_recv()
        for cp in sent:
            cp.wait_send()
        for cp in mine:
            cp.wait()

    any_spec = pl.BlockSpec(memory_space=pl.ANY)
    return pl.pallas_call(
        body, name="grad_exchange",
        in_specs=[any_spec] * n, out_specs=[any_spec] * n,
        out_shape=[jax.ShapeDtypeStruct(p.shape, p.dtype) for p in parts],
        scratch_shapes=[pltpu.SemaphoreType.DMA((n, 7)), pltpu.SemaphoreType.DMA((n, 7)),
                        pltpu.SemaphoreType.DMA((n,))],
    )(*parts)


def _small_allreduce_call(g):
    def body(g_ref, out_ref, rbuf, send1, recv1, send2, recv2):
        me = _me()
        mi = _lin(me)

        def rows(d):
            return pl.ds(pl.multiple_of(d * SMALL_PER, 8), SMALL_PER)

        sent = []
        for r in range(1, NDEV):
            peer = _flip(me, r)
            cp = pltpu.make_async_remote_copy(
                src_ref=g_ref.at[rows(_lin(peer))], dst_ref=rbuf.at[mi],
                send_sem=send1.at[r - 1], recv_sem=recv1.at[r - 1], device_id=peer, device_id_type=MESH)
            cp.start()
            sent.append(cp)
        rbuf[mi] = g_ref[rows(mi), :]
        for r in range(1, NDEV):
            peer = _flip(me, r)
            pltpu.make_async_remote_copy(
                src_ref=g_ref.at[rows(mi)], dst_ref=rbuf.at[_lin(peer)],
                send_sem=send1.at[r - 1], recv_sem=recv1.at[r - 1], device_id=peer, device_id_type=MESH).wait_recv()
        for cp in sent:
            cp.wait_send()
        tot = rbuf[0]
        for d in range(1, NDEV):
            tot = tot + rbuf[d]
        out_ref[rows(mi), :] = tot
        sent = []
        for r in range(1, NDEV):
            peer = _flip(me, r)
            cp = pltpu.make_async_remote_copy(
                src_ref=out_ref.at[rows(mi)], dst_ref=out_ref.at[rows(mi)],
                send_sem=send2.at[r - 1], recv_sem=recv2.at[r - 1], device_id=peer, device_id_type=MESH)
            cp.start()
            sent.append(cp)
        for r in range(1, NDEV):
            peer = _flip(me, r)
            pltpu.make_async_remote_copy(
                src_ref=out_ref.at[rows(mi)], dst_ref=out_ref.at[rows(_lin(peer))],
                send_sem=send2.at[r - 1], recv_sem=recv2.at[r - 1], device_id=peer, device_id_type=MESH).wait_recv()
        for cp in sent:
            cp.wait_send()

    vm = pl.BlockSpec(memory_space=pltpu.VMEM)
    return pl.pallas_call(
        body, name="small_allreduce", in_specs=[vm], out_specs=vm,
        out_shape=jax.ShapeDtypeStruct((SMALL_ROWS, D), F32),
        scratch_shapes=[pltpu.VMEM((NDEV, SMALL_PER, D), F32)] + [pltpu.SemaphoreType.DMA((NDEV - 1,))] * 4,
    )(g)


def _head_blocks(w):
    z = jnp.zeros((64, 64), w.dtype)
    groups = []
    for q in range(NQ):
        rows = [jnp.concatenate([w[4 * q + a] if a == b else z for b in range(4)], axis=1) for a in range(4)]
        groups.append(jnp.concatenate(rows, axis=0))
    return jnp.stack(groups)


def _head_unblocks(g):
    return jnp.stack([g[q, 64 * a:64 * a + 64, 64 * a:64 * a + 64] for q in range(NQ) for a in range(4)])


def _local_step(x, tgt, p):
    s = x.shape[0]
    vec = lambda a: a.reshape(1, D)
    gmix, gffn, gfin = vec(p["norm_mix_g"]), vec(p["norm_ffn_g"]), vec(p["norm_final_g"])
    cb, ba, bx, lam = vec(p["conv_b"]), vec(p["rg_ba"]), vec(p["rg_bx"]), vec(p["rg_lambda"])
    lng, lnb = vec(p["sgu_ln_g"]), vec(p["sgu_ln_b"])
    cw = p["conv_w"]
    wax = jnp.concatenate([_head_blocks(p["rg_wa"]), _head_blocks(p["rg_wx"])], axis=2).astype(BF)
    tril = jnp.tril(jnp.ones((CHUNK, CHUNK), bool))
    ws = jnp.where(tril[None], p["sgu_ws"], 0.0)
    wtr = ws.astype(BF)
    wtrt = jnp.swapaxes(ws, 1, 2).astype(BF)
    bias = jnp.repeat(p["sgu_bs"].T, GW, axis=1)
    w_in, wgu, wdn = p["w_in"], p["w_gate_up"], p["w_down"]
    wpa, wpb, wo = p["w_proj_a"], p["w_proj_b"], p["w_out"]
    nb, _, wb = w_in.shape
    fb = wgu.shape[2]
    nh = wdn.shape[0]

    proj, h1 = _proj_call(x, gmix, w_in)
    h, pa, pb, x1, ya, yb, mg = _mixer_fwd_call(proj, x, cw, cb, ba, bx, lam, lng, lnb, wax, wtr, bias, wpa, wpb, wo)
    dx1, dx1b, h2, act, dgu, dx2b, facc = _ffn_call(x1, tgt, gffn, gfin, wgu, wdn)
    dproj, dya, dpa, dpb, bvec, dws, dbs = _mixer_bwd_pre_call(dx1b, proj, pa, pb, lng, lnb, wtr, wtrt, bias,
                                                             wpa, wpb, wo)
    dproj, svec, dwax = _mixer_bwd_seq_call(dproj, dya, proj, h, cw, cb, ba, bx, lam, wax)
    dx, xacc = _dx_call(dproj, dx1, x, gmix, w_in)

    dw = {
        "w_in": _dw_call("dw_in", h1, dproj, _rows2d(D), _cols2d(wb), nb, D, wb, s),
        "w_gate_up": _dw_call("dw_gate_up", h2, dgu, _rows2d(D), _blk3d(fb), 2 * nh, D, fb, s),
        "w_down": _dw_call("dw_down", act, dx2b, _blk3d(fb), _rows2d(D), nh, fb, D, s),
        "w_proj_a": _dw_call("dw_proj_a", ya, dpa, _rows2d(D), _rows2d(D), 1, D, D, s),
        "w_proj_b": _dw_call("dw_proj_b", yb, dpb, _rows2d(D), _rows2d(D), 1, D, D, s),
        "w_out": _dw_call("dw_out", mg, dx1b, _rows2d(D), _rows2d(D), 1, D, D, s),
    }
    small = {
        "norm_mix_g": xacc[0], "norm_ffn_g": facc[0], "norm_final_g": facc[1],
        "conv_b": svec[0], "rg_ba": svec[1], "rg_bx": svec[2], "rg_lambda": svec[3], "conv_w": svec[4:8],
        "sgu_ln_g": bvec[0], "sgu_ln_b": bvec[1],
        "rg_wa": _head_unblocks(dwax[:, :, 0:QW]), "rg_wx": _head_unblocks(dwax[:, :, QW:2 * QW]),
        "sgu_ws": dws, "sgu_bs": dbs[:, ::GW].T,
    }
    loss_sum = (0.5 / D) * jnp.sum(facc[2])
    return loss_sum, dx, dw, small


_BIG = ("w_in", "w_gate_up", "w_down", "w_proj_a", "w_proj_b", "w_out")
_VEC_ROWS = ("norm_mix_g", "norm_ffn_g", "norm_final_g", "conv_b", "rg_ba", "rg_bx", "rg_lambda",
             "sgu_ln_g", "sgu_ln_b", "sgu_bs")
_WEIGHTS = ("norm_mix_g", "w_in", "conv_w", "conv_b", "rg_wa", "rg_ba", "rg_wx", "rg_bx", "rg_lambda",
            "sgu_ln_g", "sgu_ln_b", "sgu_ws", "sgu_bs", "w_proj_a", "w_proj_b", "w_out", "norm_ffn_g",
            "w_gate_up", "w_down", "norm_final_g")


def _pack_small(t, conv_w):
    head = jnp.concatenate([t[k].reshape(1, D) for k in _VEC_ROWS] + [conv_w, jnp.zeros((2, D), F32)], axis=0)
    return jnp.concatenate([head, t["rg_wa"].reshape(64, D), t["rg_wx"].reshape(64, D), t["sgu_ws"].reshape(128, D),
                            jnp.zeros((SMALL_ROWS - 272, D), F32)], axis=0)


def _unpack_small(a):
    out = {k: a[j] for j, k in enumerate(_VEC_ROWS)}
    out["conv_w"] = a[10:14]
    out["rg_wa"] = a[16:80].reshape(16, 64, 64)
    out["rg_wx"] = a[80:144].reshape(16, 64, 64)
    out["sgu_ws"] = a[144:272].reshape(NGRP, CHUNK, CHUNK)
    return out


def kernel(x, norm_mix_g, w_in, conv_w, conv_b, rg_wa, rg_ba, rg_wx, rg_bx, rg_lambda, sgu_ln_g, sgu_ln_b, sgu_ws, sgu_bs, w_proj_a, w_proj_b, w_out, norm_ffn_g, w_gate_up, w_down, norm_final_g, loss_target, m_norm_mix_g, m_w_in, m_conv_w, m_conv_b, m_rg_wa, m_rg_ba, m_rg_wx, m_rg_bx, m_rg_lambda, m_sgu_ln_g, m_sgu_ln_b, m_sgu_ws, m_sgu_bs, m_w_proj_a, m_w_proj_b, m_w_out, m_norm_ffn_g, m_w_gate_up, m_w_down, m_norm_final_g, v_norm_mix_g, v_w_in, v_conv_w, v_conv_b, v_rg_wa, v_rg_ba, v_rg_wx, v_rg_bx, v_rg_lambda, v_sgu_ln_g, v_sgu_ln_b, v_sgu_ws, v_sgu_bs, v_w_proj_a, v_w_proj_b, v_w_out, v_norm_ffn_g, v_w_gate_up, v_w_down, v_norm_final_g):
    args = dict(locals())
    w = {k: args[k] for k in _WEIGHTS}
    m = {k: args["m_" + k] for k in _WEIGHTS}
    v = {k: args["v_" + k] for k in _WEIGHTS}
    for d in (w, m, v):
        for k in _WEIGHTS:
            if k != "norm_final_g":
                d[k] = d[k][0]
    me = _lin(_me())
    me1 = me.reshape(1).astype(jnp.int32)

    shards = [w[k].astype(BF) for k in _BIG] + [w["conv_w"]]
    gathered = _all_gather_call(shards)
    full = dict(zip(_BIG, gathered[:-1]))
    p = {k: w[k] for k in _WEIGHTS if k not in _BIG and k != "conv_w"}
    p["w_in"] = full["w_in"]
    p["w_gate_up"] = full["w_gate_up"]
    p["w_down"] = full["w_down"].reshape(NDEV // 2, -1, D)
    for k in ("w_proj_a", "w_proj_b", "w_out"):
        p[k] = full[k].reshape(D, D)
    p["conv_w"] = jnp.swapaxes(gathered[-1], 0, 1).reshape(4, D)

    loss_sum, dx, dw, small = _local_step(x[0], loss_target[0], p)
    loss = lax.psum(loss_sum, ("x", "y", "c"))

    own, parts = {}, []
    for k in _BIG:
        f, b = dw[k]
        r, c = w[k].shape
        own[k] = f.reshape(NDEV, r, c)
        parts.append(b.reshape(NDEV, r, c))
    recv = dict(zip(_BIG, _exchange_call(parts)))
    grads, delta, new_m, new_v = {}, {}, {}, {}
    for k in _BIG:
        grads[k], delta[k], new_m[k], new_v[k] = _reduce_adamw_call("adamw_" + k, me1, own[k], recv[k], w[k], m[k], v[k])

    gsum = _small_allreduce_call(_pack_small(small, small["conv_w"]))
    zc = jnp.zeros((4, D), F32)
    d_s, m_s, v_s = _adamw_call("adamw_small", _pack_small(w, zc), gsum, _pack_small(m, zc), _pack_small(v, zc))
    gs, ds, ms, vs = _unpack_small(gsum), _unpack_small(d_s), _unpack_small(m_s), _unpack_small(v_s)
    g_cw = lax.dynamic_slice(gs["conv_w"], (0, me * 128), (4, 128))
    ds["conv_w"], ms["conv_w"], vs["conv_w"] = _adamw_call("adamw_conv_w", w["conv_w"], g_cw, m["conv_w"], v["conv_w"])
    gs["conv_w"] = g_cw
    for k in _WEIGHTS:
        if k not in _BIG:
            shp = w[k].shape
            grads[k], delta[k], new_m[k], new_v[k] = (t[k].reshape(shp) for t in (gs, ds, ms, vs))

    def lift(t, k):
        return t[k] if k == "norm_final_g" else t[k][None]

    outs = [loss, dx[None]]
    for t in (grads, delta, new_m, new_v):
        outs += [lift(t, k) for k in _WEIGHTS]
    return tuple(outs)
```

```python
import functools

import jax
import jax.numpy as jnp
from jax import lax
from jax.experimental import pallas as pl
from jax.experimental.pallas import tpu as pltpu

F32 = jnp.float32
BF = jnp.bfloat16

D = 1024
NDEV = 8
EPS = 1e-6
RG_C = 8.0
CHUNK = 128
NGRP = 8
GW = 128
NQ = 4
QW = 256
RC = 16
SMALL_ROWS = 320
SMALL_PER = SMALL_ROWS // NDEV

ADAM_LR = 0.001
ADAM_B1 = 0.9
ADAM_B2 = 0.999
ADAM_EPS = 1e-08
ADAM_WD = 0.01
ADAM_STEP = 10

VMEM_LIMIT = 60 * 1024 * 1024

MESH = pl.DeviceIdType.MESH


def _rows(n, fn, unroll=2, rc=RC):
    def body(i, c):
        fn(pl.multiple_of(i * rc, rc))
        return c
    lax.fori_loop(0, n // rc, body, 0, unroll=unroll)


def _dot(a, b):
    return jnp.dot(a, b, preferred_element_type=F32)


def _dot_nt(a, b):
    return lax.dot_general(a, b, (((1,), (1,)), ((), ())), preferred_element_type=F32)


def _dot_tn(a, b):
    return lax.dot_general(a, b, (((0,), (0,)), ((), ())), preferred_element_type=F32)


_GC = 0.7978845608028654
_GK = 0.044715


def _gelu(x):
    t = jnp.tanh(_GC * (x + _GK * (x * x * x)))
    return x * (0.5 * (1.0 + t))


def _gelu_grad(x):
    x2 = x * x
    t = jnp.tanh(_GC * (x + _GK * (x2 * x)))
    cdf = 0.5 * (1.0 + t)
    dg = cdf + (0.5 * x) * (1.0 - t * t) * (_GC * (1.0 + (3.0 * _GK) * x2))
    return x * cdf, dg


def _sigmoid(x):
    return jax.nn.sigmoid(x)


def _log1p(e):
    u = 1.0 + e
    d = u - 1.0
    return jnp.where(d == 0.0, e, jnp.log(u) * (e / jnp.where(d == 0.0, 1.0, d)))


def _softplus(z):
    return jnp.maximum(z, 0.0) + _log1p(jnp.exp(-jnp.abs(z)))


def _neg_expm1(z):
    u = jnp.exp(z)
    lu = jnp.log(u)
    k = (1.0 - u) * (z / jnp.where(lu == 0.0, 1.0, lu))
    small = jnp.where(lu == 0.0, -z, k)
    return jnp.where(z > -0.5, small, 1.0 - u)


def _shift_back(prev8, cur, j):
    cat = jnp.concatenate([prev8, cur], axis=0)
    return pltpu.roll(cat, j, 0)[8:8 + cur.shape[0]]


def _shift_fwd(cur, next8, j):
    cat = jnp.concatenate([cur, next8], axis=0)
    n = cat.shape[0]
    return pltpu.roll(cat, n - j, 0)[0:cur.shape[0]]


def _const_spec(shape):
    nd = len(shape)
    return pl.BlockSpec(shape, lambda *_: (0,) * nd, pipeline_mode=pl.Buffered(1))


def _params(sem):
    return pltpu.CompilerParams(dimension_semantics=sem, vmem_limit_bytes=VMEM_LIMIT)


def _gate_parts(z_ref, r0, q, ba, bx, sp):
    cs = slice(q * QW, (q + 1) * QW)
    za = z_ref[pl.ds(r0, RC), q * 2 * QW:q * 2 * QW + QW] + ba[:, cs]
    zx = z_ref[pl.ds(r0, RC), q * 2 * QW + QW:(q + 1) * 2 * QW] + bx[:, cs]
    r = _sigmoid(za)
    ig = _sigmoid(zx)
    la = (-RG_C * r) * sp[:, cs]
    return r, ig, la


TM_PROJ = 512
TM_MIX = 256
TM_FFN = 256
TM_DX = 512
TS_DW = 1024


def _proj_call(x, gmix, w_in_all):
    s = x.shape[0]
    tm = min(TM_PROJ, s)
    nb, _, wb = w_in_all.shape

    def body(x_ref, g_ref, w_ref, proj_ref, h1_ref):
        g = g_ref[...]

        def norm(r0):
            xx = x_ref[pl.ds(r0, RC), :]
            r = lax.rsqrt(jnp.mean(xx * xx, axis=-1, keepdims=True) + EPS)
            h1_ref[pl.ds(r0, RC), :] = ((xx * r) * g).astype(BF)
        _rows(tm, norm, unroll=4)
        h = h1_ref[...]
        for k in range(nb):
            proj_ref[:, k * wb:(k + 1) * wb] = _dot(h, w_ref[k])

    return pl.pallas_call(
        body, name="proj", grid=(s // tm,),
        in_specs=[pl.BlockSpec((tm, D), lambda i: (i, 0)), _const_spec((1, D)), _const_spec(w_in_all.shape)],
        out_specs=[pl.BlockSpec((tm, nb * wb), lambda i: (i, 0)), pl.BlockSpec((tm, D), lambda i: (i, 0))],
        out_shape=[jax.ShapeDtypeStruct((s, nb * wb), F32), jax.ShapeDtypeStruct((s, D), BF)],
        compiler_params=_params(("parallel",)),
    )(x, gmix, w_in_all)


def _conv_chunk(src_ref, ext, xc_s, xcb_s, cw_ref, cb, r0):
    cur = src_ref[pl.ds(r0, RC), 0:D]
    ext[pl.ds(r0 + 8, RC), :] = cur
    pv = ext[pl.ds(r0, 8), :]
    xc = cb + cw_ref[3:4, :] * cur
    for j in (1, 2, 3):
        xc = xc + cw_ref[3 - j:4 - j, :] * _shift_back(pv, cur, j)
    xc_s[pl.ds(r0, RC), :] = xc
    xcb_s[pl.ds(r0, RC), :] = xc.astype(BF)


def _mixer_fwd_call(proj, x, cw, cb, ba, bx, lam, lng, lnb, wax, wtr, bias, wpa, wpb, wo):
    s = x.shape[0]
    tm = min(TM_MIX, s)
    nt = s // tm
    pw = proj.shape[1]

    def body(proj_ref, x_ref, cw_ref, cb_ref, ba_ref, bx_ref, lam_ref, lng_ref, lnb_ref, wax_ref, wtr_ref,
             bias_ref, wpa_ref, wpb_ref, wo_ref,
             h_ref, pa_ref, pb_ref, x1_ref, ya_ref, yb_ref, mg_ref,
             ext, xc_s, xcb_s, z_s, a_s, b_s, hc_s, vn_s, mx_s):
        i = pl.program_id(0)

        @pl.when(i == 0)
        def _():
            ext[0:8, :] = jnp.zeros((8, D), F32)
            hc_s[...] = jnp.zeros((8, D), F32)

        @pl.when(i > 0)
        def _():
            ext[0:8, :] = ext[tm:tm + 8, :]

        cb = cb_ref[...]
        _rows(tm, lambda r0: _conv_chunk(proj_ref, ext, xc_s, xcb_s, cw_ref, cb, r0))

        for q in range(NQ):
            z_s[:, q * 2 * QW:(q + 1) * 2 * QW] = _dot(xcb_s[:, q * QW:(q + 1) * QW], wax_ref[q])

        sp = _softplus(-lam_ref[...])
        ba = ba_ref[...]
        bx = bx_ref[...]

        def gates(r0):
            for q in range(NQ):
                cs = slice(q * QW, (q + 1) * QW)
                _, ig, la = _gate_parts(z_s, r0, q, ba, bx, sp)
                a_s[pl.ds(r0, RC), cs] = jnp.exp(la)
                b_s[pl.ds(r0, RC), cs] = jnp.sqrt(_neg_expm1(2.0 * la)) * (ig * xc_s[pl.ds(r0, RC), cs])
        _rows(tm, gates)

        def scan(t, h):
            h = a_s[pl.ds(t, 1), :] * h + b_s[pl.ds(t, 1), :]
            h_ref[pl.ds(t, 1), :] = h
            return h
        hc_s[0:1, :] = lax.fori_loop(0, tm, scan, hc_s[0:1, :], unroll=8)

        def branch_a(r0):
            rg = proj_ref[pl.ds(r0, RC), D:2 * D]
            ya_ref[pl.ds(r0, RC), :] = (_gelu(rg) * h_ref[pl.ds(r0, RC), :]).astype(BF)
        _rows(tm, branch_a)

        lng = lng_ref[...]
        lnb = lnb_ref[...]

        def lnorm(r0):
            gv = _gelu(proj_ref[pl.ds(r0, RC), 3 * D:4 * D])
            mu = jnp.mean(gv, axis=-1, keepdims=True)
            dv = gv - mu
            var = jnp.mean(dv * dv, axis=-1, keepdims=True)
            vn_s[pl.ds(r0, RC), :] = ((dv * lax.rsqrt(var + EPS)) * lng + lnb).astype(BF)
        _rows(tm, lnorm, unroll=4)

        for c in range(tm // CHUNK):
            rs = slice(c * CHUNK, (c + 1) * CHUNK)
            for g in range(NGRP):
                cs = slice(g * GW, (g + 1) * GW)
                mx_s[rs, cs] = _dot(wtr_ref[g], vn_s[rs, cs])

        def branch_b(r0):
            u = proj_ref[pl.ds(r0, RC), 2 * D:3 * D]
            boff = pl.multiple_of(lax.rem(r0, CHUNK), RC)
            mixed = mx_s[pl.ds(r0, RC), :] + bias_ref[pl.ds(boff, RC), :]
            yb_ref[pl.ds(r0, RC), :] = (_gelu(u) * mixed).astype(BF)
        _rows(tm, branch_b)

        pa_ref[...] = _dot(ya_ref[...], wpa_ref[...])
        pb_ref[...] = _dot(yb_ref[...], wpb_ref[...])

        def merge(r0):
            ga = proj_ref[pl.ds(r0, RC), 4 * D:5 * D]
            gb = proj_ref[pl.ds(r0, RC), 5 * D:6 * D]
            mg = _sigmoid(ga) * pa_ref[pl.ds(r0, RC), :] + _sigmoid(gb) * pb_ref[pl.ds(r0, RC), :]
            mg_ref[pl.ds(r0, RC), :] = mg.astype(BF)
        _rows(tm, merge)

        x1_ref[...] = x_ref[...] + _dot(mg_ref[...], wo_ref[...])

    tile = lambda w: pl.BlockSpec((tm, w), lambda i: (i, 0))
    vec = _const_spec((1, D))
    return pl.pallas_call(
        body, name="mixer_fwd", grid=(nt,),
        in_specs=[tile(pw), tile(D), _const_spec((4, D)), vec, vec, vec, vec, vec, vec,
                  _const_spec(wax.shape), _const_spec(wtr.shape), _const_spec(bias.shape),
                  _const_spec((D, D)), _const_spec((D, D)), _const_spec((D, D))],
        out_specs=[tile(D)] * 7,
        out_shape=[jax.ShapeDtypeStruct((s, D), F32)] * 4 + [jax.ShapeDtypeStruct((s, D), BF)] * 3,
        scratch_shapes=[pltpu.VMEM((tm + 8, D), F32), pltpu.VMEM((tm, D), F32), pltpu.VMEM((tm, D), BF),
                        pltpu.VMEM((tm, 2 * D), F32), pltpu.VMEM((tm, D), F32), pltpu.VMEM((tm, D), F32),
                        pltpu.VMEM((8, D), F32), pltpu.VMEM((tm, D), BF), pltpu.VMEM((tm, D), F32)],
        compiler_params=_params(("arbitrary",)),
    )(proj, x, cw, cb, ba, bx, lam, lng, lnb, wax, wtr, bias, wpa, wpb, wo)


def _ffn_call(x1, tgt, gffn, gfin, wgu, wdn):
    s = x1.shape[0]
    tm = min(TM_FFN, s)
    nt = s // tm
    nh = wdn.shape[0]
    fb = wgu.shape[2]

    def body(x1_ref, tgt_ref, gffn_ref, gfin_ref, wgu_ref, wdn_ref,
             dx1_ref, dx1b_ref, h2_ref, act_ref, dgu_ref, dx2b_ref, acc_ref,
             g_s, u_s, x2_s, dx2_s, da_s, dh2_s, accs):
        i = pl.program_id(0)

        @pl.when(i == 0)
        def _():
            accs[...] = jnp.zeros(accs.shape, F32)

        gffn = gffn_ref[...]
        gfin = gfin_ref[...]

        def norm(r0):
            xx = x1_ref[pl.ds(r0, RC), :]
            r = lax.rsqrt(jnp.mean(xx * xx, axis=-1, keepdims=True) + EPS)
            h2_ref[pl.ds(r0, RC), :] = ((xx * r) * gffn).astype(BF)
        _rows(tm, norm, unroll=4)

        h2 = h2_ref[...]
        for k in range(nh):
            g_s[k] = _dot(h2, wgu_ref[k])
            u_s[k] = _dot(h2, wgu_ref[k + nh])

            def act(r0, k=k):
                g = g_s[k, pl.ds(r0, RC), :]
                act_ref[k, pl.ds(r0, RC), :] = ((g * _sigmoid(g)) * u_s[k, pl.ds(r0, RC), :]).astype(BF)
            _rows(tm, act)

        x2 = x1_ref[...]
        for k in range(nh):
            x2 = x2 + _dot(act_ref[k], wdn_ref[k])
        x2_s[...] = x2

        def head(r0):
            xx = x2_s[pl.ds(r0, RC), :]
            r = lax.rsqrt(jnp.mean(xx * xx, axis=-1, keepdims=True) + EPS)
            xh = xx * r
            err = xh * gfin - tgt_ref[pl.ds(r0, RC), :]
            accs[2] += err * err
            dy = err * (1.0 / D)
            accs[1] += dy * xh
            dxh = dy * gfin
            dx2 = r * (dxh - xh * jnp.mean(dxh * xh, axis=-1, keepdims=True))
            dx2_s[pl.ds(r0, RC), :] = dx2
            dx2b_ref[pl.ds(r0, RC), :] = dx2.astype(BF)
        _rows(tm, head, unroll=4)

        dx2b = dx2b_ref[...]
        for k in range(nh):
            da_s[...] = _dot_nt(dx2b, wdn_ref[k])

            def dact(r0, k=k):
                g = g_s[k, pl.ds(r0, RC), :]
                u = u_s[k, pl.ds(r0, RC), :]
                da = da_s[pl.ds(r0, RC), :]
                sg = _sigmoid(g)
                dgu_ref[k, pl.ds(r0, RC), :] = ((da * u) * (sg * (1.0 + g * (1.0 - sg)))).astype(BF)
                dgu_ref[k + nh, pl.ds(r0, RC), :] = (da * (g * sg)).astype(BF)
            _rows(tm, dact)

        dh2 = _dot_nt(dgu_ref[0], wgu_ref[0])
        for k in range(1, 2 * nh):
            dh2 = dh2 + _dot_nt(dgu_ref[k], wgu_ref[k])
        dh2_s[...] = dh2

        def tail(r0):
            xx = x1_ref[pl.ds(r0, RC), :]
            r = lax.rsqrt(jnp.mean(xx * xx, axis=-1, keepdims=True) + EPS)
            xh = xx * r
            dh = dh2_s[pl.ds(r0, RC), :]
            accs[0] += dh * xh
            dxh = dh * gffn
            dx1 = dx2_s[pl.ds(r0, RC), :] + r * (dxh - xh * jnp.mean(dxh * xh, axis=-1, keepdims=True))
            dx1_ref[pl.ds(r0, RC), :] = dx1
            dx1b_ref[pl.ds(r0, RC), :] = dx1.astype(BF)
        _rows(tm, tail, unroll=4)

        @pl.when(i == nt - 1)
        def _():
            acc_ref[...] = jnp.zeros((8, D), F32)
            for j in range(3):
                acc_ref[j:j + 1, :] = jnp.sum(accs[j], axis=0, keepdims=True)

    tile = lambda w: pl.BlockSpec((tm, w), lambda i: (i, 0))
    vec = _const_spec((1, D))
    return pl.pallas_call(
        body, name="ffn", grid=(nt,),
        in_specs=[tile(D), tile(D), vec, vec, _const_spec(wgu.shape), _const_spec(wdn.shape)],
        out_specs=[tile(D), tile(D), tile(D),
                   pl.BlockSpec((nh, tm, fb), lambda i: (0, i, 0)),
                   pl.BlockSpec((2 * nh, tm, fb), lambda i: (0, i, 0)),
                   tile(D), pl.BlockSpec((8, D), lambda i: (0, 0))],
        out_shape=[jax.ShapeDtypeStruct((s, D), F32), jax.ShapeDtypeStruct((s, D), BF),
                   jax.ShapeDtypeStruct((s, D), BF), jax.ShapeDtypeStruct((nh, s, fb), BF),
                   jax.ShapeDtypeStruct((2 * nh, s, fb), BF), jax.ShapeDtypeStruct((s, D), BF),
                   jax.ShapeDtypeStruct((8, D), F32)],
        scratch_shapes=[pltpu.VMEM((nh, tm, fb), F32), pltpu.VMEM((nh, tm, fb), F32), pltpu.VMEM((tm, D), F32),
                        pltpu.VMEM((tm, D), F32), pltpu.VMEM((tm, fb), F32), pltpu.VMEM((tm, D), F32),
                        pltpu.VMEM((3, RC, D), F32)],
        compiler_params=_params(("arbitrary",)),
    )(x1, tgt, gffn, gfin, wgu, wdn)


def _mixer_bwd_pre_call(dx1b, proj, pa, pb, lng, lnb, wtr, wtrt, bias, wpa, wpb, wo):
    s = dx1b.shape[0]
    tm = min(TM_MIX, s)
    nt = s // tm
    pw = proj.shape[1]

    def body(dx1b_ref, uv_ref, gg_ref, pa_ref, pb_ref, lng_ref, lnb_ref, wtr_ref, wtrt_ref, bias_ref,
             wpa_ref, wpb_ref, wo_ref,
             dproj_ref, dya_ref, dpa_ref, dpb_ref, vec_ref, dws_ref, dbs_ref,
             dm_s, dyb_s, vn_s, xh_s, rs_s, mx_s, dmx_s, dvn_s, accs, dbs_s):
        i = pl.program_id(0)

        @pl.when(i == 0)
        def _():
            accs[...] = jnp.zeros(accs.shape, F32)
            dbs_s[...] = jnp.zeros(dbs_s.shape, F32)
            dws_ref[...] = jnp.zeros(dws_ref.shape, F32)

        dm_s[...] = _dot_nt(dx1b_ref[...], wo_ref[...])

        def merge(r0):
            rows = pl.ds(r0, RC)
            dm = dm_s[rows, :]
            sa = _sigmoid(gg_ref[rows, 0:D])
            sb = _sigmoid(gg_ref[rows, D:2 * D])
            dpa = dm * sa
            dpb = dm * sb
            dpa_ref[rows, :] = dpa.astype(BF)
            dpb_ref[rows, :] = dpb.astype(BF)
            dproj_ref[rows, 4 * D:5 * D] = ((dpa * pa_ref[rows, :]) * (1.0 - sa)).astype(BF)
            dproj_ref[rows, 5 * D:6 * D] = ((dpb * pb_ref[rows, :]) * (1.0 - sb)).astype(BF)
            dproj_ref[rows, 0:2 * D] = jnp.zeros((RC, 2 * D), BF)
        _rows(tm, merge)

        dya_ref[...] = _dot_nt(dpa_ref[...], wpa_ref[...])
        dyb_s[...] = _dot_nt(dpb_ref[...], wpb_ref[...])

        lng = lng_ref[...]
        lnb = lnb_ref[...]

        def lnorm(r0):
            rows = pl.ds(r0, RC)
            gv = _gelu(uv_ref[rows, D:2 * D])
            mu = jnp.mean(gv, axis=-1, keepdims=True)
            dv = gv - mu
            var = jnp.mean(dv * dv, axis=-1, keepdims=True)
            rstd = lax.rsqrt(var + EPS)
            xh = dv * rstd
            xh_s[rows, :] = xh
            rs_s[rows, :] = jnp.broadcast_to(rstd, (RC, 128))
            vn_s[rows, :] = (xh * lng + lnb).astype(BF)
        _rows(tm, lnorm, unroll=4)

        for c in range(tm // CHUNK):
            rs = slice(c * CHUNK, (c + 1) * CHUNK)
            for g in range(NGRP):
                cs = slice(g * GW, (g + 1) * GW)
                mx_s[rs, cs] = _dot(wtr_ref[g], vn_s[rs, cs])

        def gate_u(r0):
            rows = pl.ds(r0, RC)
            boff = pl.multiple_of(lax.rem(r0, CHUNK), RC)
            gu, dgu = _gelu_grad(uv_ref[rows, 0:D])
            dyb = dyb_s[rows, :]
            mixed = mx_s[rows, :] + bias_ref[pl.ds(boff, RC), :]
            dproj_ref[rows, 2 * D:3 * D] = ((dyb * mixed) * dgu).astype(BF)
            dmx = dyb * gu
            dmx_s[rows, :] = dmx.astype(BF)
            dbs_s[pl.ds(boff, RC), :] += dmx
        _rows(tm, gate_u)

        for c in range(tm // CHUNK):
            rs = slice(c * CHUNK, (c + 1) * CHUNK)
            for g in range(NGRP):
                cs = slice(g * GW, (g + 1) * GW)
                dvn_s[rs, cs] = _dot(wtrt_ref[g], dmx_s[rs, cs])
                dws_ref[g] += _dot_nt(dmx_s[rs, cs], vn_s[rs, cs])

        def lnorm_bwd(r0):
            rows = pl.ds(r0, RC)
            dvn = dvn_s[rows, :]
            xh = xh_s[rows, :]
            accs[0] += dvn * xh
            accs[1] += dvn
            dxh = dvn * lng
            m1 = jnp.mean(dxh, axis=-1, keepdims=True)
            m2 = jnp.mean(dxh * xh, axis=-1, keepdims=True)
            dgv = rs_s[rows, 0:1] * (dxh - m1 - xh * m2)
            _, dg = _gelu_grad(uv_ref[rows, D:2 * D])
            dproj_ref[rows, 3 * D:4 * D] = (dgv * dg).astype(BF)
        _rows(tm, lnorm_bwd, unroll=4)

        @pl.when(i == nt - 1)
        def _():
            vec_ref[...] = jnp.zeros((8, D), F32)
            for j in range(2):
                vec_ref[j:j + 1, :] = jnp.sum(accs[j], axis=0, keepdims=True)
            row = lax.broadcasted_iota(jnp.int32, (CHUNK, CHUNK), 0)
            col = lax.broadcasted_iota(jnp.int32, (CHUNK, CHUNK), 1)
            for g in range(NGRP):
                dws_ref[g] = jnp.where(row >= col, dws_ref[g], 0.0)
                gs = jnp.sum(dbs_s[:, g * GW:(g + 1) * GW], axis=1, keepdims=True)
                dbs_ref[:, g * GW:(g + 1) * GW] = jnp.broadcast_to(gs, (CHUNK, GW))

    tile = lambda w: pl.BlockSpec((tm, w), lambda i: (i, 0))
    vec = _const_spec((1, D))
    return pl.pallas_call(
        body, name="mixer_bwd_pre", grid=(nt,),
        in_specs=[tile(D), pl.BlockSpec((tm, 2 * D), lambda i: (i, 1)), pl.BlockSpec((tm, 2 * D), lambda i: (i, 2)),
                  tile(D), tile(D), vec, vec, _const_spec(wtr.shape), _const_spec(wtrt.shape),
                  _const_spec(bias.shape), _const_spec((D, D)), _const_spec((D, D)), _const_spec((D, D))],
        out_specs=[tile(pw), tile(D), tile(D), tile(D), pl.BlockSpec((8, D), lambda i: (0, 0)),
                   pl.BlockSpec((NGRP, CHUNK, CHUNK), lambda i: (0, 0, 0)),
                   pl.BlockSpec((CHUNK, D), lambda i: (0, 0))],
        out_shape=[jax.ShapeDtypeStruct((s, pw), BF), jax.ShapeDtypeStruct((s, D), F32),
                   jax.ShapeDtypeStruct((s, D), BF), jax.ShapeDtypeStruct((s, D), BF),
                   jax.ShapeDtypeStruct((8, D), F32), jax.ShapeDtypeStruct((NGRP, CHUNK, CHUNK), F32),
                   jax.ShapeDtypeStruct((CHUNK, D), F32)],
        scratch_shapes=[pltpu.VMEM((tm, D), F32), pltpu.VMEM((tm, D), F32), pltpu.VMEM((tm, D), BF),
                        pltpu.VMEM((tm, D), F32), pltpu.VMEM((tm, 128), F32), pltpu.VMEM((tm, D), F32),
                        pltpu.VMEM((tm, D), BF), pltpu.VMEM((tm, D), F32), pltpu.VMEM((2, RC, D), F32),
                        pltpu.VMEM((CHUNK, D), F32)],
        compiler_params=_params(("arbitrary",)),
    )(dx1b, proj, proj, pa, pb, lng, lnb, wtr, wtrt, bias, wpa, wpb, wo)


def _mixer_bwd_seq_call(dproj, dya, proj, h, cw, cb, ba, bx, lam, wax):
    s = dya.shape[0]
    tm = min(TM_MIX, s)
    nt = s // tm
    tb = tm // 8

    def body(dproj_any, dya_ref, xg_ref, xh8_ref, h_ref, hh8_ref, cw_ref, cb_ref, ba_ref, bx_ref, lam_ref, wax_ref,
             dpa_ref, vec_ref, dwax_ref,
             ext, hext, xc_s, xcb_s, z_s, a_s, lm_s, dz_s, dxc_s, c_s, accs):
        del dproj_any
        i = pl.program_id(0)

        @pl.when(i == 0)
        def _():
            accs[...] = jnp.zeros(accs.shape, F32)
            dwax_ref[...] = jnp.zeros(dwax_ref.shape, F32)
            c_s[...] = jnp.zeros((8, D), F32)
            dxc_s[tm:tm + 8, :] = jnp.zeros((8, D), F32)

        @pl.when(i == nt - 1)
        def _():
            ext[0:8, :] = jnp.zeros((8, D), F32)
            hext[0:8, :] = jnp.zeros((8, D), F32)

        @pl.when(i < nt - 1)
        def _():
            ext[0:8, :] = xh8_ref[...]
            hext[0:8, :] = hh8_ref[...]

        cb = cb_ref[...]
        _rows(tm, lambda r0: _conv_chunk(xg_ref, ext, xc_s, xcb_s, cw_ref, cb, r0))

        for q in range(NQ):
            z_s[:, q * 2 * QW:(q + 1) * 2 * QW] = _dot(xcb_s[:, q * QW:(q + 1) * QW], wax_ref[q])

        lamv = lam_ref[...]
        sp = _softplus(-lamv)
        ba = ba_ref[...]
        bx = bx_ref[...]

        def prep(r0):
            rows = pl.ds(r0, RC)
            for q in range(NQ):
                cs = slice(q * QW, (q + 1) * QW)
                _, _, la = _gate_parts(z_s, r0, q, ba, bx, sp)
                a_s[rows, cs] = jnp.exp(la)
            hv = h_ref[rows, :]
            hext[pl.ds(r0 + 8, RC), :] = hv
            g, dg = _gelu_grad(xg_ref[rows, D:2 * D])
            dya = dya_ref[rows, :]
            lm_s[rows, :] = dya * g
            dpa_ref[rows, D:2 * D] = ((dya * hv) * dg).astype(BF)
        _rows(tm, prep)

        def scan(k, c):
            t = tm - 1 - k
            lm = lm_s[pl.ds(t, 1), :] + c
            lm_s[pl.ds(t, 1), :] = lm
            return a_s[pl.ds(t, 1), :] * lm
        c_s[0:1, :] = lax.fori_loop(0, tm, scan, c_s[0:1, :], unroll=8)

        def gate_bwd(r0):
            rows = pl.ds(r0, RC)
            hprev = _shift_back(hext[pl.ds(r0, 8), :], hext[pl.ds(r0 + 8, RC), :], 1)
            for q in range(NQ):
                cs = slice(q * QW, (q + 1) * QW)
                r, ig, la = _gate_parts(z_s, r0, q, ba, bx, sp)
                a = a_s[rows, cs]
                m = jnp.sqrt(_neg_expm1(2.0 * la))
                lm = lm_s[rows, cs]
                xc = xc_s[rows, cs]
                dixc = lm * m
                dig = dixc * xc
                dla = (lm * hprev[:, cs]) * a - ((lm * (ig * xc)) * (a * a)) / m
                accs[3, :, cs] += dla * r
                dza = (dla * (-RG_C * sp[:, cs])) * (r * (1.0 - r))
                dzx = dig * (ig * (1.0 - ig))
                accs[1, :, cs] += dza
                accs[2, :, cs] += dzx
                dz_s[rows, q * 2 * QW:q * 2 * QW + QW] = dza.astype(BF)
                dz_s[rows, q * 2 * QW + QW:(q + 1) * 2 * QW] = dzx.astype(BF)
                dxc_s[rows, cs] = dixc * ig
        _rows(tm, gate_bwd)

        for q in range(NQ):
            cs = slice(q * QW, (q + 1) * QW)
            dzq = dz_s[:, q * 2 * QW:(q + 1) * 2 * QW]
            dxc_s[0:tm, cs] += _dot_nt(dzq, wax_ref[q])
            dwax_ref[q] += _dot_tn(xcb_s[:, cs], dzq)

        def conv_bwd(r0):
            rows = pl.ds(r0, RC)
            cur = dxc_s[rows, :]
            nx = dxc_s[pl.ds(r0 + RC, 8), :]
            drx = cw_ref[3:4, :] * cur
            for j in (1, 2, 3):
                drx = drx + cw_ref[3 - j:4 - j, :] * _shift_fwd(cur, nx, j)
            dpa_ref[rows, 0:D] = drx.astype(BF)
            accs[0] += cur
            xcur = ext[pl.ds(r0 + 8, RC), :]
            xpv = ext[pl.ds(r0, 8), :]
            accs[7] += cur * xcur
            for j in (1, 2, 3):
                accs[7 - j] += cur * _shift_back(xpv, xcur, j)
        _rows(tm, conv_bwd)

        dxc_s[tm:tm + 8, :] = dxc_s[0:8, :]

        @pl.when(i == nt - 1)
        def _():
            vec_ref[...] = jnp.zeros((8, D), F32)
            for j in range(8):
                vec_ref[j:j + 1, :] = jnp.sum(accs[j], axis=0, keepdims=True)
            vec_ref[3:4, :] = vec_ref[3:4, :] * (RG_C * _sigmoid(-lamv))

    rev = lambda w: pl.BlockSpec((tm, w), lambda i: (nt - 1 - i, 0))
    halo = pl.BlockSpec((8, D), lambda i: (jnp.maximum((nt - 1 - i) * tb - 1, 0), 0))
    vec = _const_spec((1, D))
    return pl.pallas_call(
        body, name="mixer_bwd_seq", grid=(nt,),
        in_specs=[pl.BlockSpec(memory_space=pl.ANY), rev(D), rev(2 * D), halo, rev(D), halo,
                  _const_spec((4, D)), vec, vec, vec, vec, _const_spec(wax.shape)],
        out_specs=[rev(2 * D), pl.BlockSpec((8, D), lambda i: (0, 0)),
                   pl.BlockSpec((NQ, QW, 2 * QW), lambda i: (0, 0, 0))],
        out_shape=[jax.ShapeDtypeStruct(dproj.shape, BF), jax.ShapeDtypeStruct((8, D), F32),
                   jax.ShapeDtypeStruct((NQ, QW, 2 * QW), F32)],
        input_output_aliases={0: 0},
        scratch_shapes=[pltpu.VMEM((tm + 8, D), F32), pltpu.VMEM((tm + 8, D), F32), pltpu.VMEM((tm, D), F32),
                        pltpu.VMEM((tm, D), BF), pltpu.VMEM((tm, 2 * D), F32), pltpu.VMEM((tm, D), F32),
                        pltpu.VMEM((tm, D), F32), pltpu.VMEM((tm, 2 * D), BF), pltpu.VMEM((tm + 8, D), F32),
                        pltpu.VMEM((8, D), F32), pltpu.VMEM((8, RC, D), F32)],
        compiler_params=_params(("arbitrary",)),
    )(dproj, dya, proj, proj, h, h, cw, cb, ba, bx, lam, wax)


def _dx_call(dproj, dx1, x, gmix, w_in_all):
    s = x.shape[0]
    tm = min(TM_DX, s)
    nt = s // tm
    nb, _, wb = w_in_all.shape

    def body(dp_ref, dx1_ref, x_ref, g_ref, w_ref, dx_ref, acc_ref, dh_s, accs):
        i = pl.program_id(0)

        @pl.when(i == 0)
        def _():
            accs[...] = jnp.zeros(accs.shape, F32)

        dh = _dot_nt(dp_ref[:, 0:wb], w_ref[0])
        for k in range(1, nb):
            dh = dh + _dot_nt(dp_ref[:, k * wb:(k + 1) * wb], w_ref[k])
        dh_s[...] = dh
        g = g_ref[...]

        def tail(r0):
            rows = pl.ds(r0, RC)
            xx = x_ref[rows, :]
            r = lax.rsqrt(jnp.mean(xx * xx, axis=-1, keepdims=True) + EPS)
            xh = xx * r
            d = dh_s[rows, :]
            accs[...] += d * xh
            dxh = d * g
            dx_ref[rows, :] = dx1_ref[rows, :] + r * (dxh - xh * jnp.mean(dxh * xh, axis=-1, keepdims=True))
        _rows(tm, tail, unroll=4)

        @pl.when(i == nt - 1)
        def _():
            acc_ref[...] = jnp.zeros((8, D), F32)
            acc_ref[0:1, :] = jnp.sum(accs[...], axis=0, keepdims=True)

    tile = lambda w: pl.BlockSpec((tm, w), lambda i: (i, 0))
    return pl.pallas_call(
        body, name="dx", grid=(nt,),
        in_specs=[tile(nb * wb), tile(D), tile(D), _const_spec((1, D)), _const_spec(w_in_all.shape)],
        out_specs=[tile(D), pl.BlockSpec((8, D), lambda i: (0, 0))],
        out_shape=[jax.ShapeDtypeStruct((s, D), F32), jax.ShapeDtypeStruct((8, D), F32)],
        scratch_shapes=[pltpu.VMEM((tm, D), F32), pltpu.VMEM((RC, D), F32)],
        compiler_params=_params(("arbitrary",)),
    )(dproj, dx1, x, gmix, w_in_all)


def _device_of(d):
    return (d // 4, lax.rem(d // 2, 2), lax.rem(d, 2))


def _dw_exchange_call(name, me, a, b, a_spec, b_spec, split, k1, n1, s):
    nb = NDEV // split
    r = k1 // split
    ts = min(TS_DW, s)
    ns = s // ts

    def slab(i, me_ref):
        return lax.rem(me_ref[0] // split + 1 + i, nb)

    def body(me_ref, a_ref, b_ref, own_ref, recv_ref, acc, sbuf, send_sems, recv_sems, local_sem):
        i = pl.program_id(0)
        j = pl.program_id(1)
        me = _lin(_me())
        p = _dot_tn(a_ref[...], b_ref[...])

        @pl.when(j == 0)
        def _():
            acc[...] = p

        @pl.when(j > 0)
        def _():
            acc[...] += p

        def send(step, h):
            d = slab(step, me_ref) * split + h
            cp = pltpu.make_async_remote_copy(
                src_ref=sbuf.at[lax.rem(step, 2), pl.ds(h * r, r)], dst_ref=recv_ref.at[me],
                send_sem=send_sems.at[d], recv_sem=recv_sems.at[me],
                device_id=_device_of(d), device_id_type=MESH)
            return cp, d != me

        def drain(step):
            for h in range(split):
                cp, sent = send(step, h)
                pl.when(sent)(cp.wait_send)

        @pl.when(j == ns - 1)
        def _():
            pl.when(i >= 2)(lambda: drain(i - 2))
            sbuf[lax.rem(i, 2)] = acc[...].astype(BF)
            for h in range(split):
                cp, sent = send(i, h)
                pl.when(sent)(cp.start)

            @pl.when(i == nb - 1)
            def _():
                off = pl.multiple_of(lax.rem(me, split) * r, RC)
                own_ref[...] = acc[pl.ds(off, r), :]
                mine = pltpu.make_async_copy(sbuf.at[lax.rem(i, 2), pl.ds(off, r)], recv_ref.at[me], local_sem)
                mine.start()
                drain(i - 1)
                drain(i)
                for d in range(NDEV):
                    wait = pltpu.make_async_remote_copy(
                        src_ref=sbuf.at[0, pl.ds(0, r)], dst_ref=recv_ref.at[d],
                        send_sem=send_sems.at[d], recv_sem=recv_sems.at[d],
                        device_id=_device_of(d), device_id_type=MESH).wait_recv
                    pl.when(d != me)(wait)
                mine.wait()

    grid_spec = pltpu.PrefetchScalarGridSpec(
        num_scalar_prefetch=1, grid=(nb, ns),
        in_specs=[a_spec(ts, slab), b_spec(ts, slab)],
        out_specs=[pl.BlockSpec((r, n1), lambda i, j, me_ref: (0, 0)), pl.BlockSpec(memory_space=pl.ANY)],
        scratch_shapes=[pltpu.VMEM((k1, n1), F32), pltpu.VMEM((2, k1, n1), BF),
                        pltpu.SemaphoreType.DMA((NDEV,)), pltpu.SemaphoreType.DMA((NDEV,)),
                        pltpu.SemaphoreType.DMA(())])
    return pl.pallas_call(
        body, name=name, grid_spec=grid_spec,
        out_shape=[jax.ShapeDtypeStruct((r, n1), F32), jax.ShapeDtypeStruct((NDEV, r, n1), BF)],
        compiler_params=_params(("arbitrary", "arbitrary")),
    )(me, a, b)


def _rows2d(w):
    return lambda ts, slab: pl.BlockSpec((ts, w), lambda i, j, me_ref: (j, 0))


def _cols2d(w):
    return lambda ts, slab: pl.BlockSpec((ts, w), lambda i, j, me_ref: (j, slab(i, me_ref)))


def _blk3d(w):
    return lambda ts, slab: pl.BlockSpec((None, ts, w), lambda i, j, me_ref: (slab(i, me_ref), j, 0))


_BC1 = 1.0 - ADAM_B1 ** ADAM_STEP
_BC2 = 1.0 - ADAM_B2 ** ADAM_STEP


def _adamw_math(w, g, m, v):
    m = ADAM_B1 * m + (1.0 - ADAM_B1) * g
    v = ADAM_B2 * v + (1.0 - ADAM_B2) * (g * g)
    m_hat = m / _BC1
    v_hat = v / _BC2
    delta = -ADAM_LR * (m_hat / (jnp.sqrt(v_hat) + ADAM_EPS) + ADAM_WD * w)
    return delta, m, v


def _row_tile(r):
    for t in (256, 176, 128, 64, 32, 16, 8):
        if r % t == 0:
            return t
    return r


def _reduce_adamw_call(name, me, own, recv, w, m, v):
    r, c = own.shape
    tr = _row_tile(r)

    def body(me_ref, own_ref, recv_ref, w_ref, m_ref, v_ref, g_ref, d_ref, nm_ref, nv_ref):
        mine = me_ref[0]
        g = jnp.zeros((tr, c), F32)
        for sdev in range(NDEV):
            g = g + jnp.where(mine == sdev, own_ref[...], recv_ref[sdev].astype(F32))
        g_ref[...] = g
        d_ref[...], nm_ref[...], nv_ref[...] = _adamw_math(w_ref[...], g, m_ref[...], v_ref[...])

    tile = pl.BlockSpec((tr, c), lambda i, me_ref: (i, 0))
    grid_spec = pltpu.PrefetchScalarGridSpec(
        num_scalar_prefetch=1, grid=(r // tr,),
        in_specs=[tile, pl.BlockSpec((NDEV, tr, c), lambda i, me_ref: (0, i, 0)), tile, tile, tile],
        out_specs=[tile] * 4)
    return pl.pallas_call(
        body, name=name, grid_spec=grid_spec,
        out_shape=[jax.ShapeDtypeStruct((r, c), F32)] * 4,
        compiler_params=_params(("parallel",)),
    )(me, own, recv, w, m, v)


def _adamw_call(name, w, g, m, v):
    r, c = w.shape
    tr = _row_tile(r)

    def body(w_ref, g_ref, m_ref, v_ref, d_ref, nm_ref, nv_ref):
        d_ref[...], nm_ref[...], nv_ref[...] = _adamw_math(w_ref[...], g_ref[...], m_ref[...], v_ref[...])

    tile = pl.BlockSpec((tr, c), lambda i: (i, 0))
    return pl.pallas_call(
        body, name=name, grid=(r // tr,), in_specs=[tile] * 4, out_specs=[tile] * 3,
        out_shape=[jax.ShapeDtypeStruct((r, c), F32)] * 3,
        compiler_params=_params(("parallel",)),
    )(w, g, m, v)


def _me():
    return lax.axis_index("x"), lax.axis_index("y"), lax.axis_index("c")


def _flip(pos, r):
    x, y, c = pos
    return (1 - x if r & 4 else x, 1 - y if r & 2 else y, 1 - c if r & 1 else c)


def _lin(pos):
    return pos[0] * 4 + pos[1] * 2 + pos[2]


def _all_gather_call(shards):
    n = len(shards)
    chips = (4, 2, 6)

    def body(*refs):
        ins, outs = refs[:n], refs[n:2 * n]
        send_sems, recv_sems, local_sems = refs[2 * n:]
        me = _me()

        def copy(a, k, block, to, src=None):
            dst = outs[a].at[_lin(block)]
            return pltpu.make_async_remote_copy(
                src_ref=dst if src is None else src, dst_ref=dst,
                send_sem=send_sems.at[a, k], recv_sem=recv_sems.at[a, k], device_id=to, device_id_type=MESH)

        sibling = _flip(me, 1)
        mine, first, passed = [], [], []
        for a in range(n):
            cp = pltpu.make_async_copy(ins[a], outs[a].at[_lin(me)], local_sems.at[a])
            cp.start()
            mine.append(cp)
            first.append(copy(a, 0, me, sibling, src=ins[a]))
            for j, f in enumerate(chips):
                first.append(copy(a, 1 + j, me, _flip(me, f), src=ins[a]))
        for cp in first:
            cp.start()
        for j, f in enumerate(chips):
            for a in range(n):
                copy(a, 1 + j, _flip(me, f), me).wait_recv()
                cp = copy(a, 4 + j, _flip(me, f), sibling)
                cp.start()
                passed.append(cp)
        for a in range(n):
            copy(a, 0, sibling, me).wait_recv()
            for j, f in enumerate(chips):
                copy(a, 4 + j, _flip(me, f | 1), me).wait_recv()
        for cp in first + passed:
            cp.wait_send()
        for cp in mine:
            cp.wait()

    any_spec = pl.BlockSpec(memory_space=pl.ANY)
    return pl.pallas_call(
        body, name="all_gather",
        in_specs=[any_spec] * n, out_specs=[any_spec] * n,
        out_shape=[jax.ShapeDtypeStruct((NDEV,) + sh.shape, sh.dtype) for sh in shards],
        scratch_shapes=[pltpu.SemaphoreType.DMA((n, 7)), pltpu.SemaphoreType.DMA((n, 7)),
                        pltpu.SemaphoreType.DMA((n,))],
    )(*shards)


def _small_allreduce_call(g):
    def body(g_ref, out_ref, rbuf, send1, recv1, send2, recv2):
        me = _me()
        mi = _lin(me)

        def rows(d):
            return pl.ds(pl.multiple_of(d * SMALL_PER, 8), SMALL_PER)

        sent = []
        for r in range(1, NDEV):
            peer = _flip(me, r)
            cp = pltpu.make_async_remote_copy(
                src_ref=g_ref.at[rows(_lin(peer))], dst_ref=rbuf.at[mi],
                send_sem=send1.at[r - 1], recv_sem=recv1.at[r - 1], device_id=peer, device_id_type=MESH)
            cp.start()
            sent.append(cp)
        rbuf[mi] = g_ref[rows(mi), :]
        for r in range(1, NDEV):
            peer = _flip(me, r)
            pltpu.make_async_remote_copy(
                src_ref=g_ref.at[rows(mi)], dst_ref=rbuf.at[_lin(peer)],
                send_sem=send1.at[r - 1], recv_sem=recv1.at[r - 1], device_id=peer, device_id_type=MESH).wait_recv()
        for cp in sent:
            cp.wait_send()
        tot = rbuf[0]
        for d in range(1, NDEV):
            tot = tot + rbuf[d]
        out_ref[rows(mi), :] = tot
        sent = []
        for r in range(1, NDEV):
            peer = _flip(me, r)
            cp = pltpu.make_async_remote_copy(
                src_ref=out_ref.at[rows(mi)], dst_ref=out_ref.at[rows(mi)],
                send_sem=send2.at[r - 1], recv_sem=recv2.at[r - 1], device_id=peer, device_id_type=MESH)
            cp.start()
            sent.append(cp)
        for r in range(1, NDEV):
            peer = _flip(me, r)
            pltpu.make_async_remote_copy(
                src_ref=out_ref.at[rows(mi)], dst_ref=out_ref.at[rows(_lin(peer))],
                send_sem=send2.at[r - 1], recv_sem=recv2.at[r - 1], device_id=peer, device_id_type=MESH).wait_recv()
        for cp in sent:
            cp.wait_send()

    vm = pl.BlockSpec(memory_space=pltpu.VMEM)
    return pl.pallas_call(
        body, name="small_allreduce", in_specs=[vm], out_specs=vm,
        out_shape=jax.ShapeDtypeStruct((SMALL_ROWS, D), F32),
        scratch_shapes=[pltpu.VMEM((NDEV, SMALL_PER, D), F32)] + [pltpu.SemaphoreType.DMA((NDEV - 1,))] * 4,
    )(g)


def _head_blocks(w):
    z = jnp.zeros((64, 64), w.dtype)
    groups = []
    for q in range(NQ):
        rows = [jnp.concatenate([w[4 * q + a] if a == b else z for b in range(4)], axis=1) for a in range(4)]
        groups.append(jnp.concatenate(rows, axis=0))
    return jnp.stack(groups)


def _head_unblocks(g):
    return jnp.stack([g[q, 64 * a:64 * a + 64, 64 * a:64 * a + 64] for q in range(NQ) for a in range(4)])


def _local_step(x, tgt, p, me):
    s = x.shape[0]
    vec = lambda a: a.reshape(1, D)
    gmix, gffn, gfin = vec(p["norm_mix_g"]), vec(p["norm_ffn_g"]), vec(p["norm_final_g"])
    cb, ba, bx, lam = vec(p["conv_b"]), vec(p["rg_ba"]), vec(p["rg_bx"]), vec(p["rg_lambda"])
    lng, lnb = vec(p["sgu_ln_g"]), vec(p["sgu_ln_b"])
    cw = p["conv_w"]
    wax = jnp.concatenate([_head_blocks(p["rg_wa"]), _head_blocks(p["rg_wx"])], axis=2).astype(BF)
    tril = jnp.tril(jnp.ones((CHUNK, CHUNK), bool))
    ws = jnp.where(tril[None], p["sgu_ws"], 0.0)
    wtr = ws.astype(BF)
    wtrt = jnp.swapaxes(ws, 1, 2).astype(BF)
    bias = jnp.repeat(p["sgu_bs"].T, GW, axis=1)
    w_in, wgu, wdn = p["w_in"], p["w_gate_up"], p["w_down"]
    wpa, wpb, wo = p["w_proj_a"], p["w_proj_b"], p["w_out"]
    nb, _, wb = w_in.shape
    fb = wgu.shape[2]
    nh = wdn.shape[0]

    proj, h1 = _proj_call(x, gmix, w_in)
    h, pa, pb, x1, ya, yb, mg = _mixer_fwd_call(proj, x, cw, cb, ba, bx, lam, lng, lnb, wax, wtr, bias, wpa, wpb, wo)
    dx1, dx1b, h2, act, dgu, dx2b, facc = _ffn_call(x1, tgt, gffn, gfin, wgu, wdn)
    dproj, dya, dpa, dpb, bvec, dws, dbs = _mixer_bwd_pre_call(dx1b, proj, pa, pb, lng, lnb, wtr, wtrt, bias,
                                                             wpa, wpb, wo)
    dproj, svec, dwax = _mixer_bwd_seq_call(dproj, dya, proj, h, cw, cb, ba, bx, lam, wax)
    dx, xacc = _dx_call(dproj, dx1, x, gmix, w_in)

    assert nb == NDEV and 2 * nh == NDEV
    dw = {
        "w_gate_up": _dw_exchange_call("dw_gate_up", me, h2, dgu, _rows2d(D), _blk3d(fb), 1, D, fb, s),
        "w_down": _dw_exchange_call("dw_down", me, act, dx2b, _blk3d(fb), _rows2d(D), 2, fb, D, s),
        "w_proj_a": _dw_exchange_call("dw_proj_a", me, ya, dpa, _cols2d(QW), _rows2d(D), 2, QW, D, s),
        "w_proj_b": _dw_exchange_call("dw_proj_b", me, yb, dpb, _cols2d(QW), _rows2d(D), 2, QW, D, s),
        "w_out": _dw_exchange_call("dw_out", me, mg, dx1b, _cols2d(QW), _rows2d(D), 2, QW, D, s),
        "w_in": _dw_exchange_call("dw_in", me, h1, dproj, _rows2d(D), _cols2d(wb), 1, D, wb, s),
    }
    small = {
        "norm_mix_g": xacc[0], "norm_ffn_g": facc[0], "norm_final_g": facc[1],
        "conv_b": svec[0], "rg_ba": svec[1], "rg_bx": svec[2], "rg_lambda": svec[3], "conv_w": svec[4:8],
        "sgu_ln_g": bvec[0], "sgu_ln_b": bvec[1],
        "rg_wa": _head_unblocks(dwax[:, :, 0:QW]), "rg_wx": _head_unblocks(dwax[:, :, QW:2 * QW]),
        "sgu_ws": dws, "sgu_bs": dbs[:, ::GW].T,
    }
    loss_sum = (0.5 / D) * jnp.sum(facc[2])
    return loss_sum, dx, dw, small


_BIG = ("w_in", "w_gate_up", "w_down", "w_proj_a", "w_proj_b", "w_out")
_VEC_ROWS = ("norm_mix_g", "norm_ffn_g", "norm_final_g", "conv_b", "rg_ba", "rg_bx", "rg_lambda",
             "sgu_ln_g", "sgu_ln_b", "sgu_bs")
_WEIGHTS = ("norm_mix_g", "w_in", "conv_w", "conv_b", "rg_wa", "rg_ba", "rg_wx", "rg_bx", "rg_lambda",
            "sgu_ln_g", "sgu_ln_b", "sgu_ws", "sgu_bs", "w_proj_a", "w_proj_b", "w_out", "norm_ffn_g",
            "w_gate_up", "w_down", "norm_final_g")


def _pack_small(t, conv_w):
    head = jnp.concatenate([t[k].reshape(1, D) for k in _VEC_ROWS] + [conv_w, jnp.zeros((2, D), F32)], axis=0)
    return jnp.concatenate([head, t["rg_wa"].reshape(64, D), t["rg_wx"].reshape(64, D), t["sgu_ws"].reshape(128, D),
                            jnp.zeros((SMALL_ROWS - 272, D), F32)], axis=0)


def _unpack_small(a):
    out = {k: a[j] for j, k in enumerate(_VEC_ROWS)}
    out["conv_w"] = a[10:14]
    out["rg_wa"] = a[16:80].reshape(16, 64, 64)
    out["rg_wx"] = a[80:144].reshape(16, 64, 64)
    out["sgu_ws"] = a[144:272].reshape(NGRP, CHUNK, CHUNK)
    return out


def kernel(x, norm_mix_g, w_in, conv_w, conv_b, rg_wa, rg_ba, rg_wx, rg_bx, rg_lambda, sgu_ln_g, sgu_ln_b, sgu_ws, sgu_bs, w_proj_a, w_proj_b, w_out, norm_ffn_g, w_gate_up, w_down, norm_final_g, loss_target, m_norm_mix_g, m_w_in, m_conv_w, m_conv_b, m_rg_wa, m_rg_ba, m_rg_wx, m_rg_bx, m_rg_lambda, m_sgu_ln_g, m_sgu_ln_b, m_sgu_ws, m_sgu_bs, m_w_proj_a, m_w_proj_b, m_w_out, m_norm_ffn_g, m_w_gate_up, m_w_down, m_norm_final_g, v_norm_mix_g, v_w_in, v_conv_w, v_conv_b, v_rg_wa, v_rg_ba, v_rg_wx, v_rg_bx, v_rg_lambda, v_sgu_ln_g, v_sgu_ln_b, v_sgu_ws, v_sgu_bs, v_w_proj_a, v_w_proj_b, v_w_out, v_norm_ffn_g, v_w_gate_up, v_w_down, v_norm_final_g):
    args = dict(locals())
    w = {k: args[k] for k in _WEIGHTS}
    m = {k: args["m_" + k] for k in _WEIGHTS}
    v = {k: args["v_" + k] for k in _WEIGHTS}
    for d in (w, m, v):
        for k in _WEIGHTS:
            if k != "norm_final_g":
                d[k] = d[k][0]
    me = _lin(_me())
    me1 = me.reshape(1).astype(jnp.int32)

    shards = [w[k].astype(BF) for k in _BIG] + [w["conv_w"]]
    gathered = _all_gather_call(shards)
    full = dict(zip(_BIG, gathered[:-1]))
    p = {k: w[k] for k in _WEIGHTS if k not in _BIG and k != "conv_w"}
    p["w_in"] = full["w_in"]
    p["w_gate_up"] = full["w_gate_up"]
    p["w_down"] = full["w_down"].reshape(NDEV // 2, -1, D)
    for k in ("w_proj_a", "w_proj_b", "w_out"):
        p[k] = full[k].reshape(D, D)
    p["conv_w"] = jnp.swapaxes(gathered[-1], 0, 1).reshape(4, D)

    loss_sum, dx, dw, small = _local_step(x[0], loss_target[0], p, me1)
    loss = lax.psum(loss_sum, ("x", "y", "c"))

    grads, delta, new_m, new_v = {}, {}, {}, {}
    for k in _BIG:
        own, recv = dw[k]
        grads[k], delta[k], new_m[k], new_v[k] = _reduce_adamw_call("adamw_" + k, me1, own, recv, w[k], m[k], v[k])

    gsum = _small_allreduce_call(_pack_small(small, small["conv_w"]))
    zc = jnp.zeros((4, D), F32)
    d_s, m_s, v_s = _adamw_call("adamw_small", _pack_small(w, zc), gsum, _pack_small(m, zc), _pack_small(v, zc))
    gs, ds, ms, vs = _unpack_small(gsum), _unpack_small(d_s), _unpack_small(m_s), _unpack_small(v_s)
    g_cw = lax.dynamic_slice(gs["conv_w"], (0, me * 128), (4, 128))
    ds["conv_w"], ms["conv_w"], vs["conv_w"] = _adamw_call("adamw_conv_w", w["conv_w"], g_cw, m["conv_w"], v["conv_w"])
    gs["conv_w"] = g_cw
    for k in _WEIGHTS:
        if k not in _BIG:
            shp = w[k].shape
            grads[k], delta[k], new_m[k], new_v[k] = (t[k].reshape(shp) for t in (gs, ds, ms, vs))

    def lift(t, k):
        return t[k] if k == "norm_final_g" else t[k][None]

    outs = [loss, dx[None]]
    for t in (grads, delta, new_m, new_v):
        outs += [lift(t, k) for k in _WEIGHTS]
    return tuple(outs)
```

```python
import functools

import jax
import jax.numpy as jnp
from jax import lax
from jax.experimental import pallas as pl
from jax.experimental.pallas import tpu as pltpu

F32 = jnp.float32
BF = jnp.bfloat16

D = 1024
NDEV = 8
EPS = 1e-6
RG_C = 8.0
CHUNK = 128
NGRP = 8
GW = 128
NQ = 4
QW = 256
RC = 16
SMALL_ROWS = 320
SMALL_PER = SMALL_ROWS // NDEV

ADAM_LR = 0.001
ADAM_B1 = 0.9
ADAM_B2 = 0.999
ADAM_EPS = 1e-08
ADAM_WD = 0.01
ADAM_STEP = 10

VMEM_LIMIT = 60 * 1024 * 1024

MESH = pl.DeviceIdType.MESH


def _rows(n, fn, unroll=2, rc=RC):
    def body(i, c):
        fn(pl.multiple_of(i * rc, rc))
        return c
    lax.fori_loop(0, n // rc, body, 0, unroll=unroll)


def _dot(a, b):
    return jnp.dot(a, b, preferred_element_type=F32)


def _dot_nt(a, b):
    return lax.dot_general(a, b, (((1,), (1,)), ((), ())), preferred_element_type=F32)


def _dot_tn(a, b):
    return lax.dot_general(a, b, (((0,), (0,)), ((), ())), preferred_element_type=F32)


_GC = 0.7978845608028654
_GK = 0.044715


def _gelu(x):
    t = jnp.tanh(_GC * (x + _GK * (x * x * x)))
    return x * (0.5 * (1.0 + t))


def _gelu_grad(x):
    x2 = x * x
    t = jnp.tanh(_GC * (x + _GK * (x2 * x)))
    cdf = 0.5 * (1.0 + t)
    dg = cdf + (0.5 * x) * (1.0 - t * t) * (_GC * (1.0 + (3.0 * _GK) * x2))
    return x * cdf, dg


def _sigmoid(x):
    return jax.nn.sigmoid(x)


def _log1p(e):
    u = 1.0 + e
    d = u - 1.0
    return jnp.where(d == 0.0, e, jnp.log(u) * (e / jnp.where(d == 0.0, 1.0, d)))


def _softplus(z):
    return jnp.maximum(z, 0.0) + _log1p(jnp.exp(-jnp.abs(z)))


def _neg_expm1(z):
    u = jnp.exp(z)
    lu = jnp.log(u)
    k = (1.0 - u) * (z / jnp.where(lu == 0.0, 1.0, lu))
    small = jnp.where(lu == 0.0, -z, k)
    return jnp.where(z > -0.5, small, 1.0 - u)


def _shift_back(prev8, cur, j):
    cat = jnp.concatenate([prev8, cur], axis=0)
    return pltpu.roll(cat, j, 0)[8:8 + cur.shape[0]]


def _shift_fwd(cur, next8, j):
    cat = jnp.concatenate([cur, next8], axis=0)
    n = cat.shape[0]
    return pltpu.roll(cat, n - j, 0)[0:cur.shape[0]]


def _const_spec(shape):
    nd = len(shape)
    return pl.BlockSpec(shape, lambda *_: (0,) * nd, pipeline_mode=pl.Buffered(1))


def _params(sem):
    return pltpu.CompilerParams(dimension_semantics=sem, vmem_limit_bytes=VMEM_LIMIT)


def _gate_parts(z_ref, r0, q, ba, bx, sp):
    cs = slice(q * QW, (q + 1) * QW)
    za = z_ref[pl.ds(r0, RC), q * 2 * QW:q * 2 * QW + QW] + ba[:, cs]
    zx = z_ref[pl.ds(r0, RC), q * 2 * QW + QW:(q + 1) * 2 * QW] + bx[:, cs]
    r = _sigmoid(za)
    ig = _sigmoid(zx)
    la = (-RG_C * r) * sp[:, cs]
    return r, ig, la


TM_PROJ = 1024
TM_MIX = 256
TM_FFN = 256
TM_DX = 512
TS_DW = 1024


_CHIPS = (4, 2, 6)
_PASS_FLIPS = (0, 1, 4, 2, 6, 5, 3, 7)


class _Gather:
    def __init__(self, ins, outs, send_sems, recv_sems, local_sems):
        self.ins, self.outs = ins, outs
        self.send_sems, self.recv_sems, self.local_sems = send_sems, recv_sems, local_sems
        self.me = _me()
        self.sibling = _flip(self.me, 1)

    def _copy(self, a, kind, block, to, src=None):
        dst = self.outs[a].at[_lin(block)]
        return pltpu.make_async_remote_copy(
            src_ref=dst if src is None else src, dst_ref=dst,
            send_sem=self.send_sems.at[a, kind], recv_sem=self.recv_sems.at[a, kind],
            device_id=to, device_id_type=MESH)

    def _local(self, a):
        return pltpu.make_async_copy(self.ins[a], self.outs[a].at[_lin(self.me)], self.local_sems.at[a])

    def start(self):
        for a in range(len(self.ins)):
            self._local(a).start()
            self._copy(a, 0, self.me, self.sibling, src=self.ins[a]).start()
            for j, f in enumerate(_CHIPS):
                self._copy(a, 1 + j, self.me, _flip(self.me, f), src=self.ins[a]).start()

    def forward(self):
        for j, f in enumerate(_CHIPS):
            for a in range(len(self.ins)):
                self._copy(a, 1 + j, _flip(self.me, f), self.me).wait_recv()
                self._copy(a, 4 + j, _flip(self.me, f), self.sibling).start()

    def finish(self):
        for a in range(len(self.ins)):
            self._copy(a, 0, self.sibling, self.me).wait_recv()
            for j, f in enumerate(_CHIPS):
                self._copy(a, 4 + j, _flip(self.me, f | 1), self.me).wait_recv()
            self._copy(a, 0, self.me, self.sibling, src=self.ins[a]).wait_send()
            for j, f in enumerate(_CHIPS):
                self._copy(a, 1 + j, self.me, _flip(self.me, f), src=self.ins[a]).wait_send()
                self._copy(a, 4 + j, _flip(self.me, f), self.sibling).wait_send()
            self._local(a).wait()


def _gather_sems(n):
    return [pltpu.SemaphoreType.DMA((n, 7)), pltpu.SemaphoreType.DMA((n, 7)), pltpu.SemaphoreType.DMA((n,))]


def _proj_gather_call(x, gmix, w_shard, order, extras):
    s = x.shape[0]
    tm = min(TM_PROJ, s)
    nt = s // tm
    wb = w_shard.shape[1]
    n = len(extras)

    def body(order_ref, x_ref, g_ref, wsh_ref, *rest):
        ex_in = rest[:n]
        proj_ref, h1_hbm, wall_hbm = rest[n:n + 3]
        ex_out = rest[n + 3:2 * n + 3]
        wall, hc, ws_send, ws_recv, ex_send, ex_recv, ex_local, out_sems = rest[2 * n + 3:]
        k = pl.program_id(0)
        i = pl.program_id(1)
        me = _me()
        sibling = _flip(me, 1)
        gather = _Gather(ex_in, ex_out, ex_send, ex_recv, ex_local)

        def wcopy(kind, block, to):
            ref = wall.at[_lin(block)]
            return pltpu.make_async_remote_copy(
                src_ref=ref, dst_ref=ref, send_sem=ws_send.at[kind], recv_sem=ws_recv.at[kind],
                device_id=to, device_id_type=MESH)

        head = i == 0

        @pl.when(head & (k == 0))
        def _():
            wall[_lin(me)] = wsh_ref[...]
            wcopy(0, me, sibling).start()
            for j, f in enumerate(_CHIPS):
                wcopy(1 + j, me, _flip(me, f)).start()
            gather.start()

        @pl.when(head & (k == 1))
        def _():
            wcopy(0, sibling, me).wait_recv()

        for j, f in enumerate(_CHIPS):
            @pl.when(head & (k == 2 + j))
            def _(j=j, f=f):
                wcopy(1 + j, _flip(me, f), me).wait_recv()
                wcopy(4 + j, _flip(me, f), sibling).start()

            @pl.when(head & (k == 5 + j))
            def _(j=j, f=f):
                wcopy(4 + j, _flip(me, f | 1), me).wait_recv()

        pl.when(head & (k == 5))(gather.forward)

        base = pl.multiple_of(i * tm, tm)

        @pl.when(k == 0)
        def _():
            g = g_ref[...]

            def norm(r0):
                xx = x_ref[pl.ds(r0, RC), :]
                r = lax.rsqrt(jnp.mean(xx * xx, axis=-1, keepdims=True) + EPS)
                hc[pl.ds(base + r0, RC), :] = ((xx * r) * g).astype(BF)
            _rows(tm, norm, unroll=4)

        proj_ref[...] = _dot(hc[pl.ds(base, tm), :], wall[order_ref[k]])

        @pl.when((k == NDEV - 1) & (i == nt - 1))
        def _():
            out_w = pltpu.make_async_copy(wall, wall_hbm, out_sems.at[0])
            out_h = pltpu.make_async_copy(hc, h1_hbm, out_sems.at[1])
            out_w.start()
            out_h.start()
            wcopy(0, me, sibling).wait_send()
            for j, f in enumerate(_CHIPS):
                wcopy(1 + j, me, _flip(me, f)).wait_send()
                wcopy(4 + j, _flip(me, f), sibling).wait_send()
            gather.finish()
            out_w.wait()
            out_h.wait()

    any_spec = pl.BlockSpec(memory_space=pl.ANY)
    grid_spec = pltpu.PrefetchScalarGridSpec(
        num_scalar_prefetch=1, grid=(NDEV, nt),
        in_specs=[pl.BlockSpec((tm, D), lambda k, i, o: (jnp.where(k == 0, i, nt - 1), 0)),
                  pl.BlockSpec((1, D), lambda k, i, o: (0, 0)),
                  pl.BlockSpec(w_shard.shape, lambda k, i, o: (0, 0))] + [any_spec] * n,
        out_specs=[pl.BlockSpec((tm, wb), lambda k, i, o: (i, o[k])), any_spec, any_spec] + [any_spec] * n,
        scratch_shapes=[pltpu.VMEM((NDEV,) + w_shard.shape, BF), pltpu.VMEM((s, D), BF),
                        pltpu.SemaphoreType.DMA((7,)), pltpu.SemaphoreType.DMA((7,))] + _gather_sems(n)
        + [pltpu.SemaphoreType.DMA((2,))])
    outs = pl.pallas_call(
        body, name="proj_gather", grid_spec=grid_spec,
        out_shape=[jax.ShapeDtypeStruct((s, NDEV * wb), F32), jax.ShapeDtypeStruct((s, D), BF),
                   jax.ShapeDtypeStruct((NDEV,) + w_shard.shape, BF)]
        + [jax.ShapeDtypeStruct((NDEV,) + e.shape, e.dtype) for e in extras],
        compiler_params=_params(("arbitrary", "arbitrary")),
    )(order, x, gmix, w_shard, *extras)
    return outs[0], outs[1], outs[2], outs[3:]


def _conv_chunk(src_ref, ext, xc_s, xcb_s, cw_ref, cb, r0):
    cur = src_ref[pl.ds(r0, RC), 0:D]
    ext[pl.ds(r0 + 8, RC), :] = cur
    pv = ext[pl.ds(r0, 8), :]
    xc = cb + cw_ref[3:4, :] * cur
    for j in (1, 2, 3):
        xc = xc + cw_ref[3 - j:4 - j, :] * _shift_back(pv, cur, j)
    xc_s[pl.ds(r0, RC), :] = xc
    xcb_s[pl.ds(r0, RC), :] = xc.astype(BF)


def _mixer_fwd_call(proj, x, cw, cb, ba, bx, lam, lng, lnb, wax, wtr, bias, wpa, wpb, wo, shards):
    s = x.shape[0]
    tm = min(TM_MIX, s)
    nt = s // tm
    pw = proj.shape[1]
    n = len(shards)

    def body(proj_ref, x_ref, cw_ref, cb_ref, ba_ref, bx_ref, lam_ref, lng_ref, lnb_ref, wax_ref, wtr_ref,
             bias_ref, wpa_ref, wpb_ref, wo_ref, *rest):
        sh_in = rest[:n]
        h_ref, pa_ref, pb_ref, x1_ref, ya_ref, yb_ref, mg_ref = rest[n:n + 7]
        sh_out = rest[n + 7:2 * n + 7]
        ext, xc_s, xcb_s, z_s, a_s, b_s, hc_s, vn_s, mx_s, g_send, g_recv, g_local = rest[2 * n + 7:]
        i = pl.program_id(0)
        gather = _Gather(sh_in, sh_out, g_send, g_recv, g_local)

        @pl.when(i == 0)
        def _():
            gather.start()
            ext[0:8, :] = jnp.zeros((8, D), F32)
            hc_s[...] = jnp.zeros((8, D), F32)

        pl.when(i == nt // 2)(gather.forward)

        @pl.when(i > 0)
        def _():
            ext[0:8, :] = ext[tm:tm + 8, :]

        cb = cb_ref[...]
        _rows(tm, lambda r0: _conv_chunk(proj_ref, ext, xc_s, xcb_s, cw_ref, cb, r0))

        for q in range(NQ):
            z_s[:, q * 2 * QW:(q + 1) * 2 * QW] = _dot(xcb_s[:, q * QW:(q + 1) * QW], wax_ref[q])

        sp = _softplus(-lam_ref[...])
        ba = ba_ref[...]
        bx = bx_ref[...]

        def gates(r0):
            for q in range(NQ):
                cs = slice(q * QW, (q + 1) * QW)
                _, ig, la = _gate_parts(z_s, r0, q, ba, bx, sp)
                a_s[pl.ds(r0, RC), cs] = jnp.exp(la)
                b_s[pl.ds(r0, RC), cs] = jnp.sqrt(_neg_expm1(2.0 * la)) * (ig * xc_s[pl.ds(r0, RC), cs])
        _rows(tm, gates)

        def scan(t, h):
            h = a_s[pl.ds(t, 1), :] * h + b_s[pl.ds(t, 1), :]
            h_ref[pl.ds(t, 1), :] = h
            return h
        hc_s[0:1, :] = lax.fori_loop(0, tm, scan, hc_s[0:1, :], unroll=8)

        def branch_a(r0):
            rg = proj_ref[pl.ds(r0, RC), D:2 * D]
            ya_ref[pl.ds(r0, RC), :] = (_gelu(rg) * h_ref[pl.ds(r0, RC), :]).astype(BF)
        _rows(tm, branch_a)

        lng = lng_ref[...]
        lnb = lnb_ref[...]

        def lnorm(r0):
            gv = _gelu(proj_ref[pl.ds(r0, RC), 3 * D:4 * D])
            mu = jnp.mean(gv, axis=-1, keepdims=True)
            dv = gv - mu
            var = jnp.mean(dv * dv, axis=-1, keepdims=True)
            vn_s[pl.ds(r0, RC), :] = ((dv * lax.rsqrt(var + EPS)) * lng + lnb).astype(BF)
        _rows(tm, lnorm, unroll=4)

        for c in range(tm // CHUNK):
            rs = slice(c * CHUNK, (c + 1) * CHUNK)
            for g in range(NGRP):
                cs = slice(g * GW, (g + 1) * GW)
                mx_s[rs, cs] = _dot(wtr_ref[g], vn_s[rs, cs])

        def branch_b(r0):
            u = proj_ref[pl.ds(r0, RC), 2 * D:3 * D]
            boff = pl.multiple_of(lax.rem(r0, CHUNK), RC)
            mixed = mx_s[pl.ds(r0, RC), :] + bias_ref[pl.ds(boff, RC), :]
            yb_ref[pl.ds(r0, RC), :] = (_gelu(u) * mixed).astype(BF)
        _rows(tm, branch_b)

        pa_ref[...] = _dot(ya_ref[...], wpa_ref[...])
        pb_ref[...] = _dot(yb_ref[...], wpb_ref[...])

        def merge(r0):
            ga = proj_ref[pl.ds(r0, RC), 4 * D:5 * D]
            gb = proj_ref[pl.ds(r0, RC), 5 * D:6 * D]
            mg = _sigmoid(ga) * pa_ref[pl.ds(r0, RC), :] + _sigmoid(gb) * pb_ref[pl.ds(r0, RC), :]
            mg_ref[pl.ds(r0, RC), :] = mg.astype(BF)
        _rows(tm, merge)

        x1_ref[...] = x_ref[...] + _dot(mg_ref[...], wo_ref[...])

        pl.when(i == nt - 1)(gather.finish)

    tile = lambda w: pl.BlockSpec((tm, w), lambda i: (i, 0))
    vec = _const_spec((1, D))
    any_spec = pl.BlockSpec(memory_space=pl.ANY)
    outs = pl.pallas_call(
        body, name="mixer_fwd", grid=(nt,),
        in_specs=[tile(pw), tile(D), _const_spec((4, D)), vec, vec, vec, vec, vec, vec,
                  _const_spec(wax.shape), _const_spec(wtr.shape), _const_spec(bias.shape),
                  _const_spec((D, D)), _const_spec((D, D)), _const_spec((D, D))] + [any_spec] * n,
        out_specs=[tile(D)] * 7 + [any_spec] * n,
        out_shape=[jax.ShapeDtypeStruct((s, D), F32)] * 4 + [jax.ShapeDtypeStruct((s, D), BF)] * 3
        + [jax.ShapeDtypeStruct((NDEV,) + e.shape, e.dtype) for e in shards],
        scratch_shapes=[pltpu.VMEM((tm + 8, D), F32), pltpu.VMEM((tm, D), F32), pltpu.VMEM((tm, D), BF),
                        pltpu.VMEM((tm, 2 * D), F32), pltpu.VMEM((tm, D), F32), pltpu.VMEM((tm, D), F32),
                        pltpu.VMEM((8, D), F32), pltpu.VMEM((tm, D), BF), pltpu.VMEM((tm, D), F32)]
        + _gather_sems(n),
        compiler_params=_params(("arbitrary",)),
    )(proj, x, cw, cb, ba, bx, lam, lng, lnb, wax, wtr, bias, wpa, wpb, wo, *shards)
    return outs[:7], outs[7:]


def _ffn_call(x1, tgt, gffn, gfin, wgu, wdn):
    s = x1.shape[0]
    tm = min(TM_FFN, s)
    nt = s // tm
    nh = wdn.shape[0]
    fb = wgu.shape[2]

    def body(x1_ref, tgt_ref, gffn_ref, gfin_ref, wgu_ref, wdn_ref,
             dx1_ref, dx1b_ref, h2_ref, act_ref, dgu_ref, dx2b_ref, acc_ref,
             g_s, u_s, x2_s, dx2_s, da_s, dh2_s, accs):
        i = pl.program_id(0)

        @pl.when(i == 0)
        def _():
            accs[...] = jnp.zeros(accs.shape, F32)

        gffn = gffn_ref[...]
        gfin = gfin_ref[...]

        def norm(r0):
            xx = x1_ref[pl.ds(r0, RC), :]
            r = lax.rsqrt(jnp.mean(xx * xx, axis=-1, keepdims=True) + EPS)
            h2_ref[pl.ds(r0, RC), :] = ((xx * r) * gffn).astype(BF)
        _rows(tm, norm, unroll=4)

        h2 = h2_ref[...]
        for k in range(nh):
            g_s[k] = _dot(h2, wgu_ref[k])
            u_s[k] = _dot(h2, wgu_ref[k + nh])

            def act(r0, k=k):
                g = g_s[k, pl.ds(r0, RC), :]
                act_ref[k, pl.ds(r0, RC), :] = ((g * _sigmoid(g)) * u_s[k, pl.ds(r0, RC), :]).astype(BF)
            _rows(tm, act)

        x2 = x1_ref[...]
        for k in range(nh):
            x2 = x2 + _dot(act_ref[k], wdn_ref[k])
        x2_s[...] = x2

        def head(r0):
            xx = x2_s[pl.ds(r0, RC), :]
            r = lax.rsqrt(jnp.mean(xx * xx, axis=-1, keepdims=True) + EPS)
            xh = xx * r
            err = xh * gfin - tgt_ref[pl.ds(r0, RC), :]
            accs[2] += err * err
            dy = err * (1.0 / D)
            accs[1] += dy * xh
            dxh = dy * gfin
            dx2 = r * (dxh - xh * jnp.mean(dxh * xh, axis=-1, keepdims=True))
            dx2_s[pl.ds(r0, RC), :] = dx2
            dx2b_ref[pl.ds(r0, RC), :] = dx2.astype(BF)
        _rows(tm, head, unroll=4)

        dx2b = dx2b_ref[...]
        for k in range(nh):
            da_s[...] = _dot_nt(dx2b, wdn_ref[k])

            def dact(r0, k=k):
                g = g_s[k, pl.ds(r0, RC), :]
                u = u_s[k, pl.ds(r0, RC), :]
                da = da_s[pl.ds(r0, RC), :]
                sg = _sigmoid(g)
                dgu_ref[k, pl.ds(r0, RC), :] = ((da * u) * (sg * (1.0 + g * (1.0 - sg)))).astype(BF)
                dgu_ref[k + nh, pl.ds(r0, RC), :] = (da * (g * sg)).astype(BF)
            _rows(tm, dact)

        dh2 = _dot_nt(dgu_ref[0], wgu_ref[0])
        for k in range(1, 2 * nh):
            dh2 = dh2 + _dot_nt(dgu_ref[k], wgu_ref[k])
        dh2_s[...] = dh2

        def tail(r0):
            xx = x1_ref[pl.ds(r0, RC), :]
            r = lax.rsqrt(jnp.mean(xx * xx, axis=-1, keepdims=True) + EPS)
            xh = xx * r
            dh = dh2_s[pl.ds(r0, RC), :]
            accs[0] += dh * xh
            dxh = dh * gffn
            dx1 = dx2_s[pl.ds(r0, RC), :] + r * (dxh - xh * jnp.mean(dxh * xh, axis=-1, keepdims=True))
            dx1_ref[pl.ds(r0, RC), :] = dx1
            dx1b_ref[pl.ds(r0, RC), :] = dx1.astype(BF)
        _rows(tm, tail, unroll=4)

        @pl.when(i == nt - 1)
        def _():
            acc_ref[...] = jnp.zeros((8, D), F32)
            for j in range(3):
                acc_ref[j:j + 1, :] = jnp.sum(accs[j], axis=0, keepdims=True)

    tile = lambda w: pl.BlockSpec((tm, w), lambda i: (i, 0))
    vec = _const_spec((1, D))
    return pl.pallas_call(
        body, name="ffn", grid=(nt,),
        in_specs=[tile(D), tile(D), vec, vec, _const_spec(wgu.shape), _const_spec(wdn.shape)],
        out_specs=[tile(D), tile(D), tile(D),
                   pl.BlockSpec((nh, tm, fb), lambda i: (0, i, 0)),
                   pl.BlockSpec((2 * nh, tm, fb), lambda i: (0, i, 0)),
                   tile(D), pl.BlockSpec((8, D), lambda i: (0, 0))],
        out_shape=[jax.ShapeDtypeStruct((s, D), F32), jax.ShapeDtypeStruct((s, D), BF),
                   jax.ShapeDtypeStruct((s, D), BF), jax.ShapeDtypeStruct((nh, s, fb), BF),
                   jax.ShapeDtypeStruct((2 * nh, s, fb), BF), jax.ShapeDtypeStruct((s, D), BF),
                   jax.ShapeDtypeStruct((8, D), F32)],
        scratch_shapes=[pltpu.VMEM((nh, tm, fb), F32), pltpu.VMEM((nh, tm, fb), F32), pltpu.VMEM((tm, D), F32),
                        pltpu.VMEM((tm, D), F32), pltpu.VMEM((tm, fb), F32), pltpu.VMEM((tm, D), F32),
                        pltpu.VMEM((3, RC, D), F32)],
        compiler_params=_params(("arbitrary",)),
    )(x1, tgt, gffn, gfin, wgu, wdn)


def _mixer_bwd_pre_call(dx1b, proj, pa, pb, lng, lnb, wtr, wtrt, bias, wpa, wpb, wo):
    s = dx1b.shape[0]
    tm = min(TM_MIX, s)
    nt = s // tm
    pw = proj.shape[1]

    def body(dx1b_ref, uv_ref, gg_ref, pa_ref, pb_ref, lng_ref, lnb_ref, wtr_ref, wtrt_ref, bias_ref,
             wpa_ref, wpb_ref, wo_ref,
             dproj_ref, dya_ref, dpa_ref, dpb_ref, vec_ref, dws_ref, dbs_ref,
             dm_s, dyb_s, vn_s, xh_s, rs_s, mx_s, dmx_s, dvn_s, accs, dbs_s):
        i = pl.program_id(0)

        @pl.when(i == 0)
        def _():
            accs[...] = jnp.zeros(accs.shape, F32)
            dbs_s[...] = jnp.zeros(dbs_s.shape, F32)
            dws_ref[...] = jnp.zeros(dws_ref.shape, F32)

        dm_s[...] = _dot_nt(dx1b_ref[...], wo_ref[...])

        def merge(r0):
            rows = pl.ds(r0, RC)
            dm = dm_s[rows, :]
            sa = _sigmoid(gg_ref[rows, 0:D])
            sb = _sigmoid(gg_ref[rows, D:2 * D])
            dpa = dm * sa
            dpb = dm * sb
            dpa_ref[rows, :] = dpa.astype(BF)
            dpb_ref[rows, :] = dpb.astype(BF)
            dproj_ref[rows, 4 * D:5 * D] = ((dpa * pa_ref[rows, :]) * (1.0 - sa)).astype(BF)
            dproj_ref[rows, 5 * D:6 * D] = ((dpb * pb_ref[rows, :]) * (1.0 - sb)).astype(BF)
            dproj_ref[rows, 0:2 * D] = jnp.zeros((RC, 2 * D), BF)
        _rows(tm, merge)

        dya_ref[...] = _dot_nt(dpa_ref[...], wpa_ref[...])
        dyb_s[...] = _dot_nt(dpb_ref[...], wpb_ref[...])

        lng = lng_ref[...]
        lnb = lnb_ref[...]

        def lnorm(r0):
            rows = pl.ds(r0, RC)
            gv = _gelu(uv_ref[rows, D:2 * D])
            mu = jnp.mean(gv, axis=-1, keepdims=True)
            dv = gv - mu
            var = jnp.mean(dv * dv, axis=-1, keepdims=True)
            rstd = lax.rsqrt(var + EPS)
            xh = dv * rstd
            xh_s[rows, :] = xh
            rs_s[rows, :] = jnp.broadcast_to(rstd, (RC, 128))
            vn_s[rows, :] = (xh * lng + lnb).astype(BF)
        _rows(tm, lnorm, unroll=4)

        for c in range(tm // CHUNK):
            rs = slice(c * CHUNK, (c + 1) * CHUNK)
            for g in range(NGRP):
                cs = slice(g * GW, (g + 1) * GW)
                mx_s[rs, cs] = _dot(wtr_ref[g], vn_s[rs, cs])

        def gate_u(r0):
            rows = pl.ds(r0, RC)
            boff = pl.multiple_of(lax.rem(r0, CHUNK), RC)
            gu, dgu = _gelu_grad(uv_ref[rows, 0:D])
            dyb = dyb_s[rows, :]
            mixed = mx_s[rows, :] + bias_ref[pl.ds(boff, RC), :]
            dproj_ref[rows, 2 * D:3 * D] = ((dyb * mixed) * dgu).astype(BF)
            dmx = dyb * gu
            dmx_s[rows, :] = dmx.astype(BF)
            dbs_s[pl.ds(boff, RC), :] += dmx
        _rows(tm, gate_u)

        for c in range(tm // CHUNK):
            rs = slice(c * CHUNK, (c + 1) * CHUNK)
            for g in range(NGRP):
                cs = slice(g * GW, (g + 1) * GW)
                dvn_s[rs, cs] = _dot(wtrt_ref[g], dmx_s[rs, cs])
                dws_ref[g] += _dot_nt(dmx_s[rs, cs], vn_s[rs, cs])

        def lnorm_bwd(r0):
            rows = pl.ds(r0, RC)
            dvn = dvn_s[rows, :]
            xh = xh_s[rows, :]
            accs[0] += dvn * xh
            accs[1] += dvn
            dxh = dvn * lng
            m1 = jnp.mean(dxh, axis=-1, keepdims=True)
            m2 = jnp.mean(dxh * xh, axis=-1, keepdims=True)
            dgv = rs_s[rows, 0:1] * (dxh - m1 - xh * m2)
            _, dg = _gelu_grad(uv_ref[rows, D:2 * D])
            dproj_ref[rows, 3 * D:4 * D] = (dgv * dg).astype(BF)
        _rows(tm, lnorm_bwd, unroll=4)

        @pl.when(i == nt - 1)
        def _():
            vec_ref[...] = jnp.zeros((8, D), F32)
            for j in range(2):
                vec_ref[j:j + 1, :] = jnp.sum(accs[j], axis=0, keepdims=True)
            row = lax.broadcasted_iota(jnp.int32, (CHUNK, CHUNK), 0)
            col = lax.broadcasted_iota(jnp.int32, (CHUNK, CHUNK), 1)
            for g in range(NGRP):
                dws_ref[g] = jnp.where(row >= col, dws_ref[g], 0.0)
                gs = jnp.sum(dbs_s[:, g * GW:(g + 1) * GW], axis=1, keepdims=True)
                dbs_ref[:, g * GW:(g + 1) * GW] = jnp.broadcast_to(gs, (CHUNK, GW))

    tile = lambda w: pl.BlockSpec((tm, w), lambda i: (i, 0))
    vec = _const_spec((1, D))
    return pl.pallas_call(
        body, name="mixer_bwd_pre", grid=(nt,),
        in_specs=[tile(D), pl.BlockSpec((tm, 2 * D), lambda i: (i, 1)), pl.BlockSpec((tm, 2 * D), lambda i: (i, 2)),
                  tile(D), tile(D), vec, vec, _const_spec(wtr.shape), _const_spec(wtrt.shape),
                  _const_spec(bias.shape), _const_spec((D, D)), _const_spec((D, D)), _const_spec((D, D))],
        out_specs=[tile(pw), tile(D), tile(D), tile(D), pl.BlockSpec((8, D), lambda i: (0, 0)),
                   pl.BlockSpec((NGRP, CHUNK, CHUNK), lambda i: (0, 0, 0)),
                   pl.BlockSpec((CHUNK, D), lambda i: (0, 0))],
        out_shape=[jax.ShapeDtypeStruct((s, pw), BF), jax.ShapeDtypeStruct((s, D), F32),
                   jax.ShapeDtypeStruct((s, D), BF), jax.ShapeDtypeStruct((s, D), BF),
                   jax.ShapeDtypeStruct((8, D), F32), jax.ShapeDtypeStruct((NGRP, CHUNK, CHUNK), F32),
                   jax.ShapeDtypeStruct((CHUNK, D), F32)],
        scratch_shapes=[pltpu.VMEM((tm, D), F32), pltpu.VMEM((tm, D), F32), pltpu.VMEM((tm, D), BF),
                        pltpu.VMEM((tm, D), F32), pltpu.VMEM((tm, 128), F32), pltpu.VMEM((tm, D), F32),
                        pltpu.VMEM((tm, D), BF), pltpu.VMEM((tm, D), F32), pltpu.VMEM((2, RC, D), F32),
                        pltpu.VMEM((CHUNK, D), F32)],
        compiler_params=_params(("arbitrary",)),
    )(dx1b, proj, proj, pa, pb, lng, lnb, wtr, wtrt, bias, wpa, wpb, wo)


def _mixer_bwd_seq_call(dproj, dya, proj, h, cw, cb, ba, bx, lam, wax):
    s = dya.shape[0]
    tm = min(TM_MIX, s)
    nt = s // tm
    tb = tm // 8

    def body(dproj_any, dya_ref, xg_ref, xh8_ref, h_ref, hh8_ref, cw_ref, cb_ref, ba_ref, bx_ref, lam_ref, wax_ref,
             dpa_ref, vec_ref, dwax_ref,
             ext, hext, xc_s, xcb_s, z_s, a_s, lm_s, dz_s, dxc_s, c_s, accs):
        del dproj_any
        i = pl.program_id(0)

        @pl.when(i == 0)
        def _():
            accs[...] = jnp.zeros(accs.shape, F32)
            dwax_ref[...] = jnp.zeros(dwax_ref.shape, F32)
            c_s[...] = jnp.zeros((8, D), F32)
            dxc_s[tm:tm + 8, :] = jnp.zeros((8, D), F32)

        @pl.when(i == nt - 1)
        def _():
            ext[0:8, :] = jnp.zeros((8, D), F32)
            hext[0:8, :] = jnp.zeros((8, D), F32)

        @pl.when(i < nt - 1)
        def _():
            ext[0:8, :] = xh8_ref[...]
            hext[0:8, :] = hh8_ref[...]

        cb = cb_ref[...]
        _rows(tm, lambda r0: _conv_chunk(xg_ref, ext, xc_s, xcb_s, cw_ref, cb, r0))

        for q in range(NQ):
            z_s[:, q * 2 * QW:(q + 1) * 2 * QW] = _dot(xcb_s[:, q * QW:(q + 1) * QW], wax_ref[q])

        lamv = lam_ref[...]
        sp = _softplus(-lamv)
        ba = ba_ref[...]
        bx = bx_ref[...]

        def prep(r0):
            rows = pl.ds(r0, RC)
            for q in range(NQ):
                cs = slice(q * QW, (q + 1) * QW)
                _, _, la = _gate_parts(z_s, r0, q, ba, bx, sp)
                a_s[rows, cs] = jnp.exp(la)
            hv = h_ref[rows, :]
            hext[pl.ds(r0 + 8, RC), :] = hv
            g, dg = _gelu_grad(xg_ref[rows, D:2 * D])
            dya = dya_ref[rows, :]
            lm_s[rows, :] = dya * g
            dpa_ref[rows, D:2 * D] = ((dya * hv) * dg).astype(BF)
        _rows(tm, prep)

        def scan(k, c):
            t = tm - 1 - k
            lm = lm_s[pl.ds(t, 1), :] + c
            lm_s[pl.ds(t, 1), :] = lm
            return a_s[pl.ds(t, 1), :] * lm
        c_s[0:1, :] = lax.fori_loop(0, tm, scan, c_s[0:1, :], unroll=8)

        def gate_bwd(r0):
            rows = pl.ds(r0, RC)
            hprev = _shift_back(hext[pl.ds(r0, 8), :], hext[pl.ds(r0 + 8, RC), :], 1)
            for q in range(NQ):
                cs = slice(q * QW, (q + 1) * QW)
                r, ig, la = _gate_parts(z_s, r0, q, ba, bx, sp)
                a = a_s[rows, cs]
                m = jnp.sqrt(_neg_expm1(2.0 * la))
                lm = lm_s[rows, cs]
                xc = xc_s[rows, cs]
                dixc = lm * m
                dig = dixc * xc
                dla = (lm * hprev[:, cs]) * a - ((lm * (ig * xc)) * (a * a)) / m
                accs[3, :, cs] += dla * r
                dza = (dla * (-RG_C * sp[:, cs])) * (r * (1.0 - r))
                dzx = dig * (ig * (1.0 - ig))
                accs[1, :, cs] += dza
                accs[2, :, cs] += dzx
                dz_s[rows, q * 2 * QW:q * 2 * QW + QW] = dza.astype(BF)
                dz_s[rows, q * 2 * QW + QW:(q + 1) * 2 * QW] = dzx.astype(BF)
                dxc_s[rows, cs] = dixc * ig
        _rows(tm, gate_bwd)

        for q in range(NQ):
            cs = slice(q * QW, (q + 1) * QW)
            dzq = dz_s[:, q * 2 * QW:(q + 1) * 2 * QW]
            dxc_s[0:tm, cs] += _dot_nt(dzq, wax_ref[q])
            dwax_ref[q] += _dot_tn(xcb_s[:, cs], dzq)

        def conv_bwd(r0):
            rows = pl.ds(r0, RC)
            cur = dxc_s[rows, :]
            nx = dxc_s[pl.ds(r0 + RC, 8), :]
            drx = cw_ref[3:4, :] * cur
            for j in (1, 2, 3):
                drx = drx + cw_ref[3 - j:4 - j, :] * _shift_fwd(cur, nx, j)
            dpa_ref[rows, 0:D] = drx.astype(BF)
            accs[0] += cur
            xcur = ext[pl.ds(r0 + 8, RC), :]
            xpv = ext[pl.ds(r0, 8), :]
            accs[7] += cur * xcur
            for j in (1, 2, 3):
                accs[7 - j] += cur * _shift_back(xpv, xcur, j)
        _rows(tm, conv_bwd)

        dxc_s[tm:tm + 8, :] = dxc_s[0:8, :]

        @pl.when(i == nt - 1)
        def _():
            vec_ref[...] = jnp.zeros((8, D), F32)
            for j in range(8):
                vec_ref[j:j + 1, :] = jnp.sum(accs[j], axis=0, keepdims=True)
            vec_ref[3:4, :] = vec_ref[3:4, :] * (RG_C * _sigmoid(-lamv))

    rev = lambda w: pl.BlockSpec((tm, w), lambda i: (nt - 1 - i, 0))
    halo = pl.BlockSpec((8, D), lambda i: (jnp.maximum((nt - 1 - i) * tb - 1, 0), 0))
    vec = _const_spec((1, D))
    return pl.pallas_call(
        body, name="mixer_bwd_seq", grid=(nt,),
        in_specs=[pl.BlockSpec(memory_space=pl.ANY), rev(D), rev(2 * D), halo, rev(D), halo,
                  _const_spec((4, D)), vec, vec, vec, vec, _const_spec(wax.shape)],
        out_specs=[rev(2 * D), pl.BlockSpec((8, D), lambda i: (0, 0)),
                   pl.BlockSpec((NQ, QW, 2 * QW), lambda i: (0, 0, 0))],
        out_shape=[jax.ShapeDtypeStruct(dproj.shape, BF), jax.ShapeDtypeStruct((8, D), F32),
                   jax.ShapeDtypeStruct((NQ, QW, 2 * QW), F32)],
        input_output_aliases={0: 0},
        scratch_shapes=[pltpu.VMEM((tm + 8, D), F32), pltpu.VMEM((tm + 8, D), F32), pltpu.VMEM((tm, D), F32),
                        pltpu.VMEM((tm, D), BF), pltpu.VMEM((tm, 2 * D), F32), pltpu.VMEM((tm, D), F32),
                        pltpu.VMEM((tm, D), F32), pltpu.VMEM((tm, 2 * D), BF), pltpu.VMEM((tm + 8, D), F32),
                        pltpu.VMEM((8, D), F32), pltpu.VMEM((8, RC, D), F32)],
        compiler_params=_params(("arbitrary",)),
    )(dproj, dya, proj, proj, h, h, cw, cb, ba, bx, lam, wax)


def _dx_call(dproj, dx1, x, gmix, w_in_all):
    s = x.shape[0]
    tm = min(TM_DX, s)
    nt = s // tm
    nb, _, wb = w_in_all.shape

    def body(dp_ref, dx1_ref, x_ref, g_ref, w_ref, dx_ref, acc_ref, dh_s, accs):
        i = pl.program_id(0)

        @pl.when(i == 0)
        def _():
            accs[...] = jnp.zeros(accs.shape, F32)

        dh = _dot_nt(dp_ref[:, 0:wb], w_ref[0])
        for k in range(1, nb):
            dh = dh + _dot_nt(dp_ref[:, k * wb:(k + 1) * wb], w_ref[k])
        dh_s[...] = dh
        g = g_ref[...]

        def tail(r0):
            rows = pl.ds(r0, RC)
            xx = x_ref[rows, :]
            r = lax.rsqrt(jnp.mean(xx * xx, axis=-1, keepdims=True) + EPS)
            xh = xx * r
            d = dh_s[rows, :]
            accs[...] += d * xh
            dxh = d * g
            dx_ref[rows, :] = dx1_ref[rows, :] + r * (dxh - xh * jnp.mean(dxh * xh, axis=-1, keepdims=True))
        _rows(tm, tail, unroll=4)

        @pl.when(i == nt - 1)
        def _():
            acc_ref[...] = jnp.zeros((8, D), F32)
            acc_ref[0:1, :] = jnp.sum(accs[...], axis=0, keepdims=True)

    tile = lambda w: pl.BlockSpec((tm, w), lambda i: (i, 0))
    return pl.pallas_call(
        body, name="dx", grid=(nt,),
        in_specs=[tile(nb * wb), tile(D), tile(D), _const_spec((1, D)), _const_spec(w_in_all.shape)],
        out_specs=[tile(D), pl.BlockSpec((8, D), lambda i: (0, 0))],
        out_shape=[jax.ShapeDtypeStruct((s, D), F32), jax.ShapeDtypeStruct((8, D), F32)],
        scratch_shapes=[pltpu.VMEM((tm, D), F32), pltpu.VMEM((RC, D), F32)],
        compiler_params=_params(("arbitrary",)),
    )(dproj, dx1, x, gmix, w_in_all)


def _device_of(d):
    return (d // 4, lax.rem(d // 2, 2), lax.rem(d, 2))


def _dw_exchange_call(name, me, a, b, a_spec, b_spec, split, k1, n1, s):
    nb = NDEV // split
    r = k1 // split
    ts = min(TS_DW, s)
    ns = s // ts

    def slab(i, me_ref):
        return lax.rem(me_ref[0] // split + 1 + i, nb)

    def body(me_ref, a_ref, b_ref, own_ref, recv_ref, acc, sbuf, send_sems, recv_sems, local_sem):
        i = pl.program_id(0)
        j = pl.program_id(1)
        me = _lin(_me())
        p = _dot_tn(a_ref[...], b_ref[...])

        @pl.when(j == 0)
        def _():
            acc[...] = p

        @pl.when(j > 0)
        def _():
            acc[...] += p

        def send(step, h):
            d = slab(step, me_ref) * split + h
            cp = pltpu.make_async_remote_copy(
                src_ref=sbuf.at[lax.rem(step, 2), pl.ds(h * r, r)], dst_ref=recv_ref.at[me],
                send_sem=send_sems.at[d], recv_sem=recv_sems.at[me],
                device_id=_device_of(d), device_id_type=MESH)
            return cp, d != me

        def drain(step):
            for h in range(split):
                cp, sent = send(step, h)
                pl.when(sent)(cp.wait_send)

        @pl.when(j == ns - 1)
        def _():
            pl.when(i >= 2)(lambda: drain(i - 2))
            sbuf[lax.rem(i, 2)] = acc[...].astype(BF)
            for h in range(split):
                cp, sent = send(i, h)
                pl.when(sent)(cp.start)

            @pl.when(i == nb - 1)
            def _():
                off = pl.multiple_of(lax.rem(me, split) * r, RC)
                own_ref[...] = acc[pl.ds(off, r), :]
                mine = pltpu.make_async_copy(sbuf.at[lax.rem(i, 2), pl.ds(off, r)], recv_ref.at[me], local_sem)
                mine.start()
                drain(i - 1)
                drain(i)
                for d in range(NDEV):
                    wait = pltpu.make_async_remote_copy(
                        src_ref=sbuf.at[0, pl.ds(0, r)], dst_ref=recv_ref.at[d],
                        send_sem=send_sems.at[d], recv_sem=recv_sems.at[d],
                        device_id=_device_of(d), device_id_type=MESH).wait_recv
                    pl.when(d != me)(wait)
                mine.wait()

    grid_spec = pltpu.PrefetchScalarGridSpec(
        num_scalar_prefetch=1, grid=(nb, ns),
        in_specs=[a_spec(ts, slab), b_spec(ts, slab)],
        out_specs=[pl.BlockSpec((r, n1), lambda i, j, me_ref: (0, 0)), pl.BlockSpec(memory_space=pl.ANY)],
        scratch_shapes=[pltpu.VMEM((k1, n1), F32), pltpu.VMEM((2, k1, n1), BF),
                        pltpu.SemaphoreType.DMA((NDEV,)), pltpu.SemaphoreType.DMA((NDEV,)),
                        pltpu.SemaphoreType.DMA(())])
    return pl.pallas_call(
        body, name=name, grid_spec=grid_spec,
        out_shape=[jax.ShapeDtypeStruct((r, n1), F32), jax.ShapeDtypeStruct((NDEV, r, n1), BF)],
        compiler_params=_params(("arbitrary", "arbitrary")),
    )(me, a, b)


def _rows2d(w):
    return lambda ts, slab: pl.BlockSpec((ts, w), lambda i, j, me_ref: (j, 0))


def _cols2d(w):
    return lambda ts, slab: pl.BlockSpec((ts, w), lambda i, j, me_ref: (j, slab(i, me_ref)))


def _blk3d(w):
    return lambda ts, slab: pl.BlockSpec((None, ts, w), lambda i, j, me_ref: (slab(i, me_ref), j, 0))


_BC1 = 1.0 - ADAM_B1 ** ADAM_STEP
_BC2 = 1.0 - ADAM_B2 ** ADAM_STEP


def _adamw_math(w, g, m, v):
    m = ADAM_B1 * m + (1.0 - ADAM_B1) * g
    v = ADAM_B2 * v + (1.0 - ADAM_B2) * (g * g)
    m_hat = m / _BC1
    v_hat = v / _BC2
    delta = -ADAM_LR * (m_hat / (jnp.sqrt(v_hat) + ADAM_EPS) + ADAM_WD * w)
    return delta, m, v


def _row_tile(r):
    for t in (256, 176, 128, 64, 32, 16, 8):
        if r % t == 0:
            return t
    return r


def _reduce_adamw_call(name, me, own, recv, w, m, v):
    r, c = own.shape
    tr = _row_tile(r)

    def body(me_ref, own_ref, recv_ref, w_ref, m_ref, v_ref, g_ref, d_ref, nm_ref, nv_ref):
        mine = me_ref[0]
        g = jnp.zeros((tr, c), F32)
        for sdev in range(NDEV):
            g = g + jnp.where(mine == sdev, own_ref[...], recv_ref[sdev].astype(F32))
        g_ref[...] = g
        d_ref[...], nm_ref[...], nv_ref[...] = _adamw_math(w_ref[...], g, m_ref[...], v_ref[...])

    tile = pl.BlockSpec((tr, c), lambda i, me_ref: (i, 0))
    grid_spec = pltpu.PrefetchScalarGridSpec(
        num_scalar_prefetch=1, grid=(r // tr,),
        in_specs=[tile, pl.BlockSpec((NDEV, tr, c), lambda i, me_ref: (0, i, 0)), tile, tile, tile],
        out_specs=[tile] * 4)
    return pl.pallas_call(
        body, name=name, grid_spec=grid_spec,
        out_shape=[jax.ShapeDtypeStruct((r, c), F32)] * 4,
        compiler_params=_params(("parallel",)),
    )(me, own, recv, w, m, v)


def _adamw_call(name, w, g, m, v):
    r, c = w.shape
    tr = _row_tile(r)

    def body(w_ref, g_ref, m_ref, v_ref, d_ref, nm_ref, nv_ref):
        d_ref[...], nm_ref[...], nv_ref[...] = _adamw_math(w_ref[...], g_ref[...], m_ref[...], v_ref[...])

    tile = pl.BlockSpec((tr, c), lambda i: (i, 0))
    return pl.pallas_call(
        body, name=name, grid=(r // tr,), in_specs=[tile] * 4, out_specs=[tile] * 3,
        out_shape=[jax.ShapeDtypeStruct((r, c), F32)] * 3,
        compiler_params=_params(("parallel",)),
    )(w, g, m, v)


def _me():
    return lax.axis_index("x"), lax.axis_index("y"), lax.axis_index("c")


def _flip(pos, r):
    x, y, c = pos
    return (1 - x if r & 4 else x, 1 - y if r & 2 else y, 1 - c if r & 1 else c)


def _lin(pos):
    return pos[0] * 4 + pos[1] * 2 + pos[2]


def _small_allreduce_call(g):
    def body(g_ref, out_ref, rbuf, send1, recv1, send2, recv2):
        me = _me()
        mi = _lin(me)

        def rows(d):
            return pl.ds(pl.multiple_of(d * SMALL_PER, 8), SMALL_PER)

        sent = []
        for r in range(1, NDEV):
            peer = _flip(me, r)
            cp = pltpu.make_async_remote_copy(
                src_ref=g_ref.at[rows(_lin(peer))], dst_ref=rbuf.at[mi],
                send_sem=send1.at[r - 1], recv_sem=recv1.at[r - 1], device_id=peer, device_id_type=MESH)
            cp.start()
            sent.append(cp)
        rbuf[mi] = g_ref[rows(mi), :]
        for r in range(1, NDEV):
            peer = _flip(me, r)
            pltpu.make_async_remote_copy(
                src_ref=g_ref.at[rows(mi)], dst_ref=rbuf.at[_lin(peer)],
                send_sem=send1.at[r - 1], recv_sem=recv1.at[r - 1], device_id=peer, device_id_type=MESH).wait_recv()
        for cp in sent:
            cp.wait_send()
        tot = rbuf[0]
        for d in range(1, NDEV):
            tot = tot + rbuf[d]
        out_ref[rows(mi), :] = tot
        sent = []
        for r in range(1, NDEV):
            peer = _flip(me, r)
            cp = pltpu.make_async_remote_copy(
                src_ref=out_ref.at[rows(mi)], dst_ref=out_ref.at[rows(mi)],
                send_sem=send2.at[r - 1], recv_sem=recv2.at[r - 1], device_id=peer, device_id_type=MESH)
            cp.start()
            sent.append(cp)
        for r in range(1, NDEV):
            peer = _flip(me, r)
            pltpu.make_async_remote_copy(
                src_ref=out_ref.at[rows(mi)], dst_ref=out_ref.at[rows(_lin(peer))],
                send_sem=send2.at[r - 1], recv_sem=recv2.at[r - 1], device_id=peer, device_id_type=MESH).wait_recv()
        for cp in sent:
            cp.wait_send()

    vm = pl.BlockSpec(memory_space=pltpu.VMEM)
    return pl.pallas_call(
        body, name="small_allreduce", in_specs=[vm], out_specs=vm,
        out_shape=jax.ShapeDtypeStruct((SMALL_ROWS, D), F32),
        scratch_shapes=[pltpu.VMEM((NDEV, SMALL_PER, D), F32)] + [pltpu.SemaphoreType.DMA((NDEV - 1,))] * 4,
    )(g)


def _head_blocks(w):
    z = jnp.zeros((64, 64), w.dtype)
    groups = []
    for q in range(NQ):
        rows = [jnp.concatenate([w[4 * q + a] if a == b else z for b in range(4)], axis=1) for a in range(4)]
        groups.append(jnp.concatenate(rows, axis=0))
    return jnp.stack(groups)


def _head_unblocks(g):
    return jnp.stack([g[q, 64 * a:64 * a + 64, 64 * a:64 * a + 64] for q in range(NQ) for a in range(4)])


def _local_step(x, tgt, p, me, order):
    s = x.shape[0]
    vec = lambda a: a.reshape(1, D)
    gmix, gffn, gfin = vec(p["norm_mix_g"]), vec(p["norm_ffn_g"]), vec(p["norm_final_g"])
    cb, ba, bx, lam = vec(p["conv_b"]), vec(p["rg_ba"]), vec(p["rg_bx"]), vec(p["rg_lambda"])
    lng, lnb = vec(p["sgu_ln_g"]), vec(p["sgu_ln_b"])
    wax = jnp.concatenate([_head_blocks(p["rg_wa"]), _head_blocks(p["rg_wx"])], axis=2).astype(BF)
    tril = jnp.tril(jnp.ones((CHUNK, CHUNK), bool))
    ws = jnp.where(tril[None], p["sgu_ws"], 0.0)
    wtr = ws.astype(BF)
    wtrt = jnp.swapaxes(ws, 1, 2).astype(BF)
    bias = jnp.repeat(p["sgu_bs"].T, GW, axis=1)
    shard = {k: p[k].astype(BF) for k in _BIG}

    proj, h1, w_in, (wpa, wpb, wo, cw) = _proj_gather_call(
        x, gmix, shard["w_in"], order, [shard["w_proj_a"], shard["w_proj_b"], shard["w_out"], p["conv_w"]])
    wpa, wpb, wo = (t.reshape(D, D) for t in (wpa, wpb, wo))
    cw = jnp.swapaxes(cw, 0, 1).reshape(4, D)
    (h, pa, pb, x1, ya, yb, mg), (wgu, wdn) = _mixer_fwd_call(
        proj, x, cw, cb, ba, bx, lam, lng, lnb, wax, wtr, bias, wpa, wpb, wo, [shard["w_gate_up"], shard["w_down"]])
    wdn = wdn.reshape(NDEV // 2, -1, D)
    nb, _, wb = w_in.shape
    fb = wgu.shape[2]
    nh = wdn.shape[0]
    dx1, dx1b, h2, act, dgu, dx2b, facc = _ffn_call(x1, tgt, gffn, gfin, wgu, wdn)
    dproj, dya, dpa, dpb, bvec, dws, dbs = _mixer_bwd_pre_call(dx1b, proj, pa, pb, lng, lnb, wtr, wtrt, bias,
                                                             wpa, wpb, wo)
    dproj, svec, dwax = _mixer_bwd_seq_call(dproj, dya, proj, h, cw, cb, ba, bx, lam, wax)
    dx, xacc = _dx_call(dproj, dx1, x, gmix, w_in)

    assert nb == NDEV and 2 * nh == NDEV
    dw = {
        "w_gate_up": _dw_exchange_call("dw_gate_up", me, h2, dgu, _rows2d(D), _blk3d(fb), 1, D, fb, s),
        "w_down": _dw_exchange_call("dw_down", me, act, dx2b, _blk3d(fb), _rows2d(D), 2, fb, D, s),
        "w_proj_a": _dw_exchange_call("dw_proj_a", me, ya, dpa, _cols2d(QW), _rows2d(D), 2, QW, D, s),
        "w_proj_b": _dw_exchange_call("dw_proj_b", me, yb, dpb, _cols2d(QW), _rows2d(D), 2, QW, D, s),
        "w_out": _dw_exchange_call("dw_out", me, mg, dx1b, _cols2d(QW), _rows2d(D), 2, QW, D, s),
        "w_in": _dw_exchange_call("dw_in", me, h1, dproj, _rows2d(D), _cols2d(wb), 1, D, wb, s),
    }
    small = {
        "norm_mix_g": xacc[0], "norm_ffn_g": facc[0], "norm_final_g": facc[1],
        "conv_b": svec[0], "rg_ba": svec[1], "rg_bx": svec[2], "rg_lambda": svec[3], "conv_w": svec[4:8],
        "sgu_ln_g": bvec[0], "sgu_ln_b": bvec[1],
        "rg_wa": _head_unblocks(dwax[:, :, 0:QW]), "rg_wx": _head_unblocks(dwax[:, :, QW:2 * QW]),
        "sgu_ws": dws, "sgu_bs": dbs[:, ::GW].T,
    }
    loss_sum = (0.5 / D) * jnp.sum(facc[2])
    return loss_sum, dx, dw, small


_BIG = ("w_in", "w_gate_up", "w_down", "w_proj_a", "w_proj_b", "w_out")
_VEC_ROWS = ("norm_mix_g", "norm_ffn_g", "norm_final_g", "conv_b", "rg_ba", "rg_bx", "rg_lambda",
             "sgu_ln_g", "sgu_ln_b", "sgu_bs")
_WEIGHTS = ("norm_mix_g", "w_in", "conv_w", "conv_b", "rg_wa", "rg_ba", "rg_wx", "rg_bx", "rg_lambda",
            "sgu_ln_g", "sgu_ln_b", "sgu_ws", "sgu_bs", "w_proj_a", "w_proj_b", "w_out", "norm_ffn_g",
            "w_gate_up", "w_down", "norm_final_g")


def _pack_small(t, conv_w):
    head = jnp.concatenate([t[k].reshape(1, D) for k in _VEC_ROWS] + [conv_w, jnp.zeros((2, D), F32)], axis=0)
    return jnp.concatenate([head, t["rg_wa"].reshape(64, D), t["rg_wx"].reshape(64, D), t["sgu_ws"].reshape(128, D),
                            jnp.zeros((SMALL_ROWS - 272, D), F32)], axis=0)


def _unpack_small(a):
    out = {k: a[j] for j, k in enumerate(_VEC_ROWS)}
    out["conv_w"] = a[10:14]
    out["rg_wa"] = a[16:80].reshape(16, 64, 64)
    out["rg_wx"] = a[80:144].reshape(16, 64, 64)
    out["sgu_ws"] = a[144:272].reshape(NGRP, CHUNK, CHUNK)
    return out


def kernel(x, norm_mix_g, w_in, conv_w, conv_b, rg_wa, rg_ba, rg_wx, rg_bx, rg_lambda, sgu_ln_g, sgu_ln_b, sgu_ws, sgu_bs, w_proj_a, w_proj_b, w_out, norm_ffn_g, w_gate_up, w_down, norm_final_g, loss_target, m_norm_mix_g, m_w_in, m_conv_w, m_conv_b, m_rg_wa, m_rg_ba, m_rg_wx, m_rg_bx, m_rg_lambda, m_sgu_ln_g, m_sgu_ln_b, m_sgu_ws, m_sgu_bs, m_w_proj_a, m_w_proj_b, m_w_out, m_norm_ffn_g, m_w_gate_up, m_w_down, m_norm_final_g, v_norm_mix_g, v_w_in, v_conv_w, v_conv_b, v_rg_wa, v_rg_ba, v_rg_wx, v_rg_bx, v_rg_lambda, v_sgu_ln_g, v_sgu_ln_b, v_sgu_ws, v_sgu_bs, v_w_proj_a, v_w_proj_b, v_w_out, v_norm_ffn_g, v_w_gate_up, v_w_down, v_norm_final_g):
    args = dict(locals())
    w = {k: args[k] for k in _WEIGHTS}
    m = {k: args["m_" + k] for k in _WEIGHTS}
    v = {k: args["v_" + k] for k in _WEIGHTS}
    for d in (w, m, v):
        for k in _WEIGHTS:
            if k != "norm_final_g":
                d[k] = d[k][0]
    me = _lin(_me())
    me1 = me.reshape(1).astype(jnp.int32)

    order = jnp.bitwise_xor(me, jnp.array(_PASS_FLIPS, jnp.int32)).astype(jnp.int32)

    loss_sum, dx, dw, small = _local_step(x[0], loss_target[0], w, me1, order)
    loss = lax.psum(loss_sum, ("x", "y", "c"))

    grads, delta, new_m, new_v = {}, {}, {}, {}
    for k in _BIG:
        own, recv = dw[k]
        grads[k], delta[k], new_m[k], new_v[k] = _reduce_adamw_call("adamw_" + k, me1, own, recv, w[k], m[k], v[k])

    gsum = _small_allreduce_call(_pack_small(small, small["conv_w"]))
    zc = jnp.zeros((4, D), F32)
    d_s, m_s, v_s = _adamw_call("adamw_small", _pack_small(w, zc), gsum, _pack_small(m, zc), _pack_small(v, zc))
    gs, ds, ms, vs = _unpack_small(gsum), _unpack_small(d_s), _unpack_small(m_s), _unpack_small(v_s)
    g_cw = lax.dynamic_slice(gs["conv_w"], (0, me * 128), (4, 128))
    ds["conv_w"], ms["conv_w"], vs["conv_w"] = _adamw_call("adamw_conv_w", w["conv_w"], g_cw, m["conv_w"], v["conv_w"])
    gs["conv_w"] = g_cw
    for k in _WEIGHTS:
        if k not in _BIG:
            shp = w[k].shape
            grads[k], delta[k], new_m[k], new_v[k] = (t[k].reshape(shp) for t in (gs, ds, ms, vs))

    def lift(t, k):
        return t[k] if k == "norm_final_g" else t[k][None]

    outs = [loss, dx[None]]
    for t in (grads, delta, new_m, new_v):
        outs += [lift(t, k) for k in _WEIGHTS]
    return tuple(outs)
```

```python
import functools

import jax
import jax.numpy as jnp
from jax import lax
from jax.experimental import pallas as pl
from jax.experimental.pallas import tpu as pltpu

F32 = jnp.float32
BF = jnp.bfloat16

D = 1024
NDEV = 8
EPS = 1e-6
RG_C = 8.0
CHUNK = 128
NGRP = 8
GW = 128
NQ = 4
QW = 256
RC = 16
SMALL_ROWS = 320
SMALL_PER = SMALL_ROWS // NDEV

ADAM_LR = 0.001
ADAM_B1 = 0.9
ADAM_B2 = 0.999
ADAM_EPS = 1e-08
ADAM_WD = 0.01
ADAM_STEP = 10

VMEM_LIMIT = 60 * 1024 * 1024

MESH = pl.DeviceIdType.MESH


def _rows(n, fn, unroll=2, rc=RC):
    def body(i, c):
        fn(pl.multiple_of(i * rc, rc))
        return c
    lax.fori_loop(0, n // rc, body, 0, unroll=unroll)


def _fold(v):
    return jnp.sum(v.reshape(v.shape[0] // RC, RC, v.shape[1]), axis=0)


def _dot(a, b):
    return jnp.dot(a, b, preferred_element_type=F32)


def _dot_nt(a, b):
    return lax.dot_general(a, b, (((1,), (1,)), ((), ())), preferred_element_type=F32)


def _dot_tn(a, b):
    return lax.dot_general(a, b, (((0,), (0,)), ((), ())), preferred_element_type=F32)


_GC = 0.7978845608028654
_GK = 0.044715


def _gelu(x):
    t = jnp.tanh(_GC * (x + _GK * (x * x * x)))
    return x * (0.5 * (1.0 + t))


def _gelu_grad(x):
    x2 = x * x
    t = jnp.tanh(_GC * (x + _GK * (x2 * x)))
    cdf = 0.5 * (1.0 + t)
    dg = cdf + (0.5 * x) * (1.0 - t * t) * (_GC * (1.0 + (3.0 * _GK) * x2))
    return x * cdf, dg


def _sigmoid(x):
    return jax.nn.sigmoid(x)


def _log1p(e):
    u = 1.0 + e
    d = u - 1.0
    return jnp.where(d == 0.0, e, jnp.log(u) * (e / jnp.where(d == 0.0, 1.0, d)))


def _softplus(z):
    return jnp.maximum(z, 0.0) + _log1p(jnp.exp(-jnp.abs(z)))


def _neg_expm1(z):
    u = jnp.exp(z)
    lu = jnp.log(u)
    k = (1.0 - u) * (z / jnp.where(lu == 0.0, 1.0, lu))
    small = jnp.where(lu == 0.0, -z, k)
    return jnp.where(z > -0.5, small, 1.0 - u)


def _shift_back(prev8, cur, j):
    cat = jnp.concatenate([prev8, cur], axis=0)
    return pltpu.roll(cat, j, 0)[8:8 + cur.shape[0]]


def _shift_fwd(cur, next8, j):
    cat = jnp.concatenate([cur, next8], axis=0)
    n = cat.shape[0]
    return pltpu.roll(cat, n - j, 0)[0:cur.shape[0]]


def _const_spec(shape):
    nd = len(shape)
    return pl.BlockSpec(shape, lambda *_: (0,) * nd, pipeline_mode=pl.Buffered(1))


def _params(sem):
    return pltpu.CompilerParams(dimension_semantics=sem, vmem_limit_bytes=VMEM_LIMIT)


TM_PROJ = 1024
TM_MIX = 256
TM_FFN = 256
TM_DX = 512
TS_DW = 1024


_CHIPS = (4, 2, 6)
_PASS_FLIPS = (0, 1, 4, 2, 6, 5, 3, 7)


class _Gather:
    def __init__(self, ins, outs, send_sems, recv_sems, local_sems):
        self.ins, self.outs = ins, outs
        self.send_sems, self.recv_sems, self.local_sems = send_sems, recv_sems, local_sems
        self.me = _me()
        self.sibling = _flip(self.me, 1)

    def _copy(self, a, kind, block, to, src=None):
        dst = self.outs[a].at[_lin(block)]
        return pltpu.make_async_remote_copy(
            src_ref=dst if src is None else src, dst_ref=dst,
            send_sem=self.send_sems.at[a, kind], recv_sem=self.recv_sems.at[a, kind],
            device_id=to, device_id_type=MESH)

    def _local(self, a):
        return pltpu.make_async_copy(self.ins[a], self.outs[a].at[_lin(self.me)], self.local_sems.at[a])

    def start(self):
        for a in range(len(self.ins)):
            self._local(a).start()
            self._copy(a, 0, self.me, self.sibling, src=self.ins[a]).start()
            for j, f in enumerate(_CHIPS):
                self._copy(a, 1 + j, self.me, _flip(self.me, f), src=self.ins[a]).start()

    def forward(self):
        for j, f in enumerate(_CHIPS):
            for a in range(len(self.ins)):
                self._copy(a, 1 + j, _flip(self.me, f), self.me).wait_recv()
                self._copy(a, 4 + j, _flip(self.me, f), self.sibling).start()

    def finish(self):
        for a in range(len(self.ins)):
            self._copy(a, 0, self.sibling, self.me).wait_recv()
            for j, f in enumerate(_CHIPS):
                self._copy(a, 4 + j, _flip(self.me, f | 1), self.me).wait_recv()
            self._copy(a, 0, self.me, self.sibling, src=self.ins[a]).wait_send()
            for j, f in enumerate(_CHIPS):
                self._copy(a, 1 + j, self.me, _flip(self.me, f), src=self.ins[a]).wait_send()
                self._copy(a, 4 + j, _flip(self.me, f), self.sibling).wait_send()
            self._local(a).wait()


def _gather_sems(n):
    return [pltpu.SemaphoreType.DMA((n, 7)), pltpu.SemaphoreType.DMA((n, 7)), pltpu.SemaphoreType.DMA((n,))]


def _proj_gather_call(x, gmix, w_shard, order, extras):
    s = x.shape[0]
    tm = min(TM_PROJ, s)
    nt = s // tm
    wb = w_shard.shape[1]
    n = len(extras)

    def body(order_ref, x_ref, g_ref, wsh_ref, *rest):
        ex_in = rest[:n]
        proj_ref, h1_hbm, wall_hbm = rest[n:n + 3]
        ex_out = rest[n + 3:2 * n + 3]
        wall, hc, ws_send, ws_recv, ex_send, ex_recv, ex_local, out_sems = rest[2 * n + 3:]
        k = pl.program_id(0)
        i = pl.program_id(1)
        me = _me()
        sibling = _flip(me, 1)
        gather = _Gather(ex_in, ex_out, ex_send, ex_recv, ex_local)

        def wcopy(kind, block, to):
            ref = wall.at[_lin(block)]
            return pltpu.make_async_remote_copy(
                src_ref=ref, dst_ref=ref, send_sem=ws_send.at[kind], recv_sem=ws_recv.at[kind],
                device_id=to, device_id_type=MESH)

        head = i == 0

        @pl.when(head & (k == 0))
        def _():
            wall[_lin(me)] = wsh_ref[...]
            wcopy(0, me, sibling).start()
            for j, f in enumerate(_CHIPS):
                wcopy(1 + j, me, _flip(me, f)).start()
            gather.start()

        @pl.when(head & (k == 1))
        def _():
            wcopy(0, sibling, me).wait_recv()

        for j, f in enumerate(_CHIPS):
            @pl.when(head & (k == 2 + j))
            def _(j=j, f=f):
                wcopy(1 + j, _flip(me, f), me).wait_recv()
                wcopy(4 + j, _flip(me, f), sibling).start()

            @pl.when(head & (k == 5 + j))
            def _(j=j, f=f):
                wcopy(4 + j, _flip(me, f | 1), me).wait_recv()

        pl.when(head & (k == 5))(gather.forward)

        base = pl.multiple_of(i * tm, tm)

        @pl.when(k == 0)
        def _():
            g = g_ref[...]

            def norm(r0):
                xx = x_ref[pl.ds(r0, RC), :]
                r = lax.rsqrt(jnp.mean(xx * xx, axis=-1, keepdims=True) + EPS)
                hc[pl.ds(base + r0, RC), :] = ((xx * r) * g).astype(BF)
            _rows(tm, norm, unroll=4)

        proj_ref[...] = _dot(hc[pl.ds(base, tm), :], wall[order_ref[k]])

        @pl.when((k == NDEV - 1) & (i == nt - 1))
        def _():
            out_w = pltpu.make_async_copy(wall, wall_hbm, out_sems.at[0])
            out_h = pltpu.make_async_copy(hc, h1_hbm, out_sems.at[1])
            out_w.start()
            out_h.start()
            wcopy(0, me, sibling).wait_send()
            for j, f in enumerate(_CHIPS):
                wcopy(1 + j, me, _flip(me, f)).wait_send()
                wcopy(4 + j, _flip(me, f), sibling).wait_send()
            gather.finish()
            out_w.wait()
            out_h.wait()

    any_spec = pl.BlockSpec(memory_space=pl.ANY)
    grid_spec = pltpu.PrefetchScalarGridSpec(
        num_scalar_prefetch=1, grid=(NDEV, nt),
        in_specs=[pl.BlockSpec((tm, D), lambda k, i, o: (jnp.where(k == 0, i, nt - 1), 0)),
                  pl.BlockSpec((1, D), lambda k, i, o: (0, 0)),
                  pl.BlockSpec(w_shard.shape, lambda k, i, o: (0, 0))] + [any_spec] * n,
        out_specs=[pl.BlockSpec((tm, wb), lambda k, i, o: (i, o[k])), any_spec, any_spec] + [any_spec] * n,
        scratch_shapes=[pltpu.VMEM((NDEV,) + w_shard.shape, BF), pltpu.VMEM((s, D), BF),
                        pltpu.SemaphoreType.DMA((7,)), pltpu.SemaphoreType.DMA((7,))] + _gather_sems(n)
        + [pltpu.SemaphoreType.DMA((2,))])
    outs = pl.pallas_call(
        body, name="proj_gather", grid_spec=grid_spec,
        out_shape=[jax.ShapeDtypeStruct((s, NDEV * wb), F32), jax.ShapeDtypeStruct((s, D), BF),
                   jax.ShapeDtypeStruct((NDEV,) + w_shard.shape, BF)]
        + [jax.ShapeDtypeStruct((NDEV,) + e.shape, e.dtype) for e in extras],
        compiler_params=_params(("arbitrary", "arbitrary")),
    )(order, x, gmix, w_shard, *extras)
    return outs[0], outs[1], outs[2], outs[3:]


def _conv_tile(rx, prev8, cw_ref, cb):
    xc = cb + cw_ref[3:4, :] * rx
    for j in (1, 2, 3):
        xc = xc + cw_ref[3 - j:4 - j, :] * _shift_back(prev8, rx, j)
    return xc


def _gate_tile(xcb, wax_ref, ba, bx, sp, q):
    cs = slice(q * QW, (q + 1) * QW)
    z = _dot(xcb[:, cs], wax_ref[q])
    r = _sigmoid(z[:, 0:QW] + ba[:, cs])
    ig = _sigmoid(z[:, QW:2 * QW] + bx[:, cs])
    return r, ig, (-RG_C * r) * sp[:, cs]


def _scan_fwd(a_s, b_s, out_ref, h0, n):
    rowi = lax.broadcasted_iota(jnp.int32, (8, D), 0)

    def block(t, h):
        rows = pl.ds(pl.multiple_of(t * 8, 8), 8)
        a = a_s[rows, :]
        b = b_s[rows, :]
        for d in (1, 2, 4):
            m = rowi >= d
            b = jnp.where(m, a * pltpu.roll(b, d, 0) + b, b)
            a = jnp.where(m, a * pltpu.roll(a, d, 0), a)
        hb = b + a * h
        out_ref[rows, :] = hb
        return hb[7:8, :]
    return lax.fori_loop(0, n // 8, block, h0, unroll=4)


def _scan_bwd(a_s, lm_s, c0, n):
    rowi = lax.broadcasted_iota(jnp.int32, (8, D), 0)
    nblk = n // 8

    def block(k, cin):
        rows = pl.ds(pl.multiple_of((nblk - 1 - k) * 8, 8), 8)
        a = a_s[rows, :]
        dh = lm_s[rows, :]
        b = a * dh
        for d in (1, 2, 4):
            m = rowi < 8 - d
            b = jnp.where(m, a * pltpu.roll(b, 8 - d, 0) + b, b)
            a = jnp.where(m, a * pltpu.roll(a, 8 - d, 0), a)
        mu = b + a * cin
        lm_s[rows, :] = dh + jnp.where(rowi < 7, pltpu.roll(mu, 7, 0), cin)
        return mu[0:1, :]
    return lax.fori_loop(0, nblk, block, c0, unroll=4)


def _mixer_fwd_call(proj, x, cw, cb, ba, bx, lam, lng, lnb, wax, wtr, bias, wpa, wpb, wo, shards):
    s = x.shape[0]
    tm = min(TM_MIX, s)
    nt = s // tm
    pw = proj.shape[1]
    n = len(shards)

    def body(proj_ref, x_ref, cw_ref, cb_ref, ba_ref, bx_ref, lam_ref, lng_ref, lnb_ref, wax_ref, wtr_ref,
             bias_ref, wpa_ref, wpb_ref, wo_ref, *rest):
        sh_in = rest[:n]
        h_ref, pa_ref, pb_ref, x1_ref, ya_ref, yb_ref, mg_ref = rest[n:n + 7]
        sh_out = rest[n + 7:2 * n + 7]
        prev_s, a_s, b_s, hc_s, vn_s, mx_s, g_send, g_recv, g_local = rest[2 * n + 7:]
        i = pl.program_id(0)
        gather = _Gather(sh_in, sh_out, g_send, g_recv, g_local)

        @pl.when(i == 0)
        def _():
            gather.start()
            prev_s[...] = jnp.zeros((8, D), F32)
            hc_s[...] = jnp.zeros((8, D), F32)

        pl.when(i == nt // 2)(gather.forward)

        rx = proj_ref[:, 0:D]
        xc = _conv_tile(rx, prev_s[...], cw_ref, cb_ref[...])
        prev_s[...] = rx[tm - 8:tm, :]
        xcb = xc.astype(BF)
        sp = _softplus(-lam_ref[...])
        ba = ba_ref[...]
        bx = bx_ref[...]
        for q in range(NQ):
            cs = slice(q * QW, (q + 1) * QW)
            _, ig, la = _gate_tile(xcb, wax_ref, ba, bx, sp, q)
            a_s[:, cs] = jnp.exp(la)
            b_s[:, cs] = jnp.sqrt(_neg_expm1(2.0 * la)) * (ig * xc[:, cs])

        gv = _gelu(proj_ref[:, 3 * D:4 * D])
        dv = gv - jnp.mean(gv, axis=-1, keepdims=True)
        var = jnp.mean(dv * dv, axis=-1, keepdims=True)
        vn_s[...] = ((dv * lax.rsqrt(var + EPS)) * lng_ref[...] + lnb_ref[...]).astype(BF)
        nc = tm // CHUNK
        for c in range(nc):
            rs = slice(c * CHUNK, (c + 1) * CHUNK)
            for g in range(NGRP):
                cs = slice(g * GW, (g + 1) * GW)
                mx_s[rs, cs] = _dot(wtr_ref[g], vn_s[rs, cs])
        mixed = mx_s[...] + jnp.concatenate([bias_ref[...]] * nc, axis=0)
        yb = (_gelu(proj_ref[:, 2 * D:3 * D]) * mixed).astype(BF)
        yb_ref[...] = yb
        pb_ref[...] = _dot(yb, wpb_ref[...])

        hc_s[0:1, :] = _scan_fwd(a_s, b_s, h_ref, hc_s[0:1, :], tm)

        ya = (_gelu(proj_ref[:, D:2 * D]) * h_ref[...]).astype(BF)
        ya_ref[...] = ya
        pa = _dot(ya, wpa_ref[...])
        pa_ref[...] = pa
        mg = (_sigmoid(proj_ref[:, 4 * D:5 * D]) * pa + _sigmoid(proj_ref[:, 5 * D:6 * D]) * pb_ref[...]).astype(BF)
        mg_ref[...] = mg
        x1_ref[...] = x_ref[...] + _dot(mg, wo_ref[...])

        pl.when(i == nt - 1)(gather.finish)

    tile = lambda w: pl.BlockSpec((tm, w), lambda i: (i, 0))
    vec = _const_spec((1, D))
    any_spec = pl.BlockSpec(memory_space=pl.ANY)
    outs = pl.pallas_call(
        body, name="mixer_fwd", grid=(nt,),
        in_specs=[tile(pw), tile(D), _const_spec((4, D)), vec, vec, vec, vec, vec, vec,
                  _const_spec(wax.shape), _const_spec(wtr.shape), _const_spec(bias.shape),
                  _const_spec((D, D)), _const_spec((D, D)), _const_spec((D, D))] + [any_spec] * n,
        out_specs=[tile(D)] * 7 + [any_spec] * n,
        out_shape=[jax.ShapeDtypeStruct((s, D), F32)] * 4 + [jax.ShapeDtypeStruct((s, D), BF)] * 3
        + [jax.ShapeDtypeStruct((NDEV,) + e.shape, e.dtype) for e in shards],
        scratch_shapes=[pltpu.VMEM((8, D), F32), pltpu.VMEM((tm, D), F32), pltpu.VMEM((tm, D), F32),
                        pltpu.VMEM((8, D), F32), pltpu.VMEM((tm, D), BF), pltpu.VMEM((tm, D), F32)]
        + _gather_sems(n),
        compiler_params=_params(("arbitrary",)),
    )(proj, x, cw, cb, ba, bx, lam, lng, lnb, wax, wtr, bias, wpa, wpb, wo, *shards)
    return outs[:7], outs[7:]


def _ffn_call(x1, tgt, gffn, gfin, wgu, wdn):
    s = x1.shape[0]
    tm = min(TM_FFN, s)
    nt = s // tm
    nh = wdn.shape[0]
    fb = wgu.shape[2]

    def body(x1_ref, tgt_ref, gffn_ref, gfin_ref, wgu_ref, wdn_ref,
             dx1_ref, dx1b_ref, h2_ref, act_ref, dgu_ref, dx2b_ref, acc_ref,
             g_s, u_s, dx2_s, accs):
        i = pl.program_id(0)

        @pl.when(i == 0)
        def _():
            accs[...] = jnp.zeros(accs.shape, F32)

        gffn = gffn_ref[...]
        gfin = gfin_ref[...]

        x1 = x1_ref[...]
        r2 = lax.rsqrt(jnp.mean(x1 * x1, axis=-1, keepdims=True) + EPS)
        xh2 = x1 * r2
        h2 = (xh2 * gffn).astype(BF)
        h2_ref[...] = h2

        for k in range(nh):
            g = _dot(h2, wgu_ref[k])
            u = _dot(h2, wgu_ref[k + nh])
            g_s[k] = g
            u_s[k] = u
            act_ref[k] = ((g * _sigmoid(g)) * u).astype(BF)

        x2 = x1
        for k in range(nh):
            x2 = x2 + _dot(act_ref[k], wdn_ref[k])

        r3 = lax.rsqrt(jnp.mean(x2 * x2, axis=-1, keepdims=True) + EPS)
        xh = x2 * r3
        err = xh * gfin - tgt_ref[...]
        accs[2] += _fold(err * err)
        dy = err * (1.0 / D)
        accs[1] += _fold(dy * xh)
        dxh = dy * gfin
        dx2 = r3 * (dxh - xh * jnp.mean(dxh * xh, axis=-1, keepdims=True))
        dx2_s[...] = dx2
        dx2b = dx2.astype(BF)
        dx2b_ref[...] = dx2b

        for k in range(nh):
            da = _dot_nt(dx2b, wdn_ref[k])
            g = g_s[k]
            sg = _sigmoid(g)
            dgu_ref[k] = ((da * u_s[k]) * (sg * (1.0 + g * (1.0 - sg)))).astype(BF)
            dgu_ref[k + nh] = (da * (g * sg)).astype(BF)

        dh2 = _dot_nt(dgu_ref[0], wgu_ref[0])
        for k in range(1, 2 * nh):
            dh2 = dh2 + _dot_nt(dgu_ref[k], wgu_ref[k])

        accs[0] += _fold(dh2 * xh2)
        dxh = dh2 * gffn
        dx1 = dx2_s[...] + r2 * (dxh - xh2 * jnp.mean(dxh * xh2, axis=-1, keepdims=True))
        dx1_ref[...] = dx1
        dx1b_ref[...] = dx1.astype(BF)

        @pl.when(i == nt - 1)
        def _():
            acc_ref[...] = jnp.zeros((8, D), F32)
            for j in range(3):
                acc_ref[j:j + 1, :] = jnp.sum(accs[j], axis=0, keepdims=True)

    tile = lambda w: pl.BlockSpec((tm, w), lambda i: (i, 0))
    vec = _const_spec((1, D))
    return pl.pallas_call(
        body, name="ffn", grid=(nt,),
        in_specs=[tile(D), tile(D), vec, vec, _const_spec(wgu.shape), _const_spec(wdn.shape)],
        out_specs=[tile(D), tile(D), tile(D),
                   pl.BlockSpec((nh, tm, fb), lambda i: (0, i, 0)),
                   pl.BlockSpec((2 * nh, tm, fb), lambda i: (0, i, 0)),
                   tile(D), pl.BlockSpec((8, D), lambda i: (0, 0))],
        out_shape=[jax.ShapeDtypeStruct((s, D), F32), jax.ShapeDtypeStruct((s, D), BF),
                   jax.ShapeDtypeStruct((s, D), BF), jax.ShapeDtypeStruct((nh, s, fb), BF),
                   jax.ShapeDtypeStruct((2 * nh, s, fb), BF), jax.ShapeDtypeStruct((s, D), BF),
                   jax.ShapeDtypeStruct((8, D), F32)],
        scratch_shapes=[pltpu.VMEM((nh, tm, fb), F32), pltpu.VMEM((nh, tm, fb), F32), pltpu.VMEM((tm, D), F32),
                        pltpu.VMEM((3, RC, D), F32)],
        compiler_params=_params(("arbitrary",)),
    )(x1, tgt, gffn, gfin, wgu, wdn)


def _mixer_bwd_pre_call(dx1b, proj, pa, pb, lng, lnb, wtr, wtrt, bias, wpa, wpb, wo):
    s = dx1b.shape[0]
    tm = min(TM_MIX, s)
    nt = s // tm
    pw = proj.shape[1]

    def body(dx1b_ref, uv_ref, gg_ref, pa_ref, pb_ref, lng_ref, lnb_ref, wtr_ref, wtrt_ref, bias_ref,
             wpa_ref, wpb_ref, wo_ref,
             dproj_ref, dya_ref, dpa_ref, dpb_ref, vec_ref, dws_ref, dbs_ref,
             vn_s, mx_s, dmx_s, dvn_s, accs, dbs_s):
        i = pl.program_id(0)

        @pl.when(i == 0)
        def _():
            accs[...] = jnp.zeros(accs.shape, F32)
            dbs_s[...] = jnp.zeros(dbs_s.shape, F32)
            dws_ref[...] = jnp.zeros(dws_ref.shape, F32)

        dm = _dot_nt(dx1b_ref[...], wo_ref[...])
        sa = _sigmoid(gg_ref[:, 0:D])
        sb = _sigmoid(gg_ref[:, D:2 * D])
        dpa = dm * sa
        dpb = dm * sb
        dpab = dpa.astype(BF)
        dpbb = dpb.astype(BF)
        dpa_ref[...] = dpab
        dpb_ref[...] = dpbb
        dproj_ref[:, 4 * D:5 * D] = ((dpa * pa_ref[...]) * (1.0 - sa)).astype(BF)
        dproj_ref[:, 5 * D:6 * D] = ((dpb * pb_ref[...]) * (1.0 - sb)).astype(BF)
        dproj_ref[:, 0:2 * D] = jnp.zeros((tm, 2 * D), BF)

        dya_ref[...] = _dot_nt(dpab, wpa_ref[...])
        dyb = _dot_nt(dpbb, wpb_ref[...])

        lng = lng_ref[...]
        gv, dgelu_v = _gelu_grad(uv_ref[:, D:2 * D])
        dv = gv - jnp.mean(gv, axis=-1, keepdims=True)
        rstd = lax.rsqrt(jnp.mean(dv * dv, axis=-1, keepdims=True) + EPS)
        xh = dv * rstd
        vn_s[...] = (xh * lng + lnb_ref[...]).astype(BF)

        nc = tm // CHUNK
        for c in range(nc):
            rs = slice(c * CHUNK, (c + 1) * CHUNK)
            for g in range(NGRP):
                cs = slice(g * GW, (g + 1) * GW)
                mx_s[rs, cs] = _dot(wtr_ref[g], vn_s[rs, cs])

        gu, dgelu_u = _gelu_grad(uv_ref[:, 0:D])
        mixed = mx_s[...] + jnp.concatenate([bias_ref[...]] * nc, axis=0)
        dproj_ref[:, 2 * D:3 * D] = ((dyb * mixed) * dgelu_u).astype(BF)
        dmx = dyb * gu
        dmx_s[...] = dmx.astype(BF)
        dbs_s[...] += jnp.sum(dmx.reshape(nc, CHUNK, D), axis=0)

        for c in range(nc):
            rs = slice(c * CHUNK, (c + 1) * CHUNK)
            for g in range(NGRP):
                cs = slice(g * GW, (g + 1) * GW)
                dvn_s[rs, cs] = _dot(wtrt_ref[g], dmx_s[rs, cs])
                dws_ref[g] += _dot_nt(dmx_s[rs, cs], vn_s[rs, cs])

        dvn = dvn_s[...]
        accs[0] += _fold(dvn * xh)
        accs[1] += _fold(dvn)
        dxh = dvn * lng
        m1 = jnp.mean(dxh, axis=-1, keepdims=True)
        m2 = jnp.mean(dxh * xh, axis=-1, keepdims=True)
        dproj_ref[:, 3 * D:4 * D] = ((rstd * (dxh - m1 - xh * m2)) * dgelu_v).astype(BF)

        @pl.when(i == nt - 1)
        def _():
            vec_ref[...] = jnp.zeros((8, D), F32)
            for j in range(2):
                vec_ref[j:j + 1, :] = jnp.sum(accs[j], axis=0, keepdims=True)
            row = lax.broadcasted_iota(jnp.int32, (CHUNK, CHUNK), 0)
            col = lax.broadcasted_iota(jnp.int32, (CHUNK, CHUNK), 1)
            for g in range(NGRP):
                dws_ref[g] = jnp.where(row >= col, dws_ref[g], 0.0)
                gs = jnp.sum(dbs_s[:, g * GW:(g + 1) * GW], axis=1, keepdims=True)
                dbs_ref[:, g * GW:(g + 1) * GW] = jnp.broadcast_to(gs, (CHUNK, GW))

    tile = lambda w: pl.BlockSpec((tm, w), lambda i: (i, 0))
    vec = _const_spec((1, D))
    return pl.pallas_call(
        body, name="mixer_bwd_pre", grid=(nt,),
        in_specs=[tile(D), pl.BlockSpec((tm, 2 * D), lambda i: (i, 1)), pl.BlockSpec((tm, 2 * D), lambda i: (i, 2)),
                  tile(D), tile(D), vec, vec, _const_spec(wtr.shape), _const_spec(wtrt.shape),
                  _const_spec(bias.shape), _const_spec((D, D)), _const_spec((D, D)), _const_spec((D, D))],
        out_specs=[tile(pw), tile(D), tile(D), tile(D), pl.BlockSpec((8, D), lambda i: (0, 0)),
                   pl.BlockSpec((NGRP, CHUNK, CHUNK), lambda i: (0, 0, 0)),
                   pl.BlockSpec((CHUNK, D), lambda i: (0, 0))],
        out_shape=[jax.ShapeDtypeStruct((s, pw), BF), jax.ShapeDtypeStruct((s, D), F32),
                   jax.ShapeDtypeStruct((s, D), BF), jax.ShapeDtypeStruct((s, D), BF),
                   jax.ShapeDtypeStruct((8, D), F32), jax.ShapeDtypeStruct((NGRP, CHUNK, CHUNK), F32),
                   jax.ShapeDtypeStruct((CHUNK, D), F32)],
        scratch_shapes=[pltpu.VMEM((tm, D), BF), pltpu.VMEM((tm, D), F32), pltpu.VMEM((tm, D), BF),
                        pltpu.VMEM((tm, D), F32), pltpu.VMEM((2, RC, D), F32), pltpu.VMEM((CHUNK, D), F32)],
        compiler_params=_params(("arbitrary",)),
    )(dx1b, proj, proj, pa, pb, lng, lnb, wtr, wtrt, bias, wpa, wpb, wo)


def _mixer_bwd_seq_call(dproj, dya, proj, h, cw, cb, ba, bx, lam, wax):
    s = dya.shape[0]
    tm = min(TM_MIX, s)
    nt = s // tm
    tb = tm // 8

    def body(dproj_any, dya_ref, xg_ref, xh8_ref, h_ref, hh8_ref, cw_ref, cb_ref, ba_ref, bx_ref, lam_ref, wax_ref,
             dpa_ref, vec_ref, dwax_ref,
             xc_s, xcb_s, r_s, ig_s, a_s, lm_s, dxc_s, c_s, accs):
        del dproj_any
        i = pl.program_id(0)

        @pl.when(i == 0)
        def _():
            accs[...] = jnp.zeros(accs.shape, F32)
            dwax_ref[...] = jnp.zeros(dwax_ref.shape, F32)
            c_s[...] = jnp.zeros((8, D), F32)
            dxc_s[tm:tm + 8, :] = jnp.zeros((8, D), F32)

        first_tile = i == nt - 1
        prev8 = jnp.where(first_tile, 0.0, xh8_ref[...])
        hprev8 = jnp.where(first_tile, 0.0, hh8_ref[...])
        rx = xg_ref[:, 0:D]
        xc = _conv_tile(rx, prev8, cw_ref, cb_ref[...])
        xc_s[...] = xc
        xcb = xc.astype(BF)
        xcb_s[...] = xcb
        lamv = lam_ref[...]
        sp = _softplus(-lamv)
        ba = ba_ref[...]
        bx = bx_ref[...]
        for q in range(NQ):
            cs = slice(q * QW, (q + 1) * QW)
            r, ig, la = _gate_tile(xcb, wax_ref, ba, bx, sp, q)
            r_s[:, cs] = r
            ig_s[:, cs] = ig
            a_s[:, cs] = jnp.exp(la)
        hv = h_ref[...]
        g, dg = _gelu_grad(xg_ref[:, D:2 * D])
        dya = dya_ref[...]
        lm_s[...] = dya * g
        dpa_ref[:, D:2 * D] = ((dya * hv) * dg).astype(BF)

        c_s[0:1, :] = _scan_bwd(a_s, lm_s, c_s[0:1, :], tm)

        hprev = _shift_back(hprev8, h_ref[...], 1)
        for q in range(NQ):
            cs = slice(q * QW, (q + 1) * QW)
            r = r_s[:, cs]
            ig = ig_s[:, cs]
            a = a_s[:, cs]
            la = (-RG_C * r) * sp[:, cs]
            m = jnp.sqrt(_neg_expm1(2.0 * la))
            lm = lm_s[:, cs]
            xq = xc_s[:, cs]
            dixc = lm * m
            dla = (lm * hprev[:, cs]) * a - ((lm * (ig * xq)) * (a * a)) / m
            accs[3, :, cs] += _fold(dla * r)
            dza = (dla * (-RG_C * sp[:, cs])) * (r * (1.0 - r))
            dzx = (dixc * xq) * (ig * (1.0 - ig))
            accs[1, :, cs] += _fold(dza)
            accs[2, :, cs] += _fold(dzx)
            dz = jnp.concatenate([dza, dzx], axis=1).astype(BF)
            dxc_s[0:tm, cs] = dixc * ig + _dot_nt(dz, wax_ref[q])
            dwax_ref[q] += _dot_tn(xcb_s[:, cs], dz)

        cur = dxc_s[0:tm, :]
        nx = dxc_s[tm:tm + 8, :]
        drx = cw_ref[3:4, :] * cur
        for j in (1, 2, 3):
            drx = drx + cw_ref[3 - j:4 - j, :] * _shift_fwd(cur, nx, j)
        dpa_ref[:, 0:D] = drx.astype(BF)
        accs[0] += _fold(cur)
        rx = xg_ref[:, 0:D]
        accs[7] += _fold(cur * rx)
        for j in (1, 2, 3):
            accs[7 - j] += _fold(cur * _shift_back(prev8, rx, j))
        dxc_s[tm:tm + 8, :] = cur[0:8, :]

        @pl.when(i == nt - 1)
        def _():
            vec_ref[...] = jnp.zeros((8, D), F32)
            for j in range(8):
                vec_ref[j:j + 1, :] = jnp.sum(accs[j], axis=0, keepdims=True)
            vec_ref[3:4, :] = vec_ref[3:4, :] * (RG_C * _sigmoid(-lamv))

    rev = lambda w: pl.BlockSpec((tm, w), lambda i: (nt - 1 - i, 0))
    halo = pl.BlockSpec((8, D), lambda i: (jnp.maximum((nt - 1 - i) * tb - 1, 0), 0))
    vec = _const_spec((1, D))
    return pl.pallas_call(
        body, name="mixer_bwd_seq", grid=(nt,),
        in_specs=[pl.BlockSpec(memory_space=pl.ANY), rev(D), rev(2 * D), halo, rev(D), halo,
                  _const_spec((4, D)), vec, vec, vec, vec, _const_spec(wax.shape)],
        out_specs=[rev(2 * D), pl.BlockSpec((8, D), lambda i: (0, 0)),
                   pl.BlockSpec((NQ, QW, 2 * QW), lambda i: (0, 0, 0))],
        out_shape=[jax.ShapeDtypeStruct(dproj.shape, BF), jax.ShapeDtypeStruct((8, D), F32),
                   jax.ShapeDtypeStruct((NQ, QW, 2 * QW), F32)],
        input_output_aliases={0: 0},
        scratch_shapes=[pltpu.VMEM((tm, D), F32), pltpu.VMEM((tm, D), BF), pltpu.VMEM((tm, D), F32),
                        pltpu.VMEM((tm, D), F32), pltpu.VMEM((tm, D), F32), pltpu.VMEM((tm, D), F32),
                        pltpu.VMEM((tm + 8, D), F32), pltpu.VMEM((8, D), F32), pltpu.VMEM((8, RC, D), F32)],
        compiler_params=_params(("arbitrary",)),
    )(dproj, dya, proj, proj, h, h, cw, cb, ba, bx, lam, wax)


def _dx_call(dproj, dx1, x, gmix, w_in_all):
    s = x.shape[0]
    tm = min(TM_DX, s)
    nt = s // tm
    nb, _, wb = w_in_all.shape

    def body(dp_ref, dx1_ref, x_ref, g_ref, w_ref, dx_ref, acc_ref, accs):
        i = pl.program_id(0)

        @pl.when(i == 0)
        def _():
            accs[...] = jnp.zeros(accs.shape, F32)

        xx = x_ref[...]
        r = lax.rsqrt(jnp.mean(xx * xx, axis=-1, keepdims=True) + EPS)
        xh = xx * r
        dh = _dot_nt(dp_ref[:, 0:wb], w_ref[0])
        for k in range(1, nb):
            dh = dh + _dot_nt(dp_ref[:, k * wb:(k + 1) * wb], w_ref[k])
        accs[...] += _fold(dh * xh)
        dxh = dh * g_ref[...]
        dx_ref[...] = dx1_ref[...] + r * (dxh - xh * jnp.mean(dxh * xh, axis=-1, keepdims=True))

        @pl.when(i == nt - 1)
        def _():
            acc_ref[...] = jnp.zeros((8, D), F32)
            acc_ref[0:1, :] = jnp.sum(accs[...], axis=0, keepdims=True)

    tile = lambda w: pl.BlockSpec((tm, w), lambda i: (i, 0))
    return pl.pallas_call(
        body, name="dx", grid=(nt,),
        in_specs=[tile(nb * wb), tile(D), tile(D), _const_spec((1, D)), _const_spec(w_in_all.shape)],
        out_specs=[tile(D), pl.BlockSpec((8, D), lambda i: (0, 0))],
        out_shape=[jax.ShapeDtypeStruct((s, D), F32), jax.ShapeDtypeStruct((8, D), F32)],
        scratch_shapes=[pltpu.VMEM((RC, D), F32)],
        compiler_params=_params(("arbitrary",)),
    )(dproj, dx1, x, gmix, w_in_all)


def _device_of(d):
    return (d // 4, lax.rem(d // 2, 2), lax.rem(d, 2))


def _dw_exchange_call(name, me, a, b, a_spec, b_spec, split, k1, n1, s):
    nb = NDEV // split
    r = k1 // split
    ts = min(TS_DW, s)
    ns = s // ts

    def slab(i, me_ref):
        return lax.rem(me_ref[0] // split + 1 + i, nb)

    def body(me_ref, a_ref, b_ref, own_ref, recv_ref, acc, sbuf, send_sems, recv_sems, local_sem):
        i = pl.program_id(0)
        j = pl.program_id(1)
        me = _lin(_me())
        p = _dot_tn(a_ref[...], b_ref[...])

        @pl.when(j == 0)
        def _():
            acc[...] = p

        @pl.when(j > 0)
        def _():
            acc[...] += p

        def send(step, h):
            d = slab(step, me_ref) * split + h
            cp = pltpu.make_async_remote_copy(
                src_ref=sbuf.at[lax.rem(step, 2), pl.ds(h * r, r)], dst_ref=recv_ref.at[me],
                send_sem=send_sems.at[d], recv_sem=recv_sems.at[me],
                device_id=_device_of(d), device_id_type=MESH)
            return cp, d != me

        def drain(step):
            for h in range(split):
                cp, sent = send(step, h)
                pl.when(sent)(cp.wait_send)

        @pl.when(j == ns - 1)
        def _():
            pl.when(i >= 2)(lambda: drain(i - 2))
            sbuf[lax.rem(i, 2)] = acc[...].astype(BF)
            for h in range(split):
                cp, sent = send(i, h)
                pl.when(sent)(cp.start)

            @pl.when(i == nb - 1)
            def _():
                off = pl.multiple_of(lax.rem(me, split) * r, RC)
                own_ref[...] = acc[pl.ds(off, r), :]
                mine = pltpu.make_async_copy(sbuf.at[lax.rem(i, 2), pl.ds(off, r)], recv_ref.at[me], local_sem)
                mine.start()
                drain(i - 1)
                drain(i)
                for d in range(NDEV):
                    wait = pltpu.make_async_remote_copy(
                        src_ref=sbuf.at[0, pl.ds(0, r)], dst_ref=recv_ref.at[d],
                        send_sem=send_sems.at[d], recv_sem=recv_sems.at[d],
                        device_id=_device_of(d), device_id_type=MESH).wait_recv
                    pl.when(d != me)(wait)
                mine.wait()

    grid_spec = pltpu.PrefetchScalarGridSpec(
        num_scalar_prefetch=1, grid=(nb, ns),
        in_specs=[a_spec(ts, slab), b_spec(ts, slab)],
        out_specs=[pl.BlockSpec((r, n1), lambda i, j, me_ref: (0, 0)), pl.BlockSpec(memory_space=pl.ANY)],
        scratch_shapes=[pltpu.VMEM((k1, n1), F32), pltpu.VMEM((2, k1, n1), BF),
                        pltpu.SemaphoreType.DMA((NDEV,)), pltpu.SemaphoreType.DMA((NDEV,)),
                        pltpu.SemaphoreType.DMA(())])
    return pl.pallas_call(
        body, name=name, grid_spec=grid_spec,
        out_shape=[jax.ShapeDtypeStruct((r, n1), F32), jax.ShapeDtypeStruct((NDEV, r, n1), BF)],
        compiler_params=_params(("arbitrary", "arbitrary")),
    )(me, a, b)


def _rows2d(w):
    return lambda ts, slab: pl.BlockSpec((ts, w), lambda i, j, me_ref: (j, 0))


def _cols2d(w):
    return lambda ts, slab: pl.BlockSpec((ts, w), lambda i, j, me_ref: (j, slab(i, me_ref)))


def _blk3d(w):
    return lambda ts, slab: pl.BlockSpec((None, ts, w), lambda i, j, me_ref: (slab(i, me_ref), j, 0))


_BC1 = 1.0 - ADAM_B1 ** ADAM_STEP
_BC2 = 1.0 - ADAM_B2 ** ADAM_STEP


def _adamw_math(w, g, m, v):
    m = ADAM_B1 * m + (1.0 - ADAM_B1) * g
    v = ADAM_B2 * v + (1.0 - ADAM_B2) * (g * g)
    m_hat = m / _BC1
    v_hat = v / _BC2
    delta = -ADAM_LR * (m_hat / (jnp.sqrt(v_hat) + ADAM_EPS) + ADAM_WD * w)
    return delta, m, v


def _row_tile(r):
    for t in (256, 176, 128, 64, 32, 16, 8):
        if r % t == 0:
            return t
    return r


def _reduce_adamw_call(name, me, own, recv, w, m, v):
    r, c = own.shape
    tr = _row_tile(r)

    def body(me_ref, own_ref, recv_ref, w_ref, m_ref, v_ref, g_ref, d_ref, nm_ref, nv_ref):
        mine = me_ref[0]
        g = jnp.zeros((tr, c), F32)
        for sdev in range(NDEV):
            g = g + jnp.where(mine == sdev, own_ref[...], recv_ref[sdev].astype(F32))
        g_ref[...] = g
        d_ref[...], nm_ref[...], nv_ref[...] = _adamw_math(w_ref[...], g, m_ref[...], v_ref[...])

    tile = pl.BlockSpec((tr, c), lambda i, me_ref: (i, 0))
    grid_spec = pltpu.PrefetchScalarGridSpec(
        num_scalar_prefetch=1, grid=(r // tr,),
        in_specs=[tile, pl.BlockSpec((NDEV, tr, c), lambda i, me_ref: (0, i, 0)), tile, tile, tile],
        out_specs=[tile] * 4)
    return pl.pallas_call(
        body, name=name, grid_spec=grid_spec,
        out_shape=[jax.ShapeDtypeStruct((r, c), F32)] * 4,
        compiler_params=_params(("parallel",)),
    )(me, own, recv, w, m, v)


def _adamw_call(name, w, g, m, v):
    r, c = w.shape
    tr = _row_tile(r)

    def body(w_ref, g_ref, m_ref, v_ref, d_ref, nm_ref, nv_ref):
        d_ref[...], nm_ref[...], nv_ref[...] = _adamw_math(w_ref[...], g_ref[...], m_ref[...], v_ref[...])

    tile = pl.BlockSpec((tr, c), lambda i: (i, 0))
    return pl.pallas_call(
        body, name=name, grid=(r // tr,), in_specs=[tile] * 4, out_specs=[tile] * 3,
        out_shape=[jax.ShapeDtypeStruct((r, c), F32)] * 3,
        compiler_params=_params(("parallel",)),
    )(w, g, m, v)


def _me():
    return lax.axis_index("x"), lax.axis_index("y"), lax.axis_index("c")


def _flip(pos, r):
    x, y, c = pos
    return (1 - x if r & 4 else x, 1 - y if r & 2 else y, 1 - c if r & 1 else c)


def _lin(pos):
    return pos[0] * 4 + pos[1] * 2 + pos[2]


def _small_allreduce_call(g):
    def body(g_ref, out_ref, rbuf, send1, recv1, send2, recv2):
        me = _me()
        mi = _lin(me)

        def rows(d):
            return pl.ds(pl.multiple_of(d * SMALL_PER, 8), SMALL_PER)

        sent = []
        for r in range(1, NDEV):
            peer = _flip(me, r)
            cp = pltpu.make_async_remote_copy(
                src_ref=g_ref.at[rows(_lin(peer))], dst_ref=rbuf.at[mi],
                send_sem=send1.at[r - 1], recv_sem=recv1.at[r - 1], device_id=peer, device_id_type=MESH)
            cp.start()
            sent.append(cp)
        rbuf[mi] = g_ref[rows(mi), :]
        for r in range(1, NDEV):
            peer = _flip(me, r)
            pltpu.make_async_remote_copy(
                src_ref=g_ref.at[rows(mi)], dst_ref=rbuf.at[_lin(peer)],
                send_sem=send1.at[r - 1], recv_sem=recv1.at[r - 1], device_id=peer, device_id_type=MESH).wait_recv()
        for cp in sent:
            cp.wait_send()
        tot = rbuf[0]
        for d in range(1, NDEV):
            tot = tot + rbuf[d]
        out_ref[rows(mi), :] = tot
        sent = []
        for r in range(1, NDEV):
            peer = _flip(me, r)
            cp = pltpu.make_async_remote_copy(
                src_ref=out_ref.at[rows(mi)], dst_ref=out_ref.at[rows(mi)],
                send_sem=send2.at[r - 1], recv_sem=recv2.at[r - 1], device_id=peer, device_id_type=MESH)
            cp.start()
            sent.append(cp)
        for r in range(1, NDEV):
            peer = _flip(me, r)
            pltpu.make_async_remote_copy(
                src_ref=out_ref.at[rows(mi)], dst_ref=out_ref.at[rows(_lin(peer))],
                send_sem=send2.at[r - 1], recv_sem=recv2.at[r - 1], device_id=peer, device_id_type=MESH).wait_recv()
        for cp in sent:
            cp.wait_send()

    vm = pl.BlockSpec(memory_space=pltpu.VMEM)
    return pl.pallas_call(
        body, name="small_allreduce", in_specs=[vm], out_specs=vm,
        out_shape=jax.ShapeDtypeStruct((SMALL_ROWS, D), F32),
        scratch_shapes=[pltpu.VMEM((NDEV, SMALL_PER, D), F32)] + [pltpu.SemaphoreType.DMA((NDEV - 1,))] * 4,
    )(g)


def _head_blocks(w):
    z = jnp.zeros((64, 64), w.dtype)
    groups = []
    for q in range(NQ):
        rows = [jnp.concatenate([w[4 * q + a] if a == b else z for b in range(4)], axis=1) for a in range(4)]
        groups.append(jnp.concatenate(rows, axis=0))
    return jnp.stack(groups)


def _head_unblocks(g):
    return jnp.stack([g[q, 64 * a:64 * a + 64, 64 * a:64 * a + 64] for q in range(NQ) for a in range(4)])


def _local_step(x, tgt, p, me, order):
    s = x.shape[0]
    vec = lambda a: a.reshape(1, D)
    gmix, gffn, gfin = vec(p["norm_mix_g"]), vec(p["norm_ffn_g"]), vec(p["norm_final_g"])
    cb, ba, bx, lam = vec(p["conv_b"]), vec(p["rg_ba"]), vec(p["rg_bx"]), vec(p["rg_lambda"])
    lng, lnb = vec(p["sgu_ln_g"]), vec(p["sgu_ln_b"])
    wax = jnp.concatenate([_head_blocks(p["rg_wa"]), _head_blocks(p["rg_wx"])], axis=2).astype(BF)
    tril = jnp.tril(jnp.ones((CHUNK, CHUNK), bool))
    ws = jnp.where(tril[None], p["sgu_ws"], 0.0)
    wtr = ws.astype(BF)
    wtrt = jnp.swapaxes(ws, 1, 2).astype(BF)
    bias = jnp.repeat(p["sgu_bs"].T, GW, axis=1)
    shard = {k: p[k].astype(BF) for k in _BIG}

    proj, h1, w_in, (wpa, wpb, wo, cw) = _proj_gather_call(
        x, gmix, shard["w_in"], order, [shard["w_proj_a"], shard["w_proj_b"], shard["w_out"], p["conv_w"]])
    wpa, wpb, wo = (t.reshape(D, D) for t in (wpa, wpb, wo))
    cw = jnp.swapaxes(cw, 0, 1).reshape(4, D)
    (h, pa, pb, x1, ya, yb, mg), (wgu, wdn) = _mixer_fwd_call(
        proj, x, cw, cb, ba, bx, lam, lng, lnb, wax, wtr, bias, wpa, wpb, wo, [shard["w_gate_up"], shard["w_down"]])
    wdn = wdn.reshape(NDEV // 2, -1, D)
    nb, _, wb = w_in.shape
    fb = wgu.shape[2]
    nh = wdn.shape[0]
    dx1, dx1b, h2, act, dgu, dx2b, facc = _ffn_call(x1, tgt, gffn, gfin, wgu, wdn)
    dproj, dya, dpa, dpb, bvec, dws, dbs = _mixer_bwd_pre_call(dx1b, proj, pa, pb, lng, lnb, wtr, wtrt, bias,
                                                             wpa, wpb, wo)
    dproj, svec, dwax = _mixer_bwd_seq_call(dproj, dya, proj, h, cw, cb, ba, bx, lam, wax)
    dx, xacc = _dx_call(dproj, dx1, x, gmix, w_in)

    assert nb == NDEV and 2 * nh == NDEV
    dw = {
        "w_gate_up": _dw_exchange_call("dw_gate_up", me, h2, dgu, _rows2d(D), _blk3d(fb), 1, D, fb, s),
        "w_down": _dw_exchange_call("dw_down", me, act, dx2b, _blk3d(fb), _rows2d(D), 2, fb, D, s),
        "w_proj_a": _dw_exchange_call("dw_proj_a", me, ya, dpa, _cols2d(QW), _rows2d(D), 2, QW, D, s),
        "w_proj_b": _dw_exchange_call("dw_proj_b", me, yb, dpb, _cols2d(QW), _rows2d(D), 2, QW, D, s),
        "w_out": _dw_exchange_call("dw_out", me, mg, dx1b, _cols2d(QW), _rows2d(D), 2, QW, D, s),
        "w_in": _dw_exchange_call("dw_in", me, h1, dproj, _rows2d(D), _cols2d(wb), 1, D, wb, s),
    }
    small = {
        "norm_mix_g": xacc[0], "norm_ffn_g": facc[0], "norm_final_g": facc[1],
        "conv_b": svec[0], "rg_ba": svec[1], "rg_bx": svec[2], "rg_lambda": svec[3], "conv_w": svec[4:8],
        "sgu_ln_g": bvec[0], "sgu_ln_b": bvec[1],
        "rg_wa": _head_unblocks(dwax[:, :, 0:QW]), "rg_wx": _head_unblocks(dwax[:, :, QW:2 * QW]),
        "sgu_ws": dws, "sgu_bs": dbs[:, ::GW].T,
    }
    loss_sum = (0.5 / D) * jnp.sum(facc[2])
    return loss_sum, dx, dw, small


_BIG = ("w_in", "w_gate_up", "w_down", "w_proj_a", "w_proj_b", "w_out")
_VEC_ROWS = ("norm_mix_g", "norm_ffn_g", "norm_final_g", "conv_b", "rg_ba", "rg_bx", "rg_lambda",
             "sgu_ln_g", "sgu_ln_b", "sgu_bs")
_WEIGHTS = ("norm_mix_g", "w_in", "conv_w", "conv_b", "rg_wa", "rg_ba", "rg_wx", "rg_bx", "rg_lambda",
            "sgu_ln_g", "sgu_ln_b", "sgu_ws", "sgu_bs", "w_proj_a", "w_proj_b", "w_out", "norm_ffn_g",
            "w_gate_up", "w_down", "norm_final_g")


def _pack_small(t, conv_w):
    head = jnp.concatenate([t[k].reshape(1, D) for k in _VEC_ROWS] + [conv_w, jnp.zeros((2, D), F32)], axis=0)
    return jnp.concatenate([head, t["rg_wa"].reshape(64, D), t["rg_wx"].reshape(64, D), t["sgu_ws"].reshape(128, D),
                            jnp.zeros((SMALL_ROWS - 272, D), F32)], axis=0)


def _unpack_small(a):
    out = {k: a[j] for j, k in enumerate(_VEC_ROWS)}
    out["conv_w"] = a[10:14]
    out["rg_wa"] = a[16:80].reshape(16, 64, 64)
    out["rg_wx"] = a[80:144].reshape(16, 64, 64)
    out["sgu_ws"] = a[144:272].reshape(NGRP, CHUNK, CHUNK)
    return out


def kernel(x, norm_mix_g, w_in, conv_w, conv_b, rg_wa, rg_ba, rg_wx, rg_bx, rg_lambda, sgu_ln_g, sgu_ln_b, sgu_ws, sgu_bs, w_proj_a, w_proj_b, w_out, norm_ffn_g, w_gate_up, w_down, norm_final_g, loss_target, m_norm_mix_g, m_w_in, m_conv_w, m_conv_b, m_rg_wa, m_rg_ba, m_rg_wx, m_rg_bx, m_rg_lambda, m_sgu_ln_g, m_sgu_ln_b, m_sgu_ws, m_sgu_bs, m_w_proj_a, m_w_proj_b, m_w_out, m_norm_ffn_g, m_w_gate_up, m_w_down, m_norm_final_g, v_norm_mix_g, v_w_in, v_conv_w, v_conv_b, v_rg_wa, v_rg_ba, v_rg_wx, v_rg_bx, v_rg_lambda, v_sgu_ln_g, v_sgu_ln_b, v_sgu_ws, v_sgu_bs, v_w_proj_a, v_w_proj_b, v_w_out, v_norm_ffn_g, v_w_gate_up, v_w_down, v_norm_final_g):
    args = dict(locals())
    w = {k: args[k] for k in _WEIGHTS}
    m = {k: args["m_" + k] for k in _WEIGHTS}
    v = {k: args["v_" + k] for k in _WEIGHTS}
    for d in (w, m, v):
        for k in _WEIGHTS:
            if k != "norm_final_g":
                d[k] = d[k][0]
    me = _lin(_me())
    me1 = me.reshape(1).astype(jnp.int32)

    order = jnp.bitwise_xor(me, jnp.array(_PASS_FLIPS, jnp.int32)).astype(jnp.int32)

    loss_sum, dx, dw, small = _local_step(x[0], loss_target[0], w, me1, order)
    loss = lax.psum(loss_sum, ("x", "y", "c"))

    grads, delta, new_m, new_v = {}, {}, {}, {}
    for k in _BIG:
        own, recv = dw[k]
        grads[k], delta[k], new_m[k], new_v[k] = _reduce_adamw_call("adamw_" + k, me1, own, recv, w[k], m[k], v[k])

    gsum = _small_allreduce_call(_pack_small(small, small["conv_w"]))
    zc = jnp.zeros((4, D), F32)
    d_s, m_s, v_s = _adamw_call("adamw_small", _pack_small(w, zc), gsum, _pack_small(m, zc), _pack_small(v, zc))
    gs, ds, ms, vs = _unpack_small(gsum), _unpack_small(d_s), _unpack_small(m_s), _unpack_small(v_s)
    g_cw = lax.dynamic_slice(gs["conv_w"], (0, me * 128), (4, 128))
    ds["conv_w"], ms["conv_w"], vs["conv_w"] = _adamw_call("adamw_conv_w", w["conv_w"], g_cw, m["conv_w"], v["conv_w"])
    gs["conv_w"] = g_cw
    for k in _WEIGHTS:
        if k not in _BIG:
            shp = w[k].shape
            grads[k], delta[k], new_m[k], new_v[k] = (t[k].reshape(shp) for t in (gs, ds, ms, vs))

    def lift(t, k):
        return t[k] if k == "norm_final_g" else t[k][None]

    outs = [loss, dx[None]]
    for t in (grads, delta, new_m, new_v):
        outs += [lift(t, k) for k in _WEIGHTS]
    return tuple(outs)
```

```python
import functools

import jax
import jax.numpy as jnp
from jax import lax
from jax.experimental import pallas as pl
from jax.experimental.pallas import tpu as pltpu

F32 = jnp.float32
BF = jnp.bfloat16

D = 1024
NDEV = 8
EPS = 1e-6
RG_C = 8.0
CHUNK = 128
NGRP = 8
GW = 128
NQ = 4
QW = 256
RC = 16
SMALL_ROWS = 320
SMALL_PER = SMALL_ROWS // NDEV

ADAM_LR = 0.001
ADAM_B1 = 0.9
ADAM_B2 = 0.999
ADAM_EPS = 1e-08
ADAM_WD = 0.01
ADAM_STEP = 10

VMEM_LIMIT = 60 * 1024 * 1024

MESH = pl.DeviceIdType.MESH


def _rows(n, fn, unroll=2, rc=RC):
    def body(i, c):
        fn(pl.multiple_of(i * rc, rc))
        return c
    lax.fori_loop(0, n // rc, body, 0, unroll=unroll)


def _fold(v):
    return jnp.sum(v.reshape(v.shape[0] // RC, RC, v.shape[1]), axis=0)


def _dot(a, b):
    return jnp.dot(a, b, preferred_element_type=F32)


def _dot_nt(a, b):
    return lax.dot_general(a, b, (((1,), (1,)), ((), ())), preferred_element_type=F32)


def _dot_tn(a, b):
    return lax.dot_general(a, b, (((0,), (0,)), ((), ())), preferred_element_type=F32)


_GC = 0.7978845608028654
_GK = 0.044715


def _gelu(x):
    t = jnp.tanh(_GC * (x + _GK * (x * x * x)))
    return x * (0.5 * (1.0 + t))


def _gelu_grad(x):
    x2 = x * x
    t = jnp.tanh(_GC * (x + _GK * (x2 * x)))
    cdf = 0.5 * (1.0 + t)
    dg = cdf + (0.5 * x) * (1.0 - t * t) * (_GC * (1.0 + (3.0 * _GK) * x2))
    return x * cdf, dg


def _sigmoid(x):
    return jax.nn.sigmoid(x)


def _log1p(e):
    u = 1.0 + e
    d = u - 1.0
    return jnp.where(d == 0.0, e, jnp.log(u) * (e / jnp.where(d == 0.0, 1.0, d)))


def _softplus(z):
    return jnp.maximum(z, 0.0) + _log1p(jnp.exp(-jnp.abs(z)))


def _neg_expm1(z):
    u = jnp.exp(z)
    lu = jnp.log(u)
    k = (1.0 - u) * (z / jnp.where(lu == 0.0, 1.0, lu))
    small = jnp.where(lu == 0.0, -z, k)
    return jnp.where(z > -0.5, small, 1.0 - u)


def _shift_back(prev8, cur, j):
    cat = jnp.concatenate([prev8, cur], axis=0)
    return pltpu.roll(cat, j, 0)[8:8 + cur.shape[0]]


def _shift_fwd(cur, next8, j):
    cat = jnp.concatenate([cur, next8], axis=0)
    n = cat.shape[0]
    return pltpu.roll(cat, n - j, 0)[0:cur.shape[0]]


def _const_spec(shape):
    nd = len(shape)
    return pl.BlockSpec(shape, lambda *_: (0,) * nd, pipeline_mode=pl.Buffered(1))


def _params(sem):
    return pltpu.CompilerParams(dimension_semantics=sem, vmem_limit_bytes=VMEM_LIMIT)


TM_PROJ = 1024
TM_MIX = 256
TM_FFN = 256
TM_DX = 512
TS_DW = 1024


_CHIPS = (4, 2, 6)
_PASS_FLIPS = (0, 1, 4, 2, 6, 5, 3, 7)


class _Gather:
    def __init__(self, ins, outs, send_sems, recv_sems, local_sems):
        self.ins, self.outs = ins, outs
        self.send_sems, self.recv_sems, self.local_sems = send_sems, recv_sems, local_sems
        self.me = _me()
        self.sibling = _flip(self.me, 1)

    def _copy(self, a, kind, block, to, src=None):
        dst = self.outs[a].at[_lin(block)]
        return pltpu.make_async_remote_copy(
            src_ref=dst if src is None else src, dst_ref=dst,
            send_sem=self.send_sems.at[a, kind], recv_sem=self.recv_sems.at[a, kind],
            device_id=to, device_id_type=MESH)

    def _local(self, a):
        return pltpu.make_async_copy(self.ins[a], self.outs[a].at[_lin(self.me)], self.local_sems.at[a])

    def start(self):
        for a in range(len(self.ins)):
            self._local(a).start()
            self._copy(a, 0, self.me, self.sibling, src=self.ins[a]).start()
            for j, f in enumerate(_CHIPS):
                self._copy(a, 1 + j, self.me, _flip(self.me, f), src=self.ins[a]).start()

    def forward(self):
        for j, f in enumerate(_CHIPS):
            for a in range(len(self.ins)):
                self._copy(a, 1 + j, _flip(self.me, f), self.me).wait_recv()
                self._copy(a, 4 + j, _flip(self.me, f), self.sibling).start()

    def finish(self):
        for a in range(len(self.ins)):
            self._copy(a, 0, self.sibling, self.me).wait_recv()
            for j, f in enumerate(_CHIPS):
                self._copy(a, 4 + j, _flip(self.me, f | 1), self.me).wait_recv()
            self._copy(a, 0, self.me, self.sibling, src=self.ins[a]).wait_send()
            for j, f in enumerate(_CHIPS):
                self._copy(a, 1 + j, self.me, _flip(self.me, f), src=self.ins[a]).wait_send()
                self._copy(a, 4 + j, _flip(self.me, f), self.sibling).wait_send()
            self._local(a).wait()


class _Exchange:
    def __init__(self, ins, outs, send_sems, recv_sems, local_sems):
        self.ins, self.outs = ins, outs
        self.send_sems, self.recv_sems, self.local_sems = send_sems, recv_sems, local_sems
        self.me = _me()

    def _copy(self, a, r, outgoing):
        peer = _flip(self.me, r)
        src, dst = (peer, self.me) if outgoing else (self.me, peer)
        return pltpu.make_async_remote_copy(
            src_ref=self.ins[a].at[_lin(src)], dst_ref=self.outs[a].at[_lin(dst)],
            send_sem=self.send_sems.at[a, r - 1], recv_sem=self.recv_sems.at[a, r - 1],
            device_id=peer, device_id_type=MESH)

    def _local(self, a):
        mi = _lin(self.me)
        return pltpu.make_async_copy(self.ins[a].at[mi], self.outs[a].at[mi], self.local_sems.at[a])

    def start(self):
        for a in range(len(self.ins)):
            self._local(a).start()
        for r in range(1, NDEV):
            for a in range(len(self.ins)):
                self._copy(a, r, True).start()

    def finish(self):
        for r in range(1, NDEV):
            for a in range(len(self.ins)):
                self._copy(a, r, False).wait_recv()
        for r in range(1, NDEV):
            for a in range(len(self.ins)):
                self._copy(a, r, True).wait_send()
        for a in range(len(self.ins)):
            self._local(a).wait()


def _gather_sems(n):
    return [pltpu.SemaphoreType.DMA((n, 7)), pltpu.SemaphoreType.DMA((n, 7)), pltpu.SemaphoreType.DMA((n,))]


def _proj_gather_call(x, gmix, w_shard, order, extras):
    s = x.shape[0]
    tm = min(TM_PROJ, s)
    nt = s // tm
    wb = w_shard.shape[1]
    n = len(extras)

    def body(order_ref, x_ref, g_ref, wsh_ref, *rest):
        ex_in = rest[:n]
        proj_ref, h1_hbm, wall_hbm = rest[n:n + 3]
        ex_out = rest[n + 3:2 * n + 3]
        wall, hc, ws_send, ws_recv, ex_send, ex_recv, ex_local, out_sems = rest[2 * n + 3:]
        k = pl.program_id(0)
        i = pl.program_id(1)
        me = _me()
        sibling = _flip(me, 1)
        gather = _Gather(ex_in, ex_out, ex_send, ex_recv, ex_local)

        def wcopy(kind, block, to):
            ref = wall.at[_lin(block)]
            return pltpu.make_async_remote_copy(
                src_ref=ref, dst_ref=ref, send_sem=ws_send.at[kind], recv_sem=ws_recv.at[kind],
                device_id=to, device_id_type=MESH)

        head = i == 0

        @pl.when(head & (k == 0))
        def _():
            wall[_lin(me)] = wsh_ref[...]
            wcopy(0, me, sibling).start()
            for j, f in enumerate(_CHIPS):
                wcopy(1 + j, me, _flip(me, f)).start()
            gather.start()

        @pl.when(head & (k == 1))
        def _():
            wcopy(0, sibling, me).wait_recv()

        for j, f in enumerate(_CHIPS):
            @pl.when(head & (k == 2 + j))
            def _(j=j, f=f):
                wcopy(1 + j, _flip(me, f), me).wait_recv()
                wcopy(4 + j, _flip(me, f), sibling).start()

            @pl.when(head & (k == 5 + j))
            def _(j=j, f=f):
                wcopy(4 + j, _flip(me, f | 1), me).wait_recv()

        pl.when(head & (k == 5))(gather.forward)

        base = pl.multiple_of(i * tm, tm)

        @pl.when(k == 0)
        def _():
            g = g_ref[...]

            def norm(r0):
                xx = x_ref[pl.ds(r0, RC), :]
                r = lax.rsqrt(jnp.mean(xx * xx, axis=-1, keepdims=True) + EPS)
                hc[pl.ds(base + r0, RC), :] = ((xx * r) * g).astype(BF)
            _rows(tm, norm, unroll=4)

        proj_ref[...] = _dot(hc[pl.ds(base, tm), :], wall[order_ref[k]])

        @pl.when((k == NDEV - 1) & (i == nt - 1))
        def _():
            out_w = pltpu.make_async_copy(wall, wall_hbm, out_sems.at[0])
            out_h = pltpu.make_async_copy(hc, h1_hbm, out_sems.at[1])
            out_w.start()
            out_h.start()
            wcopy(0, me, sibling).wait_send()
            for j, f in enumerate(_CHIPS):
                wcopy(1 + j, me, _flip(me, f)).wait_send()
                wcopy(4 + j, _flip(me, f), sibling).wait_send()
            gather.finish()
            out_w.wait()
            out_h.wait()

    any_spec = pl.BlockSpec(memory_space=pl.ANY)
    grid_spec = pltpu.PrefetchScalarGridSpec(
        num_scalar_prefetch=1, grid=(NDEV, nt),
        in_specs=[pl.BlockSpec((tm, D), lambda k, i, o: (jnp.where(k == 0, i, nt - 1), 0)),
                  pl.BlockSpec((1, D), lambda k, i, o: (0, 0)),
                  pl.BlockSpec(w_shard.shape, lambda k, i, o: (0, 0))] + [any_spec] * n,
        out_specs=[pl.BlockSpec((tm, wb), lambda k, i, o: (i, o[k])), any_spec, any_spec] + [any_spec] * n,
        scratch_shapes=[pltpu.VMEM((NDEV,) + w_shard.shape, BF), pltpu.VMEM((s, D), BF),
                        pltpu.SemaphoreType.DMA((7,)), pltpu.SemaphoreType.DMA((7,))] + _gather_sems(n)
        + [pltpu.SemaphoreType.DMA((2,))])
    outs = pl.pallas_call(
        body, name="proj_gather", grid_spec=grid_spec,
        out_shape=[jax.ShapeDtypeStruct((s, NDEV * wb), F32), jax.ShapeDtypeStruct((s, D), BF),
                   jax.ShapeDtypeStruct((NDEV,) + w_shard.shape, BF)]
        + [jax.ShapeDtypeStruct((NDEV,) + e.shape, e.dtype) for e in extras],
        compiler_params=_params(("arbitrary", "arbitrary")),
    )(order, x, gmix, w_shard, *extras)
    return outs[0], outs[1], outs[2], outs[3:]


def _conv_tile(rx, prev8, cw_ref, cb):
    xc = cb + cw_ref[3:4, :] * rx
    for j in (1, 2, 3):
        xc = xc + cw_ref[3 - j:4 - j, :] * _shift_back(prev8, rx, j)
    return xc


def _gate_tile(xcb, wax_ref, ba, bx, sp, q):
    cs = slice(q * QW, (q + 1) * QW)
    z = _dot(xcb[:, cs], wax_ref[q])
    r = _sigmoid(z[:, 0:QW] + ba[:, cs])
    ig = _sigmoid(z[:, QW:2 * QW] + bx[:, cs])
    return r, ig, (-RG_C * r) * sp[:, cs]


def _scan_fwd(a_s, b_s, out_ref, h0, n):
    rowi = lax.broadcasted_iota(jnp.int32, (8, D), 0)

    def block(t, h):
        rows = pl.ds(pl.multiple_of(t * 8, 8), 8)
        a = a_s[rows, :]
        b = b_s[rows, :]
        for d in (1, 2, 4):
            m = rowi >= d
            b = jnp.where(m, a * pltpu.roll(b, d, 0) + b, b)
            a = jnp.where(m, a * pltpu.roll(a, d, 0), a)
        hb = b + a * h
        out_ref[rows, :] = hb
        return hb[7:8, :]
    return lax.fori_loop(0, n // 8, block, h0, unroll=4)


def _scan_bwd(a_s, lm_s, c0, n):
    rowi = lax.broadcasted_iota(jnp.int32, (8, D), 0)
    nblk = n // 8

    def block(k, cin):
        rows = pl.ds(pl.multiple_of((nblk - 1 - k) * 8, 8), 8)
        a = a_s[rows, :]
        dh = lm_s[rows, :]
        b = a * dh
        for d in (1, 2, 4):
            m = rowi < 8 - d
            b = jnp.where(m, a * pltpu.roll(b, 8 - d, 0) + b, b)
            a = jnp.where(m, a * pltpu.roll(a, 8 - d, 0), a)
        mu = b + a * cin
        lm_s[rows, :] = dh + jnp.where(rowi < 7, pltpu.roll(mu, 7, 0), cin)
        return mu[0:1, :]
    return lax.fori_loop(0, nblk, block, c0, unroll=4)


def _mixer_fwd_call(proj, x, cw, cb, ba, bx, lam, lng, lnb, wax, wtr, bias, wpa, wpb, wo, shards):
    s = x.shape[0]
    tm = min(TM_MIX, s)
    nt = s // tm
    pw = proj.shape[1]
    n = len(shards)

    def body(proj_ref, x_ref, cw_ref, cb_ref, ba_ref, bx_ref, lam_ref, lng_ref, lnb_ref, wax_ref, wtr_ref,
             bias_ref, wpa_ref, wpb_ref, wo_ref, *rest):
        sh_in = rest[:n]
        h_ref, pa_ref, pb_ref, x1_ref, ya_ref, yb_ref, mg_ref = rest[n:n + 7]
        sh_out = rest[n + 7:2 * n + 7]
        prev_s, a_s, b_s, hc_s, vn_s, mx_s, g_send, g_recv, g_local = rest[2 * n + 7:]
        i = pl.program_id(0)
        gather = _Gather(sh_in, sh_out, g_send, g_recv, g_local)

        @pl.when(i == 0)
        def _():
            gather.start()
            prev_s[...] = jnp.zeros((8, D), F32)
            hc_s[...] = jnp.zeros((8, D), F32)

        pl.when(i == nt // 2)(gather.forward)

        rx = proj_ref[:, 0:D]
        xc = _conv_tile(rx, prev_s[...], cw_ref, cb_ref[...])
        prev_s[...] = rx[tm - 8:tm, :]
        xcb = xc.astype(BF)
        sp = _softplus(-lam_ref[...])
        ba = ba_ref[...]
        bx = bx_ref[...]
        for q in range(NQ):
            cs = slice(q * QW, (q + 1) * QW)
            _, ig, la = _gate_tile(xcb, wax_ref, ba, bx, sp, q)
            a_s[:, cs] = jnp.exp(la)
            b_s[:, cs] = jnp.sqrt(_neg_expm1(2.0 * la)) * (ig * xc[:, cs])

        gv = _gelu(proj_ref[:, 3 * D:4 * D])
        dv = gv - jnp.mean(gv, axis=-1, keepdims=True)
        var = jnp.mean(dv * dv, axis=-1, keepdims=True)
        vn_s[...] = ((dv * lax.rsqrt(var + EPS)) * lng_ref[...] + lnb_ref[...]).astype(BF)
        nc = tm // CHUNK
        for c in range(nc):
            rs = slice(c * CHUNK, (c + 1) * CHUNK)
            for g in range(NGRP):
                cs = slice(g * GW, (g + 1) * GW)
                mx_s[rs, cs] = _dot(wtr_ref[g], vn_s[rs, cs])
        mixed = mx_s[...] + jnp.concatenate([bias_ref[...]] * nc, axis=0)
        yb = (_gelu(proj_ref[:, 2 * D:3 * D]) * mixed).astype(BF)
        yb_ref[...] = yb
        pb_ref[...] = _dot(yb, wpb_ref[...])

        hc_s[0:1, :] = _scan_fwd(a_s, b_s, h_ref, hc_s[0:1, :], tm)

        ya = (_gelu(proj_ref[:, D:2 * D]) * h_ref[...]).astype(BF)
        ya_ref[...] = ya
        pa = _dot(ya, wpa_ref[...])
        pa_ref[...] = pa
        mg = (_sigmoid(proj_ref[:, 4 * D:5 * D]) * pa + _sigmoid(proj_ref[:, 5 * D:6 * D]) * pb_ref[...]).astype(BF)
        mg_ref[...] = mg
        x1_ref[...] = x_ref[...] + _dot(mg, wo_ref[...])

        pl.when(i == nt - 1)(gather.finish)

    tile = lambda w: pl.BlockSpec((tm, w), lambda i: (i, 0))
    vec = _const_spec((1, D))
    any_spec = pl.BlockSpec(memory_space=pl.ANY)
    outs = pl.pallas_call(
        body, name="mixer_fwd", grid=(nt,),
        in_specs=[tile(pw), tile(D), _const_spec((4, D)), vec, vec, vec, vec, vec, vec,
                  _const_spec(wax.shape), _const_spec(wtr.shape), _const_spec(bias.shape),
                  _const_spec((D, D)), _const_spec((D, D)), _const_spec((D, D))] + [any_spec] * n,
        out_specs=[tile(D)] * 7 + [any_spec] * n,
        out_shape=[jax.ShapeDtypeStruct((s, D), F32)] * 4 + [jax.ShapeDtypeStruct((s, D), BF)] * 3
        + [jax.ShapeDtypeStruct((NDEV,) + e.shape, e.dtype) for e in shards],
        scratch_shapes=[pltpu.VMEM((8, D), F32), pltpu.VMEM((tm, D), F32), pltpu.VMEM((tm, D), F32),
                        pltpu.VMEM((8, D), F32), pltpu.VMEM((tm, D), BF), pltpu.VMEM((tm, D), F32)]
        + _gather_sems(n),
        compiler_params=_params(("arbitrary",)),
    )(proj, x, cw, cb, ba, bx, lam, lng, lnb, wax, wtr, bias, wpa, wpb, wo, *shards)
    return outs[:7], outs[7:]


def _ffn_call(x1, tgt, gffn, gfin, wgu, wdn):
    s = x1.shape[0]
    tm = min(TM_FFN, s)
    nt = s // tm
    nh = wdn.shape[0]
    fb = wgu.shape[2]

    def body(x1_ref, tgt_ref, gffn_ref, gfin_ref, wgu_ref, wdn_ref,
             dx1_ref, dx1b_ref, h2_ref, act_ref, dgu_ref, dx2b_ref, acc_ref,
             g_s, u_s, dx2_s, accs):
        i = pl.program_id(0)

        @pl.when(i == 0)
        def _():
            accs[...] = jnp.zeros(accs.shape, F32)

        gffn = gffn_ref[...]
        gfin = gfin_ref[...]

        x1 = x1_ref[...]
        r2 = lax.rsqrt(jnp.mean(x1 * x1, axis=-1, keepdims=True) + EPS)
        xh2 = x1 * r2
        h2 = (xh2 * gffn).astype(BF)
        h2_ref[...] = h2

        for k in range(nh):
            g = _dot(h2, wgu_ref[k])
            u = _dot(h2, wgu_ref[k + nh])
            g_s[k] = g
            u_s[k] = u
            act_ref[k] = ((g * _sigmoid(g)) * u).astype(BF)

        x2 = x1
        for k in range(nh):
            x2 = x2 + _dot(act_ref[k], wdn_ref[k])

        r3 = lax.rsqrt(jnp.mean(x2 * x2, axis=-1, keepdims=True) + EPS)
        xh = x2 * r3
        err = xh * gfin - tgt_ref[...]
        accs[2] += _fold(err * err)
        dy = err * (1.0 / D)
        accs[1] += _fold(dy * xh)
        dxh = dy * gfin
        dx2 = r3 * (dxh - xh * jnp.mean(dxh * xh, axis=-1, keepdims=True))
        dx2_s[...] = dx2
        dx2b = dx2.astype(BF)
        dx2b_ref[...] = dx2b

        for k in range(nh):
            da = _dot_nt(dx2b, wdn_ref[k])
            g = g_s[k]
            sg = _sigmoid(g)
            dgu_ref[k] = ((da * u_s[k]) * (sg * (1.0 + g * (1.0 - sg)))).astype(BF)
            dgu_ref[k + nh] = (da * (g * sg)).astype(BF)

        dh2 = _dot_nt(dgu_ref[0], wgu_ref[0])
        for k in range(1, 2 * nh):
            dh2 = dh2 + _dot_nt(dgu_ref[k], wgu_ref[k])

        accs[0] += _fold(dh2 * xh2)
        dxh = dh2 * gffn
        dx1 = dx2_s[...] + r2 * (dxh - xh2 * jnp.mean(dxh * xh2, axis=-1, keepdims=True))
        dx1_ref[...] = dx1
        dx1b_ref[...] = dx1.astype(BF)

        @pl.when(i == nt - 1)
        def _():
            acc_ref[...] = jnp.zeros((8, D), F32)
            for j in range(3):
                acc_ref[j:j + 1, :] = jnp.sum(accs[j], axis=0, keepdims=True)

    tile = lambda w: pl.BlockSpec((tm, w), lambda i: (i, 0))
    vec = _const_spec((1, D))
    return pl.pallas_call(
        body, name="ffn", grid=(nt,),
        in_specs=[tile(D), tile(D), vec, vec, _const_spec(wgu.shape), _const_spec(wdn.shape)],
        out_specs=[tile(D), tile(D), tile(D),
                   pl.BlockSpec((nh, tm, fb), lambda i: (0, i, 0)),
                   pl.BlockSpec((2 * nh, tm, fb), lambda i: (0, i, 0)),
                   tile(D), pl.BlockSpec((8, D), lambda i: (0, 0))],
        out_shape=[jax.ShapeDtypeStruct((s, D), F32), jax.ShapeDtypeStruct((s, D), BF),
                   jax.ShapeDtypeStruct((s, D), BF), jax.ShapeDtypeStruct((nh, s, fb), BF),
                   jax.ShapeDtypeStruct((2 * nh, s, fb), BF), jax.ShapeDtypeStruct((s, D), BF),
                   jax.ShapeDtypeStruct((8, D), F32)],
        scratch_shapes=[pltpu.VMEM((nh, tm, fb), F32), pltpu.VMEM((nh, tm, fb), F32), pltpu.VMEM((tm, D), F32),
                        pltpu.VMEM((3, RC, D), F32)],
        compiler_params=_params(("arbitrary",)),
    )(x1, tgt, gffn, gfin, wgu, wdn)


def _mixer_bwd_pre_call(dx1b, proj, pa, pb, lng, lnb, wtr, wtrt, bias, wpa, wpb, wo, parts):
    s = dx1b.shape[0]
    tm = min(TM_MIX, s)
    nt = s // tm
    pw = proj.shape[1]
    n = len(parts)

    def body(dx1b_ref, uv_ref, gg_ref, pa_ref, pb_ref, lng_ref, lnb_ref, wtr_ref, wtrt_ref, bias_ref,
             wpa_ref, wpb_ref, wo_ref, *rest):
        ex_in = rest[:n]
        dproj_ref, dya_ref, dpa_ref, dpb_ref, vec_ref, dws_ref, dbs_ref = rest[n:n + 7]
        ex_out = rest[n + 7:2 * n + 7]
        vn_s, mx_s, dmx_s, dvn_s, accs, dbs_s, e_send, e_recv, e_local = rest[2 * n + 7:]
        i = pl.program_id(0)
        exchange = _Exchange(ex_in, ex_out, e_send, e_recv, e_local)

        @pl.when(i == 0)
        def _():
            exchange.start()
            accs[...] = jnp.zeros(accs.shape, F32)
            dbs_s[...] = jnp.zeros(dbs_s.shape, F32)
            dws_ref[...] = jnp.zeros(dws_ref.shape, F32)

        dm = _dot_nt(dx1b_ref[...], wo_ref[...])
        sa = _sigmoid(gg_ref[:, 0:D])
        sb = _sigmoid(gg_ref[:, D:2 * D])
        dpa = dm * sa
        dpb = dm * sb
        dpab = dpa.astype(BF)
        dpbb = dpb.astype(BF)
        dpa_ref[...] = dpab
        dpb_ref[...] = dpbb
        dproj_ref[:, 4 * D:5 * D] = ((dpa * pa_ref[...]) * (1.0 - sa)).astype(BF)
        dproj_ref[:, 5 * D:6 * D] = ((dpb * pb_ref[...]) * (1.0 - sb)).astype(BF)
        dproj_ref[:, 0:2 * D] = jnp.zeros((tm, 2 * D), BF)

        dya_ref[...] = _dot_nt(dpab, wpa_ref[...])
        dyb = _dot_nt(dpbb, wpb_ref[...])

        lng = lng_ref[...]
        gv, dgelu_v = _gelu_grad(uv_ref[:, D:2 * D])
        dv = gv - jnp.mean(gv, axis=-1, keepdims=True)
        rstd = lax.rsqrt(jnp.mean(dv * dv, axis=-1, keepdims=True) + EPS)
        xh = dv * rstd
        vn_s[...] = (xh * lng + lnb_ref[...]).astype(BF)

        nc = tm // CHUNK
        for c in range(nc):
            rs = slice(c * CHUNK, (c + 1) * CHUNK)
            for g in range(NGRP):
                cs = slice(g * GW, (g + 1) * GW)
                mx_s[rs, cs] = _dot(wtr_ref[g], vn_s[rs, cs])

        gu, dgelu_u = _gelu_grad(uv_ref[:, 0:D])
        mixed = mx_s[...] + jnp.concatenate([bias_ref[...]] * nc, axis=0)
        dproj_ref[:, 2 * D:3 * D] = ((dyb * mixed) * dgelu_u).astype(BF)
        dmx = dyb * gu
        dmx_s[...] = dmx.astype(BF)
        dbs_s[...] += jnp.sum(dmx.reshape(nc, CHUNK, D), axis=0)

        for c in range(nc):
            rs = slice(c * CHUNK, (c + 1) * CHUNK)
            for g in range(NGRP):
                cs = slice(g * GW, (g + 1) * GW)
                dvn_s[rs, cs] = _dot(wtrt_ref[g], dmx_s[rs, cs])
                dws_ref[g] += _dot_nt(dmx_s[rs, cs], vn_s[rs, cs])

        dvn = dvn_s[...]
        accs[0] += _fold(dvn * xh)
        accs[1] += _fold(dvn)
        dxh = dvn * lng
        m1 = jnp.mean(dxh, axis=-1, keepdims=True)
        m2 = jnp.mean(dxh * xh, axis=-1, keepdims=True)
        dproj_ref[:, 3 * D:4 * D] = ((rstd * (dxh - m1 - xh * m2)) * dgelu_v).astype(BF)

        @pl.when(i == nt - 1)
        def _():
            vec_ref[...] = jnp.zeros((8, D), F32)
            for j in range(2):
                vec_ref[j:j + 1, :] = jnp.sum(accs[j], axis=0, keepdims=True)
            row = lax.broadcasted_iota(jnp.int32, (CHUNK, CHUNK), 0)
            col = lax.broadcasted_iota(jnp.int32, (CHUNK, CHUNK), 1)
            for g in range(NGRP):
                dws_ref[g] = jnp.where(row >= col, dws_ref[g], 0.0)
                gs = jnp.sum(dbs_s[:, g * GW:(g + 1) * GW], axis=1, keepdims=True)
                dbs_ref[:, g * GW:(g + 1) * GW] = jnp.broadcast_to(gs, (CHUNK, GW))
            exchange.finish()

    tile = lambda w: pl.BlockSpec((tm, w), lambda i: (i, 0))
    vec = _const_spec((1, D))
    any_spec = pl.BlockSpec(memory_space=pl.ANY)
    outs = pl.pallas_call(
        body, name="mixer_bwd_pre", grid=(nt,),
        in_specs=[tile(D), pl.BlockSpec((tm, 2 * D), lambda i: (i, 1)), pl.BlockSpec((tm, 2 * D), lambda i: (i, 2)),
                  tile(D), tile(D), vec, vec, _const_spec(wtr.shape), _const_spec(wtrt.shape),
                  _const_spec(bias.shape), _const_spec((D, D)), _const_spec((D, D)), _const_spec((D, D))]
        + [any_spec] * n,
        out_specs=[tile(pw), tile(D), tile(D), tile(D), pl.BlockSpec((8, D), lambda i: (0, 0)),
                   pl.BlockSpec((NGRP, CHUNK, CHUNK), lambda i: (0, 0, 0)),
                   pl.BlockSpec((CHUNK, D), lambda i: (0, 0))] + [any_spec] * n,
        out_shape=[jax.ShapeDtypeStruct((s, pw), BF), jax.ShapeDtypeStruct((s, D), F32),
                   jax.ShapeDtypeStruct((s, D), BF), jax.ShapeDtypeStruct((s, D), BF),
                   jax.ShapeDtypeStruct((8, D), F32), jax.ShapeDtypeStruct((NGRP, CHUNK, CHUNK), F32),
                   jax.ShapeDtypeStruct((CHUNK, D), F32)]
        + [jax.ShapeDtypeStruct(p.shape, p.dtype) for p in parts],
        scratch_shapes=[pltpu.VMEM((tm, D), BF), pltpu.VMEM((tm, D), F32), pltpu.VMEM((tm, D), BF),
                        pltpu.VMEM((tm, D), F32), pltpu.VMEM((2, RC, D), F32), pltpu.VMEM((CHUNK, D), F32)]
        + _gather_sems(n),
        compiler_params=_params(("arbitrary",)),
    )(dx1b, proj, proj, pa, pb, lng, lnb, wtr, wtrt, bias, wpa, wpb, wo, *parts)
    return outs[:7], outs[7:]


def _mixer_bwd_seq_call(dproj, dya, proj, h, cw, cb, ba, bx, lam, wax, parts):
    s = dya.shape[0]
    tm = min(TM_MIX, s)
    nt = s // tm
    tb = tm // 8
    n = len(parts)

    def body(dproj_any, dya_ref, xg_ref, xh8_ref, h_ref, hh8_ref, cw_ref, cb_ref, ba_ref, bx_ref, lam_ref, wax_ref,
             *rest):
        del dproj_any
        ex_in = rest[:n]
        dpa_ref, vec_ref, dwax_ref = rest[n:n + 3]
        ex_out = rest[n + 3:2 * n + 3]
        xc_s, xcb_s, r_s, ig_s, a_s, lm_s, dxc_s, c_s, accs, e_send, e_recv, e_local = rest[2 * n + 3:]
        i = pl.program_id(0)
        exchange = _Exchange(ex_in, ex_out, e_send, e_recv, e_local)

        @pl.when(i == 0)
        def _():
            exchange.start()
            accs[...] = jnp.zeros(accs.shape, F32)
            dwax_ref[...] = jnp.zeros(dwax_ref.shape, F32)
            c_s[...] = jnp.zeros((8, D), F32)
            dxc_s[tm:tm + 8, :] = jnp.zeros((8, D), F32)

        first_tile = i == nt - 1
        prev8 = jnp.where(first_tile, 0.0, xh8_ref[...])
        hprev8 = jnp.where(first_tile, 0.0, hh8_ref[...])
        rx = xg_ref[:, 0:D]
        xc = _conv_tile(rx, prev8, cw_ref, cb_ref[...])
        xc_s[...] = xc
        xcb = xc.astype(BF)
        xcb_s[...] = xcb
        lamv = lam_ref[...]
        sp = _softplus(-lamv)
        ba = ba_ref[...]
        bx = bx_ref[...]
        for q in range(NQ):
            cs = slice(q * QW, (q + 1) * QW)
            r, ig, la = _gate_tile(xcb, wax_ref, ba, bx, sp, q)
            r_s[:, cs] = r
            ig_s[:, cs] = ig
            a_s[:, cs] = jnp.exp(la)
        hv = h_ref[...]
        g, dg = _gelu_grad(xg_ref[:, D:2 * D])
        dya = dya_ref[...]
        lm_s[...] = dya * g
        dpa_ref[:, D:2 * D] = ((dya * hv) * dg).astype(BF)

        c_s[0:1, :] = _scan_bwd(a_s, lm_s, c_s[0:1, :], tm)

        hprev = _shift_back(hprev8, h_ref[...], 1)
        for q in range(NQ):
            cs = slice(q * QW, (q + 1) * QW)
            r = r_s[:, cs]
            ig = ig_s[:, cs]
            a = a_s[:, cs]
            la = (-RG_C * r) * sp[:, cs]
            m = jnp.sqrt(_neg_expm1(2.0 * la))
            lm = lm_s[:, cs]
            xq = xc_s[:, cs]
            dixc = lm * m
            dla = (lm * hprev[:, cs]) * a - ((lm * (ig * xq)) * (a * a)) / m
            accs[3, :, cs] += _fold(dla * r)
            dza = (dla * (-RG_C * sp[:, cs])) * (r * (1.0 - r))
            dzx = (dixc * xq) * (ig * (1.0 - ig))
            accs[1, :, cs] += _fold(dza)
            accs[2, :, cs] += _fold(dzx)
            dz = jnp.concatenate([dza, dzx], axis=1).astype(BF)
            dxc_s[0:tm, cs] = dixc * ig + _dot_nt(dz, wax_ref[q])
            dwax_ref[q] += _dot_tn(xcb_s[:, cs], dz)

        cur = dxc_s[0:tm, :]
        nx = dxc_s[tm:tm + 8, :]
        drx = cw_ref[3:4, :] * cur
        for j in (1, 2, 3):
            drx = drx + cw_ref[3 - j:4 - j, :] * _shift_fwd(cur, nx, j)
        dpa_ref[:, 0:D] = drx.astype(BF)
        accs[0] += _fold(cur)
        rx = xg_ref[:, 0:D]
        accs[7] += _fold(cur * rx)
        for j in (1, 2, 3):
            accs[7 - j] += _fold(cur * _shift_back(prev8, rx, j))
        dxc_s[tm:tm + 8, :] = cur[0:8, :]

        @pl.when(i == nt - 1)
        def _():
            vec_ref[...] = jnp.zeros((8, D), F32)
            for j in range(8):
                vec_ref[j:j + 1, :] = jnp.sum(accs[j], axis=0, keepdims=True)
            vec_ref[3:4, :] = vec_ref[3:4, :] * (RG_C * _sigmoid(-lamv))
            exchange.finish()

    rev = lambda w: pl.BlockSpec((tm, w), lambda i: (nt - 1 - i, 0))
    halo = pl.BlockSpec((8, D), lambda i: (jnp.maximum((nt - 1 - i) * tb - 1, 0), 0))
    vec = _const_spec((1, D))
    any_spec = pl.BlockSpec(memory_space=pl.ANY)
    outs = pl.pallas_call(
        body, name="mixer_bwd_seq", grid=(nt,),
        in_specs=[any_spec, rev(D), rev(2 * D), halo, rev(D), halo,
                  _const_spec((4, D)), vec, vec, vec, vec, _const_spec(wax.shape)] + [any_spec] * n,
        out_specs=[rev(2 * D), pl.BlockSpec((8, D), lambda i: (0, 0)),
                   pl.BlockSpec((NQ, QW, 2 * QW), lambda i: (0, 0, 0))] + [any_spec] * n,
        out_shape=[jax.ShapeDtypeStruct(dproj.shape, BF), jax.ShapeDtypeStruct((8, D), F32),
                   jax.ShapeDtypeStruct((NQ, QW, 2 * QW), F32)]
        + [jax.ShapeDtypeStruct(p.shape, p.dtype) for p in parts],
        input_output_aliases={0: 0},
        scratch_shapes=[pltpu.VMEM((tm, D), F32), pltpu.VMEM((tm, D), BF), pltpu.VMEM((tm, D), F32),
                        pltpu.VMEM((tm, D), F32), pltpu.VMEM((tm, D), F32), pltpu.VMEM((tm, D), F32),
                        pltpu.VMEM((tm + 8, D), F32), pltpu.VMEM((8, D), F32), pltpu.VMEM((8, RC, D), F32)]
        + _gather_sems(n),
        compiler_params=_params(("arbitrary",)),
    )(dproj, dya, proj, proj, h, h, cw, cb, ba, bx, lam, wax, *parts)
    return outs[:3], outs[3:]


def _dx_call(dproj, dx1, x, gmix, w_in_all):
    s = x.shape[0]
    tm = min(TM_DX, s)
    nt = s // tm
    nb, _, wb = w_in_all.shape

    def body(dp_ref, dx1_ref, x_ref, g_ref, w_ref, dx_ref, acc_ref, accs):
        i = pl.program_id(0)

        @pl.when(i == 0)
        def _():
            accs[...] = jnp.zeros(accs.shape, F32)

        xx = x_ref[...]
        r = lax.rsqrt(jnp.mean(xx * xx, axis=-1, keepdims=True) + EPS)
        xh = xx * r
        dh = _dot_nt(dp_ref[:, 0:wb], w_ref[0])
        for k in range(1, nb):
            dh = dh + _dot_nt(dp_ref[:, k * wb:(k + 1) * wb], w_ref[k])
        accs[...] += _fold(dh * xh)
        dxh = dh * g_ref[...]
        dx_ref[...] = dx1_ref[...] + r * (dxh - xh * jnp.mean(dxh * xh, axis=-1, keepdims=True))

        @pl.when(i == nt - 1)
        def _():
            acc_ref[...] = jnp.zeros((8, D), F32)
            acc_ref[0:1, :] = jnp.sum(accs[...], axis=0, keepdims=True)

    tile = lambda w: pl.BlockSpec((tm, w), lambda i: (i, 0))
    return pl.pallas_call(
        body, name="dx", grid=(nt,),
        in_specs=[tile(nb * wb), tile(D), tile(D), _const_spec((1, D)), _const_spec(w_in_all.shape)],
        out_specs=[tile(D), pl.BlockSpec((8, D), lambda i: (0, 0))],
        out_shape=[jax.ShapeDtypeStruct((s, D), F32), jax.ShapeDtypeStruct((8, D), F32)],
        scratch_shapes=[pltpu.VMEM((RC, D), F32)],
        compiler_params=_params(("arbitrary",)),
    )(dproj, dx1, x, gmix, w_in_all)


def _device_of(d):
    return (d // 4, lax.rem(d // 2, 2), lax.rem(d, 2))


def _dw_exchange_call(name, me, a, b, a_spec, b_spec, split, k1, n1, s):
    nb = NDEV // split
    r = k1 // split
    ts = min(TS_DW, s)
    ns = s // ts

    def slab(i, me_ref):
        return lax.rem(me_ref[0] // split + 1 + i, nb)

    def body(me_ref, a_ref, b_ref, own_ref, recv_ref, acc, sbuf, send_sems, recv_sems, local_sem):
        i = pl.program_id(0)
        j = pl.program_id(1)
        me = _lin(_me())
        p = _dot_tn(a_ref[...], b_ref[...])

        @pl.when(j == 0)
        def _():
            acc[...] = p

        @pl.when(j > 0)
        def _():
            acc[...] += p

        def send(step, h):
            d = slab(step, me_ref) * split + h
            cp = pltpu.make_async_remote_copy(
                src_ref=sbuf.at[lax.rem(step, 2), pl.ds(h * r, r)], dst_ref=recv_ref.at[me],
                send_sem=send_sems.at[d], recv_sem=recv_sems.at[me],
                device_id=_device_of(d), device_id_type=MESH)
            return cp, d != me

        def drain(step):
            for h in range(split):
                cp, sent = send(step, h)
                pl.when(sent)(cp.wait_send)

        @pl.when(j == ns - 1)
        def _():
            pl.when(i >= 2)(lambda: drain(i - 2))
            sbuf[lax.rem(i, 2)] = acc[...].astype(BF)
            for h in range(split):
                cp, sent = send(i, h)
                pl.when(sent)(cp.start)

            @pl.when(i == nb - 1)
            def _():
                off = pl.multiple_of(lax.rem(me, split) * r, RC)
                own_ref[...] = acc[pl.ds(off, r), :]
                mine = pltpu.make_async_copy(sbuf.at[lax.rem(i, 2), pl.ds(off, r)], recv_ref.at[me], local_sem)
                mine.start()
                drain(i - 1)
                drain(i)
                for d in range(NDEV):
                    wait = pltpu.make_async_remote_copy(
                        src_ref=sbuf.at[0, pl.ds(0, r)], dst_ref=recv_ref.at[d],
                        send_sem=send_sems.at[d], recv_sem=recv_sems.at[d],
                        device_id=_device_of(d), device_id_type=MESH).wait_recv
                    pl.when(d != me)(wait)
                mine.wait()

    grid_spec = pltpu.PrefetchScalarGridSpec(
        num_scalar_prefetch=1, grid=(nb, ns),
        in_specs=[a_spec(ts, slab), b_spec(ts, slab)],
        out_specs=[pl.BlockSpec((r, n1), lambda i, j, me_ref: (0, 0)), pl.BlockSpec(memory_space=pl.ANY)],
        scratch_shapes=[pltpu.VMEM((k1, n1), F32), pltpu.VMEM((2, k1, n1), BF),
                        pltpu.SemaphoreType.DMA((NDEV,)), pltpu.SemaphoreType.DMA((NDEV,)),
                        pltpu.SemaphoreType.DMA(())])
    return pl.pallas_call(
        body, name=name, grid_spec=grid_spec,
        out_shape=[jax.ShapeDtypeStruct((r, n1), F32), jax.ShapeDtypeStruct((NDEV, r, n1), BF)],
        compiler_params=_params(("arbitrary", "arbitrary")),
    )(me, a, b)


def _dw_plain_call(name, me, a, b, a_spec, b_spec, split, k1, n1, s):
    nb = NDEV // split
    r = k1 // split
    ts = min(TS_DW, s)
    ns = s // ts

    def slab(i, me_ref):
        return i

    def body(me_ref, a_ref, b_ref, own_ref, part_ref, acc):
        i = pl.program_id(0)
        j = pl.program_id(1)
        p = _dot_tn(a_ref[...], b_ref[...])

        @pl.when(j == 0)
        def _():
            acc[...] = p

        @pl.when(j > 0)
        def _():
            acc[...] += p

        @pl.when(j == ns - 1)
        def _():
            part_ref[...] = acc[...].astype(BF)

            @pl.when(i == me_ref[0] // split)
            def _():
                off = pl.multiple_of(lax.rem(me_ref[0], split) * r, RC)
                own_ref[...] = acc[pl.ds(off, r), :]

    grid_spec = pltpu.PrefetchScalarGridSpec(
        num_scalar_prefetch=1, grid=(nb, ns),
        in_specs=[a_spec(ts, slab), b_spec(ts, slab)],
        out_specs=[pl.BlockSpec((r, n1), lambda i, j, me_ref: (0, 0)),
                   pl.BlockSpec((None, k1, n1), lambda i, j, me_ref: (i, 0, 0))],
        scratch_shapes=[pltpu.VMEM((k1, n1), F32)])
    own, part = pl.pallas_call(
        body, name=name, grid_spec=grid_spec,
        out_shape=[jax.ShapeDtypeStruct((r, n1), F32), jax.ShapeDtypeStruct((nb, k1, n1), BF)],
        compiler_params=_params(("arbitrary", "arbitrary")),
    )(me, a, b)
    return own, part.reshape(NDEV, r, n1)


def _rows2d(w):
    return lambda ts, slab: pl.BlockSpec((ts, w), lambda i, j, me_ref: (j, 0))


def _cols2d(w):
    return lambda ts, slab: pl.BlockSpec((ts, w), lambda i, j, me_ref: (j, slab(i, me_ref)))


def _blk3d(w):
    return lambda ts, slab: pl.BlockSpec((None, ts, w), lambda i, j, me_ref: (slab(i, me_ref), j, 0))


_BC1 = 1.0 - ADAM_B1 ** ADAM_STEP
_BC2 = 1.0 - ADAM_B2 ** ADAM_STEP


def _adamw_math(w, g, m, v):
    m = ADAM_B1 * m + (1.0 - ADAM_B1) * g
    v = ADAM_B2 * v + (1.0 - ADAM_B2) * (g * g)
    m_hat = m / _BC1
    v_hat = v / _BC2
    delta = -ADAM_LR * (m_hat / (jnp.sqrt(v_hat) + ADAM_EPS) + ADAM_WD * w)
    return delta, m, v


def _row_tile(r):
    for t in (256, 176, 128, 64, 32, 16, 8):
        if r % t == 0:
            return t
    return r


def _reduce_adamw_call(name, me, own, recv, w, m, v):
    r, c = own.shape
    tr = _row_tile(r)

    def body(me_ref, own_ref, recv_ref, w_ref, m_ref, v_ref, g_ref, d_ref, nm_ref, nv_ref):
        mine = me_ref[0]
        g = jnp.zeros((tr, c), F32)
        for sdev in range(NDEV):
            g = g + jnp.where(mine == sdev, own_ref[...], recv_ref[sdev].astype(F32))
        g_ref[...] = g
        d_ref[...], nm_ref[...], nv_ref[...] = _adamw_math(w_ref[...], g, m_ref[...], v_ref[...])

    tile = pl.BlockSpec((tr, c), lambda i, me_ref: (i, 0))
    grid_spec = pltpu.PrefetchScalarGridSpec(
        num_scalar_prefetch=1, grid=(r // tr,),
        in_specs=[tile, pl.BlockSpec((NDEV, tr, c), lambda i, me_ref: (0, i, 0)), tile, tile, tile],
        out_specs=[tile] * 4)
    return pl.pallas_call(
        body, name=name, grid_spec=grid_spec,
        out_shape=[jax.ShapeDtypeStruct((r, c), F32)] * 4,
        compiler_params=_params(("parallel",)),
    )(me, own, recv, w, m, v)


def _adamw_call(name, w, g, m, v):
    r, c = w.shape
    tr = _row_tile(r)

    def body(w_ref, g_ref, m_ref, v_ref, d_ref, nm_ref, nv_ref):
        d_ref[...], nm_ref[...], nv_ref[...] = _adamw_math(w_ref[...], g_ref[...], m_ref[...], v_ref[...])

    tile = pl.BlockSpec((tr, c), lambda i: (i, 0))
    return pl.pallas_call(
        body, name=name, grid=(r // tr,), in_specs=[tile] * 4, out_specs=[tile] * 3,
        out_shape=[jax.ShapeDtypeStruct((r, c), F32)] * 3,
        compiler_params=_params(("parallel",)),
    )(w, g, m, v)


def _me():
    return lax.axis_index("x"), lax.axis_index("y"), lax.axis_index("c")


def _flip(pos, r):
    x, y, c = pos
    return (1 - x if r & 4 else x, 1 - y if r & 2 else y, 1 - c if r & 1 else c)


def _lin(pos):
    return pos[0] * 4 + pos[1] * 2 + pos[2]


def _small_allreduce_call(g):
    def body(g_ref, out_ref, rbuf, send1, recv1, send2, recv2):
        me = _me()
        mi = _lin(me)

        def rows(d):
            return pl.ds(pl.multiple_of(d * SMALL_PER, 8), SMALL_PER)

        sent = []
        for r in range(1, NDEV):
            peer = _flip(me, r)
            cp = pltpu.make_async_remote_copy(
                src_ref=g_ref.at[rows(_lin(peer))], dst_ref=rbuf.at[mi],
                send_sem=send1.at[r - 1], recv_sem=recv1.at[r - 1], device_id=peer, device_id_type=MESH)
            cp.start()
            sent.append(cp)
        rbuf[mi] = g_ref[rows(mi), :]
        for r in range(1, NDEV):
            peer = _flip(me, r)
            pltpu.make_async_remote_copy(
                src_ref=g_ref.at[rows(mi)], dst_ref=rbuf.at[_lin(peer)],
                send_sem=send1.at[r - 1], recv_sem=recv1.at[r - 1], device_id=peer, device_id_type=MESH).wait_recv()
        for cp in sent:
            cp.wait_send()
        tot = rbuf[0]
        for d in range(1, NDEV):
            tot = tot + rbuf[d]
        out_ref[rows(mi), :] = tot
        sent = []
        for r in range(1, NDEV):
            peer = _flip(me, r)
            cp = pltpu.make_async_remote_copy(
                src_ref=out_ref.at[rows(mi)], dst_ref=out_ref.at[rows(mi)],
                send_sem=send2.at[r - 1], recv_sem=recv2.at[r - 1], device_id=peer, device_id_type=MESH)
            cp.start()
            sent.append(cp)
        for r in range(1, NDEV):
            peer = _flip(me, r)
            pltpu.make_async_remote_copy(
                src_ref=out_ref.at[rows(mi)], dst_ref=out_ref.at[rows(_lin(peer))],
                send_sem=send2.at[r - 1], recv_sem=recv2.at[r - 1], device_id=peer, device_id_type=MESH).wait_recv()
        for cp in sent:
            cp.wait_send()

    vm = pl.BlockSpec(memory_space=pltpu.VMEM)
    return pl.pallas_call(
        body, name="small_allreduce", in_specs=[vm], out_specs=vm,
        out_shape=jax.ShapeDtypeStruct((SMALL_ROWS, D), F32),
        scratch_shapes=[pltpu.VMEM((NDEV, SMALL_PER, D), F32)] + [pltpu.SemaphoreType.DMA((NDEV - 1,))] * 4,
    )(g)


def _head_blocks(w):
    z = jnp.zeros((64, 64), w.dtype)
    groups = []
    for q in range(NQ):
        rows = [jnp.concatenate([w[4 * q + a] if a == b else z for b in range(4)], axis=1) for a in range(4)]
        groups.append(jnp.concatenate(rows, axis=0))
    return jnp.stack(groups)


def _head_unblocks(g):
    return jnp.stack([g[q, 64 * a:64 * a + 64, 64 * a:64 * a + 64] for q in range(NQ) for a in range(4)])


def _local_step(x, tgt, p, me, order):
    s = x.shape[0]
    vec = lambda a: a.reshape(1, D)
    gmix, gffn, gfin = vec(p["norm_mix_g"]), vec(p["norm_ffn_g"]), vec(p["norm_final_g"])
    cb, ba, bx, lam = vec(p["conv_b"]), vec(p["rg_ba"]), vec(p["rg_bx"]), vec(p["rg_lambda"])
    lng, lnb = vec(p["sgu_ln_g"]), vec(p["sgu_ln_b"])
    wax = jnp.concatenate([_head_blocks(p["rg_wa"]), _head_blocks(p["rg_wx"])], axis=2).astype(BF)
    tril = jnp.tril(jnp.ones((CHUNK, CHUNK), bool))
    ws = jnp.where(tril[None], p["sgu_ws"], 0.0)
    wtr = ws.astype(BF)
    wtrt = jnp.swapaxes(ws, 1, 2).astype(BF)
    bias = jnp.repeat(p["sgu_bs"].T, GW, axis=1)
    shard = {k: p[k].astype(BF) for k in _BIG}

    proj, h1, w_in, (wpa, wpb, wo, cw) = _proj_gather_call(
        x, gmix, shard["w_in"], order, [shard["w_proj_a"], shard["w_proj_b"], shard["w_out"], p["conv_w"]])
    wpa, wpb, wo = (t.reshape(D, D) for t in (wpa, wpb, wo))
    cw = jnp.swapaxes(cw, 0, 1).reshape(4, D)
    (h, pa, pb, x1, ya, yb, mg), (wgu, wdn) = _mixer_fwd_call(
        proj, x, cw, cb, ba, bx, lam, lng, lnb, wax, wtr, bias, wpa, wpb, wo, [shard["w_gate_up"], shard["w_down"]])
    wdn = wdn.reshape(NDEV // 2, -1, D)
    nb, _, wb = w_in.shape
    fb = wgu.shape[2]
    nh = wdn.shape[0]
    dx1, dx1b, h2, act, dgu, dx2b, facc = _ffn_call(x1, tgt, gffn, gfin, wgu, wdn)
    assert nb == NDEV and 2 * nh == NDEV
    own_gu, part_gu = _dw_plain_call("dw_gate_up", me, h2, dgu, _rows2d(D), _blk3d(fb), 1, D, fb, s)
    own_dn, part_dn = _dw_plain_call("dw_down", me, act, dx2b, _blk3d(fb), _rows2d(D), 2, fb, D, s)
    (dproj, dya, dpa, dpb, bvec, dws, dbs), (recv_dn,) = _mixer_bwd_pre_call(
        dx1b, proj, pa, pb, lng, lnb, wtr, wtrt, bias, wpa, wpb, wo, [part_dn])
    own_pa, part_pa = _dw_plain_call("dw_proj_a", me, ya, dpa, _cols2d(QW), _rows2d(D), 2, QW, D, s)
    own_pb, part_pb = _dw_plain_call("dw_proj_b", me, yb, dpb, _cols2d(QW), _rows2d(D), 2, QW, D, s)
    own_wo, part_wo = _dw_plain_call("dw_out", me, mg, dx1b, _cols2d(QW), _rows2d(D), 2, QW, D, s)
    (dproj, svec, dwax), (recv_gu, recv_pa, recv_pb, recv_wo) = _mixer_bwd_seq_call(
        dproj, dya, proj, h, cw, cb, ba, bx, lam, wax, [part_gu, part_pa, part_pb, part_wo])
    dx, xacc = _dx_call(dproj, dx1, x, gmix, w_in)
    dw = {
        "w_gate_up": (own_gu, recv_gu), "w_down": (own_dn, recv_dn), "w_proj_a": (own_pa, recv_pa),
        "w_proj_b": (own_pb, recv_pb), "w_out": (own_wo, recv_wo),
        "w_in": _dw_exchange_call("dw_in", me, h1, dproj, _rows2d(D), _cols2d(wb), 1, D, wb, s),
    }
    small = {
        "norm_mix_g": xacc[0], "norm_ffn_g": facc[0], "norm_final_g": facc[1],
        "conv_b": svec[0], "rg_ba": svec[1], "rg_bx": svec[2], "rg_lambda": svec[3], "conv_w": svec[4:8],
        "sgu_ln_g": bvec[0], "sgu_ln_b": bvec[1],
        "rg_wa": _head_unblocks(dwax[:, :, 0:QW]), "rg_wx": _head_unblocks(dwax[:, :, QW:2 * QW]),
        "sgu_ws": dws, "sgu_bs": dbs[:, ::GW].T,
    }
    loss_sum = (0.5 / D) * jnp.sum(facc[2])
    return loss_sum, dx, dw, small


_BIG = ("w_in", "w_gate_up", "w_down", "w_proj_a", "w_proj_b", "w_out")
_VEC_ROWS = ("norm_mix_g", "norm_ffn_g", "norm_final_g", "conv_b", "rg_ba", "rg_bx", "rg_lambda",
             "sgu_ln_g", "sgu_ln_b", "sgu_bs")
_WEIGHTS = ("norm_mix_g", "w_in", "conv_w", "conv_b", "rg_wa", "rg_ba", "rg_wx", "rg_bx", "rg_lambda",
            "sgu_ln_g", "sgu_ln_b", "sgu_ws", "sgu_bs", "w_proj_a", "w_proj_b", "w_out", "norm_ffn_g",
            "w_gate_up", "w_down", "norm_final_g")


def _pack_small(t, conv_w):
    head = jnp.concatenate([t[k].reshape(1, D) for k in _VEC_ROWS] + [conv_w, jnp.zeros((2, D), F32)], axis=0)
    return jnp.concatenate([head, t["rg_wa"].reshape(64, D), t["rg_wx"].reshape(64, D), t["sgu_ws"].reshape(128, D),
                            jnp.zeros((SMALL_ROWS - 272, D), F32)], axis=0)


def _unpack_small(a):
    out = {k: a[j] for j, k in enumerate(_VEC_ROWS)}
    out["conv_w"] = a[10:14]
    out["rg_wa"] = a[16:80].reshape(16, 64, 64)
    out["rg_wx"] = a[80:144].reshape(16, 64, 64)
    out["sgu_ws"] = a[144:272].reshape(NGRP, CHUNK, CHUNK)
    return out


def kernel(x, norm_mix_g, w_in, conv_w, conv_b, rg_wa, rg_ba, rg_wx, rg_bx, rg_lambda, sgu_ln_g, sgu_ln_b, sgu_ws, sgu_bs, w_proj_a, w_proj_b, w_out, norm_ffn_g, w_gate_up, w_down, norm_final_g, loss_target, m_norm_mix_g, m_w_in, m_conv_w, m_conv_b, m_rg_wa, m_rg_ba, m_rg_wx, m_rg_bx, m_rg_lambda, m_sgu_ln_g, m_sgu_ln_b, m_sgu_ws, m_sgu_bs, m_w_proj_a, m_w_proj_b, m_w_out, m_norm_ffn_g, m_w_gate_up, m_w_down, m_norm_final_g, v_norm_mix_g, v_w_in, v_conv_w, v_conv_b, v_rg_wa, v_rg_ba, v_rg_wx, v_rg_bx, v_rg_lambda, v_sgu_ln_g, v_sgu_ln_b, v_sgu_ws, v_sgu_bs, v_w_proj_a, v_w_proj_b, v_w_out, v_norm_ffn_g, v_w_gate_up, v_w_down, v_norm_final_g):
    args = dict(locals())
    w = {k: args[k] for k in _WEIGHTS}
    m = {k: args["m_" + k] for k in _WEIGHTS}
    v = {k: args["v_" + k] for k in _WEIGHTS}
    for d in (w, m, v):
        for k in _WEIGHTS:
            if k != "norm_final_g":
                d[k] = d[k][0]
    me = _lin(_me())
    me1 = me.reshape(1).astype(jnp.int32)

    order = jnp.bitwise_xor(me, jnp.array(_PASS_FLIPS, jnp.int32)).astype(jnp.int32)

    loss_sum, dx, dw, small = _local_step(x[0], loss_target[0], w, me1, order)
    loss = lax.psum(loss_sum, ("x", "y", "c"))

    grads, delta, new_m, new_v = {}, {}, {}, {}
    for k in _BIG:
        own, recv = dw[k]
        grads[k], delta[k], new_m[k], new_v[k] = _reduce_adamw_call("adamw_" + k, me1, own, recv, w[k], m[k], v[k])

    gsum = _small_allreduce_call(_pack_small(small, small["conv_w"]))
    zc = jnp.zeros((4, D), F32)
    d_s, m_s, v_s = _adamw_call("adamw_small", _pack_small(w, zc), gsum, _pack_small(m, zc), _pack_small(v, zc))
    gs, ds, ms, vs = _unpack_small(gsum), _unpack_small(d_s), _unpack_small(m_s), _unpack_small(v_s)
    g_cw = lax.dynamic_slice(gs["conv_w"], (0, me * 128), (4, 128))
    ds["conv_w"], ms["conv_w"], vs["conv_w"] = _adamw_call("adamw_conv_w", w["conv_w"], g_cw, m["conv_w"], v["conv_w"])
    gs["conv_w"] = g_cw
    for k in _WEIGHTS:
        if k not in _BIG:
            shp = w[k].shape
            grads[k], delta[k], new_m[k], new_v[k] = (t[k].reshape(shp) for t in (gs, ds, ms, vs))

    def lift(t, k):
        return t[k] if k == "norm_final_g" else t[k][None]

    outs = [loss, dx[None]]
    for t in (grads, delta, new_m, new_v):
        outs += [lift(t, k) for k in _WEIGHTS]
    return tuple(outs)
```

```python
import functools

import jax
import jax.numpy as jnp
from jax import lax
from jax.experimental import pallas as pl
from jax.experimental.pallas import tpu as pltpu

F32 = jnp.float32
BF = jnp.bfloat16

D = 1024
NDEV = 8
EPS = 1e-6
RG_C = 8.0
CHUNK = 128
NGRP = 8
GW = 128
NQ = 4
QW = 256
RC = 16
SMALL_ROWS = 320
SMALL_PER = SMALL_ROWS // NDEV

ADAM_LR = 0.001
ADAM_B1 = 0.9
ADAM_B2 = 0.999
ADAM_EPS = 1e-08
ADAM_WD = 0.01
ADAM_STEP = 10

VMEM_LIMIT = 60 * 1024 * 1024

MESH = pl.DeviceIdType.MESH


def _rows(n, fn, unroll=2, rc=RC):
    def body(i, c):
        fn(pl.multiple_of(i * rc, rc))
        return c
    lax.fori_loop(0, n // rc, body, 0, unroll=unroll)


def _fold(v):
    return jnp.sum(v.reshape(v.shape[0] // RC, RC, v.shape[1]), axis=0)


def _dot(a, b):
    return jnp.dot(a, b, preferred_element_type=F32)


def _dot_nt(a, b):
    return lax.dot_general(a, b, (((1,), (1,)), ((), ())), preferred_element_type=F32)


def _dot_tn(a, b):
    return lax.dot_general(a, b, (((0,), (0,)), ((), ())), preferred_element_type=F32)


_GC = 0.7978845608028654
_GK = 0.044715


def _gelu(x):
    t = jnp.tanh(_GC * (x + _GK * (x * x * x)))
    return x * (0.5 * (1.0 + t))


def _gelu_grad(x):
    x2 = x * x
    t = jnp.tanh(_GC * (x + _GK * (x2 * x)))
    cdf = 0.5 * (1.0 + t)
    dg = cdf + (0.5 * x) * (1.0 - t * t) * (_GC * (1.0 + (3.0 * _GK) * x2))
    return x * cdf, dg


def _sigmoid(x):
    return jax.nn.sigmoid(x)


def _log1p(e):
    u = 1.0 + e
    d = u - 1.0
    return jnp.where(d == 0.0, e, jnp.log(u) * (e / jnp.where(d == 0.0, 1.0, d)))


def _softplus(z):
    return jnp.maximum(z, 0.0) + _log1p(jnp.exp(-jnp.abs(z)))


def _neg_expm1(z):
    u = jnp.exp(z)
    lu = jnp.log(u)
    k = (1.0 - u) * (z / jnp.where(lu == 0.0, 1.0, lu))
    small = jnp.where(lu == 0.0, -z, k)
    return jnp.where(z > -0.5, small, 1.0 - u)


def _shift_back(prev8, cur, j):
    cat = jnp.concatenate([prev8, cur], axis=0)
    return pltpu.roll(cat, j, 0)[8:8 + cur.shape[0]]


def _shift_fwd(cur, next8, j):
    cat = jnp.concatenate([cur, next8], axis=0)
    n = cat.shape[0]
    return pltpu.roll(cat, n - j, 0)[0:cur.shape[0]]


def _const_spec(shape):
    nd = len(shape)
    return pl.BlockSpec(shape, lambda *_: (0,) * nd, pipeline_mode=pl.Buffered(1))


def _params(sem):
    return pltpu.CompilerParams(dimension_semantics=sem, vmem_limit_bytes=VMEM_LIMIT)


TM_PROJ = 1024
TM_MIX = 256
TM_FFN = 256
TM_DX = 512
TS_DW = 4096


_CHIPS = (4, 2, 6)
_PASS_FLIPS = (0, 1, 4, 2, 6, 5, 3, 7)


class _Gather:
    def __init__(self, ins, outs, send_sems, recv_sems, local_sems):
        self.ins, self.outs = ins, outs
        self.send_sems, self.recv_sems, self.local_sems = send_sems, recv_sems, local_sems
        self.me = _me()
        self.sibling = _flip(self.me, 1)

    def _copy(self, a, kind, block, to, src=None):
        dst = self.outs[a].at[_lin(block)]
        return pltpu.make_async_remote_copy(
            src_ref=dst if src is None else src, dst_ref=dst,
            send_sem=self.send_sems.at[a, kind], recv_sem=self.recv_sems.at[a, kind],
            device_id=to, device_id_type=MESH)

    def _local(self, a):
        return pltpu.make_async_copy(self.ins[a], self.outs[a].at[_lin(self.me)], self.local_sems.at[a])

    def start(self):
        for a in range(len(self.ins)):
            self._local(a).start()
            self._copy(a, 0, self.me, self.sibling, src=self.ins[a]).start()
            for j, f in enumerate(_CHIPS):
                self._copy(a, 1 + j, self.me, _flip(self.me, f), src=self.ins[a]).start()

    def forward(self):
        for j, f in enumerate(_CHIPS):
            for a in range(len(self.ins)):
                self._copy(a, 1 + j, _flip(self.me, f), self.me).wait_recv()
                self._copy(a, 4 + j, _flip(self.me, f), self.sibling).start()

    def finish(self):
        for a in range(len(self.ins)):
            self._copy(a, 0, self.sibling, self.me).wait_recv()
            for j, f in enumerate(_CHIPS):
                self._copy(a, 4 + j, _flip(self.me, f | 1), self.me).wait_recv()
            self._copy(a, 0, self.me, self.sibling, src=self.ins[a]).wait_send()
            for j, f in enumerate(_CHIPS):
                self._copy(a, 1 + j, self.me, _flip(self.me, f), src=self.ins[a]).wait_send()
                self._copy(a, 4 + j, _flip(self.me, f), self.sibling).wait_send()
            self._local(a).wait()


class _Exchange:
    def __init__(self, ins, outs, send_sems, recv_sems, local_sems):
        self.ins, self.outs = ins, outs
        self.send_sems, self.recv_sems, self.local_sems = send_sems, recv_sems, local_sems
        self.me = _me()

    def _copy(self, a, r, outgoing):
        peer = _flip(self.me, r)
        src, dst = (peer, self.me) if outgoing else (self.me, peer)
        return pltpu.make_async_remote_copy(
            src_ref=self.ins[a].at[_lin(src)], dst_ref=self.outs[a].at[_lin(dst)],
            send_sem=self.send_sems.at[a, r - 1], recv_sem=self.recv_sems.at[a, r - 1],
            device_id=peer, device_id_type=MESH)

    def _local(self, a):
        mi = _lin(self.me)
        return pltpu.make_async_copy(self.ins[a].at[mi], self.outs[a].at[mi], self.local_sems.at[a])

    def start(self):
        for a in range(len(self.ins)):
            self._local(a).start()
        for r in range(1, NDEV):
            for a in range(len(self.ins)):
                self._copy(a, r, True).start()

    def finish(self):
        for r in range(1, NDEV):
            for a in range(len(self.ins)):
                self._copy(a, r, False).wait_recv()
        for r in range(1, NDEV):
            for a in range(len(self.ins)):
                self._copy(a, r, True).wait_send()
        for a in range(len(self.ins)):
            self._local(a).wait()


def _gather_sems(n):
    return [pltpu.SemaphoreType.DMA((n, 7)), pltpu.SemaphoreType.DMA((n, 7)), pltpu.SemaphoreType.DMA((n,))]


def _proj_gather_call(x, gmix, w_shard, order, extras):
    s = x.shape[0]
    tm = min(TM_PROJ, s)
    nt = s // tm
    wb = w_shard.shape[1]
    n = len(extras)

    def body(order_ref, x_ref, g_ref, wsh_ref, *rest):
        ex_in = rest[:n]
        proj_ref, h1_hbm, wall_hbm = rest[n:n + 3]
        ex_out = rest[n + 3:2 * n + 3]
        wall, hc, ws_send, ws_recv, ex_send, ex_recv, ex_local, out_sems = rest[2 * n + 3:]
        k = pl.program_id(0)
        i = pl.program_id(1)
        me = _me()
        sibling = _flip(me, 1)
        gather = _Gather(ex_in, ex_out, ex_send, ex_recv, ex_local)

        def wcopy(kind, block, to):
            ref = wall.at[_lin(block)]
            return pltpu.make_async_remote_copy(
                src_ref=ref, dst_ref=ref, send_sem=ws_send.at[kind], recv_sem=ws_recv.at[kind],
                device_id=to, device_id_type=MESH)

        head = i == 0

        @pl.when(head & (k == 0))
        def _():
            wall[_lin(me)] = wsh_ref[...]
            wcopy(0, me, sibling).start()
            for j, f in enumerate(_CHIPS):
                wcopy(1 + j, me, _flip(me, f)).start()
            gather.start()

        @pl.when(head & (k == 1))
        def _():
            wcopy(0, sibling, me).wait_recv()

        for j, f in enumerate(_CHIPS):
            @pl.when(head & (k == 2 + j))
            def _(j=j, f=f):
                wcopy(1 + j, _flip(me, f), me).wait_recv()
                wcopy(4 + j, _flip(me, f), sibling).start()

            @pl.when(head & (k == 5 + j))
            def _(j=j, f=f):
                wcopy(4 + j, _flip(me, f | 1), me).wait_recv()

        pl.when(head & (k == 5))(gather.forward)

        base = pl.multiple_of(i * tm, tm)

        @pl.when(k == 0)
        def _():
            g = g_ref[...]

            def norm(r0):
                xx = x_ref[pl.ds(r0, RC), :]
                r = lax.rsqrt(jnp.mean(xx * xx, axis=-1, keepdims=True) + EPS)
                hc[pl.ds(base + r0, RC), :] = ((xx * r) * g).astype(BF)
            _rows(tm, norm, unroll=4)

        proj_ref[...] = _dot(hc[pl.ds(base, tm), :], wall[order_ref[k]])

        @pl.when((k == NDEV - 1) & (i == nt - 1))
        def _():
            out_w = pltpu.make_async_copy(wall, wall_hbm, out_sems.at[0])
            out_h = pltpu.make_async_copy(hc, h1_hbm, out_sems.at[1])
            out_w.start()
            out_h.start()
            wcopy(0, me, sibling).wait_send()
            for j, f in enumerate(_CHIPS):
                wcopy(1 + j, me, _flip(me, f)).wait_send()
                wcopy(4 + j, _flip(me, f), sibling).wait_send()
            gather.finish()
            out_w.wait()
            out_h.wait()

    any_spec = pl.BlockSpec(memory_space=pl.ANY)
    grid_spec = pltpu.PrefetchScalarGridSpec(
        num_scalar_prefetch=1, grid=(NDEV, nt),
        in_specs=[pl.BlockSpec((tm, D), lambda k, i, o: (jnp.where(k == 0, i, nt - 1), 0)),
                  pl.BlockSpec((1, D), lambda k, i, o: (0, 0)),
                  pl.BlockSpec(w_shard.shape, lambda k, i, o: (0, 0))] + [any_spec] * n,
        out_specs=[pl.BlockSpec((tm, wb), lambda k, i, o: (i, o[k])), any_spec, any_spec] + [any_spec] * n,
        scratch_shapes=[pltpu.VMEM((NDEV,) + w_shard.shape, BF), pltpu.VMEM((s, D), BF),
                        pltpu.SemaphoreType.DMA((7,)), pltpu.SemaphoreType.DMA((7,))] + _gather_sems(n)
        + [pltpu.SemaphoreType.DMA((2,))])
    outs = pl.pallas_call(
        body, name="proj_gather", grid_spec=grid_spec,
        out_shape=[jax.ShapeDtypeStruct((s, NDEV * wb), F32), jax.ShapeDtypeStruct((s, D), BF),
                   jax.ShapeDtypeStruct((NDEV,) + w_shard.shape, BF)]
        + [jax.ShapeDtypeStruct((NDEV,) + e.shape, e.dtype) for e in extras],
        compiler_params=_params(("arbitrary", "arbitrary")),
    )(order, x, gmix, w_shard, *extras)
    return outs[0], outs[1], outs[2], outs[3:]


def _conv_tile(rx, prev8, cw_ref, cb):
    xc = cb + cw_ref[3:4, :] * rx
    for j in (1, 2, 3):
        xc = xc + cw_ref[3 - j:4 - j, :] * _shift_back(prev8, rx, j)
    return xc


def _gate_tile(xcb, wax_ref, ba, bx, sp, q):
    cs = slice(q * QW, (q + 1) * QW)
    z = _dot(xcb[:, cs], wax_ref[q])
    r = _sigmoid(z[:, 0:QW] + ba[:, cs])
    ig = _sigmoid(z[:, QW:2 * QW] + bx[:, cs])
    return r, ig, (-RG_C * r) * sp[:, cs]


def _scan_fwd(a_s, b_s, out_ref, h0, n):
    rowi = lax.broadcasted_iota(jnp.int32, (8, D), 0)

    def block(t, h):
        rows = pl.ds(pl.multiple_of(t * 8, 8), 8)
        a = a_s[rows, :]
        b = b_s[rows, :]
        for d in (1, 2, 4):
            m = rowi >= d
            b = jnp.where(m, a * pltpu.roll(b, d, 0) + b, b)
            a = jnp.where(m, a * pltpu.roll(a, d, 0), a)
        hb = b + a * h
        out_ref[rows, :] = hb
        return hb[7:8, :]
    return lax.fori_loop(0, n // 8, block, h0, unroll=4)


def _scan_bwd(a_s, lm_s, c0, n):
    rowi = lax.broadcasted_iota(jnp.int32, (8, D), 0)
    nblk = n // 8

    def block(k, cin):
        rows = pl.ds(pl.multiple_of((nblk - 1 - k) * 8, 8), 8)
        a = a_s[rows, :]
        dh = lm_s[rows, :]
        b = a * dh
        for d in (1, 2, 4):
            m = rowi < 8 - d
            b = jnp.where(m, a * pltpu.roll(b, 8 - d, 0) + b, b)
            a = jnp.where(m, a * pltpu.roll(a, 8 - d, 0), a)
        mu = b + a * cin
        lm_s[rows, :] = dh + jnp.where(rowi < 7, pltpu.roll(mu, 7, 0), cin)
        return mu[0:1, :]
    return lax.fori_loop(0, nblk, block, c0, unroll=4)


def _mixer_fwd_call(proj, x, cw, cb, ba, bx, lam, lng, lnb, wax, wtr, bias, wpa, wpb, wo, shards):
    s = x.shape[0]
    tm = min(TM_MIX, s)
    nt = s // tm
    pw = proj.shape[1]
    n = len(shards)

    def body(proj_ref, x_ref, cw_ref, cb_ref, ba_ref, bx_ref, lam_ref, lng_ref, lnb_ref, wax_ref, wtr_ref,
             bias_ref, wpa_ref, wpb_ref, wo_ref, *rest):
        sh_in = rest[:n]
        h_ref, pa_ref, pb_ref, x1_ref, ya_ref, yb_ref, mg_ref = rest[n:n + 7]
        sh_out = rest[n + 7:2 * n + 7]
        prev_s, a_s, b_s, hc_s, vn_s, mx_s, g_send, g_recv, g_local = rest[2 * n + 7:]
        i = pl.program_id(0)
        gather = _Gather(sh_in, sh_out, g_send, g_recv, g_local)

        @pl.when(i == 0)
        def _():
            gather.start()
            prev_s[...] = jnp.zeros((8, D), F32)
            hc_s[...] = jnp.zeros((8, D), F32)

        pl.when(i == nt // 2)(gather.forward)

        rx = proj_ref[:, 0:D]
        xc = _conv_tile(rx, prev_s[...], cw_ref, cb_ref[...])
        prev_s[...] = rx[tm - 8:tm, :]
        xcb = xc.astype(BF)
        sp = _softplus(-lam_ref[...])
        ba = ba_ref[...]
        bx = bx_ref[...]
        for q in range(NQ):
            cs = slice(q * QW, (q + 1) * QW)
            _, ig, la = _gate_tile(xcb, wax_ref, ba, bx, sp, q)
            a_s[:, cs] = jnp.exp(la)
            b_s[:, cs] = jnp.sqrt(_neg_expm1(2.0 * la)) * (ig * xc[:, cs])

        gv = _gelu(proj_ref[:, 3 * D:4 * D])
        dv = gv - jnp.mean(gv, axis=-1, keepdims=True)
        var = jnp.mean(dv * dv, axis=-1, keepdims=True)
        vn_s[...] = ((dv * lax.rsqrt(var + EPS)) * lng_ref[...] + lnb_ref[...]).astype(BF)
        nc = tm // CHUNK
        for c in range(nc):
            rs = slice(c * CHUNK, (c + 1) * CHUNK)
            for g in range(NGRP):
                cs = slice(g * GW, (g + 1) * GW)
                mx_s[rs, cs] = _dot(wtr_ref[g], vn_s[rs, cs])
        mixed = mx_s[...] + jnp.concatenate([bias_ref[...]] * nc, axis=0)
        yb = (_gelu(proj_ref[:, 2 * D:3 * D]) * mixed).astype(BF)
        yb_ref[...] = yb
        pb_ref[...] = _dot(yb, wpb_ref[...])

        hc_s[0:1, :] = _scan_fwd(a_s, b_s, h_ref, hc_s[0:1, :], tm)

        ya = (_gelu(proj_ref[:, D:2 * D]) * h_ref[...]).astype(BF)
        ya_ref[...] = ya
        pa = _dot(ya, wpa_ref[...])
        pa_ref[...] = pa
        mg = (_sigmoid(proj_ref[:, 4 * D:5 * D]) * pa + _sigmoid(proj_ref[:, 5 * D:6 * D]) * pb_ref[...]).astype(BF)
        mg_ref[...] = mg
        x1_ref[...] = x_ref[...] + _dot(mg, wo_ref[...])

        pl.when(i == nt - 1)(gather.finish)

    tile = lambda w: pl.BlockSpec((tm, w), lambda i: (i, 0))
    vec = _const_spec((1, D))
    any_spec = pl.BlockSpec(memory_space=pl.ANY)
    outs = pl.pallas_call(
        body, name="mixer_fwd", grid=(nt,),
        in_specs=[tile(pw), tile(D), _const_spec((4, D)), vec, vec, vec, vec, vec, vec,
                  _const_spec(wax.shape), _const_spec(wtr.shape), _const_spec(bias.shape),
                  _const_spec((D, D)), _const_spec((D, D)), _const_spec((D, D))] + [any_spec] * n,
        out_specs=[tile(D)] * 7 + [any_spec] * n,
        out_shape=[jax.ShapeDtypeStruct((s, D), F32)] * 4 + [jax.ShapeDtypeStruct((s, D), BF)] * 3
        + [jax.ShapeDtypeStruct((NDEV,) + e.shape, e.dtype) for e in shards],
        scratch_shapes=[pltpu.VMEM((8, D), F32), pltpu.VMEM((tm, D), F32), pltpu.VMEM((tm, D), F32),
                        pltpu.VMEM((8, D), F32), pltpu.VMEM((tm, D), BF), pltpu.VMEM((tm, D), F32)]
        + _gather_sems(n),
        compiler_params=_params(("arbitrary",)),
    )(proj, x, cw, cb, ba, bx, lam, lng, lnb, wax, wtr, bias, wpa, wpb, wo, *shards)
    return outs[:7], outs[7:]


def _ffn_call(x1, tgt, gffn, gfin, wgu, wdn):
    s = x1.shape[0]
    tm = min(TM_FFN, s)
    nt = s // tm
    nh = wdn.shape[0]
    fb = wgu.shape[2]

    def body(x1_ref, tgt_ref, gffn_ref, gfin_ref, wgu_ref, wdn_ref,
             dx1_ref, dx1b_ref, h2_ref, act_ref, dgu_ref, dx2b_ref, acc_ref,
             g_s, u_s, dx2_s, accs):
        i = pl.program_id(0)

        @pl.when(i == 0)
        def _():
            accs[...] = jnp.zeros(accs.shape, F32)

        gffn = gffn_ref[...]
        gfin = gfin_ref[...]

        x1 = x1_ref[...]
        r2 = lax.rsqrt(jnp.mean(x1 * x1, axis=-1, keepdims=True) + EPS)
        xh2 = x1 * r2
        h2 = (xh2 * gffn).astype(BF)
        h2_ref[...] = h2

        for k in range(nh):
            g = _dot(h2, wgu_ref[k])
            u = _dot(h2, wgu_ref[k + nh])
            g_s[k] = g
            u_s[k] = u
            act_ref[k] = ((g * _sigmoid(g)) * u).astype(BF)

        x2 = x1
        for k in range(nh):
            x2 = x2 + _dot(act_ref[k], wdn_ref[k])

        r3 = lax.rsqrt(jnp.mean(x2 * x2, axis=-1, keepdims=True) + EPS)
        xh = x2 * r3
        err = xh * gfin - tgt_ref[...]
        accs[2] += _fold(err * err)
        dy = err * (1.0 / D)
        accs[1] += _fold(dy * xh)
        dxh = dy * gfin
        dx2 = r3 * (dxh - xh * jnp.mean(dxh * xh, axis=-1, keepdims=True))
        dx2_s[...] = dx2
        dx2b = dx2.astype(BF)
        dx2b_ref[...] = dx2b

        for k in range(nh):
            da = _dot_nt(dx2b, wdn_ref[k])
            g = g_s[k]
            sg = _sigmoid(g)
            dgu_ref[k] = ((da * u_s[k]) * (sg * (1.0 + g * (1.0 - sg)))).astype(BF)
            dgu_ref[k + nh] = (da * (g * sg)).astype(BF)

        dh2 = _dot_nt(dgu_ref[0], wgu_ref[0])
        for k in range(1, 2 * nh):
            dh2 = dh2 + _dot_nt(dgu_ref[k], wgu_ref[k])

        accs[0] += _fold(dh2 * xh2)
        dxh = dh2 * gffn
        dx1 = dx2_s[...] + r2 * (dxh - xh2 * jnp.mean(dxh * xh2, axis=-1, keepdims=True))
        dx1_ref[...] = dx1
        dx1b_ref[...] = dx1.astype(BF)

        @pl.when(i == nt - 1)
        def _():
            acc_ref[...] = jnp.zeros((8, D), F32)
            for j in range(3):
                acc_ref[j:j + 1, :] = jnp.sum(accs[j], axis=0, keepdims=True)

    tile = lambda w: pl.BlockSpec((tm, w), lambda i: (i, 0))
    vec = _const_spec((1, D))
    return pl.pallas_call(
        body, name="ffn", grid=(nt,),
        in_specs=[tile(D), tile(D), vec, vec, _const_spec(wgu.shape), _const_spec(wdn.shape)],
        out_specs=[tile(D), tile(D), tile(D),
                   pl.BlockSpec((nh, tm, fb), lambda i: (0, i, 0)),
                   pl.BlockSpec((2 * nh, tm, fb), lambda i: (0, i, 0)),
                   tile(D), pl.BlockSpec((8, D), lambda i: (0, 0))],
        out_shape=[jax.ShapeDtypeStruct((s, D), F32), jax.ShapeDtypeStruct((s, D), BF),
                   jax.ShapeDtypeStruct((s, D), BF), jax.ShapeDtypeStruct((nh, s, fb), BF),
                   jax.ShapeDtypeStruct((2 * nh, s, fb), BF), jax.ShapeDtypeStruct((s, D), BF),
                   jax.ShapeDtypeStruct((8, D), F32)],
        scratch_shapes=[pltpu.VMEM((nh, tm, fb), F32), pltpu.VMEM((nh, tm, fb), F32), pltpu.VMEM((tm, D), F32),
                        pltpu.VMEM((3, RC, D), F32)],
        compiler_params=_params(("arbitrary",)),
    )(x1, tgt, gffn, gfin, wgu, wdn)


def _mixer_bwd_pre_call(dx1b, proj, pa, pb, lng, lnb, wtr, wtrt, bias, wpa, wpb, wo, parts):
    s = dx1b.shape[0]
    tm = min(TM_MIX, s)
    nt = s // tm
    pw = proj.shape[1]
    n = len(parts)

    def body(dx1b_ref, uv_ref, gg_ref, pa_ref, pb_ref, lng_ref, lnb_ref, wtr_ref, wtrt_ref, bias_ref,
             wpa_ref, wpb_ref, wo_ref, *rest):
        ex_in = rest[:n]
        dproj_ref, dya_ref, dpa_ref, dpb_ref, vec_ref, dws_ref, dbs_ref = rest[n:n + 7]
        ex_out = rest[n + 7:2 * n + 7]
        vn_s, mx_s, dmx_s, dvn_s, accs, dbs_s, e_send, e_recv, e_local = rest[2 * n + 7:]
        i = pl.program_id(0)
        exchange = _Exchange(ex_in, ex_out, e_send, e_recv, e_local)

        @pl.when(i == 0)
        def _():
            exchange.start()
            accs[...] = jnp.zeros(accs.shape, F32)
            dbs_s[...] = jnp.zeros(dbs_s.shape, F32)
            dws_ref[...] = jnp.zeros(dws_ref.shape, F32)

        dm = _dot_nt(dx1b_ref[...], wo_ref[...])
        sa = _sigmoid(gg_ref[:, 0:D])
        sb = _sigmoid(gg_ref[:, D:2 * D])
        dpa = dm * sa
        dpb = dm * sb
        dpab = dpa.astype(BF)
        dpbb = dpb.astype(BF)
        dpa_ref[...] = dpab
        dpb_ref[...] = dpbb
        dproj_ref[:, 4 * D:5 * D] = ((dpa * pa_ref[...]) * (1.0 - sa)).astype(BF)
        dproj_ref[:, 5 * D:6 * D] = ((dpb * pb_ref[...]) * (1.0 - sb)).astype(BF)
        dproj_ref[:, 0:2 * D] = jnp.zeros((tm, 2 * D), BF)

        dya_ref[...] = _dot_nt(dpab, wpa_ref[...])
        dyb = _dot_nt(dpbb, wpb_ref[...])

        lng = lng_ref[...]
        gv, dgelu_v = _gelu_grad(uv_ref[:, D:2 * D])
        dv = gv - jnp.mean(gv, axis=-1, keepdims=True)
        rstd = lax.rsqrt(jnp.mean(dv * dv, axis=-1, keepdims=True) + EPS)
        xh = dv * rstd
        vn_s[...] = (xh * lng + lnb_ref[...]).astype(BF)

        nc = tm // CHUNK
        for c in range(nc):
            rs = slice(c * CHUNK, (c + 1) * CHUNK)
            for g in range(NGRP):
                cs = slice(g * GW, (g + 1) * GW)
                mx_s[rs, cs] = _dot(wtr_ref[g], vn_s[rs, cs])

        gu, dgelu_u = _gelu_grad(uv_ref[:, 0:D])
        mixed = mx_s[...] + jnp.concatenate([bias_ref[...]] * nc, axis=0)
        dproj_ref[:, 2 * D:3 * D] = ((dyb * mixed) * dgelu_u).astype(BF)
        dmx = dyb * gu
        dmx_s[...] = dmx.astype(BF)
        dbs_s[...] += jnp.sum(dmx.reshape(nc, CHUNK, D), axis=0)

        for c in range(nc):
            rs = slice(c * CHUNK, (c + 1) * CHUNK)
            for g in range(NGRP):
                cs = slice(g * GW, (g + 1) * GW)
                dvn_s[rs, cs] = _dot(wtrt_ref[g], dmx_s[rs, cs])
                dws_ref[g] += _dot_nt(dmx_s[rs, cs], vn_s[rs, cs])

        dvn = dvn_s[...]
        accs[0] += _fold(dvn * xh)
        accs[1] += _fold(dvn)
        dxh = dvn * lng
        m1 = jnp.mean(dxh, axis=-1, keepdims=True)
        m2 = jnp.mean(dxh * xh, axis=-1, keepdims=True)
        dproj_ref[:, 3 * D:4 * D] = ((rstd * (dxh - m1 - xh * m2)) * dgelu_v).astype(BF)

        @pl.when(i == nt - 1)
        def _():
            vec_ref[...] = jnp.zeros((8, D), F32)
            for j in range(2):
                vec_ref[j:j + 1, :] = jnp.sum(accs[j], axis=0, keepdims=True)
            row = lax.broadcasted_iota(jnp.int32, (CHUNK, CHUNK), 0)
            col = lax.broadcasted_iota(jnp.int32, (CHUNK, CHUNK), 1)
            for g in range(NGRP):
                dws_ref[g] = jnp.where(row >= col, dws_ref[g], 0.0)
                gs = jnp.sum(dbs_s[:, g * GW:(g + 1) * GW], axis=1, keepdims=True)
                dbs_ref[:, g * GW:(g + 1) * GW] = jnp.broadcast_to(gs, (CHUNK, GW))
            exchange.finish()

    tile = lambda w: pl.BlockSpec((tm, w), lambda i: (i, 0))
    vec = _const_spec((1, D))
    any_spec = pl.BlockSpec(memory_space=pl.ANY)
    outs = pl.pallas_call(
        body, name="mixer_bwd_pre", grid=(nt,),
        in_specs=[tile(D), pl.BlockSpec((tm, 2 * D), lambda i: (i, 1)), pl.BlockSpec((tm, 2 * D), lambda i: (i, 2)),
                  tile(D), tile(D), vec, vec, _const_spec(wtr.shape), _const_spec(wtrt.shape),
                  _const_spec(bias.shape), _const_spec((D, D)), _const_spec((D, D)), _const_spec((D, D))]
        + [any_spec] * n,
        out_specs=[tile(pw), tile(D), tile(D), tile(D), pl.BlockSpec((8, D), lambda i: (0, 0)),
                   pl.BlockSpec((NGRP, CHUNK, CHUNK), lambda i: (0, 0, 0)),
                   pl.BlockSpec((CHUNK, D), lambda i: (0, 0))] + [any_spec] * n,
        out_shape=[jax.ShapeDtypeStruct((s, pw), BF), jax.ShapeDtypeStruct((s, D), F32),
                   jax.ShapeDtypeStruct((s, D), BF), jax.ShapeDtypeStruct((s, D), BF),
                   jax.ShapeDtypeStruct((8, D), F32), jax.ShapeDtypeStruct((NGRP, CHUNK, CHUNK), F32),
                   jax.ShapeDtypeStruct((CHUNK, D), F32)]
        + [jax.ShapeDtypeStruct(p.shape, p.dtype) for p in parts],
        scratch_shapes=[pltpu.VMEM((tm, D), BF), pltpu.VMEM((tm, D), F32), pltpu.VMEM((tm, D), BF),
                        pltpu.VMEM((tm, D), F32), pltpu.VMEM((2, RC, D), F32), pltpu.VMEM((CHUNK, D), F32)]
        + _gather_sems(n),
        compiler_params=_params(("arbitrary",)),
    )(dx1b, proj, proj, pa, pb, lng, lnb, wtr, wtrt, bias, wpa, wpb, wo, *parts)
    return outs[:7], outs[7:]


def _mixer_bwd_seq_call(dproj, dya, proj, h, cw, cb, ba, bx, lam, wax, parts):
    s = dya.shape[0]
    tm = min(TM_MIX, s)
    nt = s // tm
    tb = tm // 8
    n = len(parts)

    def body(dproj_any, dya_ref, xg_ref, xh8_ref, h_ref, hh8_ref, cw_ref, cb_ref, ba_ref, bx_ref, lam_ref, wax_ref,
             *rest):
        del dproj_any
        ex_in = rest[:n]
        dpa_ref, vec_ref, dwax_ref = rest[n:n + 3]
        ex_out = rest[n + 3:2 * n + 3]
        xc_s, xcb_s, r_s, ig_s, a_s, lm_s, dxc_s, c_s, accs, e_send, e_recv, e_local = rest[2 * n + 3:]
        i = pl.program_id(0)
        exchange = _Exchange(ex_in, ex_out, e_send, e_recv, e_local)

        @pl.when(i == 0)
        def _():
            exchange.start()
            accs[...] = jnp.zeros(accs.shape, F32)
            dwax_ref[...] = jnp.zeros(dwax_ref.shape, F32)
            c_s[...] = jnp.zeros((8, D), F32)
            dxc_s[tm:tm + 8, :] = jnp.zeros((8, D), F32)

        first_tile = i == nt - 1
        prev8 = jnp.where(first_tile, 0.0, xh8_ref[...])
        hprev8 = jnp.where(first_tile, 0.0, hh8_ref[...])
        rx = xg_ref[:, 0:D]
        xc = _conv_tile(rx, prev8, cw_ref, cb_ref[...])
        xc_s[...] = xc
        xcb = xc.astype(BF)
        xcb_s[...] = xcb
        lamv = lam_ref[...]
        sp = _softplus(-lamv)
        ba = ba_ref[...]
        bx = bx_ref[...]
        for q in range(NQ):
            cs = slice(q * QW, (q + 1) * QW)
            r, ig, la = _gate_tile(xcb, wax_ref, ba, bx, sp, q)
            r_s[:, cs] = r
            ig_s[:, cs] = ig
            a_s[:, cs] = jnp.exp(la)
        hv = h_ref[...]
        g, dg = _gelu_grad(xg_ref[:, D:2 * D])
        dya = dya_ref[...]
        lm_s[...] = dya * g
        dpa_ref[:, D:2 * D] = ((dya * hv) * dg).astype(BF)

        c_s[0:1, :] = _scan_bwd(a_s, lm_s, c_s[0:1, :], tm)

        hprev = _shift_back(hprev8, h_ref[...], 1)
        for q in range(NQ):
            cs = slice(q * QW, (q + 1) * QW)
            r = r_s[:, cs]
            ig = ig_s[:, cs]
            a = a_s[:, cs]
            la = (-RG_C * r) * sp[:, cs]
            m = jnp.sqrt(_neg_expm1(2.0 * la))
            lm = lm_s[:, cs]
            xq = xc_s[:, cs]
            dixc = lm * m
            dla = (lm * hprev[:, cs]) * a - ((lm * (ig * xq)) * (a * a)) / m
            accs[3, :, cs] += _fold(dla * r)
            dza = (dla * (-RG_C * sp[:, cs])) * (r * (1.0 - r))
            dzx = (dixc * xq) * (ig * (1.0 - ig))
            accs[1, :, cs] += _fold(dza)
            accs[2, :, cs] += _fold(dzx)
            dz = jnp.concatenate([dza, dzx], axis=1).astype(BF)
            dxc_s[0:tm, cs] = dixc * ig + _dot_nt(dz, wax_ref[q])
            dwax_ref[q] += _dot_tn(xcb_s[:, cs], dz)

        cur = dxc_s[0:tm, :]
        nx = dxc_s[tm:tm + 8, :]
        drx = cw_ref[3:4, :] * cur
        for j in (1, 2, 3):
            drx = drx + cw_ref[3 - j:4 - j, :] * _shift_fwd(cur, nx, j)
        dpa_ref[:, 0:D] = drx.astype(BF)
        accs[0] += _fold(cur)
        rx = xg_ref[:, 0:D]
        accs[7] += _fold(cur * rx)
        for j in (1, 2, 3):
            accs[7 - j] += _fold(cur * _shift_back(prev8, rx, j))
        dxc_s[tm:tm + 8, :] = cur[0:8, :]

        @pl.when(i == nt - 1)
        def _():
            vec_ref[...] = jnp.zeros((8, D), F32)
            for j in range(8):
                vec_ref[j:j + 1, :] = jnp.sum(accs[j], axis=0, keepdims=True)
            vec_ref[3:4, :] = vec_ref[3:4, :] * (RG_C * _sigmoid(-lamv))
            exchange.finish()

    rev = lambda w: pl.BlockSpec((tm, w), lambda i: (nt - 1 - i, 0))
    halo = pl.BlockSpec((8, D), lambda i: (jnp.maximum((nt - 1 - i) * tb - 1, 0), 0))
    vec = _const_spec((1, D))
    any_spec = pl.BlockSpec(memory_space=pl.ANY)
    outs = pl.pallas_call(
        body, name="mixer_bwd_seq", grid=(nt,),
        in_specs=[any_spec, rev(D), rev(2 * D), halo, rev(D), halo,
                  _const_spec((4, D)), vec, vec, vec, vec, _const_spec(wax.shape)] + [any_spec] * n,
        out_specs=[rev(2 * D), pl.BlockSpec((8, D), lambda i: (0, 0)),
                   pl.BlockSpec((NQ, QW, 2 * QW), lambda i: (0, 0, 0))] + [any_spec] * n,
        out_shape=[jax.ShapeDtypeStruct(dproj.shape, BF), jax.ShapeDtypeStruct((8, D), F32),
                   jax.ShapeDtypeStruct((NQ, QW, 2 * QW), F32)]
        + [jax.ShapeDtypeStruct(p.shape, p.dtype) for p in parts],
        input_output_aliases={0: 0},
        scratch_shapes=[pltpu.VMEM((tm, D), F32), pltpu.VMEM((tm, D), BF), pltpu.VMEM((tm, D), F32),
                        pltpu.VMEM((tm, D), F32), pltpu.VMEM((tm, D), F32), pltpu.VMEM((tm, D), F32),
                        pltpu.VMEM((tm + 8, D), F32), pltpu.VMEM((8, D), F32), pltpu.VMEM((8, RC, D), F32)]
        + _gather_sems(n),
        compiler_params=_params(("arbitrary",)),
    )(dproj, dya, proj, proj, h, h, cw, cb, ba, bx, lam, wax, *parts)
    return outs[:3], outs[3:]


def _dx_call(dproj, dx1, x, gmix, w_in_all):
    s = x.shape[0]
    tm = min(TM_DX, s)
    nt = s // tm
    nb, _, wb = w_in_all.shape

    def body(dp_ref, dx1_ref, x_ref, g_ref, w_ref, dx_ref, acc_ref, accs):
        i = pl.program_id(0)

        @pl.when(i == 0)
        def _():
            accs[...] = jnp.zeros(accs.shape, F32)

        xx = x_ref[...]
        r = lax.rsqrt(jnp.mean(xx * xx, axis=-1, keepdims=True) + EPS)
        xh = xx * r
        dh = _dot_nt(dp_ref[:, 0:wb], w_ref[0])
        for k in range(1, nb):
            dh = dh + _dot_nt(dp_ref[:, k * wb:(k + 1) * wb], w_ref[k])
        accs[...] += _fold(dh * xh)
        dxh = dh * g_ref[...]
        dx_ref[...] = dx1_ref[...] + r * (dxh - xh * jnp.mean(dxh * xh, axis=-1, keepdims=True))

        @pl.when(i == nt - 1)
        def _():
            acc_ref[...] = jnp.zeros((8, D), F32)
            acc_ref[0:1, :] = jnp.sum(accs[...], axis=0, keepdims=True)

    tile = lambda w: pl.BlockSpec((tm, w), lambda i: (i, 0))
    return pl.pallas_call(
        body, name="dx", grid=(nt,),
        in_specs=[tile(nb * wb), tile(D), tile(D), _const_spec((1, D)), _const_spec(w_in_all.shape)],
        out_specs=[tile(D), pl.BlockSpec((8, D), lambda i: (0, 0))],
        out_shape=[jax.ShapeDtypeStruct((s, D), F32), jax.ShapeDtypeStruct((8, D), F32)],
        scratch_shapes=[pltpu.VMEM((RC, D), F32)],
        compiler_params=_params(("arbitrary",)),
    )(dproj, dx1, x, gmix, w_in_all)


def _device_of(d):
    return (d // 4, lax.rem(d // 2, 2), lax.rem(d, 2))


def _dw_exchange_call(name, me, a, b, a_spec, b_spec, split, k1, n1, s):
    nb = NDEV // split
    r = k1 // split
    ts = min(TS_DW, s)
    ns = s // ts

    def slab(i, me_ref):
        return lax.rem(me_ref[0] // split + 1 + i, nb)

    def body(me_ref, a_ref, b_ref, own_ref, recv_ref, acc, sbuf, send_sems, recv_sems, local_sem):
        i = pl.program_id(0)
        j = pl.program_id(1)
        me = _lin(_me())
        p = _dot_tn(a_ref[...], b_ref[...])

        @pl.when(j == 0)
        def _():
            acc[...] = p

        @pl.when(j > 0)
        def _():
            acc[...] += p

        def send(step, h):
            d = slab(step, me_ref) * split + h
            cp = pltpu.make_async_remote_copy(
                src_ref=sbuf.at[lax.rem(step, 2), pl.ds(h * r, r)], dst_ref=recv_ref.at[me],
                send_sem=send_sems.at[d], recv_sem=recv_sems.at[me],
                device_id=_device_of(d), device_id_type=MESH)
            return cp, d != me

        def drain(step):
            for h in range(split):
                cp, sent = send(step, h)
                pl.when(sent)(cp.wait_send)

        @pl.when(j == ns - 1)
        def _():
            pl.when(i >= 2)(lambda: drain(i - 2))
            sbuf[lax.rem(i, 2)] = acc[...].astype(BF)
            for h in range(split):
                cp, sent = send(i, h)
                pl.when(sent)(cp.start)

            @pl.when(i == nb - 1)
            def _():
                off = pl.multiple_of(lax.rem(me, split) * r, RC)
                own_ref[...] = acc[pl.ds(off, r), :]
                mine = pltpu.make_async_copy(sbuf.at[lax.rem(i, 2), pl.ds(off, r)], recv_ref.at[me], local_sem)
                mine.start()
                drain(i - 1)
                drain(i)
                for d in range(NDEV):
                    wait = pltpu.make_async_remote_copy(
                        src_ref=sbuf.at[0, pl.ds(0, r)], dst_ref=recv_ref.at[d],
                        send_sem=send_sems.at[d], recv_sem=recv_sems.at[d],
                        device_id=_device_of(d), device_id_type=MESH).wait_recv
                    pl.when(d != me)(wait)
                mine.wait()

    grid_spec = pltpu.PrefetchScalarGridSpec(
        num_scalar_prefetch=1, grid=(nb, ns),
        in_specs=[a_spec(ts, slab), b_spec(ts, slab)],
        out_specs=[pl.BlockSpec((r, n1), lambda i, j, me_ref: (0, 0)), pl.BlockSpec(memory_space=pl.ANY)],
        scratch_shapes=[pltpu.VMEM((k1, n1), F32), pltpu.VMEM((2, k1, n1), BF),
                        pltpu.SemaphoreType.DMA((NDEV,)), pltpu.SemaphoreType.DMA((NDEV,)),
                        pltpu.SemaphoreType.DMA(())])
    return pl.pallas_call(
        body, name=name, grid_spec=grid_spec,
        out_shape=[jax.ShapeDtypeStruct((r, n1), F32), jax.ShapeDtypeStruct((NDEV, r, n1), BF)],
        compiler_params=_params(("arbitrary", "arbitrary")),
    )(me, a, b)


def _dw_plain_call(name, me, a, b, a_spec, b_spec, split, k1, n1, s):
    nb = NDEV // split
    r = k1 // split
    ts = min(TS_DW, s)
    ns = s // ts

    def slab(i, me_ref):
        return i

    def body(me_ref, a_ref, b_ref, own_ref, part_ref, acc):
        i = pl.program_id(0)
        j = pl.program_id(1)
        p = _dot_tn(a_ref[...], b_ref[...])

        @pl.when(j == 0)
        def _():
            acc[...] = p

        @pl.when(j > 0)
        def _():
            acc[...] += p

        @pl.when(j == ns - 1)
        def _():
            part_ref[...] = acc[...].astype(BF)

            @pl.when(i == me_ref[0] // split)
            def _():
                off = pl.multiple_of(lax.rem(me_ref[0], split) * r, RC)
                own_ref[...] = acc[pl.ds(off, r), :]

    grid_spec = pltpu.PrefetchScalarGridSpec(
        num_scalar_prefetch=1, grid=(nb, ns),
        in_specs=[a_spec(ts, slab), b_spec(ts, slab)],
        out_specs=[pl.BlockSpec((r, n1), lambda i, j, me_ref: (0, 0)),
                   pl.BlockSpec((None, k1, n1), lambda i, j, me_ref: (i, 0, 0))],
        scratch_shapes=[pltpu.VMEM((k1, n1), F32)])
    own, part = pl.pallas_call(
        body, name=name, grid_spec=grid_spec,
        out_shape=[jax.ShapeDtypeStruct((r, n1), F32), jax.ShapeDtypeStruct((nb, k1, n1), BF)],
        compiler_params=_params(("arbitrary", "arbitrary")),
    )(me, a, b)
    return own, part.reshape(NDEV, r, n1)


def _rows2d(w):
    return lambda ts, slab: pl.BlockSpec((ts, w), lambda i, j, me_ref: (j, 0))


def _cols2d(w):
    return lambda ts, slab: pl.BlockSpec((ts, w), lambda i, j, me_ref: (j, slab(i, me_ref)))


def _blk3d(w):
    return lambda ts, slab: pl.BlockSpec((None, ts, w), lambda i, j, me_ref: (slab(i, me_ref), j, 0))


_BC1 = 1.0 - ADAM_B1 ** ADAM_STEP
_BC2 = 1.0 - ADAM_B2 ** ADAM_STEP


def _adamw_math(w, g, m, v):
    m = ADAM_B1 * m + (1.0 - ADAM_B1) * g
    v = ADAM_B2 * v + (1.0 - ADAM_B2) * (g * g)
    m_hat = m / _BC1
    v_hat = v / _BC2
    delta = -ADAM_LR * (m_hat / (jnp.sqrt(v_hat) + ADAM_EPS) + ADAM_WD * w)
    return delta, m, v


def _row_tile(r):
    for t in (256, 176, 128, 64, 32, 16, 8):
        if r % t == 0:
            return t
    return r


def _reduce_adamw_call(name, me, own, recv, w, m, v):
    r, c = own.shape
    tr = _row_tile(r)

    def body(me_ref, own_ref, recv_ref, w_ref, m_ref, v_ref, g_ref, d_ref, nm_ref, nv_ref):
        mine = me_ref[0]
        g = jnp.zeros((tr, c), F32)
        for sdev in range(NDEV):
            g = g + jnp.where(mine == sdev, own_ref[...], recv_ref[sdev].astype(F32))
        g_ref[...] = g
        d_ref[...], nm_ref[...], nv_ref[...] = _adamw_math(w_ref[...], g, m_ref[...], v_ref[...])

    tile = pl.BlockSpec((tr, c), lambda i, me_ref: (i, 0))
    grid_spec = pltpu.PrefetchScalarGridSpec(
        num_scalar_prefetch=1, grid=(r // tr,),
        in_specs=[tile, pl.BlockSpec((NDEV, tr, c), lambda i, me_ref: (0, i, 0)), tile, tile, tile],
        out_specs=[tile] * 4)
    return pl.pallas_call(
        body, name=name, grid_spec=grid_spec,
        out_shape=[jax.ShapeDtypeStruct((r, c), F32)] * 4,
        compiler_params=_params(("parallel",)),
    )(me, own, recv, w, m, v)


def _adamw_call(name, w, g, m, v):
    r, c = w.shape
    tr = _row_tile(r)

    def body(w_ref, g_ref, m_ref, v_ref, d_ref, nm_ref, nv_ref):
        d_ref[...], nm_ref[...], nv_ref[...] = _adamw_math(w_ref[...], g_ref[...], m_ref[...], v_ref[...])

    tile = pl.BlockSpec((tr, c), lambda i: (i, 0))
    return pl.pallas_call(
        body, name=name, grid=(r // tr,), in_specs=[tile] * 4, out_specs=[tile] * 3,
        out_shape=[jax.ShapeDtypeStruct((r, c), F32)] * 3,
        compiler_params=_params(("parallel",)),
    )(w, g, m, v)


def _me():
    return lax.axis_index("x"), lax.axis_index("y"), lax.axis_index("c")


def _flip(pos, r):
    x, y, c = pos
    return (1 - x if r & 4 else x, 1 - y if r & 2 else y, 1 - c if r & 1 else c)


def _lin(pos):
    return pos[0] * 4 + pos[1] * 2 + pos[2]


def _small_allreduce_call(g):
    def body(g_ref, out_ref, rbuf, send1, recv1, send2, recv2):
        me = _me()
        mi = _lin(me)

        def rows(d):
            return pl.ds(pl.multiple_of(d * SMALL_PER, 8), SMALL_PER)

        sent = []
        for r in range(1, NDEV):
            peer = _flip(me, r)
            cp = pltpu.make_async_remote_copy(
                src_ref=g_ref.at[rows(_lin(peer))], dst_ref=rbuf.at[mi],
                send_sem=send1.at[r - 1], recv_sem=recv1.at[r - 1], device_id=peer, device_id_type=MESH)
            cp.start()
            sent.append(cp)
        rbuf[mi] = g_ref[rows(mi), :]
        for r in range(1, NDEV):
            peer = _flip(me, r)
            pltpu.make_async_remote_copy(
                src_ref=g_ref.at[rows(mi)], dst_ref=rbuf.at[_lin(peer)],
                send_sem=send1.at[r - 1], recv_sem=recv1.at[r - 1], device_id=peer, device_id_type=MESH).wait_recv()
        for cp in sent:
            cp.wait_send()
        tot = rbuf[0]
        for d in range(1, NDEV):
            tot = tot + rbuf[d]
        out_ref[rows(mi), :] = tot
        sent = []
        for r in range(1, NDEV):
            peer = _flip(me, r)
            cp = pltpu.make_async_remote_copy(
                src_ref=out_ref.at[rows(mi)], dst_ref=out_ref.at[rows(mi)],
                send_sem=send2.at[r - 1], recv_sem=recv2.at[r - 1], device_id=peer, device_id_type=MESH)
            cp.start()
            sent.append(cp)
        for r in range(1, NDEV):
            peer = _flip(me, r)
            pltpu.make_async_remote_copy(
                src_ref=out_ref.at[rows(mi)], dst_ref=out_ref.at[rows(_lin(peer))],
                send_sem=send2.at[r - 1], recv_sem=recv2.at[r - 1], device_id=peer, device_id_type=MESH).wait_recv()
        for cp in sent:
            cp.wait_send()

    vm = pl.BlockSpec(memory_space=pltpu.VMEM)
    return pl.pallas_call(
        body, name="small_allreduce", in_specs=[vm], out_specs=vm,
        out_shape=jax.ShapeDtypeStruct((SMALL_ROWS, D), F32),
        scratch_shapes=[pltpu.VMEM((NDEV, SMALL_PER, D), F32)] + [pltpu.SemaphoreType.DMA((NDEV - 1,))] * 4,
    )(g)


def _head_blocks(w):
    z = jnp.zeros((64, 64), w.dtype)
    groups = []
    for q in range(NQ):
        rows = [jnp.concatenate([w[4 * q + a] if a == b else z for b in range(4)], axis=1) for a in range(4)]
        groups.append(jnp.concatenate(rows, axis=0))
    return jnp.stack(groups)


def _head_unblocks(g):
    return jnp.stack([g[q, 64 * a:64 * a + 64, 64 * a:64 * a + 64] for q in range(NQ) for a in range(4)])


def _local_step(x, tgt, p, me, order):
    s = x.shape[0]
    vec = lambda a: a.reshape(1, D)
    gmix, gffn, gfin = vec(p["norm_mix_g"]), vec(p["norm_ffn_g"]), vec(p["norm_final_g"])
    cb, ba, bx, lam = vec(p["conv_b"]), vec(p["rg_ba"]), vec(p["rg_bx"]), vec(p["rg_lambda"])
    lng, lnb = vec(p["sgu_ln_g"]), vec(p["sgu_ln_b"])
    wax = jnp.concatenate([_head_blocks(p["rg_wa"]), _head_blocks(p["rg_wx"])], axis=2).astype(BF)
    tril = jnp.tril(jnp.ones((CHUNK, CHUNK), bool))
    ws = jnp.where(tril[None], p["sgu_ws"], 0.0)
    wtr = ws.astype(BF)
    wtrt = jnp.swapaxes(ws, 1, 2).astype(BF)
    bias = jnp.repeat(p["sgu_bs"].T, GW, axis=1)
    shard = {k: p[k].astype(BF) for k in _BIG}

    proj, h1, w_in, (wpa, wpb, wo, cw) = _proj_gather_call(
        x, gmix, shard["w_in"], order, [shard["w_proj_a"], shard["w_proj_b"], shard["w_out"], p["conv_w"]])
    wpa, wpb, wo = (t.reshape(D, D) for t in (wpa, wpb, wo))
    cw = jnp.swapaxes(cw, 0, 1).reshape(4, D)
    (h, pa, pb, x1, ya, yb, mg), (wgu, wdn) = _mixer_fwd_call(
        proj, x, cw, cb, ba, bx, lam, lng, lnb, wax, wtr, bias, wpa, wpb, wo, [shard["w_gate_up"], shard["w_down"]])
    wdn = wdn.reshape(NDEV // 2, -1, D)
    nb, _, wb = w_in.shape
    fb = wgu.shape[2]
    nh = wdn.shape[0]
    dx1, dx1b, h2, act, dgu, dx2b, facc = _ffn_call(x1, tgt, gffn, gfin, wgu, wdn)
    assert nb == NDEV and 2 * nh == NDEV
    own_gu, part_gu = _dw_plain_call("dw_gate_up", me, h2, dgu, _rows2d(D), _blk3d(fb), 1, D, fb, s)
    own_dn, part_dn = _dw_plain_call("dw_down", me, act, dx2b, _blk3d(fb), _rows2d(D), 2, fb, D, s)
    (dproj, dya, dpa, dpb, bvec, dws, dbs), (recv_dn,) = _mixer_bwd_pre_call(
        dx1b, proj, pa, pb, lng, lnb, wtr, wtrt, bias, wpa, wpb, wo, [part_dn])
    own_pa, part_pa = _dw_plain_call("dw_proj_a", me, ya, dpa, _rows2d(D), _rows2d(D), NDEV, D, D, s)
    own_pb, part_pb = _dw_plain_call("dw_proj_b", me, yb, dpb, _rows2d(D), _rows2d(D), NDEV, D, D, s)
    own_wo, part_wo = _dw_plain_call("dw_out", me, mg, dx1b, _rows2d(D), _rows2d(D), NDEV, D, D, s)
    (dproj, svec, dwax), (recv_gu, recv_pa, recv_pb, recv_wo) = _mixer_bwd_seq_call(
        dproj, dya, proj, h, cw, cb, ba, bx, lam, wax, [part_gu, part_pa, part_pb, part_wo])
    dx, xacc = _dx_call(dproj, dx1, x, gmix, w_in)
    dw = {
        "w_gate_up": (own_gu, recv_gu), "w_down": (own_dn, recv_dn), "w_proj_a": (own_pa, recv_pa),
        "w_proj_b": (own_pb, recv_pb), "w_out": (own_wo, recv_wo),
        "w_in": _dw_exchange_call("dw_in", me, h1, dproj, _rows2d(D), _cols2d(wb), 1, D, wb, s),
    }
    small = {
        "norm_mix_g": xacc[0], "norm_ffn_g": facc[0], "norm_final_g": facc[1],
        "conv_b": svec[0], "rg_ba": svec[1], "rg_bx": svec[2], "rg_lambda": svec[3], "conv_w": svec[4:8],
        "sgu_ln_g": bvec[0], "sgu_ln_b": bvec[1],
        "rg_wa": _head_unblocks(dwax[:, :, 0:QW]), "rg_wx": _head_unblocks(dwax[:, :, QW:2 * QW]),
        "sgu_ws": dws, "sgu_bs": dbs[:, ::GW].T,
    }
    loss_sum = (0.5 / D) * jnp.sum(facc[2])
    return loss_sum, dx, dw, small


_BIG = ("w_in", "w_gate_up", "w_down", "w_proj_a", "w_proj_b", "w_out")
_VEC_ROWS = ("norm_mix_g", "norm_ffn_g", "norm_final_g", "conv_b", "rg_ba", "rg_bx", "rg_lambda",
             "sgu_ln_g", "sgu_ln_b", "sgu_bs")
_WEIGHTS = ("norm_mix_g", "w_in", "conv_w", "conv_b", "rg_wa", "rg_ba", "rg_wx", "rg_bx", "rg_lambda",
            "sgu_ln_g", "sgu_ln_b", "sgu_ws", "sgu_bs", "w_proj_a", "w_proj_b", "w_out", "norm_ffn_g",
            "w_gate_up", "w_down", "norm_final_g")


def _pack_small(t, conv_w):
    head = jnp.concatenate([t[k].reshape(1, D) for k in _VEC_ROWS] + [conv_w, jnp.zeros((2, D), F32)], axis=0)
    return jnp.concatenate([head, t["rg_wa"].reshape(64, D), t["rg_wx"].reshape(64, D), t["sgu_ws"].reshape(128, D),
                            jnp.zeros((SMALL_ROWS - 272, D), F32)], axis=0)


def _unpack_small(a):
    out = {k: a[j] for j, k in enumerate(_VEC_ROWS)}
    out["conv_w"] = a[10:14]
    out["rg_wa"] = a[16:80].reshape(16, 64, 64)
    out["rg_wx"] = a[80:144].reshape(16, 64, 64)
    out["sgu_ws"] = a[144:272].reshape(NGRP, CHUNK, CHUNK)
    return out


def kernel(x, norm_mix_g, w_in, conv_w, conv_b, rg_wa, rg_ba, rg_wx, rg_bx, rg_lambda, sgu_ln_g, sgu_ln_b, sgu_ws, sgu_bs, w_proj_a, w_proj_b, w_out, norm_ffn_g, w_gate_up, w_down, norm_final_g, loss_target, m_norm_mix_g, m_w_in, m_conv_w, m_conv_b, m_rg_wa, m_rg_ba, m_rg_wx, m_rg_bx, m_rg_lambda, m_sgu_ln_g, m_sgu_ln_b, m_sgu_ws, m_sgu_bs, m_w_proj_a, m_w_proj_b, m_w_out, m_norm_ffn_g, m_w_gate_up, m_w_down, m_norm_final_g, v_norm_mix_g, v_w_in, v_conv_w, v_conv_b, v_rg_wa, v_rg_ba, v_rg_wx, v_rg_bx, v_rg_lambda, v_sgu_ln_g, v_sgu_ln_b, v_sgu_ws, v_sgu_bs, v_w_proj_a, v_w_proj_b, v_w_out, v_norm_ffn_g, v_w_gate_up, v_w_down, v_norm_final_g):
    args = dict(locals())
    w = {k: args[k] for k in _WEIGHTS}
    m = {k: args["m_" + k] for k in _WEIGHTS}
    v = {k: args["v_" + k] for k in _WEIGHTS}
    for d in (w, m, v):
        for k in _WEIGHTS:
            if k != "norm_final_g":
                d[k] = d[k][0]
    me = _lin(_me())
    me1 = me.reshape(1).astype(jnp.int32)

    order = jnp.bitwise_xor(me, jnp.array(_PASS_FLIPS, jnp.int32)).astype(jnp.int32)

    loss_sum, dx, dw, small = _local_step(x[0], loss_target[0], w, me1, order)
    loss = lax.psum(loss_sum, ("x", "y", "c"))

    grads, delta, new_m, new_v = {}, {}, {}, {}
    for k in _BIG:
        own, recv = dw[k]
        grads[k], delta[k], new_m[k], new_v[k] = _reduce_adamw_call("adamw_" + k, me1, own, recv, w[k], m[k], v[k])

    gsum = _small_allreduce_call(_pack_small(small, small["conv_w"]))
    zc = jnp.zeros((4, D), F32)
    d_s, m_s, v_s = _adamw_call("adamw_small", _pack_small(w, zc), gsum, _pack_small(m, zc), _pack_small(v, zc))
    gs, ds, ms, vs = _unpack_small(gsum), _unpack_small(d_s), _unpack_small(m_s), _unpack_small(v_s)
    g_cw = lax.dynamic_slice(gs["conv_w"], (0, me * 128), (4, 128))
    ds["conv_w"], ms["conv_w"], vs["conv_w"] = _adamw_call("adamw_conv_w", w["conv_w"], g_cw, m["conv_w"], v["conv_w"])
    gs["conv_w"] = g_cw
    for k in _WEIGHTS:
        if k not in _BIG:
            shp = w[k].shape
            grads[k], delta[k], new_m[k], new_v[k] = (t[k].reshape(shp) for t in (gs, ds, ms, vs))

    def lift(t, k):
        return t[k] if k == "norm_final_g" else t[k][None]

    outs = [loss, dx[None]]
    for t in (grads, delta, new_m, new_v):
        outs += [lift(t, k) for k in _WEIGHTS]
    return tuple(outs)
```

```python
import functools

import jax
import jax.numpy as jnp
from jax import lax
from jax.experimental import pallas as pl
from jax.experimental.pallas import tpu as pltpu

F32 = jnp.float32
BF = jnp.bfloat16

D = 1024
NDEV = 8
EPS = 1e-6
RG_C = 8.0
CHUNK = 128
NGRP = 8
GW = 128
NQ = 4
QW = 256
RC = 16
SMALL_ROWS = 320
SMALL_PER = SMALL_ROWS // NDEV

ADAM_LR = 0.001
ADAM_B1 = 0.9
ADAM_B2 = 0.999
ADAM_EPS = 1e-08
ADAM_WD = 0.01
ADAM_STEP = 10

VMEM_LIMIT = 60 * 1024 * 1024

MESH = pl.DeviceIdType.MESH


def _rows(n, fn, unroll=2, rc=RC):
    def body(i, c):
        fn(pl.multiple_of(i * rc, rc))
        return c
    lax.fori_loop(0, n // rc, body, 0, unroll=unroll)


def _fold(v):
    return jnp.sum(v.reshape(v.shape[0] // RC, RC, v.shape[1]), axis=0)


def _dot(a, b):
    return jnp.dot(a, b, preferred_element_type=F32)


def _dot_nt(a, b):
    return lax.dot_general(a, b, (((1,), (1,)), ((), ())), preferred_element_type=F32)


def _dot_tn(a, b):
    return lax.dot_general(a, b, (((0,), (0,)), ((), ())), preferred_element_type=F32)


_GC = 0.7978845608028654
_GK = 0.044715


def _gelu(x):
    t = jnp.tanh(_GC * (x + _GK * (x * x * x)))
    return x * (0.5 * (1.0 + t))


def _gelu_grad(x):
    x2 = x * x
    t = jnp.tanh(_GC * (x + _GK * (x2 * x)))
    cdf = 0.5 * (1.0 + t)
    dg = cdf + (0.5 * x) * (1.0 - t * t) * (_GC * (1.0 + (3.0 * _GK) * x2))
    return x * cdf, dg


def _sigmoid(x):
    return jax.nn.sigmoid(x)


def _log1p(e):
    u = 1.0 + e
    d = u - 1.0
    return jnp.where(d == 0.0, e, jnp.log(u) * (e / jnp.where(d == 0.0, 1.0, d)))


def _softplus(z):
    return jnp.maximum(z, 0.0) + _log1p(jnp.exp(-jnp.abs(z)))


def _neg_expm1(z):
    u = jnp.exp(z)
    lu = jnp.log(u)
    k = (1.0 - u) * (z / jnp.where(lu == 0.0, 1.0, lu))
    small = jnp.where(lu == 0.0, -z, k)
    return jnp.where(z > -0.5, small, 1.0 - u)


def _shift_back(prev8, cur, j):
    cat = jnp.concatenate([prev8, cur], axis=0)
    return pltpu.roll(cat, j, 0)[8:8 + cur.shape[0]]


def _shift_fwd(cur, next8, j):
    cat = jnp.concatenate([cur, next8], axis=0)
    n = cat.shape[0]
    return pltpu.roll(cat, n - j, 0)[0:cur.shape[0]]


def _const_spec(shape):
    nd = len(shape)
    return pl.BlockSpec(shape, lambda *_: (0,) * nd, pipeline_mode=pl.Buffered(1))


def _params(sem):
    return pltpu.CompilerParams(dimension_semantics=sem, vmem_limit_bytes=VMEM_LIMIT)


TM_PROJ = 1024
TM_MIX = 256
TM_FFN = 256
TM_DX = 512
TS_DW = 4096


_CHIPS = (4, 2, 6)
_PASS_FLIPS = (0, 1, 4, 2, 6, 5, 3, 7)


class _Gather:
    def __init__(self, ins, outs, send_sems, recv_sems, local_sems):
        self.ins, self.outs = ins, outs
        self.send_sems, self.recv_sems, self.local_sems = send_sems, recv_sems, local_sems
        self.me = _me()
        self.sibling = _flip(self.me, 1)

    def _copy(self, a, kind, block, to, src=None):
        dst = self.outs[a].at[_lin(block)]
        return pltpu.make_async_remote_copy(
            src_ref=dst if src is None else src, dst_ref=dst,
            send_sem=self.send_sems.at[a, kind], recv_sem=self.recv_sems.at[a, kind],
            device_id=to, device_id_type=MESH)

    def _local(self, a):
        return pltpu.make_async_copy(self.ins[a], self.outs[a].at[_lin(self.me)], self.local_sems.at[a])

    def start(self):
        for a in range(len(self.ins)):
            self._local(a).start()
            self._copy(a, 0, self.me, self.sibling, src=self.ins[a]).start()
            for j, f in enumerate(_CHIPS):
                self._copy(a, 1 + j, self.me, _flip(self.me, f), src=self.ins[a]).start()

    def forward(self):
        for j, f in enumerate(_CHIPS):
            for a in range(len(self.ins)):
                self._copy(a, 1 + j, _flip(self.me, f), self.me).wait_recv()
                self._copy(a, 4 + j, _flip(self.me, f), self.sibling).start()

    def finish(self):
        for a in range(len(self.ins)):
            self._copy(a, 0, self.sibling, self.me).wait_recv()
            for j, f in enumerate(_CHIPS):
                self._copy(a, 4 + j, _flip(self.me, f | 1), self.me).wait_recv()
            self._copy(a, 0, self.me, self.sibling, src=self.ins[a]).wait_send()
            for j, f in enumerate(_CHIPS):
                self._copy(a, 1 + j, self.me, _flip(self.me, f), src=self.ins[a]).wait_send()
                self._copy(a, 4 + j, _flip(self.me, f), self.sibling).wait_send()
            self._local(a).wait()


class _Exchange:
    def __init__(self, ins, outs, send_sems, recv_sems, local_sems):
        self.ins, self.outs = ins, outs
        self.send_sems, self.recv_sems, self.local_sems = send_sems, recv_sems, local_sems
        self.me = _me()

    def _copy(self, a, r, outgoing):
        peer = _flip(self.me, r)
        src, dst = (peer, self.me) if outgoing else (self.me, peer)
        return pltpu.make_async_remote_copy(
            src_ref=self.ins[a].at[_lin(src)], dst_ref=self.outs[a].at[_lin(dst)],
            send_sem=self.send_sems.at[a, r - 1], recv_sem=self.recv_sems.at[a, r - 1],
            device_id=peer, device_id_type=MESH)

    def _local(self, a):
        mi = _lin(self.me)
        return pltpu.make_async_copy(self.ins[a].at[mi], self.outs[a].at[mi], self.local_sems.at[a])

    def start(self):
        for a in range(len(self.ins)):
            self._local(a).start()
        for r in range(1, NDEV):
            for a in range(len(self.ins)):
                self._copy(a, r, True).start()

    def finish(self):
        for r in range(1, NDEV):
            for a in range(len(self.ins)):
                self._copy(a, r, False).wait_recv()
        for r in range(1, NDEV):
            for a in range(len(self.ins)):
                self._copy(a, r, True).wait_send()
        for a in range(len(self.ins)):
            self._local(a).wait()


def _gather_sems(n):
    return [pltpu.SemaphoreType.DMA((n, 7)), pltpu.SemaphoreType.DMA((n, 7)), pltpu.SemaphoreType.DMA((n,))]


def _proj_gather_call(x, gmix, w_shard, order, extras):
    s = x.shape[0]
    tm = min(TM_PROJ, s)
    nt = s // tm
    wb = w_shard.shape[1]
    n = len(extras)

    def body(order_ref, x_ref, g_ref, wsh_ref, *rest):
        ex_in = rest[:n]
        proj_ref, h1_hbm, wall_hbm = rest[n:n + 3]
        ex_out = rest[n + 3:2 * n + 3]
        wall, hc, ws_send, ws_recv, ex_send, ex_recv, ex_local, out_sems = rest[2 * n + 3:]
        k = pl.program_id(0)
        i = pl.program_id(1)
        me = _me()
        sibling = _flip(me, 1)
        gather = _Gather(ex_in, ex_out, ex_send, ex_recv, ex_local)

        def wcopy(kind, block, to):
            ref = wall.at[_lin(block)]
            return pltpu.make_async_remote_copy(
                src_ref=ref, dst_ref=ref, send_sem=ws_send.at[kind], recv_sem=ws_recv.at[kind],
                device_id=to, device_id_type=MESH)

        head = i == 0

        @pl.when(head & (k == 0))
        def _():
            wall[_lin(me)] = wsh_ref[...]
            wcopy(0, me, sibling).start()
            for j, f in enumerate(_CHIPS):
                wcopy(1 + j, me, _flip(me, f)).start()
            gather.start()

        @pl.when(head & (k == 1))
        def _():
            wcopy(0, sibling, me).wait_recv()

        for j, f in enumerate(_CHIPS):
            @pl.when(head & (k == 2 + j))
            def _(j=j, f=f):
                wcopy(1 + j, _flip(me, f), me).wait_recv()
                wcopy(4 + j, _flip(me, f), sibling).start()

            @pl.when(head & (k == 5 + j))
            def _(j=j, f=f):
                wcopy(4 + j, _flip(me, f | 1), me).wait_recv()

        pl.when(head & (k == 5))(gather.forward)

        base = pl.multiple_of(i * tm, tm)

        @pl.when(k == 0)
        def _():
            g = g_ref[...]

            def norm(r0):
                xx = x_ref[pl.ds(r0, RC), :]
                r = lax.rsqrt(jnp.mean(xx * xx, axis=-1, keepdims=True) + EPS)
                hc[pl.ds(base + r0, RC), :] = ((xx * r) * g).astype(BF)
            _rows(tm, norm, unroll=4)

        proj_ref[...] = _dot(hc[pl.ds(base, tm), :], wall[order_ref[k]])

        @pl.when((k == NDEV - 1) & (i == nt - 1))
        def _():
            out_w = pltpu.make_async_copy(wall, wall_hbm, out_sems.at[0])
            out_h = pltpu.make_async_copy(hc, h1_hbm, out_sems.at[1])
            out_w.start()
            out_h.start()
            wcopy(0, me, sibling).wait_send()
            for j, f in enumerate(_CHIPS):
                wcopy(1 + j, me, _flip(me, f)).wait_send()
                wcopy(4 + j, _flip(me, f), sibling).wait_send()
            gather.finish()
            out_w.wait()
            out_h.wait()

    any_spec = pl.BlockSpec(memory_space=pl.ANY)
    grid_spec = pltpu.PrefetchScalarGridSpec(
        num_scalar_prefetch=1, grid=(NDEV, nt),
        in_specs=[pl.BlockSpec((tm, D), lambda k, i, o: (jnp.where(k == 0, i, nt - 1), 0)),
                  pl.BlockSpec((1, D), lambda k, i, o: (0, 0)),
                  pl.BlockSpec(w_shard.shape, lambda k, i, o: (0, 0))] + [any_spec] * n,
        out_specs=[pl.BlockSpec((tm, wb), lambda k, i, o: (i, o[k])), any_spec, any_spec] + [any_spec] * n,
        scratch_shapes=[pltpu.VMEM((NDEV,) + w_shard.shape, BF), pltpu.VMEM((s, D), BF),
                        pltpu.SemaphoreType.DMA((7,)), pltpu.SemaphoreType.DMA((7,))] + _gather_sems(n)
        + [pltpu.SemaphoreType.DMA((2,))])
    outs = pl.pallas_call(
        body, name="proj_gather", grid_spec=grid_spec,
        out_shape=[jax.ShapeDtypeStruct((s, NDEV * wb), F32), jax.ShapeDtypeStruct((s, D), BF),
                   jax.ShapeDtypeStruct((NDEV,) + w_shard.shape, BF)]
        + [jax.ShapeDtypeStruct((NDEV,) + e.shape, e.dtype) for e in extras],
        compiler_params=_params(("arbitrary", "arbitrary")),
    )(order, x, gmix, w_shard, *extras)
    return outs[0], outs[1], outs[2], outs[3:]


def _conv_tile(rx, prev8, cw_ref, cb):
    xc = cb + cw_ref[3:4, :] * rx
    for j in (1, 2, 3):
        xc = xc + cw_ref[3 - j:4 - j, :] * _shift_back(prev8, rx, j)
    return xc


def _gate_tile(xcb, wax_ref, ba, bx, sp, q):
    cs = slice(q * QW, (q + 1) * QW)
    z = _dot(xcb[:, cs], wax_ref[q])
    r = _sigmoid(z[:, 0:QW] + ba[:, cs])
    ig = _sigmoid(z[:, QW:2 * QW] + bx[:, cs])
    return r, ig, (-RG_C * r) * sp[:, cs]


def _scan_fwd(a_s, b_s, out_ref, h0, n):
    rowi = lax.broadcasted_iota(jnp.int32, (8, D), 0)

    def block(t, h):
        rows = pl.ds(pl.multiple_of(t * 8, 8), 8)
        a = a_s[rows, :]
        b = b_s[rows, :]
        for d in (1, 2, 4):
            m = rowi >= d
            b = jnp.where(m, a * pltpu.roll(b, d, 0) + b, b)
            a = jnp.where(m, a * pltpu.roll(a, d, 0), a)
        hb = b + a * h
        out_ref[rows, :] = hb
        return hb[7:8, :]
    return lax.fori_loop(0, n // 8, block, h0, unroll=4)


def _scan_bwd(a_s, lm_s, c0, n):
    rowi = lax.broadcasted_iota(jnp.int32, (8, D), 0)
    nblk = n // 8

    def block(k, cin):
        rows = pl.ds(pl.multiple_of((nblk - 1 - k) * 8, 8), 8)
        a = a_s[rows, :]
        dh = lm_s[rows, :]
        b = a * dh
        for d in (1, 2, 4):
            m = rowi < 8 - d
            b = jnp.where(m, a * pltpu.roll(b, 8 - d, 0) + b, b)
            a = jnp.where(m, a * pltpu.roll(a, 8 - d, 0), a)
        mu = b + a * cin
        lm_s[rows, :] = dh + jnp.where(rowi < 7, pltpu.roll(mu, 7, 0), cin)
        return mu[0:1, :]
    return lax.fori_loop(0, nblk, block, c0, unroll=4)


def _mixer_fwd_call(proj, x, cw, cb, ba, bx, lam, lng, lnb, wax, wtr, bias, wpa, wpb, wo, shards):
    s = x.shape[0]
    tm = min(TM_MIX, s)
    nt = s // tm
    pw = proj.shape[1]
    n = len(shards)

    def body(proj_ref, x_ref, cw_ref, cb_ref, ba_ref, bx_ref, lam_ref, lng_ref, lnb_ref, wax_ref, wtr_ref,
             bias_ref, wpa_ref, wpb_ref, wo_ref, *rest):
        sh_in = rest[:n]
        h_ref, pa_ref, pb_ref, x1_ref, ya_ref, yb_ref, mg_ref = rest[n:n + 7]
        sh_out = rest[n + 7:2 * n + 7]
        prev_s, a_s, b_s, hc_s, vn_s, mx_s, g_send, g_recv, g_local = rest[2 * n + 7:]
        i = pl.program_id(0)
        gather = _Gather(sh_in, sh_out, g_send, g_recv, g_local)

        @pl.when(i == 0)
        def _():
            gather.start()
            prev_s[...] = jnp.zeros((8, D), F32)
            hc_s[...] = jnp.zeros((8, D), F32)

        pl.when(i == nt // 2)(gather.forward)

        rx = proj_ref[:, 0:D]
        xc = _conv_tile(rx, prev_s[...], cw_ref, cb_ref[...])
        prev_s[...] = rx[tm - 8:tm, :]
        xcb = xc.astype(BF)
        sp = _softplus(-lam_ref[...])
        ba = ba_ref[...]
        bx = bx_ref[...]
        for q in range(NQ):
            cs = slice(q * QW, (q + 1) * QW)
            _, ig, la = _gate_tile(xcb, wax_ref, ba, bx, sp, q)
            a_s[:, cs] = jnp.exp(la)
            b_s[:, cs] = jnp.sqrt(_neg_expm1(2.0 * la)) * (ig * xc[:, cs])

        gv = _gelu(proj_ref[:, 3 * D:4 * D])
        dv = gv - jnp.mean(gv, axis=-1, keepdims=True)
        var = jnp.mean(dv * dv, axis=-1, keepdims=True)
        vn_s[...] = ((dv * lax.rsqrt(var + EPS)) * lng_ref[...] + lnb_ref[...]).astype(BF)
        nc = tm // CHUNK
        for c in range(nc):
            rs = slice(c * CHUNK, (c + 1) * CHUNK)
            for g in range(NGRP):
                cs = slice(g * GW, (g + 1) * GW)
                mx_s[rs, cs] = _dot(wtr_ref[g], vn_s[rs, cs])
        mixed = mx_s[...] + jnp.concatenate([bias_ref[...]] * nc, axis=0)
        yb = (_gelu(proj_ref[:, 2 * D:3 * D]) * mixed).astype(BF)
        yb_ref[...] = yb
        pb_ref[...] = _dot(yb, wpb_ref[...])

        hc_s[0:1, :] = _scan_fwd(a_s, b_s, h_ref, hc_s[0:1, :], tm)

        ya = (_gelu(proj_ref[:, D:2 * D]) * h_ref[...]).astype(BF)
        ya_ref[...] = ya
        pa = _dot(ya, wpa_ref[...])
        pa_ref[...] = pa
        mg = (_sigmoid(proj_ref[:, 4 * D:5 * D]) * pa + _sigmoid(proj_ref[:, 5 * D:6 * D]) * pb_ref[...]).astype(BF)
        mg_ref[...] = mg
        x1_ref[...] = x_ref[...] + _dot(mg, wo_ref[...])

        pl.when(i == nt - 1)(gather.finish)

    tile = lambda w: pl.BlockSpec((tm, w), lambda i: (i, 0))
    vec = _const_spec((1, D))
    any_spec = pl.BlockSpec(memory_space=pl.ANY)
    outs = pl.pallas_call(
        body, name="mixer_fwd", grid=(nt,),
        in_specs=[tile(pw), tile(D), _const_spec((4, D)), vec, vec, vec, vec, vec, vec,
                  _const_spec(wax.shape), _const_spec(wtr.shape), _const_spec(bias.shape),
                  _const_spec((D, D)), _const_spec((D, D)), _const_spec((D, D))] + [any_spec] * n,
        out_specs=[tile(D)] * 7 + [any_spec] * n,
        out_shape=[jax.ShapeDtypeStruct((s, D), F32)] * 4 + [jax.ShapeDtypeStruct((s, D), BF)] * 3
        + [jax.ShapeDtypeStruct((NDEV,) + e.shape, e.dtype) for e in shards],
        scratch_shapes=[pltpu.VMEM((8, D), F32), pltpu.VMEM((tm, D), F32), pltpu.VMEM((tm, D), F32),
                        pltpu.VMEM((8, D), F32), pltpu.VMEM((tm, D), BF), pltpu.VMEM((tm, D), F32)]
        + _gather_sems(n),
        compiler_params=_params(("arbitrary",)),
    )(proj, x, cw, cb, ba, bx, lam, lng, lnb, wax, wtr, bias, wpa, wpb, wo, *shards)
    return outs[:7], outs[7:]


def _ffn_call(x1, tgt, gffn, gfin, wgu, wdn):
    s = x1.shape[0]
    tm = min(TM_FFN, s)
    nt = s // tm
    nh = wdn.shape[0]
    fb = wgu.shape[2]

    def body(x1_ref, tgt_ref, gffn_ref, gfin_ref, wgu_ref, wdn_ref,
             dx1_ref, dx1b_ref, h2_ref, act_ref, dgu_ref, dx2b_ref, acc_ref,
             g_s, u_s, dx2_s, accs):
        i = pl.program_id(0)

        @pl.when(i == 0)
        def _():
            accs[...] = jnp.zeros(accs.shape, F32)

        gffn = gffn_ref[...]
        gfin = gfin_ref[...]

        x1 = x1_ref[...]
        r2 = lax.rsqrt(jnp.mean(x1 * x1, axis=-1, keepdims=True) + EPS)
        xh2 = x1 * r2
        h2 = (xh2 * gffn).astype(BF)
        h2_ref[...] = h2

        for k in range(nh):
            g = _dot(h2, wgu_ref[k])
            u = _dot(h2, wgu_ref[k + nh])
            g_s[k] = g
            u_s[k] = u
            act_ref[k] = ((g * _sigmoid(g)) * u).astype(BF)

        x2 = x1
        for k in range(nh):
            x2 = x2 + _dot(act_ref[k], wdn_ref[k])

        r3 = lax.rsqrt(jnp.mean(x2 * x2, axis=-1, keepdims=True) + EPS)
        xh = x2 * r3
        err = xh * gfin - tgt_ref[...]
        accs[2] += _fold(err * err)
        dy = err * (1.0 / D)
        accs[1] += _fold(dy * xh)
        dxh = dy * gfin
        dx2 = r3 * (dxh - xh * jnp.mean(dxh * xh, axis=-1, keepdims=True))
        dx2_s[...] = dx2
        dx2b = dx2.astype(BF)
        dx2b_ref[...] = dx2b

        for k in range(nh):
            da = _dot_nt(dx2b, wdn_ref[k])
            g = g_s[k]
            sg = _sigmoid(g)
            dgu_ref[k] = ((da * u_s[k]) * (sg * (1.0 + g * (1.0 - sg)))).astype(BF)
            dgu_ref[k + nh] = (da * (g * sg)).astype(BF)

        dh2 = _dot_nt(dgu_ref[0], wgu_ref[0])
        for k in range(1, 2 * nh):
            dh2 = dh2 + _dot_nt(dgu_ref[k], wgu_ref[k])

        accs[0] += _fold(dh2 * xh2)
        dxh = dh2 * gffn
        dx1 = dx2_s[...] + r2 * (dxh - xh2 * jnp.mean(dxh * xh2, axis=-1, keepdims=True))
        dx1_ref[...] = dx1
        dx1b_ref[...] = dx1.astype(BF)

        @pl.when(i == nt - 1)
        def _():
            acc_ref[...] = jnp.zeros((8, D), F32)
            for j in range(3):
                acc_ref[j:j + 1, :] = jnp.sum(accs[j], axis=0, keepdims=True)

    tile = lambda w: pl.BlockSpec((tm, w), lambda i: (i, 0))
    vec = _const_spec((1, D))
    return pl.pallas_call(
        body, name="ffn", grid=(nt,),
        in_specs=[tile(D), tile(D), vec, vec, _const_spec(wgu.shape), _const_spec(wdn.shape)],
        out_specs=[tile(D), tile(D), tile(D),
                   pl.BlockSpec((nh, tm, fb), lambda i: (0, i, 0)),
                   pl.BlockSpec((2 * nh, tm, fb), lambda i: (0, i, 0)),
                   tile(D), pl.BlockSpec((8, D), lambda i: (0, 0))],
        out_shape=[jax.ShapeDtypeStruct((s, D), F32), jax.ShapeDtypeStruct((s, D), BF),
                   jax.ShapeDtypeStruct((s, D), BF), jax.ShapeDtypeStruct((nh, s, fb), BF),
                   jax.ShapeDtypeStruct((2 * nh, s, fb), BF), jax.ShapeDtypeStruct((s, D), BF),
                   jax.ShapeDtypeStruct((8, D), F32)],
        scratch_shapes=[pltpu.VMEM((nh, tm, fb), F32), pltpu.VMEM((nh, tm, fb), F32), pltpu.VMEM((tm, D), F32),
                        pltpu.VMEM((3, RC, D), F32)],
        compiler_params=_params(("arbitrary",)),
    )(x1, tgt, gffn, gfin, wgu, wdn)


def _mixer_bwd_pre_call(dx1b, proj, pa, pb, lng, lnb, wtr, wtrt, bias, wpa, wpb, wo, parts):
    s = dx1b.shape[0]
    tm = min(TM_MIX, s)
    nt = s // tm
    pw = proj.shape[1]
    n = len(parts)

    def body(dx1b_ref, uv_ref, gg_ref, pa_ref, pb_ref, lng_ref, lnb_ref, wtr_ref, wtrt_ref, bias_ref,
             wpa_ref, wpb_ref, wo_ref, *rest):
        ex_in = rest[:n]
        dproj_ref, dya_ref, dpa_ref, dpb_ref, vec_ref, dws_ref, dbs_ref = rest[n:n + 7]
        ex_out = rest[n + 7:2 * n + 7]
        vn_s, mx_s, dmx_s, dvn_s, accs, dbs_s, e_send, e_recv, e_local = rest[2 * n + 7:]
        i = pl.program_id(0)
        exchange = _Exchange(ex_in, ex_out, e_send, e_recv, e_local)

        @pl.when(i == 0)
        def _():
            exchange.start()
            accs[...] = jnp.zeros(accs.shape, F32)
            dbs_s[...] = jnp.zeros(dbs_s.shape, F32)
            dws_ref[...] = jnp.zeros(dws_ref.shape, F32)

        dm = _dot_nt(dx1b_ref[...], wo_ref[...])
        sa = _sigmoid(gg_ref[:, 0:D])
        sb = _sigmoid(gg_ref[:, D:2 * D])
        dpa = dm * sa
        dpb = dm * sb
        dpab = dpa.astype(BF)
        dpbb = dpb.astype(BF)
        dpa_ref[...] = dpab
        dpb_ref[...] = dpbb
        dproj_ref[:, 2 * D:3 * D] = ((dpa * pa_ref[...]) * (1.0 - sa)).astype(BF)
        dproj_ref[:, 3 * D:4 * D] = ((dpb * pb_ref[...]) * (1.0 - sb)).astype(BF)

        dya_ref[...] = _dot_nt(dpab, wpa_ref[...])
        dyb = _dot_nt(dpbb, wpb_ref[...])

        lng = lng_ref[...]
        gv, dgelu_v = _gelu_grad(uv_ref[:, D:2 * D])
        dv = gv - jnp.mean(gv, axis=-1, keepdims=True)
        rstd = lax.rsqrt(jnp.mean(dv * dv, axis=-1, keepdims=True) + EPS)
        xh = dv * rstd
        vn_s[...] = (xh * lng + lnb_ref[...]).astype(BF)

        nc = tm // CHUNK
        for c in range(nc):
            rs = slice(c * CHUNK, (c + 1) * CHUNK)
            for g in range(NGRP):
                cs = slice(g * GW, (g + 1) * GW)
                mx_s[rs, cs] = _dot(wtr_ref[g], vn_s[rs, cs])

        gu, dgelu_u = _gelu_grad(uv_ref[:, 0:D])
        mixed = mx_s[...] + jnp.concatenate([bias_ref[...]] * nc, axis=0)
        dproj_ref[:, 0:D] = ((dyb * mixed) * dgelu_u).astype(BF)
        dmx = dyb * gu
        dmx_s[...] = dmx.astype(BF)
        dbs_s[...] += jnp.sum(dmx.reshape(nc, CHUNK, D), axis=0)

        for c in range(nc):
            rs = slice(c * CHUNK, (c + 1) * CHUNK)
            for g in range(NGRP):
                cs = slice(g * GW, (g + 1) * GW)
                dvn_s[rs, cs] = _dot(wtrt_ref[g], dmx_s[rs, cs])
                dws_ref[g] += _dot_nt(dmx_s[rs, cs], vn_s[rs, cs])

        dvn = dvn_s[...]
        accs[0] += _fold(dvn * xh)
        accs[1] += _fold(dvn)
        dxh = dvn * lng
        m1 = jnp.mean(dxh, axis=-1, keepdims=True)
        m2 = jnp.mean(dxh * xh, axis=-1, keepdims=True)
        dproj_ref[:, D:2 * D] = ((rstd * (dxh - m1 - xh * m2)) * dgelu_v).astype(BF)

        @pl.when(i == nt - 1)
        def _():
            vec_ref[...] = jnp.zeros((8, D), F32)
            for j in range(2):
                vec_ref[j:j + 1, :] = jnp.sum(accs[j], axis=0, keepdims=True)
            row = lax.broadcasted_iota(jnp.int32, (CHUNK, CHUNK), 0)
            col = lax.broadcasted_iota(jnp.int32, (CHUNK, CHUNK), 1)
            for g in range(NGRP):
                dws_ref[g] = jnp.where(row >= col, dws_ref[g], 0.0)
                gs = jnp.sum(dbs_s[:, g * GW:(g + 1) * GW], axis=1, keepdims=True)
                dbs_ref[:, g * GW:(g + 1) * GW] = jnp.broadcast_to(gs, (CHUNK, GW))
            exchange.finish()

    tile = lambda w: pl.BlockSpec((tm, w), lambda i: (i, 0))
    vec = _const_spec((1, D))
    any_spec = pl.BlockSpec(memory_space=pl.ANY)
    outs = pl.pallas_call(
        body, name="mixer_bwd_pre", grid=(nt,),
        in_specs=[tile(D), pl.BlockSpec((tm, 2 * D), lambda i: (i, 1)), pl.BlockSpec((tm, 2 * D), lambda i: (i, 2)),
                  tile(D), tile(D), vec, vec, _const_spec(wtr.shape), _const_spec(wtrt.shape),
                  _const_spec(bias.shape), _const_spec((D, D)), _const_spec((D, D)), _const_spec((D, D))]
        + [any_spec] * n,
        out_specs=[tile(4 * D), tile(D), tile(D), tile(D), pl.BlockSpec((8, D), lambda i: (0, 0)),
                   pl.BlockSpec((NGRP, CHUNK, CHUNK), lambda i: (0, 0, 0)),
                   pl.BlockSpec((CHUNK, D), lambda i: (0, 0))] + [any_spec] * n,
        out_shape=[jax.ShapeDtypeStruct((s, 4 * D), BF), jax.ShapeDtypeStruct((s, D), F32),
                   jax.ShapeDtypeStruct((s, D), BF), jax.ShapeDtypeStruct((s, D), BF),
                   jax.ShapeDtypeStruct((8, D), F32), jax.ShapeDtypeStruct((NGRP, CHUNK, CHUNK), F32),
                   jax.ShapeDtypeStruct((CHUNK, D), F32)]
        + [jax.ShapeDtypeStruct(p.shape, p.dtype) for p in parts],
        scratch_shapes=[pltpu.VMEM((tm, D), BF), pltpu.VMEM((tm, D), F32), pltpu.VMEM((tm, D), BF),
                        pltpu.VMEM((tm, D), F32), pltpu.VMEM((2, RC, D), F32), pltpu.VMEM((CHUNK, D), F32)]
        + _gather_sems(n),
        compiler_params=_params(("arbitrary",)),
    )(dx1b, proj, proj, pa, pb, lng, lnb, wtr, wtrt, bias, wpa, wpb, wo, *parts)
    return outs[:7], outs[7:]


def _mixer_bwd_seq_call(dprojb, dya, proj, h, x, dx1, gmix, w_in_all, cw, cb, ba, bx, lam, wax, parts):
    s = dya.shape[0]
    tm = min(TM_MIX, s)
    nt = s // tm
    tb = tm // 8
    n = len(parts)
    nb, _, wb = w_in_all.shape

    def body(dpb_ref, dya_ref, xg_ref, xh8_ref, h_ref, hh8_ref, x_ref, dx1_ref, gmix_ref, win_ref,
             cw_ref, cb_ref, ba_ref, bx_ref, lam_ref, wax_ref, *rest):
        ex_in = rest[:n]
        dpa_ref, dx_ref, vec_ref, dwax_ref = rest[n:n + 4]
        ex_out = rest[n + 4:2 * n + 4]
        xc_s, xcb_s, r_s, ig_s, a_s, lm_s, dh_s, dxc_s, c_s, accs, e_send, e_recv, e_local = rest[2 * n + 4:]
        i = pl.program_id(0)
        exchange = _Exchange(ex_in, ex_out, e_send, e_recv, e_local)

        @pl.when(i == 0)
        def _():
            exchange.start()
            accs[...] = jnp.zeros(accs.shape, F32)
            dwax_ref[...] = jnp.zeros(dwax_ref.shape, F32)
            c_s[...] = jnp.zeros((8, D), F32)
            dxc_s[tm:tm + 8, :] = jnp.zeros((8, D), F32)

        first_tile = i == nt - 1
        prev8 = jnp.where(first_tile, 0.0, xh8_ref[...])
        hprev8 = jnp.where(first_tile, 0.0, hh8_ref[...])
        rx = xg_ref[:, 0:D]
        xc = _conv_tile(rx, prev8, cw_ref, cb_ref[...])
        xc_s[...] = xc
        xcb = xc.astype(BF)
        xcb_s[...] = xcb
        lamv = lam_ref[...]
        sp = _softplus(-lamv)
        ba = ba_ref[...]
        bx = bx_ref[...]
        for q in range(NQ):
            cs = slice(q * QW, (q + 1) * QW)
            r, ig, la = _gate_tile(xcb, wax_ref, ba, bx, sp, q)
            r_s[:, cs] = r
            ig_s[:, cs] = ig
            a_s[:, cs] = jnp.exp(la)
        hv = h_ref[...]
        g, dg = _gelu_grad(xg_ref[:, D:2 * D])
        dya = dya_ref[...]
        lm_s[...] = dya * g
        drg = ((dya * hv) * dg).astype(BF)
        dpa_ref[:, D:2 * D] = drg
        dpa_ref[:, 2 * D:6 * D] = dpb_ref[...]

        def dpb_block(k):
            return _dot_nt(dpb_ref[:, k * wb - 2 * D:(k + 1) * wb - 2 * D], win_ref[k])
        dh = (_dot_nt(drg[:, 0:2 * wb - D], win_ref[1, :, D - wb:wb])
              + _dot_nt(drg[:, 2 * wb - D:D], win_ref[2, :, 0:2 * D - 2 * wb])
              + _dot_nt(dpb_ref[:, 0:3 * wb - 2 * D], win_ref[2, :, 2 * D - 2 * wb:wb]))
        for k in range(3, 6):
            dh = dh + dpb_block(k)
        dh_s[...] = dh

        c_s[0:1, :] = _scan_bwd(a_s, lm_s, c_s[0:1, :], tm)

        hprev = _shift_back(hprev8, h_ref[...], 1)
        for q in range(NQ):
            cs = slice(q * QW, (q + 1) * QW)
            r = r_s[:, cs]
            ig = ig_s[:, cs]
            a = a_s[:, cs]
            la = (-RG_C * r) * sp[:, cs]
            m = jnp.sqrt(_neg_expm1(2.0 * la))
            lm = lm_s[:, cs]
            xq = xc_s[:, cs]
            dixc = lm * m
            dla = (lm * hprev[:, cs]) * a - ((lm * (ig * xq)) * (a * a)) / m
            accs[3, :, cs] += _fold(dla * r)
            dza = (dla * (-RG_C * sp[:, cs])) * (r * (1.0 - r))
            dzx = (dixc * xq) * (ig * (1.0 - ig))
            accs[1, :, cs] += _fold(dza)
            accs[2, :, cs] += _fold(dzx)
            dz = jnp.concatenate([dza, dzx], axis=1).astype(BF)
            dxc_s[0:tm, cs] = dixc * ig + _dot_nt(dz, wax_ref[q])
            dwax_ref[q] += _dot_tn(xcb_s[:, cs], dz)

        cur = dxc_s[0:tm, :]
        nx = dxc_s[tm:tm + 8, :]
        drx = cw_ref[3:4, :] * cur
        for j in (1, 2, 3):
            drx = drx + cw_ref[3 - j:4 - j, :] * _shift_fwd(cur, nx, j)
        accs[0] += _fold(cur)
        rx = xg_ref[:, 0:D]
        accs[7] += _fold(cur * rx)
        for j in (1, 2, 3):
            accs[7 - j] += _fold(cur * _shift_back(prev8, rx, j))
        dxc_s[tm:tm + 8, :] = cur[0:8, :]

        drxb = drx.astype(BF)
        dpa_ref[:, 0:D] = drxb
        dh = dh_s[...] + _dot_nt(drxb[:, 0:wb], win_ref[0]) + _dot_nt(drxb[:, wb:D], win_ref[1, :, 0:D - wb])
        for k in range(6, nb):
            dh = dh + dpb_block(k)
        xx = x_ref[...]
        rn = lax.rsqrt(jnp.mean(xx * xx, axis=-1, keepdims=True) + EPS)
        xhn = xx * rn
        accs[8] += _fold(dh * xhn)
        dxh = dh * gmix_ref[...]
        dx_ref[...] = dx1_ref[...] + rn * (dxh - xhn * jnp.mean(dxh * xhn, axis=-1, keepdims=True))

        @pl.when(i == nt - 1)
        def _():
            vec_ref[...] = jnp.zeros((16, D), F32)
            for j in range(9):
                vec_ref[j:j + 1, :] = jnp.sum(accs[j], axis=0, keepdims=True)
            vec_ref[3:4, :] = vec_ref[3:4, :] * (RG_C * _sigmoid(-lamv))
            exchange.finish()

    rev = lambda w: pl.BlockSpec((tm, w), lambda i: (nt - 1 - i, 0))
    halo = pl.BlockSpec((8, D), lambda i: (jnp.maximum((nt - 1 - i) * tb - 1, 0), 0))
    vec = _const_spec((1, D))
    any_spec = pl.BlockSpec(memory_space=pl.ANY)
    outs = pl.pallas_call(
        body, name="mixer_bwd_seq", grid=(nt,),
        in_specs=[rev(4 * D), rev(D), rev(2 * D), halo, rev(D), halo, rev(D), rev(D), vec,
                  _const_spec(w_in_all.shape),
                  _const_spec((4, D)), vec, vec, vec, vec, _const_spec(wax.shape)] + [any_spec] * n,
        out_specs=[rev(6 * D), rev(D), pl.BlockSpec((16, D), lambda i: (0, 0)),
                   pl.BlockSpec((NQ, QW, 2 * QW), lambda i: (0, 0, 0))] + [any_spec] * n,
        out_shape=[jax.ShapeDtypeStruct((s, 6 * D), BF), jax.ShapeDtypeStruct((s, D), F32),
                   jax.ShapeDtypeStruct((16, D), F32), jax.ShapeDtypeStruct((NQ, QW, 2 * QW), F32)]
        + [jax.ShapeDtypeStruct(p.shape, p.dtype) for p in parts],
        scratch_shapes=[pltpu.VMEM((tm, D), F32), pltpu.VMEM((tm, D), BF), pltpu.VMEM((tm, D), F32),
                        pltpu.VMEM((tm, D), F32), pltpu.VMEM((tm, D), F32), pltpu.VMEM((tm, D), F32),
                        pltpu.VMEM((tm, D), F32),
                        pltpu.VMEM((tm + 8, D), F32), pltpu.VMEM((8, D), F32), pltpu.VMEM((9, RC, D), F32)]
        + _gather_sems(n),
        compiler_params=_params(("arbitrary",)),
    )(dprojb, dya, proj, proj, h, h, x, dx1, gmix, w_in_all, cw, cb, ba, bx, lam, wax, *parts)
    return outs[:4], outs[4:]


def _device_of(d):
    return (d // 4, lax.rem(d // 2, 2), lax.rem(d, 2))


def _dw_exchange_call(name, me, a, b, a_spec, b_spec, split, k1, n1, s):
    nb = NDEV // split
    r = k1 // split
    ts = min(TS_DW, s)
    ns = s // ts

    def slab(i, me_ref):
        return lax.rem(me_ref[0] // split + 1 + i, nb)

    def body(me_ref, a_ref, b_ref, own_ref, recv_ref, acc, sbuf, send_sems, recv_sems, local_sem):
        i = pl.program_id(0)
        j = pl.program_id(1)
        me = _lin(_me())
        p = _dot_tn(a_ref[...], b_ref[...])

        @pl.when(j == 0)
        def _():
            acc[...] = p

        @pl.when(j > 0)
        def _():
            acc[...] += p

        def send(step, h):
            d = slab(step, me_ref) * split + h
            cp = pltpu.make_async_remote_copy(
                src_ref=sbuf.at[lax.rem(step, 2), pl.ds(h * r, r)], dst_ref=recv_ref.at[me],
                send_sem=send_sems.at[d], recv_sem=recv_sems.at[me],
                device_id=_device_of(d), device_id_type=MESH)
            return cp, d != me

        def drain(step):
            for h in range(split):
                cp, sent = send(step, h)
                pl.when(sent)(cp.wait_send)

        @pl.when(j == ns - 1)
        def _():
            pl.when(i >= 2)(lambda: drain(i - 2))
            sbuf[lax.rem(i, 2)] = acc[...].astype(BF)
            for h in range(split):
                cp, sent = send(i, h)
                pl.when(sent)(cp.start)

            @pl.when(i == nb - 1)
            def _():
                off = pl.multiple_of(lax.rem(me, split) * r, RC)
                own_ref[...] = acc[pl.ds(off, r), :]
                mine = pltpu.make_async_copy(sbuf.at[lax.rem(i, 2), pl.ds(off, r)], recv_ref.at[me], local_sem)
                mine.start()
                drain(i - 1)
                drain(i)
                for d in range(NDEV):
                    wait = pltpu.make_async_remote_copy(
                        src_ref=sbuf.at[0, pl.ds(0, r)], dst_ref=recv_ref.at[d],
                        send_sem=send_sems.at[d], recv_sem=recv_sems.at[d],
                        device_id=_device_of(d), device_id_type=MESH).wait_recv
                    pl.when(d != me)(wait)
                mine.wait()

    grid_spec = pltpu.PrefetchScalarGridSpec(
        num_scalar_prefetch=1, grid=(nb, ns),
        in_specs=[a_spec(ts, slab), b_spec(ts, slab)],
        out_specs=[pl.BlockSpec((r, n1), lambda i, j, me_ref: (0, 0)), pl.BlockSpec(memory_space=pl.ANY)],
        scratch_shapes=[pltpu.VMEM((k1, n1), F32), pltpu.VMEM((2, k1, n1), BF),
                        pltpu.SemaphoreType.DMA((NDEV,)), pltpu.SemaphoreType.DMA((NDEV,)),
                        pltpu.SemaphoreType.DMA(())])
    return pl.pallas_call(
        body, name=name, grid_spec=grid_spec,
        out_shape=[jax.ShapeDtypeStruct((r, n1), F32), jax.ShapeDtypeStruct((NDEV, r, n1), BF)],
        compiler_params=_params(("arbitrary", "arbitrary")),
    )(me, a, b)


def _dw_plain_call(name, me, a, b, a_spec, b_spec, split, k1, n1, s):
    nb = NDEV // split
    r = k1 // split
    ts = min(TS_DW, s)
    ns = s // ts

    def slab(i, me_ref):
        return i

    def body(me_ref, a_ref, b_ref, own_ref, part_ref, acc):
        i = pl.program_id(0)
        j = pl.program_id(1)
        p = _dot_tn(a_ref[...], b_ref[...])

        @pl.when(j == 0)
        def _():
            acc[...] = p

        @pl.when(j > 0)
        def _():
            acc[...] += p

        @pl.when(j == ns - 1)
        def _():
            part_ref[...] = acc[...].astype(BF)

            @pl.when(i == me_ref[0] // split)
            def _():
                off = pl.multiple_of(lax.rem(me_ref[0], split) * r, RC)
                own_ref[...] = acc[pl.ds(off, r), :]

    grid_spec = pltpu.PrefetchScalarGridSpec(
        num_scalar_prefetch=1, grid=(nb, ns),
        in_specs=[a_spec(ts, slab), b_spec(ts, slab)],
        out_specs=[pl.BlockSpec((r, n1), lambda i, j, me_ref: (0, 0)),
                   pl.BlockSpec((None, k1, n1), lambda i, j, me_ref: (i, 0, 0))],
        scratch_shapes=[pltpu.VMEM((k1, n1), F32)])
    own, part = pl.pallas_call(
        body, name=name, grid_spec=grid_spec,
        out_shape=[jax.ShapeDtypeStruct((r, n1), F32), jax.ShapeDtypeStruct((nb, k1, n1), BF)],
        compiler_params=_params(("arbitrary", "arbitrary")),
    )(me, a, b)
    return own, part.reshape(NDEV, r, n1)


def _rows2d(w):
    return lambda ts, slab: pl.BlockSpec((ts, w), lambda i, j, me_ref: (j, 0))


def _cols2d(w):
    return lambda ts, slab: pl.BlockSpec((ts, w), lambda i, j, me_ref: (j, slab(i, me_ref)))


def _blk3d(w):
    return lambda ts, slab: pl.BlockSpec((None, ts, w), lambda i, j, me_ref: (slab(i, me_ref), j, 0))


_BC1 = 1.0 - ADAM_B1 ** ADAM_STEP
_BC2 = 1.0 - ADAM_B2 ** ADAM_STEP


def _adamw_math(w, g, m, v):
    m = ADAM_B1 * m + (1.0 - ADAM_B1) * g
    v = ADAM_B2 * v + (1.0 - ADAM_B2) * (g * g)
    m_hat = m / _BC1
    v_hat = v / _BC2
    delta = -ADAM_LR * (m_hat / (jnp.sqrt(v_hat) + ADAM_EPS) + ADAM_WD * w)
    return delta, m, v


def _row_tile(r):
    for t in (256, 176, 128, 64, 32, 16, 8):
        if r % t == 0:
            return t
    return r


def _reduce_adamw_call(name, me, own, recv, w, m, v):
    r, c = own.shape
    tr = _row_tile(r)

    def body(me_ref, own_ref, recv_ref, w_ref, m_ref, v_ref, g_ref, d_ref, nm_ref, nv_ref):
        mine = me_ref[0]
        g = jnp.zeros((tr, c), F32)
        for sdev in range(NDEV):
            g = g + jnp.where(mine == sdev, own_ref[...], recv_ref[sdev].astype(F32))
        g_ref[...] = g
        d_ref[...], nm_ref[...], nv_ref[...] = _adamw_math(w_ref[...], g, m_ref[...], v_ref[...])

    tile = pl.BlockSpec((tr, c), lambda i, me_ref: (i, 0))
    grid_spec = pltpu.PrefetchScalarGridSpec(
        num_scalar_prefetch=1, grid=(r // tr,),
        in_specs=[tile, pl.BlockSpec((NDEV, tr, c), lambda i, me_ref: (0, i, 0)), tile, tile, tile],
        out_specs=[tile] * 4)
    return pl.pallas_call(
        body, name=name, grid_spec=grid_spec,
        out_shape=[jax.ShapeDtypeStruct((r, c), F32)] * 4,
        compiler_params=_params(("parallel",)),
    )(me, own, recv, w, m, v)


def _adamw_call(name, w, g, m, v):
    r, c = w.shape
    tr = _row_tile(r)

    def body(w_ref, g_ref, m_ref, v_ref, d_ref, nm_ref, nv_ref):
        d_ref[...], nm_ref[...], nv_ref[...] = _adamw_math(w_ref[...], g_ref[...], m_ref[...], v_ref[...])

    tile = pl.BlockSpec((tr, c), lambda i: (i, 0))
    return pl.pallas_call(
        body, name=name, grid=(r // tr,), in_specs=[tile] * 4, out_specs=[tile] * 3,
        out_shape=[jax.ShapeDtypeStruct((r, c), F32)] * 3,
        compiler_params=_params(("parallel",)),
    )(w, g, m, v)


def _me():
    return lax.axis_index("x"), lax.axis_index("y"), lax.axis_index("c")


def _flip(pos, r):
    x, y, c = pos
    return (1 - x if r & 4 else x, 1 - y if r & 2 else y, 1 - c if r & 1 else c)


def _lin(pos):
    return pos[0] * 4 + pos[1] * 2 + pos[2]


def _small_allreduce_call(g):
    def body(g_ref, out_ref, rbuf, send1, recv1, send2, recv2):
        me = _me()
        mi = _lin(me)

        def rows(d):
            return pl.ds(pl.multiple_of(d * SMALL_PER, 8), SMALL_PER)

        sent = []
        for r in range(1, NDEV):
            peer = _flip(me, r)
            cp = pltpu.make_async_remote_copy(
                src_ref=g_ref.at[rows(_lin(peer))], dst_ref=rbuf.at[mi],
                send_sem=send1.at[r - 1], recv_sem=recv1.at[r - 1], device_id=peer, device_id_type=MESH)
            cp.start()
            sent.append(cp)
        rbuf[mi] = g_ref[rows(mi), :]
        for r in range(1, NDEV):
            peer = _flip(me, r)
            pltpu.make_async_remote_copy(
                src_ref=g_ref.at[rows(mi)], dst_ref=rbuf.at[_lin(peer)],
                send_sem=send1.at[r - 1], recv_sem=recv1.at[r - 1], device_id=peer, device_id_type=MESH).wait_recv()
        for cp in sent:
            cp.wait_send()
        tot = rbuf[0]
        for d in range(1, NDEV):
            tot = tot + rbuf[d]
        out_ref[rows(mi), :] = tot
        sent = []
        for r in range(1, NDEV):
            peer = _flip(me, r)
            cp = pltpu.make_async_remote_copy(
                src_ref=out_ref.at[rows(mi)], dst_ref=out_ref.at[rows(mi)],
                send_sem=send2.at[r - 1], recv_sem=recv2.at[r - 1], device_id=peer, device_id_type=MESH)
            cp.start()
            sent.append(cp)
        for r in range(1, NDEV):
            peer = _flip(me, r)
            pltpu.make_async_remote_copy(
                src_ref=out_ref.at[rows(mi)], dst_ref=out_ref.at[rows(_lin(peer))],
                send_sem=send2.at[r - 1], recv_sem=recv2.at[r - 1], device_id=peer, device_id_type=MESH).wait_recv()
        for cp in sent:
            cp.wait_send()

    vm = pl.BlockSpec(memory_space=pltpu.VMEM)
    return pl.pallas_call(
        body, name="small_allreduce", in_specs=[vm], out_specs=vm,
        out_shape=jax.ShapeDtypeStruct((SMALL_ROWS, D), F32),
        scratch_shapes=[pltpu.VMEM((NDEV, SMALL_PER, D), F32)] + [pltpu.SemaphoreType.DMA((NDEV - 1,))] * 4,
    )(g)


def _head_blocks(w):
    z = jnp.zeros((64, 64), w.dtype)
    groups = []
    for q in range(NQ):
        rows = [jnp.concatenate([w[4 * q + a] if a == b else z for b in range(4)], axis=1) for a in range(4)]
        groups.append(jnp.concatenate(rows, axis=0))
    return jnp.stack(groups)


def _head_unblocks(g):
    return jnp.stack([g[q, 64 * a:64 * a + 64, 64 * a:64 * a + 64] for q in range(NQ) for a in range(4)])


def _local_step(x, tgt, p, me, order):
    s = x.shape[0]
    vec = lambda a: a.reshape(1, D)
    gmix, gffn, gfin = vec(p["norm_mix_g"]), vec(p["norm_ffn_g"]), vec(p["norm_final_g"])
    cb, ba, bx, lam = vec(p["conv_b"]), vec(p["rg_ba"]), vec(p["rg_bx"]), vec(p["rg_lambda"])
    lng, lnb = vec(p["sgu_ln_g"]), vec(p["sgu_ln_b"])
    wax = jnp.concatenate([_head_blocks(p["rg_wa"]), _head_blocks(p["rg_wx"])], axis=2).astype(BF)
    tril = jnp.tril(jnp.ones((CHUNK, CHUNK), bool))
    ws = jnp.where(tril[None], p["sgu_ws"], 0.0)
    wtr = ws.astype(BF)
    wtrt = jnp.swapaxes(ws, 1, 2).astype(BF)
    bias = jnp.repeat(p["sgu_bs"].T, GW, axis=1)
    shard = {k: p[k].astype(BF) for k in _BIG}

    proj, h1, w_in, (wpa, wpb, wo, cw) = _proj_gather_call(
        x, gmix, shard["w_in"], order, [shard["w_proj_a"], shard["w_proj_b"], shard["w_out"], p["conv_w"]])
    wpa, wpb, wo = (t.reshape(D, D) for t in (wpa, wpb, wo))
    cw = jnp.swapaxes(cw, 0, 1).reshape(4, D)
    (h, pa, pb, x1, ya, yb, mg), (wgu, wdn) = _mixer_fwd_call(
        proj, x, cw, cb, ba, bx, lam, lng, lnb, wax, wtr, bias, wpa, wpb, wo, [shard["w_gate_up"], shard["w_down"]])
    wdn = wdn.reshape(NDEV // 2, -1, D)
    nb, _, wb = w_in.shape
    fb = wgu.shape[2]
    nh = wdn.shape[0]
    dx1, dx1b, h2, act, dgu, dx2b, facc = _ffn_call(x1, tgt, gffn, gfin, wgu, wdn)
    assert nb == NDEV and 2 * nh == NDEV
    own_gu, part_gu = _dw_plain_call("dw_gate_up", me, h2, dgu, _rows2d(D), _blk3d(fb), 1, D, fb, s)
    own_dn, part_dn = _dw_plain_call("dw_down", me, act, dx2b, _blk3d(fb), _rows2d(D), 2, fb, D, s)
    (dprojb, dya, dpa, dpb, bvec, dws, dbs), (recv_dn,) = _mixer_bwd_pre_call(
        dx1b, proj, pa, pb, lng, lnb, wtr, wtrt, bias, wpa, wpb, wo, [part_dn])
    own_pa, part_pa = _dw_plain_call("dw_proj_a", me, ya, dpa, _rows2d(D), _rows2d(D), NDEV, D, D, s)
    own_pb, part_pb = _dw_plain_call("dw_proj_b", me, yb, dpb, _rows2d(D), _rows2d(D), NDEV, D, D, s)
    own_wo, part_wo = _dw_plain_call("dw_out", me, mg, dx1b, _rows2d(D), _rows2d(D), NDEV, D, D, s)
    (dproj, dx, svec, dwax), (recv_gu, recv_pa, recv_pb, recv_wo) = _mixer_bwd_seq_call(
        dprojb, dya, proj, h, x, dx1, gmix, w_in, cw, cb, ba, bx, lam, wax, [part_gu, part_pa, part_pb, part_wo])
    dw = {
        "w_gate_up": (own_gu, recv_gu), "w_down": (own_dn, recv_dn), "w_proj_a": (own_pa, recv_pa),
        "w_proj_b": (own_pb, recv_pb), "w_out": (own_wo, recv_wo),
        "w_in": _dw_exchange_call("dw_in", me, h1, dproj, _rows2d(D), _cols2d(wb), 1, D, wb, s),
    }
    small = {
        "norm_mix_g": svec[8], "norm_ffn_g": facc[0], "norm_final_g": facc[1],
        "conv_b": svec[0], "rg_ba": svec[1], "rg_bx": svec[2], "rg_lambda": svec[3], "conv_w": svec[4:8],
        "sgu_ln_g": bvec[0], "sgu_ln_b": bvec[1],
        "rg_wa": _head_unblocks(dwax[:, :, 0:QW]), "rg_wx": _head_unblocks(dwax[:, :, QW:2 * QW]),
        "sgu_ws": dws, "sgu_bs": dbs[:, ::GW].T,
    }
    return facc[2], dx, dw, small


_BIG = ("w_in", "w_gate_up", "w_down", "w_proj_a", "w_proj_b", "w_out")
_VEC_ROWS = ("norm_mix_g", "norm_ffn_g", "norm_final_g", "conv_b", "rg_ba", "rg_bx", "rg_lambda",
             "sgu_ln_g", "sgu_ln_b", "sgu_bs")
_WEIGHTS = ("norm_mix_g", "w_in", "conv_w", "conv_b", "rg_wa", "rg_ba", "rg_wx", "rg_bx", "rg_lambda",
            "sgu_ln_g", "sgu_ln_b", "sgu_ws", "sgu_bs", "w_proj_a", "w_proj_b", "w_out", "norm_ffn_g",
            "w_gate_up", "w_down", "norm_final_g")


def _pack_small(t, conv_w, extra=None):
    extra = jnp.zeros((1, D), F32) if extra is None else extra.reshape(1, D)
    head = jnp.concatenate([t[k].reshape(1, D) for k in _VEC_ROWS] + [conv_w, extra, jnp.zeros((1, D), F32)], axis=0)
    return jnp.concatenate([head, t["rg_wa"].reshape(64, D), t["rg_wx"].reshape(64, D), t["sgu_ws"].reshape(128, D),
                            jnp.zeros((SMALL_ROWS - 272, D), F32)], axis=0)


def _unpack_small(a):
    out = {k: a[j] for j, k in enumerate(_VEC_ROWS)}
    out["conv_w"] = a[10:14]
    out["rg_wa"] = a[16:80].reshape(16, 64, 64)
    out["rg_wx"] = a[80:144].reshape(16, 64, 64)
    out["sgu_ws"] = a[144:272].reshape(NGRP, CHUNK, CHUNK)
    return out


def kernel(x, norm_mix_g, w_in, conv_w, conv_b, rg_wa, rg_ba, rg_wx, rg_bx, rg_lambda, sgu_ln_g, sgu_ln_b, sgu_ws, sgu_bs, w_proj_a, w_proj_b, w_out, norm_ffn_g, w_gate_up, w_down, norm_final_g, loss_target, m_norm_mix_g, m_w_in, m_conv_w, m_conv_b, m_rg_wa, m_rg_ba, m_rg_wx, m_rg_bx, m_rg_lambda, m_sgu_ln_g, m_sgu_ln_b, m_sgu_ws, m_sgu_bs, m_w_proj_a, m_w_proj_b, m_w_out, m_norm_ffn_g, m_w_gate_up, m_w_down, m_norm_final_g, v_norm_mix_g, v_w_in, v_conv_w, v_conv_b, v_rg_wa, v_rg_ba, v_rg_wx, v_rg_bx, v_rg_lambda, v_sgu_ln_g, v_sgu_ln_b, v_sgu_ws, v_sgu_bs, v_w_proj_a, v_w_proj_b, v_w_out, v_norm_ffn_g, v_w_gate_up, v_w_down, v_norm_final_g):
    args = dict(locals())
    w = {k: args[k] for k in _WEIGHTS}
    m = {k: args["m_" + k] for k in _WEIGHTS}
    v = {k: args["v_" + k] for k in _WEIGHTS}
    for d in (w, m, v):
        for k in _WEIGHTS:
            if k != "norm_final_g":
                d[k] = d[k][0]
    me = _lin(_me())
    me1 = me.reshape(1).astype(jnp.int32)

    order = jnp.bitwise_xor(me, jnp.array(_PASS_FLIPS, jnp.int32)).astype(jnp.int32)

    sq_err, dx, dw, small = _local_step(x[0], loss_target[0], w, me1, order)

    grads, delta, new_m, new_v = {}, {}, {}, {}
    for k in _BIG:
        own, recv = dw[k]
        grads[k], delta[k], new_m[k], new_v[k] = _reduce_adamw_call("adamw_" + k, me1, own, recv, w[k], m[k], v[k])

    gsum = _small_allreduce_call(_pack_small(small, small["conv_w"], sq_err))
    loss = (0.5 / D) * jnp.sum(gsum[14])
    zc = jnp.zeros((4, D), F32)
    d_s, m_s, v_s = _adamw_call("adamw_small", _pack_small(w, zc), gsum, _pack_small(m, zc), _pack_small(v, zc))
    gs, ds, ms, vs = _unpack_small(gsum), _unpack_small(d_s), _unpack_small(m_s), _unpack_small(v_s)
    g_cw = lax.dynamic_slice(gs["conv_w"], (0, me * 128), (4, 128))
    ds["conv_w"], ms["conv_w"], vs["conv_w"] = _adamw_call("adamw_conv_w", w["conv_w"], g_cw, m["conv_w"], v["conv_w"])
    gs["conv_w"] = g_cw
    for k in _WEIGHTS:
        if k not in _BIG:
            shp = w[k].shape
            grads[k], delta[k], new_m[k], new_v[k] = (t[k].reshape(shp) for t in (gs, ds, ms, vs))

    def lift(t, k):
        return t[k] if k == "norm_final_g" else t[k][None]

    outs = [loss, dx[None]]
    for t in (grads, delta, new_m, new_v):
        outs += [lift(t, k) for k in _WEIGHTS]
    return tuple(outs)
```

```python
import functools

import jax
import jax.numpy as jnp
from jax import lax
from jax.experimental import pallas as pl
from jax.experimental.pallas import tpu as pltpu

F32 = jnp.float32
BF = jnp.bfloat16

D = 1024
NDEV = 8
EPS = 1e-6
RG_C = 8.0
CHUNK = 128
NGRP = 8
GW = 128
NQ = 4
QW = 256
RC = 16
SMALL_ROWS = 320
SMALL_PER = SMALL_ROWS // NDEV

ADAM_LR = 0.001
ADAM_B1 = 0.9
ADAM_B2 = 0.999
ADAM_EPS = 1e-08
ADAM_WD = 0.01
ADAM_STEP = 10

VMEM_LIMIT = 60 * 1024 * 1024

MESH = pl.DeviceIdType.MESH


def _rows(n, fn, unroll=2, rc=RC):
    def body(i, c):
        fn(pl.multiple_of(i * rc, rc))
        return c
    lax.fori_loop(0, n // rc, body, 0, unroll=unroll)


def _fold(v):
    return jnp.sum(v.reshape(v.shape[0] // RC, RC, v.shape[1]), axis=0)


def _dot(a, b):
    return jnp.dot(a, b, preferred_element_type=F32)


def _dot_nt(a, b):
    return lax.dot_general(a, b, (((1,), (1,)), ((), ())), preferred_element_type=F32)


def _dot_tn(a, b):
    return lax.dot_general(a, b, (((0,), (0,)), ((), ())), preferred_element_type=F32)


_GC = 0.7978845608028654
_GK = 0.044715


def _gelu(x):
    t = jnp.tanh(_GC * (x + _GK * (x * x * x)))
    return x * (0.5 * (1.0 + t))


def _gelu_grad(x):
    x2 = x * x
    t = jnp.tanh(_GC * (x + _GK * (x2 * x)))
    cdf = 0.5 * (1.0 + t)
    dg = cdf + (0.5 * x) * (1.0 - t * t) * (_GC * (1.0 + (3.0 * _GK) * x2))
    return x * cdf, dg


def _sigmoid(x):
    return jax.nn.sigmoid(x)


def _log1p(e):
    u = 1.0 + e
    d = u - 1.0
    return jnp.where(d == 0.0, e, jnp.log(u) * (e / jnp.where(d == 0.0, 1.0, d)))


def _softplus(z):
    return jnp.maximum(z, 0.0) + _log1p(jnp.exp(-jnp.abs(z)))


def _neg_expm1(z):
    u = jnp.exp(z)
    lu = jnp.log(u)
    k = (1.0 - u) * (z / jnp.where(lu == 0.0, 1.0, lu))
    small = jnp.where(lu == 0.0, -z, k)
    return jnp.where(z > -0.5, small, 1.0 - u)


def _shift_back(prev8, cur, j):
    cat = jnp.concatenate([prev8, cur], axis=0)
    return pltpu.roll(cat, j, 0)[8:8 + cur.shape[0]]


def _shift_fwd(cur, next8, j):
    cat = jnp.concatenate([cur, next8], axis=0)
    n = cat.shape[0]
    return pltpu.roll(cat, n - j, 0)[0:cur.shape[0]]


def _const_spec(shape):
    nd = len(shape)
    return pl.BlockSpec(shape, lambda *_: (0,) * nd, pipeline_mode=pl.Buffered(1))


def _params(sem):
    return pltpu.CompilerParams(dimension_semantics=sem, vmem_limit_bytes=VMEM_LIMIT)


TM_PROJ = 1024
TM_MIX = 256
TM_FFN = 256
TM_DX = 512
TS_DW = 4096


_CHIPS = (4, 2, 6)
_PASS_FLIPS = (0, 1, 4, 2, 6, 5, 3, 7)


class _Gather:
    def __init__(self, ins, outs, send_sems, recv_sems, local_sems):
        self.ins, self.outs = ins, outs
        self.send_sems, self.recv_sems, self.local_sems = send_sems, recv_sems, local_sems
        self.me = _me()
        self.sibling = _flip(self.me, 1)

    def _copy(self, a, kind, block, to, src=None):
        dst = self.outs[a].at[_lin(block)]
        return pltpu.make_async_remote_copy(
            src_ref=dst if src is None else src, dst_ref=dst,
            send_sem=self.send_sems.at[a, kind], recv_sem=self.recv_sems.at[a, kind],
            device_id=to, device_id_type=MESH)

    def _local(self, a):
        return pltpu.make_async_copy(self.ins[a], self.outs[a].at[_lin(self.me)], self.local_sems.at[a])

    def start(self):
        for a in range(len(self.ins)):
            self._local(a).start()
            self._copy(a, 0, self.me, self.sibling, src=self.ins[a]).start()
            for j, f in enumerate(_CHIPS):
                self._copy(a, 1 + j, self.me, _flip(self.me, f), src=self.ins[a]).start()

    def forward(self):
        for j, f in enumerate(_CHIPS):
            for a in range(len(self.ins)):
                self._copy(a, 1 + j, _flip(self.me, f), self.me).wait_recv()
                self._copy(a, 4 + j, _flip(self.me, f), self.sibling).start()

    def finish(self):
        for a in range(len(self.ins)):
            self._copy(a, 0, self.sibling, self.me).wait_recv()
            for j, f in enumerate(_CHIPS):
                self._copy(a, 4 + j, _flip(self.me, f | 1), self.me).wait_recv()
            self._copy(a, 0, self.me, self.sibling, src=self.ins[a]).wait_send()
            for j, f in enumerate(_CHIPS):
                self._copy(a, 1 + j, self.me, _flip(self.me, f), src=self.ins[a]).wait_send()
                self._copy(a, 4 + j, _flip(self.me, f), self.sibling).wait_send()
            self._local(a).wait()


class _Exchange:
    def __init__(self, ins, outs, send_sems, recv_sems, local_sems):
        self.ins, self.outs = ins, outs
        self.send_sems, self.recv_sems, self.local_sems = send_sems, recv_sems, local_sems
        self.me = _me()

    def _copy(self, a, r, outgoing):
        peer = _flip(self.me, r)
        src, dst = (peer, self.me) if outgoing else (self.me, peer)
        return pltpu.make_async_remote_copy(
            src_ref=self.ins[a].at[_lin(src)], dst_ref=self.outs[a].at[_lin(dst)],
            send_sem=self.send_sems.at[a, r - 1], recv_sem=self.recv_sems.at[a, r - 1],
            device_id=peer, device_id_type=MESH)

    def _local(self, a):
        mi = _lin(self.me)
        return pltpu.make_async_copy(self.ins[a].at[mi], self.outs[a].at[mi], self.local_sems.at[a])

    def start(self):
        for a in range(len(self.ins)):
            self._local(a).start()
        for r in range(1, NDEV):
            for a in range(len(self.ins)):
                self._copy(a, r, True).start()

    def finish(self):
        for r in range(1, NDEV):
            for a in range(len(self.ins)):
                self._copy(a, r, False).wait_recv()
        for r in range(1, NDEV):
            for a in range(len(self.ins)):
                self._copy(a, r, True).wait_send()
        for a in range(len(self.ins)):
            self._local(a).wait()


def _gather_sems(n):
    return [pltpu.SemaphoreType.DMA((n, 7)), pltpu.SemaphoreType.DMA((n, 7)), pltpu.SemaphoreType.DMA((n,))]


def _proj_gather_call(x, gmix, w_shard, order, extras):
    s = x.shape[0]
    tm = min(TM_PROJ, s)
    nt = s // tm
    wb = w_shard.shape[1]
    n = len(extras)

    def body(order_ref, x_ref, g_ref, wsh_ref, *rest):
        ex_in = rest[:n]
        proj_ref, h1_hbm, wall_hbm = rest[n:n + 3]
        ex_out = rest[n + 3:2 * n + 3]
        wall, hc, ws_send, ws_recv, ex_send, ex_recv, ex_local, out_sems = rest[2 * n + 3:]
        k = pl.program_id(0)
        i = pl.program_id(1)
        me = _me()
        sibling = _flip(me, 1)
        gather = _Gather(ex_in, ex_out, ex_send, ex_recv, ex_local)

        def wcopy(kind, block, to):
            ref = wall.at[_lin(block)]
            return pltpu.make_async_remote_copy(
                src_ref=ref, dst_ref=ref, send_sem=ws_send.at[kind], recv_sem=ws_recv.at[kind],
                device_id=to, device_id_type=MESH)

        head = i == 0

        @pl.when(head & (k == 0))
        def _():
            wall[_lin(me)] = wsh_ref[...]
            wcopy(0, me, sibling).start()
            for j, f in enumerate(_CHIPS):
                wcopy(1 + j, me, _flip(me, f)).start()
            gather.start()

        @pl.when(head & (k == 1))
        def _():
            wcopy(0, sibling, me).wait_recv()

        for j, f in enumerate(_CHIPS):
            @pl.when(head & (k == 2 + j))
            def _(j=j, f=f):
                wcopy(1 + j, _flip(me, f), me).wait_recv()
                wcopy(4 + j, _flip(me, f), sibling).start()

            @pl.when(head & (k == 5 + j))
            def _(j=j, f=f):
                wcopy(4 + j, _flip(me, f | 1), me).wait_recv()

        pl.when(head & (k == 5))(gather.forward)

        base = pl.multiple_of(i * tm, tm)

        @pl.when(k == 0)
        def _():
            g = g_ref[...]

            def norm(r0):
                xx = x_ref[pl.ds(r0, RC), :]
                r = lax.rsqrt(jnp.mean(xx * xx, axis=-1, keepdims=True) + EPS)
                hc[pl.ds(base + r0, RC), :] = ((xx * r) * g).astype(BF)
            _rows(tm, norm, unroll=4)

        proj_ref[...] = _dot(hc[pl.ds(base, tm), :], wall[order_ref[k]])

        @pl.when((k == NDEV - 1) & (i == nt - 1))
        def _():
            out_w = pltpu.make_async_copy(wall, wall_hbm, out_sems.at[0])
            out_h = pltpu.make_async_copy(hc, h1_hbm, out_sems.at[1])
            out_w.start()
            out_h.start()
            wcopy(0, me, sibling).wait_send()
            for j, f in enumerate(_CHIPS):
                wcopy(1 + j, me, _flip(me, f)).wait_send()
                wcopy(4 + j, _flip(me, f), sibling).wait_send()
            gather.finish()
            out_w.wait()
            out_h.wait()

    any_spec = pl.BlockSpec(memory_space=pl.ANY)
    grid_spec = pltpu.PrefetchScalarGridSpec(
        num_scalar_prefetch=1, grid=(NDEV, nt),
        in_specs=[pl.BlockSpec((tm, D), lambda k, i, o: (jnp.where(k == 0, i, nt - 1), 0)),
                  pl.BlockSpec((1, D), lambda k, i, o: (0, 0)),
                  pl.BlockSpec(w_shard.shape, lambda k, i, o: (0, 0))] + [any_spec] * n,
        out_specs=[pl.BlockSpec((tm, wb), lambda k, i, o: (i, o[k])), any_spec, any_spec] + [any_spec] * n,
        scratch_shapes=[pltpu.VMEM((NDEV,) + w_shard.shape, BF), pltpu.VMEM((s, D), BF),
                        pltpu.SemaphoreType.DMA((7,)), pltpu.SemaphoreType.DMA((7,))] + _gather_sems(n)
        + [pltpu.SemaphoreType.DMA((2,))])
    outs = pl.pallas_call(
        body, name="proj_gather", grid_spec=grid_spec,
        out_shape=[jax.ShapeDtypeStruct((s, NDEV * wb), F32), jax.ShapeDtypeStruct((s, D), BF),
                   jax.ShapeDtypeStruct((NDEV,) + w_shard.shape, BF)]
        + [jax.ShapeDtypeStruct((NDEV,) + e.shape, e.dtype) for e in extras],
        compiler_params=_params(("arbitrary", "arbitrary")),
    )(order, x, gmix, w_shard, *extras)
    return outs[0], outs[1], outs[2], outs[3:]


def _conv_tile(rx, prev8, cw_ref, cb):
    xc = cb + cw_ref[3:4, :] * rx
    for j in (1, 2, 3):
        xc = xc + cw_ref[3 - j:4 - j, :] * _shift_back(prev8, rx, j)
    return xc


def _gate_tile(xcb, wax_ref, ba, bx, sp, q):
    cs = slice(q * QW, (q + 1) * QW)
    z = _dot(xcb[:, cs], wax_ref[q])
    r = _sigmoid(z[:, 0:QW] + ba[:, cs])
    ig = _sigmoid(z[:, QW:2 * QW] + bx[:, cs])
    return r, ig, (-RG_C * r) * sp[:, cs]


def _scan_fwd(a_s, b_s, out_ref, h0, n):
    rowi = lax.broadcasted_iota(jnp.int32, (8, D), 0)

    def block(t, h):
        rows = pl.ds(pl.multiple_of(t * 8, 8), 8)
        a = a_s[rows, :]
        b = b_s[rows, :]
        for d in (1, 2, 4):
            m = rowi >= d
            b = jnp.where(m, a * pltpu.roll(b, d, 0) + b, b)
            a = jnp.where(m, a * pltpu.roll(a, d, 0), a)
        hb = b + a * h
        out_ref[rows, :] = hb
        return hb[7:8, :]
    return lax.fori_loop(0, n // 8, block, h0, unroll=4)


def _scan_bwd(a_s, lm_s, c0, n):
    rowi = lax.broadcasted_iota(jnp.int32, (8, D), 0)
    nblk = n // 8

    def block(k, cin):
        rows = pl.ds(pl.multiple_of((nblk - 1 - k) * 8, 8), 8)
        a = a_s[rows, :]
        dh = lm_s[rows, :]
        b = a * dh
        for d in (1, 2, 4):
            m = rowi < 8 - d
            b = jnp.where(m, a * pltpu.roll(b, 8 - d, 0) + b, b)
            a = jnp.where(m, a * pltpu.roll(a, 8 - d, 0), a)
        mu = b + a * cin
        lm_s[rows, :] = dh + jnp.where(rowi < 7, pltpu.roll(mu, 7, 0), cin)
        return mu[0:1, :]
    return lax.fori_loop(0, nblk, block, c0, unroll=4)


def _mixer_fwd_call(proj, x, cw, cb, ba, bx, lam, lng, lnb, wax, wtr, bias, wpa, wpb, wo, shards):
    s = x.shape[0]
    tm = min(TM_MIX, s)
    nt = s // tm
    pw = proj.shape[1]
    n = len(shards)

    def body(proj_ref, x_ref, cw_ref, cb_ref, ba_ref, bx_ref, lam_ref, lng_ref, lnb_ref, wax_ref, wtr_ref,
             bias_ref, wpa_ref, wpb_ref, wo_ref, *rest):
        sh_in = rest[:n]
        h_ref, pa_ref, pb_ref, x1_ref, ya_ref, yb_ref, mg_ref = rest[n:n + 7]
        sh_out = rest[n + 7:2 * n + 7]
        prev_s, a_s, b_s, hc_s, vn_s, mx_s, g_send, g_recv, g_local = rest[2 * n + 7:]
        i = pl.program_id(0)
        gather = _Gather(sh_in, sh_out, g_send, g_recv, g_local)

        @pl.when(i == 0)
        def _():
            gather.start()
            prev_s[...] = jnp.zeros((8, D), F32)
            hc_s[...] = jnp.zeros((8, D), F32)

        pl.when(i == nt // 2)(gather.forward)

        rx = proj_ref[:, 0:D]
        xc = _conv_tile(rx, prev_s[...], cw_ref, cb_ref[...])
        prev_s[...] = rx[tm - 8:tm, :]
        xcb = xc.astype(BF)
        sp = _softplus(-lam_ref[...])
        ba = ba_ref[...]
        bx = bx_ref[...]
        for q in range(NQ):
            cs = slice(q * QW, (q + 1) * QW)
            _, ig, la = _gate_tile(xcb, wax_ref, ba, bx, sp, q)
            a_s[:, cs] = jnp.exp(la)
            b_s[:, cs] = jnp.sqrt(_neg_expm1(2.0 * la)) * (ig * xc[:, cs])

        gv = _gelu(proj_ref[:, 3 * D:4 * D])
        dv = gv - jnp.mean(gv, axis=-1, keepdims=True)
        var = jnp.mean(dv * dv, axis=-1, keepdims=True)
        vn_s[...] = ((dv * lax.rsqrt(var + EPS)) * lng_ref[...] + lnb_ref[...]).astype(BF)
        nc = tm // CHUNK
        for c in range(nc):
            rs = slice(c * CHUNK, (c + 1) * CHUNK)
            for g in range(NGRP):
                cs = slice(g * GW, (g + 1) * GW)
                mx_s[rs, cs] = _dot(wtr_ref[g], vn_s[rs, cs])
        mixed = mx_s[...] + jnp.concatenate([bias_ref[...]] * nc, axis=0)
        yb = (_gelu(proj_ref[:, 2 * D:3 * D]) * mixed).astype(BF)
        yb_ref[...] = yb
        pb_ref[...] = _dot(yb, wpb_ref[...])

        hc_s[0:1, :] = _scan_fwd(a_s, b_s, h_ref, hc_s[0:1, :], tm)

        ya = (_gelu(proj_ref[:, D:2 * D]) * h_ref[...]).astype(BF)
        ya_ref[...] = ya
        pa = _dot(ya, wpa_ref[...])
        pa_ref[...] = pa
        mg = (_sigmoid(proj_ref[:, 4 * D:5 * D]) * pa + _sigmoid(proj_ref[:, 5 * D:6 * D]) * pb_ref[...]).astype(BF)
        mg_ref[...] = mg
        x1_ref[...] = x_ref[...] + _dot(mg, wo_ref[...])

        pl.when(i == nt - 1)(gather.finish)

    tile = lambda w: pl.BlockSpec((tm, w), lambda i: (i, 0))
    vec = _const_spec((1, D))
    any_spec = pl.BlockSpec(memory_space=pl.ANY)
    outs = pl.pallas_call(
        body, name="mixer_fwd", grid=(nt,),
        in_specs=[tile(pw), tile(D), _const_spec((4, D)), vec, vec, vec, vec, vec, vec,
                  _const_spec(wax.shape), _const_spec(wtr.shape), _const_spec(bias.shape),
                  _const_spec((D, D)), _const_spec((D, D)), _const_spec((D, D))] + [any_spec] * n,
        out_specs=[tile(D)] * 7 + [any_spec] * n,
        out_shape=[jax.ShapeDtypeStruct((s, D), F32)] * 4 + [jax.ShapeDtypeStruct((s, D), BF)] * 3
        + [jax.ShapeDtypeStruct((NDEV,) + e.shape, e.dtype) for e in shards],
        scratch_shapes=[pltpu.VMEM((8, D), F32), pltpu.VMEM((tm, D), F32), pltpu.VMEM((tm, D), F32),
                        pltpu.VMEM((8, D), F32), pltpu.VMEM((tm, D), BF), pltpu.VMEM((tm, D), F32)]
        + _gather_sems(n),
        compiler_params=_params(("arbitrary",)),
    )(proj, x, cw, cb, ba, bx, lam, lng, lnb, wax, wtr, bias, wpa, wpb, wo, *shards)
    return outs[:7], outs[7:]


def _ffn_call(x1, tgt, gffn, gfin, wgu, wdn):
    s = x1.shape[0]
    tm = min(TM_FFN, s)
    nt = s // tm
    nh = wdn.shape[0]
    fb = wgu.shape[2]

    def body(x1_ref, tgt_ref, gffn_ref, gfin_ref, wgu_ref, wdn_ref,
             dx1_ref, dx1b_ref, h2_ref, act_ref, dgu_ref, dx2b_ref, acc_ref,
             g_s, u_s, dx2_s, accs):
        i = pl.program_id(0)

        @pl.when(i == 0)
        def _():
            accs[...] = jnp.zeros(accs.shape, F32)

        gffn = gffn_ref[...]
        gfin = gfin_ref[...]

        x1 = x1_ref[...]
        r2 = lax.rsqrt(jnp.mean(x1 * x1, axis=-1, keepdims=True) + EPS)
        xh2 = x1 * r2
        h2 = (xh2 * gffn).astype(BF)
        h2_ref[...] = h2

        for k in range(nh):
            g = _dot(h2, wgu_ref[k])
            u = _dot(h2, wgu_ref[k + nh])
            g_s[k] = g
            u_s[k] = u
            act_ref[k] = ((g * _sigmoid(g)) * u).astype(BF)

        x2 = x1
        for k in range(nh):
            x2 = x2 + _dot(act_ref[k], wdn_ref[k])

        r3 = lax.rsqrt(jnp.mean(x2 * x2, axis=-1, keepdims=True) + EPS)
        xh = x2 * r3
        err = xh * gfin - tgt_ref[...]
        accs[2] += _fold(err * err)
        dy = err * (1.0 / D)
        accs[1] += _fold(dy * xh)
        dxh = dy * gfin
        dx2 = r3 * (dxh - xh * jnp.mean(dxh * xh, axis=-1, keepdims=True))
        dx2_s[...] = dx2
        dx2b = dx2.astype(BF)
        dx2b_ref[...] = dx2b

        for k in range(nh):
            da = _dot_nt(dx2b, wdn_ref[k])
            g = g_s[k]
            sg = _sigmoid(g)
            dgu_ref[k] = ((da * u_s[k]) * (sg * (1.0 + g * (1.0 - sg)))).astype(BF)
            dgu_ref[k + nh] = (da * (g * sg)).astype(BF)

        dh2 = _dot_nt(dgu_ref[0], wgu_ref[0])
        for k in range(1, 2 * nh):
            dh2 = dh2 + _dot_nt(dgu_ref[k], wgu_ref[k])

        accs[0] += _fold(dh2 * xh2)
        dxh = dh2 * gffn
        dx1 = dx2_s[...] + r2 * (dxh - xh2 * jnp.mean(dxh * xh2, axis=-1, keepdims=True))
        dx1_ref[...] = dx1
        dx1b_ref[...] = dx1.astype(BF)

        @pl.when(i == nt - 1)
        def _():
            acc_ref[...] = jnp.zeros((8, D), F32)
            for j in range(3):
                acc_ref[j:j + 1, :] = jnp.sum(accs[j], axis=0, keepdims=True)

    tile = lambda w: pl.BlockSpec((tm, w), lambda i: (i, 0))
    vec = _const_spec((1, D))
    return pl.pallas_call(
        body, name="ffn", grid=(nt,),
        in_specs=[tile(D), tile(D), vec, vec, _const_spec(wgu.shape), _const_spec(wdn.shape)],
        out_specs=[tile(D), tile(D), tile(D),
                   pl.BlockSpec((nh, tm, fb), lambda i: (0, i, 0)),
                   pl.BlockSpec((2 * nh, tm, fb), lambda i: (0, i, 0)),
                   tile(D), pl.BlockSpec((8, D), lambda i: (0, 0))],
        out_shape=[jax.ShapeDtypeStruct((s, D), F32), jax.ShapeDtypeStruct((s, D), BF),
                   jax.ShapeDtypeStruct((s, D), BF), jax.ShapeDtypeStruct((nh, s, fb), BF),
                   jax.ShapeDtypeStruct((2 * nh, s, fb), BF), jax.ShapeDtypeStruct((s, D), BF),
                   jax.ShapeDtypeStruct((8, D), F32)],
        scratch_shapes=[pltpu.VMEM((nh, tm, fb), F32), pltpu.VMEM((nh, tm, fb), F32), pltpu.VMEM((tm, D), F32),
                        pltpu.VMEM((3, RC, D), F32)],
        compiler_params=_params(("arbitrary",)),
    )(x1, tgt, gffn, gfin, wgu, wdn)


def _mixer_bwd_pre_call(dx1b, proj, pa, pb, lng, lnb, wtr, wtrt, bias, wpa, wpb, wo, parts):
    s = dx1b.shape[0]
    tm = min(TM_MIX, s)
    nt = s // tm
    pw = proj.shape[1]
    n = len(parts)

    def body(dx1b_ref, uv_ref, gg_ref, pa_ref, pb_ref, lng_ref, lnb_ref, wtr_ref, wtrt_ref, bias_ref,
             wpa_ref, wpb_ref, wo_ref, *rest):
        ex_in = rest[:n]
        dproj_ref, dya_ref, dpa_ref, dpb_ref, vec_ref, dws_ref, dbs_ref = rest[n:n + 7]
        ex_out = rest[n + 7:2 * n + 7]
        vn_s, mx_s, dmx_s, dvn_s, accs, dbs_s, e_send, e_recv, e_local = rest[2 * n + 7:]
        i = pl.program_id(0)
        exchange = _Exchange(ex_in, ex_out, e_send, e_recv, e_local)

        @pl.when(i == 0)
        def _():
            exchange.start()
            accs[...] = jnp.zeros(accs.shape, F32)
            dbs_s[...] = jnp.zeros(dbs_s.shape, F32)
            dws_ref[...] = jnp.zeros(dws_ref.shape, F32)

        dm = _dot_nt(dx1b_ref[...], wo_ref[...])
        sa = _sigmoid(gg_ref[:, 0:D])
        sb = _sigmoid(gg_ref[:, D:2 * D])
        dpa = dm * sa
        dpb = dm * sb
        dpab = dpa.astype(BF)
        dpbb = dpb.astype(BF)
        dpa_ref[...] = dpab
        dpb_ref[...] = dpbb
        dproj_ref[:, 2 * D:3 * D] = ((dpa * pa_ref[...]) * (1.0 - sa)).astype(BF)
        dproj_ref[:, 3 * D:4 * D] = ((dpb * pb_ref[...]) * (1.0 - sb)).astype(BF)

        dya_ref[...] = _dot_nt(dpab, wpa_ref[...])
        dyb = _dot_nt(dpbb, wpb_ref[...])

        lng = lng_ref[...]
        gv, dgelu_v = _gelu_grad(uv_ref[:, D:2 * D])
        dv = gv - jnp.mean(gv, axis=-1, keepdims=True)
        rstd = lax.rsqrt(jnp.mean(dv * dv, axis=-1, keepdims=True) + EPS)
        xh = dv * rstd
        vn_s[...] = (xh * lng + lnb_ref[...]).astype(BF)

        nc = tm // CHUNK
        for c in range(nc):
            rs = slice(c * CHUNK, (c + 1) * CHUNK)
            for g in range(NGRP):
                cs = slice(g * GW, (g + 1) * GW)
                mx_s[rs, cs] = _dot(wtr_ref[g], vn_s[rs, cs])

        gu, dgelu_u = _gelu_grad(uv_ref[:, 0:D])
        mixed = mx_s[...] + jnp.concatenate([bias_ref[...]] * nc, axis=0)
        dproj_ref[:, 0:D] = ((dyb * mixed) * dgelu_u).astype(BF)
        dmx = dyb * gu
        dmx_s[...] = dmx.astype(BF)
        dbs_s[...] += jnp.sum(dmx.reshape(nc, CHUNK, D), axis=0)

        for c in range(nc):
            rs = slice(c * CHUNK, (c + 1) * CHUNK)
            for g in range(NGRP):
                cs = slice(g * GW, (g + 1) * GW)
                dvn_s[rs, cs] = _dot(wtrt_ref[g], dmx_s[rs, cs])
                dws_ref[g] += _dot_nt(dmx_s[rs, cs], vn_s[rs, cs])

        dvn = dvn_s[...]
        accs[0] += _fold(dvn * xh)
        accs[1] += _fold(dvn)
        dxh = dvn * lng
        m1 = jnp.mean(dxh, axis=-1, keepdims=True)
        m2 = jnp.mean(dxh * xh, axis=-1, keepdims=True)
        dproj_ref[:, D:2 * D] = ((rstd * (dxh - m1 - xh * m2)) * dgelu_v).astype(BF)

        @pl.when(i == nt - 1)
        def _():
            vec_ref[...] = jnp.zeros((8, D), F32)
            for j in range(2):
                vec_ref[j:j + 1, :] = jnp.sum(accs[j], axis=0, keepdims=True)
            row = lax.broadcasted_iota(jnp.int32, (CHUNK, CHUNK), 0)
            col = lax.broadcasted_iota(jnp.int32, (CHUNK, CHUNK), 1)
            for g in range(NGRP):
                dws_ref[g] = jnp.where(row >= col, dws_ref[g], 0.0)
                gs = jnp.sum(dbs_s[:, g * GW:(g + 1) * GW], axis=1, keepdims=True)
                dbs_ref[:, g * GW:(g + 1) * GW] = jnp.broadcast_to(gs, (CHUNK, GW))
            exchange.finish()

    tile = lambda w: pl.BlockSpec((tm, w), lambda i: (i, 0))
    vec = _const_spec((1, D))
    any_spec = pl.BlockSpec(memory_space=pl.ANY)
    outs = pl.pallas_call(
        body, name="mixer_bwd_pre", grid=(nt,),
        in_specs=[tile(D), pl.BlockSpec((tm, 2 * D), lambda i: (i, 1)), pl.BlockSpec((tm, 2 * D), lambda i: (i, 2)),
                  tile(D), tile(D), vec, vec, _const_spec(wtr.shape), _const_spec(wtrt.shape),
                  _const_spec(bias.shape), _const_spec((D, D)), _const_spec((D, D)), _const_spec((D, D))]
        + [any_spec] * n,
        out_specs=[tile(4 * D), tile(D), tile(D), tile(D), pl.BlockSpec((8, D), lambda i: (0, 0)),
                   pl.BlockSpec((NGRP, CHUNK, CHUNK), lambda i: (0, 0, 0)),
                   pl.BlockSpec((CHUNK, D), lambda i: (0, 0))] + [any_spec] * n,
        out_shape=[jax.ShapeDtypeStruct((s, 4 * D), BF), jax.ShapeDtypeStruct((s, D), F32),
                   jax.ShapeDtypeStruct((s, D), BF), jax.ShapeDtypeStruct((s, D), BF),
                   jax.ShapeDtypeStruct((8, D), F32), jax.ShapeDtypeStruct((NGRP, CHUNK, CHUNK), F32),
                   jax.ShapeDtypeStruct((CHUNK, D), F32)]
        + [jax.ShapeDtypeStruct(p.shape, p.dtype) for p in parts],
        scratch_shapes=[pltpu.VMEM((tm, D), BF), pltpu.VMEM((tm, D), F32), pltpu.VMEM((tm, D), BF),
                        pltpu.VMEM((tm, D), F32), pltpu.VMEM((2, RC, D), F32), pltpu.VMEM((CHUNK, D), F32)]
        + _gather_sems(n),
        compiler_params=_params(("arbitrary",)),
    )(dx1b, proj, proj, pa, pb, lng, lnb, wtr, wtrt, bias, wpa, wpb, wo, *parts)
    return outs[:7], outs[7:]


def _mixer_bwd_seq_call(dprojb, dya, proj, h, x, dx1, gmix, w_in_all, cw, cb, ba, bx, lam, wax, parts):
    s = dya.shape[0]
    tm = min(TM_MIX, s)
    nt = s // tm
    tb = tm // 8
    n = len(parts)
    nb, _, wb = w_in_all.shape

    def body(dpb_ref, dya_ref, xg_ref, xh8_ref, h_ref, hh8_ref, x_ref, dx1_ref, gmix_ref, win_ref,
             cw_ref, cb_ref, ba_ref, bx_ref, lam_ref, wax_ref, *rest):
        ex_in = rest[:n]
        dpa_ref, dx_ref, vec_ref, dwax_ref = rest[n:n + 4]
        ex_out = rest[n + 4:2 * n + 4]
        xc_s, xcb_s, r_s, ig_s, a_s, lm_s, dh_s, dxc_s, c_s, accs, e_send, e_recv, e_local = rest[2 * n + 4:]
        i = pl.program_id(0)
        exchange = _Exchange(ex_in, ex_out, e_send, e_recv, e_local)

        @pl.when(i == 0)
        def _():
            exchange.start()
            accs[...] = jnp.zeros(accs.shape, F32)
            dwax_ref[...] = jnp.zeros(dwax_ref.shape, F32)
            c_s[...] = jnp.zeros((8, D), F32)
            dxc_s[tm:tm + 8, :] = jnp.zeros((8, D), F32)

        first_tile = i == nt - 1
        prev8 = jnp.where(first_tile, 0.0, xh8_ref[...])
        hprev8 = jnp.where(first_tile, 0.0, hh8_ref[...])
        rx = xg_ref[:, 0:D]
        xc = _conv_tile(rx, prev8, cw_ref, cb_ref[...])
        xc_s[...] = xc
        xcb = xc.astype(BF)
        xcb_s[...] = xcb
        lamv = lam_ref[...]
        sp = _softplus(-lamv)
        ba = ba_ref[...]
        bx = bx_ref[...]
        for q in range(NQ):
            cs = slice(q * QW, (q + 1) * QW)
            r, ig, la = _gate_tile(xcb, wax_ref, ba, bx, sp, q)
            r_s[:, cs] = r
            ig_s[:, cs] = ig
            a_s[:, cs] = jnp.exp(la)
        hv = h_ref[...]
        g, dg = _gelu_grad(xg_ref[:, D:2 * D])
        dya = dya_ref[...]
        lm_s[...] = dya * g
        drg = ((dya * hv) * dg).astype(BF)
        dpa_ref[:, D:2 * D] = drg
        dpa_ref[:, 2 * D:6 * D] = dpb_ref[...]

        def dpb_block(k):
            return _dot_nt(dpb_ref[:, k * wb - 2 * D:(k + 1) * wb - 2 * D], win_ref[k])
        dh = (_dot_nt(drg[:, 0:2 * wb - D], win_ref[1, :, D - wb:wb])
              + _dot_nt(drg[:, 2 * wb - D:D], win_ref[2, :, 0:2 * D - 2 * wb])
              + _dot_nt(dpb_ref[:, 0:3 * wb - 2 * D], win_ref[2, :, 2 * D - 2 * wb:wb]))
        for k in range(3, 6):
            dh = dh + dpb_block(k)
        dh_s[...] = dh

        c_s[0:1, :] = _scan_bwd(a_s, lm_s, c_s[0:1, :], tm)

        hprev = _shift_back(hprev8, h_ref[...], 1)
        for q in range(NQ):
            cs = slice(q * QW, (q + 1) * QW)
            r = r_s[:, cs]
            ig = ig_s[:, cs]
            a = a_s[:, cs]
            la = (-RG_C * r) * sp[:, cs]
            m = jnp.sqrt(_neg_expm1(2.0 * la))
            lm = lm_s[:, cs]
            xq = xc_s[:, cs]
            dixc = lm * m
            dla = (lm * hprev[:, cs]) * a - ((lm * (ig * xq)) * (a * a)) / m
            accs[3, :, cs] += _fold(dla * r)
            dza = (dla * (-RG_C * sp[:, cs])) * (r * (1.0 - r))
            dzx = (dixc * xq) * (ig * (1.0 - ig))
            accs[1, :, cs] += _fold(dza)
            accs[2, :, cs] += _fold(dzx)
            dz = jnp.concatenate([dza, dzx], axis=1).astype(BF)
            dxc_s[0:tm, cs] = dixc * ig + _dot_nt(dz, wax_ref[q])
            dwax_ref[q] += _dot_tn(xcb_s[:, cs], dz)

        cur = dxc_s[0:tm, :]
        nx = dxc_s[tm:tm + 8, :]
        drx = cw_ref[3:4, :] * cur
        for j in (1, 2, 3):
            drx = drx + cw_ref[3 - j:4 - j, :] * _shift_fwd(cur, nx, j)
        accs[0] += _fold(cur)
        rx = xg_ref[:, 0:D]
        accs[7] += _fold(cur * rx)
        for j in (1, 2, 3):
            accs[7 - j] += _fold(cur * _shift_back(prev8, rx, j))
        dxc_s[tm:tm + 8, :] = cur[0:8, :]

        drxb = drx.astype(BF)
        dpa_ref[:, 0:D] = drxb
        dh = dh_s[...] + _dot_nt(drxb[:, 0:wb], win_ref[0]) + _dot_nt(drxb[:, wb:D], win_ref[1, :, 0:D - wb])
        for k in range(6, nb):
            dh = dh + dpb_block(k)
        xx = x_ref[...]
        rn = lax.rsqrt(jnp.mean(xx * xx, axis=-1, keepdims=True) + EPS)
        xhn = xx * rn
        accs[8] += _fold(dh * xhn)
        dxh = dh * gmix_ref[...]
        dx_ref[...] = dx1_ref[...] + rn * (dxh - xhn * jnp.mean(dxh * xhn, axis=-1, keepdims=True))

        @pl.when(i == nt - 1)
        def _():
            vec_ref[...] = jnp.zeros((16, D), F32)
            for j in range(9):
                vec_ref[j:j + 1, :] = jnp.sum(accs[j], axis=0, keepdims=True)
            vec_ref[3:4, :] = vec_ref[3:4, :] * (RG_C * _sigmoid(-lamv))
            exchange.finish()

    rev = lambda w: pl.BlockSpec((tm, w), lambda i: (nt - 1 - i, 0))
    halo = pl.BlockSpec((8, D), lambda i: (jnp.maximum((nt - 1 - i) * tb - 1, 0), 0))
    vec = _const_spec((1, D))
    any_spec = pl.BlockSpec(memory_space=pl.ANY)
    outs = pl.pallas_call(
        body, name="mixer_bwd_seq", grid=(nt,),
        in_specs=[rev(4 * D), rev(D), rev(2 * D), halo, rev(D), halo, rev(D), rev(D), vec,
                  _const_spec(w_in_all.shape),
                  _const_spec((4, D)), vec, vec, vec, vec, _const_spec(wax.shape)] + [any_spec] * n,
        out_specs=[rev(6 * D), rev(D), pl.BlockSpec((16, D), lambda i: (0, 0)),
                   pl.BlockSpec((NQ, QW, 2 * QW), lambda i: (0, 0, 0))] + [any_spec] * n,
        out_shape=[jax.ShapeDtypeStruct((s, 6 * D), BF), jax.ShapeDtypeStruct((s, D), F32),
                   jax.ShapeDtypeStruct((16, D), F32), jax.ShapeDtypeStruct((NQ, QW, 2 * QW), F32)]
        + [jax.ShapeDtypeStruct(p.shape, p.dtype) for p in parts],
        scratch_shapes=[pltpu.VMEM((tm, D), F32), pltpu.VMEM((tm, D), BF), pltpu.VMEM((tm, D), F32),
                        pltpu.VMEM((tm, D), F32), pltpu.VMEM((tm, D), F32), pltpu.VMEM((tm, D), F32),
                        pltpu.VMEM((tm, D), F32),
                        pltpu.VMEM((tm + 8, D), F32), pltpu.VMEM((8, D), F32), pltpu.VMEM((9, RC, D), F32)]
        + _gather_sems(n),
        compiler_params=_params(("arbitrary",)),
    )(dprojb, dya, proj, proj, h, h, x, dx1, gmix, w_in_all, cw, cb, ba, bx, lam, wax, *parts)
    return outs[:4], outs[4:]


def _device_of(d):
    return (d // 4, lax.rem(d // 2, 2), lax.rem(d, 2))


def _dw_exchange_call(name, me, a, b, a_spec, b_spec, split, k1, n1, s, small):
    nb = NDEV // split
    r = k1 // split
    ts = min(TS_DW, s)
    ns = s // ts

    def slab(i, me_ref):
        return lax.rem(me_ref[0] // split + 1 + i, nb)

    def body(me_ref, a_ref, b_ref, g_ref, own_ref, recv_ref, gsum_ref, acc, sbuf, send_sems, recv_sems, local_sem,
             rbuf, gacc, send1, recv1, send2, recv2):
        i = pl.program_id(0)
        j = pl.program_id(1)
        me = _lin(_me())
        allreduce = _SmallAllReduce(g_ref, gacc, rbuf, send1, recv1, send2, recv2)
        pl.when((i == 0) & (j == 0))(allreduce.scatter)
        pl.when((i == nb // 2) & (j == 0))(allreduce.reduce)
        p = _dot_tn(a_ref[...], b_ref[...])

        @pl.when(j == 0)
        def _():
            acc[...] = p

        @pl.when(j > 0)
        def _():
            acc[...] += p

        def send(step, h):
            d = slab(step, me_ref) * split + h
            cp = pltpu.make_async_remote_copy(
                src_ref=sbuf.at[lax.rem(step, 2), pl.ds(h * r, r)], dst_ref=recv_ref.at[me],
                send_sem=send_sems.at[d], recv_sem=recv_sems.at[me],
                device_id=_device_of(d), device_id_type=MESH)
            return cp, d != me

        def drain(step):
            for h in range(split):
                cp, sent = send(step, h)
                pl.when(sent)(cp.wait_send)

        @pl.when(j == ns - 1)
        def _():
            pl.when(i >= 2)(lambda: drain(i - 2))
            sbuf[lax.rem(i, 2)] = acc[...].astype(BF)
            for h in range(split):
                cp, sent = send(i, h)
                pl.when(sent)(cp.start)

            @pl.when(i == nb - 1)
            def _():
                off = pl.multiple_of(lax.rem(me, split) * r, RC)
                own_ref[...] = acc[pl.ds(off, r), :]
                mine = pltpu.make_async_copy(sbuf.at[lax.rem(i, 2), pl.ds(off, r)], recv_ref.at[me], local_sem)
                mine.start()
                drain(i - 1)
                drain(i)
                for d in range(NDEV):
                    wait = pltpu.make_async_remote_copy(
                        src_ref=sbuf.at[0, pl.ds(0, r)], dst_ref=recv_ref.at[d],
                        send_sem=send_sems.at[d], recv_sem=recv_sems.at[d],
                        device_id=_device_of(d), device_id_type=MESH).wait_recv
                    pl.when(d != me)(wait)
                mine.wait()
                allreduce.finish()
                gsum_ref[...] = gacc[...]

    vmem = pl.BlockSpec(memory_space=pltpu.VMEM)
    grid_spec = pltpu.PrefetchScalarGridSpec(
        num_scalar_prefetch=1, grid=(nb, ns),
        in_specs=[a_spec(ts, slab), b_spec(ts, slab), vmem],
        out_specs=[pl.BlockSpec((r, n1), lambda i, j, me_ref: (0, 0)), pl.BlockSpec(memory_space=pl.ANY), vmem],
        scratch_shapes=[pltpu.VMEM((k1, n1), F32), pltpu.VMEM((2, k1, n1), BF),
                        pltpu.SemaphoreType.DMA((NDEV,)), pltpu.SemaphoreType.DMA((NDEV,)),
                        pltpu.SemaphoreType.DMA(()), pltpu.VMEM((NDEV, SMALL_PER, D), F32),
                        pltpu.VMEM((SMALL_ROWS, D), F32)]
        + [pltpu.SemaphoreType.DMA((NDEV - 1,))] * 4)
    return pl.pallas_call(
        body, name=name, grid_spec=grid_spec,
        out_shape=[jax.ShapeDtypeStruct((r, n1), F32), jax.ShapeDtypeStruct((NDEV, r, n1), BF),
                   jax.ShapeDtypeStruct((SMALL_ROWS, D), F32)],
        compiler_params=_params(("arbitrary", "arbitrary")),
    )(me, a, b, small)


def _dw_plain_call(name, me, a, b, a_spec, b_spec, split, k1, n1, s):
    nb = NDEV // split
    r = k1 // split
    ts = min(TS_DW, s)
    ns = s // ts

    def slab(i, me_ref):
        return i

    def body(me_ref, a_ref, b_ref, own_ref, part_ref, acc):
        i = pl.program_id(0)
        j = pl.program_id(1)
        p = _dot_tn(a_ref[...], b_ref[...])

        @pl.when(j == 0)
        def _():
            acc[...] = p

        @pl.when(j > 0)
        def _():
            acc[...] += p

        @pl.when(j == ns - 1)
        def _():
            part_ref[...] = acc[...].astype(BF)

            @pl.when(i == me_ref[0] // split)
            def _():
                off = pl.multiple_of(lax.rem(me_ref[0], split) * r, RC)
                own_ref[...] = acc[pl.ds(off, r), :]

    grid_spec = pltpu.PrefetchScalarGridSpec(
        num_scalar_prefetch=1, grid=(nb, ns),
        in_specs=[a_spec(ts, slab), b_spec(ts, slab)],
        out_specs=[pl.BlockSpec((r, n1), lambda i, j, me_ref: (0, 0)),
                   pl.BlockSpec((None, k1, n1), lambda i, j, me_ref: (i, 0, 0))],
        scratch_shapes=[pltpu.VMEM((k1, n1), F32)])
    own, part = pl.pallas_call(
        body, name=name, grid_spec=grid_spec,
        out_shape=[jax.ShapeDtypeStruct((r, n1), F32), jax.ShapeDtypeStruct((nb, k1, n1), BF)],
        compiler_params=_params(("arbitrary", "arbitrary")),
    )(me, a, b)
    return own, part.reshape(NDEV, r, n1)


def _rows2d(w):
    return lambda ts, slab: pl.BlockSpec((ts, w), lambda i, j, me_ref: (j, 0))


def _cols2d(w):
    return lambda ts, slab: pl.BlockSpec((ts, w), lambda i, j, me_ref: (j, slab(i, me_ref)))


def _blk3d(w):
    return lambda ts, slab: pl.BlockSpec((None, ts, w), lambda i, j, me_ref: (slab(i, me_ref), j, 0))


_BC1 = 1.0 - ADAM_B1 ** ADAM_STEP
_BC2 = 1.0 - ADAM_B2 ** ADAM_STEP


def _adamw_math(w, g, m, v):
    m = ADAM_B1 * m + (1.0 - ADAM_B1) * g
    v = ADAM_B2 * v + (1.0 - ADAM_B2) * (g * g)
    m_hat = m / _BC1
    v_hat = v / _BC2
    delta = -ADAM_LR * (m_hat / (jnp.sqrt(v_hat) + ADAM_EPS) + ADAM_WD * w)
    return delta, m, v


def _row_tile(r):
    for t in (256, 176, 128, 64, 32, 16, 8):
        if r % t == 0:
            return t
    return r


def _reduce_adamw_call(name, me, own, recv, w, m, v):
    r, c = own.shape
    tr = _row_tile(r)

    def body(me_ref, own_ref, recv_ref, w_ref, m_ref, v_ref, g_ref, d_ref, nm_ref, nv_ref):
        mine = me_ref[0]
        g = jnp.zeros((tr, c), F32)
        for sdev in range(NDEV):
            g = g + jnp.where(mine == sdev, own_ref[...], recv_ref[sdev].astype(F32))
        g_ref[...] = g
        d_ref[...], nm_ref[...], nv_ref[...] = _adamw_math(w_ref[...], g, m_ref[...], v_ref[...])

    tile = pl.BlockSpec((tr, c), lambda i, me_ref: (i, 0))
    grid_spec = pltpu.PrefetchScalarGridSpec(
        num_scalar_prefetch=1, grid=(r // tr,),
        in_specs=[tile, pl.BlockSpec((NDEV, tr, c), lambda i, me_ref: (0, i, 0)), tile, tile, tile],
        out_specs=[tile] * 4)
    return pl.pallas_call(
        body, name=name, grid_spec=grid_spec,
        out_shape=[jax.ShapeDtypeStruct((r, c), F32)] * 4,
        compiler_params=_params(("parallel",)),
    )(me, own, recv, w, m, v)


def _adamw_call(name, w, g, m, v):
    r, c = w.shape
    tr = _row_tile(r)

    def body(w_ref, g_ref, m_ref, v_ref, d_ref, nm_ref, nv_ref):
        d_ref[...], nm_ref[...], nv_ref[...] = _adamw_math(w_ref[...], g_ref[...], m_ref[...], v_ref[...])

    tile = pl.BlockSpec((tr, c), lambda i: (i, 0))
    return pl.pallas_call(
        body, name=name, grid=(r // tr,), in_specs=[tile] * 4, out_specs=[tile] * 3,
        out_shape=[jax.ShapeDtypeStruct((r, c), F32)] * 3,
        compiler_params=_params(("parallel",)),
    )(w, g, m, v)


def _me():
    return lax.axis_index("x"), lax.axis_index("y"), lax.axis_index("c")


def _flip(pos, r):
    x, y, c = pos
    return (1 - x if r & 4 else x, 1 - y if r & 2 else y, 1 - c if r & 1 else c)


def _lin(pos):
    return pos[0] * 4 + pos[1] * 2 + pos[2]


class _SmallAllReduce:
    def __init__(self, g_ref, out_ref, rbuf, send1, recv1, send2, recv2):
        self.g, self.out, self.rbuf = g_ref, out_ref, rbuf
        self.sems = (send1, recv1, send2, recv2)
        self.me = _me()
        self.mi = _lin(self.me)

    @staticmethod
    def _rows(d):
        return pl.ds(pl.multiple_of(d * SMALL_PER, 8), SMALL_PER)

    def _scatter(self, r, outgoing):
        peer = _flip(self.me, r)
        src, dst = (_lin(peer), self.mi) if outgoing else (self.mi, _lin(peer))
        return pltpu.make_async_remote_copy(
            src_ref=self.g.at[self._rows(src)], dst_ref=self.rbuf.at[dst],
            send_sem=self.sems[0].at[r - 1], recv_sem=self.sems[1].at[r - 1], device_id=peer, device_id_type=MESH)

    def _spread(self, r, outgoing):
        peer = _flip(self.me, r)
        rows = self._rows(self.mi if outgoing else _lin(peer))
        return pltpu.make_async_remote_copy(
            src_ref=self.out.at[rows], dst_ref=self.out.at[rows],
            send_sem=self.sems[2].at[r - 1], recv_sem=self.sems[3].at[r - 1], device_id=peer, device_id_type=MESH)

    def scatter(self):
        for r in range(1, NDEV):
            self._scatter(r, True).start()
        self.rbuf[self.mi] = self.g[self._rows(self.mi), :]

    def reduce(self):
        for r in range(1, NDEV):
            self._scatter(r, False).wait_recv()
        for r in range(1, NDEV):
            self._scatter(r, True).wait_send()
        tot = self.rbuf[0]
        for d in range(1, NDEV):
            tot = tot + self.rbuf[d]
        self.out[self._rows(self.mi), :] = tot
        for r in range(1, NDEV):
            self._spread(r, True).start()

    def finish(self):
        for r in range(1, NDEV):
            self._spread(r, False).wait_recv()
        for r in range(1, NDEV):
            self._spread(r, True).wait_send()


def _head_blocks(w):
    z = jnp.zeros((64, 64), w.dtype)
    groups = []
    for q in range(NQ):
        rows = [jnp.concatenate([w[4 * q + a] if a == b else z for b in range(4)], axis=1) for a in range(4)]
        groups.append(jnp.concatenate(rows, axis=0))
    return jnp.stack(groups)


def _head_unblocks(g):
    return jnp.stack([g[q, 64 * a:64 * a + 64, 64 * a:64 * a + 64] for q in range(NQ) for a in range(4)])


def _local_step(x, tgt, p, me, order):
    s = x.shape[0]
    vec = lambda a: a.reshape(1, D)
    gmix, gffn, gfin = vec(p["norm_mix_g"]), vec(p["norm_ffn_g"]), vec(p["norm_final_g"])
    cb, ba, bx, lam = vec(p["conv_b"]), vec(p["rg_ba"]), vec(p["rg_bx"]), vec(p["rg_lambda"])
    lng, lnb = vec(p["sgu_ln_g"]), vec(p["sgu_ln_b"])
    wax = jnp.concatenate([_head_blocks(p["rg_wa"]), _head_blocks(p["rg_wx"])], axis=2).astype(BF)
    tril = jnp.tril(jnp.ones((CHUNK, CHUNK), bool))
    ws = jnp.where(tril[None], p["sgu_ws"], 0.0)
    wtr = ws.astype(BF)
    wtrt = jnp.swapaxes(ws, 1, 2).astype(BF)
    bias = jnp.repeat(p["sgu_bs"].T, GW, axis=1)
    shard = {k: p[k].astype(BF) for k in _BIG}

    proj, h1, w_in, (wpa, wpb, wo, cw) = _proj_gather_call(
        x, gmix, shard["w_in"], order, [shard["w_proj_a"], shard["w_proj_b"], shard["w_out"], p["conv_w"]])
    wpa, wpb, wo = (t.reshape(D, D) for t in (wpa, wpb, wo))
    cw = jnp.swapaxes(cw, 0, 1).reshape(4, D)
    (h, pa, pb, x1, ya, yb, mg), (wgu, wdn) = _mixer_fwd_call(
        proj, x, cw, cb, ba, bx, lam, lng, lnb, wax, wtr, bias, wpa, wpb, wo, [shard["w_gate_up"], shard["w_down"]])
    wdn = wdn.reshape(NDEV // 2, -1, D)
    nb, _, wb = w_in.shape
    fb = wgu.shape[2]
    nh = wdn.shape[0]
    dx1, dx1b, h2, act, dgu, dx2b, facc = _ffn_call(x1, tgt, gffn, gfin, wgu, wdn)
    assert nb == NDEV and 2 * nh == NDEV
    own_gu, part_gu = _dw_plain_call("dw_gate_up", me, dgu, h2, _blk3d(fb), _rows2d(D), 1, fb, D, s)
    own_dn, part_dn = _dw_plain_call("dw_down", me, act, dx2b, _blk3d(fb), _rows2d(D), 2, fb, D, s)
    (dprojb, dya, dpa, dpb, bvec, dws, dbs), (recv_dn,) = _mixer_bwd_pre_call(
        dx1b, proj, pa, pb, lng, lnb, wtr, wtrt, bias, wpa, wpb, wo, [part_dn])
    own_pa, part_pa = _dw_plain_call("dw_proj_a", me, ya, dpa, _rows2d(D), _rows2d(D), NDEV, D, D, s)
    own_pb, part_pb = _dw_plain_call("dw_proj_b", me, yb, dpb, _rows2d(D), _rows2d(D), NDEV, D, D, s)
    own_wo, part_wo = _dw_plain_call("dw_out", me, mg, dx1b, _rows2d(D), _rows2d(D), NDEV, D, D, s)
    (dproj, dx, svec, dwax), (recv_gu, recv_pa, recv_pb, recv_wo) = _mixer_bwd_seq_call(
        dprojb, dya, proj, h, x, dx1, gmix, w_in, cw, cb, ba, bx, lam, wax, [part_gu, part_pa, part_pb, part_wo])
    small = {
        "norm_mix_g": svec[8], "norm_ffn_g": facc[0], "norm_final_g": facc[1],
        "conv_b": svec[0], "rg_ba": svec[1], "rg_bx": svec[2], "rg_lambda": svec[3],
        "sgu_ln_g": bvec[0], "sgu_ln_b": bvec[1],
        "rg_wa": _head_unblocks(dwax[:, :, 0:QW]), "rg_wx": _head_unblocks(dwax[:, :, QW:2 * QW]),
        "sgu_ws": dws, "sgu_bs": dbs[:, ::GW].T,
    }
    packed = _pack_small(small, svec[4:8], facc[2])
    own_in, recv_in, gsum = _dw_exchange_call("dw_in", me, h1, dproj, _rows2d(D), _cols2d(wb), 1, D, wb, s, packed)
    dw = {
        "w_gate_up": (own_gu, recv_gu), "w_down": (own_dn, recv_dn), "w_proj_a": (own_pa, recv_pa),
        "w_proj_b": (own_pb, recv_pb), "w_out": (own_wo, recv_wo), "w_in": (own_in, recv_in),
    }
    return gsum, dx, dw


_BIG = ("w_in", "w_gate_up", "w_down", "w_proj_a", "w_proj_b", "w_out")
_VEC_ROWS = ("norm_mix_g", "norm_ffn_g", "norm_final_g", "conv_b", "rg_ba", "rg_bx", "rg_lambda",
             "sgu_ln_g", "sgu_ln_b", "sgu_bs")
_WEIGHTS = ("norm_mix_g", "w_in", "conv_w", "conv_b", "rg_wa", "rg_ba", "rg_wx", "rg_bx", "rg_lambda",
            "sgu_ln_g", "sgu_ln_b", "sgu_ws", "sgu_bs", "w_proj_a", "w_proj_b", "w_out", "norm_ffn_g",
            "w_gate_up", "w_down", "norm_final_g")


def _pack_small(t, conv_w, extra=None):
    extra = jnp.zeros((1, D), F32) if extra is None else extra.reshape(1, D)
    head = jnp.concatenate([t[k].reshape(1, D) for k in _VEC_ROWS] + [conv_w, extra, jnp.zeros((1, D), F32)], axis=0)
    return jnp.concatenate([head, t["rg_wa"].reshape(64, D), t["rg_wx"].reshape(64, D), t["sgu_ws"].reshape(128, D),
                            jnp.zeros((SMALL_ROWS - 272, D), F32)], axis=0)


def _unpack_small(a):
    out = {k: a[j] for j, k in enumerate(_VEC_ROWS)}
    out["conv_w"] = a[10:14]
    out["rg_wa"] = a[16:80].reshape(16, 64, 64)
    out["rg_wx"] = a[80:144].reshape(16, 64, 64)
    out["sgu_ws"] = a[144:272].reshape(NGRP, CHUNK, CHUNK)
    return out


def kernel(x, norm_mix_g, w_in, conv_w, conv_b, rg_wa, rg_ba, rg_wx, rg_bx, rg_lambda, sgu_ln_g, sgu_ln_b, sgu_ws, sgu_bs, w_proj_a, w_proj_b, w_out, norm_ffn_g, w_gate_up, w_down, norm_final_g, loss_target, m_norm_mix_g, m_w_in, m_conv_w, m_conv_b, m_rg_wa, m_rg_ba, m_rg_wx, m_rg_bx, m_rg_lambda, m_sgu_ln_g, m_sgu_ln_b, m_sgu_ws, m_sgu_bs, m_w_proj_a, m_w_proj_b, m_w_out, m_norm_ffn_g, m_w_gate_up, m_w_down, m_norm_final_g, v_norm_mix_g, v_w_in, v_conv_w, v_conv_b, v_rg_wa, v_rg_ba, v_rg_wx, v_rg_bx, v_rg_lambda, v_sgu_ln_g, v_sgu_ln_b, v_sgu_ws, v_sgu_bs, v_w_proj_a, v_w_proj_b, v_w_out, v_norm_ffn_g, v_w_gate_up, v_w_down, v_norm_final_g):
    args = dict(locals())
    w = {k: args[k] for k in _WEIGHTS}
    m = {k: args["m_" + k] for k in _WEIGHTS}
    v = {k: args["v_" + k] for k in _WEIGHTS}
    for d in (w, m, v):
        for k in _WEIGHTS:
            if k != "norm_final_g":
                d[k] = d[k][0]
    me = _lin(_me())
    me1 = me.reshape(1).astype(jnp.int32)

    order = jnp.bitwise_xor(me, jnp.array(_PASS_FLIPS, jnp.int32)).astype(jnp.int32)

    gsum, dx, dw = _local_step(x[0], loss_target[0], w, me1, order)

    grads, delta, new_m, new_v = {}, {}, {}, {}
    for k in _BIG:
        own, recv = dw[k]
        flip = own.shape != w[k].shape
        wmv = [jnp.swapaxes(t, 0, 1) if flip else t for t in (w[k], m[k], v[k])]
        res = _reduce_adamw_call("adamw_" + k, me1, own, recv, *wmv)
        grads[k], delta[k], new_m[k], new_v[k] = (jnp.swapaxes(t, 0, 1) if flip else t for t in res)

    loss = (0.5 / D) * jnp.sum(gsum[14])
    zc = jnp.zeros((4, D), F32)
    d_s, m_s, v_s = _adamw_call("adamw_small", _pack_small(w, zc), gsum, _pack_small(m, zc), _pack_small(v, zc))
    gs, ds, ms, vs = _unpack_small(gsum), _unpack_small(d_s), _unpack_small(m_s), _unpack_small(v_s)
    g_cw = lax.dynamic_slice(gs["conv_w"], (0, me * 128), (4, 128))
    ds["conv_w"], ms["conv_w"], vs["conv_w"] = _adamw_call("adamw_conv_w", w["conv_w"], g_cw, m["conv_w"], v["conv_w"])
    gs["conv_w"] = g_cw
    for k in _WEIGHTS:
        if k not in _BIG:
            shp = w[k].shape
            grads[k], delta[k], new_m[k], new_v[k] = (t[k].reshape(shp) for t in (gs, ds, ms, vs))

    def lift(t, k):
        return t[k] if k == "norm_final_g" else t[k][None]

    outs = [loss, dx[None]]
    for t in (grads, delta, new_m, new_v):
        outs += [lift(t, k) for k in _WEIGHTS]
    return tuple(outs)
```

```python
import functools

import jax
import jax.numpy as jnp
from jax import lax
from jax.experimental import pallas as pl
from jax.experimental.pallas import tpu as pltpu

F32 = jnp.float32
BF = jnp.bfloat16

D = 1024
NDEV = 8
EPS = 1e-6
RG_C = 8.0
CHUNK = 128
NGRP = 8
GW = 128
NQ = 4
QW = 256
RC = 16
SMALL_ROWS = 320
SMALL_PER = SMALL_ROWS // NDEV

ADAM_LR = 0.001
ADAM_B1 = 0.9
ADAM_B2 = 0.999
ADAM_EPS = 1e-08
ADAM_WD = 0.01
ADAM_STEP = 10

VMEM_LIMIT = 60 * 1024 * 1024

MESH = pl.DeviceIdType.MESH


def _rows(n, fn, unroll=2, rc=RC):
    def body(i, c):
        fn(pl.multiple_of(i * rc, rc))
        return c
    lax.fori_loop(0, n // rc, body, 0, unroll=unroll)


def _fold(v):
    return jnp.sum(v.reshape(v.shape[0] // RC, RC, v.shape[1]), axis=0)


def _dot(a, b):
    return jnp.dot(a, b, preferred_element_type=F32)


def _dot_nt(a, b):
    return lax.dot_general(a, b, (((1,), (1,)), ((), ())), preferred_element_type=F32)


def _dot_tn(a, b):
    return lax.dot_general(a, b, (((0,), (0,)), ((), ())), preferred_element_type=F32)


_GC = 0.7978845608028654
_GK = 0.044715


def _gelu(x):
    t = jnp.tanh(_GC * (x + _GK * (x * x * x)))
    return x * (0.5 * (1.0 + t))


def _gelu_grad(x):
    x2 = x * x
    t = jnp.tanh(_GC * (x + _GK * (x2 * x)))
    cdf = 0.5 * (1.0 + t)
    dg = cdf + (0.5 * x) * (1.0 - t * t) * (_GC * (1.0 + (3.0 * _GK) * x2))
    return x * cdf, dg


def _sigmoid(x):
    return jax.nn.sigmoid(x)


def _log1p(e):
    u = 1.0 + e
    d = u - 1.0
    return jnp.where(d == 0.0, e, jnp.log(u) * (e / jnp.where(d == 0.0, 1.0, d)))


def _softplus(z):
    return jnp.maximum(z, 0.0) + _log1p(jnp.exp(-jnp.abs(z)))


def _neg_expm1(z):
    u = jnp.exp(z)
    lu = jnp.log(u)
    k = (1.0 - u) * (z / jnp.where(lu == 0.0, 1.0, lu))
    small = jnp.where(lu == 0.0, -z, k)
    return jnp.where(z > -0.5, small, 1.0 - u)


def _shift_back(prev8, cur, j):
    cat = jnp.concatenate([prev8, cur], axis=0)
    return pltpu.roll(cat, j, 0)[8:8 + cur.shape[0]]


def _shift_fwd(cur, next8, j):
    cat = jnp.concatenate([cur, next8], axis=0)
    n = cat.shape[0]
    return pltpu.roll(cat, n - j, 0)[0:cur.shape[0]]


def _const_spec(shape):
    nd = len(shape)
    return pl.BlockSpec(shape, lambda *_: (0,) * nd, pipeline_mode=pl.Buffered(1))


def _params(sem):
    return pltpu.CompilerParams(dimension_semantics=sem, vmem_limit_bytes=VMEM_LIMIT)


TM_PROJ = 1024
TM_MIX = 256
TM_FFN = 256
TM_DX = 512
TS_DW = 4096


_CHIPS = (4, 2, 6)
_PASS_FLIPS = (0, 1, 4, 2, 6, 5, 3, 7)


class _Gather:
    def __init__(self, ins, outs, send_sems, recv_sems, local_sems):
        self.ins, self.outs = ins, outs
        self.send_sems, self.recv_sems, self.local_sems = send_sems, recv_sems, local_sems
        self.me = _me()
        self.sibling = _flip(self.me, 1)

    def _copy(self, a, kind, block, to, src=None):
        dst = self.outs[a].at[_lin(block)]
        return pltpu.make_async_remote_copy(
            src_ref=dst if src is None else src, dst_ref=dst,
            send_sem=self.send_sems.at[a, kind], recv_sem=self.recv_sems.at[a, kind],
            device_id=to, device_id_type=MESH)

    def _local(self, a):
        return pltpu.make_async_copy(self.ins[a], self.outs[a].at[_lin(self.me)], self.local_sems.at[a])

    def start(self):
        for a in range(len(self.ins)):
            self._local(a).start()
            self._copy(a, 0, self.me, self.sibling, src=self.ins[a]).start()
            for j, f in enumerate(_CHIPS):
                self._copy(a, 1 + j, self.me, _flip(self.me, f), src=self.ins[a]).start()

    def forward(self):
        for j, f in enumerate(_CHIPS):
            for a in range(len(self.ins)):
                self._copy(a, 1 + j, _flip(self.me, f), self.me).wait_recv()
                self._copy(a, 4 + j, _flip(self.me, f), self.sibling).start()

    def finish(self):
        for a in range(len(self.ins)):
            self._copy(a, 0, self.sibling, self.me).wait_recv()
            for j, f in enumerate(_CHIPS):
                self._copy(a, 4 + j, _flip(self.me, f | 1), self.me).wait_recv()
            self._copy(a, 0, self.me, self.sibling, src=self.ins[a]).wait_send()
            for j, f in enumerate(_CHIPS):
                self._copy(a, 1 + j, self.me, _flip(self.me, f), src=self.ins[a]).wait_send()
                self._copy(a, 4 + j, _flip(self.me, f), self.sibling).wait_send()
            self._local(a).wait()


class _Exchange:
    def __init__(self, ins, outs, send_sems, recv_sems, local_sems):
        self.ins, self.outs = ins, outs
        self.send_sems, self.recv_sems, self.local_sems = send_sems, recv_sems, local_sems
        self.me = _me()

    def _copy(self, a, r, outgoing):
        peer = _flip(self.me, r)
        src, dst = (peer, self.me) if outgoing else (self.me, peer)
        return pltpu.make_async_remote_copy(
            src_ref=self.ins[a].at[_lin(src)], dst_ref=self.outs[a].at[_lin(dst)],
            send_sem=self.send_sems.at[a, r - 1], recv_sem=self.recv_sems.at[a, r - 1],
            device_id=peer, device_id_type=MESH)

    def _local(self, a):
        mi = _lin(self.me)
        return pltpu.make_async_copy(self.ins[a].at[mi], self.outs[a].at[mi], self.local_sems.at[a])

    def start(self):
        for a in range(len(self.ins)):
            self._local(a).start()
        for r in range(1, NDEV):
            for a in range(len(self.ins)):
                self._copy(a, r, True).start()

    def finish(self):
        for r in range(1, NDEV):
            for a in range(len(self.ins)):
                self._copy(a, r, False).wait_recv()
        for r in range(1, NDEV):
            for a in range(len(self.ins)):
                self._copy(a, r, True).wait_send()
        for a in range(len(self.ins)):
            self._local(a).wait()


def _gather_sems(n):
    return [pltpu.SemaphoreType.DMA((n, 7)), pltpu.SemaphoreType.DMA((n, 7)), pltpu.SemaphoreType.DMA((n,))]


def _proj_gather_call(x, gmix, w_shard, order, extras):
    s = x.shape[0]
    tm = min(TM_PROJ, s)
    nt = s // tm
    wb = w_shard.shape[1]
    n = len(extras)

    def body(order_ref, x_ref, g_ref, wsh_ref, *rest):
        ex_in = rest[:n]
        proj_ref, h1_hbm, wall_hbm = rest[n:n + 3]
        ex_out = rest[n + 3:2 * n + 3]
        wall, hc, ws_send, ws_recv, ex_send, ex_recv, ex_local, out_sems = rest[2 * n + 3:]
        k = pl.program_id(0)
        i = pl.program_id(1)
        me = _me()
        sibling = _flip(me, 1)
        gather = _Gather(ex_in, ex_out, ex_send, ex_recv, ex_local)

        def wcopy(kind, block, to):
            ref = wall.at[_lin(block)]
            return pltpu.make_async_remote_copy(
                src_ref=ref, dst_ref=ref, send_sem=ws_send.at[kind], recv_sem=ws_recv.at[kind],
                device_id=to, device_id_type=MESH)

        head = i == 0

        @pl.when(head & (k == 0))
        def _():
            wall[_lin(me)] = wsh_ref[...]
            wcopy(0, me, sibling).start()
            for j, f in enumerate(_CHIPS):
                wcopy(1 + j, me, _flip(me, f)).start()
            gather.start()

        @pl.when(head & (k == 1))
        def _():
            wcopy(0, sibling, me).wait_recv()

        for j, f in enumerate(_CHIPS):
            @pl.when(head & (k == 2 + j))
            def _(j=j, f=f):
                wcopy(1 + j, _flip(me, f), me).wait_recv()
                wcopy(4 + j, _flip(me, f), sibling).start()

            @pl.when(head & (k == 5 + j))
            def _(j=j, f=f):
                wcopy(4 + j, _flip(me, f | 1), me).wait_recv()

        pl.when(head & (k == 5))(gather.forward)

        base = pl.multiple_of(i * tm, tm)

        @pl.when(k == 0)
        def _():
            g = g_ref[...]

            def norm(r0):
                xx = x_ref[pl.ds(r0, RC), :]
                r = lax.rsqrt(jnp.mean(xx * xx, axis=-1, keepdims=True) + EPS)
                hc[pl.ds(base + r0, RC), :] = ((xx * r) * g).astype(BF)
            _rows(tm, norm, unroll=4)

        proj_ref[...] = _dot(hc[pl.ds(base, tm), :], wall[order_ref[k]])

        @pl.when((k == NDEV - 1) & (i == nt - 1))
        def _():
            out_w = pltpu.make_async_copy(wall, wall_hbm, out_sems.at[0])
            out_h = pltpu.make_async_copy(hc, h1_hbm, out_sems.at[1])
            out_w.start()
            out_h.start()
            wcopy(0, me, sibling).wait_send()
            for j, f in enumerate(_CHIPS):
                wcopy(1 + j, me, _flip(me, f)).wait_send()
                wcopy(4 + j, _flip(me, f), sibling).wait_send()
            gather.finish()
            out_w.wait()
            out_h.wait()

    any_spec = pl.BlockSpec(memory_space=pl.ANY)
    grid_spec = pltpu.PrefetchScalarGridSpec(
        num_scalar_prefetch=1, grid=(NDEV, nt),
        in_specs=[pl.BlockSpec((tm, D), lambda k, i, o: (jnp.where(k == 0, i, nt - 1), 0)),
                  pl.BlockSpec((1, D), lambda k, i, o: (0, 0)),
                  pl.BlockSpec(w_shard.shape, lambda k, i, o: (0, 0))] + [any_spec] * n,
        out_specs=[pl.BlockSpec((tm, wb), lambda k, i, o: (i, o[k])), any_spec, any_spec] + [any_spec] * n,
        scratch_shapes=[pltpu.VMEM((NDEV,) + w_shard.shape, BF), pltpu.VMEM((s, D), BF),
                        pltpu.SemaphoreType.DMA((7,)), pltpu.SemaphoreType.DMA((7,))] + _gather_sems(n)
        + [pltpu.SemaphoreType.DMA((2,))])
    outs = pl.pallas_call(
        body, name="proj_gather", grid_spec=grid_spec,
        out_shape=[jax.ShapeDtypeStruct((s, NDEV * wb), F32), jax.ShapeDtypeStruct((s, D), BF),
                   jax.ShapeDtypeStruct((NDEV,) + w_shard.shape, BF)]
        + [jax.ShapeDtypeStruct((NDEV,) + e.shape, e.dtype) for e in extras],
        compiler_params=_params(("arbitrary", "arbitrary")),
    )(order, x, gmix, w_shard, *extras)
    return outs[0], outs[1], outs[2], outs[3:]


def _conv_tile(rx, prev8, cw_ref, cb):
    xc = cb + cw_ref[3:4, :] * rx
    for j in (1, 2, 3):
        xc = xc + cw_ref[3 - j:4 - j, :] * _shift_back(prev8, rx, j)
    return xc


def _gate_tile(xcb, wax_ref, ba, bx, sp, q):
    cs = slice(q * QW, (q + 1) * QW)
    z = _dot(xcb[:, cs], wax_ref[q])
    r = _sigmoid(z[:, 0:QW] + ba[:, cs])
    ig = _sigmoid(z[:, QW:2 * QW] + bx[:, cs])
    return r, ig, (-RG_C * r) * sp[:, cs]


def _scan_fwd(a_s, b_s, out_ref, h0, n):
    rowi = lax.broadcasted_iota(jnp.int32, (8, D), 0)

    def block(t, h):
        rows = pl.ds(pl.multiple_of(t * 8, 8), 8)
        a = a_s[rows, :]
        b = b_s[rows, :]
        for d in (1, 2, 4):
            m = rowi >= d
            b = jnp.where(m, a * pltpu.roll(b, d, 0) + b, b)
            a = jnp.where(m, a * pltpu.roll(a, d, 0), a)
        hb = b + a * h
        out_ref[rows, :] = hb
        return hb[7:8, :]
    return lax.fori_loop(0, n // 8, block, h0, unroll=4)


def _scan_bwd(a_s, lm_s, c0, n):
    rowi = lax.broadcasted_iota(jnp.int32, (8, D), 0)
    nblk = n // 8

    def block(k, cin):
        rows = pl.ds(pl.multiple_of((nblk - 1 - k) * 8, 8), 8)
        a = a_s[rows, :]
        dh = lm_s[rows, :]
        b = a * dh
        for d in (1, 2, 4):
            m = rowi < 8 - d
            b = jnp.where(m, a * pltpu.roll(b, 8 - d, 0) + b, b)
            a = jnp.where(m, a * pltpu.roll(a, 8 - d, 0), a)
        mu = b + a * cin
        lm_s[rows, :] = dh + jnp.where(rowi < 7, pltpu.roll(mu, 7, 0), cin)
        return mu[0:1, :]
    return lax.fori_loop(0, nblk, block, c0, unroll=4)


def _mixer_fwd_call(proj, x, cw, cb, ba, bx, lam, lng, lnb, wax, wtr, bias, wpa, wpb, wo, shards):
    s = x.shape[0]
    tm = min(TM_MIX, s)
    nt = s // tm
    pw = proj.shape[1]
    n = len(shards)

    def body(proj_ref, x_ref, cw_ref, cb_ref, ba_ref, bx_ref, lam_ref, lng_ref, lnb_ref, wax_ref, wtr_ref,
             bias_ref, wpa_ref, wpb_ref, wo_ref, *rest):
        sh_in = rest[:n]
        h_ref, pa_ref, pb_ref, x1_ref, xc_ref, r_ref, ig_ref, a_s, m_ref, ya_ref, yb_ref, mg_ref = rest[n:n + 12]
        sh_out = rest[n + 12:2 * n + 12]
        prev_s, b_s, hc_s, vn_s, mx_s, g_send, g_recv, g_local = rest[2 * n + 12:]
        i = pl.program_id(0)
        gather = _Gather(sh_in, sh_out, g_send, g_recv, g_local)

        @pl.when(i == 0)
        def _():
            gather.start()
            prev_s[...] = jnp.zeros((8, D), F32)
            hc_s[...] = jnp.zeros((8, D), F32)

        pl.when(i == nt // 2)(gather.forward)

        rx = proj_ref[:, 0:D]
        xc = _conv_tile(rx, prev_s[...], cw_ref, cb_ref[...])
        prev_s[...] = rx[tm - 8:tm, :]
        xc_ref[...] = xc
        xcb = xc.astype(BF)
        sp = _softplus(-lam_ref[...])
        ba = ba_ref[...]
        bx = bx_ref[...]
        for q in range(NQ):
            cs = slice(q * QW, (q + 1) * QW)
            r, ig, la = _gate_tile(xcb, wax_ref, ba, bx, sp, q)
            mq = jnp.sqrt(_neg_expm1(2.0 * la))
            r_ref[:, cs] = r
            ig_ref[:, cs] = ig
            m_ref[:, cs] = mq
            a_s[:, cs] = jnp.exp(la)
            b_s[:, cs] = mq * (ig * xc[:, cs])

        gv = _gelu(proj_ref[:, 3 * D:4 * D])
        dv = gv - jnp.mean(gv, axis=-1, keepdims=True)
        var = jnp.mean(dv * dv, axis=-1, keepdims=True)
        vn_s[...] = ((dv * lax.rsqrt(var + EPS)) * lng_ref[...] + lnb_ref[...]).astype(BF)
        nc = tm // CHUNK
        for c in range(nc):
            rs = slice(c * CHUNK, (c + 1) * CHUNK)
            for g in range(NGRP):
                cs = slice(g * GW, (g + 1) * GW)
                mx_s[rs, cs] = _dot(wtr_ref[g], vn_s[rs, cs])
        mixed = mx_s[...] + jnp.concatenate([bias_ref[...]] * nc, axis=0)
        yb = (_gelu(proj_ref[:, 2 * D:3 * D]) * mixed).astype(BF)
        yb_ref[...] = yb
        pb_ref[...] = _dot(yb, wpb_ref[...])

        hc_s[0:1, :] = _scan_fwd(a_s, b_s, h_ref, hc_s[0:1, :], tm)

        ya = (_gelu(proj_ref[:, D:2 * D]) * h_ref[...]).astype(BF)
        ya_ref[...] = ya
        pa = _dot(ya, wpa_ref[...])
        pa_ref[...] = pa
        mg = (_sigmoid(proj_ref[:, 4 * D:5 * D]) * pa + _sigmoid(proj_ref[:, 5 * D:6 * D]) * pb_ref[...]).astype(BF)
        mg_ref[...] = mg
        x1_ref[...] = x_ref[...] + _dot(mg, wo_ref[...])

        pl.when(i == nt - 1)(gather.finish)

    tile = lambda w: pl.BlockSpec((tm, w), lambda i: (i, 0))
    vec = _const_spec((1, D))
    any_spec = pl.BlockSpec(memory_space=pl.ANY)
    outs = pl.pallas_call(
        body, name="mixer_fwd", grid=(nt,),
        in_specs=[tile(pw), tile(D), _const_spec((4, D)), vec, vec, vec, vec, vec, vec,
                  _const_spec(wax.shape), _const_spec(wtr.shape), _const_spec(bias.shape),
                  _const_spec((D, D)), _const_spec((D, D)), _const_spec((D, D))] + [any_spec] * n,
        out_specs=[tile(D)] * 12 + [any_spec] * n,
        out_shape=[jax.ShapeDtypeStruct((s, D), F32)] * 9 + [jax.ShapeDtypeStruct((s, D), BF)] * 3
        + [jax.ShapeDtypeStruct((NDEV,) + e.shape, e.dtype) for e in shards],
        scratch_shapes=[pltpu.VMEM((8, D), F32), pltpu.VMEM((tm, D), F32),
                        pltpu.VMEM((8, D), F32), pltpu.VMEM((tm, D), BF), pltpu.VMEM((tm, D), F32)]
        + _gather_sems(n),
        compiler_params=_params(("arbitrary",)),
    )(proj, x, cw, cb, ba, bx, lam, lng, lnb, wax, wtr, bias, wpa, wpb, wo, *shards)
    return outs[:12], outs[12:]


def _ffn_call(x1, tgt, gffn, gfin, wgu, wdn):
    s = x1.shape[0]
    tm = min(TM_FFN, s)
    nt = s // tm
    nh = wdn.shape[0]
    fb = wgu.shape[2]

    def body(x1_ref, tgt_ref, gffn_ref, gfin_ref, wgu_ref, wdn_ref,
             dx1_ref, dx1b_ref, h2_ref, act_ref, dgu_ref, dx2b_ref, acc_ref,
             g_s, u_s, dx2_s, accs):
        i = pl.program_id(0)

        @pl.when(i == 0)
        def _():
            accs[...] = jnp.zeros(accs.shape, F32)

        gffn = gffn_ref[...]
        gfin = gfin_ref[...]

        x1 = x1_ref[...]
        r2 = lax.rsqrt(jnp.mean(x1 * x1, axis=-1, keepdims=True) + EPS)
        xh2 = x1 * r2
        h2 = (xh2 * gffn).astype(BF)
        h2_ref[...] = h2

        for k in range(nh):
            g = _dot(h2, wgu_ref[k])
            u = _dot(h2, wgu_ref[k + nh])
            g_s[k] = g
            u_s[k] = u
            act_ref[k] = ((g * _sigmoid(g)) * u).astype(BF)

        x2 = x1
        for k in range(nh):
            x2 = x2 + _dot(act_ref[k], wdn_ref[k])

        r3 = lax.rsqrt(jnp.mean(x2 * x2, axis=-1, keepdims=True) + EPS)
        xh = x2 * r3
        err = xh * gfin - tgt_ref[...]
        accs[2] += _fold(err * err)
        dy = err * (1.0 / D)
        accs[1] += _fold(dy * xh)
        dxh = dy * gfin
        dx2 = r3 * (dxh - xh * jnp.mean(dxh * xh, axis=-1, keepdims=True))
        dx2_s[...] = dx2
        dx2b = dx2.astype(BF)
        dx2b_ref[...] = dx2b

        for k in range(nh):
            da = _dot_nt(dx2b, wdn_ref[k])
            g = g_s[k]
            sg = _sigmoid(g)
            dgu_ref[k] = ((da * u_s[k]) * (sg * (1.0 + g * (1.0 - sg)))).astype(BF)
            dgu_ref[k + nh] = (da * (g * sg)).astype(BF)

        dh2 = _dot_nt(dgu_ref[0], wgu_ref[0])
        for k in range(1, 2 * nh):
            dh2 = dh2 + _dot_nt(dgu_ref[k], wgu_ref[k])

        accs[0] += _fold(dh2 * xh2)
        dxh = dh2 * gffn
        dx1 = dx2_s[...] + r2 * (dxh - xh2 * jnp.mean(dxh * xh2, axis=-1, keepdims=True))
        dx1_ref[...] = dx1
        dx1b_ref[...] = dx1.astype(BF)

        @pl.when(i == nt - 1)
        def _():
            acc_ref[...] = jnp.zeros((8, D), F32)
            for j in range(3):
                acc_ref[j:j + 1, :] = jnp.sum(accs[j], axis=0, keepdims=True)

    tile = lambda w: pl.BlockSpec((tm, w), lambda i: (i, 0))
    vec = _const_spec((1, D))
    return pl.pallas_call(
        body, name="ffn", grid=(nt,),
        in_specs=[tile(D), tile(D), vec, vec, _const_spec(wgu.shape), _const_spec(wdn.shape)],
        out_specs=[tile(D), tile(D), tile(D),
                   pl.BlockSpec((nh, tm, fb), lambda i: (0, i, 0)),
                   pl.BlockSpec((2 * nh, tm, fb), lambda i: (0, i, 0)),
                   tile(D), pl.BlockSpec((8, D), lambda i: (0, 0))],
        out_shape=[jax.ShapeDtypeStruct((s, D), F32), jax.ShapeDtypeStruct((s, D), BF),
                   jax.ShapeDtypeStruct((s, D), BF), jax.ShapeDtypeStruct((nh, s, fb), BF),
                   jax.ShapeDtypeStruct((2 * nh, s, fb), BF), jax.ShapeDtypeStruct((s, D), BF),
                   jax.ShapeDtypeStruct((8, D), F32)],
        scratch_shapes=[pltpu.VMEM((nh, tm, fb), F32), pltpu.VMEM((nh, tm, fb), F32), pltpu.VMEM((tm, D), F32),
                        pltpu.VMEM((3, RC, D), F32)],
        compiler_params=_params(("arbitrary",)),
    )(x1, tgt, gffn, gfin, wgu, wdn)


def _mixer_bwd_pre_call(dx1b, proj, pa, pb, lng, lnb, wtr, wtrt, bias, wpa, wpb, wo, parts):
    s = dx1b.shape[0]
    tm = min(TM_MIX, s)
    nt = s // tm
    pw = proj.shape[1]
    n = len(parts)

    def body(dx1b_ref, uv_ref, gg_ref, pa_ref, pb_ref, lng_ref, lnb_ref, wtr_ref, wtrt_ref, bias_ref,
             wpa_ref, wpb_ref, wo_ref, *rest):
        ex_in = rest[:n]
        dproj_ref, dya_ref, dpa_ref, dpb_ref, vec_ref, dws_ref, dbs_ref = rest[n:n + 7]
        ex_out = rest[n + 7:2 * n + 7]
        vn_s, mx_s, dmx_s, dvn_s, accs, dbs_s, e_send, e_recv, e_local = rest[2 * n + 7:]
        i = pl.program_id(0)
        exchange = _Exchange(ex_in, ex_out, e_send, e_recv, e_local)

        @pl.when(i == 0)
        def _():
            exchange.start()
            accs[...] = jnp.zeros(accs.shape, F32)
            dbs_s[...] = jnp.zeros(dbs_s.shape, F32)
            dws_ref[...] = jnp.zeros(dws_ref.shape, F32)

        dm = _dot_nt(dx1b_ref[...], wo_ref[...])
        sa = _sigmoid(gg_ref[:, 0:D])
        sb = _sigmoid(gg_ref[:, D:2 * D])
        dpa = dm * sa
        dpb = dm * sb
        dpab = dpa.astype(BF)
        dpbb = dpb.astype(BF)
        dpa_ref[...] = dpab
        dpb_ref[...] = dpbb
        dproj_ref[:, 2 * D:3 * D] = ((dpa * pa_ref[...]) * (1.0 - sa)).astype(BF)
        dproj_ref[:, 3 * D:4 * D] = ((dpb * pb_ref[...]) * (1.0 - sb)).astype(BF)

        dya_ref[...] = _dot_nt(dpab, wpa_ref[...])
        dyb = _dot_nt(dpbb, wpb_ref[...])

        lng = lng_ref[...]
        gv, dgelu_v = _gelu_grad(uv_ref[:, D:2 * D])
        dv = gv - jnp.mean(gv, axis=-1, keepdims=True)
        rstd = lax.rsqrt(jnp.mean(dv * dv, axis=-1, keepdims=True) + EPS)
        xh = dv * rstd
        vn_s[...] = (xh * lng + lnb_ref[...]).astype(BF)

        nc = tm // CHUNK
        for c in range(nc):
            rs = slice(c * CHUNK, (c + 1) * CHUNK)
            for g in range(NGRP):
                cs = slice(g * GW, (g + 1) * GW)
                mx_s[rs, cs] = _dot(wtr_ref[g], vn_s[rs, cs])

        gu, dgelu_u = _gelu_grad(uv_ref[:, 0:D])
        mixed = mx_s[...] + jnp.concatenate([bias_ref[...]] * nc, axis=0)
        dproj_ref[:, 0:D] = ((dyb * mixed) * dgelu_u).astype(BF)
        dmx = dyb * gu
        dmx_s[...] = dmx.astype(BF)
        dbs_s[...] += jnp.sum(dmx.reshape(nc, CHUNK, D), axis=0)

        for c in range(nc):
            rs = slice(c * CHUNK, (c + 1) * CHUNK)
            for g in range(NGRP):
                cs = slice(g * GW, (g + 1) * GW)
                dvn_s[rs, cs] = _dot(wtrt_ref[g], dmx_s[rs, cs])
                dws_ref[g] += _dot_nt(dmx_s[rs, cs], vn_s[rs, cs])

        dvn = dvn_s[...]
        accs[0] += _fold(dvn * xh)
        accs[1] += _fold(dvn)
        dxh = dvn * lng
        m1 = jnp.mean(dxh, axis=-1, keepdims=True)
        m2 = jnp.mean(dxh * xh, axis=-1, keepdims=True)
        dproj_ref[:, D:2 * D] = ((rstd * (dxh - m1 - xh * m2)) * dgelu_v).astype(BF)

        @pl.when(i == nt - 1)
        def _():
            vec_ref[...] = jnp.zeros((8, D), F32)
            for j in range(2):
                vec_ref[j:j + 1, :] = jnp.sum(accs[j], axis=0, keepdims=True)
            row = lax.broadcasted_iota(jnp.int32, (CHUNK, CHUNK), 0)
            col = lax.broadcasted_iota(jnp.int32, (CHUNK, CHUNK), 1)
            for g in range(NGRP):
                dws_ref[g] = jnp.where(row >= col, dws_ref[g], 0.0)
                gs = jnp.sum(dbs_s[:, g * GW:(g + 1) * GW], axis=1, keepdims=True)
                dbs_ref[:, g * GW:(g + 1) * GW] = jnp.broadcast_to(gs, (CHUNK, GW))
            exchange.finish()

    tile = lambda w: pl.BlockSpec((tm, w), lambda i: (i, 0))
    vec = _const_spec((1, D))
    any_spec = pl.BlockSpec(memory_space=pl.ANY)
    outs = pl.pallas_call(
        body, name="mixer_bwd_pre", grid=(nt,),
        in_specs=[tile(D), pl.BlockSpec((tm, 2 * D), lambda i: (i, 1)), pl.BlockSpec((tm, 2 * D), lambda i: (i, 2)),
                  tile(D), tile(D), vec, vec, _const_spec(wtr.shape), _const_spec(wtrt.shape),
                  _const_spec(bias.shape), _const_spec((D, D)), _const_spec((D, D)), _const_spec((D, D))]
        + [any_spec] * n,
        out_specs=[tile(4 * D), tile(D), tile(D), tile(D), pl.BlockSpec((8, D), lambda i: (0, 0)),
                   pl.BlockSpec((NGRP, CHUNK, CHUNK), lambda i: (0, 0, 0)),
                   pl.BlockSpec((CHUNK, D), lambda i: (0, 0))] + [any_spec] * n,
        out_shape=[jax.ShapeDtypeStruct((s, 4 * D), BF), jax.ShapeDtypeStruct((s, D), F32),
                   jax.ShapeDtypeStruct((s, D), BF), jax.ShapeDtypeStruct((s, D), BF),
                   jax.ShapeDtypeStruct((8, D), F32), jax.ShapeDtypeStruct((NGRP, CHUNK, CHUNK), F32),
                   jax.ShapeDtypeStruct((CHUNK, D), F32)]
        + [jax.ShapeDtypeStruct(p.shape, p.dtype) for p in parts],
        scratch_shapes=[pltpu.VMEM((tm, D), BF), pltpu.VMEM((tm, D), F32), pltpu.VMEM((tm, D), BF),
                        pltpu.VMEM((tm, D), F32), pltpu.VMEM((2, RC, D), F32), pltpu.VMEM((CHUNK, D), F32)]
        + _gather_sems(n),
        compiler_params=_params(("arbitrary",)),
    )(dx1b, proj, proj, pa, pb, lng, lnb, wtr, wtrt, bias, wpa, wpb, wo, *parts)
    return outs[:7], outs[7:]


def _mixer_bwd_seq_call(dprojb, dya, proj, h, x, dx1, gates, gmix, w_in_all, cw, lam, wax, parts):
    s = dya.shape[0]
    tm = min(TM_MIX, s)
    nt = s // tm
    tb = tm // 8
    n = len(parts)
    nb, _, wb = w_in_all.shape

    def body(dpb_ref, dya_ref, xg_ref, xh8_ref, h_ref, hh8_ref, x_ref, dx1_ref, xc_s, r_s, ig_s, a_s, m_s,
             gmix_ref, win_ref, cw_ref, lam_ref, wax_ref, *rest):
        ex_in = rest[:n]
        dpa_ref, dx_ref, vec_ref, dwax_ref = rest[n:n + 4]
        ex_out = rest[n + 4:2 * n + 4]
        lm_s, dh_s, dxc_s, c_s, accs, e_send, e_recv, e_local = rest[2 * n + 4:]
        i = pl.program_id(0)
        exchange = _Exchange(ex_in, ex_out, e_send, e_recv, e_local)

        @pl.when(i == 0)
        def _():
            exchange.start()
            accs[...] = jnp.zeros(accs.shape, F32)
            dwax_ref[...] = jnp.zeros(dwax_ref.shape, F32)
            c_s[...] = jnp.zeros((8, D), F32)
            dxc_s[tm:tm + 8, :] = jnp.zeros((8, D), F32)

        first_tile = i == nt - 1
        prev8 = jnp.where(first_tile, 0.0, xh8_ref[...])
        hprev8 = jnp.where(first_tile, 0.0, hh8_ref[...])
        lamv = lam_ref[...]
        sp = _softplus(-lamv)
        hv = h_ref[...]
        g, dg = _gelu_grad(xg_ref[:, D:2 * D])
        dya = dya_ref[...]
        lm_s[...] = dya * g
        drg = ((dya * hv) * dg).astype(BF)
        dpa_ref[:, D:2 * D] = drg
        dpa_ref[:, 2 * D:6 * D] = dpb_ref[...]

        def dpb_block(k):
            return _dot_nt(dpb_ref[:, k * wb - 2 * D:(k + 1) * wb - 2 * D], win_ref[k])
        dh = (_dot_nt(drg[:, 0:2 * wb - D], win_ref[1, :, D - wb:wb])
              + _dot_nt(drg[:, 2 * wb - D:D], win_ref[2, :, 0:2 * D - 2 * wb])
              + _dot_nt(dpb_ref[:, 0:3 * wb - 2 * D], win_ref[2, :, 2 * D - 2 * wb:wb]))
        for k in range(3, 6):
            dh = dh + dpb_block(k)
        dh_s[...] = dh

        c_s[0:1, :] = _scan_bwd(a_s, lm_s, c_s[0:1, :], tm)

        hprev = _shift_back(hprev8, h_ref[...], 1)
        for q in range(NQ):
            cs = slice(q * QW, (q + 1) * QW)
            r = r_s[:, cs]
            ig = ig_s[:, cs]
            a = a_s[:, cs]
            m = m_s[:, cs]
            lm = lm_s[:, cs]
            xq = xc_s[:, cs]
            dixc = lm * m
            dla = (lm * hprev[:, cs]) * a - ((lm * (ig * xq)) * (a * a)) / m
            accs[3, :, cs] += _fold(dla * r)
            dza = (dla * (-RG_C * sp[:, cs])) * (r * (1.0 - r))
            dzx = (dixc * xq) * (ig * (1.0 - ig))
            accs[1, :, cs] += _fold(dza)
            accs[2, :, cs] += _fold(dzx)
            dz = jnp.concatenate([dza, dzx], axis=1).astype(BF)
            dxc_s[0:tm, cs] = dixc * ig + _dot_nt(dz, wax_ref[q])
            dwax_ref[q] += _dot_tn(xq.astype(BF), dz)

        cur = dxc_s[0:tm, :]
        nx = dxc_s[tm:tm + 8, :]
        drx = cw_ref[3:4, :] * cur
        for j in (1, 2, 3):
            drx = drx + cw_ref[3 - j:4 - j, :] * _shift_fwd(cur, nx, j)
        accs[0] += _fold(cur)
        rx = xg_ref[:, 0:D]
        accs[7] += _fold(cur * rx)
        for j in (1, 2, 3):
            accs[7 - j] += _fold(cur * _shift_back(prev8, rx, j))
        dxc_s[tm:tm + 8, :] = cur[0:8, :]

        drxb = drx.astype(BF)
        dpa_ref[:, 0:D] = drxb
        dh = dh_s[...] + _dot_nt(drxb[:, 0:wb], win_ref[0]) + _dot_nt(drxb[:, wb:D], win_ref[1, :, 0:D - wb])
        for k in range(6, nb):
            dh = dh + dpb_block(k)
        xx = x_ref[...]
        rn = lax.rsqrt(jnp.mean(xx * xx, axis=-1, keepdims=True) + EPS)
        xhn = xx * rn
        accs[8] += _fold(dh * xhn)
        dxh = dh * gmix_ref[...]
        dx_ref[...] = dx1_ref[...] + rn * (dxh - xhn * jnp.mean(dxh * xhn, axis=-1, keepdims=True))

        @pl.when(i == nt - 1)
        def _():
            vec_ref[...] = jnp.zeros((16, D), F32)
            for j in range(9):
                vec_ref[j:j + 1, :] = jnp.sum(accs[j], axis=0, keepdims=True)
            vec_ref[3:4, :] = vec_ref[3:4, :] * (RG_C * _sigmoid(-lamv))
            exchange.finish()

    rev = lambda w: pl.BlockSpec((tm, w), lambda i: (nt - 1 - i, 0))
    halo = pl.BlockSpec((8, D), lambda i: (jnp.maximum((nt - 1 - i) * tb - 1, 0), 0))
    vec = _const_spec((1, D))
    any_spec = pl.BlockSpec(memory_space=pl.ANY)
    outs = pl.pallas_call(
        body, name="mixer_bwd_seq", grid=(nt,),
        in_specs=[rev(4 * D), rev(D), rev(2 * D), halo, rev(D), halo] + [rev(D)] * 7
        + [vec, _const_spec(w_in_all.shape), _const_spec((4, D)), vec, _const_spec(wax.shape)] + [any_spec] * n,
        out_specs=[rev(6 * D), rev(D), pl.BlockSpec((16, D), lambda i: (0, 0)),
                   pl.BlockSpec((NQ, QW, 2 * QW), lambda i: (0, 0, 0))] + [any_spec] * n,
        out_shape=[jax.ShapeDtypeStruct((s, 6 * D), BF), jax.ShapeDtypeStruct((s, D), F32),
                   jax.ShapeDtypeStruct((16, D), F32), jax.ShapeDtypeStruct((NQ, QW, 2 * QW), F32)]
        + [jax.ShapeDtypeStruct(p.shape, p.dtype) for p in parts],
        scratch_shapes=[pltpu.VMEM((tm, D), F32), pltpu.VMEM((tm, D), F32),
                        pltpu.VMEM((tm + 8, D), F32), pltpu.VMEM((8, D), F32), pltpu.VMEM((9, RC, D), F32)]
        + _gather_sems(n),
        compiler_params=_params(("arbitrary",)),
    )(dprojb, dya, proj, proj, h, h, x, dx1, *gates, gmix, w_in_all, cw, lam, wax, *parts)
    return outs[:4], outs[4:]


def _device_of(d):
    return (d // 4, lax.rem(d // 2, 2), lax.rem(d, 2))


def _dw_exchange_call(name, me, a, b, a_spec, b_spec, split, k1, n1, s, small):
    nb = NDEV // split
    r = k1 // split
    ts = min(TS_DW, s)
    ns = s // ts

    def slab(i, me_ref):
        return lax.rem(me_ref[0] // split + 1 + i, nb)

    def body(me_ref, a_ref, b_ref, g_ref, own_ref, recv_ref, gsum_ref, acc, sbuf, send_sems, recv_sems, local_sem,
             rbuf, gacc, send1, recv1, send2, recv2):
        i = pl.program_id(0)
        j = pl.program_id(1)
        me = _lin(_me())
        allreduce = _SmallAllReduce(g_ref, gacc, rbuf, send1, recv1, send2, recv2)
        pl.when((i == 0) & (j == 0))(allreduce.scatter)
        pl.when((i == nb // 2) & (j == 0))(allreduce.reduce)
        p = _dot_tn(a_ref[...], b_ref[...])

        @pl.when(j == 0)
        def _():
            acc[...] = p

        @pl.when(j > 0)
        def _():
            acc[...] += p

        def send(step, h):
            d = slab(step, me_ref) * split + h
            cp = pltpu.make_async_remote_copy(
                src_ref=sbuf.at[lax.rem(step, 2), pl.ds(h * r, r)], dst_ref=recv_ref.at[me],
                send_sem=send_sems.at[d], recv_sem=recv_sems.at[me],
                device_id=_device_of(d), device_id_type=MESH)
            return cp, d != me

        def drain(step):
            for h in range(split):
                cp, sent = send(step, h)
                pl.when(sent)(cp.wait_send)

        @pl.when(j == ns - 1)
        def _():
            pl.when(i >= 2)(lambda: drain(i - 2))
            sbuf[lax.rem(i, 2)] = acc[...].astype(BF)
            for h in range(split):
                cp, sent = send(i, h)
                pl.when(sent)(cp.start)

            @pl.when(i == nb - 1)
            def _():
                off = pl.multiple_of(lax.rem(me, split) * r, RC)
                own_ref[...] = acc[pl.ds(off, r), :]
                mine = pltpu.make_async_copy(sbuf.at[lax.rem(i, 2), pl.ds(off, r)], recv_ref.at[me], local_sem)
                mine.start()
                drain(i - 1)
                drain(i)
                for d in range(NDEV):
                    wait = pltpu.make_async_remote_copy(
                        src_ref=sbuf.at[0, pl.ds(0, r)], dst_ref=recv_ref.at[d],
                        send_sem=send_sems.at[d], recv_sem=recv_sems.at[d],
                        device_id=_device_of(d), device_id_type=MESH).wait_recv
                    pl.when(d != me)(wait)
                mine.wait()
                allreduce.finish()
                gsum_ref[...] = gacc[...]

    vmem = pl.BlockSpec(memory_space=pltpu.VMEM)
    grid_spec = pltpu.PrefetchScalarGridSpec(
        num_scalar_prefetch=1, grid=(nb, ns),
        in_specs=[a_spec(ts, slab), b_spec(ts, slab), vmem],
        out_specs=[pl.BlockSpec((r, n1), lambda i, j, me_ref: (0, 0)), pl.BlockSpec(memory_space=pl.ANY), vmem],
        scratch_shapes=[pltpu.VMEM((k1, n1), F32), pltpu.VMEM((2, k1, n1), BF),
                        pltpu.SemaphoreType.DMA((NDEV,)), pltpu.SemaphoreType.DMA((NDEV,)),
                        pltpu.SemaphoreType.DMA(()), pltpu.VMEM((NDEV, SMALL_PER, D), F32),
                        pltpu.VMEM((SMALL_ROWS, D), F32)]
        + [pltpu.SemaphoreType.DMA((NDEV - 1,))] * 4)
    return pl.pallas_call(
        body, name=name, grid_spec=grid_spec,
        out_shape=[jax.ShapeDtypeStruct((r, n1), F32), jax.ShapeDtypeStruct((NDEV, r, n1), BF),
                   jax.ShapeDtypeStruct((SMALL_ROWS, D), F32)],
        compiler_params=_params(("arbitrary", "arbitrary")),
    )(me, a, b, small)


def _dw_plain_call(name, me, a, b, a_spec, b_spec, split, k1, n1, s):
    nb = NDEV // split
    r = k1 // split
    ts = min(TS_DW, s)
    ns = s // ts

    def slab(i, me_ref):
        return i

    def body(me_ref, a_ref, b_ref, own_ref, part_ref, acc):
        i = pl.program_id(0)
        j = pl.program_id(1)
        p = _dot_tn(a_ref[...], b_ref[...])

        @pl.when(j == 0)
        def _():
            acc[...] = p

        @pl.when(j > 0)
        def _():
            acc[...] += p

        @pl.when(j == ns - 1)
        def _():
            part_ref[...] = acc[...].astype(BF)

            @pl.when(i == me_ref[0] // split)
            def _():
                off = pl.multiple_of(lax.rem(me_ref[0], split) * r, RC)
                own_ref[...] = acc[pl.ds(off, r), :]

    grid_spec = pltpu.PrefetchScalarGridSpec(
        num_scalar_prefetch=1, grid=(nb, ns),
        in_specs=[a_spec(ts, slab), b_spec(ts, slab)],
        out_specs=[pl.BlockSpec((r, n1), lambda i, j, me_ref: (0, 0)),
                   pl.BlockSpec((None, k1, n1), lambda i, j, me_ref: (i, 0, 0))],
        scratch_shapes=[pltpu.VMEM((k1, n1), F32)])
    own, part = pl.pallas_call(
        body, name=name, grid_spec=grid_spec,
        out_shape=[jax.ShapeDtypeStruct((r, n1), F32), jax.ShapeDtypeStruct((nb, k1, n1), BF)],
        compiler_params=_params(("arbitrary", "arbitrary")),
    )(me, a, b)
    return own, part.reshape(NDEV, r, n1)


def _rows2d(w):
    return lambda ts, slab: pl.BlockSpec((ts, w), lambda i, j, me_ref: (j, 0))


def _cols2d(w):
    return lambda ts, slab: pl.BlockSpec((ts, w), lambda i, j, me_ref: (j, slab(i, me_ref)))


def _blk3d(w):
    return lambda ts, slab: pl.BlockSpec((None, ts, w), lambda i, j, me_ref: (slab(i, me_ref), j, 0))


_BC1 = 1.0 - ADAM_B1 ** ADAM_STEP
_BC2 = 1.0 - ADAM_B2 ** ADAM_STEP


def _adamw_math(w, g, m, v):
    m = ADAM_B1 * m + (1.0 - ADAM_B1) * g
    v = ADAM_B2 * v + (1.0 - ADAM_B2) * (g * g)
    m_hat = m / _BC1
    v_hat = v / _BC2
    delta = -ADAM_LR * (m_hat / (jnp.sqrt(v_hat) + ADAM_EPS) + ADAM_WD * w)
    return delta, m, v


def _row_tile(r):
    for t in (256, 176, 128, 64, 32, 16, 8):
        if r % t == 0:
            return t
    return r


def _reduce_adamw_call(name, me, own, recv, w, m, v):
    r, c = own.shape
    tr = _row_tile(r)

    def body(me_ref, own_ref, recv_ref, w_ref, m_ref, v_ref, g_ref, d_ref, nm_ref, nv_ref):
        mine = me_ref[0]
        g = jnp.zeros((tr, c), F32)
        for sdev in range(NDEV):
            g = g + jnp.where(mine == sdev, own_ref[...], recv_ref[sdev].astype(F32))
        g_ref[...] = g
        d_ref[...], nm_ref[...], nv_ref[...] = _adamw_math(w_ref[...], g, m_ref[...], v_ref[...])

    tile = pl.BlockSpec((tr, c), lambda i, me_ref: (i, 0))
    grid_spec = pltpu.PrefetchScalarGridSpec(
        num_scalar_prefetch=1, grid=(r // tr,),
        in_specs=[tile, pl.BlockSpec((NDEV, tr, c), lambda i, me_ref: (0, i, 0)), tile, tile, tile],
        out_specs=[tile] * 4)
    return pl.pallas_call(
        body, name=name, grid_spec=grid_spec,
        out_shape=[jax.ShapeDtypeStruct((r, c), F32)] * 4,
        compiler_params=_params(("parallel",)),
    )(me, own, recv, w, m, v)


def _adamw_call(name, w, g, m, v):
    r, c = w.shape
    tr = _row_tile(r)

    def body(w_ref, g_ref, m_ref, v_ref, d_ref, nm_ref, nv_ref):
        d_ref[...], nm_ref[...], nv_ref[...] = _adamw_math(w_ref[...], g_ref[...], m_ref[...], v_ref[...])

    tile = pl.BlockSpec((tr, c), lambda i: (i, 0))
    return pl.pallas_call(
        body, name=name, grid=(r // tr,), in_specs=[tile] * 4, out_specs=[tile] * 3,
        out_shape=[jax.ShapeDtypeStruct((r, c), F32)] * 3,
        compiler_params=_params(("parallel",)),
    )(w, g, m, v)


def _me():
    return lax.axis_index("x"), lax.axis_index("y"), lax.axis_index("c")


def _flip(pos, r):
    x, y, c = pos
    return (1 - x if r & 4 else x, 1 - y if r & 2 else y, 1 - c if r & 1 else c)


def _lin(pos):
    return pos[0] * 4 + pos[1] * 2 + pos[2]


class _SmallAllReduce:
    def __init__(self, g_ref, out_ref, rbuf, send1, recv1, send2, recv2):
        self.g, self.out, self.rbuf = g_ref, out_ref, rbuf
        self.sems = (send1, recv1, send2, recv2)
        self.me = _me()
        self.mi = _lin(self.me)

    @staticmethod
    def _rows(d):
        return pl.ds(pl.multiple_of(d * SMALL_PER, 8), SMALL_PER)

    def _scatter(self, r, outgoing):
        peer = _flip(self.me, r)
        src, dst = (_lin(peer), self.mi) if outgoing else (self.mi, _lin(peer))
        return pltpu.make_async_remote_copy(
            src_ref=self.g.at[self._rows(src)], dst_ref=self.rbuf.at[dst],
            send_sem=self.sems[0].at[r - 1], recv_sem=self.sems[1].at[r - 1], device_id=peer, device_id_type=MESH)

    def _spread(self, r, outgoing):
        peer = _flip(self.me, r)
        rows = self._rows(self.mi if outgoing else _lin(peer))
        return pltpu.make_async_remote_copy(
            src_ref=self.out.at[rows], dst_ref=self.out.at[rows],
            send_sem=self.sems[2].at[r - 1], recv_sem=self.sems[3].at[r - 1], device_id=peer, device_id_type=MESH)

    def scatter(self):
        for r in range(1, NDEV):
            self._scatter(r, True).start()
        self.rbuf[self.mi] = self.g[self._rows(self.mi), :]

    def reduce(self):
        for r in range(1, NDEV):
            self._scatter(r, False).wait_recv()
        for r in range(1, NDEV):
            self._scatter(r, True).wait_send()
        tot = self.rbuf[0]
        for d in range(1, NDEV):
            tot = tot + self.rbuf[d]
        self.out[self._rows(self.mi), :] = tot
        for r in range(1, NDEV):
            self._spread(r, True).start()

    def finish(self):
        for r in range(1, NDEV):
            self._spread(r, False).wait_recv()
        for r in range(1, NDEV):
            self._spread(r, True).wait_send()


def _head_blocks(w):
    z = jnp.zeros((64, 64), w.dtype)
    groups = []
    for q in range(NQ):
        rows = [jnp.concatenate([w[4 * q + a] if a == b else z for b in range(4)], axis=1) for a in range(4)]
        groups.append(jnp.concatenate(rows, axis=0))
    return jnp.stack(groups)


def _head_unblocks(g):
    return jnp.stack([g[q, 64 * a:64 * a + 64, 64 * a:64 * a + 64] for q in range(NQ) for a in range(4)])


def _local_step(x, tgt, p, me, order):
    s = x.shape[0]
    vec = lambda a: a.reshape(1, D)
    gmix, gffn, gfin = vec(p["norm_mix_g"]), vec(p["norm_ffn_g"]), vec(p["norm_final_g"])
    cb, ba, bx, lam = vec(p["conv_b"]), vec(p["rg_ba"]), vec(p["rg_bx"]), vec(p["rg_lambda"])
    lng, lnb = vec(p["sgu_ln_g"]), vec(p["sgu_ln_b"])
    wax = jnp.concatenate([_head_blocks(p["rg_wa"]), _head_blocks(p["rg_wx"])], axis=2).astype(BF)
    tril = jnp.tril(jnp.ones((CHUNK, CHUNK), bool))
    ws = jnp.where(tril[None], p["sgu_ws"], 0.0)
    wtr = ws.astype(BF)
    wtrt = jnp.swapaxes(ws, 1, 2).astype(BF)
    bias = jnp.repeat(p["sgu_bs"].T, GW, axis=1)
    shard = {k: p[k].astype(BF) for k in _BIG}

    proj, h1, w_in, (wpa, wpb, wo, cw) = _proj_gather_call(
        x, gmix, shard["w_in"], order, [shard["w_proj_a"], shard["w_proj_b"], shard["w_out"], p["conv_w"]])
    wpa, wpb, wo = (t.reshape(D, D) for t in (wpa, wpb, wo))
    cw = jnp.swapaxes(cw, 0, 1).reshape(4, D)
    (h, pa, pb, x1, *gates, ya, yb, mg), (wgu, wdn) = _mixer_fwd_call(
        proj, x, cw, cb, ba, bx, lam, lng, lnb, wax, wtr, bias, wpa, wpb, wo, [shard["w_gate_up"], shard["w_down"]])
    wdn = wdn.reshape(NDEV // 2, -1, D)
    nb, _, wb = w_in.shape
    fb = wgu.shape[2]
    nh = wdn.shape[0]
    dx1, dx1b, h2, act, dgu, dx2b, facc = _ffn_call(x1, tgt, gffn, gfin, wgu, wdn)
    assert nb == NDEV and 2 * nh == NDEV
    own_gu, part_gu = _dw_plain_call("dw_gate_up", me, dgu, h2, _blk3d(fb), _rows2d(D), 1, fb, D, s)
    own_dn, part_dn = _dw_plain_call("dw_down", me, act, dx2b, _blk3d(fb), _rows2d(D), 2, fb, D, s)
    (dprojb, dya, dpa, dpb, bvec, dws, dbs), (recv_dn,) = _mixer_bwd_pre_call(
        dx1b, proj, pa, pb, lng, lnb, wtr, wtrt, bias, wpa, wpb, wo, [part_dn])
    own_pa, part_pa = _dw_plain_call("dw_proj_a", me, ya, dpa, _rows2d(D), _rows2d(D), NDEV, D, D, s)
    own_pb, part_pb = _dw_plain_call("dw_proj_b", me, yb, dpb, _rows2d(D), _rows2d(D), NDEV, D, D, s)
    own_wo, part_wo = _dw_plain_call("dw_out", me, mg, dx1b, _rows2d(D), _rows2d(D), NDEV, D, D, s)
    (dproj, dx, svec, dwax), (recv_gu, recv_pa, recv_pb, recv_wo) = _mixer_bwd_seq_call(
        dprojb, dya, proj, h, x, dx1, gates, gmix, w_in, cw, lam, wax, [part_gu, part_pa, part_pb, part_wo])
    small = {
        "norm_mix_g": svec[8], "norm_ffn_g": facc[0], "norm_final_g": facc[1],
        "conv_b": svec[0], "rg_ba": svec[1], "rg_bx": svec[2], "rg_lambda": svec[3],
        "sgu_ln_g": bvec[0], "sgu_ln_b": bvec[1],
        "rg_wa": _head_unblocks(dwax[:, :, 0:QW]), "rg_wx": _head_unblocks(dwax[:, :, QW:2 * QW]),
        "sgu_ws": dws, "sgu_bs": dbs[:, ::GW].T,
    }
    packed = _pack_small(small, svec[4:8], facc[2])
    own_in, recv_in, gsum = _dw_exchange_call("dw_in", me, h1, dproj, _rows2d(D), _cols2d(wb), 1, D, wb, s, packed)
    dw = {
        "w_gate_up": (own_gu, recv_gu), "w_down": (own_dn, recv_dn), "w_proj_a": (own_pa, recv_pa),
        "w_proj_b": (own_pb, recv_pb), "w_out": (own_wo, recv_wo), "w_in": (own_in, recv_in),
    }
    return gsum, dx, dw


_BIG = ("w_in", "w_gate_up", "w_down", "w_proj_a", "w_proj_b", "w_out")
_VEC_ROWS = ("norm_mix_g", "norm_ffn_g", "norm_final_g", "conv_b", "rg_ba", "rg_bx", "rg_lambda",
             "sgu_ln_g", "sgu_ln_b", "sgu_bs")
_WEIGHTS = ("norm_mix_g", "w_in", "conv_w", "conv_b", "rg_wa", "rg_ba", "rg_wx", "rg_bx", "rg_lambda",
            "sgu_ln_g", "sgu_ln_b", "sgu_ws", "sgu_bs", "w_proj_a", "w_proj_b", "w_out", "norm_ffn_g",
            "w_gate_up", "w_down", "norm_final_g")


def _pack_small(t, conv_w, extra=None):
    extra = jnp.zeros((1, D), F32) if extra is None else extra.reshape(1, D)
    head = jnp.concatenate([t[k].reshape(1, D) for k in _VEC_ROWS] + [conv_w, extra, jnp.zeros((1, D), F32)], axis=0)
    return jnp.concatenate([head, t["rg_wa"].reshape(64, D), t["rg_wx"].reshape(64, D), t["sgu_ws"].reshape(128, D),
                            jnp.zeros((SMALL_ROWS - 272, D), F32)], axis=0)


def _unpack_small(a):
    out = {k: a[j] for j, k in enumerate(_VEC_ROWS)}
    out["conv_w"] = a[10:14]
    out["rg_wa"] = a[16:80].reshape(16, 64, 64)
    out["rg_wx"] = a[80:144].reshape(16, 64, 64)
    out["sgu_ws"] = a[144:272].reshape(NGRP, CHUNK, CHUNK)
    return out


def kernel(x, norm_mix_g, w_in, conv_w, conv_b, rg_wa, rg_ba, rg_wx, rg_bx, rg_lambda, sgu_ln_g, sgu_ln_b, sgu_ws, sgu_bs, w_proj_a, w_proj_b, w_out, norm_ffn_g, w_gate_up, w_down, norm_final_g, loss_target, m_norm_mix_g, m_w_in, m_conv_w, m_conv_b, m_rg_wa, m_rg_ba, m_rg_wx, m_rg_bx, m_rg_lambda, m_sgu_ln_g, m_sgu_ln_b, m_sgu_ws, m_sgu_bs, m_w_proj_a, m_w_proj_b, m_w_out, m_norm_ffn_g, m_w_gate_up, m_w_down, m_norm_final_g, v_norm_mix_g, v_w_in, v_conv_w, v_conv_b, v_rg_wa, v_rg_ba, v_rg_wx, v_rg_bx, v_rg_lambda, v_sgu_ln_g, v_sgu_ln_b, v_sgu_ws, v_sgu_bs, v_w_proj_a, v_w_proj_b, v_w_out, v_norm_ffn_g, v_w_gate_up, v_w_down, v_norm_final_g):
    args = dict(locals())
    w = {k: args[k] for k in _WEIGHTS}
    m = {k: args["m_" + k] for k in _WEIGHTS}
    v = {k: args["v_" + k] for k in _WEIGHTS}
    for d in (w, m, v):
        for k in _WEIGHTS:
            if k != "norm_final_g":
                d[k] = d[k][0]
    me = _lin(_me())
    me1 = me.reshape(1).astype(jnp.int32)

    order = jnp.bitwise_xor(me, jnp.array(_PASS_FLIPS, jnp.int32)).astype(jnp.int32)

    gsum, dx, dw = _local_step(x[0], loss_target[0], w, me1, order)

    grads, delta, new_m, new_v = {}, {}, {}, {}
    for k in _BIG:
        own, recv = dw[k]
        flip = own.shape != w[k].shape
        wmv = [jnp.swapaxes(t, 0, 1) if flip else t for t in (w[k], m[k], v[k])]
        res = _reduce_adamw_call("adamw_" + k, me1, own, recv, *wmv)
        grads[k], delta[k], new_m[k], new_v[k] = (jnp.swapaxes(t, 0, 1) if flip else t for t in res)

    loss = (0.5 / D) * jnp.sum(gsum[14])
    zc = jnp.zeros((4, D), F32)
    d_s, m_s, v_s = _adamw_call("adamw_small", _pack_small(w, zc), gsum, _pack_small(m, zc), _pack_small(v, zc))
    gs, ds, ms, vs = _unpack_small(gsum), _unpack_small(d_s), _unpack_small(m_s), _unpack_small(v_s)
    g_cw = lax.dynamic_slice(gs["conv_w"], (0, me * 128), (4, 128))
    ds["conv_w"], ms["conv_w"], vs["conv_w"] = _adamw_call("adamw_conv_w", w["conv_w"], g_cw, m["conv_w"], v["conv_w"])
    gs["conv_w"] = g_cw
    for k in _WEIGHTS:
        if k not in _BIG:
            shp = w[k].shape
            grads[k], delta[k], new_m[k], new_v[k] = (t[k].reshape(shp) for t in (gs, ds, ms, vs))

    def lift(t, k):
        return t[k] if k == "norm_final_g" else t[k][None]

    outs = [loss, dx[None]]
    for t in (grads, delta, new_m, new_v):
        outs += [lift(t, k) for k in _WEIGHTS]
    return tuple(outs)
```

```python
import functools

import jax
import jax.numpy as jnp
from jax import lax
from jax.experimental import pallas as pl
from jax.experimental.pallas import tpu as pltpu

F32 = jnp.float32
BF = jnp.bfloat16

D = 1024
NDEV = 8
EPS = 1e-6
RG_C = 8.0
CHUNK = 128
NGRP = 8
GW = 128
NQ = 4
QW = 256
RC = 16
SMALL_ROWS = 320
SMALL_PER = SMALL_ROWS // NDEV

ADAM_LR = 0.001
ADAM_B1 = 0.9
ADAM_B2 = 0.999
ADAM_EPS = 1e-08
ADAM_WD = 0.01
ADAM_STEP = 10

VMEM_LIMIT = 60 * 1024 * 1024

MESH = pl.DeviceIdType.MESH


def _rows(n, fn, unroll=2, rc=RC):
    def body(i, c):
        fn(pl.multiple_of(i * rc, rc))
        return c
    lax.fori_loop(0, n // rc, body, 0, unroll=unroll)


def _fold(v):
    return jnp.sum(v.reshape(v.shape[0] // RC, RC, v.shape[1]), axis=0)


def _dot(a, b):
    return jnp.dot(a, b, preferred_element_type=F32)


def _dot_nt(a, b):
    return lax.dot_general(a, b, (((1,), (1,)), ((), ())), preferred_element_type=F32)


def _dot_tn(a, b):
    return lax.dot_general(a, b, (((0,), (0,)), ((), ())), preferred_element_type=F32)


_GC = 0.7978845608028654
_GK = 0.044715


def _gelu(x):
    t = jnp.tanh(_GC * (x + _GK * (x * x * x)))
    return x * (0.5 * (1.0 + t))


def _gelu_grad(x):
    x2 = x * x
    t = jnp.tanh(_GC * (x + _GK * (x2 * x)))
    cdf = 0.5 * (1.0 + t)
    dg = cdf + (0.5 * x) * (1.0 - t * t) * (_GC * (1.0 + (3.0 * _GK) * x2))
    return x * cdf, dg


def _sigmoid(x):
    return jax.nn.sigmoid(x)


def _log1p(e):
    u = 1.0 + e
    d = u - 1.0
    return jnp.where(d == 0.0, e, jnp.log(u) * (e / jnp.where(d == 0.0, 1.0, d)))


def _softplus(z):
    return jnp.maximum(z, 0.0) + _log1p(jnp.exp(-jnp.abs(z)))


def _neg_expm1(z):
    u = jnp.exp(z)
    lu = jnp.log(u)
    k = (1.0 - u) * (z / jnp.where(lu == 0.0, 1.0, lu))
    small = jnp.where(lu == 0.0, -z, k)
    return jnp.where(z > -0.5, small, 1.0 - u)


def _shift_back(prev8, cur, j):
    cat = jnp.concatenate([prev8, cur], axis=0)
    return pltpu.roll(cat, j, 0)[8:8 + cur.shape[0]]


def _shift_fwd(cur, next8, j):
    cat = jnp.concatenate([cur, next8], axis=0)
    n = cat.shape[0]
    return pltpu.roll(cat, n - j, 0)[0:cur.shape[0]]


def _const_spec(shape):
    nd = len(shape)
    return pl.BlockSpec(shape, lambda *_: (0,) * nd, pipeline_mode=pl.Buffered(1))


def _params(sem):
    return pltpu.CompilerParams(dimension_semantics=sem, vmem_limit_bytes=VMEM_LIMIT)


TM_PROJ = 1024
TM_MIX = 256
TM_FFN = 256
TM_DX = 512
TS_DW = 4096


_CHIPS = (4, 2, 6)
_PASS_FLIPS = (0, 1, 4, 2, 6, 5, 3, 7)


class _Gather:
    def __init__(self, ins, outs, send_sems, recv_sems, local_sems):
        self.ins, self.outs = ins, outs
        self.send_sems, self.recv_sems, self.local_sems = send_sems, recv_sems, local_sems
        self.me = _me()
        self.sibling = _flip(self.me, 1)

    def _copy(self, a, kind, block, to, src=None):
        dst = self.outs[a].at[_lin(block)]
        return pltpu.make_async_remote_copy(
            src_ref=dst if src is None else src, dst_ref=dst,
            send_sem=self.send_sems.at[a, kind], recv_sem=self.recv_sems.at[a, kind],
            device_id=to, device_id_type=MESH)

    def _local(self, a):
        return pltpu.make_async_copy(self.ins[a], self.outs[a].at[_lin(self.me)], self.local_sems.at[a])

    def start(self):
        for a in range(len(self.ins)):
            self._local(a).start()
            self._copy(a, 0, self.me, self.sibling, src=self.ins[a]).start()
            for j, f in enumerate(_CHIPS):
                self._copy(a, 1 + j, self.me, _flip(self.me, f), src=self.ins[a]).start()

    def forward(self):
        for j, f in enumerate(_CHIPS):
            for a in range(len(self.ins)):
                self._copy(a, 1 + j, _flip(self.me, f), self.me).wait_recv()
                self._copy(a, 4 + j, _flip(self.me, f), self.sibling).start()

    def finish(self):
        for a in range(len(self.ins)):
            self._copy(a, 0, self.sibling, self.me).wait_recv()
            for j, f in enumerate(_CHIPS):
                self._copy(a, 4 + j, _flip(self.me, f | 1), self.me).wait_recv()
            self._copy(a, 0, self.me, self.sibling, src=self.ins[a]).wait_send()
            for j, f in enumerate(_CHIPS):
                self._copy(a, 1 + j, self.me, _flip(self.me, f), src=self.ins[a]).wait_send()
                self._copy(a, 4 + j, _flip(self.me, f), self.sibling).wait_send()
            self._local(a).wait()


class _Exchange:
    def __init__(self, ins, outs, send_sems, recv_sems, local_sems):
        self.ins, self.outs = ins, outs
        self.send_sems, self.recv_sems, self.local_sems = send_sems, recv_sems, local_sems
        self.me = _me()

    def _copy(self, a, r, outgoing):
        peer = _flip(self.me, r)
        src, dst = (peer, self.me) if outgoing else (self.me, peer)
        return pltpu.make_async_remote_copy(
            src_ref=self.ins[a].at[_lin(src)], dst_ref=self.outs[a].at[_lin(dst)],
            send_sem=self.send_sems.at[a, r - 1], recv_sem=self.recv_sems.at[a, r - 1],
            device_id=peer, device_id_type=MESH)

    def _local(self, a):
        mi = _lin(self.me)
        return pltpu.make_async_copy(self.ins[a].at[mi], self.outs[a].at[mi], self.local_sems.at[a])

    def start(self):
        for a in range(len(self.ins)):
            self._local(a).start()
        for r in range(1, NDEV):
            for a in range(len(self.ins)):
                self._copy(a, r, True).start()

    def finish(self):
        for r in range(1, NDEV):
            for a in range(len(self.ins)):
                self._copy(a, r, False).wait_recv()
        for r in range(1, NDEV):
            for a in range(len(self.ins)):
                self._copy(a, r, True).wait_send()
        for a in range(len(self.ins)):
            self._local(a).wait()


def _gather_sems(n):
    return [pltpu.SemaphoreType.DMA((n, 7)), pltpu.SemaphoreType.DMA((n, 7)), pltpu.SemaphoreType.DMA((n,))]


def _proj_gather_call(x, gmix, w_shard, order, extras):
    s = x.shape[0]
    tm = min(TM_PROJ, s)
    nt = s // tm
    wb = w_shard.shape[1]
    n = len(extras)

    def body(order_ref, x_ref, g_ref, wsh_ref, *rest):
        ex_in = rest[:n]
        proj_ref, h1_hbm, wall_hbm = rest[n:n + 3]
        ex_out = rest[n + 3:2 * n + 3]
        wall, hc, ws_send, ws_recv, ex_send, ex_recv, ex_local, out_sems = rest[2 * n + 3:]
        k = pl.program_id(0)
        i = pl.program_id(1)
        me = _me()
        sibling = _flip(me, 1)
        gather = _Gather(ex_in, ex_out, ex_send, ex_recv, ex_local)

        def wcopy(kind, block, to):
            ref = wall.at[_lin(block)]
            return pltpu.make_async_remote_copy(
                src_ref=ref, dst_ref=ref, send_sem=ws_send.at[kind], recv_sem=ws_recv.at[kind],
                device_id=to, device_id_type=MESH)

        head = i == 0

        @pl.when(head & (k == 0))
        def _():
            wall[_lin(me)] = wsh_ref[...]
            wcopy(0, me, sibling).start()
            for j, f in enumerate(_CHIPS):
                wcopy(1 + j, me, _flip(me, f)).start()
            gather.start()

        @pl.when(head & (k == 1))
        def _():
            wcopy(0, sibling, me).wait_recv()

        for j, f in enumerate(_CHIPS):
            @pl.when(head & (k == 2 + j))
            def _(j=j, f=f):
                wcopy(1 + j, _flip(me, f), me).wait_recv()
                wcopy(4 + j, _flip(me, f), sibling).start()

            @pl.when(head & (k == 5 + j))
            def _(j=j, f=f):
                wcopy(4 + j, _flip(me, f | 1), me).wait_recv()

        pl.when(head & (k == 5))(gather.forward)

        base = pl.multiple_of(i * tm, tm)

        @pl.when(k == 0)
        def _():
            g = g_ref[...]

            def norm(r0):
                xx = x_ref[pl.ds(r0, RC), :]
                r = lax.rsqrt(jnp.mean(xx * xx, axis=-1, keepdims=True) + EPS)
                hc[pl.ds(base + r0, RC), :] = ((xx * r) * g).astype(BF)
            _rows(tm, norm, unroll=4)

        wk = wall.at[order_ref[k]]
        for r0 in range(0, tm, TM_MIX):
            rs = min(TM_MIX, tm - r0)
            proj_ref[r0:r0 + rs, :] = _dot(hc[pl.ds(base + r0, rs), :], wk[...])

        @pl.when((k == NDEV - 1) & (i == nt - 1))
        def _():
            out_w = pltpu.make_async_copy(wall, wall_hbm, out_sems.at[0])
            out_h = pltpu.make_async_copy(hc, h1_hbm, out_sems.at[1])
            out_w.start()
            out_h.start()
            wcopy(0, me, sibling).wait_send()
            for j, f in enumerate(_CHIPS):
                wcopy(1 + j, me, _flip(me, f)).wait_send()
                wcopy(4 + j, _flip(me, f), sibling).wait_send()
            gather.finish()
            out_w.wait()
            out_h.wait()

    any_spec = pl.BlockSpec(memory_space=pl.ANY)
    grid_spec = pltpu.PrefetchScalarGridSpec(
        num_scalar_prefetch=1, grid=(NDEV, nt),
        in_specs=[pl.BlockSpec((tm, D), lambda k, i, o: (jnp.where(k == 0, i, nt - 1), 0)),
                  pl.BlockSpec((1, D), lambda k, i, o: (0, 0)),
                  pl.BlockSpec(w_shard.shape, lambda k, i, o: (0, 0))] + [any_spec] * n,
        out_specs=[pl.BlockSpec((tm, wb), lambda k, i, o: (i, o[k])), any_spec, any_spec] + [any_spec] * n,
        scratch_shapes=[pltpu.VMEM((NDEV,) + w_shard.shape, BF), pltpu.VMEM((s, D), BF),
                        pltpu.SemaphoreType.DMA((7,)), pltpu.SemaphoreType.DMA((7,))] + _gather_sems(n)
        + [pltpu.SemaphoreType.DMA((2,))])
    outs = pl.pallas_call(
        body, name="proj_gather", grid_spec=grid_spec,
        out_shape=[jax.ShapeDtypeStruct((s, NDEV * wb), F32), jax.ShapeDtypeStruct((s, D), BF),
                   jax.ShapeDtypeStruct((NDEV,) + w_shard.shape, BF)]
        + [jax.ShapeDtypeStruct((NDEV,) + e.shape, e.dtype) for e in extras],
        compiler_params=_params(("arbitrary", "arbitrary")),
    )(order, x, gmix, w_shard, *extras)
    return outs[0], outs[1], outs[2], outs[3:]


def _conv_tile(rx, prev8, cw_ref, cb):
    xc = cb + cw_ref[3:4, :] * rx
    for j in (1, 2, 3):
        xc = xc + cw_ref[3 - j:4 - j, :] * _shift_back(prev8, rx, j)
    return xc


def _gate_tile(xcb, wax_ref, ba, bx, sp, q):
    cs = slice(q * QW, (q + 1) * QW)
    z = _dot(xcb[:, cs], wax_ref[q])
    r = _sigmoid(z[:, 0:QW] + ba[:, cs])
    ig = _sigmoid(z[:, QW:2 * QW] + bx[:, cs])
    return r, ig, (-RG_C * r) * sp[:, cs]


def _scan_fwd(a_s, b_s, out_ref, h0, n):
    rowi = lax.broadcasted_iota(jnp.int32, (8, D), 0)

    def block(t, h):
        rows = pl.ds(pl.multiple_of(t * 8, 8), 8)
        a = a_s[rows, :]
        b = b_s[rows, :]
        for d in (1, 2, 4):
            m = rowi >= d
            b = jnp.where(m, a * pltpu.roll(b, d, 0) + b, b)
            a = jnp.where(m, a * pltpu.roll(a, d, 0), a)
        hb = b + a * h
        out_ref[rows, :] = hb
        return hb[7:8, :]
    return lax.fori_loop(0, n // 8, block, h0, unroll=4)


def _scan_bwd(a_s, lm_s, c0, n):
    rowi = lax.broadcasted_iota(jnp.int32, (8, D), 0)
    nblk = n // 8

    def block(k, cin):
        rows = pl.ds(pl.multiple_of((nblk - 1 - k) * 8, 8), 8)
        a = a_s[rows, :]
        dh = lm_s[rows, :]
        b = a * dh
        for d in (1, 2, 4):
            m = rowi < 8 - d
            b = jnp.where(m, a * pltpu.roll(b, 8 - d, 0) + b, b)
            a = jnp.where(m, a * pltpu.roll(a, 8 - d, 0), a)
        mu = b + a * cin
        lm_s[rows, :] = dh + jnp.where(rowi < 7, pltpu.roll(mu, 7, 0), cin)
        return mu[0:1, :]
    return lax.fori_loop(0, nblk, block, c0, unroll=4)


def _mixer_fwd_call(proj, x, cw, cb, ba, bx, lam, lng, lnb, wax, wtr, bias, wpa, wpb, wo, shards):
    s = x.shape[0]
    tm = min(TM_MIX, s)
    nt = s // tm
    pw = proj.shape[1]
    n = len(shards)

    def body(proj_ref, x_ref, cw_ref, cb_ref, ba_ref, bx_ref, lam_ref, lng_ref, lnb_ref, wax_ref, wtr_ref,
             bias_ref, wpa_ref, wpb_ref, wo_ref, *rest):
        sh_in = rest[:n]
        h_ref, pa_ref, pb_ref, x1_ref, xc_ref, r_ref, ig_ref, a_s, m_ref, ya_ref, yb_ref, mg_ref = rest[n:n + 12]
        sh_out = rest[n + 12:2 * n + 12]
        prev_s, b_s, hc_s, vn_s, mx_s, g_send, g_recv, g_local = rest[2 * n + 12:]
        i = pl.program_id(0)
        gather = _Gather(sh_in, sh_out, g_send, g_recv, g_local)

        @pl.when(i == 0)
        def _():
            gather.start()
            prev_s[...] = jnp.zeros((8, D), F32)
            hc_s[...] = jnp.zeros((8, D), F32)

        pl.when(i == nt // 2)(gather.forward)

        rx = proj_ref[:, 0:D]
        xc = _conv_tile(rx, prev_s[...], cw_ref, cb_ref[...])
        prev_s[...] = rx[tm - 8:tm, :]
        xc_ref[...] = xc
        xcb = xc.astype(BF)
        sp = _softplus(-lam_ref[...])
        ba = ba_ref[...]
        bx = bx_ref[...]
        for q in range(NQ):
            cs = slice(q * QW, (q + 1) * QW)
            r, ig, la = _gate_tile(xcb, wax_ref, ba, bx, sp, q)
            mq = jnp.sqrt(_neg_expm1(2.0 * la))
            r_ref[:, cs] = r
            ig_ref[:, cs] = ig
            m_ref[:, cs] = mq
            a_s[:, cs] = jnp.exp(la)
            b_s[:, cs] = mq * (ig * xc[:, cs])

        gv = _gelu(proj_ref[:, 3 * D:4 * D])
        dv = gv - jnp.mean(gv, axis=-1, keepdims=True)
        var = jnp.mean(dv * dv, axis=-1, keepdims=True)
        vn_s[...] = ((dv * lax.rsqrt(var + EPS)) * lng_ref[...] + lnb_ref[...]).astype(BF)
        nc = tm // CHUNK
        for c in range(nc):
            rs = slice(c * CHUNK, (c + 1) * CHUNK)
            for g in range(NGRP):
                cs = slice(g * GW, (g + 1) * GW)
                mx_s[rs, cs] = _dot(wtr_ref[g], vn_s[rs, cs])
        mixed = mx_s[...] + jnp.concatenate([bias_ref[...]] * nc, axis=0)
        yb = (_gelu(proj_ref[:, 2 * D:3 * D]) * mixed).astype(BF)
        yb_ref[...] = yb
        pb_ref[...] = _dot(yb, wpb_ref[...])

        hc_s[0:1, :] = _scan_fwd(a_s, b_s, h_ref, hc_s[0:1, :], tm)

        ya = (_gelu(proj_ref[:, D:2 * D]) * h_ref[...]).astype(BF)
        ya_ref[...] = ya
        pa = _dot(ya, wpa_ref[...])
        pa_ref[...] = pa
        mg = (_sigmoid(proj_ref[:, 4 * D:5 * D]) * pa + _sigmoid(proj_ref[:, 5 * D:6 * D]) * pb_ref[...]).astype(BF)
        mg_ref[...] = mg
        x1_ref[...] = x_ref[...] + _dot(mg, wo_ref[...])

        pl.when(i == nt - 1)(gather.finish)

    tile = lambda w: pl.BlockSpec((tm, w), lambda i: (i, 0))
    vec = _const_spec((1, D))
    any_spec = pl.BlockSpec(memory_space=pl.ANY)
    outs = pl.pallas_call(
        body, name="mixer_fwd", grid=(nt,),
        in_specs=[tile(pw), tile(D), _const_spec((4, D)), vec, vec, vec, vec, vec, vec,
                  _const_spec(wax.shape), _const_spec(wtr.shape), _const_spec(bias.shape),
                  _const_spec((D, D)), _const_spec((D, D)), _const_spec((D, D))] + [any_spec] * n,
        out_specs=[tile(D)] * 12 + [any_spec] * n,
        out_shape=[jax.ShapeDtypeStruct((s, D), F32)] * 9 + [jax.ShapeDtypeStruct((s, D), BF)] * 3
        + [jax.ShapeDtypeStruct((NDEV,) + e.shape, e.dtype) for e in shards],
        scratch_shapes=[pltpu.VMEM((8, D), F32), pltpu.VMEM((tm, D), F32),
                        pltpu.VMEM((8, D), F32), pltpu.VMEM((tm, D), BF), pltpu.VMEM((tm, D), F32)]
        + _gather_sems(n),
        compiler_params=_params(("arbitrary",)),
    )(proj, x, cw, cb, ba, bx, lam, lng, lnb, wax, wtr, bias, wpa, wpb, wo, *shards)
    return outs[:12], outs[12:]


def _ffn_call(x1, tgt, gffn, gfin, wgu, wdn):
    s = x1.shape[0]
    tm = min(TM_FFN, s)
    nt = s // tm
    nh = wdn.shape[0]
    fb = wgu.shape[2]

    def body(x1_ref, tgt_ref, gffn_ref, gfin_ref, wgu_ref, wdn_ref,
             dx1_ref, dx1b_ref, h2_ref, act_ref, dgu_ref, dx2b_ref, acc_ref,
             g_s, u_s, dx2_s, accs):
        i = pl.program_id(0)

        @pl.when(i == 0)
        def _():
            accs[...] = jnp.zeros(accs.shape, F32)

        gffn = gffn_ref[...]
        gfin = gfin_ref[...]

        x1 = x1_ref[...]
        r2 = lax.rsqrt(jnp.mean(x1 * x1, axis=-1, keepdims=True) + EPS)
        xh2 = x1 * r2
        h2 = (xh2 * gffn).astype(BF)
        h2_ref[...] = h2

        for k in range(nh):
            g = _dot(h2, wgu_ref[k])
            u = _dot(h2, wgu_ref[k + nh])
            g_s[k] = g
            u_s[k] = u
            act_ref[k] = ((g * _sigmoid(g)) * u).astype(BF)

        x2 = x1
        for k in range(nh):
            x2 = x2 + _dot(act_ref[k], wdn_ref[k])

        r3 = lax.rsqrt(jnp.mean(x2 * x2, axis=-1, keepdims=True) + EPS)
        xh = x2 * r3
        err = xh * gfin - tgt_ref[...]
        accs[2] += _fold(err * err)
        dy = err * (1.0 / D)
        accs[1] += _fold(dy * xh)
        dxh = dy * gfin
        dx2 = r3 * (dxh - xh * jnp.mean(dxh * xh, axis=-1, keepdims=True))
        dx2_s[...] = dx2
        dx2b = dx2.astype(BF)
        dx2b_ref[...] = dx2b

        for k in range(nh):
            da = _dot_nt(dx2b, wdn_ref[k])
            g = g_s[k]
            sg = _sigmoid(g)
            dgu_ref[k] = ((da * u_s[k]) * (sg * (1.0 + g * (1.0 - sg)))).astype(BF)
            dgu_ref[k + nh] = (da * (g * sg)).astype(BF)

        dh2 = _dot_nt(dgu_ref[0], wgu_ref[0])
        for k in range(1, 2 * nh):
            dh2 = dh2 + _dot_nt(dgu_ref[k], wgu_ref[k])

        accs[0] += _fold(dh2 * xh2)
        dxh = dh2 * gffn
        dx1 = dx2_s[...] + r2 * (dxh - xh2 * jnp.mean(dxh * xh2, axis=-1, keepdims=True))
        dx1_ref[...] = dx1
        dx1b_ref[...] = dx1.astype(BF)

        @pl.when(i == nt - 1)
        def _():
            acc_ref[...] = jnp.zeros((8, D), F32)
            for j in range(3):
                acc_ref[j:j + 1, :] = jnp.sum(accs[j], axis=0, keepdims=True)

    tile = lambda w: pl.BlockSpec((tm, w), lambda i: (i, 0))
    vec = _const_spec((1, D))
    return pl.pallas_call(
        body, name="ffn", grid=(nt,),
        in_specs=[tile(D), tile(D), vec, vec, _const_spec(wgu.shape), _const_spec(wdn.shape)],
        out_specs=[tile(D), tile(D), tile(D),
                   pl.BlockSpec((nh, tm, fb), lambda i: (0, i, 0)),
                   pl.BlockSpec((2 * nh, tm, fb), lambda i: (0, i, 0)),
                   tile(D), pl.BlockSpec((8, D), lambda i: (0, 0))],
        out_shape=[jax.ShapeDtypeStruct((s, D), F32), jax.ShapeDtypeStruct((s, D), BF),
                   jax.ShapeDtypeStruct((s, D), BF), jax.ShapeDtypeStruct((nh, s, fb), BF),
                   jax.ShapeDtypeStruct((2 * nh, s, fb), BF), jax.ShapeDtypeStruct((s, D), BF),
                   jax.ShapeDtypeStruct((8, D), F32)],
        scratch_shapes=[pltpu.VMEM((nh, tm, fb), F32), pltpu.VMEM((nh, tm, fb), F32), pltpu.VMEM((tm, D), F32),
                        pltpu.VMEM((3, RC, D), F32)],
        compiler_params=_params(("arbitrary",)),
    )(x1, tgt, gffn, gfin, wgu, wdn)


def _mixer_bwd_pre_call(dx1b, proj, pa, pb, lng, lnb, wtr, wtrt, bias, wpa, wpb, wo, parts):
    s = dx1b.shape[0]
    tm = min(TM_MIX, s)
    nt = s // tm
    pw = proj.shape[1]
    n = len(parts)

    def body(dx1b_ref, uv_ref, gg_ref, pa_ref, pb_ref, lng_ref, lnb_ref, wtr_ref, wtrt_ref, bias_ref,
             wpa_ref, wpb_ref, wo_ref, *rest):
        ex_in = rest[:n]
        dproj_ref, dya_ref, dpa_ref, dpb_ref, vec_ref, dws_ref, dbs_ref = rest[n:n + 7]
        ex_out = rest[n + 7:2 * n + 7]
        vn_s, mx_s, dmx_s, dvn_s, accs, dbs_s, e_send, e_recv, e_local = rest[2 * n + 7:]
        i = pl.program_id(0)
        exchange = _Exchange(ex_in, ex_out, e_send, e_recv, e_local)

        @pl.when(i == 0)
        def _():
            exchange.start()
            accs[...] = jnp.zeros(accs.shape, F32)
            dbs_s[...] = jnp.zeros(dbs_s.shape, F32)
            dws_ref[...] = jnp.zeros(dws_ref.shape, F32)

        dm = _dot_nt(dx1b_ref[...], wo_ref[...])
        sa = _sigmoid(gg_ref[:, 0:D])
        sb = _sigmoid(gg_ref[:, D:2 * D])
        dpa = dm * sa
        dpb = dm * sb
        dpab = dpa.astype(BF)
        dpbb = dpb.astype(BF)
        dpa_ref[...] = dpab
        dpb_ref[...] = dpbb
        dproj_ref[:, 2 * D:3 * D] = ((dpa * pa_ref[...]) * (1.0 - sa)).astype(BF)
        dproj_ref[:, 3 * D:4 * D] = ((dpb * pb_ref[...]) * (1.0 - sb)).astype(BF)

        dya_ref[...] = _dot_nt(dpab, wpa_ref[...])
        dyb = _dot_nt(dpbb, wpb_ref[...])

        lng = lng_ref[...]
        gv, dgelu_v = _gelu_grad(uv_ref[:, D:2 * D])
        dv = gv - jnp.mean(gv, axis=-1, keepdims=True)
        rstd = lax.rsqrt(jnp.mean(dv * dv, axis=-1, keepdims=True) + EPS)
        xh = dv * rstd
        vn_s[...] = (xh * lng + lnb_ref[...]).astype(BF)

        nc = tm // CHUNK
        for c in range(nc):
            rs = slice(c * CHUNK, (c + 1) * CHUNK)
            for g in range(NGRP):
                cs = slice(g * GW, (g + 1) * GW)
                mx_s[rs, cs] = _dot(wtr_ref[g], vn_s[rs, cs])

        gu, dgelu_u = _gelu_grad(uv_ref[:, 0:D])
        mixed = mx_s[...] + jnp.concatenate([bias_ref[...]] * nc, axis=0)
        dproj_ref[:, 0:D] = ((dyb * mixed) * dgelu_u).astype(BF)
        dmx = dyb * gu
        dmx_s[...] = dmx.astype(BF)
        dbs_s[...] += jnp.sum(dmx.reshape(nc, CHUNK, D), axis=0)

        for c in range(nc):
            rs = slice(c * CHUNK, (c + 1) * CHUNK)
            for g in range(NGRP):
                cs = slice(g * GW, (g + 1) * GW)
                dvn_s[rs, cs] = _dot(wtrt_ref[g], dmx_s[rs, cs])
                dws_ref[g] += _dot_nt(dmx_s[rs, cs], vn_s[rs, cs])

        dvn = dvn_s[...]
        accs[0] += _fold(dvn * xh)
        accs[1] += _fold(dvn)
        dxh = dvn * lng
        m1 = jnp.mean(dxh, axis=-1, keepdims=True)
        m2 = jnp.mean(dxh * xh, axis=-1, keepdims=True)
        dproj_ref[:, D:2 * D] = ((rstd * (dxh - m1 - xh * m2)) * dgelu_v).astype(BF)

        @pl.when(i == nt - 1)
        def _():
            vec_ref[...] = jnp.zeros((8, D), F32)
            for j in range(2):
                vec_ref[j:j + 1, :] = jnp.sum(accs[j], axis=0, keepdims=True)
            row = lax.broadcasted_iota(jnp.int32, (CHUNK, CHUNK), 0)
            col = lax.broadcasted_iota(jnp.int32, (CHUNK, CHUNK), 1)
            for g in range(NGRP):
                dws_ref[g] = jnp.where(row >= col, dws_ref[g], 0.0)
                gs = jnp.sum(dbs_s[:, g * GW:(g + 1) * GW], axis=1, keepdims=True)
                dbs_ref[:, g * GW:(g + 1) * GW] = jnp.broadcast_to(gs, (CHUNK, GW))
            exchange.finish()

    tile = lambda w: pl.BlockSpec((tm, w), lambda i: (i, 0))
    vec = _const_spec((1, D))
    any_spec = pl.BlockSpec(memory_space=pl.ANY)
    outs = pl.pallas_call(
        body, name="mixer_bwd_pre", grid=(nt,),
        in_specs=[tile(D), pl.BlockSpec((tm, 2 * D), lambda i: (i, 1)), pl.BlockSpec((tm, 2 * D), lambda i: (i, 2)),
                  tile(D), tile(D), vec, vec, _const_spec(wtr.shape), _const_spec(wtrt.shape),
                  _const_spec(bias.shape), _const_spec((D, D)), _const_spec((D, D)), _const_spec((D, D))]
        + [any_spec] * n,
        out_specs=[tile(4 * D), tile(D), tile(D), tile(D), pl.BlockSpec((8, D), lambda i: (0, 0)),
                   pl.BlockSpec((NGRP, CHUNK, CHUNK), lambda i: (0, 0, 0)),
                   pl.BlockSpec((CHUNK, D), lambda i: (0, 0))] + [any_spec] * n,
        out_shape=[jax.ShapeDtypeStruct((s, 4 * D), BF), jax.ShapeDtypeStruct((s, D), F32),
                   jax.ShapeDtypeStruct((s, D), BF), jax.ShapeDtypeStruct((s, D), BF),
                   jax.ShapeDtypeStruct((8, D), F32), jax.ShapeDtypeStruct((NGRP, CHUNK, CHUNK), F32),
                   jax.ShapeDtypeStruct((CHUNK, D), F32)]
        + [jax.ShapeDtypeStruct(p.shape, p.dtype) for p in parts],
        scratch_shapes=[pltpu.VMEM((tm, D), BF), pltpu.VMEM((tm, D), F32), pltpu.VMEM((tm, D), BF),
                        pltpu.VMEM((tm, D), F32), pltpu.VMEM((2, RC, D), F32), pltpu.VMEM((CHUNK, D), F32)]
        + _gather_sems(n),
        compiler_params=_params(("arbitrary",)),
    )(dx1b, proj, proj, pa, pb, lng, lnb, wtr, wtrt, bias, wpa, wpb, wo, *parts)
    return outs[:7], outs[7:]


def _mixer_bwd_seq_call(dprojb, dya, proj, h, x, dx1, gates, gmix, w_in_all, cw, lam, wax, parts):
    s = dya.shape[0]
    tm = min(TM_MIX, s)
    nt = s // tm
    tb = tm // 8
    n = len(parts)
    nb, _, wb = w_in_all.shape

    def body(dpb_ref, dya_ref, xg_ref, xh8_ref, h_ref, hh8_ref, x_ref, dx1_ref, xc_s, r_s, ig_s, a_s, m_s,
             gmix_ref, win_ref, cw_ref, lam_ref, wax_ref, *rest):
        ex_in = rest[:n]
        dpa_ref, dx_ref, vec_ref, dwax_ref = rest[n:n + 4]
        ex_out = rest[n + 4:2 * n + 4]
        lm_s, dh_s, dxc_s, c_s, accs, e_send, e_recv, e_local = rest[2 * n + 4:]
        i = pl.program_id(0)
        exchange = _Exchange(ex_in, ex_out, e_send, e_recv, e_local)

        @pl.when(i == 0)
        def _():
            exchange.start()
            accs[...] = jnp.zeros(accs.shape, F32)
            dwax_ref[...] = jnp.zeros(dwax_ref.shape, F32)
            c_s[...] = jnp.zeros((8, D), F32)
            dxc_s[tm:tm + 8, :] = jnp.zeros((8, D), F32)

        first_tile = i == nt - 1
        prev8 = jnp.where(first_tile, 0.0, xh8_ref[...])
        hprev8 = jnp.where(first_tile, 0.0, hh8_ref[...])
        lamv = lam_ref[...]
        sp = _softplus(-lamv)
        hv = h_ref[...]
        g, dg = _gelu_grad(xg_ref[:, D:2 * D])
        dya = dya_ref[...]
        lm_s[...] = dya * g
        drg = ((dya * hv) * dg).astype(BF)
        dpa_ref[:, D:2 * D] = drg
        dpa_ref[:, 2 * D:6 * D] = dpb_ref[...]

        def dpb_block(k):
            return _dot_nt(dpb_ref[:, k * wb - 2 * D:(k + 1) * wb - 2 * D], win_ref[k])
        dh = (_dot_nt(drg[:, 0:2 * wb - D], win_ref[1, :, D - wb:wb])
              + _dot_nt(drg[:, 2 * wb - D:D], win_ref[2, :, 0:2 * D - 2 * wb])
              + _dot_nt(dpb_ref[:, 0:3 * wb - 2 * D], win_ref[2, :, 2 * D - 2 * wb:wb]))
        for k in range(3, 6):
            dh = dh + dpb_block(k)
        dh_s[...] = dh

        c_s[0:1, :] = _scan_bwd(a_s, lm_s, c_s[0:1, :], tm)

        hprev = _shift_back(hprev8, h_ref[...], 1)
        for q in range(NQ):
            cs = slice(q * QW, (q + 1) * QW)
            r = r_s[:, cs]
            ig = ig_s[:, cs]
            a = a_s[:, cs]
            m = m_s[:, cs]
            lm = lm_s[:, cs]
            xq = xc_s[:, cs]
            dixc = lm * m
            dla = (lm * hprev[:, cs]) * a - ((lm * (ig * xq)) * (a * a)) / m
            accs[3, :, cs] += _fold(dla * r)
            dza = (dla * (-RG_C * sp[:, cs])) * (r * (1.0 - r))
            dzx = (dixc * xq) * (ig * (1.0 - ig))
            accs[1, :, cs] += _fold(dza)
            accs[2, :, cs] += _fold(dzx)
            dz = jnp.concatenate([dza, dzx], axis=1).astype(BF)
            dxc_s[0:tm, cs] = dixc * ig + _dot_nt(dz, wax_ref[q])
            dwax_ref[q] += _dot_tn(xq.astype(BF), dz)

        cur = dxc_s[0:tm, :]
        nx = dxc_s[tm:tm + 8, :]
        drx = cw_ref[3:4, :] * cur
        for j in (1, 2, 3):
            drx = drx + cw_ref[3 - j:4 - j, :] * _shift_fwd(cur, nx, j)
        accs[0] += _fold(cur)
        rx = xg_ref[:, 0:D]
        accs[7] += _fold(cur * rx)
        for j in (1, 2, 3):
            accs[7 - j] += _fold(cur * _shift_back(prev8, rx, j))
        dxc_s[tm:tm + 8, :] = cur[0:8, :]

        drxb = drx.astype(BF)
        dpa_ref[:, 0:D] = drxb
        dh = dh_s[...] + _dot_nt(drxb[:, 0:wb], win_ref[0]) + _dot_nt(drxb[:, wb:D], win_ref[1, :, 0:D - wb])
        for k in range(6, nb):
            dh = dh + dpb_block(k)
        xx = x_ref[...]
        rn = lax.rsqrt(jnp.mean(xx * xx, axis=-1, keepdims=True) + EPS)
        xhn = xx * rn
        accs[8] += _fold(dh * xhn)
        dxh = dh * gmix_ref[...]
        dx_ref[...] = dx1_ref[...] + rn * (dxh - xhn * jnp.mean(dxh * xhn, axis=-1, keepdims=True))

        @pl.when(i == nt - 1)
        def _():
            vec_ref[...] = jnp.zeros((16, D), F32)
            for j in range(9):
                vec_ref[j:j + 1, :] = jnp.sum(accs[j], axis=0, keepdims=True)
            vec_ref[3:4, :] = vec_ref[3:4, :] * (RG_C * _sigmoid(-lamv))
            exchange.finish()

    rev = lambda w: pl.BlockSpec((tm, w), lambda i: (nt - 1 - i, 0))
    halo = pl.BlockSpec((8, D), lambda i: (jnp.maximum((nt - 1 - i) * tb - 1, 0), 0))
    vec = _const_spec((1, D))
    any_spec = pl.BlockSpec(memory_space=pl.ANY)
    outs = pl.pallas_call(
        body, name="mixer_bwd_seq", grid=(nt,),
        in_specs=[rev(4 * D), rev(D), rev(2 * D), halo, rev(D), halo] + [rev(D)] * 7
        + [vec, _const_spec(w_in_all.shape), _const_spec((4, D)), vec, _const_spec(wax.shape)] + [any_spec] * n,
        out_specs=[rev(6 * D), rev(D), pl.BlockSpec((16, D), lambda i: (0, 0)),
                   pl.BlockSpec((NQ, QW, 2 * QW), lambda i: (0, 0, 0))] + [any_spec] * n,
        out_shape=[jax.ShapeDtypeStruct((s, 6 * D), BF), jax.ShapeDtypeStruct((s, D), F32),
                   jax.ShapeDtypeStruct((16, D), F32), jax.ShapeDtypeStruct((NQ, QW, 2 * QW), F32)]
        + [jax.ShapeDtypeStruct(p.shape, p.dtype) for p in parts],
        scratch_shapes=[pltpu.VMEM((tm, D), F32), pltpu.VMEM((tm, D), F32),
                        pltpu.VMEM((tm + 8, D), F32), pltpu.VMEM((8, D), F32), pltpu.VMEM((9, RC, D), F32)]
        + _gather_sems(n),
        compiler_params=_params(("arbitrary",)),
    )(dprojb, dya, proj, proj, h, h, x, dx1, *gates, gmix, w_in_all, cw, lam, wax, *parts)
    return outs[:4], outs[4:]


def _device_of(d):
    return (d // 4, lax.rem(d // 2, 2), lax.rem(d, 2))


_DW_PLAN = (("C", 1), ("C", 2), ("A", 1), ("C", 3), ("A", 2), ("S", 0), ("A", 3), ("O", 0))
_DW_FLIPS = tuple({"C": 2 * j + 1, "A": 2 * j, "S": 1, "O": 0}[kind] for kind, j in _DW_PLAN)


def _dw_exchange_call(name, order, a, b, a_spec, b_spec, k1, n1, s, small):
    ts = min(TS_DW, s)
    ns = s // ts
    nstep = len(_DW_PLAN)

    def slab(i, order_ref):
        return order_ref[i]

    def body(order_ref, a_ref, b_ref, g_ref, own_ref, recv_ref, gsum_ref, acc, sbuf, stage, send_sems, recv_sems,
             st_send, st_recv, local_sem, rbuf, gacc, send1, recv1, send2, recv2):
        i = pl.program_id(0)
        j = pl.program_id(1)
        me = _me()
        mi = _lin(me)
        sibling = _flip(me, 1)
        allreduce = _SmallAllReduce(g_ref, gacc, rbuf, send1, recv1, send2, recv2)
        pl.when((i == 0) & (j == 0))(allreduce.scatter)
        pl.when((i == nstep // 2) & (j == 0))(allreduce.reduce)
        p = _dot_tn(a_ref[...], b_ref[...])

        @pl.when(j == 0)
        def _():
            acc[...] = p

        @pl.when(j > 0)
        def _():
            acc[...] += p

        def send(step):
            kind, jj = _DW_PLAN[step]
            src = sbuf.at[step % 2]
            if kind == "C":
                return pltpu.make_async_remote_copy(
                    src_ref=src, dst_ref=stage.at[jj - 1], send_sem=st_send.at[jj - 1], recv_sem=st_recv.at[jj - 1],
                    device_id=sibling, device_id_type=MESH)
            to = sibling if kind == "S" else _flip(me, 2 * jj)
            return pltpu.make_async_remote_copy(
                src_ref=src, dst_ref=recv_ref.at[mi], send_sem=send_sems.at[jj], recv_sem=recv_sems.at[jj],
                device_id=to, device_id_type=MESH)

        def arrival(jj):
            frm = sibling if jj == 0 else _flip(me, 2 * jj)
            return pltpu.make_async_remote_copy(
                src_ref=sbuf.at[0], dst_ref=recv_ref.at[_lin(frm)], send_sem=send_sems.at[jj],
                recv_sem=recv_sems.at[jj], device_id=frm, device_id_type=MESH)

        def staged(jj):
            return pltpu.make_async_remote_copy(
                src_ref=sbuf.at[0], dst_ref=stage.at[jj - 1], send_sem=st_send.at[jj - 1], recv_sem=st_recv.at[jj - 1],
                device_id=sibling, device_id_type=MESH)

        for step, (kind, jj) in enumerate(_DW_PLAN):
            @pl.when((i == step) & (j == ns - 1))
            def _(step=step, kind=kind, jj=jj):
                if step >= 2:
                    send(step - 2).wait_send()
                if kind == "A":
                    staged(jj).wait_recv()
                    sbuf[step % 2] = (acc[...] + stage[jj - 1].astype(F32)).astype(BF)
                else:
                    sbuf[step % 2] = acc[...].astype(BF)
                if kind != "O":
                    send(step).start()
                else:
                    own_ref[...] = acc[...]
                    mine = pltpu.make_async_copy(sbuf.at[step % 2], recv_ref.at[mi], local_sem)
                    mine.start()
                    send(step - 1).wait_send()
                    for q in range(4):
                        arrival(q).wait_recv()
                    mine.wait()
                    allreduce.finish()
                    gsum_ref[...] = gacc[...]

    vmem = pl.BlockSpec(memory_space=pltpu.VMEM)
    grid_spec = pltpu.PrefetchScalarGridSpec(
        num_scalar_prefetch=1, grid=(nstep, ns),
        in_specs=[a_spec(ts, slab), b_spec(ts, slab), vmem],
        out_specs=[pl.BlockSpec((k1, n1), lambda i, j, o: (0, 0)), pl.BlockSpec(memory_space=pl.ANY), vmem],
        scratch_shapes=[pltpu.VMEM((k1, n1), F32), pltpu.VMEM((2, k1, n1), BF), pltpu.VMEM((3, k1, n1), BF),
                        pltpu.SemaphoreType.DMA((4,)), pltpu.SemaphoreType.DMA((4,)),
                        pltpu.SemaphoreType.DMA((3,)), pltpu.SemaphoreType.DMA((3,)),
                        pltpu.SemaphoreType.DMA(()), pltpu.VMEM((NDEV, SMALL_PER, D), F32),
                        pltpu.VMEM((SMALL_ROWS, D), F32)]
        + [pltpu.SemaphoreType.DMA((NDEV - 1,))] * 4)
    return pl.pallas_call(
        body, name=name, grid_spec=grid_spec,
        out_shape=[jax.ShapeDtypeStruct((k1, n1), F32), jax.ShapeDtypeStruct((NDEV, k1, n1), BF),
                   jax.ShapeDtypeStruct((SMALL_ROWS, D), F32)],
        compiler_params=_params(("arbitrary", "arbitrary")),
    )(order, a, b, small)


def _dw_plain_call(name, me, a, b, a_spec, b_spec, split, k1, n1, s):
    nb = NDEV // split
    r = k1 // split
    ts = min(TS_DW, s)
    ns = s // ts

    def slab(i, me_ref):
        return i

    def body(me_ref, a_ref, b_ref, own_ref, part_ref, acc):
        i = pl.program_id(0)
        j = pl.program_id(1)
        p = _dot_tn(a_ref[...], b_ref[...])

        @pl.when(j == 0)
        def _():
            acc[...] = p

        @pl.when(j > 0)
        def _():
            acc[...] += p

        @pl.when(j == ns - 1)
        def _():
            part_ref[...] = acc[...].astype(BF)

            @pl.when(i == me_ref[0] // split)
            def _():
                off = pl.multiple_of(lax.rem(me_ref[0], split) * r, RC)
                own_ref[...] = acc[pl.ds(off, r), :]

    grid_spec = pltpu.PrefetchScalarGridSpec(
        num_scalar_prefetch=1, grid=(nb, ns),
        in_specs=[a_spec(ts, slab), b_spec(ts, slab)],
        out_specs=[pl.BlockSpec((r, n1), lambda i, j, me_ref: (0, 0)),
                   pl.BlockSpec((None, k1, n1), lambda i, j, me_ref: (i, 0, 0))],
        scratch_shapes=[pltpu.VMEM((k1, n1), F32)])
    own, part = pl.pallas_call(
        body, name=name, grid_spec=grid_spec,
        out_shape=[jax.ShapeDtypeStruct((r, n1), F32), jax.ShapeDtypeStruct((nb, k1, n1), BF)],
        compiler_params=_params(("arbitrary", "arbitrary")),
    )(me, a, b)
    return own, part.reshape(NDEV, r, n1)


def _rows2d(w):
    return lambda ts, slab: pl.BlockSpec((ts, w), lambda i, j, me_ref: (j, 0))


def _cols2d(w):
    return lambda ts, slab: pl.BlockSpec((ts, w), lambda i, j, me_ref: (j, slab(i, me_ref)))


def _blk3d(w):
    return lambda ts, slab: pl.BlockSpec((None, ts, w), lambda i, j, me_ref: (slab(i, me_ref), j, 0))


_BC1 = 1.0 - ADAM_B1 ** ADAM_STEP
_BC2 = 1.0 - ADAM_B2 ** ADAM_STEP


def _adamw_math(w, g, m, v):
    m = ADAM_B1 * m + (1.0 - ADAM_B1) * g
    v = ADAM_B2 * v + (1.0 - ADAM_B2) * (g * g)
    m_hat = m / _BC1
    v_hat = v / _BC2
    delta = -ADAM_LR * (m_hat / (jnp.sqrt(v_hat) + ADAM_EPS) + ADAM_WD * w)
    return delta, m, v


def _row_tile(r):
    for t in (256, 176, 128, 64, 32, 16, 8):
        if r % t == 0:
            return t
    return r


def _reduce_adamw_call(name, sel, own, recv, w, m, v):
    r, c = own.shape
    tr = _row_tile(r)

    def body(sel_ref, own_ref, recv_ref, w_ref, m_ref, v_ref, g_ref, d_ref, nm_ref, nv_ref):
        g = jnp.zeros((tr, c), F32)
        for sdev in range(NDEV):
            part = jnp.where(sel_ref[sdev] == 1, recv_ref[sdev].astype(F32), 0.0)
            g = g + jnp.where(sel_ref[sdev] == 2, own_ref[...], part)
        g_ref[...] = g
        d_ref[...], nm_ref[...], nv_ref[...] = _adamw_math(w_ref[...], g, m_ref[...], v_ref[...])

    tile = pl.BlockSpec((tr, c), lambda i, me_ref: (i, 0))
    grid_spec = pltpu.PrefetchScalarGridSpec(
        num_scalar_prefetch=1, grid=(r // tr,),
        in_specs=[tile, pl.BlockSpec((NDEV, tr, c), lambda i, me_ref: (0, i, 0)), tile, tile, tile],
        out_specs=[tile] * 4)
    return pl.pallas_call(
        body, name=name, grid_spec=grid_spec,
        out_shape=[jax.ShapeDtypeStruct((r, c), F32)] * 4,
        compiler_params=_params(("parallel",)),
    )(sel, own, recv, w, m, v)


def _adamw_call(name, w, g, m, v):
    r, c = w.shape
    tr = _row_tile(r)

    def body(w_ref, g_ref, m_ref, v_ref, d_ref, nm_ref, nv_ref):
        d_ref[...], nm_ref[...], nv_ref[...] = _adamw_math(w_ref[...], g_ref[...], m_ref[...], v_ref[...])

    tile = pl.BlockSpec((tr, c), lambda i: (i, 0))
    return pl.pallas_call(
        body, name=name, grid=(r // tr,), in_specs=[tile] * 4, out_specs=[tile] * 3,
        out_shape=[jax.ShapeDtypeStruct((r, c), F32)] * 3,
        compiler_params=_params(("parallel",)),
    )(w, g, m, v)


def _me():
    return lax.axis_index("x"), lax.axis_index("y"), lax.axis_index("c")


def _flip(pos, r):
    x, y, c = pos
    return (1 - x if r & 4 else x, 1 - y if r & 2 else y, 1 - c if r & 1 else c)


def _lin(pos):
    return pos[0] * 4 + pos[1] * 2 + pos[2]


class _SmallAllReduce:
    def __init__(self, g_ref, out_ref, rbuf, send1, recv1, send2, recv2):
        self.g, self.out, self.rbuf = g_ref, out_ref, rbuf
        self.sems = (send1, recv1, send2, recv2)
        self.me = _me()
        self.mi = _lin(self.me)

    @staticmethod
    def _rows(d):
        return pl.ds(pl.multiple_of(d * SMALL_PER, 8), SMALL_PER)

    def _scatter(self, r, outgoing):
        peer = _flip(self.me, r)
        src, dst = (_lin(peer), self.mi) if outgoing else (self.mi, _lin(peer))
        return pltpu.make_async_remote_copy(
            src_ref=self.g.at[self._rows(src)], dst_ref=self.rbuf.at[dst],
            send_sem=self.sems[0].at[r - 1], recv_sem=self.sems[1].at[r - 1], device_id=peer, device_id_type=MESH)

    def _spread(self, r, outgoing):
        peer = _flip(self.me, r)
        rows = self._rows(self.mi if outgoing else _lin(peer))
        return pltpu.make_async_remote_copy(
            src_ref=self.out.at[rows], dst_ref=self.out.at[rows],
            send_sem=self.sems[2].at[r - 1], recv_sem=self.sems[3].at[r - 1], device_id=peer, device_id_type=MESH)

    def scatter(self):
        for r in range(1, NDEV):
            self._scatter(r, True).start()
        self.rbuf[self.mi] = self.g[self._rows(self.mi), :]

    def reduce(self):
        for r in range(1, NDEV):
            self._scatter(r, False).wait_recv()
        for r in range(1, NDEV):
            self._scatter(r, True).wait_send()
        tot = self.rbuf[0]
        for d in range(1, NDEV):
            tot = tot + self.rbuf[d]
        self.out[self._rows(self.mi), :] = tot
        for r in range(1, NDEV):
            self._spread(r, True).start()

    def finish(self):
        for r in range(1, NDEV):
            self._spread(r, False).wait_recv()
        for r in range(1, NDEV):
            self._spread(r, True).wait_send()


def _head_blocks(w):
    z = jnp.zeros((64, 64), w.dtype)
    groups = []
    for q in range(NQ):
        rows = [jnp.concatenate([w[4 * q + a] if a == b else z for b in range(4)], axis=1) for a in range(4)]
        groups.append(jnp.concatenate(rows, axis=0))
    return jnp.stack(groups)


def _head_unblocks(g):
    return jnp.stack([g[q, 64 * a:64 * a + 64, 64 * a:64 * a + 64] for q in range(NQ) for a in range(4)])


def _local_step(x, tgt, p, me, order, order_dw):
    s = x.shape[0]
    vec = lambda a: a.reshape(1, D)
    gmix, gffn, gfin = vec(p["norm_mix_g"]), vec(p["norm_ffn_g"]), vec(p["norm_final_g"])
    cb, ba, bx, lam = vec(p["conv_b"]), vec(p["rg_ba"]), vec(p["rg_bx"]), vec(p["rg_lambda"])
    lng, lnb = vec(p["sgu_ln_g"]), vec(p["sgu_ln_b"])
    wax = jnp.concatenate([_head_blocks(p["rg_wa"]), _head_blocks(p["rg_wx"])], axis=2).astype(BF)
    tril = jnp.tril(jnp.ones((CHUNK, CHUNK), bool))
    ws = jnp.where(tril[None], p["sgu_ws"], 0.0)
    wtr = ws.astype(BF)
    wtrt = jnp.swapaxes(ws, 1, 2).astype(BF)
    bias = jnp.repeat(p["sgu_bs"].T, GW, axis=1)
    shard = {k: p[k].astype(BF) for k in _BIG}

    proj, h1, w_in, (wpa, wpb, wo, cw) = _proj_gather_call(
        x, gmix, shard["w_in"], order, [shard["w_proj_a"], shard["w_proj_b"], shard["w_out"], p["conv_w"]])
    wpa, wpb, wo = (t.reshape(D, D) for t in (wpa, wpb, wo))
    cw = jnp.swapaxes(cw, 0, 1).reshape(4, D)
    (h, pa, pb, x1, *gates, ya, yb, mg), (wgu, wdn) = _mixer_fwd_call(
        proj, x, cw, cb, ba, bx, lam, lng, lnb, wax, wtr, bias, wpa, wpb, wo, [shard["w_gate_up"], shard["w_down"]])
    wdn = wdn.reshape(NDEV // 2, -1, D)
    nb, _, wb = w_in.shape
    fb = wgu.shape[2]
    nh = wdn.shape[0]
    dx1, dx1b, h2, act, dgu, dx2b, facc = _ffn_call(x1, tgt, gffn, gfin, wgu, wdn)
    assert nb == NDEV and 2 * nh == NDEV
    own_gu, part_gu = _dw_plain_call("dw_gate_up", me, dgu, h2, _blk3d(fb), _rows2d(D), 1, fb, D, s)
    own_dn, part_dn = _dw_plain_call("dw_down", me, act, dx2b, _blk3d(fb), _rows2d(D), 2, fb, D, s)
    (dprojb, dya, dpa, dpb, bvec, dws, dbs), (recv_dn,) = _mixer_bwd_pre_call(
        dx1b, proj, pa, pb, lng, lnb, wtr, wtrt, bias, wpa, wpb, wo, [part_dn])
    own_pa, part_pa = _dw_plain_call("dw_proj_a", me, ya, dpa, _rows2d(D), _rows2d(D), NDEV, D, D, s)
    own_pb, part_pb = _dw_plain_call("dw_proj_b", me, yb, dpb, _rows2d(D), _rows2d(D), NDEV, D, D, s)
    own_wo, part_wo = _dw_plain_call("dw_out", me, mg, dx1b, _rows2d(D), _rows2d(D), NDEV, D, D, s)
    (dproj, dx, svec, dwax), (recv_gu, recv_pa, recv_pb, recv_wo) = _mixer_bwd_seq_call(
        dprojb, dya, proj, h, x, dx1, gates, gmix, w_in, cw, lam, wax, [part_gu, part_pa, part_pb, part_wo])
    small = {
        "norm_mix_g": svec[8], "norm_ffn_g": facc[0], "norm_final_g": facc[1],
        "conv_b": svec[0], "rg_ba": svec[1], "rg_bx": svec[2], "rg_lambda": svec[3],
        "sgu_ln_g": bvec[0], "sgu_ln_b": bvec[1],
        "rg_wa": _head_unblocks(dwax[:, :, 0:QW]), "rg_wx": _head_unblocks(dwax[:, :, QW:2 * QW]),
        "sgu_ws": dws, "sgu_bs": dbs[:, ::GW].T,
    }
    packed = _pack_small(small, svec[4:8], facc[2])
    own_in, recv_in, gsum = _dw_exchange_call("dw_in", order_dw, h1, dproj, _rows2d(D), _cols2d(wb), D, wb, s, packed)
    dw = {
        "w_gate_up": (own_gu, recv_gu), "w_down": (own_dn, recv_dn), "w_proj_a": (own_pa, recv_pa),
        "w_proj_b": (own_pb, recv_pb), "w_out": (own_wo, recv_wo), "w_in": (own_in, recv_in),
    }
    return gsum, dx, dw


_BIG = ("w_in", "w_gate_up", "w_down", "w_proj_a", "w_proj_b", "w_out")
_VEC_ROWS = ("norm_mix_g", "norm_ffn_g", "norm_final_g", "conv_b", "rg_ba", "rg_bx", "rg_lambda",
             "sgu_ln_g", "sgu_ln_b", "sgu_bs")
_WEIGHTS = ("norm_mix_g", "w_in", "conv_w", "conv_b", "rg_wa", "rg_ba", "rg_wx", "rg_bx", "rg_lambda",
            "sgu_ln_g", "sgu_ln_b", "sgu_ws", "sgu_bs", "w_proj_a", "w_proj_b", "w_out", "norm_ffn_g",
            "w_gate_up", "w_down", "norm_final_g")


def _pack_small(t, conv_w, extra=None):
    extra = jnp.zeros((1, D), F32) if extra is None else extra.reshape(1, D)
    head = jnp.concatenate([t[k].reshape(1, D) for k in _VEC_ROWS] + [conv_w, extra, jnp.zeros((1, D), F32)], axis=0)
    return jnp.concatenate([head, t["rg_wa"].reshape(64, D), t["rg_wx"].reshape(64, D), t["sgu_ws"].reshape(128, D),
                            jnp.zeros((SMALL_ROWS - 272, D), F32)], axis=0)


def _unpack_small(a):
    out = {k: a[j] for j, k in enumerate(_VEC_ROWS)}
    out["conv_w"] = a[10:14]
    out["rg_wa"] = a[16:80].reshape(16, 64, 64)
    out["rg_wx"] = a[80:144].reshape(16, 64, 64)
    out["sgu_ws"] = a[144:272].reshape(NGRP, CHUNK, CHUNK)
    return out


def kernel(x, norm_mix_g, w_in, conv_w, conv_b, rg_wa, rg_ba, rg_wx, rg_bx, rg_lambda, sgu_ln_g, sgu_ln_b, sgu_ws, sgu_bs, w_proj_a, w_proj_b, w_out, norm_ffn_g, w_gate_up, w_down, norm_final_g, loss_target, m_norm_mix_g, m_w_in, m_conv_w, m_conv_b, m_rg_wa, m_rg_ba, m_rg_wx, m_rg_bx, m_rg_lambda, m_sgu_ln_g, m_sgu_ln_b, m_sgu_ws, m_sgu_bs, m_w_proj_a, m_w_proj_b, m_w_out, m_norm_ffn_g, m_w_gate_up, m_w_down, m_norm_final_g, v_norm_mix_g, v_w_in, v_conv_w, v_conv_b, v_rg_wa, v_rg_ba, v_rg_wx, v_rg_bx, v_rg_lambda, v_sgu_ln_g, v_sgu_ln_b, v_sgu_ws, v_sgu_bs, v_w_proj_a, v_w_proj_b, v_w_out, v_norm_ffn_g, v_w_gate_up, v_w_down, v_norm_final_g):
    args = dict(locals())
    w = {k: args[k] for k in _WEIGHTS}
    m = {k: args["m_" + k] for k in _WEIGHTS}
    v = {k: args["v_" + k] for k in _WEIGHTS}
    for d in (w, m, v):
        for k in _WEIGHTS:
            if k != "norm_final_g":
                d[k] = d[k][0]
    me = _lin(_me())
    me1 = me.reshape(1).astype(jnp.int32)

    order = jnp.bitwise_xor(me, jnp.array(_PASS_FLIPS, jnp.int32)).astype(jnp.int32)
    order_dw = jnp.bitwise_xor(me, jnp.array(_DW_FLIPS, jnp.int32)).astype(jnp.int32)

    gsum, dx, dw = _local_step(x[0], loss_target[0], w, me1, order, order_dw)

    peer = jnp.bitwise_xor(jnp.arange(NDEV, dtype=jnp.int32), me)
    sel_direct = jnp.where(peer == 0, 2, 1).astype(jnp.int32)
    sel_two_level = jnp.where(peer == 0, 2, jnp.where((peer == 1) | (peer % 2 == 0), 1, 0)).astype(jnp.int32)
    grads, delta, new_m, new_v = {}, {}, {}, {}
    for k in _BIG:
        own, recv = dw[k]
        flip = own.shape != w[k].shape
        wmv = [jnp.swapaxes(t, 0, 1) if flip else t for t in (w[k], m[k], v[k])]
        res = _reduce_adamw_call("adamw_" + k, sel_two_level if k == "w_in" else sel_direct, own, recv, *wmv)
        grads[k], delta[k], new_m[k], new_v[k] = (jnp.swapaxes(t, 0, 1) if flip else t for t in res)

    loss = (0.5 / D) * jnp.sum(gsum[14])
    zc = jnp.zeros((4, D), F32)
    d_s, m_s, v_s = _adamw_call("adamw_small", _pack_small(w, zc), gsum, _pack_small(m, zc), _pack_small(v, zc))
    gs, ds, ms, vs = _unpack_small(gsum), _unpack_small(d_s), _unpack_small(m_s), _unpack_small(v_s)
    g_cw = lax.dynamic_slice(gs["conv_w"], (0, me * 128), (4, 128))
    ds["conv_w"], ms["conv_w"], vs["conv_w"] = _adamw_call("adamw_conv_w", w["conv_w"], g_cw, m["conv_w"], v["conv_w"])
    gs["conv_w"] = g_cw
    for k in _WEIGHTS:
        if k not in _BIG:
            shp = w[k].shape
            grads[k], delta[k], new_m[k], new_v[k] = (t[k].reshape(shp) for t in (gs, ds, ms, vs))

    def lift(t, k):
        return t[k] if k == "norm_final_g" else t[k][None]

    outs = [loss, dx[None]]
    for t in (grads, delta, new_m, new_v):
        outs += [lift(t, k) for k in _WEIGHTS]
    return tuple(outs)
```

```python
import functools

import jax
import jax.numpy as jnp
from jax import lax
from jax.experimental import pallas as pl
from jax.experimental.pallas import tpu as pltpu

F32 = jnp.float32
BF = jnp.bfloat16

D = 1024
NDEV = 8
EPS = 1e-6
RG_C = 8.0
CHUNK = 128
NGRP = 8
GW = 128
NQ = 4
QW = 256
RC = 16
SMALL_ROWS = 320
SMALL_PER = SMALL_ROWS // NDEV

ADAM_LR = 0.001
ADAM_B1 = 0.9
ADAM_B2 = 0.999
ADAM_EPS = 1e-08
ADAM_WD = 0.01
ADAM_STEP = 10

VMEM_LIMIT = 60 * 1024 * 1024

MESH = pl.DeviceIdType.MESH


def _rows(n, fn, unroll=2, rc=RC):
    def body(i, c):
        fn(pl.multiple_of(i * rc, rc))
        return c
    lax.fori_loop(0, n // rc, body, 0, unroll=unroll)


def _fold(v):
    return jnp.sum(v.reshape(v.shape[0] // RC, RC, v.shape[1]), axis=0)


def _dot(a, b):
    return jnp.dot(a, b, preferred_element_type=F32)


def _dot_nt(a, b):
    return lax.dot_general(a, b, (((1,), (1,)), ((), ())), preferred_element_type=F32)


def _dot_tn(a, b):
    return lax.dot_general(a, b, (((0,), (0,)), ((), ())), preferred_element_type=F32)


_GC = 0.7978845608028654
_GK = 0.044715


def _gelu(x):
    t = jnp.tanh(_GC * (x + _GK * (x * x * x)))
    return x * (0.5 * (1.0 + t))


def _gelu_grad(x):
    x2 = x * x
    t = jnp.tanh(_GC * (x + _GK * (x2 * x)))
    cdf = 0.5 * (1.0 + t)
    dg = cdf + (0.5 * x) * (1.0 - t * t) * (_GC * (1.0 + (3.0 * _GK) * x2))
    return x * cdf, dg


def _sigmoid(x):
    return jax.nn.sigmoid(x)


def _log1p(e):
    u = 1.0 + e
    d = u - 1.0
    return jnp.where(d == 0.0, e, jnp.log(u) * (e / jnp.where(d == 0.0, 1.0, d)))


def _softplus(z):
    return jnp.maximum(z, 0.0) + _log1p(jnp.exp(-jnp.abs(z)))


def _neg_expm1(z):
    u = jnp.exp(z)
    lu = jnp.log(u)
    k = (1.0 - u) * (z / jnp.where(lu == 0.0, 1.0, lu))
    small = jnp.where(lu == 0.0, -z, k)
    return jnp.where(z > -0.5, small, 1.0 - u)


def _shift_back(prev8, cur, j):
    cat = jnp.concatenate([prev8, cur], axis=0)
    return pltpu.roll(cat, j, 0)[8:8 + cur.shape[0]]


def _shift_fwd(cur, next8, j):
    cat = jnp.concatenate([cur, next8], axis=0)
    n = cat.shape[0]
    return pltpu.roll(cat, n - j, 0)[0:cur.shape[0]]


def _const_spec(shape):
    nd = len(shape)
    return pl.BlockSpec(shape, lambda *_: (0,) * nd, pipeline_mode=pl.Buffered(1))


def _params(sem):
    return pltpu.CompilerParams(dimension_semantics=sem, vmem_limit_bytes=VMEM_LIMIT)


TM_PROJ = 1024
TM_MIX = 256
TM_FFN = 256
TS_DW = 4096


_CHIPS = (4, 2, 6)
_PASS_FLIPS = (0, 1, 4, 2, 6, 5, 3, 7)


class _Gather:
    def __init__(self, ins, outs, send_sems, recv_sems, local_sems):
        self.ins, self.outs = ins, outs
        self.send_sems, self.recv_sems, self.local_sems = send_sems, recv_sems, local_sems
        self.me = _me()
        self.sibling = _flip(self.me, 1)

    def _copy(self, a, kind, block, to, src=None):
        dst = self.outs[a].at[_lin(block)]
        return pltpu.make_async_remote_copy(
            src_ref=dst if src is None else src, dst_ref=dst,
            send_sem=self.send_sems.at[a, kind], recv_sem=self.recv_sems.at[a, kind],
            device_id=to, device_id_type=MESH)

    def _local(self, a):
        return pltpu.make_async_copy(self.ins[a], self.outs[a].at[_lin(self.me)], self.local_sems.at[a])

    def start(self):
        for a in range(len(self.ins)):
            self._local(a).start()
            self._copy(a, 0, self.me, self.sibling, src=self.ins[a]).start()
            for j, f in enumerate(_CHIPS):
                self._copy(a, 1 + j, self.me, _flip(self.me, f), src=self.ins[a]).start()

    def forward(self):
        for j, f in enumerate(_CHIPS):
            for a in range(len(self.ins)):
                self._copy(a, 1 + j, _flip(self.me, f), self.me).wait_recv()
                self._copy(a, 4 + j, _flip(self.me, f), self.sibling).start()

    def finish(self):
        for a in range(len(self.ins)):
            self._copy(a, 0, self.sibling, self.me).wait_recv()
            for j, f in enumerate(_CHIPS):
                self._copy(a, 4 + j, _flip(self.me, f | 1), self.me).wait_recv()
            self._copy(a, 0, self.me, self.sibling, src=self.ins[a]).wait_send()
            for j, f in enumerate(_CHIPS):
                self._copy(a, 1 + j, self.me, _flip(self.me, f), src=self.ins[a]).wait_send()
                self._copy(a, 4 + j, _flip(self.me, f), self.sibling).wait_send()
            self._local(a).wait()


class _Exchange:
    def __init__(self, ins, outs, send_sems, recv_sems, local_sems):
        self.ins, self.outs = ins, outs
        self.send_sems, self.recv_sems, self.local_sems = send_sems, recv_sems, local_sems
        self.me = _me()

    def _copy(self, a, r, outgoing):
        peer = _flip(self.me, r)
        src, dst = (peer, self.me) if outgoing else (self.me, peer)
        return pltpu.make_async_remote_copy(
            src_ref=self.ins[a].at[_lin(src)], dst_ref=self.outs[a].at[_lin(dst)],
            send_sem=self.send_sems.at[a, r - 1], recv_sem=self.recv_sems.at[a, r - 1],
            device_id=peer, device_id_type=MESH)

    def _local(self, a):
        mi = _lin(self.me)
        return pltpu.make_async_copy(self.ins[a].at[mi], self.outs[a].at[mi], self.local_sems.at[a])

    def start(self):
        for a in range(len(self.ins)):
            self._local(a).start()
        for r in range(1, NDEV):
            for a in range(len(self.ins)):
                self._copy(a, r, True).start()

    def finish(self):
        for r in range(1, NDEV):
            for a in range(len(self.ins)):
                self._copy(a, r, False).wait_recv()
        for r in range(1, NDEV):
            for a in range(len(self.ins)):
                self._copy(a, r, True).wait_send()
        for a in range(len(self.ins)):
            self._local(a).wait()


def _gather_sems(n):
    return [pltpu.SemaphoreType.DMA((n, 7)), pltpu.SemaphoreType.DMA((n, 7)), pltpu.SemaphoreType.DMA((n,))]


def _proj_gather_call(x, gmix, w_shard, order, extras):
    s = x.shape[0]
    tm = min(TM_PROJ, s)
    nt = s // tm
    wb = w_shard.shape[1]
    n = len(extras)

    def body(order_ref, x_ref, g_ref, wsh_ref, *rest):
        ex_in = rest[:n]
        proj_ref, h1_hbm, wall_hbm = rest[n:n + 3]
        ex_out = rest[n + 3:2 * n + 3]
        wall, hc, ws_send, ws_recv, ex_send, ex_recv, ex_local, out_sems = rest[2 * n + 3:]
        k = pl.program_id(0)
        i = pl.program_id(1)
        me = _me()
        sibling = _flip(me, 1)
        gather = _Gather(ex_in, ex_out, ex_send, ex_recv, ex_local)

        def wcopy(kind, block, to):
            ref = wall.at[_lin(block)]
            return pltpu.make_async_remote_copy(
                src_ref=ref, dst_ref=ref, send_sem=ws_send.at[kind], recv_sem=ws_recv.at[kind],
                device_id=to, device_id_type=MESH)

        head = i == 0

        @pl.when(head & (k == 0))
        def _():
            wall[_lin(me)] = wsh_ref[...]
            wcopy(0, me, sibling).start()
            for j, f in enumerate(_CHIPS):
                wcopy(1 + j, me, _flip(me, f)).start()
            gather.start()

        @pl.when(head & (k == 1))
        def _():
            wcopy(0, sibling, me).wait_recv()

        for j, f in enumerate(_CHIPS):
            @pl.when(head & (k == 2 + j))
            def _(j=j, f=f):
                wcopy(1 + j, _flip(me, f), me).wait_recv()
                wcopy(4 + j, _flip(me, f), sibling).start()

            @pl.when(head & (k == 5 + j))
            def _(j=j, f=f):
                wcopy(4 + j, _flip(me, f | 1), me).wait_recv()

        pl.when(head & (k == 5))(gather.forward)

        base = pl.multiple_of(i * tm, tm)

        @pl.when(k == 0)
        def _():
            g = g_ref[...]

            def norm(r0):
                xx = x_ref[pl.ds(r0, RC), :]
                r = lax.rsqrt(jnp.mean(xx * xx, axis=-1, keepdims=True) + EPS)
                hc[pl.ds(base + r0, RC), :] = ((xx * r) * g).astype(BF)
            _rows(tm, norm, unroll=4)

        wk = wall.at[order_ref[k]]
        for r0 in range(0, tm, TM_MIX):
            rs = min(TM_MIX, tm - r0)
            proj_ref[r0:r0 + rs, :] = _dot(hc[pl.ds(base + r0, rs), :], wk[...])

        @pl.when((k == NDEV - 1) & (i == nt - 1))
        def _():
            out_w = pltpu.make_async_copy(wall, wall_hbm, out_sems.at[0])
            out_h = pltpu.make_async_copy(hc, h1_hbm, out_sems.at[1])
            out_w.start()
            out_h.start()
            wcopy(0, me, sibling).wait_send()
            for j, f in enumerate(_CHIPS):
                wcopy(1 + j, me, _flip(me, f)).wait_send()
                wcopy(4 + j, _flip(me, f), sibling).wait_send()
            gather.finish()
            out_w.wait()
            out_h.wait()

    any_spec = pl.BlockSpec(memory_space=pl.ANY)
    grid_spec = pltpu.PrefetchScalarGridSpec(
        num_scalar_prefetch=1, grid=(NDEV, nt),
        in_specs=[pl.BlockSpec((tm, D), lambda k, i, o: (jnp.where(k == 0, i, nt - 1), 0)),
                  pl.BlockSpec((1, D), lambda k, i, o: (0, 0)),
                  pl.BlockSpec(w_shard.shape, lambda k, i, o: (0, 0))] + [any_spec] * n,
        out_specs=[pl.BlockSpec((tm, wb), lambda k, i, o: (i, o[k])), any_spec, any_spec] + [any_spec] * n,
        scratch_shapes=[pltpu.VMEM((NDEV,) + w_shard.shape, BF), pltpu.VMEM((s, D), BF),
                        pltpu.SemaphoreType.DMA((7,)), pltpu.SemaphoreType.DMA((7,))] + _gather_sems(n)
        + [pltpu.SemaphoreType.DMA((2,))])
    outs = pl.pallas_call(
        body, name="proj_gather", grid_spec=grid_spec,
        out_shape=[jax.ShapeDtypeStruct((s, NDEV * wb), F32), jax.ShapeDtypeStruct((s, D), BF),
                   jax.ShapeDtypeStruct((NDEV,) + w_shard.shape, BF)]
        + [jax.ShapeDtypeStruct((NDEV,) + e.shape, e.dtype) for e in extras],
        compiler_params=_params(("arbitrary", "arbitrary")),
    )(order, x, gmix, w_shard, *extras)
    return outs[0], outs[1], outs[2], outs[3:]


def _conv_tile(rx, prev8, cw_ref, cb):
    xc = cb + cw_ref[3:4, :] * rx
    for j in (1, 2, 3):
        xc = xc + cw_ref[3 - j:4 - j, :] * _shift_back(prev8, rx, j)
    return xc


def _gate_tile(xcb, wax_ref, ba, bx, sp, q):
    cs = slice(q * QW, (q + 1) * QW)
    z = _dot(xcb[:, cs], wax_ref[q])
    r = _sigmoid(z[:, 0:QW] + ba[:, cs])
    ig = _sigmoid(z[:, QW:2 * QW] + bx[:, cs])
    return r, ig, (-RG_C * r) * sp[:, cs]


def _scan_fwd(a_s, b_s, out_ref, h0, n):
    rowi = lax.broadcasted_iota(jnp.int32, (8, D), 0)
    masks = [(d, rowi >= d) for d in (1, 2, 4)]

    def block(t, h):
        rows = pl.ds(pl.multiple_of(t * 8, 8), 8)
        a = a_s[rows, :]
        b = b_s[rows, :]
        for d, m in masks:
            b = jnp.where(m, a * pltpu.roll(b, d, 0) + b, b)
            a = jnp.where(m, a * pltpu.roll(a, d, 0), a)
        hb = b + a * h
        out_ref[rows, :] = hb
        return hb[7:8, :]
    return lax.fori_loop(0, n // 8, block, h0, unroll=4)


def _scan_bwd(a_s, lm_s, c0, n):
    rowi = lax.broadcasted_iota(jnp.int32, (8, D), 0)
    masks = [(d, rowi < 8 - d) for d in (1, 2, 4)]
    inner = rowi < 7
    nblk = n // 8

    def block(k, cin):
        rows = pl.ds(pl.multiple_of((nblk - 1 - k) * 8, 8), 8)
        a = a_s[rows, :]
        dh = lm_s[rows, :]
        b = a * dh
        for d, m in masks:
            b = jnp.where(m, a * pltpu.roll(b, 8 - d, 0) + b, b)
            a = jnp.where(m, a * pltpu.roll(a, 8 - d, 0), a)
        mu = b + a * cin
        lm_s[rows, :] = dh + jnp.where(inner, pltpu.roll(mu, 7, 0), cin)
        return mu[0:1, :]
    return lax.fori_loop(0, nblk, block, c0, unroll=4)


def _mixer_fwd_call(proj, x, cw, cb, ba, bx, lam, lng, lnb, wax, wtr, bias, wpa, wpb, wo, shards):
    s = x.shape[0]
    tm = min(TM_MIX, s)
    nt = s // tm
    pw = proj.shape[1]
    n = len(shards)

    def body(proj_ref, x_ref, cw_ref, cb_ref, ba_ref, bx_ref, lam_ref, lng_ref, lnb_ref, wax_ref, wtr_ref,
             bias_ref, wpa_ref, wpb_ref, wo_ref, *rest):
        sh_in = rest[:n]
        h_ref, pa_ref, pb_ref, x1_ref, xc_ref, r_ref, ig_ref, a_s, m_ref, ya_ref, yb_ref, mg_ref = rest[n:n + 12]
        sh_out = rest[n + 12:2 * n + 12]
        prev_s, b_s, hc_s, vn_s, mx_s, g_send, g_recv, g_local = rest[2 * n + 12:]
        i = pl.program_id(0)
        gather = _Gather(sh_in, sh_out, g_send, g_recv, g_local)

        @pl.when(i == 0)
        def _():
            gather.start()
            prev_s[...] = jnp.zeros((8, D), F32)
            hc_s[...] = jnp.zeros((8, D), F32)

        pl.when(i == nt // 2)(gather.forward)

        rx = proj_ref[:, 0:D]
        xc = _conv_tile(rx, prev_s[...], cw_ref, cb_ref[...])
        prev_s[...] = rx[tm - 8:tm, :]
        xc_ref[...] = xc
        xcb = xc.astype(BF)
        sp = _softplus(-lam_ref[...])
        ba = ba_ref[...]
        bx = bx_ref[...]
        for q in range(NQ):
            cs = slice(q * QW, (q + 1) * QW)
            r, ig, la = _gate_tile(xcb, wax_ref, ba, bx, sp, q)
            mq = jnp.sqrt(_neg_expm1(2.0 * la))
            r_ref[:, cs] = r
            ig_ref[:, cs] = ig
            m_ref[:, cs] = mq
            a_s[:, cs] = jnp.exp(la)
            b_s[:, cs] = mq * (ig * xc[:, cs])

        gv = _gelu(proj_ref[:, 3 * D:4 * D])
        dv = gv - jnp.mean(gv, axis=-1, keepdims=True)
        var = jnp.mean(dv * dv, axis=-1, keepdims=True)
        vn_s[...] = ((dv * lax.rsqrt(var + EPS)) * lng_ref[...] + lnb_ref[...]).astype(BF)
        nc = tm // CHUNK
        for c in range(nc):
            rs = slice(c * CHUNK, (c + 1) * CHUNK)
            for g in range(NGRP):
                cs = slice(g * GW, (g + 1) * GW)
                mx_s[rs, cs] = _dot(wtr_ref[g], vn_s[rs, cs])
        mixed = mx_s[...] + jnp.concatenate([bias_ref[...]] * nc, axis=0)
        yb = (_gelu(proj_ref[:, 2 * D:3 * D]) * mixed).astype(BF)
        yb_ref[...] = yb
        pb_ref[...] = _dot(yb, wpb_ref[...])

        hc_s[0:1, :] = _scan_fwd(a_s, b_s, h_ref, hc_s[0:1, :], tm)

        ya = (_gelu(proj_ref[:, D:2 * D]) * h_ref[...]).astype(BF)
        ya_ref[...] = ya
        pa = _dot(ya, wpa_ref[...])
        pa_ref[...] = pa
        mg = (_sigmoid(proj_ref[:, 4 * D:5 * D]) * pa + _sigmoid(proj_ref[:, 5 * D:6 * D]) * pb_ref[...]).astype(BF)
        mg_ref[...] = mg
        x1_ref[...] = x_ref[...] + _dot(mg, wo_ref[...])

        pl.when(i == nt - 1)(gather.finish)

    tile = lambda w: pl.BlockSpec((tm, w), lambda i: (i, 0))
    vec = _const_spec((1, D))
    any_spec = pl.BlockSpec(memory_space=pl.ANY)
    outs = pl.pallas_call(
        body, name="mixer_fwd", grid=(nt,),
        in_specs=[tile(pw), tile(D), _const_spec((4, D)), vec, vec, vec, vec, vec, vec,
                  _const_spec(wax.shape), _const_spec(wtr.shape), _const_spec(bias.shape),
                  _const_spec((D, D)), _const_spec((D, D)), _const_spec((D, D))] + [any_spec] * n,
        out_specs=[tile(D)] * 12 + [any_spec] * n,
        out_shape=[jax.ShapeDtypeStruct((s, D), F32)] * 9 + [jax.ShapeDtypeStruct((s, D), BF)] * 3
        + [jax.ShapeDtypeStruct((NDEV,) + e.shape, e.dtype) for e in shards],
        scratch_shapes=[pltpu.VMEM((8, D), F32), pltpu.VMEM((tm, D), F32),
                        pltpu.VMEM((8, D), F32), pltpu.VMEM((tm, D), BF), pltpu.VMEM((tm, D), F32)]
        + _gather_sems(n),
        compiler_params=_params(("arbitrary",)),
    )(proj, x, cw, cb, ba, bx, lam, lng, lnb, wax, wtr, bias, wpa, wpb, wo, *shards)
    return outs[:12], outs[12:]


def _ffn_call(x1, tgt, gffn, gfin, wgu, wdn):
    s = x1.shape[0]
    tm = min(TM_FFN, s)
    nt = s // tm
    nh = wdn.shape[0]
    fb = wgu.shape[2]

    def body(x1_ref, tgt_ref, gffn_ref, gfin_ref, wgu_ref, wdn_ref,
             dx1_ref, dx1b_ref, h2_ref, act_ref, dgu_ref, dx2b_ref, acc_ref,
             g_s, u_s, dx2_s, accs):
        i = pl.program_id(0)

        @pl.when(i == 0)
        def _():
            accs[...] = jnp.zeros(accs.shape, F32)

        gffn = gffn_ref[...]
        gfin = gfin_ref[...]

        x1 = x1_ref[...]
        r2 = lax.rsqrt(jnp.mean(x1 * x1, axis=-1, keepdims=True) + EPS)
        xh2 = x1 * r2
        h2 = (xh2 * gffn).astype(BF)
        h2_ref[...] = h2

        for k in range(nh):
            g = _dot(h2, wgu_ref[k])
            u = _dot(h2, wgu_ref[k + nh])
            g_s[k] = g
            u_s[k] = u
            act_ref[k] = ((g * _sigmoid(g)) * u).astype(BF)

        x2 = x1
        for k in range(nh):
            x2 = x2 + _dot(act_ref[k], wdn_ref[k])

        r3 = lax.rsqrt(jnp.mean(x2 * x2, axis=-1, keepdims=True) + EPS)
        xh = x2 * r3
        err = xh * gfin - tgt_ref[...]
        accs[2] += _fold(err * err)
        dy = err * (1.0 / D)
        accs[1] += _fold(dy * xh)
        dxh = dy * gfin
        dx2 = r3 * (dxh - xh * jnp.mean(dxh * xh, axis=-1, keepdims=True))
        dx2_s[...] = dx2
        dx2b = dx2.astype(BF)
        dx2b_ref[...] = dx2b

        for k in range(nh):
            da = _dot_nt(dx2b, wdn_ref[k])
            g = g_s[k]
            sg = _sigmoid(g)
            dgu_ref[k] = ((da * u_s[k]) * (sg * (1.0 + g * (1.0 - sg)))).astype(BF)
            dgu_ref[k + nh] = (da * (g * sg)).astype(BF)

        dh2 = _dot_nt(dgu_ref[0], wgu_ref[0])
        for k in range(1, 2 * nh):
            dh2 = dh2 + _dot_nt(dgu_ref[k], wgu_ref[k])

        accs[0] += _fold(dh2 * xh2)
        dxh = dh2 * gffn
        dx1 = dx2_s[...] + r2 * (dxh - xh2 * jnp.mean(dxh * xh2, axis=-1, keepdims=True))
        dx1_ref[...] = dx1
        dx1b_ref[...] = dx1.astype(BF)

        @pl.when(i == nt - 1)
        def _():
            acc_ref[...] = jnp.zeros((8, D), F32)
            for j in range(3):
                acc_ref[j:j + 1, :] = jnp.sum(accs[j], axis=0, keepdims=True)

    tile = lambda w: pl.BlockSpec((tm, w), lambda i: (i, 0))
    vec = _const_spec((1, D))
    return pl.pallas_call(
        body, name="ffn", grid=(nt,),
        in_specs=[tile(D), tile(D), vec, vec, _const_spec(wgu.shape), _const_spec(wdn.shape)],
        out_specs=[tile(D), tile(D), tile(D),
                   pl.BlockSpec((nh, tm, fb), lambda i: (0, i, 0)),
                   pl.BlockSpec((2 * nh, tm, fb), lambda i: (0, i, 0)),
                   tile(D), pl.BlockSpec((8, D), lambda i: (0, 0))],
        out_shape=[jax.ShapeDtypeStruct((s, D), F32), jax.ShapeDtypeStruct((s, D), BF),
                   jax.ShapeDtypeStruct((s, D), BF), jax.ShapeDtypeStruct((nh, s, fb), BF),
                   jax.ShapeDtypeStruct((2 * nh, s, fb), BF), jax.ShapeDtypeStruct((s, D), BF),
                   jax.ShapeDtypeStruct((8, D), F32)],
        scratch_shapes=[pltpu.VMEM((nh, tm, fb), F32), pltpu.VMEM((nh, tm, fb), F32), pltpu.VMEM((tm, D), F32),
                        pltpu.VMEM((3, RC, D), F32)],
        compiler_params=_params(("arbitrary",)),
    )(x1, tgt, gffn, gfin, wgu, wdn)


def _mixer_bwd_pre_call(dx1b, proj, pa, pb, lng, lnb, wtr, wtrt, bias, wpa, wpb, wo, parts):
    s = dx1b.shape[0]
    tm = min(TM_MIX, s)
    nt = s // tm
    pw = proj.shape[1]
    n = len(parts)

    def body(dx1b_ref, uv_ref, gg_ref, pa_ref, pb_ref, lng_ref, lnb_ref, wtr_ref, wtrt_ref, bias_ref,
             wpa_ref, wpb_ref, wo_ref, *rest):
        ex_in = rest[:n]
        dproj_ref, dya_ref, dpa_ref, dpb_ref, vec_ref, dws_ref, dbs_ref = rest[n:n + 7]
        ex_out = rest[n + 7:2 * n + 7]
        vn_s, mx_s, dmx_s, dvn_s, accs, dbs_s, e_send, e_recv, e_local = rest[2 * n + 7:]
        i = pl.program_id(0)
        exchange = _Exchange(ex_in, ex_out, e_send, e_recv, e_local)

        @pl.when(i == 0)
        def _():
            exchange.start()
            accs[...] = jnp.zeros(accs.shape, F32)
            dbs_s[...] = jnp.zeros(dbs_s.shape, F32)
            dws_ref[...] = jnp.zeros(dws_ref.shape, F32)

        dm = _dot_nt(dx1b_ref[...], wo_ref[...])
        sa = _sigmoid(gg_ref[:, 0:D])
        sb = _sigmoid(gg_ref[:, D:2 * D])
        dpa = dm * sa
        dpb = dm * sb
        dpab = dpa.astype(BF)
        dpbb = dpb.astype(BF)
        dpa_ref[...] = dpab
        dpb_ref[...] = dpbb
        dproj_ref[:, 2 * D:3 * D] = ((dpa * pa_ref[...]) * (1.0 - sa)).astype(BF)
        dproj_ref[:, 3 * D:4 * D] = ((dpb * pb_ref[...]) * (1.0 - sb)).astype(BF)

        dya_ref[...] = _dot_nt(dpab, wpa_ref[...])
        dyb = _dot_nt(dpbb, wpb_ref[...])

        lng = lng_ref[...]
        gv, dgelu_v = _gelu_grad(uv_ref[:, D:2 * D])
        dv = gv - jnp.mean(gv, axis=-1, keepdims=True)
        rstd = lax.rsqrt(jnp.mean(dv * dv, axis=-1, keepdims=True) + EPS)
        xh = dv * rstd
        vn_s[...] = (xh * lng + lnb_ref[...]).astype(BF)

        nc = tm // CHUNK
        for c in range(nc):
            rs = slice(c * CHUNK, (c + 1) * CHUNK)
            for g in range(NGRP):
                cs = slice(g * GW, (g + 1) * GW)
                mx_s[rs, cs] = _dot(wtr_ref[g], vn_s[rs, cs])

        gu, dgelu_u = _gelu_grad(uv_ref[:, 0:D])
        mixed = mx_s[...] + jnp.concatenate([bias_ref[...]] * nc, axis=0)
        dproj_ref[:, 0:D] = ((dyb * mixed) * dgelu_u).astype(BF)
        dmx = dyb * gu
        dmx_s[...] = dmx.astype(BF)
        dbs_s[...] += jnp.sum(dmx.reshape(nc, CHUNK, D), axis=0)

        for c in range(nc):
            rs = slice(c * CHUNK, (c + 1) * CHUNK)
            for g in range(NGRP):
                cs = slice(g * GW, (g + 1) * GW)
                dvn_s[rs, cs] = _dot(wtrt_ref[g], dmx_s[rs, cs])
                dws_ref[g] += _dot_nt(dmx_s[rs, cs], vn_s[rs, cs])

        dvn = dvn_s[...]
        accs[0] += _fold(dvn * xh)
        accs[1] += _fold(dvn)
        dxh = dvn * lng
        m1 = jnp.mean(dxh, axis=-1, keepdims=True)
        m2 = jnp.mean(dxh * xh, axis=-1, keepdims=True)
        dproj_ref[:, D:2 * D] = ((rstd * (dxh - m1 - xh * m2)) * dgelu_v).astype(BF)

        @pl.when(i == nt - 1)
        def _():
            vec_ref[...] = jnp.zeros((8, D), F32)
            for j in range(2):
                vec_ref[j:j + 1, :] = jnp.sum(accs[j], axis=0, keepdims=True)
            row = lax.broadcasted_iota(jnp.int32, (CHUNK, CHUNK), 0)
            col = lax.broadcasted_iota(jnp.int32, (CHUNK, CHUNK), 1)
            for g in range(NGRP):
                dws_ref[g] = jnp.where(row >= col, dws_ref[g], 0.0)
                gs = jnp.sum(dbs_s[:, g * GW:(g + 1) * GW], axis=1, keepdims=True)
                dbs_ref[:, g * GW:(g + 1) * GW] = jnp.broadcast_to(gs, (CHUNK, GW))
            exchange.finish()

    tile = lambda w: pl.BlockSpec((tm, w), lambda i: (i, 0))
    vec = _const_spec((1, D))
    any_spec = pl.BlockSpec(memory_space=pl.ANY)
    outs = pl.pallas_call(
        body, name="mixer_bwd_pre", grid=(nt,),
        in_specs=[tile(D), pl.BlockSpec((tm, 2 * D), lambda i: (i, 1)), pl.BlockSpec((tm, 2 * D), lambda i: (i, 2)),
                  tile(D), tile(D), vec, vec, _const_spec(wtr.shape), _const_spec(wtrt.shape),
                  _const_spec(bias.shape), _const_spec((D, D)), _const_spec((D, D)), _const_spec((D, D))]
        + [any_spec] * n,
        out_specs=[tile(4 * D), tile(D), tile(D), tile(D), pl.BlockSpec((8, D), lambda i: (0, 0)),
                   pl.BlockSpec((NGRP, CHUNK, CHUNK), lambda i: (0, 0, 0)),
                   pl.BlockSpec((CHUNK, D), lambda i: (0, 0))] + [any_spec] * n,
        out_shape=[jax.ShapeDtypeStruct((s, 4 * D), BF), jax.ShapeDtypeStruct((s, D), F32),
                   jax.ShapeDtypeStruct((s, D), BF), jax.ShapeDtypeStruct((s, D), BF),
                   jax.ShapeDtypeStruct((8, D), F32), jax.ShapeDtypeStruct((NGRP, CHUNK, CHUNK), F32),
                   jax.ShapeDtypeStruct((CHUNK, D), F32)]
        + [jax.ShapeDtypeStruct(p.shape, p.dtype) for p in parts],
        scratch_shapes=[pltpu.VMEM((tm, D), BF), pltpu.VMEM((tm, D), F32), pltpu.VMEM((tm, D), BF),
                        pltpu.VMEM((tm, D), F32), pltpu.VMEM((2, RC, D), F32), pltpu.VMEM((CHUNK, D), F32)]
        + _gather_sems(n),
        compiler_params=_params(("arbitrary",)),
    )(dx1b, proj, proj, pa, pb, lng, lnb, wtr, wtrt, bias, wpa, wpb, wo, *parts)
    return outs[:7], outs[7:]


def _mixer_bwd_seq_call(dprojb, dya, proj, h, x, dx1, gates, gmix, w_in_all, cw, lam, wax, parts):
    s = dya.shape[0]
    tm = min(TM_MIX, s)
    nt = s // tm
    tb = tm // 8
    n = len(parts)
    nb, _, wb = w_in_all.shape

    def body(dpb_ref, dya_ref, xg_ref, xh8_ref, h_ref, hh8_ref, x_ref, dx1_ref, xc_s, r_s, ig_s, a_s, m_s,
             gmix_ref, win_ref, cw_ref, lam_ref, wax_ref, *rest):
        ex_in = rest[:n]
        dpa_ref, dx_ref, vec_ref, dwax_ref = rest[n:n + 4]
        ex_out = rest[n + 4:2 * n + 4]
        lm_s, dh_s, dxc_s, c_s, accs, e_send, e_recv, e_local = rest[2 * n + 4:]
        i = pl.program_id(0)
        exchange = _Exchange(ex_in, ex_out, e_send, e_recv, e_local)

        @pl.when(i == 0)
        def _():
            exchange.start()
            accs[...] = jnp.zeros(accs.shape, F32)
            dwax_ref[...] = jnp.zeros(dwax_ref.shape, F32)
            c_s[...] = jnp.zeros((8, D), F32)
            dxc_s[tm:tm + 8, :] = jnp.zeros((8, D), F32)

        first_tile = i == nt - 1
        prev8 = jnp.where(first_tile, 0.0, xh8_ref[...])
        hprev8 = jnp.where(first_tile, 0.0, hh8_ref[...])
        lamv = lam_ref[...]
        sp = _softplus(-lamv)
        hv = h_ref[...]
        g, dg = _gelu_grad(xg_ref[:, D:2 * D])
        dya = dya_ref[...]
        lm_s[...] = dya * g
        drg = ((dya * hv) * dg).astype(BF)
        dpa_ref[:, D:2 * D] = drg
        dpa_ref[:, 2 * D:6 * D] = dpb_ref[...]

        def dpb_block(k):
            return _dot_nt(dpb_ref[:, k * wb - 2 * D:(k + 1) * wb - 2 * D], win_ref[k])
        dh = (_dot_nt(drg[:, 0:2 * wb - D], win_ref[1, :, D - wb:wb])
              + _dot_nt(drg[:, 2 * wb - D:D], win_ref[2, :, 0:2 * D - 2 * wb])
              + _dot_nt(dpb_ref[:, 0:3 * wb - 2 * D], win_ref[2, :, 2 * D - 2 * wb:wb]))
        for k in range(3, 6):
            dh = dh + dpb_block(k)
        dh_s[...] = dh

        c_s[0:1, :] = _scan_bwd(a_s, lm_s, c_s[0:1, :], tm)

        hprev = _shift_back(hprev8, h_ref[...], 1)
        for q in range(NQ):
            cs = slice(q * QW, (q + 1) * QW)
            r = r_s[:, cs]
            ig = ig_s[:, cs]
            a = a_s[:, cs]
            m = m_s[:, cs]
            lm = lm_s[:, cs]
            xq = xc_s[:, cs]
            dixc = lm * m
            dla = (lm * hprev[:, cs]) * a - ((lm * (ig * xq)) * (a * a)) / m
            accs[3, :, cs] += _fold(dla * r)
            dza = (dla * (-RG_C * sp[:, cs])) * (r * (1.0 - r))
            dzx = (dixc * xq) * (ig * (1.0 - ig))
            accs[1, :, cs] += _fold(dza)
            accs[2, :, cs] += _fold(dzx)
            dz = jnp.concatenate([dza, dzx], axis=1).astype(BF)
            dxc_s[0:tm, cs] = dixc * ig + _dot_nt(dz, wax_ref[q])
            dwax_ref[q] += _dot_tn(xq.astype(BF), dz)

        cur = dxc_s[0:tm, :]
        nx = dxc_s[tm:tm + 8, :]
        drx = cw_ref[3:4, :] * cur
        for j in (1, 2, 3):
            drx = drx + cw_ref[3 - j:4 - j, :] * _shift_fwd(cur, nx, j)
        accs[0] += _fold(cur)
        rx = xg_ref[:, 0:D]
        accs[7] += _fold(cur * rx)
        for j in (1, 2, 3):
            accs[7 - j] += _fold(cur * _shift_back(prev8, rx, j))
        dxc_s[tm:tm + 8, :] = cur[0:8, :]

        drxb = drx.astype(BF)
        dpa_ref[:, 0:D] = drxb
        dh = dh_s[...] + _dot_nt(drxb[:, 0:wb], win_ref[0]) + _dot_nt(drxb[:, wb:D], win_ref[1, :, 0:D - wb])
        for k in range(6, nb):
            dh = dh + dpb_block(k)
        xx = x_ref[...]
        rn = lax.rsqrt(jnp.mean(xx * xx, axis=-1, keepdims=True) + EPS)
        xhn = xx * rn
        accs[8] += _fold(dh * xhn)
        dxh = dh * gmix_ref[...]
        dx_ref[...] = dx1_ref[...] + rn * (dxh - xhn * jnp.mean(dxh * xhn, axis=-1, keepdims=True))

        @pl.when(i == nt - 1)
        def _():
            vec_ref[...] = jnp.zeros((16, D), F32)
            for j in range(9):
                vec_ref[j:j + 1, :] = jnp.sum(accs[j], axis=0, keepdims=True)
            vec_ref[3:4, :] = vec_ref[3:4, :] * (RG_C * _sigmoid(-lamv))
            exchange.finish()

    rev = lambda w: pl.BlockSpec((tm, w), lambda i: (nt - 1 - i, 0))
    halo = pl.BlockSpec((8, D), lambda i: (jnp.maximum((nt - 1 - i) * tb - 1, 0), 0))
    vec = _const_spec((1, D))
    any_spec = pl.BlockSpec(memory_space=pl.ANY)
    outs = pl.pallas_call(
        body, name="mixer_bwd_seq", grid=(nt,),
        in_specs=[rev(4 * D), rev(D), rev(2 * D), halo, rev(D), halo] + [rev(D)] * 7
        + [vec, _const_spec(w_in_all.shape), _const_spec((4, D)), vec, _const_spec(wax.shape)] + [any_spec] * n,
        out_specs=[rev(6 * D), rev(D), pl.BlockSpec((16, D), lambda i: (0, 0)),
                   pl.BlockSpec((NQ, QW, 2 * QW), lambda i: (0, 0, 0))] + [any_spec] * n,
        out_shape=[jax.ShapeDtypeStruct((s, 6 * D), BF), jax.ShapeDtypeStruct((s, D), F32),
                   jax.ShapeDtypeStruct((16, D), F32), jax.ShapeDtypeStruct((NQ, QW, 2 * QW), F32)]
        + [jax.ShapeDtypeStruct(p.shape, p.dtype) for p in parts],
        scratch_shapes=[pltpu.VMEM((tm, D), F32), pltpu.VMEM((tm, D), F32),
                        pltpu.VMEM((tm + 8, D), F32), pltpu.VMEM((8, D), F32), pltpu.VMEM((9, RC, D), F32)]
        + _gather_sems(n),
        compiler_params=_params(("arbitrary",)),
    )(dprojb, dya, proj, proj, h, h, x, dx1, *gates, gmix, w_in_all, cw, lam, wax, *parts)
    return outs[:4], outs[4:]


def _device_of(d):
    return (d // 4, lax.rem(d // 2, 2), lax.rem(d, 2))


_DW_PLAN = (("C", 1), ("C", 2), ("A", 1), ("C", 3), ("A", 2), ("S", 0), ("A", 3), ("O", 0))
_DW_FLIPS = tuple({"C": 2 * j + 1, "A": 2 * j, "S": 1, "O": 0}[kind] for kind, j in _DW_PLAN)


def _dw_exchange_call(name, order, a, b, a_spec, b_spec, k1, n1, s, small):
    ts = min(TS_DW, s)
    ns = s // ts
    nstep = len(_DW_PLAN)

    def slab(i, order_ref):
        return order_ref[i]

    def body(order_ref, a_ref, b_ref, g_ref, own_ref, recv_ref, gsum_ref, acc, sbuf, stage, send_sems, recv_sems,
             st_send, st_recv, local_sem, rbuf, gacc, send1, recv1, send2, recv2):
        i = pl.program_id(0)
        j = pl.program_id(1)
        me = _me()
        mi = _lin(me)
        sibling = _flip(me, 1)
        allreduce = _SmallAllReduce(g_ref, gacc, rbuf, send1, recv1, send2, recv2)
        pl.when((i == 0) & (j == 0))(allreduce.scatter)
        pl.when((i == nstep // 2) & (j == 0))(allreduce.reduce)
        p = _dot_tn(a_ref[...], b_ref[...])

        @pl.when(j == 0)
        def _():
            acc[...] = p

        @pl.when(j > 0)
        def _():
            acc[...] += p

        def send(step):
            kind, jj = _DW_PLAN[step]
            src = sbuf.at[step % 2]
            if kind == "C":
                return pltpu.make_async_remote_copy(
                    src_ref=src, dst_ref=stage.at[jj - 1], send_sem=st_send.at[jj - 1], recv_sem=st_recv.at[jj - 1],
                    device_id=sibling, device_id_type=MESH)
            to = sibling if kind == "S" else _flip(me, 2 * jj)
            return pltpu.make_async_remote_copy(
                src_ref=src, dst_ref=recv_ref.at[mi], send_sem=send_sems.at[jj], recv_sem=recv_sems.at[jj],
                device_id=to, device_id_type=MESH)

        def arrival(jj):
            frm = sibling if jj == 0 else _flip(me, 2 * jj)
            return pltpu.make_async_remote_copy(
                src_ref=sbuf.at[0], dst_ref=recv_ref.at[_lin(frm)], send_sem=send_sems.at[jj],
                recv_sem=recv_sems.at[jj], device_id=frm, device_id_type=MESH)

        def staged(jj):
            return pltpu.make_async_remote_copy(
                src_ref=sbuf.at[0], dst_ref=stage.at[jj - 1], send_sem=st_send.at[jj - 1], recv_sem=st_recv.at[jj - 1],
                device_id=sibling, device_id_type=MESH)

        for step, (kind, jj) in enumerate(_DW_PLAN):
            @pl.when((i == step) & (j == ns - 1))
            def _(step=step, kind=kind, jj=jj):
                if step >= 2:
                    send(step - 2).wait_send()
                if kind == "A":
                    staged(jj).wait_recv()
                    sbuf[step % 2] = (acc[...] + stage[jj - 1].astype(F32)).astype(BF)
                else:
                    sbuf[step % 2] = acc[...].astype(BF)
                if kind != "O":
                    send(step).start()
                else:
                    own_ref[...] = acc[...]
                    mine = pltpu.make_async_copy(sbuf.at[step % 2], recv_ref.at[mi], local_sem)
                    mine.start()
                    send(step - 1).wait_send()
                    for q in range(4):
                        arrival(q).wait_recv()
                    mine.wait()
                    allreduce.finish()
                    gsum_ref[...] = gacc[...]

    vmem = pl.BlockSpec(memory_space=pltpu.VMEM)
    grid_spec = pltpu.PrefetchScalarGridSpec(
        num_scalar_prefetch=1, grid=(nstep, ns),
        in_specs=[a_spec(ts, slab), b_spec(ts, slab), vmem],
        out_specs=[pl.BlockSpec((k1, n1), lambda i, j, o: (0, 0)), pl.BlockSpec(memory_space=pl.ANY), vmem],
        scratch_shapes=[pltpu.VMEM((k1, n1), F32), pltpu.VMEM((2, k1, n1), BF), pltpu.VMEM((3, k1, n1), BF),
                        pltpu.SemaphoreType.DMA((4,)), pltpu.SemaphoreType.DMA((4,)),
                        pltpu.SemaphoreType.DMA((3,)), pltpu.SemaphoreType.DMA((3,)),
                        pltpu.SemaphoreType.DMA(()), pltpu.VMEM((NDEV, SMALL_PER, D), F32),
                        pltpu.VMEM((SMALL_ROWS, D), F32)]
        + [pltpu.SemaphoreType.DMA((NDEV - 1,))] * 4)
    return pl.pallas_call(
        body, name=name, grid_spec=grid_spec,
        out_shape=[jax.ShapeDtypeStruct((k1, n1), F32), jax.ShapeDtypeStruct((NDEV, k1, n1), BF),
                   jax.ShapeDtypeStruct((SMALL_ROWS, D), F32)],
        compiler_params=_params(("arbitrary", "arbitrary")),
    )(order, a, b, small)


def _dw_plain_call(name, me, a, b, a_spec, b_spec, split, k1, n1, s):
    nb = NDEV // split
    r = k1 // split
    ts = min(TS_DW, s)
    ns = s // ts

    def slab(i, me_ref):
        return i

    def body(me_ref, a_ref, b_ref, own_ref, part_ref, acc):
        i = pl.program_id(0)
        j = pl.program_id(1)
        p = _dot_tn(a_ref[...], b_ref[...])

        @pl.when(j == 0)
        def _():
            acc[...] = p

        @pl.when(j > 0)
        def _():
            acc[...] += p

        @pl.when(j == ns - 1)
        def _():
            part_ref[...] = acc[...].astype(BF)

            @pl.when(i == me_ref[0] // split)
            def _():
                off = pl.multiple_of(lax.rem(me_ref[0], split) * r, RC)
                own_ref[...] = acc[pl.ds(off, r), :]

    grid_spec = pltpu.PrefetchScalarGridSpec(
        num_scalar_prefetch=1, grid=(nb, ns),
        in_specs=[a_spec(ts, slab), b_spec(ts, slab)],
        out_specs=[pl.BlockSpec((r, n1), lambda i, j, me_ref: (0, 0)),
                   pl.BlockSpec((None, k1, n1), lambda i, j, me_ref: (i, 0, 0))],
        scratch_shapes=[pltpu.VMEM((k1, n1), F32)])
    own, part = pl.pallas_call(
        body, name=name, grid_spec=grid_spec,
        out_shape=[jax.ShapeDtypeStruct((r, n1), F32), jax.ShapeDtypeStruct((nb, k1, n1), BF)],
        compiler_params=_params(("arbitrary", "arbitrary")),
    )(me, a, b)
    return own, part.reshape(NDEV, r, n1)


def _rows2d(w):
    return lambda ts, slab: pl.BlockSpec((ts, w), lambda i, j, me_ref: (j, 0))


def _cols2d(w):
    return lambda ts, slab: pl.BlockSpec((ts, w), lambda i, j, me_ref: (j, slab(i, me_ref)))


def _blk3d(w):
    return lambda ts, slab: pl.BlockSpec((None, ts, w), lambda i, j, me_ref: (slab(i, me_ref), j, 0))


_BC1 = 1.0 - ADAM_B1 ** ADAM_STEP
_BC2 = 1.0 - ADAM_B2 ** ADAM_STEP


def _adamw_math(w, g, m, v):
    m = ADAM_B1 * m + (1.0 - ADAM_B1) * g
    v = ADAM_B2 * v + (1.0 - ADAM_B2) * (g * g)
    m_hat = m / _BC1
    v_hat = v / _BC2
    delta = -ADAM_LR * (m_hat / (jnp.sqrt(v_hat) + ADAM_EPS) + ADAM_WD * w)
    return delta, m, v


def _row_tile(r):
    for t in (256, 176, 128, 64, 32, 16, 8):
        if r % t == 0:
            return t
    return r


def _reduce_adamw_call(name, sel, own, recv, w, m, v):
    r, c = own.shape
    tr = _row_tile(r)

    def body(sel_ref, own_ref, recv_ref, w_ref, m_ref, v_ref, g_ref, d_ref, nm_ref, nv_ref):
        g = jnp.zeros((tr, c), F32)
        for sdev in range(NDEV):
            part = jnp.where(sel_ref[sdev] == 1, recv_ref[sdev].astype(F32), 0.0)
            g = g + jnp.where(sel_ref[sdev] == 2, own_ref[...], part)
        g_ref[...] = g
        d_ref[...], nm_ref[...], nv_ref[...] = _adamw_math(w_ref[...], g, m_ref[...], v_ref[...])

    tile = pl.BlockSpec((tr, c), lambda i, me_ref: (i, 0))
    grid_spec = pltpu.PrefetchScalarGridSpec(
        num_scalar_prefetch=1, grid=(r // tr,),
        in_specs=[tile, pl.BlockSpec((NDEV, tr, c), lambda i, me_ref: (0, i, 0)), tile, tile, tile],
        out_specs=[tile] * 4)
    return pl.pallas_call(
        body, name=name, grid_spec=grid_spec,
        out_shape=[jax.ShapeDtypeStruct((r, c), F32)] * 4,
        compiler_params=_params(("parallel",)),
    )(sel, own, recv, w, m, v)


def _adamw_call(name, w, g, m, v):
    r, c = w.shape
    tr = _row_tile(r)

    def body(w_ref, g_ref, m_ref, v_ref, d_ref, nm_ref, nv_ref):
        d_ref[...], nm_ref[...], nv_ref[...] = _adamw_math(w_ref[...], g_ref[...], m_ref[...], v_ref[...])

    tile = pl.BlockSpec((tr, c), lambda i: (i, 0))
    return pl.pallas_call(
        body, name=name, grid=(r // tr,), in_specs=[tile] * 4, out_specs=[tile] * 3,
        out_shape=[jax.ShapeDtypeStruct((r, c), F32)] * 3,
        compiler_params=_params(("parallel",)),
    )(w, g, m, v)


def _me():
    return lax.axis_index("x"), lax.axis_index("y"), lax.axis_index("c")


def _flip(pos, r):
    x, y, c = pos
    return (1 - x if r & 4 else x, 1 - y if r & 2 else y, 1 - c if r & 1 else c)


def _lin(pos):
    return pos[0] * 4 + pos[1] * 2 + pos[2]


class _SmallAllReduce:
    def __init__(self, g_ref, out_ref, rbuf, send1, recv1, send2, recv2):
        self.g, self.out, self.rbuf = g_ref, out_ref, rbuf
        self.sems = (send1, recv1, send2, recv2)
        self.me = _me()
        self.mi = _lin(self.me)

    @staticmethod
    def _rows(d):
        return pl.ds(pl.multiple_of(d * SMALL_PER, 8), SMALL_PER)

    def _scatter(self, r, outgoing):
        peer = _flip(self.me, r)
        src, dst = (_lin(peer), self.mi) if outgoing else (self.mi, _lin(peer))
        return pltpu.make_async_remote_copy(
            src_ref=self.g.at[self._rows(src)], dst_ref=self.rbuf.at[dst],
            send_sem=self.sems[0].at[r - 1], recv_sem=self.sems[1].at[r - 1], device_id=peer, device_id_type=MESH)

    def _spread(self, r, outgoing):
        peer = _flip(self.me, r)
        rows = self._rows(self.mi if outgoing else _lin(peer))
        return pltpu.make_async_remote_copy(
            src_ref=self.out.at[rows], dst_ref=self.out.at[rows],
            send_sem=self.sems[2].at[r - 1], recv_sem=self.sems[3].at[r - 1], device_id=peer, device_id_type=MESH)

    def scatter(self):
        for r in range(1, NDEV):
            self._scatter(r, True).start()
        self.rbuf[self.mi] = self.g[self._rows(self.mi), :]

    def reduce(self):
        for r in range(1, NDEV):
            self._scatter(r, False).wait_recv()
        for r in range(1, NDEV):
            self._scatter(r, True).wait_send()
        tot = self.rbuf[0]
        for d in range(1, NDEV):
            tot = tot + self.rbuf[d]
        self.out[self._rows(self.mi), :] = tot
        for r in range(1, NDEV):
            self._spread(r, True).start()

    def finish(self):
        for r in range(1, NDEV):
            self._spread(r, False).wait_recv()
        for r in range(1, NDEV):
            self._spread(r, True).wait_send()


def _head_blocks(w):
    hpg = w.shape[0] // NQ
    hd = w.shape[1]
    eye = jnp.eye(hpg, dtype=w.dtype)
    blocks = w.reshape(NQ, hpg, hd, 1, hd) * eye[None, :, None, :, None]
    return blocks.reshape(NQ, hpg * hd, hpg * hd)


def _head_unblocks(g):
    hd = 64
    hpg = g.shape[1] // hd
    eye = jnp.eye(hpg, dtype=g.dtype)
    picked = jnp.sum(g.reshape(NQ, hpg, hd, hpg, hd) * eye[None, :, None, :, None], axis=3)
    return picked.reshape(NQ * hpg, hd, hd)


def _local_step(x, tgt, p, me, order, order_dw):
    s = x.shape[0]
    vec = lambda a: a.reshape(1, D)
    gmix, gffn, gfin = vec(p["norm_mix_g"]), vec(p["norm_ffn_g"]), vec(p["norm_final_g"])
    cb, ba, bx, lam = vec(p["conv_b"]), vec(p["rg_ba"]), vec(p["rg_bx"]), vec(p["rg_lambda"])
    lng, lnb = vec(p["sgu_ln_g"]), vec(p["sgu_ln_b"])
    wax = jnp.concatenate([_head_blocks(p["rg_wa"]), _head_blocks(p["rg_wx"])], axis=2).astype(BF)
    tril = jnp.tril(jnp.ones((CHUNK, CHUNK), bool))
    ws = jnp.where(tril[None], p["sgu_ws"], 0.0)
    wtr = ws.astype(BF)
    wtrt = jnp.swapaxes(ws, 1, 2).astype(BF)
    bias = jnp.repeat(p["sgu_bs"].T, GW, axis=1)
    shard = {k: p[k].astype(BF) for k in _BIG}

    proj, h1, w_in, (wpa, wpb, wo, cw) = _proj_gather_call(
        x, gmix, shard["w_in"], order, [shard["w_proj_a"], shard["w_proj_b"], shard["w_out"], p["conv_w"]])
    wpa, wpb, wo = (t.reshape(D, D) for t in (wpa, wpb, wo))
    cw = jnp.swapaxes(cw, 0, 1).reshape(4, D)
    (h, pa, pb, x1, *gates, ya, yb, mg), (wgu, wdn) = _mixer_fwd_call(
        proj, x, cw, cb, ba, bx, lam, lng, lnb, wax, wtr, bias, wpa, wpb, wo, [shard["w_gate_up"], shard["w_down"]])
    wdn = wdn.reshape(NDEV // 2, -1, D)
    nb, _, wb = w_in.shape
    fb = wgu.shape[2]
    nh = wdn.shape[0]
    dx1, dx1b, h2, act, dgu, dx2b, facc = _ffn_call(x1, tgt, gffn, gfin, wgu, wdn)
    assert nb == NDEV and 2 * nh == NDEV
    own_gu, part_gu = _dw_plain_call("dw_gate_up", me, dgu, h2, _blk3d(fb), _rows2d(D), 1, fb, D, s)
    own_dn, part_dn = _dw_plain_call("dw_down", me, act, dx2b, _blk3d(fb), _rows2d(D), 2, fb, D, s)
    (dprojb, dya, dpa, dpb, bvec, dws, dbs), (recv_dn,) = _mixer_bwd_pre_call(
        dx1b, proj, pa, pb, lng, lnb, wtr, wtrt, bias, wpa, wpb, wo, [part_dn])
    own_pa, part_pa = _dw_plain_call("dw_proj_a", me, ya, dpa, _rows2d(D), _rows2d(D), NDEV, D, D, s)
    own_pb, part_pb = _dw_plain_call("dw_proj_b", me, yb, dpb, _rows2d(D), _rows2d(D), NDEV, D, D, s)
    own_wo, part_wo = _dw_plain_call("dw_out", me, mg, dx1b, _rows2d(D), _rows2d(D), NDEV, D, D, s)
    (dproj, dx, svec, dwax), (recv_gu, recv_pa, recv_pb, recv_wo) = _mixer_bwd_seq_call(
        dprojb, dya, proj, h, x, dx1, gates, gmix, w_in, cw, lam, wax, [part_gu, part_pa, part_pb, part_wo])
    small = {
        "norm_mix_g": svec[8], "norm_ffn_g": facc[0], "norm_final_g": facc[1],
        "conv_b": svec[0], "rg_ba": svec[1], "rg_bx": svec[2], "rg_lambda": svec[3],
        "sgu_ln_g": bvec[0], "sgu_ln_b": bvec[1],
        "rg_wa": _head_unblocks(dwax[:, :, 0:QW]), "rg_wx": _head_unblocks(dwax[:, :, QW:2 * QW]),
        "sgu_ws": dws, "sgu_bs": dbs[:, ::GW].T,
    }
    packed = _pack_small(small, svec[4:8], facc[2])
    own_in, recv_in, gsum = _dw_exchange_call("dw_in", order_dw, h1, dproj, _rows2d(D), _cols2d(wb), D, wb, s, packed)
    dw = {
        "w_gate_up": (own_gu, recv_gu), "w_down": (own_dn, recv_dn), "w_proj_a": (own_pa, recv_pa),
        "w_proj_b": (own_pb, recv_pb), "w_out": (own_wo, recv_wo), "w_in": (own_in, recv_in),
    }
    return gsum, dx, dw


_BIG = ("w_in", "w_gate_up", "w_down", "w_proj_a", "w_proj_b", "w_out")
_VEC_ROWS = ("norm_mix_g", "norm_ffn_g", "norm_final_g", "conv_b", "rg_ba", "rg_bx", "rg_lambda",
             "sgu_ln_g", "sgu_ln_b", "sgu_bs")
_WEIGHTS = ("norm_mix_g", "w_in", "conv_w", "conv_b", "rg_wa", "rg_ba", "rg_wx", "rg_bx", "rg_lambda",
            "sgu_ln_g", "sgu_ln_b", "sgu_ws", "sgu_bs", "w_proj_a", "w_proj_b", "w_out", "norm_ffn_g",
            "w_gate_up", "w_down", "norm_final_g")


def _pack_small(t, conv_w, extra=None):
    extra = jnp.zeros((1, D), F32) if extra is None else extra.reshape(1, D)
    head = jnp.concatenate([t[k].reshape(1, D) for k in _VEC_ROWS] + [conv_w, extra, jnp.zeros((1, D), F32)], axis=0)
    return jnp.concatenate([head, t["rg_wa"].reshape(64, D), t["rg_wx"].reshape(64, D), t["sgu_ws"].reshape(128, D),
                            jnp.zeros((SMALL_ROWS - 272, D), F32)], axis=0)


def _unpack_small(a):
    out = {k: a[j] for j, k in enumerate(_VEC_ROWS)}
    out["conv_w"] = a[10:14]
    out["rg_wa"] = a[16:80].reshape(16, 64, 64)
    out["rg_wx"] = a[80:144].reshape(16, 64, 64)
    out["sgu_ws"] = a[144:272].reshape(NGRP, CHUNK, CHUNK)
    return out


def kernel(x, norm_mix_g, w_in, conv_w, conv_b, rg_wa, rg_ba, rg_wx, rg_bx, rg_lambda, sgu_ln_g, sgu_ln_b, sgu_ws, sgu_bs, w_proj_a, w_proj_b, w_out, norm_ffn_g, w_gate_up, w_down, norm_final_g, loss_target, m_norm_mix_g, m_w_in, m_conv_w, m_conv_b, m_rg_wa, m_rg_ba, m_rg_wx, m_rg_bx, m_rg_lambda, m_sgu_ln_g, m_sgu_ln_b, m_sgu_ws, m_sgu_bs, m_w_proj_a, m_w_proj_b, m_w_out, m_norm_ffn_g, m_w_gate_up, m_w_down, m_norm_final_g, v_norm_mix_g, v_w_in, v_conv_w, v_conv_b, v_rg_wa, v_rg_ba, v_rg_wx, v_rg_bx, v_rg_lambda, v_sgu_ln_g, v_sgu_ln_b, v_sgu_ws, v_sgu_bs, v_w_proj_a, v_w_proj_b, v_w_out, v_norm_ffn_g, v_w_gate_up, v_w_down, v_norm_final_g):
    args = dict(locals())
    w = {k: args[k] for k in _WEIGHTS}
    m = {k: args["m_" + k] for k in _WEIGHTS}
    v = {k: args["v_" + k] for k in _WEIGHTS}
    for d in (w, m, v):
        for k in _WEIGHTS:
            if k != "norm_final_g":
                d[k] = d[k][0]
    me = _lin(_me())
    me1 = me.reshape(1).astype(jnp.int32)

    order = jnp.bitwise_xor(me, jnp.array(_PASS_FLIPS, jnp.int32)).astype(jnp.int32)
    order_dw = jnp.bitwise_xor(me, jnp.array(_DW_FLIPS, jnp.int32)).astype(jnp.int32)

    gsum, dx, dw = _local_step(x[0], loss_target[0], w, me1, order, order_dw)

    peer = jnp.bitwise_xor(jnp.arange(NDEV, dtype=jnp.int32), me)
    sel_direct = jnp.where(peer == 0, 2, 1).astype(jnp.int32)
    sel_two_level = jnp.where(peer == 0, 2, jnp.where((peer == 1) | (peer % 2 == 0), 1, 0)).astype(jnp.int32)
    grads, delta, new_m, new_v = {}, {}, {}, {}
    for k in _BIG:
        own, recv = dw[k]
        flip = own.shape != w[k].shape
        wmv = [jnp.swapaxes(t, 0, 1) if flip else t for t in (w[k], m[k], v[k])]
        res = _reduce_adamw_call("adamw_" + k, sel_two_level if k == "w_in" else sel_direct, own, recv, *wmv)
        grads[k], delta[k], new_m[k], new_v[k] = (jnp.swapaxes(t, 0, 1) if flip else t for t in res)

    loss = (0.5 / D) * jnp.sum(gsum[14])
    zc = jnp.zeros((4, D), F32)
    d_s, m_s, v_s = _adamw_call("adamw_small", _pack_small(w, zc), gsum, _pack_small(m, zc), _pack_small(v, zc))
    gs, ds, ms, vs = _unpack_small(gsum), _unpack_small(d_s), _unpack_small(m_s), _unpack_small(v_s)
    g_cw = lax.dynamic_slice(gs["conv_w"], (0, me * 128), (4, 128))
    ds["conv_w"], ms["conv_w"], vs["conv_w"] = _adamw_call("adamw_conv_w", w["conv_w"], g_cw, m["conv_w"], v["conv_w"])
    gs["conv_w"] = g_cw
    for k in _WEIGHTS:
        if k not in _BIG:
            shp = w[k].shape
            grads[k], delta[k], new_m[k], new_v[k] = (t[k].reshape(shp) for t in (gs, ds, ms, vs))

    def lift(t, k):
        return t[k] if k == "norm_final_g" else t[k][None]

    outs = [loss, dx[None]]
    for t in (grads, delta, new_m, new_v):
        outs += [lift(t, k) for k in _WEIGHTS]
    return tuple(outs)
```

```python
import jax
import jax.numpy as jnp
from jax import lax
from jax.experimental import pallas as pl
from jax.experimental.pallas import tpu as pltpu

F32 = jnp.float32
BF = jnp.bfloat16

D = 1024
NDEV = 8
EPS = 1e-6
RG_C = 8.0
CHUNK = 128
NGRP = 8
GW = 128
NQ = 4
QW = 256
RC = 16
SMALL_ROWS = 320
SMALL_PER = SMALL_ROWS // NDEV

ADAM_LR = 0.001
ADAM_B1 = 0.9
ADAM_B2 = 0.999
ADAM_EPS = 1e-08
ADAM_WD = 0.01
ADAM_STEP = 10

VMEM_LIMIT = 60 * 1024 * 1024

MESH = pl.DeviceIdType.MESH


def _rows(n, fn, unroll=2, rc=RC):
    def body(i, c):
        fn(pl.multiple_of(i * rc, rc))
        return c
    lax.fori_loop(0, n // rc, body, 0, unroll=unroll)


def _fold(v):
    return jnp.sum(v.reshape(v.shape[0] // RC, RC, v.shape[1]), axis=0)


def _dot(a, b):
    return jnp.dot(a, b, preferred_element_type=F32)


def _dot_nt(a, b):
    return lax.dot_general(a, b, (((1,), (1,)), ((), ())), preferred_element_type=F32)


def _dot_tn(a, b):
    return lax.dot_general(a, b, (((0,), (0,)), ((), ())), preferred_element_type=F32)


_GC = 0.7978845608028654
_GK = 0.044715


def _gelu(x):
    t = jnp.tanh(_GC * (x + _GK * (x * x * x)))
    return x * (0.5 * (1.0 + t))


def _gelu_grad(x):
    x2 = x * x
    t = jnp.tanh(_GC * (x + _GK * (x2 * x)))
    cdf = 0.5 * (1.0 + t)
    dg = cdf + (0.5 * x) * (1.0 - t * t) * (_GC * (1.0 + (3.0 * _GK) * x2))
    return x * cdf, dg


def _sigmoid(x):
    return jax.nn.sigmoid(x)


def _log1p(e):
    u = 1.0 + e
    d = u - 1.0
    return jnp.where(d == 0.0, e, jnp.log(u) * (e / jnp.where(d == 0.0, 1.0, d)))


def _softplus(z):
    return jnp.maximum(z, 0.0) + _log1p(jnp.exp(-jnp.abs(z)))


def _neg_expm1(z):
    u = jnp.exp(z)
    lu = jnp.log(u)
    k = (1.0 - u) * (z / jnp.where(lu == 0.0, 1.0, lu))
    small = jnp.where(lu == 0.0, -z, k)
    return jnp.where(z > -0.5, small, 1.0 - u)


def _shift_back(prev8, cur, j):
    cat = jnp.concatenate([prev8, cur], axis=0)
    return pltpu.roll(cat, j, 0)[8:8 + cur.shape[0]]


def _shift_fwd(cur, next8, j):
    cat = jnp.concatenate([cur, next8], axis=0)
    n = cat.shape[0]
    return pltpu.roll(cat, n - j, 0)[0:cur.shape[0]]


def _const_spec(shape):
    nd = len(shape)
    return pl.BlockSpec(shape, lambda *_: (0,) * nd, pipeline_mode=pl.Buffered(1))


def _params(sem):
    return pltpu.CompilerParams(dimension_semantics=sem, vmem_limit_bytes=VMEM_LIMIT)


TM_PROJ = 1024
TM_MIX = 256
TM_FFN = 256
TS_DW = 4096


_CHIPS = (4, 2, 6)
_PASS_FLIPS = (0, 1, 4, 2, 6, 5, 3, 7)


class _Gather:
    def __init__(self, ins, outs, send_sems, recv_sems, local_sems):
        self.ins, self.outs = ins, outs
        self.send_sems, self.recv_sems, self.local_sems = send_sems, recv_sems, local_sems
        self.me = _me()
        self.sibling = _flip(self.me, 1)

    def _copy(self, a, kind, block, to, src=None):
        dst = self.outs[a].at[_lin(block)]
        return pltpu.make_async_remote_copy(
            src_ref=dst if src is None else src, dst_ref=dst,
            send_sem=self.send_sems.at[a, kind], recv_sem=self.recv_sems.at[a, kind],
            device_id=to, device_id_type=MESH)

    def _local(self, a):
        return pltpu.make_async_copy(self.ins[a], self.outs[a].at[_lin(self.me)], self.local_sems.at[a])

    def start(self):
        for a in range(len(self.ins)):
            self._local(a).start()
            self._copy(a, 0, self.me, self.sibling, src=self.ins[a]).start()
            for j, f in enumerate(_CHIPS):
                self._copy(a, 1 + j, self.me, _flip(self.me, f), src=self.ins[a]).start()

    def forward(self):
        for j, f in enumerate(_CHIPS):
            for a in range(len(self.ins)):
                self._copy(a, 1 + j, _flip(self.me, f), self.me).wait_recv()
                self._copy(a, 4 + j, _flip(self.me, f), self.sibling).start()

    def finish(self):
        for a in range(len(self.ins)):
            self._copy(a, 0, self.sibling, self.me).wait_recv()
            for j, f in enumerate(_CHIPS):
                self._copy(a, 4 + j, _flip(self.me, f | 1), self.me).wait_recv()
            self._copy(a, 0, self.me, self.sibling, src=self.ins[a]).wait_send()
            for j, f in enumerate(_CHIPS):
                self._copy(a, 1 + j, self.me, _flip(self.me, f), src=self.ins[a]).wait_send()
                self._copy(a, 4 + j, _flip(self.me, f), self.sibling).wait_send()
            self._local(a).wait()


class _Exchange:
    def __init__(self, ins, outs, send_sems, recv_sems, local_sems):
        self.ins, self.outs = ins, outs
        self.send_sems, self.recv_sems, self.local_sems = send_sems, recv_sems, local_sems
        self.me = _me()

    def _copy(self, a, r, outgoing):
        peer = _flip(self.me, r)
        src, dst = (peer, self.me) if outgoing else (self.me, peer)
        return pltpu.make_async_remote_copy(
            src_ref=self.ins[a].at[_lin(src)], dst_ref=self.outs[a].at[_lin(dst)],
            send_sem=self.send_sems.at[a, r - 1], recv_sem=self.recv_sems.at[a, r - 1],
            device_id=peer, device_id_type=MESH)

    def _local(self, a):
        mi = _lin(self.me)
        return pltpu.make_async_copy(self.ins[a].at[mi], self.outs[a].at[mi], self.local_sems.at[a])

    def start(self):
        for a in range(len(self.ins)):
            self._local(a).start()
        for r in range(1, NDEV):
            for a in range(len(self.ins)):
                self._copy(a, r, True).start()

    def finish(self):
        for r in range(1, NDEV):
            for a in range(len(self.ins)):
                self._copy(a, r, False).wait_recv()
        for r in range(1, NDEV):
            for a in range(len(self.ins)):
                self._copy(a, r, True).wait_send()
        for a in range(len(self.ins)):
            self._local(a).wait()


def _gather_sems(n):
    return [pltpu.SemaphoreType.DMA((n, 7)), pltpu.SemaphoreType.DMA((n, 7)), pltpu.SemaphoreType.DMA((n,))]


def _proj_gather_call(x, gmix, w_shard, order, extras):
    s = x.shape[0]
    tm = min(TM_PROJ, s)
    nt = s // tm
    wb = w_shard.shape[1]
    n = len(extras)

    def body(order_ref, x_ref, g_ref, wsh_ref, *rest):
        ex_in = rest[:n]
        proj_ref, h1_hbm, wall_hbm = rest[n:n + 3]
        ex_out = rest[n + 3:2 * n + 3]
        wall, hc, ws_send, ws_recv, ex_send, ex_recv, ex_local, out_sems = rest[2 * n + 3:]
        k = pl.program_id(0)
        i = pl.program_id(1)
        me = _me()
        sibling = _flip(me, 1)
        gather = _Gather(ex_in, ex_out, ex_send, ex_recv, ex_local)

        def wcopy(kind, block, to):
            ref = wall.at[_lin(block)]
            return pltpu.make_async_remote_copy(
                src_ref=ref, dst_ref=ref, send_sem=ws_send.at[kind], recv_sem=ws_recv.at[kind],
                device_id=to, device_id_type=MESH)

        head = i == 0

        @pl.when(head & (k == 0))
        def _():
            wall[_lin(me)] = wsh_ref[...]
            wcopy(0, me, sibling).start()
            for j in (0, 1):
                wcopy(1 + j, me, _flip(me, _CHIPS[j])).start()

        @pl.when(head & (k == 1))
        def _():
            wcopy(0, sibling, me).wait_recv()

        for j, f in enumerate(_CHIPS):
            @pl.when(head & (k == 2 + j))
            def _(j=j, f=f):
                wcopy(1 + j, _flip(me, f), me).wait_recv()
                wcopy(4 + j, _flip(me, f), sibling).start()
                if j == 0:
                    wcopy(3, me, _flip(me, _CHIPS[2])).start()
                if j == 1:
                    gather.start()

            @pl.when(head & (k == 5 + j))
            def _(j=j, f=f):
                wcopy(4 + j, _flip(me, f | 1), me).wait_recv()

        pl.when(head & (k == 6))(gather.forward)

        base = pl.multiple_of(i * tm, tm)

        @pl.when(k == 0)
        def _():
            g = g_ref[...]

            def norm(r0):
                xx = x_ref[pl.ds(r0, RC), :]
                r = lax.rsqrt(jnp.mean(xx * xx, axis=-1, keepdims=True) + EPS)
                hc[pl.ds(base + r0, RC), :] = ((xx * r) * g).astype(BF)
            _rows(tm, norm, unroll=4)

        wk = wall.at[order_ref[k]]
        for r0 in range(0, tm, TM_MIX):
            rs = min(TM_MIX, tm - r0)
            proj_ref[r0:r0 + rs, :] = _dot(hc[pl.ds(base + r0, rs), :], wk[...])

        @pl.when((k == NDEV - 1) & (i == nt - 1))
        def _():
            out_w = pltpu.make_async_copy(wall, wall_hbm, out_sems.at[0])
            out_h = pltpu.make_async_copy(hc, h1_hbm, out_sems.at[1])
            out_w.start()
            out_h.start()
            wcopy(0, me, sibling).wait_send()
            for j, f in enumerate(_CHIPS):
                wcopy(1 + j, me, _flip(me, f)).wait_send()
                wcopy(4 + j, _flip(me, f), sibling).wait_send()
            gather.finish()
            out_w.wait()
            out_h.wait()

    any_spec = pl.BlockSpec(memory_space=pl.ANY)
    grid_spec = pltpu.PrefetchScalarGridSpec(
        num_scalar_prefetch=1, grid=(NDEV, nt),
        in_specs=[pl.BlockSpec((tm, D), lambda k, i, o: (jnp.where(k == 0, i, nt - 1), 0)),
                  pl.BlockSpec((1, D), lambda k, i, o: (0, 0)),
                  pl.BlockSpec(w_shard.shape, lambda k, i, o: (0, 0))] + [any_spec] * n,
        out_specs=[pl.BlockSpec((tm, wb), lambda k, i, o: (i, o[k])), any_spec, any_spec] + [any_spec] * n,
        scratch_shapes=[pltpu.VMEM((NDEV,) + w_shard.shape, BF), pltpu.VMEM((s, D), BF),
                        pltpu.SemaphoreType.DMA((7,)), pltpu.SemaphoreType.DMA((7,))] + _gather_sems(n)
        + [pltpu.SemaphoreType.DMA((2,))])
    outs = pl.pallas_call(
        body, name="proj_gather", grid_spec=grid_spec,
        out_shape=[jax.ShapeDtypeStruct((s, NDEV * wb), F32), jax.ShapeDtypeStruct((s, D), BF),
                   jax.ShapeDtypeStruct((NDEV,) + w_shard.shape, BF)]
        + [jax.ShapeDtypeStruct((NDEV,) + e.shape, e.dtype) for e in extras],
        compiler_params=_params(("arbitrary", "arbitrary")),
    )(order, x, gmix, w_shard, *extras)
    return outs[0], outs[1], outs[2], outs[3:]


def _conv_tile(rx, prev8, cw_ref, cb):
    xc = cb + cw_ref[3:4, :] * rx
    for j in (1, 2, 3):
        xc = xc + cw_ref[3 - j:4 - j, :] * _shift_back(prev8, rx, j)
    return xc


def _gate_tile(xcb, wax_ref, ba, bx, sp, q):
    cs = slice(q * QW, (q + 1) * QW)
    z = _dot(xcb[:, cs], wax_ref[q])
    r = _sigmoid(z[:, 0:QW] + ba[:, cs])
    ig = _sigmoid(z[:, QW:2 * QW] + bx[:, cs])
    return r, ig, (-RG_C * r) * sp[:, cs]


def _scan_fwd(a_s, b_s, out_ref, h0, n):
    rowi = lax.broadcasted_iota(jnp.int32, (8, D), 0)
    masks = [(d, rowi >= d) for d in (1, 2, 4)]

    def block(t, h):
        rows = pl.ds(pl.multiple_of(t * 8, 8), 8)
        a = a_s[rows, :]
        b = b_s[rows, :]
        for d, m in masks:
            b = jnp.where(m, a * pltpu.roll(b, d, 0) + b, b)
            a = jnp.where(m, a * pltpu.roll(a, d, 0), a)
        hb = b + a * h
        out_ref[rows, :] = hb
        return hb[7:8, :]
    return lax.fori_loop(0, n // 8, block, h0, unroll=4)


def _scan_bwd(a_s, lm_s, c0, n):
    rowi = lax.broadcasted_iota(jnp.int32, (8, D), 0)
    masks = [(d, rowi < 8 - d) for d in (1, 2, 4)]
    inner = rowi < 7
    nblk = n // 8

    def block(k, cin):
        rows = pl.ds(pl.multiple_of((nblk - 1 - k) * 8, 8), 8)
        a = a_s[rows, :]
        dh = lm_s[rows, :]
        b = a * dh
        for d, m in masks:
            b = jnp.where(m, a * pltpu.roll(b, 8 - d, 0) + b, b)
            a = jnp.where(m, a * pltpu.roll(a, 8 - d, 0), a)
        mu = b + a * cin
        lm_s[rows, :] = dh + jnp.where(inner, pltpu.roll(mu, 7, 0), cin)
        return mu[0:1, :]
    return lax.fori_loop(0, nblk, block, c0, unroll=4)


def _mixer_fwd_call(proj, x, cw, cb, ba, bx, lam, lng, lnb, wax, wtr, bias, wpa, wpb, wo, shards):
    s = x.shape[0]
    tm = min(TM_MIX, s)
    nt = s // tm
    pw = proj.shape[1]
    n = len(shards)

    def body(proj_ref, x_ref, cw_ref, cb_ref, ba_ref, bx_ref, lam_ref, lng_ref, lnb_ref, wax_ref, wtr_ref,
             bias_ref, wpa_ref, wpb_ref, wo_ref, *rest):
        sh_in = rest[:n]
        h_ref, pa_ref, pb_ref, x1_ref, xc_ref, r_ref, ig_ref, a_s, m_ref, ya_ref, yb_ref, mg_ref = rest[n:n + 12]
        sh_out = rest[n + 12:2 * n + 12]
        prev_s, b_s, hc_s, vn_s, mx_s, g_send, g_recv, g_local = rest[2 * n + 12:]
        i = pl.program_id(0)
        gather = _Gather(sh_in, sh_out, g_send, g_recv, g_local)

        @pl.when(i == 0)
        def _():
            gather.start()
            prev_s[...] = jnp.zeros((8, D), F32)
            hc_s[...] = jnp.zeros((8, D), F32)

        pl.when(i == nt // 2)(gather.forward)

        rx = proj_ref[:, 0:D]
        xc = _conv_tile(rx, prev_s[...], cw_ref, cb_ref[...])
        prev_s[...] = rx[tm - 8:tm, :]
        xc_ref[...] = xc
        xcb = xc.astype(BF)
        sp = _softplus(-lam_ref[...])
        ba = ba_ref[...]
        bx = bx_ref[...]
        for q in range(NQ):
            cs = slice(q * QW, (q + 1) * QW)
            r, ig, la = _gate_tile(xcb, wax_ref, ba, bx, sp, q)
            mq = jnp.sqrt(_neg_expm1(2.0 * la))
            r_ref[:, cs] = r
            ig_ref[:, cs] = ig
            m_ref[:, cs] = mq
            a_s[:, cs] = jnp.exp(la)
            b_s[:, cs] = mq * (ig * xc[:, cs])

        gv = _gelu(proj_ref[:, 3 * D:4 * D])
        dv = gv - jnp.mean(gv, axis=-1, keepdims=True)
        var = jnp.mean(dv * dv, axis=-1, keepdims=True)
        vn_s[...] = ((dv * lax.rsqrt(var + EPS)) * lng_ref[...] + lnb_ref[...]).astype(BF)
        nc = tm // CHUNK
        for c in range(nc):
            rs = slice(c * CHUNK, (c + 1) * CHUNK)
            for g in range(NGRP):
                cs = slice(g * GW, (g + 1) * GW)
                mx_s[rs, cs] = _dot(wtr_ref[g], vn_s[rs, cs])
        mixed = mx_s[...] + jnp.concatenate([bias_ref[...]] * nc, axis=0)
        yb = (_gelu(proj_ref[:, 2 * D:3 * D]) * mixed).astype(BF)
        yb_ref[...] = yb
        pb_ref[...] = _dot(yb, wpb_ref[...])

        hc_s[0:1, :] = _scan_fwd(a_s, b_s, h_ref, hc_s[0:1, :], tm)

        ya = (_gelu(proj_ref[:, D:2 * D]) * h_ref[...]).astype(BF)
        ya_ref[...] = ya
        pa = _dot(ya, wpa_ref[...])
        pa_ref[...] = pa
        mg = (_sigmoid(proj_ref[:, 4 * D:5 * D]) * pa + _sigmoid(proj_ref[:, 5 * D:6 * D]) * pb_ref[...]).astype(BF)
        mg_ref[...] = mg
        x1_ref[...] = x_ref[...] + _dot(mg, wo_ref[...])

        pl.when(i == nt - 1)(gather.finish)

    tile = lambda w: pl.BlockSpec((tm, w), lambda i: (i, 0))
    vec = _const_spec((1, D))
    any_spec = pl.BlockSpec(memory_space=pl.ANY)
    outs = pl.pallas_call(
        body, name="mixer_fwd", grid=(nt,),
        in_specs=[tile(pw), tile(D), _const_spec((4, D)), vec, vec, vec, vec, vec, vec,
                  _const_spec(wax.shape), _const_spec(wtr.shape), _const_spec(bias.shape),
                  _const_spec((D, D)), _const_spec((D, D)), _const_spec((D, D))] + [any_spec] * n,
        out_specs=[tile(D)] * 12 + [any_spec] * n,
        out_shape=[jax.ShapeDtypeStruct((s, D), F32)] * 9 + [jax.ShapeDtypeStruct((s, D), BF)] * 3
        + [jax.ShapeDtypeStruct((NDEV,) + e.shape, e.dtype) for e in shards],
        scratch_shapes=[pltpu.VMEM((8, D), F32), pltpu.VMEM((tm, D), F32),
                        pltpu.VMEM((8, D), F32), pltpu.VMEM((tm, D), BF), pltpu.VMEM((tm, D), F32)]
        + _gather_sems(n),
        compiler_params=_params(("arbitrary",)),
    )(proj, x, cw, cb, ba, bx, lam, lng, lnb, wax, wtr, bias, wpa, wpb, wo, *shards)
    return outs[:12], outs[12:]


def _ffn_call(x1, tgt, gffn, gfin, wgu, wdn):
    s = x1.shape[0]
    tm = min(TM_FFN, s)
    nt = s // tm
    nh = wdn.shape[0]
    fb = wgu.shape[2]

    def body(x1_ref, tgt_ref, gffn_ref, gfin_ref, wgu_ref, wdn_ref,
             dx1_ref, dx1b_ref, h2_ref, act_ref, dgu_ref, dx2b_ref, acc_ref,
             g_s, u_s, dx2_s, accs):
        i = pl.program_id(0)

        @pl.when(i == 0)
        def _():
            accs[...] = jnp.zeros(accs.shape, F32)

        gffn = gffn_ref[...]
        gfin = gfin_ref[...]

        x1 = x1_ref[...]
        r2 = lax.rsqrt(jnp.mean(x1 * x1, axis=-1, keepdims=True) + EPS)
        xh2 = x1 * r2
        h2 = (xh2 * gffn).astype(BF)
        h2_ref[...] = h2

        for k in range(nh):
            g = _dot(h2, wgu_ref[k])
            u = _dot(h2, wgu_ref[k + nh])
            g_s[k] = g
            u_s[k] = u
            act_ref[k] = ((g * _sigmoid(g)) * u).astype(BF)

        x2 = x1
        for k in range(nh):
            x2 = x2 + _dot(act_ref[k], wdn_ref[k])

        r3 = lax.rsqrt(jnp.mean(x2 * x2, axis=-1, keepdims=True) + EPS)
        xh = x2 * r3
        err = xh * gfin - tgt_ref[...]
        accs[2] += _fold(err * err)
        dy = err * (1.0 / D)
        accs[1] += _fold(dy * xh)
        dxh = dy * gfin
        dx2 = r3 * (dxh - xh * jnp.mean(dxh * xh, axis=-1, keepdims=True))
        dx2_s[...] = dx2
        dx2b = dx2.astype(BF)
        dx2b_ref[...] = dx2b

        for k in range(nh):
            da = _dot_nt(dx2b, wdn_ref[k])
            g = g_s[k]
            sg = _sigmoid(g)
            dgu_ref[k] = ((da * u_s[k]) * (sg * (1.0 + g * (1.0 - sg)))).astype(BF)
            dgu_ref[k + nh] = (da * (g * sg)).astype(BF)

        dh2 = _dot_nt(dgu_ref[0], wgu_ref[0])
        for k in range(1, 2 * nh):
            dh2 = dh2 + _dot_nt(dgu_ref[k], wgu_ref[k])

        accs[0] += _fold(dh2 * xh2)
        dxh = dh2 * gffn
        dx1 = dx2_s[...] + r2 * (dxh - xh2 * jnp.mean(dxh * xh2, axis=-1, keepdims=True))
        dx1_ref[...] = dx1
        dx1b_ref[...] = dx1.astype(BF)

        @pl.when(i == nt - 1)
        def _():
            acc_ref[...] = jnp.zeros((8, D), F32)
            for j in range(3):
                acc_ref[j:j + 1, :] = jnp.sum(accs[j], axis=0, keepdims=True)

    tile = lambda w: pl.BlockSpec((tm, w), lambda i: (i, 0))
    vec = _const_spec((1, D))
    return pl.pallas_call(
        body, name="ffn", grid=(nt,),
        in_specs=[tile(D), tile(D), vec, vec, _const_spec(wgu.shape), _const_spec(wdn.shape)],
        out_specs=[tile(D), tile(D), tile(D),
                   pl.BlockSpec((nh, tm, fb), lambda i: (0, i, 0)),
                   pl.BlockSpec((2 * nh, tm, fb), lambda i: (0, i, 0)),
                   tile(D), pl.BlockSpec((8, D), lambda i: (0, 0))],
        out_shape=[jax.ShapeDtypeStruct((s, D), F32), jax.ShapeDtypeStruct((s, D), BF),
                   jax.ShapeDtypeStruct((s, D), BF), jax.ShapeDtypeStruct((nh, s, fb), BF),
                   jax.ShapeDtypeStruct((2 * nh, s, fb), BF), jax.ShapeDtypeStruct((s, D), BF),
                   jax.ShapeDtypeStruct((8, D), F32)],
        scratch_shapes=[pltpu.VMEM((nh, tm, fb), F32), pltpu.VMEM((nh, tm, fb), F32), pltpu.VMEM((tm, D), F32),
                        pltpu.VMEM((3, RC, D), F32)],
        compiler_params=_params(("arbitrary",)),
    )(x1, tgt, gffn, gfin, wgu, wdn)


def _mixer_bwd_pre_call(dx1b, proj, pa, pb, lng, lnb, wtr, wtrt, bias, wpa, wpb, wo, parts):
    s = dx1b.shape[0]
    tm = min(TM_MIX, s)
    nt = s // tm
    pw = proj.shape[1]
    n = len(parts)

    def body(dx1b_ref, uv_ref, gg_ref, pa_ref, pb_ref, lng_ref, lnb_ref, wtr_ref, wtrt_ref, bias_ref,
             wpa_ref, wpb_ref, wo_ref, *rest):
        ex_in = rest[:n]
        dproj_ref, dya_ref, dpa_ref, dpb_ref, vec_ref, dws_ref, dbs_ref = rest[n:n + 7]
        ex_out = rest[n + 7:2 * n + 7]
        vn_s, mx_s, dmx_s, dvn_s, accs, dbs_s, e_send, e_recv, e_local = rest[2 * n + 7:]
        i = pl.program_id(0)
        exchange = _Exchange(ex_in, ex_out, e_send, e_recv, e_local)

        @pl.when(i == 0)
        def _():
            exchange.start()
            accs[...] = jnp.zeros(accs.shape, F32)
            dbs_s[...] = jnp.zeros(dbs_s.shape, F32)
            dws_ref[...] = jnp.zeros(dws_ref.shape, F32)

        dm = _dot_nt(dx1b_ref[...], wo_ref[...])
        sa = _sigmoid(gg_ref[:, 0:D])
        sb = _sigmoid(gg_ref[:, D:2 * D])
        dpa = dm * sa
        dpb = dm * sb
        dpab = dpa.astype(BF)
        dpbb = dpb.astype(BF)
        dpa_ref[...] = dpab
        dpb_ref[...] = dpbb
        dproj_ref[:, 2 * D:3 * D] = ((dpa * pa_ref[...]) * (1.0 - sa)).astype(BF)
        dproj_ref[:, 3 * D:4 * D] = ((dpb * pb_ref[...]) * (1.0 - sb)).astype(BF)

        dya_ref[...] = _dot_nt(dpab, wpa_ref[...])
        dyb = _dot_nt(dpbb, wpb_ref[...])

        lng = lng_ref[...]
        gv, dgelu_v = _gelu_grad(uv_ref[:, D:2 * D])
        dv = gv - jnp.mean(gv, axis=-1, keepdims=True)
        rstd = lax.rsqrt(jnp.mean(dv * dv, axis=-1, keepdims=True) + EPS)
        xh = dv * rstd
        vn_s[...] = (xh * lng + lnb_ref[...]).astype(BF)

        nc = tm // CHUNK
        for c in range(nc):
            rs = slice(c * CHUNK, (c + 1) * CHUNK)
            for g in range(NGRP):
                cs = slice(g * GW, (g + 1) * GW)
                mx_s[rs, cs] = _dot(wtr_ref[g], vn_s[rs, cs])

        gu, dgelu_u = _gelu_grad(uv_ref[:, 0:D])
        mixed = mx_s[...] + jnp.concatenate([bias_ref[...]] * nc, axis=0)
        dproj_ref[:, 0:D] = ((dyb * mixed) * dgelu_u).astype(BF)
        dmx = dyb * gu
        dmx_s[...] = dmx.astype(BF)
        dbs_s[...] += jnp.sum(dmx.reshape(nc, CHUNK, D), axis=0)

        for c in range(nc):
            rs = slice(c * CHUNK, (c + 1) * CHUNK)
            for g in range(NGRP):
                cs = slice(g * GW, (g + 1) * GW)
                dvn_s[rs, cs] = _dot(wtrt_ref[g], dmx_s[rs, cs])
                dws_ref[g] += _dot_nt(dmx_s[rs, cs], vn_s[rs, cs])

        dvn = dvn_s[...]
        accs[0] += _fold(dvn * xh)
        accs[1] += _fold(dvn)
        dxh = dvn * lng
        m1 = jnp.mean(dxh, axis=-1, keepdims=True)
        m2 = jnp.mean(dxh * xh, axis=-1, keepdims=True)
        dproj_ref[:, D:2 * D] = ((rstd * (dxh - m1 - xh * m2)) * dgelu_v).astype(BF)

        @pl.when(i == nt - 1)
        def _():
            vec_ref[...] = jnp.zeros((8, D), F32)
            for j in range(2):
                vec_ref[j:j + 1, :] = jnp.sum(accs[j], axis=0, keepdims=True)
            row = lax.broadcasted_iota(jnp.int32, (CHUNK, CHUNK), 0)
            col = lax.broadcasted_iota(jnp.int32, (CHUNK, CHUNK), 1)
            for g in range(NGRP):
                dws_ref[g] = jnp.where(row >= col, dws_ref[g], 0.0)
                gs = jnp.sum(dbs_s[:, g * GW:(g + 1) * GW], axis=1, keepdims=True)
                dbs_ref[:, g * GW:(g + 1) * GW] = jnp.broadcast_to(gs, (CHUNK, GW))
            exchange.finish()

    tile = lambda w: pl.BlockSpec((tm, w), lambda i: (i, 0))
    vec = _const_spec((1, D))
    any_spec = pl.BlockSpec(memory_space=pl.ANY)
    outs = pl.pallas_call(
        body, name="mixer_bwd_pre", grid=(nt,),
        in_specs=[tile(D), pl.BlockSpec((tm, 2 * D), lambda i: (i, 1)), pl.BlockSpec((tm, 2 * D), lambda i: (i, 2)),
                  tile(D), tile(D), vec, vec, _const_spec(wtr.shape), _const_spec(wtrt.shape),
                  _const_spec(bias.shape), _const_spec((D, D)), _const_spec((D, D)), _const_spec((D, D))]
        + [any_spec] * n,
        out_specs=[tile(4 * D), tile(D), tile(D), tile(D), pl.BlockSpec((8, D), lambda i: (0, 0)),
                   pl.BlockSpec((NGRP, CHUNK, CHUNK), lambda i: (0, 0, 0)),
                   pl.BlockSpec((CHUNK, D), lambda i: (0, 0))] + [any_spec] * n,
        out_shape=[jax.ShapeDtypeStruct((s, 4 * D), BF), jax.ShapeDtypeStruct((s, D), F32),
                   jax.ShapeDtypeStruct((s, D), BF), jax.ShapeDtypeStruct((s, D), BF),
                   jax.ShapeDtypeStruct((8, D), F32), jax.ShapeDtypeStruct((NGRP, CHUNK, CHUNK), F32),
                   jax.ShapeDtypeStruct((CHUNK, D), F32)]
        + [jax.ShapeDtypeStruct(p.shape, p.dtype) for p in parts],
        scratch_shapes=[pltpu.VMEM((tm, D), BF), pltpu.VMEM((tm, D), F32), pltpu.VMEM((tm, D), BF),
                        pltpu.VMEM((tm, D), F32), pltpu.VMEM((2, RC, D), F32), pltpu.VMEM((CHUNK, D), F32)]
        + _gather_sems(n),
        compiler_params=_params(("arbitrary",)),
    )(dx1b, proj, proj, pa, pb, lng, lnb, wtr, wtrt, bias, wpa, wpb, wo, *parts)
    return outs[:7], outs[7:]


def _mixer_bwd_seq_call(dprojb, dya, proj, h, x, dx1, gates, gmix, w_in_all, cw, lam, wax, parts):
    s = dya.shape[0]
    tm = min(TM_MIX, s)
    nt = s // tm
    tb = tm // 8
    n = len(parts)
    nb, _, wb = w_in_all.shape

    def body(dpb_ref, dya_ref, xg_ref, xh8_ref, h_ref, hh8_ref, x_ref, dx1_ref, xc_s, r_s, ig_s, a_s, m_s,
             gmix_ref, win_ref, cw_ref, lam_ref, wax_ref, *rest):
        ex_in = rest[:n]
        dpa_ref, dx_ref, vec_ref, dwax_ref = rest[n:n + 4]
        ex_out = rest[n + 4:2 * n + 4]
        lm_s, dh_s, dxc_s, c_s, accs, e_send, e_recv, e_local = rest[2 * n + 4:]
        i = pl.program_id(0)
        exchange = _Exchange(ex_in, ex_out, e_send, e_recv, e_local)

        @pl.when(i == 0)
        def _():
            exchange.start()
            accs[...] = jnp.zeros(accs.shape, F32)
            dwax_ref[...] = jnp.zeros(dwax_ref.shape, F32)
            c_s[...] = jnp.zeros((8, D), F32)
            dxc_s[tm:tm + 8, :] = jnp.zeros((8, D), F32)

        first_tile = i == nt - 1
        prev8 = jnp.where(first_tile, 0.0, xh8_ref[...])
        hprev8 = jnp.where(first_tile, 0.0, hh8_ref[...])
        lamv = lam_ref[...]
        sp = _softplus(-lamv)
        hv = h_ref[...]
        g, dg = _gelu_grad(xg_ref[:, D:2 * D])
        dya = dya_ref[...]
        lm_s[...] = dya * g
        drg = ((dya * hv) * dg).astype(BF)
        dpa_ref[:, D:2 * D] = drg
        dpa_ref[:, 2 * D:6 * D] = dpb_ref[...]

        def dpb_block(k):
            return _dot_nt(dpb_ref[:, k * wb - 2 * D:(k + 1) * wb - 2 * D], win_ref[k])
        dh = (_dot_nt(drg[:, 0:2 * wb - D], win_ref[1, :, D - wb:wb])
              + _dot_nt(drg[:, 2 * wb - D:D], win_ref[2, :, 0:2 * D - 2 * wb])
              + _dot_nt(dpb_ref[:, 0:3 * wb - 2 * D], win_ref[2, :, 2 * D - 2 * wb:wb]))
        for k in range(3, 6):
            dh = dh + dpb_block(k)
        dh_s[...] = dh

        c_s[0:1, :] = _scan_bwd(a_s, lm_s, c_s[0:1, :], tm)

        hprev = _shift_back(hprev8, h_ref[...], 1)
        for q in range(NQ):
            cs = slice(q * QW, (q + 1) * QW)
            r = r_s[:, cs]
            ig = ig_s[:, cs]
            a = a_s[:, cs]
            m = m_s[:, cs]
            lm = lm_s[:, cs]
            xq = xc_s[:, cs]
            dixc = lm * m
            dla = (lm * hprev[:, cs]) * a - ((lm * (ig * xq)) * (a * a)) / m
            accs[3, :, cs] += _fold(dla * r)
            dza = (dla * (-RG_C * sp[:, cs])) * (r * (1.0 - r))
            dzx = (dixc * xq) * (ig * (1.0 - ig))
            accs[1, :, cs] += _fold(dza)
            accs[2, :, cs] += _fold(dzx)
            dz = jnp.concatenate([dza, dzx], axis=1).astype(BF)
            dxc_s[0:tm, cs] = dixc * ig + _dot_nt(dz, wax_ref[q])
            dwax_ref[q] += _dot_tn(xq.astype(BF), dz)

        cur = dxc_s[0:tm, :]
        nx = dxc_s[tm:tm + 8, :]
        drx = cw_ref[3:4, :] * cur
        for j in (1, 2, 3):
            drx = drx + cw_ref[3 - j:4 - j, :] * _shift_fwd(cur, nx, j)
        accs[0] += _fold(cur)
        rx = xg_ref[:, 0:D]
        accs[7] += _fold(cur * rx)
        for j in (1, 2, 3):
            accs[7 - j] += _fold(cur * _shift_back(prev8, rx, j))
        dxc_s[tm:tm + 8, :] = cur[0:8, :]

        drxb = drx.astype(BF)
        dpa_ref[:, 0:D] = drxb
        dh = dh_s[...] + _dot_nt(drxb[:, 0:wb], win_ref[0]) + _dot_nt(drxb[:, wb:D], win_ref[1, :, 0:D - wb])
        for k in range(6, nb):
            dh = dh + dpb_block(k)
        xx = x_ref[...]
        rn = lax.rsqrt(jnp.mean(xx * xx, axis=-1, keepdims=True) + EPS)
        xhn = xx * rn
        accs[8] += _fold(dh * xhn)
        dxh = dh * gmix_ref[...]
        dx_ref[...] = dx1_ref[...] + rn * (dxh - xhn * jnp.mean(dxh * xhn, axis=-1, keepdims=True))

        @pl.when(i == nt - 1)
        def _():
            vec_ref[...] = jnp.zeros((16, D), F32)
            for j in range(9):
                vec_ref[j:j + 1, :] = jnp.sum(accs[j], axis=0, keepdims=True)
            vec_ref[3:4, :] = vec_ref[3:4, :] * (RG_C * _sigmoid(-lamv))
            exchange.finish()

    rev = lambda w: pl.BlockSpec((tm, w), lambda i: (nt - 1 - i, 0))
    halo = pl.BlockSpec((8, D), lambda i: (jnp.maximum((nt - 1 - i) * tb - 1, 0), 0))
    vec = _const_spec((1, D))
    any_spec = pl.BlockSpec(memory_space=pl.ANY)
    outs = pl.pallas_call(
        body, name="mixer_bwd_seq", grid=(nt,),
        in_specs=[rev(4 * D), rev(D), rev(2 * D), halo, rev(D), halo] + [rev(D)] * 7
        + [vec, _const_spec(w_in_all.shape), _const_spec((4, D)), vec, _const_spec(wax.shape)] + [any_spec] * n,
        out_specs=[rev(6 * D), rev(D), pl.BlockSpec((16, D), lambda i: (0, 0)),
                   pl.BlockSpec((NQ, QW, 2 * QW), lambda i: (0, 0, 0))] + [any_spec] * n,
        out_shape=[jax.ShapeDtypeStruct((s, 6 * D), BF), jax.ShapeDtypeStruct((s, D), F32),
                   jax.ShapeDtypeStruct((16, D), F32), jax.ShapeDtypeStruct((NQ, QW, 2 * QW), F32)]
        + [jax.ShapeDtypeStruct(p.shape, p.dtype) for p in parts],
        scratch_shapes=[pltpu.VMEM((tm, D), F32), pltpu.VMEM((tm, D), F32),
                        pltpu.VMEM((tm + 8, D), F32), pltpu.VMEM((8, D), F32), pltpu.VMEM((9, RC, D), F32)]
        + _gather_sems(n),
        compiler_params=_params(("arbitrary",)),
    )(dprojb, dya, proj, proj, h, h, x, dx1, *gates, gmix, w_in_all, cw, lam, wax, *parts)
    return outs[:4], outs[4:]


def _device_of(d):
    return (d // 4, lax.rem(d // 2, 2), lax.rem(d, 2))


_DW_PLAN = (("C", 1), ("C", 2), ("A", 1), ("C", 3), ("A", 2), ("S", 0), ("A", 3), ("O", 0))
_DW_FLIPS = tuple({"C": 2 * j + 1, "A": 2 * j, "S": 1, "O": 0}[kind] for kind, j in _DW_PLAN)


def _dw_exchange_call(name, order, a, b, a_spec, b_spec, k1, n1, s, small):
    ts = min(TS_DW, s)
    ns = s // ts
    nstep = len(_DW_PLAN)

    def slab(i, order_ref):
        return order_ref[i]

    def body(order_ref, a_ref, b_ref, g_ref, own_ref, recv_ref, gsum_ref, acc, sbuf, stage, send_sems, recv_sems,
             st_send, st_recv, local_sem, rbuf, gacc, send1, recv1, send2, recv2):
        i = pl.program_id(0)
        j = pl.program_id(1)
        me = _me()
        mi = _lin(me)
        sibling = _flip(me, 1)
        allreduce = _SmallAllReduce(g_ref, gacc, rbuf, send1, recv1, send2, recv2)
        pl.when((i == 0) & (j == 0))(allreduce.scatter)
        pl.when((i == nstep // 2) & (j == 0))(allreduce.reduce)
        p = _dot_tn(a_ref[...], b_ref[...])

        @pl.when(j == 0)
        def _():
            acc[...] = p

        @pl.when(j > 0)
        def _():
            acc[...] += p

        def send(step):
            kind, jj = _DW_PLAN[step]
            src = sbuf.at[step % 2]
            if kind == "C":
                return pltpu.make_async_remote_copy(
                    src_ref=src, dst_ref=stage.at[jj - 1], send_sem=st_send.at[jj - 1], recv_sem=st_recv.at[jj - 1],
                    device_id=sibling, device_id_type=MESH)
            to = sibling if kind == "S" else _flip(me, 2 * jj)
            return pltpu.make_async_remote_copy(
                src_ref=src, dst_ref=recv_ref.at[mi], send_sem=send_sems.at[jj], recv_sem=recv_sems.at[jj],
                device_id=to, device_id_type=MESH)

        def arrival(jj):
            frm = sibling if jj == 0 else _flip(me, 2 * jj)
            return pltpu.make_async_remote_copy(
                src_ref=sbuf.at[0], dst_ref=recv_ref.at[_lin(frm)], send_sem=send_sems.at[jj],
                recv_sem=recv_sems.at[jj], device_id=frm, device_id_type=MESH)

        def staged(jj):
            return pltpu.make_async_remote_copy(
                src_ref=sbuf.at[0], dst_ref=stage.at[jj - 1], send_sem=st_send.at[jj - 1], recv_sem=st_recv.at[jj - 1],
                device_id=sibling, device_id_type=MESH)

        for step, (kind, jj) in enumerate(_DW_PLAN):
            @pl.when((i == step) & (j == ns - 1))
            def _(step=step, kind=kind, jj=jj):
                if step >= 2:
                    send(step - 2).wait_send()
                if kind == "A":
                    staged(jj).wait_recv()
                    sbuf[step % 2] = (acc[...] + stage[jj - 1].astype(F32)).astype(BF)
                else:
                    sbuf[step % 2] = acc[...].astype(BF)
                if kind != "O":
                    send(step).start()
                else:
                    own_ref[...] = acc[...]
                    mine = pltpu.make_async_copy(sbuf.at[step % 2], recv_ref.at[mi], local_sem)
                    mine.start()
                    send(step - 1).wait_send()
                    for q in range(4):
                        arrival(q).wait_recv()
                    mine.wait()
                    allreduce.finish()
                    gsum_ref[...] = gacc[...]

    vmem = pl.BlockSpec(memory_space=pltpu.VMEM)
    grid_spec = pltpu.PrefetchScalarGridSpec(
        num_scalar_prefetch=1, grid=(nstep, ns),
        in_specs=[a_spec(ts, slab), b_spec(ts, slab), vmem],
        out_specs=[pl.BlockSpec((k1, n1), lambda i, j, o: (0, 0)), pl.BlockSpec(memory_space=pl.ANY), vmem],
        scratch_shapes=[pltpu.VMEM((k1, n1), F32), pltpu.VMEM((2, k1, n1), BF), pltpu.VMEM((3, k1, n1), BF),
                        pltpu.SemaphoreType.DMA((4,)), pltpu.SemaphoreType.DMA((4,)),
                        pltpu.SemaphoreType.DMA((3,)), pltpu.SemaphoreType.DMA((3,)),
                        pltpu.SemaphoreType.DMA(()), pltpu.VMEM((NDEV, SMALL_PER, D), F32),
                        pltpu.VMEM((SMALL_ROWS, D), F32)]
        + [pltpu.SemaphoreType.DMA((NDEV - 1,))] * 4)
    return pl.pallas_call(
        body, name=name, grid_spec=grid_spec,
        out_shape=[jax.ShapeDtypeStruct((k1, n1), F32), jax.ShapeDtypeStruct((NDEV, k1, n1), BF),
                   jax.ShapeDtypeStruct((SMALL_ROWS, D), F32)],
        compiler_params=_params(("arbitrary", "arbitrary")),
    )(order, a, b, small)


def _dw_plain_call(name, me, a, b, a_spec, b_spec, split, k1, n1, s):
    nb = NDEV // split
    r = k1 // split
    ts = min(TS_DW, s)
    ns = s // ts

    def slab(i, me_ref):
        return i

    def body(me_ref, a_ref, b_ref, own_ref, part_ref, acc):
        i = pl.program_id(0)
        j = pl.program_id(1)
        p = _dot_tn(a_ref[...], b_ref[...])

        @pl.when(j == 0)
        def _():
            acc[...] = p

        @pl.when(j > 0)
        def _():
            acc[...] += p

        @pl.when(j == ns - 1)
        def _():
            part_ref[...] = acc[...].astype(BF)

            @pl.when(i == me_ref[0] // split)
            def _():
                off = pl.multiple_of(lax.rem(me_ref[0], split) * r, RC)
                own_ref[...] = acc[pl.ds(off, r), :]

    grid_spec = pltpu.PrefetchScalarGridSpec(
        num_scalar_prefetch=1, grid=(nb, ns),
        in_specs=[a_spec(ts, slab), b_spec(ts, slab)],
        out_specs=[pl.BlockSpec((r, n1), lambda i, j, me_ref: (0, 0)),
                   pl.BlockSpec((None, k1, n1), lambda i, j, me_ref: (i, 0, 0))],
        scratch_shapes=[pltpu.VMEM((k1, n1), F32)])
    own, part = pl.pallas_call(
        body, name=name, grid_spec=grid_spec,
        out_shape=[jax.ShapeDtypeStruct((r, n1), F32), jax.ShapeDtypeStruct((nb, k1, n1), BF)],
        compiler_params=_params(("arbitrary", "arbitrary")),
    )(me, a, b)
    return own, part.reshape(NDEV, r, n1)


def _rows2d(w):
    return lambda ts, slab: pl.BlockSpec((ts, w), lambda i, j, me_ref: (j, 0))


def _cols2d(w):
    return lambda ts, slab: pl.BlockSpec((ts, w), lambda i, j, me_ref: (j, slab(i, me_ref)))


def _blk3d(w):
    return lambda ts, slab: pl.BlockSpec((None, ts, w), lambda i, j, me_ref: (slab(i, me_ref), j, 0))


_BC1 = 1.0 - ADAM_B1 ** ADAM_STEP
_BC2 = 1.0 - ADAM_B2 ** ADAM_STEP


def _adamw_math(w, g, m, v):
    m = ADAM_B1 * m + (1.0 - ADAM_B1) * g
    v = ADAM_B2 * v + (1.0 - ADAM_B2) * (g * g)
    m_hat = m / _BC1
    v_hat = v / _BC2
    delta = -ADAM_LR * (m_hat / (jnp.sqrt(v_hat) + ADAM_EPS) + ADAM_WD * w)
    return delta, m, v


def _row_tile(r):
    for t in (256, 176, 128, 64, 32, 16, 8):
        if r % t == 0:
            return t
    return r


def _reduce_adamw_call(name, sel, own, recv, w, m, v):
    r, c = own.shape
    tr = _row_tile(r)

    def body(sel_ref, own_ref, recv_ref, w_ref, m_ref, v_ref, g_ref, d_ref, nm_ref, nv_ref):
        g = jnp.zeros((tr, c), F32)
        for sdev in range(NDEV):
            part = jnp.where(sel_ref[sdev] == 1, recv_ref[sdev].astype(F32), 0.0)
            g = g + jnp.where(sel_ref[sdev] == 2, own_ref[...], part)
        g_ref[...] = g
        d_ref[...], nm_ref[...], nv_ref[...] = _adamw_math(w_ref[...], g, m_ref[...], v_ref[...])

    tile = pl.BlockSpec((tr, c), lambda i, me_ref: (i, 0))
    grid_spec = pltpu.PrefetchScalarGridSpec(
        num_scalar_prefetch=1, grid=(r // tr,),
        in_specs=[tile, pl.BlockSpec((NDEV, tr, c), lambda i, me_ref: (0, i, 0)), tile, tile, tile],
        out_specs=[tile] * 4)
    return pl.pallas_call(
        body, name=name, grid_spec=grid_spec,
        out_shape=[jax.ShapeDtypeStruct((r, c), F32)] * 4,
        compiler_params=_params(("parallel",)),
    )(sel, own, recv, w, m, v)


def _adamw_call(name, w, g, m, v):
    r, c = w.shape
    tr = _row_tile(r)

    def body(w_ref, g_ref, m_ref, v_ref, d_ref, nm_ref, nv_ref):
        d_ref[...], nm_ref[...], nv_ref[...] = _adamw_math(w_ref[...], g_ref[...], m_ref[...], v_ref[...])

    tile = pl.BlockSpec((tr, c), lambda i: (i, 0))
    return pl.pallas_call(
        body, name=name, grid=(r // tr,), in_specs=[tile] * 4, out_specs=[tile] * 3,
        out_shape=[jax.ShapeDtypeStruct((r, c), F32)] * 3,
        compiler_params=_params(("parallel",)),
    )(w, g, m, v)


def _me():
    return lax.axis_index("x"), lax.axis_index("y"), lax.axis_index("c")


def _flip(pos, r):
    x, y, c = pos
    return (1 - x if r & 4 else x, 1 - y if r & 2 else y, 1 - c if r & 1 else c)


def _lin(pos):
    return pos[0] * 4 + pos[1] * 2 + pos[2]


class _SmallAllReduce:
    def __init__(self, g_ref, out_ref, rbuf, send1, recv1, send2, recv2):
        self.g, self.out, self.rbuf = g_ref, out_ref, rbuf
        self.sems = (send1, recv1, send2, recv2)
        self.me = _me()
        self.mi = _lin(self.me)

    @staticmethod
    def _rows(d):
        return pl.ds(pl.multiple_of(d * SMALL_PER, 8), SMALL_PER)

    def _scatter(self, r, outgoing):
        peer = _flip(self.me, r)
        src, dst = (_lin(peer), self.mi) if outgoing else (self.mi, _lin(peer))
        return pltpu.make_async_remote_copy(
            src_ref=self.g.at[self._rows(src)], dst_ref=self.rbuf.at[dst],
            send_sem=self.sems[0].at[r - 1], recv_sem=self.sems[1].at[r - 1], device_id=peer, device_id_type=MESH)

    def _spread(self, r, outgoing):
        peer = _flip(self.me, r)
        rows = self._rows(self.mi if outgoing else _lin(peer))
        return pltpu.make_async_remote_copy(
            src_ref=self.out.at[rows], dst_ref=self.out.at[rows],
            send_sem=self.sems[2].at[r - 1], recv_sem=self.sems[3].at[r - 1], device_id=peer, device_id_type=MESH)

    def scatter(self):
        for r in range(1, NDEV):
            self._scatter(r, True).start()
        self.rbuf[self.mi] = self.g[self._rows(self.mi), :]

    def reduce(self):
        for r in range(1, NDEV):
            self._scatter(r, False).wait_recv()
        for r in range(1, NDEV):
            self._scatter(r, True).wait_send()
        tot = self.rbuf[0]
        for d in range(1, NDEV):
            tot = tot + self.rbuf[d]
        self.out[self._rows(self.mi), :] = tot
        for r in range(1, NDEV):
            self._spread(r, True).start()

    def finish(self):
        for r in range(1, NDEV):
            self._spread(r, False).wait_recv()
        for r in range(1, NDEV):
            self._spread(r, True).wait_send()


def _head_blocks(w):
    hpg = w.shape[0] // NQ
    hd = w.shape[1]
    eye = jnp.eye(hpg, dtype=w.dtype)
    blocks = w.reshape(NQ, hpg, hd, 1, hd) * eye[None, :, None, :, None]
    return blocks.reshape(NQ, hpg * hd, hpg * hd)


def _head_unblocks(g):
    hd = 64
    hpg = g.shape[1] // hd
    eye = jnp.eye(hpg, dtype=g.dtype)
    picked = jnp.sum(g.reshape(NQ, hpg, hd, hpg, hd) * eye[None, :, None, :, None], axis=3)
    return picked.reshape(NQ * hpg, hd, hd)


def _local_step(x, tgt, p, me, order, order_dw):
    s = x.shape[0]
    vec = lambda a: a.reshape(1, D)
    gmix, gffn, gfin = vec(p["norm_mix_g"]), vec(p["norm_ffn_g"]), vec(p["norm_final_g"])
    cb, ba, bx, lam = vec(p["conv_b"]), vec(p["rg_ba"]), vec(p["rg_bx"]), vec(p["rg_lambda"])
    lng, lnb = vec(p["sgu_ln_g"]), vec(p["sgu_ln_b"])
    wax = jnp.concatenate([_head_blocks(p["rg_wa"]), _head_blocks(p["rg_wx"])], axis=2).astype(BF)
    tril = jnp.tril(jnp.ones((CHUNK, CHUNK), bool))
    ws = jnp.where(tril[None], p["sgu_ws"], 0.0)
    wtr = ws.astype(BF)
    wtrt = jnp.swapaxes(ws, 1, 2).astype(BF)
    bias = jnp.repeat(p["sgu_bs"].T, GW, axis=1)
    shard = {k: p[k].astype(BF) for k in _BIG}

    proj, h1, w_in, (wpa, wpb, wo, cw) = _proj_gather_call(
        x, gmix, shard["w_in"], order, [shard["w_proj_a"], shard["w_proj_b"], shard["w_out"], p["conv_w"]])
    wpa, wpb, wo = (t.reshape(D, D) for t in (wpa, wpb, wo))
    cw = jnp.swapaxes(cw, 0, 1).reshape(4, D)
    (h, pa, pb, x1, *gates, ya, yb, mg), (wgu, wdn) = _mixer_fwd_call(
        proj, x, cw, cb, ba, bx, lam, lng, lnb, wax, wtr, bias, wpa, wpb, wo, [shard["w_gate_up"], shard["w_down"]])
    wdn = wdn.reshape(NDEV // 2, -1, D)
    nb, _, wb = w_in.shape
    fb = wgu.shape[2]
    nh = wdn.shape[0]
    dx1, dx1b, h2, act, dgu, dx2b, facc = _ffn_call(x1, tgt, gffn, gfin, wgu, wdn)
    assert nb == NDEV and 2 * nh == NDEV
    own_gu, part_gu = _dw_plain_call("dw_gate_up", me, dgu, h2, _blk3d(fb), _rows2d(D), 1, fb, D, s)
    own_dn, part_dn = _dw_plain_call("dw_down", me, act, dx2b, _blk3d(fb), _rows2d(D), 2, fb, D, s)
    (dprojb, dya, dpa, dpb, bvec, dws, dbs), (recv_dn,) = _mixer_bwd_pre_call(
        dx1b, proj, pa, pb, lng, lnb, wtr, wtrt, bias, wpa, wpb, wo, [part_dn])
    own_pa, part_pa = _dw_plain_call("dw_proj_a", me, ya, dpa, _rows2d(D), _rows2d(D), NDEV, D, D, s)
    own_pb, part_pb = _dw_plain_call("dw_proj_b", me, yb, dpb, _rows2d(D), _rows2d(D), NDEV, D, D, s)
    own_wo, part_wo = _dw_plain_call("dw_out", me, mg, dx1b, _rows2d(D), _rows2d(D), NDEV, D, D, s)
    (dproj, dx, svec, dwax), (recv_gu, recv_pa, recv_pb, recv_wo) = _mixer_bwd_seq_call(
        dprojb, dya, proj, h, x, dx1, gates, gmix, w_in, cw, lam, wax, [part_gu, part_pa, part_pb, part_wo])
    small = {
        "norm_mix_g": svec[8], "norm_ffn_g": facc[0], "norm_final_g": facc[1],
        "conv_b": svec[0], "rg_ba": svec[1], "rg_bx": svec[2], "rg_lambda": svec[3],
        "sgu_ln_g": bvec[0], "sgu_ln_b": bvec[1],
        "rg_wa": _head_unblocks(dwax[:, :, 0:QW]), "rg_wx": _head_unblocks(dwax[:, :, QW:2 * QW]),
        "sgu_ws": dws, "sgu_bs": dbs[:, ::GW].T,
    }
    packed = _pack_small(small, svec[4:8], facc[2])
    own_in, recv_in, gsum = _dw_exchange_call("dw_in", order_dw, h1, dproj, _rows2d(D), _cols2d(wb), D, wb, s, packed)
    dw = {
        "w_gate_up": (own_gu, recv_gu), "w_down": (own_dn, recv_dn), "w_proj_a": (own_pa, recv_pa),
        "w_proj_b": (own_pb, recv_pb), "w_out": (own_wo, recv_wo), "w_in": (own_in, recv_in),
    }
    return gsum, dx, dw


_BIG = ("w_in", "w_gate_up", "w_down", "w_proj_a", "w_proj_b", "w_out")
_VEC_ROWS = ("norm_mix_g", "norm_ffn_g", "norm_final_g", "conv_b", "rg_ba", "rg_bx", "rg_lambda",
             "sgu_ln_g", "sgu_ln_b", "sgu_bs")
_WEIGHTS = ("norm_mix_g", "w_in", "conv_w", "conv_b", "rg_wa", "rg_ba", "rg_wx", "rg_bx", "rg_lambda",
            "sgu_ln_g", "sgu_ln_b", "sgu_ws", "sgu_bs", "w_proj_a", "w_proj_b", "w_out", "norm_ffn_g",
            "w_gate_up", "w_down", "norm_final_g")


def _pack_small(t, conv_w, extra=None):
    extra = jnp.zeros((1, D), F32) if extra is None else extra.reshape(1, D)
    head = jnp.concatenate([t[k].reshape(1, D) for k in _VEC_ROWS] + [conv_w, extra, jnp.zeros((1, D), F32)], axis=0)
    return jnp.concatenate([head, t["rg_wa"].reshape(64, D), t["rg_wx"].reshape(64, D), t["sgu_ws"].reshape(128, D),
                            jnp.zeros((SMALL_ROWS - 272, D), F32)], axis=0)


def _unpack_small(a):
    out = {k: a[j] for j, k in enumerate(_VEC_ROWS)}
    out["conv_w"] = a[10:14]
    out["rg_wa"] = a[16:80].reshape(16, 64, 64)
    out["rg_wx"] = a[80:144].reshape(16, 64, 64)
    out["sgu_ws"] = a[144:272].reshape(NGRP, CHUNK, CHUNK)
    return out


def kernel(x, norm_mix_g, w_in, conv_w, conv_b, rg_wa, rg_ba, rg_wx, rg_bx, rg_lambda, sgu_ln_g, sgu_ln_b, sgu_ws, sgu_bs, w_proj_a, w_proj_b, w_out, norm_ffn_g, w_gate_up, w_down, norm_final_g, loss_target, m_norm_mix_g, m_w_in, m_conv_w, m_conv_b, m_rg_wa, m_rg_ba, m_rg_wx, m_rg_bx, m_rg_lambda, m_sgu_ln_g, m_sgu_ln_b, m_sgu_ws, m_sgu_bs, m_w_proj_a, m_w_proj_b, m_w_out, m_norm_ffn_g, m_w_gate_up, m_w_down, m_norm_final_g, v_norm_mix_g, v_w_in, v_conv_w, v_conv_b, v_rg_wa, v_rg_ba, v_rg_wx, v_rg_bx, v_rg_lambda, v_sgu_ln_g, v_sgu_ln_b, v_sgu_ws, v_sgu_bs, v_w_proj_a, v_w_proj_b, v_w_out, v_norm_ffn_g, v_w_gate_up, v_w_down, v_norm_final_g):
    args = dict(locals())
    w = {k: args[k] for k in _WEIGHTS}
    m = {k: args["m_" + k] for k in _WEIGHTS}
    v = {k: args["v_" + k] for k in _WEIGHTS}
    for d in (w, m, v):
        for k in _WEIGHTS:
            if k != "norm_final_g":
                d[k] = d[k][0]
    me = _lin(_me())
    me1 = me.reshape(1).astype(jnp.int32)

    order = jnp.bitwise_xor(me, jnp.array(_PASS_FLIPS, jnp.int32)).astype(jnp.int32)
    order_dw = jnp.bitwise_xor(me, jnp.array(_DW_FLIPS, jnp.int32)).astype(jnp.int32)

    gsum, dx, dw = _local_step(x[0], loss_target[0], w, me1, order, order_dw)

    peer = jnp.bitwise_xor(jnp.arange(NDEV, dtype=jnp.int32), me)
    sel_direct = jnp.where(peer == 0, 2, 1).astype(jnp.int32)
    sel_two_level = jnp.where(peer == 0, 2, jnp.where((peer == 1) | (peer % 2 == 0), 1, 0)).astype(jnp.int32)
    grads, delta, new_m, new_v = {}, {}, {}, {}
    for k in _BIG:
        own, recv = dw[k]
        flip = own.shape != w[k].shape
        wmv = [jnp.swapaxes(t, 0, 1) if flip else t for t in (w[k], m[k], v[k])]
        res = _reduce_adamw_call("adamw_" + k, sel_two_level if k == "w_in" else sel_direct, own, recv, *wmv)
        grads[k], delta[k], new_m[k], new_v[k] = (jnp.swapaxes(t, 0, 1) if flip else t for t in res)

    loss = (0.5 / D) * jnp.sum(gsum[14])
    zc = jnp.zeros((4, D), F32)
    d_s, m_s, v_s = _adamw_call("adamw_small", _pack_small(w, zc), gsum, _pack_small(m, zc), _pack_small(v, zc))
    gs, ds, ms, vs = _unpack_small(gsum), _unpack_small(d_s), _unpack_small(m_s), _unpack_small(v_s)
    g_cw = lax.dynamic_slice(gs["conv_w"], (0, me * 128), (4, 128))
    ds["conv_w"], ms["conv_w"], vs["conv_w"] = _adamw_call("adamw_conv_w", w["conv_w"], g_cw, m["conv_w"], v["conv_w"])
    gs["conv_w"] = g_cw
    for k in _WEIGHTS:
        if k not in _BIG:
            shp = w[k].shape
            grads[k], delta[k], new_m[k], new_v[k] = (t[k].reshape(shp) for t in (gs, ds, ms, vs))

    def lift(t, k):
        return t[k] if k == "norm_final_g" else t[k][None]

    outs = [loss, dx[None]]
    for t in (grads, delta, new_m, new_v):
        outs += [lift(t, k) for k in _WEIGHTS]
    return tuple(outs)
```

```python
import functools

import jax
import jax.numpy as jnp
from jax import lax
from jax.experimental import pallas as pl
from jax.experimental.pallas import tpu as pltpu

F32 = jnp.float32
BF = jnp.bfloat16

D = 1024
NDEV = 8
EPS = 1e-6
RG_C = 8.0
CHUNK = 128
NGRP = 8
GW = 128
NQ = 4
QW = 256
RC = 16
SMALL_ROWS = 320
SMALL_PER = SMALL_ROWS // NDEV

ADAM_LR = 0.001
ADAM_B1 = 0.9
ADAM_B2 = 0.999
ADAM_EPS = 1e-08
ADAM_WD = 0.01
ADAM_STEP = 10

VMEM_LIMIT = 60 * 1024 * 1024

MESH = pl.DeviceIdType.MESH


def _rows(n, fn, unroll=2, rc=RC):
    def body(i, c):
        fn(pl.multiple_of(i * rc, rc))
        return c
    lax.fori_loop(0, n // rc, body, 0, unroll=unroll)


def _fold(v):
    return jnp.sum(v.reshape(v.shape[0] // RC, RC, v.shape[1]), axis=0)


def _dot(a, b):
    return jnp.dot(a, b, preferred_element_type=F32)


def _dot_nt(a, b):
    return lax.dot_general(a, b, (((1,), (1,)), ((), ())), preferred_element_type=F32)


def _dot_tn(a, b):
    return lax.dot_general(a, b, (((0,), (0,)), ((), ())), preferred_element_type=F32)


_GC = 0.7978845608028654
_GK = 0.044715


def _gelu(x):
    t = jnp.tanh(_GC * (x + _GK * (x * x * x)))
    return x * (0.5 * (1.0 + t))


def _gelu_grad(x):
    x2 = x * x
    t = jnp.tanh(_GC * (x + _GK * (x2 * x)))
    cdf = 0.5 * (1.0 + t)
    dg = cdf + (0.5 * x) * (1.0 - t * t) * (_GC * (1.0 + (3.0 * _GK) * x2))
    return x * cdf, dg


def _sigmoid(x):
    return jax.nn.sigmoid(x)


def _log1p(e):
    u = 1.0 + e
    d = u - 1.0
    return jnp.where(d == 0.0, e, jnp.log(u) * (e / jnp.where(d == 0.0, 1.0, d)))


def _softplus(z):
    return jnp.maximum(z, 0.0) + _log1p(jnp.exp(-jnp.abs(z)))


def _neg_expm1(z):
    u = jnp.exp(z)
    lu = jnp.log(u)
    k = (1.0 - u) * (z / jnp.where(lu == 0.0, 1.0, lu))
    small = jnp.where(lu == 0.0, -z, k)
    return jnp.where(z > -0.5, small, 1.0 - u)


def _shift_back(prev8, cur, j):
    cat = jnp.concatenate([prev8, cur], axis=0)
    return pltpu.roll(cat, j, 0)[8:8 + cur.shape[0]]


def _shift_fwd(cur, next8, j):
    cat = jnp.concatenate([cur, next8], axis=0)
    n = cat.shape[0]
    return pltpu.roll(cat, n - j, 0)[0:cur.shape[0]]


def _const_spec(shape):
    nd = len(shape)
    return pl.BlockSpec(shape, lambda *_: (0,) * nd, pipeline_mode=pl.Buffered(1))


def _params(sem):
    return pltpu.CompilerParams(dimension_semantics=sem, vmem_limit_bytes=VMEM_LIMIT)


TM_PROJ = 1024
TM_MIX = 256
TM_FFN = 256
TM_DX = 512
TS_DW = 4096


_CHIPS = (4, 2, 6)
_PASS_FLIPS = (0, 1, 4, 2, 6, 5, 3, 7)


class _Gather:
    def __init__(self, ins, outs, send_sems, recv_sems, local_sems):
        self.ins, self.outs = ins, outs
        self.send_sems, self.recv_sems, self.local_sems = send_sems, recv_sems, local_sems
        self.me = _me()
        self.sibling = _flip(self.me, 1)

    def _copy(self, a, kind, block, to, src=None):
        dst = self.outs[a].at[_lin(block)]
        return pltpu.make_async_remote_copy(
            src_ref=dst if src is None else src, dst_ref=dst,
            send_sem=self.send_sems.at[a, kind], recv_sem=self.recv_sems.at[a, kind],
            device_id=to, device_id_type=MESH)

    def _local(self, a):
        return pltpu.make_async_copy(self.ins[a], self.outs[a].at[_lin(self.me)], self.local_sems.at[a])

    def start(self):
        for a in range(len(self.ins)):
            self._local(a).start()
            self._copy(a, 0, self.me, self.sibling, src=self.ins[a]).start()
            for j, f in enumerate(_CHIPS):
                self._copy(a, 1 + j, self.me, _flip(self.me, f), src=self.ins[a]).start()

    def forward(self):
        for j, f in enumerate(_CHIPS):
            for a in range(len(self.ins)):
                self._copy(a, 1 + j, _flip(self.me, f), self.me).wait_recv()
                self._copy(a, 4 + j, _flip(self.me, f), self.sibling).start()

    def finish(self):
        for a in range(len(self.ins)):
            self._copy(a, 0, self.sibling, self.me).wait_recv()
            for j, f in enumerate(_CHIPS):
                self._copy(a, 4 + j, _flip(self.me, f | 1), self.me).wait_recv()
            self._copy(a, 0, self.me, self.sibling, src=self.ins[a]).wait_send()
            for j, f in enumerate(_CHIPS):
                self._copy(a, 1 + j, self.me, _flip(self.me, f), src=self.ins[a]).wait_send()
                self._copy(a, 4 + j, _flip(self.me, f), self.sibling).wait_send()
            self._local(a).wait()


class _Exchange:
    def __init__(self, ins, outs, send_sems, recv_sems, local_sems):
        self.ins, self.outs = ins, outs
        self.send_sems, self.recv_sems, self.local_sems = send_sems, recv_sems, local_sems
        self.me = _me()

    def _copy(self, a, r, outgoing):
        peer = _flip(self.me, r)
        src, dst = (peer, self.me) if outgoing else (self.me, peer)
        return pltpu.make_async_remote_copy(
            src_ref=self.ins[a].at[_lin(src)], dst_ref=self.outs[a].at[_lin(dst)],
            send_sem=self.send_sems.at[a, r - 1], recv_sem=self.recv_sems.at[a, r - 1],
            device_id=peer, device_id_type=MESH)

    def _local(self, a):
        mi = _lin(self.me)
        return pltpu.make_async_copy(self.ins[a].at[mi], self.outs[a].at[mi], self.local_sems.at[a])

    def start(self):
        for a in range(len(self.ins)):
            self._local(a).start()
        for r in range(1, NDEV):
            for a in range(len(self.ins)):
                self._copy(a, r, True).start()

    def finish(self):
        for r in range(1, NDEV):
            for a in range(len(self.ins)):
                self._copy(a, r, False).wait_recv()
        for r in range(1, NDEV):
            for a in range(len(self.ins)):
                self._copy(a, r, True).wait_send()
        for a in range(len(self.ins)):
            self._local(a).wait()


def _gather_sems(n):
    return [pltpu.SemaphoreType.DMA((n, 7)), pltpu.SemaphoreType.DMA((n, 7)), pltpu.SemaphoreType.DMA((n,))]


def _proj_gather_call(x, gmix, w_shard, order, extras):
    s = x.shape[0]
    tm = min(TM_PROJ, s)
    nt = s // tm
    wb = w_shard.shape[1]
    n = len(extras)

    def body(order_ref, x_ref, g_ref, wsh_ref, *rest):
        ex_in = rest[:n]
        proj_ref, h1_hbm, wall_hbm = rest[n:n + 3]
        ex_out = rest[n + 3:2 * n + 3]
        wall, hc, ws_send, ws_recv, ex_send, ex_recv, ex_local, out_sems = rest[2 * n + 3:]
        k = pl.program_id(0)
        i = pl.program_id(1)
        me = _me()
        sibling = _flip(me, 1)
        gather = _Gather(ex_in, ex_out, ex_send, ex_recv, ex_local)

        def wcopy(kind, block, to):
            ref = wall.at[_lin(block)]
            return pltpu.make_async_remote_copy(
                src_ref=ref, dst_ref=ref, send_sem=ws_send.at[kind], recv_sem=ws_recv.at[kind],
                device_id=to, device_id_type=MESH)

        head = i == 0

        @pl.when(head & (k == 0))
        def _():
            wall[_lin(me)] = wsh_ref[...]
            wcopy(0, me, sibling).start()
            for j, f in enumerate(_CHIPS):
                wcopy(1 + j, me, _flip(me, f)).start()
            gather.start()

        @pl.when(head & (k == 1))
        def _():
            wcopy(0, sibling, me).wait_recv()

        for j, f in enumerate(_CHIPS):
            @pl.when(head & (k == 2 + j))
            def _(j=j, f=f):
                wcopy(1 + j, _flip(me, f), me).wait_recv()
                wcopy(4 + j, _flip(me, f), sibling).start()

            @pl.when(head & (k == 5 + j))
            def _(j=j, f=f):
                wcopy(4 + j, _flip(me, f | 1), me).wait_recv()

        pl.when(head & (k == 5))(gather.forward)

        base = pl.multiple_of(i * tm, tm)

        @pl.when(k == 0)
        def _():
            g = g_ref[...]

            def norm(r0):
                xx = x_ref[pl.ds(r0, RC), :]
                r = lax.rsqrt(jnp.mean(xx * xx, axis=-1, keepdims=True) + EPS)
                hc[pl.ds(base + r0, RC), :] = ((xx * r) * g).astype(BF)
            _rows(tm, norm, unroll=4)

        wk = wall.at[order_ref[k]]
        for r0 in range(0, tm, TM_MIX):
            rs = min(TM_MIX, tm - r0)
            proj_ref[r0:r0 + rs, :] = _dot(hc[pl.ds(base + r0, rs), :], wk[...])

        @pl.when((k == NDEV - 1) & (i == nt - 1))
        def _():
            out_w = pltpu.make_async_copy(wall, wall_hbm, out_sems.at[0])
            out_h = pltpu.make_async_copy(hc, h1_hbm, out_sems.at[1])
            out_w.start()
            out_h.start()
            wcopy(0, me, sibling).wait_send()
            for j, f in enumerate(_CHIPS):
                wcopy(1 + j, me, _flip(me, f)).wait_send()
                wcopy(4 + j, _flip(me, f), sibling).wait_send()
            gather.finish()
            out_w.wait()
            out_h.wait()

    any_spec = pl.BlockSpec(memory_space=pl.ANY)
    grid_spec = pltpu.PrefetchScalarGridSpec(
        num_scalar_prefetch=1, grid=(NDEV, nt),
        in_specs=[pl.BlockSpec((tm, D), lambda k, i, o: (jnp.where(k == 0, i, nt - 1), 0)),
                  pl.BlockSpec((1, D), lambda k, i, o: (0, 0)),
                  pl.BlockSpec(w_shard.shape, lambda k, i, o: (0, 0))] + [any_spec] * n,
        out_specs=[pl.BlockSpec((tm, wb), lambda k, i, o: (i, o[k])), any_spec, any_spec] + [any_spec] * n,
        scratch_shapes=[pltpu.VMEM((NDEV,) + w_shard.shape, BF), pltpu.VMEM((s, D), BF),
                        pltpu.SemaphoreType.DMA((7,)), pltpu.SemaphoreType.DMA((7,))] + _gather_sems(n)
        + [pltpu.SemaphoreType.DMA((2,))])
    outs = pl.pallas_call(
        body, name="proj_gather", grid_spec=grid_spec,
        out_shape=[jax.ShapeDtypeStruct((s, NDEV * wb), F32), jax.ShapeDtypeStruct((s, D), BF),
                   jax.ShapeDtypeStruct((NDEV,) + w_shard.shape, BF)]
        + [jax.ShapeDtypeStruct((NDEV,) + e.shape, e.dtype) for e in extras],
        compiler_params=_params(("arbitrary", "arbitrary")),
    )(order, x, gmix, w_shard, *extras)
    return outs[0], outs[1], outs[2], outs[3:]


def _conv_tile(rx, prev8, cw_ref, cb):
    xc = cb + cw_ref[3:4, :] * rx
    for j in (1, 2, 3):
        xc = xc + cw_ref[3 - j:4 - j, :] * _shift_back(prev8, rx, j)
    return xc


def _gate_tile(xcb, wax_ref, ba, bx, sp, q):
    cs = slice(q * QW, (q + 1) * QW)
    z = _dot(xcb[:, cs], wax_ref[q])
    r = _sigmoid(z[:, 0:QW] + ba[:, cs])
    ig = _sigmoid(z[:, QW:2 * QW] + bx[:, cs])
    return r, ig, (-RG_C * r) * sp[:, cs]


def _scan_fwd(a_s, b_s, out_ref, h0, n):
    rowi = lax.broadcasted_iota(jnp.int32, (8, D), 0)

    def block(t, h):
        rows = pl.ds(pl.multiple_of(t * 8, 8), 8)
        a = a_s[rows, :]
        b = b_s[rows, :]
        for d in (1, 2, 4):
            m = rowi >= d
            b = jnp.where(m, a * pltpu.roll(b, d, 0) + b, b)
            a = jnp.where(m, a * pltpu.roll(a, d, 0), a)
        hb = b + a * h
        out_ref[rows, :] = hb
        return hb[7:8, :]
    return lax.fori_loop(0, n // 8, block, h0, unroll=4)


def _scan_bwd(a_s, lm_s, c0, n):
    rowi = lax.broadcasted_iota(jnp.int32, (8, D), 0)
    nblk = n // 8

    def block(k, cin):
        rows = pl.ds(pl.multiple_of((nblk - 1 - k) * 8, 8), 8)
        a = a_s[rows, :]
        dh = lm_s[rows, :]
        b = a * dh
        for d in (1, 2, 4):
            m = rowi < 8 - d
            b = jnp.where(m, a * pltpu.roll(b, 8 - d, 0) + b, b)
            a = jnp.where(m, a * pltpu.roll(a, 8 - d, 0), a)
        mu = b + a * cin
        lm_s[rows, :] = dh + jnp.where(rowi < 7, pltpu.roll(mu, 7, 0), cin)
        return mu[0:1, :]
    return lax.fori_loop(0, nblk, block, c0, unroll=4)


def _mixer_fwd_call(proj, x, cw, cb, ba, bx, lam, lng, lnb, wax, wtr, bias, wpa, wpb, wo, shards):
    s = x.shape[0]
    tm = min(TM_MIX, s)
    nt = s // tm
    pw = proj.shape[1]
    n = len(shards)

    def body(proj_ref, x_ref, cw_ref, cb_ref, ba_ref, bx_ref, lam_ref, lng_ref, lnb_ref, wax_ref, wtr_ref,
             bias_ref, wpa_ref, wpb_ref, wo_ref, *rest):
        sh_in = rest[:n]
        h_ref, pa_ref, pb_ref, x1_ref, xc_ref, r_ref, ig_ref, a_s, m_ref, ya_ref, yb_ref, mg_ref = rest[n:n + 12]
        sh_out = rest[n + 12:2 * n + 12]
        prev_s, b_s, hc_s, vn_s, mx_s, g_send, g_recv, g_local = rest[2 * n + 12:]
        i = pl.program_id(0)
        gather = _Gather(sh_in, sh_out, g_send, g_recv, g_local)

        @pl.when(i == 0)
        def _():
            gather.start()
            prev_s[...] = jnp.zeros((8, D), F32)
            hc_s[...] = jnp.zeros((8, D), F32)

        pl.when(i == nt // 2)(gather.forward)

        rx = proj_ref[:, 0:D]
        xc = _conv_tile(rx, prev_s[...], cw_ref, cb_ref[...])
        prev_s[...] = rx[tm - 8:tm, :]
        xc_ref[...] = xc.astype(BF)
        xcb = xc.astype(BF)
        sp = _softplus(-lam_ref[...])
        ba = ba_ref[...]
        bx = bx_ref[...]
        for q in range(NQ):
            cs = slice(q * QW, (q + 1) * QW)
            r, ig, la = _gate_tile(xcb, wax_ref, ba, bx, sp, q)
            mq = jnp.sqrt(_neg_expm1(2.0 * la))
            r_ref[:, cs] = r.astype(BF)
            ig_ref[:, cs] = ig.astype(BF)
            m_ref[:, cs] = mq.astype(BF)
            a_s[:, cs] = jnp.exp(la)
            b_s[:, cs] = mq * (ig * xc[:, cs])

        gv = _gelu(proj_ref[:, 3 * D:4 * D])
        dv = gv - jnp.mean(gv, axis=-1, keepdims=True)
        var = jnp.mean(dv * dv, axis=-1, keepdims=True)
        vn_s[...] = ((dv * lax.rsqrt(var + EPS)) * lng_ref[...] + lnb_ref[...]).astype(BF)
        nc = tm // CHUNK
        for c in range(nc):
            rs = slice(c * CHUNK, (c + 1) * CHUNK)
            for g in range(NGRP):
                cs = slice(g * GW, (g + 1) * GW)
                mx_s[rs, cs] = _dot(wtr_ref[g], vn_s[rs, cs])
        mixed = mx_s[...] + jnp.concatenate([bias_ref[...]] * nc, axis=0)
        yb = (_gelu(proj_ref[:, 2 * D:3 * D]) * mixed).astype(BF)
        yb_ref[...] = yb
        pb = _dot(yb, wpb_ref[...])
        pb_ref[...] = pb.astype(BF)
        mx_s[...] = pb

        hc_s[0:1, :] = _scan_fwd(a_s, b_s, h_ref, hc_s[0:1, :], tm)

        ya = (_gelu(proj_ref[:, D:2 * D]) * h_ref[...]).astype(BF)
        ya_ref[...] = ya
        pa = _dot(ya, wpa_ref[...])
        pa_ref[...] = pa.astype(BF)
        mg = (_sigmoid(proj_ref[:, 4 * D:5 * D]) * pa + _sigmoid(proj_ref[:, 5 * D:6 * D]) * mx_s[...]).astype(BF)
        mg_ref[...] = mg
        x1_ref[...] = x_ref[...] + _dot(mg, wo_ref[...])

        pl.when(i == nt - 1)(gather.finish)

    tile = lambda w: pl.BlockSpec((tm, w), lambda i: (i, 0))
    vec = _const_spec((1, D))
    any_spec = pl.BlockSpec(memory_space=pl.ANY)
    outs = pl.pallas_call(
        body, name="mixer_fwd", grid=(nt,),
        in_specs=[tile(pw), tile(D), _const_spec((4, D)), vec, vec, vec, vec, vec, vec,
                  _const_spec(wax.shape), _const_spec(wtr.shape), _const_spec(bias.shape),
                  _const_spec((D, D)), _const_spec((D, D)), _const_spec((D, D))] + [any_spec] * n,
        out_specs=[tile(D)] * 12 + [any_spec] * n,
        out_shape=[jax.ShapeDtypeStruct((s, D), dt) for dt in (F32, BF, BF, F32, BF, BF, BF, F32, BF, BF, BF, BF)]
        + [jax.ShapeDtypeStruct((NDEV,) + e.shape, e.dtype) for e in shards],
        scratch_shapes=[pltpu.VMEM((8, D), F32), pltpu.VMEM((tm, D), F32),
                        pltpu.VMEM((8, D), F32), pltpu.VMEM((tm, D), BF), pltpu.VMEM((tm, D), F32)]
        + _gather_sems(n),
        compiler_params=_params(("arbitrary",)),
    )(proj, x, cw, cb, ba, bx, lam, lng, lnb, wax, wtr, bias, wpa, wpb, wo, *shards)
    return outs[:12], outs[12:]


def _ffn_call(x1, tgt, gffn, gfin, wgu, wdn):
    s = x1.shape[0]
    tm = min(TM_FFN, s)
    nt = s // tm
    nh = wdn.shape[0]
    fb = wgu.shape[2]

    def body(x1_ref, tgt_ref, gffn_ref, gfin_ref, wgu_ref, wdn_ref,
             dx1_ref, dx1b_ref, h2_ref, act_ref, dgu_ref, dx2b_ref, acc_ref,
             g_s, u_s, dx2_s, accs):
        i = pl.program_id(0)

        @pl.when(i == 0)
        def _():
            accs[...] = jnp.zeros(accs.shape, F32)

        gffn = gffn_ref[...]
        gfin = gfin_ref[...]

        x1 = x1_ref[...]
        r2 = lax.rsqrt(jnp.mean(x1 * x1, axis=-1, keepdims=True) + EPS)
        xh2 = x1 * r2
        h2 = (xh2 * gffn).astype(BF)
        h2_ref[...] = h2

        for k in range(nh):
            g = _dot(h2, wgu_ref[k])
            u = _dot(h2, wgu_ref[k + nh])
            g_s[k] = g
            u_s[k] = u
            act_ref[k] = ((g * _sigmoid(g)) * u).astype(BF)

        x2 = x1
        for k in range(nh):
            x2 = x2 + _dot(act_ref[k], wdn_ref[k])

        r3 = lax.rsqrt(jnp.mean(x2 * x2, axis=-1, keepdims=True) + EPS)
        xh = x2 * r3
        err = xh * gfin - tgt_ref[...]
        accs[2] += _fold(err * err)
        dy = err * (1.0 / D)
        accs[1] += _fold(dy * xh)
        dxh = dy * gfin
        dx2 = r3 * (dxh - xh * jnp.mean(dxh * xh, axis=-1, keepdims=True))
        dx2_s[...] = dx2
        dx2b = dx2.astype(BF)
        dx2b_ref[...] = dx2b

        for k in range(nh):
            da = _dot_nt(dx2b, wdn_ref[k])
            g = g_s[k]
            sg = _sigmoid(g)
            dgu_ref[k] = ((da * u_s[k]) * (sg * (1.0 + g * (1.0 - sg)))).astype(BF)
            dgu_ref[k + nh] = (da * (g * sg)).astype(BF)

        dh2 = _dot_nt(dgu_ref[0], wgu_ref[0])
        for k in range(1, 2 * nh):
            dh2 = dh2 + _dot_nt(dgu_ref[k], wgu_ref[k])

        accs[0] += _fold(dh2 * xh2)
        dxh = dh2 * gffn
        dx1 = dx2_s[...] + r2 * (dxh - xh2 * jnp.mean(dxh * xh2, axis=-1, keepdims=True))
        dx1_ref[...] = dx1
        dx1b_ref[...] = dx1.astype(BF)

        @pl.when(i == nt - 1)
        def _():
            acc_ref[...] = jnp.zeros((8, D), F32)
            for j in range(3):
                acc_ref[j:j + 1, :] = jnp.sum(accs[j], axis=0, keepdims=True)

    tile = lambda w: pl.BlockSpec((tm, w), lambda i: (i, 0))
    vec = _const_spec((1, D))
    return pl.pallas_call(
        body, name="ffn", grid=(nt,),
        in_specs=[tile(D), tile(D), vec, vec, _const_spec(wgu.shape), _const_spec(wdn.shape)],
        out_specs=[tile(D), tile(D), tile(D),
                   pl.BlockSpec((nh, tm, fb), lambda i: (0, i, 0)),
                   pl.BlockSpec((2 * nh, tm, fb), lambda i: (0, i, 0)),
                   tile(D), pl.BlockSpec((8, D), lambda i: (0, 0))],
        out_shape=[jax.ShapeDtypeStruct((s, D), F32), jax.ShapeDtypeStruct((s, D), BF),
                   jax.ShapeDtypeStruct((s, D), BF), jax.ShapeDtypeStruct((nh, s, fb), BF),
                   jax.ShapeDtypeStruct((2 * nh, s, fb), BF), jax.ShapeDtypeStruct((s, D), BF),
                   jax.ShapeDtypeStruct((8, D), F32)],
        scratch_shapes=[pltpu.VMEM((nh, tm, fb), F32), pltpu.VMEM((nh, tm, fb), F32), pltpu.VMEM((tm, D), F32),
                        pltpu.VMEM((3, RC, D), F32)],
        compiler_params=_params(("arbitrary",)),
    )(x1, tgt, gffn, gfin, wgu, wdn)


def _mixer_bwd_pre_call(dx1b, proj, pa, pb, lng, lnb, wtr, wtrt, bias, wpa, wpb, wo, parts):
    s = dx1b.shape[0]
    tm = min(TM_MIX, s)
    nt = s // tm
    pw = proj.shape[1]
    n = len(parts)

    def body(dx1b_ref, uv_ref, gg_ref, pa_ref, pb_ref, lng_ref, lnb_ref, wtr_ref, wtrt_ref, bias_ref,
             wpa_ref, wpb_ref, wo_ref, *rest):
        ex_in = rest[:n]
        dproj_ref, dya_ref, dpa_ref, dpb_ref, vec_ref, dws_ref, dbs_ref = rest[n:n + 7]
        ex_out = rest[n + 7:2 * n + 7]
        vn_s, mx_s, dmx_s, dvn_s, accs, dbs_s, e_send, e_recv, e_local = rest[2 * n + 7:]
        i = pl.program_id(0)
        exchange = _Exchange(ex_in, ex_out, e_send, e_recv, e_local)

        @pl.when(i == 0)
        def _():
            exchange.start()
            accs[...] = jnp.zeros(accs.shape, F32)
            dbs_s[...] = jnp.zeros(dbs_s.shape, F32)
            dws_ref[...] = jnp.zeros(dws_ref.shape, F32)

        dm = _dot_nt(dx1b_ref[...], wo_ref[...])
        sa = _sigmoid(gg_ref[:, 0:D])
        sb = _sigmoid(gg_ref[:, D:2 * D])
        dpa = dm * sa
        dpb = dm * sb
        dpab = dpa.astype(BF)
        dpbb = dpb.astype(BF)
        dpa_ref[...] = dpab
        dpb_ref[...] = dpbb
        dproj_ref[:, 2 * D:3 * D] = ((dpa * pa_ref[...].astype(F32)) * (1.0 - sa)).astype(BF)
        dproj_ref[:, 3 * D:4 * D] = ((dpb * pb_ref[...].astype(F32)) * (1.0 - sb)).astype(BF)

        dya_ref[...] = _dot_nt(dpab, wpa_ref[...])
        dyb = _dot_nt(dpbb, wpb_ref[...])

        lng = lng_ref[...]
        gv, dgelu_v = _gelu_grad(uv_ref[:, D:2 * D])
        dv = gv - jnp.mean(gv, axis=-1, keepdims=True)
        rstd = lax.rsqrt(jnp.mean(dv * dv, axis=-1, keepdims=True) + EPS)
        xh = dv * rstd
        vn_s[...] = (xh * lng + lnb_ref[...]).astype(BF)

        nc = tm // CHUNK
        for c in range(nc):
            rs = slice(c * CHUNK, (c + 1) * CHUNK)
            for g in range(NGRP):
                cs = slice(g * GW, (g + 1) * GW)
                mx_s[rs, cs] = _dot(wtr_ref[g], vn_s[rs, cs])

        gu, dgelu_u = _gelu_grad(uv_ref[:, 0:D])
        mixed = mx_s[...] + jnp.concatenate([bias_ref[...]] * nc, axis=0)
        dproj_ref[:, 0:D] = ((dyb * mixed) * dgelu_u).astype(BF)
        dmx = dyb * gu
        dmx_s[...] = dmx.astype(BF)
        dbs_s[...] += jnp.sum(dmx.reshape(nc, CHUNK, D), axis=0)

        for c in range(nc):
            rs = slice(c * CHUNK, (c + 1) * CHUNK)
            for g in range(NGRP):
                cs = slice(g * GW, (g + 1) * GW)
                dvn_s[rs, cs] = _dot(wtrt_ref[g], dmx_s[rs, cs])
                dws_ref[g] += _dot_nt(dmx_s[rs, cs], vn_s[rs, cs])

        dvn = dvn_s[...]
        accs[0] += _fold(dvn * xh)
        accs[1] += _fold(dvn)
        dxh = dvn * lng
        m1 = jnp.mean(dxh, axis=-1, keepdims=True)
        m2 = jnp.mean(dxh * xh, axis=-1, keepdims=True)
        dproj_ref[:, D:2 * D] = ((rstd * (dxh - m1 - xh * m2)) * dgelu_v).astype(BF)

        @pl.when(i == nt - 1)
        def _():
            vec_ref[...] = jnp.zeros((8, D), F32)
            for j in range(2):
                vec_ref[j:j + 1, :] = jnp.sum(accs[j], axis=0, keepdims=True)
            row = lax.broadcasted_iota(jnp.int32, (CHUNK, CHUNK), 0)
            col = lax.broadcasted_iota(jnp.int32, (CHUNK, CHUNK), 1)
            for g in range(NGRP):
                dws_ref[g] = jnp.where(row >= col, dws_ref[g], 0.0)
                gs = jnp.sum(dbs_s[:, g * GW:(g + 1) * GW], axis=1, keepdims=True)
                dbs_ref[:, g * GW:(g + 1) * GW] = jnp.broadcast_to(gs, (CHUNK, GW))
            exchange.finish()

    tile = lambda w: pl.BlockSpec((tm, w), lambda i: (i, 0))
    vec = _const_spec((1, D))
    any_spec = pl.BlockSpec(memory_space=pl.ANY)
    outs = pl.pallas_call(
        body, name="mixer_bwd_pre", grid=(nt,),
        in_specs=[tile(D), pl.BlockSpec((tm, 2 * D), lambda i: (i, 1)), pl.BlockSpec((tm, 2 * D), lambda i: (i, 2)),
                  tile(D), tile(D), vec, vec, _const_spec(wtr.shape), _const_spec(wtrt.shape),
                  _const_spec(bias.shape), _const_spec((D, D)), _const_spec((D, D)), _const_spec((D, D))]
        + [any_spec] * n,
        out_specs=[tile(4 * D), tile(D), tile(D), tile(D), pl.BlockSpec((8, D), lambda i: (0, 0)),
                   pl.BlockSpec((NGRP, CHUNK, CHUNK), lambda i: (0, 0, 0)),
                   pl.BlockSpec((CHUNK, D), lambda i: (0, 0))] + [any_spec] * n,
        out_shape=[jax.ShapeDtypeStruct((s, 4 * D), BF), jax.ShapeDtypeStruct((s, D), F32),
                   jax.ShapeDtypeStruct((s, D), BF), jax.ShapeDtypeStruct((s, D), BF),
                   jax.ShapeDtypeStruct((8, D), F32), jax.ShapeDtypeStruct((NGRP, CHUNK, CHUNK), F32),
                   jax.ShapeDtypeStruct((CHUNK, D), F32)]
        + [jax.ShapeDtypeStruct(p.shape, p.dtype) for p in parts],
        scratch_shapes=[pltpu.VMEM((tm, D), BF), pltpu.VMEM((tm, D), F32), pltpu.VMEM((tm, D), BF),
                        pltpu.VMEM((tm, D), F32), pltpu.VMEM((2, RC, D), F32), pltpu.VMEM((CHUNK, D), F32)]
        + _gather_sems(n),
        compiler_params=_params(("arbitrary",)),
    )(dx1b, proj, proj, pa, pb, lng, lnb, wtr, wtrt, bias, wpa, wpb, wo, *parts)
    return outs[:7], outs[7:]


def _mixer_bwd_seq_call(dprojb, dya, proj, h, x, dx1, gates, gmix, w_in_all, cw, lam, wax, parts):
    s = dya.shape[0]
    tm = min(TM_MIX, s)
    nt = s // tm
    tb = tm // 8
    n = len(parts)
    nb, _, wb = w_in_all.shape

    def body(dpb_ref, dya_ref, xg_ref, xh8_ref, h_ref, hh8_ref, x_ref, dx1_ref, xc_s, r_s, ig_s, a_s, m_s,
             gmix_ref, win_ref, cw_ref, lam_ref, wax_ref, *rest):
        ex_in = rest[:n]
        dpa_ref, dx_ref, vec_ref, dwax_ref = rest[n:n + 4]
        ex_out = rest[n + 4:2 * n + 4]
        lm_s, dh_s, dxc_s, c_s, accs, e_send, e_recv, e_local = rest[2 * n + 4:]
        i = pl.program_id(0)
        exchange = _Exchange(ex_in, ex_out, e_send, e_recv, e_local)

        @pl.when(i == 0)
        def _():
            exchange.start()
            accs[...] = jnp.zeros(accs.shape, F32)
            dwax_ref[...] = jnp.zeros(dwax_ref.shape, F32)
            c_s[...] = jnp.zeros((8, D), F32)
            dxc_s[tm:tm + 8, :] = jnp.zeros((8, D), F32)

        first_tile = i == nt - 1
        prev8 = jnp.where(first_tile, 0.0, xh8_ref[...])
        hprev8 = jnp.where(first_tile, 0.0, hh8_ref[...])
        lamv = lam_ref[...]
        sp = _softplus(-lamv)
        hv = h_ref[...]
        g, dg = _gelu_grad(xg_ref[:, D:2 * D])
        dya = dya_ref[...]
        lm_s[...] = dya * g
        drg = ((dya * hv) * dg).astype(BF)
        dpa_ref[:, D:2 * D] = drg
        dpa_ref[:, 2 * D:6 * D] = dpb_ref[...]

        def dpb_block(k):
            return _dot_nt(dpb_ref[:, k * wb - 2 * D:(k + 1) * wb - 2 * D], win_ref[k])
        dh = (_dot_nt(drg[:, 0:2 * wb - D], win_ref[1, :, D - wb:wb])
              + _dot_nt(drg[:, 2 * wb - D:D], win_ref[2, :, 0:2 * D - 2 * wb])
              + _dot_nt(dpb_ref[:, 0:3 * wb - 2 * D], win_ref[2, :, 2 * D - 2 * wb:wb]))
        for k in range(3, 6):
            dh = dh + dpb_block(k)
        dh_s[...] = dh

        c_s[0:1, :] = _scan_bwd(a_s, lm_s, c_s[0:1, :], tm)

        hprev = _shift_back(hprev8, h_ref[...], 1)
        for q in range(NQ):
            cs = slice(q * QW, (q + 1) * QW)
            r = r_s[:, cs].astype(F32)
            ig = ig_s[:, cs].astype(F32)
            a = a_s[:, cs]
            m = m_s[:, cs].astype(F32)
            lm = lm_s[:, cs]
            xq = xc_s[:, cs].astype(F32)
            dixc = lm * m
            dla = (lm * hprev[:, cs]) * a - ((lm * (ig * xq)) * (a * a)) / m
            accs[3, :, cs] += _fold(dla * r)
            dza = (dla * (-RG_C * sp[:, cs])) * (r * (1.0 - r))
            dzx = (dixc * xq) * (ig * (1.0 - ig))
            accs[1, :, cs] += _fold(dza)
            accs[2, :, cs] += _fold(dzx)
            dz = jnp.concatenate([dza, dzx], axis=1).astype(BF)
            dxc_s[0:tm, cs] = dixc * ig + _dot_nt(dz, wax_ref[q])
            dwax_ref[q] += _dot_tn(xc_s[:, cs], dz)

        cur = dxc_s[0:tm, :]
        nx = dxc_s[tm:tm + 8, :]
        drx = cw_ref[3:4, :] * cur
        for j in (1, 2, 3):
            drx = drx + cw_ref[3 - j:4 - j, :] * _shift_fwd(cur, nx, j)
        accs[0] += _fold(cur)
        rx = xg_ref[:, 0:D]
        accs[7] += _fold(cur * rx)
        for j in (1, 2, 3):
            accs[7 - j] += _fold(cur * _shift_back(prev8, rx, j))
        dxc_s[tm:tm + 8, :] = cur[0:8, :]

        drxb = drx.astype(BF)
        dpa_ref[:, 0:D] = drxb
        dh = dh_s[...] + _dot_nt(drxb[:, 0:wb], win_ref[0]) + _dot_nt(drxb[:, wb:D], win_ref[1, :, 0:D - wb])
        for k in range(6, nb):
            dh = dh + dpb_block(k)
        xx = x_ref[...]
        rn = lax.rsqrt(jnp.mean(xx * xx, axis=-1, keepdims=True) + EPS)
        xhn = xx * rn
        accs[8] += _fold(dh * xhn)
        dxh = dh * gmix_ref[...]
        dx_ref[...] = dx1_ref[...] + rn * (dxh - xhn * jnp.mean(dxh * xhn, axis=-1, keepdims=True))

        @pl.when(i == nt - 1)
        def _():
            vec_ref[...] = jnp.zeros((16, D), F32)
            for j in range(9):
                vec_ref[j:j + 1, :] = jnp.sum(accs[j], axis=0, keepdims=True)
            vec_ref[3:4, :] = vec_ref[3:4, :] * (RG_C * _sigmoid(-lamv))
            exchange.finish()

    rev = lambda w: pl.BlockSpec((tm, w), lambda i: (nt - 1 - i, 0))
    halo = pl.BlockSpec((8, D), lambda i: (jnp.maximum((nt - 1 - i) * tb - 1, 0), 0))
    vec = _const_spec((1, D))
    any_spec = pl.BlockSpec(memory_space=pl.ANY)
    outs = pl.pallas_call(
        body, name="mixer_bwd_seq", grid=(nt,),
        in_specs=[rev(4 * D), rev(D), rev(2 * D), halo, rev(D), halo] + [rev(D)] * 7
        + [vec, _const_spec(w_in_all.shape), _const_spec((4, D)), vec, _const_spec(wax.shape)] + [any_spec] * n,
        out_specs=[rev(6 * D), rev(D), pl.BlockSpec((16, D), lambda i: (0, 0)),
                   pl.BlockSpec((NQ, QW, 2 * QW), lambda i: (0, 0, 0))] + [any_spec] * n,
        out_shape=[jax.ShapeDtypeStruct((s, 6 * D), BF), jax.ShapeDtypeStruct((s, D), F32),
                   jax.ShapeDtypeStruct((16, D), F32), jax.ShapeDtypeStruct((NQ, QW, 2 * QW), F32)]
        + [jax.ShapeDtypeStruct(p.shape, p.dtype) for p in parts],
        scratch_shapes=[pltpu.VMEM((tm, D), F32), pltpu.VMEM((tm, D), F32),
                        pltpu.VMEM((tm + 8, D), F32), pltpu.VMEM((8, D), F32), pltpu.VMEM((9, RC, D), F32)]
        + _gather_sems(n),
        compiler_params=_params(("arbitrary",)),
    )(dprojb, dya, proj, proj, h, h, x, dx1, *gates, gmix, w_in_all, cw, lam, wax, *parts)
    return outs[:4], outs[4:]


def _device_of(d):
    return (d // 4, lax.rem(d // 2, 2), lax.rem(d, 2))


_DW_PLAN = (("C", 1), ("C", 2), ("A", 1), ("C", 3), ("A", 2), ("S", 0), ("A", 3), ("O", 0))
_DW_FLIPS = tuple({"C": 2 * j + 1, "A": 2 * j, "S": 1, "O": 0}[kind] for kind, j in _DW_PLAN)


def _dw_exchange_call(name, order, a, b, a_spec, b_spec, k1, n1, s, small):
    ts = min(TS_DW, s)
    ns = s // ts
    nstep = len(_DW_PLAN)

    def slab(i, order_ref):
        return order_ref[i]

    def body(order_ref, a_ref, b_ref, g_ref, own_ref, recv_ref, gsum_ref, acc, sbuf, stage, send_sems, recv_sems,
             st_send, st_recv, local_sem, rbuf, gacc, send1, recv1, send2, recv2):
        i = pl.program_id(0)
        j = pl.program_id(1)
        me = _me()
        mi = _lin(me)
        sibling = _flip(me, 1)
        allreduce = _SmallAllReduce(g_ref, gacc, rbuf, send1, recv1, send2, recv2)
        pl.when((i == 0) & (j == 0))(allreduce.scatter)
        pl.when((i == nstep // 2) & (j == 0))(allreduce.reduce)
        p = _dot_tn(a_ref[...], b_ref[...])

        @pl.when(j == 0)
        def _():
            acc[...] = p

        @pl.when(j > 0)
        def _():
            acc[...] += p

        def send(step):
            kind, jj = _DW_PLAN[step]
            src = sbuf.at[step % 2]
            if kind == "C":
                return pltpu.make_async_remote_copy(
                    src_ref=src, dst_ref=stage.at[jj - 1], send_sem=st_send.at[jj - 1], recv_sem=st_recv.at[jj - 1],
                    device_id=sibling, device_id_type=MESH)
            to = sibling if kind == "S" else _flip(me, 2 * jj)
            return pltpu.make_async_remote_copy(
                src_ref=src, dst_ref=recv_ref.at[mi], send_sem=send_sems.at[jj], recv_sem=recv_sems.at[jj],
                device_id=to, device_id_type=MESH)

        def arrival(jj):
            frm = sibling if jj == 0 else _flip(me, 2 * jj)
            return pltpu.make_async_remote_copy(
                src_ref=sbuf.at[0], dst_ref=recv_ref.at[_lin(frm)], send_sem=send_sems.at[jj],
                recv_sem=recv_sems.at[jj], device_id=frm, device_id_type=MESH)

        def staged(jj):
            return pltpu.make_async_remote_copy(
                src_ref=sbuf.at[0], dst_ref=stage.at[jj - 1], send_sem=st_send.at[jj - 1], recv_sem=st_recv.at[jj - 1],
                device_id=sibling, device_id_type=MESH)

        for step, (kind, jj) in enumerate(_DW_PLAN):
            @pl.when((i == step) & (j == ns - 1))
            def _(step=step, kind=kind, jj=jj):
                if step >= 2:
                    send(step - 2).wait_send()
                if kind == "A":
                    staged(jj).wait_recv()
                    sbuf[step % 2] = (acc[...] + stage[jj - 1].astype(F32)).astype(BF)
                else:
                    sbuf[step % 2] = acc[...].astype(BF)
                if kind != "O":
                    send(step).start()
                else:
                    own_ref[...] = acc[...]
                    mine = pltpu.make_async_copy(sbuf.at[step % 2], recv_ref.at[mi], local_sem)
                    mine.start()
                    send(step - 1).wait_send()
                    for q in range(4):
                        arrival(q).wait_recv()
                    mine.wait()
                    allreduce.finish()
                    gsum_ref[...] = gacc[...]

    vmem = pl.BlockSpec(memory_space=pltpu.VMEM)
    grid_spec = pltpu.PrefetchScalarGridSpec(
        num_scalar_prefetch=1, grid=(nstep, ns),
        in_specs=[a_spec(ts, slab), b_spec(ts, slab), vmem],
        out_specs=[pl.BlockSpec((k1, n1), lambda i, j, o: (0, 0)), pl.BlockSpec(memory_space=pl.ANY), vmem],
        scratch_shapes=[pltpu.VMEM((k1, n1), F32), pltpu.VMEM((2, k1, n1), BF), pltpu.VMEM((3, k1, n1), BF),
                        pltpu.SemaphoreType.DMA((4,)), pltpu.SemaphoreType.DMA((4,)),
                        pltpu.SemaphoreType.DMA((3,)), pltpu.SemaphoreType.DMA((3,)),
                        pltpu.SemaphoreType.DMA(()), pltpu.VMEM((NDEV, SMALL_PER, D), F32),
                        pltpu.VMEM((SMALL_ROWS, D), F32)]
        + [pltpu.SemaphoreType.DMA((NDEV - 1,))] * 4)
    return pl.pallas_call(
        body, name=name, grid_spec=grid_spec,
        out_shape=[jax.ShapeDtypeStruct((k1, n1), F32), jax.ShapeDtypeStruct((NDEV, k1, n1), BF),
                   jax.ShapeDtypeStruct((SMALL_ROWS, D), F32)],
        compiler_params=_params(("arbitrary", "arbitrary")),
    )(order, a, b, small)


def _dw_plain_call(name, me, a, b, a_spec, b_spec, split, k1, n1, s):
    nb = NDEV // split
    r = k1 // split
    ts = min(TS_DW, s)
    ns = s // ts

    def slab(i, me_ref):
        return i

    def body(me_ref, a_ref, b_ref, own_ref, part_ref, acc):
        i = pl.program_id(0)
        j = pl.program_id(1)
        p = _dot_tn(a_ref[...], b_ref[...])

        @pl.when(j == 0)
        def _():
            acc[...] = p

        @pl.when(j > 0)
        def _():
            acc[...] += p

        @pl.when(j == ns - 1)
        def _():
            part_ref[...] = acc[...].astype(BF)

            @pl.when(i == me_ref[0] // split)
            def _():
                off = pl.multiple_of(lax.rem(me_ref[0], split) * r, RC)
                own_ref[...] = acc[pl.ds(off, r), :]

    grid_spec = pltpu.PrefetchScalarGridSpec(
        num_scalar_prefetch=1, grid=(nb, ns),
        in_specs=[a_spec(ts, slab), b_spec(ts, slab)],
        out_specs=[pl.BlockSpec((r, n1), lambda i, j, me_ref: (0, 0)),
                   pl.BlockSpec((None, k1, n1), lambda i, j, me_ref: (i, 0, 0))],
        scratch_shapes=[pltpu.VMEM((k1, n1), F32)])
    own, part = pl.pallas_call(
        body, name=name, grid_spec=grid_spec,
        out_shape=[jax.ShapeDtypeStruct((r, n1), F32), jax.ShapeDtypeStruct((nb, k1, n1), BF)],
        compiler_params=_params(("arbitrary", "arbitrary")),
    )(me, a, b)
    return own, part.reshape(NDEV, r, n1)


def _rows2d(w):
    return lambda ts, slab: pl.BlockSpec((ts, w), lambda i, j, me_ref: (j, 0))


def _cols2d(w):
    return lambda ts, slab: pl.BlockSpec((ts, w), lambda i, j, me_ref: (j, slab(i, me_ref)))


def _blk3d(w):
    return lambda ts, slab: pl.BlockSpec((None, ts, w), lambda i, j, me_ref: (slab(i, me_ref), j, 0))


_BC1 = 1.0 - ADAM_B1 ** ADAM_STEP
_BC2 = 1.0 - ADAM_B2 ** ADAM_STEP


def _adamw_math(w, g, m, v):
    m = ADAM_B1 * m + (1.0 - ADAM_B1) * g
    v = ADAM_B2 * v + (1.0 - ADAM_B2) * (g * g)
    m_hat = m / _BC1
    v_hat = v / _BC2
    delta = -ADAM_LR * (m_hat / (jnp.sqrt(v_hat) + ADAM_EPS) + ADAM_WD * w)
    return delta, m, v


def _row_tile(r):
    for t in (256, 176, 128, 64, 32, 16, 8):
        if r % t == 0:
            return t
    return r


def _reduce_adamw_call(name, sel, own, recv, w, m, v):
    r, c = own.shape
    tr = _row_tile(r)

    def body(sel_ref, own_ref, recv_ref, w_ref, m_ref, v_ref, g_ref, d_ref, nm_ref, nv_ref):
        g = jnp.zeros((tr, c), F32)
        for sdev in range(NDEV):
            part = jnp.where(sel_ref[sdev] == 1, recv_ref[sdev].astype(F32), 0.0)
            g = g + jnp.where(sel_ref[sdev] == 2, own_ref[...], part)
        g_ref[...] = g
        d_ref[...], nm_ref[...], nv_ref[...] = _adamw_math(w_ref[...], g, m_ref[...], v_ref[...])

    tile = pl.BlockSpec((tr, c), lambda i, me_ref: (i, 0))
    grid_spec = pltpu.PrefetchScalarGridSpec(
        num_scalar_prefetch=1, grid=(r // tr,),
        in_specs=[tile, pl.BlockSpec((NDEV, tr, c), lambda i, me_ref: (0, i, 0)), tile, tile, tile],
        out_specs=[tile] * 4)
    return pl.pallas_call(
        body, name=name, grid_spec=grid_spec,
        out_shape=[jax.ShapeDtypeStruct((r, c), F32)] * 4,
        compiler_params=_params(("parallel",)),
    )(sel, own, recv, w, m, v)


def _adamw_call(name, w, g, m, v):
    r, c = w.shape
    tr = _row_tile(r)

    def body(w_ref, g_ref, m_ref, v_ref, d_ref, nm_ref, nv_ref):
        d_ref[...], nm_ref[...], nv_ref[...] = _adamw_math(w_ref[...], g_ref[...], m_ref[...], v_ref[...])

    tile = pl.BlockSpec((tr, c), lambda i: (i, 0))
    return pl.pallas_call(
        body, name=name, grid=(r // tr,), in_specs=[tile] * 4, out_specs=[tile] * 3,
        out_shape=[jax.ShapeDtypeStruct((r, c), F32)] * 3,
        compiler_params=_params(("parallel",)),
    )(w, g, m, v)


def _me():
    return lax.axis_index("x"), lax.axis_index("y"), lax.axis_index("c")


def _flip(pos, r):
    x, y, c = pos
    return (1 - x if r & 4 else x, 1 - y if r & 2 else y, 1 - c if r & 1 else c)


def _lin(pos):
    return pos[0] * 4 + pos[1] * 2 + pos[2]


class _SmallAllReduce:
    def __init__(self, g_ref, out_ref, rbuf, send1, recv1, send2, recv2):
        self.g, self.out, self.rbuf = g_ref, out_ref, rbuf
        self.sems = (send1, recv1, send2, recv2)
        self.me = _me()
        self.mi = _lin(self.me)

    @staticmethod
    def _rows(d):
        return pl.ds(pl.multiple_of(d * SMALL_PER, 8), SMALL_PER)

    def _scatter(self, r, outgoing):
        peer = _flip(self.me, r)
        src, dst = (_lin(peer), self.mi) if outgoing else (self.mi, _lin(peer))
        return pltpu.make_async_remote_copy(
            src_ref=self.g.at[self._rows(src)], dst_ref=self.rbuf.at[dst],
            send_sem=self.sems[0].at[r - 1], recv_sem=self.sems[1].at[r - 1], device_id=peer, device_id_type=MESH)

    def _spread(self, r, outgoing):
        peer = _flip(self.me, r)
        rows = self._rows(self.mi if outgoing else _lin(peer))
        return pltpu.make_async_remote_copy(
            src_ref=self.out.at[rows], dst_ref=self.out.at[rows],
            send_sem=self.sems[2].at[r - 1], recv_sem=self.sems[3].at[r - 1], device_id=peer, device_id_type=MESH)

    def scatter(self):
        for r in range(1, NDEV):
            self._scatter(r, True).start()
        self.rbuf[self.mi] = self.g[self._rows(self.mi), :]

    def reduce(self):
        for r in range(1, NDEV):
            self._scatter(r, False).wait_recv()
        for r in range(1, NDEV):
            self._scatter(r, True).wait_send()
        tot = self.rbuf[0]
        for d in range(1, NDEV):
            tot = tot + self.rbuf[d]
        self.out[self._rows(self.mi), :] = tot
        for r in range(1, NDEV):
            self._spread(r, True).start()

    def finish(self):
        for r in range(1, NDEV):
            self._spread(r, False).wait_recv()
        for r in range(1, NDEV):
            self._spread(r, True).wait_send()


def _head_blocks(w):
    z = jnp.zeros((64, 64), w.dtype)
    groups = []
    for q in range(NQ):
        rows = [jnp.concatenate([w[4 * q + a] if a == b else z for b in range(4)], axis=1) for a in range(4)]
        groups.append(jnp.concatenate(rows, axis=0))
    return jnp.stack(groups)


def _head_unblocks(g):
    return jnp.stack([g[q, 64 * a:64 * a + 64, 64 * a:64 * a + 64] for q in range(NQ) for a in range(4)])


def _local_step(x, tgt, p, me, order, order_dw):
    s = x.shape[0]
    vec = lambda a: a.reshape(1, D)
    gmix, gffn, gfin = vec(p["norm_mix_g"]), vec(p["norm_ffn_g"]), vec(p["norm_final_g"])
    cb, ba, bx, lam = vec(p["conv_b"]), vec(p["rg_ba"]), vec(p["rg_bx"]), vec(p["rg_lambda"])
    lng, lnb = vec(p["sgu_ln_g"]), vec(p["sgu_ln_b"])
    wax = jnp.concatenate([_head_blocks(p["rg_wa"]), _head_blocks(p["rg_wx"])], axis=2).astype(BF)
    tril = jnp.tril(jnp.ones((CHUNK, CHUNK), bool))
    ws = jnp.where(tril[None], p["sgu_ws"], 0.0)
    wtr = ws.astype(BF)
    wtrt = jnp.swapaxes(ws, 1, 2).astype(BF)
    bias = jnp.repeat(p["sgu_bs"].T, GW, axis=1)
    shard = {k: p[k].astype(BF) for k in _BIG}

    proj, h1, w_in, (wpa, wpb, wo, cw) = _proj_gather_call(
        x, gmix, shard["w_in"], order, [shard["w_proj_a"], shard["w_proj_b"], shard["w_out"], p["conv_w"]])
    wpa, wpb, wo = (t.reshape(D, D) for t in (wpa, wpb, wo))
    cw = jnp.swapaxes(cw, 0, 1).reshape(4, D)
    (h, pa, pb, x1, *gates, ya, yb, mg), (wgu, wdn) = _mixer_fwd_call(
        proj, x, cw, cb, ba, bx, lam, lng, lnb, wax, wtr, bias, wpa, wpb, wo, [shard["w_gate_up"], shard["w_down"]])
    wdn = wdn.reshape(NDEV // 2, -1, D)
    nb, _, wb = w_in.shape
    fb = wgu.shape[2]
    nh = wdn.shape[0]
    dx1, dx1b, h2, act, dgu, dx2b, facc = _ffn_call(x1, tgt, gffn, gfin, wgu, wdn)
    assert nb == NDEV and 2 * nh == NDEV
    own_gu, part_gu = _dw_plain_call("dw_gate_up", me, dgu, h2, _blk3d(fb), _rows2d(D), 1, fb, D, s)
    own_dn, part_dn = _dw_plain_call("dw_down", me, act, dx2b, _blk3d(fb), _rows2d(D), 2, fb, D, s)
    (dprojb, dya, dpa, dpb, bvec, dws, dbs), (recv_dn,) = _mixer_bwd_pre_call(
        dx1b, proj, pa, pb, lng, lnb, wtr, wtrt, bias, wpa, wpb, wo, [part_dn])
    own_pa, part_pa = _dw_plain_call("dw_proj_a", me, ya, dpa, _rows2d(D), _rows2d(D), NDEV, D, D, s)
    own_pb, part_pb = _dw_plain_call("dw_proj_b", me, yb, dpb, _rows2d(D), _rows2d(D), NDEV, D, D, s)
    own_wo, part_wo = _dw_plain_call("dw_out", me, mg, dx1b, _rows2d(D), _rows2d(D), NDEV, D, D, s)
    (dproj, dx, svec, dwax), (recv_gu, recv_pa, recv_pb, recv_wo) = _mixer_bwd_seq_call(
        dprojb, dya, proj, h, x, dx1, gates, gmix, w_in, cw, lam, wax, [part_gu, part_pa, part_pb, part_wo])
    small = {
        "norm_mix_g": svec[8], "norm_ffn_g": facc[0], "norm_final_g": facc[1],
        "conv_b": svec[0], "rg_ba": svec[1], "rg_bx": svec[2], "rg_lambda": svec[3],
        "sgu_ln_g": bvec[0], "sgu_ln_b": bvec[1],
        "rg_wa": _head_unblocks(dwax[:, :, 0:QW]), "rg_wx": _head_unblocks(dwax[:, :, QW:2 * QW]),
        "sgu_ws": dws, "sgu_bs": dbs[:, ::GW].T,
    }
    packed = _pack_small(small, svec[4:8], facc[2])
    own_in, recv_in, gsum = _dw_exchange_call("dw_in", order_dw, h1, dproj, _rows2d(D), _cols2d(wb), D, wb, s, packed)
    dw = {
        "w_gate_up": (own_gu, recv_gu), "w_down": (own_dn, recv_dn), "w_proj_a": (own_pa, recv_pa),
        "w_proj_b": (own_pb, recv_pb), "w_out": (own_wo, recv_wo), "w_in": (own_in, recv_in),
    }
    return gsum, dx, dw


_BIG = ("w_in", "w_gate_up", "w_down", "w_proj_a", "w_proj_b", "w_out")
_VEC_ROWS = ("norm_mix_g", "norm_ffn_g", "norm_final_g", "conv_b", "rg_ba", "rg_bx", "rg_lambda",
             "sgu_ln_g", "sgu_ln_b", "sgu_bs")
_WEIGHTS = ("norm_mix_g", "w_in", "conv_w", "conv_b", "rg_wa", "rg_ba", "rg_wx", "rg_bx", "rg_lambda",
            "sgu_ln_g", "sgu_ln_b", "sgu_ws", "sgu_bs", "w_proj_a", "w_proj_b", "w_out", "norm_ffn_g",
            "w_gate_up", "w_down", "norm_final_g")


def _pack_small(t, conv_w, extra=None):
    extra = jnp.zeros((1, D), F32) if extra is None else extra.reshape(1, D)
    head = jnp.concatenate([t[k].reshape(1, D) for k in _VEC_ROWS] + [conv_w, extra, jnp.zeros((1, D), F32)], axis=0)
    return jnp.concatenate([head, t["rg_wa"].reshape(64, D), t["rg_wx"].reshape(64, D), t["sgu_ws"].reshape(128, D),
                            jnp.zeros((SMALL_ROWS - 272, D), F32)], axis=0)


def _unpack_small(a):
    out = {k: a[j] for j, k in enumerate(_VEC_ROWS)}
    out["conv_w"] = a[10:14]
    out["rg_wa"] = a[16:80].reshape(16, 64, 64)
    out["rg_wx"] = a[80:144].reshape(16, 64, 64)
    out["sgu_ws"] = a[144:272].reshape(NGRP, CHUNK, CHUNK)
    return out


def kernel(x, norm_mix_g, w_in, conv_w, conv_b, rg_wa, rg_ba, rg_wx, rg_bx, rg_lambda, sgu_ln_g, sgu_ln_b, sgu_ws, sgu_bs, w_proj_a, w_proj_b, w_out, norm_ffn_g, w_gate_up, w_down, norm_final_g, loss_target, m_norm_mix_g, m_w_in, m_conv_w, m_conv_b, m_rg_wa, m_rg_ba, m_rg_wx, m_rg_bx, m_rg_lambda, m_sgu_ln_g, m_sgu_ln_b, m_sgu_ws, m_sgu_bs, m_w_proj_a, m_w_proj_b, m_w_out, m_norm_ffn_g, m_w_gate_up, m_w_down, m_norm_final_g, v_norm_mix_g, v_w_in, v_conv_w, v_conv_b, v_rg_wa, v_rg_ba, v_rg_wx, v_rg_bx, v_rg_lambda, v_sgu_ln_g, v_sgu_ln_b, v_sgu_ws, v_sgu_bs, v_w_proj_a, v_w_proj_b, v_w_out, v_norm_ffn_g, v_w_gate_up, v_w_down, v_norm_final_g):
    args = dict(locals())
    w = {k: args[k] for k in _WEIGHTS}
    m = {k: args["m_" + k] for k in _WEIGHTS}
    v = {k: args["v_" + k] for k in _WEIGHTS}
    for d in (w, m, v):
        for k in _WEIGHTS:
            if k != "norm_final_g":
                d[k] = d[k][0]
    me = _lin(_me())
    me1 = me.reshape(1).astype(jnp.int32)

    order = jnp.bitwise_xor(me, jnp.array(_PASS_FLIPS, jnp.int32)).astype(jnp.int32)
    order_dw = jnp.bitwise_xor(me, jnp.array(_DW_FLIPS, jnp.int32)).astype(jnp.int32)

    gsum, dx, dw = _local_step(x[0], loss_target[0], w, me1, order, order_dw)

    peer = jnp.bitwise_xor(jnp.arange(NDEV, dtype=jnp.int32), me)
    sel_direct = jnp.where(peer == 0, 2, 1).astype(jnp.int32)
    sel_two_level = jnp.where(peer == 0, 2, jnp.where((peer == 1) | (peer % 2 == 0), 1, 0)).astype(jnp.int32)
    grads, delta, new_m, new_v = {}, {}, {}, {}
    for k in _BIG:
        own, recv = dw[k]
        flip = own.shape != w[k].shape
        wmv = [jnp.swapaxes(t, 0, 1) if flip else t for t in (w[k], m[k], v[k])]
        res = _reduce_adamw_call("adamw_" + k, sel_two_level if k == "w_in" else sel_direct, own, recv, *wmv)
        grads[k], delta[k], new_m[k], new_v[k] = (jnp.swapaxes(t, 0, 1) if flip else t for t in res)

    loss = (0.5 / D) * jnp.sum(gsum[14])
    zc = jnp.zeros((4, D), F32)
    d_s, m_s, v_s = _adamw_call("adamw_small", _pack_small(w, zc), gsum, _pack_small(m, zc), _pack_small(v, zc))
    gs, ds, ms, vs = _unpack_small(gsum), _unpack_small(d_s), _unpack_small(m_s), _unpack_small(v_s)
    g_cw = lax.dynamic_slice(gs["conv_w"], (0, me * 128), (4, 128))
    ds["conv_w"], ms["conv_w"], vs["conv_w"] = _adamw_call("adamw_conv_w", w["conv_w"], g_cw, m["conv_w"], v["conv_w"])
    gs["conv_w"] = g_cw
    for k in _WEIGHTS:
        if k not in _BIG:
            shp = w[k].shape
            grads[k], delta[k], new_m[k], new_v[k] = (t[k].reshape(shp) for t in (gs, ds, ms, vs))

    def lift(t, k):
        return t[k] if k == "norm_final_g" else t[k][None]

    outs = [loss, dx[None]]
    for t in (grads, delta, new_m, new_v):
        outs += [lift(t, k) for k in _WEIGHTS]
    return tuple(outs)
```

```python
import functools

import jax
import jax.numpy as jnp
from jax import lax
from jax.experimental import pallas as pl
from jax.experimental.pallas import tpu as pltpu

F32 = jnp.float32
BF = jnp.bfloat16

D = 1024
NDEV = 8
EPS = 1e-6
RG_C = 8.0
CHUNK = 128
NGRP = 8
GW = 128
NQ = 4
QW = 256
RC = 16
SMALL_ROWS = 320
SMALL_PER = SMALL_ROWS // NDEV

ADAM_LR = 0.001
ADAM_B1 = 0.9
ADAM_B2 = 0.999
ADAM_EPS = 1e-08
ADAM_WD = 0.01
ADAM_STEP = 10

VMEM_LIMIT = 60 * 1024 * 1024

MESH = pl.DeviceIdType.MESH


def _rows(n, fn, unroll=2, rc=RC):
    def body(i, c):
        fn(pl.multiple_of(i * rc, rc))
        return c
    lax.fori_loop(0, n // rc, body, 0, unroll=unroll)


def _fold(v):
    return jnp.sum(v.reshape(v.shape[0] // RC, RC, v.shape[1]), axis=0)


def _dot(a, b):
    return jnp.dot(a, b, preferred_element_type=F32)


def _dot_nt(a, b):
    return lax.dot_general(a, b, (((1,), (1,)), ((), ())), preferred_element_type=F32)


def _dot_tn(a, b):
    return lax.dot_general(a, b, (((0,), (0,)), ((), ())), preferred_element_type=F32)


_GC = 0.7978845608028654
_GK = 0.044715


def _gelu(x):
    t = jnp.tanh(_GC * (x + _GK * (x * x * x)))
    return x * (0.5 * (1.0 + t))


def _gelu_grad(x):
    x2 = x * x
    t = jnp.tanh(_GC * (x + _GK * (x2 * x)))
    cdf = 0.5 * (1.0 + t)
    dg = cdf + (0.5 * x) * (1.0 - t * t) * (_GC * (1.0 + (3.0 * _GK) * x2))
    return x * cdf, dg


def _sigmoid(x):
    return jax.nn.sigmoid(x)


def _log1p(e):
    u = 1.0 + e
    d = u - 1.0
    return jnp.where(d == 0.0, e, jnp.log(u) * (e / jnp.where(d == 0.0, 1.0, d)))


def _softplus(z):
    return jnp.maximum(z, 0.0) + _log1p(jnp.exp(-jnp.abs(z)))


def _neg_expm1(z):
    u = jnp.exp(z)
    lu = jnp.log(u)
    k = (1.0 - u) * (z / jnp.where(lu == 0.0, 1.0, lu))
    small = jnp.where(lu == 0.0, -z, k)
    return jnp.where(z > -0.5, small, 1.0 - u)


def _shift_back(prev8, cur, j):
    cat = jnp.concatenate([prev8, cur], axis=0)
    return pltpu.roll(cat, j, 0)[8:8 + cur.shape[0]]


def _shift_fwd(cur, next8, j):
    cat = jnp.concatenate([cur, next8], axis=0)
    n = cat.shape[0]
    return pltpu.roll(cat, n - j, 0)[0:cur.shape[0]]


def _const_spec(shape):
    nd = len(shape)
    return pl.BlockSpec(shape, lambda *_: (0,) * nd, pipeline_mode=pl.Buffered(1))


def _params(sem):
    return pltpu.CompilerParams(dimension_semantics=sem, vmem_limit_bytes=VMEM_LIMIT)


TM_PROJ = 1024
TM_MIX = 256
TM_FFN = 256
TM_DX = 512
TS_DW = 4096


_CHIPS = (4, 2, 6)
_PASS_FLIPS = (0, 1, 4, 2, 6, 5, 3, 7)


class _Gather:
    def __init__(self, ins, outs, send_sems, recv_sems, local_sems):
        self.ins, self.outs = ins, outs
        self.send_sems, self.recv_sems, self.local_sems = send_sems, recv_sems, local_sems
        self.me = _me()
        self.sibling = _flip(self.me, 1)

    def _copy(self, a, kind, block, to, src=None):
        dst = self.outs[a].at[_lin(block)]
        return pltpu.make_async_remote_copy(
            src_ref=dst if src is None else src, dst_ref=dst,
            send_sem=self.send_sems.at[a, kind], recv_sem=self.recv_sems.at[a, kind],
            device_id=to, device_id_type=MESH)

    def _local(self, a):
        return pltpu.make_async_copy(self.ins[a], self.outs[a].at[_lin(self.me)], self.local_sems.at[a])

    def start(self):
        for a in range(len(self.ins)):
            self._local(a).start()
            self._copy(a, 0, self.me, self.sibling, src=self.ins[a]).start()
            for j, f in enumerate(_CHIPS):
                self._copy(a, 1 + j, self.me, _flip(self.me, f), src=self.ins[a]).start()

    def forward(self):
        for j, f in enumerate(_CHIPS):
            for a in range(len(self.ins)):
                self._copy(a, 1 + j, _flip(self.me, f), self.me).wait_recv()
                self._copy(a, 4 + j, _flip(self.me, f), self.sibling).start()

    def finish(self):
        for a in range(len(self.ins)):
            self._copy(a, 0, self.sibling, self.me).wait_recv()
            for j, f in enumerate(_CHIPS):
                self._copy(a, 4 + j, _flip(self.me, f | 1), self.me).wait_recv()
            self._copy(a, 0, self.me, self.sibling, src=self.ins[a]).wait_send()
            for j, f in enumerate(_CHIPS):
                self._copy(a, 1 + j, self.me, _flip(self.me, f), src=self.ins[a]).wait_send()
                self._copy(a, 4 + j, _flip(self.me, f), self.sibling).wait_send()
            self._local(a).wait()


class _Exchange:
    def __init__(self, ins, outs, send_sems, recv_sems, local_sems):
        self.ins, self.outs = ins, outs
        self.send_sems, self.recv_sems, self.local_sems = send_sems, recv_sems, local_sems
        self.me = _me()

    def _copy(self, a, r, outgoing):
        peer = _flip(self.me, r)
        src, dst = (peer, self.me) if outgoing else (self.me, peer)
        return pltpu.make_async_remote_copy(
            src_ref=self.ins[a].at[_lin(src)], dst_ref=self.outs[a].at[_lin(dst)],
            send_sem=self.send_sems.at[a, r - 1], recv_sem=self.recv_sems.at[a, r - 1],
            device_id=peer, device_id_type=MESH)

    def _local(self, a):
        mi = _lin(self.me)
        return pltpu.make_async_copy(self.ins[a].at[mi], self.outs[a].at[mi], self.local_sems.at[a])

    def start(self):
        for a in range(len(self.ins)):
            self._local(a).start()
        for r in range(1, NDEV):
            for a in range(len(self.ins)):
                self._copy(a, r, True).start()

    def finish(self):
        for r in range(1, NDEV):
            for a in range(len(self.ins)):
                self._copy(a, r, False).wait_recv()
        for r in range(1, NDEV):
            for a in range(len(self.ins)):
                self._copy(a, r, True).wait_send()
        for a in range(len(self.ins)):
            self._local(a).wait()


def _gather_sems(n):
    return [pltpu.SemaphoreType.DMA((n, 7)), pltpu.SemaphoreType.DMA((n, 7)), pltpu.SemaphoreType.DMA((n,))]


def _proj_gather_call(x, gmix, w_shard, order, extras):
    s = x.shape[0]
    tm = min(TM_PROJ, s)
    nt = s // tm
    wb = w_shard.shape[1]
    n = len(extras)

    def body(order_ref, x_ref, g_ref, wsh_ref, *rest):
        ex_in = rest[:n]
        proj_ref, h1_hbm, wall_hbm = rest[n:n + 3]
        ex_out = rest[n + 3:2 * n + 3]
        wall, hc, ws_send, ws_recv, ex_send, ex_recv, ex_local, out_sems = rest[2 * n + 3:]
        k = pl.program_id(0)
        i = pl.program_id(1)
        me = _me()
        sibling = _flip(me, 1)
        gather = _Gather(ex_in, ex_out, ex_send, ex_recv, ex_local)

        def wcopy(kind, block, to):
            ref = wall.at[_lin(block)]
            return pltpu.make_async_remote_copy(
                src_ref=ref, dst_ref=ref, send_sem=ws_send.at[kind], recv_sem=ws_recv.at[kind],
                device_id=to, device_id_type=MESH)

        head = i == 0

        @pl.when(head & (k == 0))
        def _():
            wall[_lin(me)] = wsh_ref[...]
            wcopy(0, me, sibling).start()
            for j, f in enumerate(_CHIPS):
                wcopy(1 + j, me, _flip(me, f)).start()
            gather.start()

        @pl.when(head & (k == 1))
        def _():
            wcopy(0, sibling, me).wait_recv()

        for j, f in enumerate(_CHIPS):
            @pl.when(head & (k == 2 + j))
            def _(j=j, f=f):
                wcopy(1 + j, _flip(me, f), me).wait_recv()
                wcopy(4 + j, _flip(me, f), sibling).start()

            @pl.when(head & (k == 5 + j))
            def _(j=j, f=f):
                wcopy(4 + j, _flip(me, f | 1), me).wait_recv()

        pl.when(head & (k == 5))(gather.forward)

        base = pl.multiple_of(i * tm, tm)

        @pl.when(k == 0)
        def _():
            g = g_ref[...]

            def norm(r0):
                xx = x_ref[pl.ds(r0, RC), :]
                r = lax.rsqrt(jnp.mean(xx * xx, axis=-1, keepdims=True) + EPS)
                hc[pl.ds(base + r0, RC), :] = ((xx * r) * g).astype(BF)
            _rows(tm, norm, unroll=4)

        wk = wall.at[order_ref[k]]
        for r0 in range(0, tm, TM_MIX):
            rs = min(TM_MIX, tm - r0)
            proj_ref[r0:r0 + rs, :] = _dot(hc[pl.ds(base + r0, rs), :], wk[...])

        @pl.when((k == NDEV - 1) & (i == nt - 1))
        def _():
            out_w = pltpu.make_async_copy(wall, wall_hbm, out_sems.at[0])
            out_h = pltpu.make_async_copy(hc, h1_hbm, out_sems.at[1])
            out_w.start()
            out_h.start()
            wcopy(0, me, sibling).wait_send()
            for j, f in enumerate(_CHIPS):
                wcopy(1 + j, me, _flip(me, f)).wait_send()
                wcopy(4 + j, _flip(me, f), sibling).wait_send()
            gather.finish()
            out_w.wait()
            out_h.wait()

    any_spec = pl.BlockSpec(memory_space=pl.ANY)
    grid_spec = pltpu.PrefetchScalarGridSpec(
        num_scalar_prefetch=1, grid=(NDEV, nt),
        in_specs=[pl.BlockSpec((tm, D), lambda k, i, o: (jnp.where(k == 0, i, nt - 1), 0)),
                  pl.BlockSpec((1, D), lambda k, i, o: (0, 0)),
                  pl.BlockSpec(w_shard.shape, lambda k, i, o: (0, 0))] + [any_spec] * n,
        out_specs=[pl.BlockSpec((tm, wb), lambda k, i, o: (i, o[k])), any_spec, any_spec] + [any_spec] * n,
        scratch_shapes=[pltpu.VMEM((NDEV,) + w_shard.shape, BF), pltpu.VMEM((s, D), BF),
                        pltpu.SemaphoreType.DMA((7,)), pltpu.SemaphoreType.DMA((7,))] + _gather_sems(n)
        + [pltpu.SemaphoreType.DMA((2,))])
    outs = pl.pallas_call(
        body, name="proj_gather", grid_spec=grid_spec,
        out_shape=[jax.ShapeDtypeStruct((s, NDEV * wb), F32), jax.ShapeDtypeStruct((s, D), BF),
                   jax.ShapeDtypeStruct((NDEV,) + w_shard.shape, BF)]
        + [jax.ShapeDtypeStruct((NDEV,) + e.shape, e.dtype) for e in extras],
        compiler_params=_params(("arbitrary", "arbitrary")),
    )(order, x, gmix, w_shard, *extras)
    return outs[0], outs[1], outs[2], outs[3:]


def _conv_tile(rx, prev8, cw_ref, cb):
    xc = cb + cw_ref[3:4, :] * rx
    for j in (1, 2, 3):
        xc = xc + cw_ref[3 - j:4 - j, :] * _shift_back(prev8, rx, j)
    return xc


def _gate_tile(xcb, wax_ref, ba, bx, sp, q):
    cs = slice(q * QW, (q + 1) * QW)
    z = _dot(xcb[:, cs], wax_ref[q])
    r = _sigmoid(z[:, 0:QW] + ba[:, cs])
    ig = _sigmoid(z[:, QW:2 * QW] + bx[:, cs])
    return r, ig, (-RG_C * r) * sp[:, cs]


def _scan_fwd(a_s, b_s, out_ref, h0, n):
    rowi = lax.broadcasted_iota(jnp.int32, (8, D), 0)

    def block(t, h):
        rows = pl.ds(t * 8, 8)
        a = a_s[rows, :]
        b = b_s[rows, :]
        for d in (1, 2, 4):
            m = rowi >= d
            b = jnp.where(m, a * pltpu.roll(b, d, 0) + b, b)
            a = jnp.where(m, a * pltpu.roll(a, d, 0), a)
        hb = b + a * h
        out_ref[rows, :] = hb
        return hb[7:8, :]
    h = h0
    for t in range(n // 8):
        h = block(t, h)
    return h


def _scan_bwd(a_s, lm_s, c0, n):
    rowi = lax.broadcasted_iota(jnp.int32, (8, D), 0)
    nblk = n // 8

    def block(k, cin):
        rows = pl.ds((nblk - 1 - k) * 8, 8)
        a = a_s[rows, :]
        dh = lm_s[rows, :]
        b = a * dh
        for d in (1, 2, 4):
            m = rowi < 8 - d
            b = jnp.where(m, a * pltpu.roll(b, 8 - d, 0) + b, b)
            a = jnp.where(m, a * pltpu.roll(a, 8 - d, 0), a)
        mu = b + a * cin
        lm_s[rows, :] = dh + jnp.where(rowi < 7, pltpu.roll(mu, 7, 0), cin)
        return mu[0:1, :]
    c = c0
    for k in range(nblk):
        c = block(k, c)
    return c


def _mixer_fwd_call(proj, x, cw, cb, ba, bx, lam, lng, lnb, wax, wtr, bias, wpa, wpb, wo, shards):
    s = x.shape[0]
    tm = min(TM_MIX, s)
    nt = s // tm
    pw = proj.shape[1]
    n = len(shards)

    def body(proj_ref, x_ref, cw_ref, cb_ref, ba_ref, bx_ref, lam_ref, lng_ref, lnb_ref, wax_ref, wtr_ref,
             bias_ref, wpa_ref, wpb_ref, wo_ref, *rest):
        sh_in = rest[:n]
        h_ref, pa_ref, pb_ref, x1_ref, xc_ref, r_ref, ig_ref, a_s, m_ref, ya_ref, yb_ref, mg_ref = rest[n:n + 12]
        sh_out = rest[n + 12:2 * n + 12]
        prev_s, b_s, hc_s, vn_s, mx_s, g_send, g_recv, g_local = rest[2 * n + 12:]
        i = pl.program_id(0)
        gather = _Gather(sh_in, sh_out, g_send, g_recv, g_local)

        @pl.when(i == 0)
        def _():
            gather.start()
            prev_s[...] = jnp.zeros((8, D), F32)
            hc_s[...] = jnp.zeros((8, D), F32)

        pl.when(i == nt // 2)(gather.forward)

        rx = proj_ref[:, 0:D]
        xc = _conv_tile(rx, prev_s[...], cw_ref, cb_ref[...])
        prev_s[...] = rx[tm - 8:tm, :]
        xc_ref[...] = xc.astype(BF)
        xcb = xc.astype(BF)
        sp = _softplus(-lam_ref[...])
        ba = ba_ref[...]
        bx = bx_ref[...]
        for q in range(NQ):
            cs = slice(q * QW, (q + 1) * QW)
            r, ig, la = _gate_tile(xcb, wax_ref, ba, bx, sp, q)
            mq = jnp.sqrt(_neg_expm1(2.0 * la))
            r_ref[:, cs] = r.astype(BF)
            ig_ref[:, cs] = ig.astype(BF)
            m_ref[:, cs] = mq.astype(BF)
            a_s[:, cs] = jnp.exp(la)
            b_s[:, cs] = mq * (ig * xc[:, cs])

        gv = _gelu(proj_ref[:, 3 * D:4 * D])
        dv = gv - jnp.mean(gv, axis=-1, keepdims=True)
        var = jnp.mean(dv * dv, axis=-1, keepdims=True)
        vn_s[...] = ((dv * lax.rsqrt(var + EPS)) * lng_ref[...] + lnb_ref[...]).astype(BF)
        nc = tm // CHUNK
        for c in range(nc):
            rs = slice(c * CHUNK, (c + 1) * CHUNK)
            for g in range(NGRP):
                cs = slice(g * GW, (g + 1) * GW)
                mx_s[rs, cs] = _dot(wtr_ref[g], vn_s[rs, cs])
        mixed = mx_s[...] + jnp.concatenate([bias_ref[...]] * nc, axis=0)
        yb = (_gelu(proj_ref[:, 2 * D:3 * D]) * mixed).astype(BF)
        yb_ref[...] = yb
        pb = _dot(yb, wpb_ref[...])
        pb_ref[...] = pb.astype(BF)
        mx_s[...] = pb

        hc_s[0:1, :] = _scan_fwd(a_s, b_s, h_ref, hc_s[0:1, :], tm)

        ya = (_gelu(proj_ref[:, D:2 * D]) * h_ref[...]).astype(BF)
        ya_ref[...] = ya
        pa = _dot(ya, wpa_ref[...])
        pa_ref[...] = pa.astype(BF)
        mg = (_sigmoid(proj_ref[:, 4 * D:5 * D]) * pa + _sigmoid(proj_ref[:, 5 * D:6 * D]) * mx_s[...]).astype(BF)
        mg_ref[...] = mg
        x1_ref[...] = x_ref[...] + _dot(mg, wo_ref[...])

        pl.when(i == nt - 1)(gather.finish)

    tile = lambda w: pl.BlockSpec((tm, w), lambda i: (i, 0))
    vec = _const_spec((1, D))
    any_spec = pl.BlockSpec(memory_space=pl.ANY)
    outs = pl.pallas_call(
        body, name="mixer_fwd", grid=(nt,),
        in_specs=[tile(pw), tile(D), _const_spec((4, D)), vec, vec, vec, vec, vec, vec,
                  _const_spec(wax.shape), _const_spec(wtr.shape), _const_spec(bias.shape),
                  _const_spec((D, D)), _const_spec((D, D)), _const_spec((D, D))] + [any_spec] * n,
        out_specs=[tile(D)] * 12 + [any_spec] * n,
        out_shape=[jax.ShapeDtypeStruct((s, D), dt) for dt in (F32, BF, BF, F32, BF, BF, BF, F32, BF, BF, BF, BF)]
        + [jax.ShapeDtypeStruct((NDEV,) + e.shape, e.dtype) for e in shards],
        scratch_shapes=[pltpu.VMEM((8, D), F32), pltpu.VMEM((tm, D), F32),
                        pltpu.VMEM((8, D), F32), pltpu.VMEM((tm, D), BF), pltpu.VMEM((tm, D), F32)]
        + _gather_sems(n),
        compiler_params=_params(("arbitrary",)),
    )(proj, x, cw, cb, ba, bx, lam, lng, lnb, wax, wtr, bias, wpa, wpb, wo, *shards)
    return outs[:12], outs[12:]


def _ffn_call(x1, tgt, gffn, gfin, wgu, wdn):
    s = x1.shape[0]
    tm = min(TM_FFN, s)
    nt = s // tm
    nh = wdn.shape[0]
    fb = wgu.shape[2]

    def body(x1_ref, tgt_ref, gffn_ref, gfin_ref, wgu_ref, wdn_ref,
             dx1_ref, dx1b_ref, h2_ref, act_ref, dgu_ref, dx2b_ref, acc_ref,
             g_s, u_s, dx2_s, accs):
        i = pl.program_id(0)

        @pl.when(i == 0)
        def _():
            accs[...] = jnp.zeros(accs.shape, F32)

        gffn = gffn_ref[...]
        gfin = gfin_ref[...]

        x1 = x1_ref[...]
        r2 = lax.rsqrt(jnp.mean(x1 * x1, axis=-1, keepdims=True) + EPS)
        xh2 = x1 * r2
        h2 = (xh2 * gffn).astype(BF)
        h2_ref[...] = h2

        for k in range(nh):
            g = _dot(h2, wgu_ref[k])
            u = _dot(h2, wgu_ref[k + nh])
            g_s[k] = g
            u_s[k] = u
            act_ref[k] = ((g * _sigmoid(g)) * u).astype(BF)

        x2 = x1
        for k in range(nh):
            x2 = x2 + _dot(act_ref[k], wdn_ref[k])

        r3 = lax.rsqrt(jnp.mean(x2 * x2, axis=-1, keepdims=True) + EPS)
        xh = x2 * r3
        err = xh * gfin - tgt_ref[...]
        accs[2] += _fold(err * err)
        dy = err * (1.0 / D)
        accs[1] += _fold(dy * xh)
        dxh = dy * gfin
        dx2 = r3 * (dxh - xh * jnp.mean(dxh * xh, axis=-1, keepdims=True))
        dx2_s[...] = dx2
        dx2b = dx2.astype(BF)
        dx2b_ref[...] = dx2b

        for k in range(nh):
            da = _dot_nt(dx2b, wdn_ref[k])
            g = g_s[k]
            sg = _sigmoid(g)
            dgu_ref[k] = ((da * u_s[k]) * (sg * (1.0 + g * (1.0 - sg)))).astype(BF)
            dgu_ref[k + nh] = (da * (g * sg)).astype(BF)

        dh2 = _dot_nt(dgu_ref[0], wgu_ref[0])
        for k in range(1, 2 * nh):
            dh2 = dh2 + _dot_nt(dgu_ref[k], wgu_ref[k])

        accs[0] += _fold(dh2 * xh2)
        dxh = dh2 * gffn
        dx1 = dx2_s[...] + r2 * (dxh - xh2 * jnp.mean(dxh * xh2, axis=-1, keepdims=True))
        dx1_ref[...] = dx1
        dx1b_ref[...] = dx1.astype(BF)

        @pl.when(i == nt - 1)
        def _():
            acc_ref[...] = jnp.zeros((8, D), F32)
            for j in range(3):
                acc_ref[j:j + 1, :] = jnp.sum(accs[j], axis=0, keepdims=True)

    tile = lambda w: pl.BlockSpec((tm, w), lambda i: (i, 0))
    vec = _const_spec((1, D))
    return pl.pallas_call(
        body, name="ffn", grid=(nt,),
        in_specs=[tile(D), tile(D), vec, vec, _const_spec(wgu.shape), _const_spec(wdn.shape)],
        out_specs=[tile(D), tile(D), tile(D),
                   pl.BlockSpec((nh, tm, fb), lambda i: (0, i, 0)),
                   pl.BlockSpec((2 * nh, tm, fb), lambda i: (0, i, 0)),
                   tile(D), pl.BlockSpec((8, D), lambda i: (0, 0))],
        out_shape=[jax.ShapeDtypeStruct((s, D), F32), jax.ShapeDtypeStruct((s, D), BF),
                   jax.ShapeDtypeStruct((s, D), BF), jax.ShapeDtypeStruct((nh, s, fb), BF),
                   jax.ShapeDtypeStruct((2 * nh, s, fb), BF), jax.ShapeDtypeStruct((s, D), BF),
                   jax.ShapeDtypeStruct((8, D), F32)],
        scratch_shapes=[pltpu.VMEM((nh, tm, fb), F32), pltpu.VMEM((nh, tm, fb), F32), pltpu.VMEM((tm, D), F32),
                        pltpu.VMEM((3, RC, D), F32)],
        compiler_params=_params(("arbitrary",)),
    )(x1, tgt, gffn, gfin, wgu, wdn)


def _mixer_bwd_pre_call(dx1b, proj, pa, pb, lng, lnb, wtr, wtrt, bias, wpa, wpb, wo, parts):
    s = dx1b.shape[0]
    tm = min(TM_MIX, s)
    nt = s // tm
    pw = proj.shape[1]
    n = len(parts)

    def body(dx1b_ref, uv_ref, gg_ref, pa_ref, pb_ref, lng_ref, lnb_ref, wtr_ref, wtrt_ref, bias_ref,
             wpa_ref, wpb_ref, wo_ref, *rest):
        ex_in = rest[:n]
        dproj_ref, dya_ref, dpa_ref, dpb_ref, vec_ref, dws_ref, dbs_ref = rest[n:n + 7]
        ex_out = rest[n + 7:2 * n + 7]
        vn_s, mx_s, dmx_s, dvn_s, accs, dbs_s, e_send, e_recv, e_local = rest[2 * n + 7:]
        i = pl.program_id(0)
        exchange = _Exchange(ex_in, ex_out, e_send, e_recv, e_local)

        @pl.when(i == 0)
        def _():
            exchange.start()
            accs[...] = jnp.zeros(accs.shape, F32)
            dbs_s[...] = jnp.zeros(dbs_s.shape, F32)
            dws_ref[...] = jnp.zeros(dws_ref.shape, F32)

        dm = _dot_nt(dx1b_ref[...], wo_ref[...])
        sa = _sigmoid(gg_ref[:, 0:D])
        sb = _sigmoid(gg_ref[:, D:2 * D])
        dpa = dm * sa
        dpb = dm * sb
        dpab = dpa.astype(BF)
        dpbb = dpb.astype(BF)
        dpa_ref[...] = dpab
        dpb_ref[...] = dpbb
        dproj_ref[:, 2 * D:3 * D] = ((dpa * pa_ref[...].astype(F32)) * (1.0 - sa)).astype(BF)
        dproj_ref[:, 3 * D:4 * D] = ((dpb * pb_ref[...].astype(F32)) * (1.0 - sb)).astype(BF)

        dya_ref[...] = _dot_nt(dpab, wpa_ref[...])
        dyb = _dot_nt(dpbb, wpb_ref[...])

        lng = lng_ref[...]
        gv, dgelu_v = _gelu_grad(uv_ref[:, D:2 * D])
        dv = gv - jnp.mean(gv, axis=-1, keepdims=True)
        rstd = lax.rsqrt(jnp.mean(dv * dv, axis=-1, keepdims=True) + EPS)
        xh = dv * rstd
        vn_s[...] = (xh * lng + lnb_ref[...]).astype(BF)

        nc = tm // CHUNK
        for c in range(nc):
            rs = slice(c * CHUNK, (c + 1) * CHUNK)
            for g in range(NGRP):
                cs = slice(g * GW, (g + 1) * GW)
                mx_s[rs, cs] = _dot(wtr_ref[g], vn_s[rs, cs])

        gu, dgelu_u = _gelu_grad(uv_ref[:, 0:D])
        mixed = mx_s[...] + jnp.concatenate([bias_ref[...]] * nc, axis=0)
        dproj_ref[:, 0:D] = ((dyb * mixed) * dgelu_u).astype(BF)
        dmx = dyb * gu
        dmx_s[...] = dmx.astype(BF)
        dbs_s[...] += jnp.sum(dmx.reshape(nc, CHUNK, D), axis=0)

        for c in range(nc):
            rs = slice(c * CHUNK, (c + 1) * CHUNK)
            for g in range(NGRP):
                cs = slice(g * GW, (g + 1) * GW)
                dvn_s[rs, cs] = _dot(wtrt_ref[g], dmx_s[rs, cs])
                dws_ref[g] += _dot_nt(dmx_s[rs, cs], vn_s[rs, cs])

        dvn = dvn_s[...]
        accs[0] += _fold(dvn * xh)
        accs[1] += _fold(dvn)
        dxh = dvn * lng
        m1 = jnp.mean(dxh, axis=-1, keepdims=True)
        m2 = jnp.mean(dxh * xh, axis=-1, keepdims=True)
        dproj_ref[:, D:2 * D] = ((rstd * (dxh - m1 - xh * m2)) * dgelu_v).astype(BF)

        @pl.when(i == nt - 1)
        def _():
            vec_ref[...] = jnp.zeros((8, D), F32)
            for j in range(2):
                vec_ref[j:j + 1, :] = jnp.sum(accs[j], axis=0, keepdims=True)
            row = lax.broadcasted_iota(jnp.int32, (CHUNK, CHUNK), 0)
            col = lax.broadcasted_iota(jnp.int32, (CHUNK, CHUNK), 1)
            for g in range(NGRP):
                dws_ref[g] = jnp.where(row >= col, dws_ref[g], 0.0)
                gs = jnp.sum(dbs_s[:, g * GW:(g + 1) * GW], axis=1, keepdims=True)
                dbs_ref[:, g * GW:(g + 1) * GW] = jnp.broadcast_to(gs, (CHUNK, GW))
            exchange.finish()

    tile = lambda w: pl.BlockSpec((tm, w), lambda i: (i, 0))
    vec = _const_spec((1, D))
    any_spec = pl.BlockSpec(memory_space=pl.ANY)
    outs = pl.pallas_call(
        body, name="mixer_bwd_pre", grid=(nt,),
        in_specs=[tile(D), pl.BlockSpec((tm, 2 * D), lambda i: (i, 1)), pl.BlockSpec((tm, 2 * D), lambda i: (i, 2)),
                  tile(D), tile(D), vec, vec, _const_spec(wtr.shape), _const_spec(wtrt.shape),
                  _const_spec(bias.shape), _const_spec((D, D)), _const_spec((D, D)), _const_spec((D, D))]
        + [any_spec] * n,
        out_specs=[tile(4 * D), tile(D), tile(D), tile(D), pl.BlockSpec((8, D), lambda i: (0, 0)),
                   pl.BlockSpec((NGRP, CHUNK, CHUNK), lambda i: (0, 0, 0)),
                   pl.BlockSpec((CHUNK, D), lambda i: (0, 0))] + [any_spec] * n,
        out_shape=[jax.ShapeDtypeStruct((s, 4 * D), BF), jax.ShapeDtypeStruct((s, D), F32),
                   jax.ShapeDtypeStruct((s, D), BF), jax.ShapeDtypeStruct((s, D), BF),
                   jax.ShapeDtypeStruct((8, D), F32), jax.ShapeDtypeStruct((NGRP, CHUNK, CHUNK), F32),
                   jax.ShapeDtypeStruct((CHUNK, D), F32)]
        + [jax.ShapeDtypeStruct(p.shape, p.dtype) for p in parts],
        scratch_shapes=[pltpu.VMEM((tm, D), BF), pltpu.VMEM((tm, D), F32), pltpu.VMEM((tm, D), BF),
                        pltpu.VMEM((tm, D), F32), pltpu.VMEM((2, RC, D), F32), pltpu.VMEM((CHUNK, D), F32)]
        + _gather_sems(n),
        compiler_params=_params(("arbitrary",)),
    )(dx1b, proj, proj, pa, pb, lng, lnb, wtr, wtrt, bias, wpa, wpb, wo, *parts)
    return outs[:7], outs[7:]


def _mixer_bwd_seq_call(dprojb, dya, proj, h, x, dx1, gates, gmix, w_in_all, cw, lam, wax, parts):
    s = dya.shape[0]
    tm = min(TM_MIX, s)
    nt = s // tm
    tb = tm // 8
    n = len(parts)
    nb, _, wb = w_in_all.shape

    def body(dpb_ref, dya_ref, xg_ref, xh8_ref, h_ref, hh8_ref, x_ref, dx1_ref, xc_s, r_s, ig_s, a_s, m_s,
             gmix_ref, win_ref, cw_ref, lam_ref, wax_ref, *rest):
        ex_in = rest[:n]
        dpa_ref, dx_ref, vec_ref, dwax_ref = rest[n:n + 4]
        ex_out = rest[n + 4:2 * n + 4]
        lm_s, dh_s, dxc_s, c_s, accs, e_send, e_recv, e_local = rest[2 * n + 4:]
        i = pl.program_id(0)
        exchange = _Exchange(ex_in, ex_out, e_send, e_recv, e_local)

        @pl.when(i == 0)
        def _():
            exchange.start()
            accs[...] = jnp.zeros(accs.shape, F32)
            dwax_ref[...] = jnp.zeros(dwax_ref.shape, F32)
            c_s[...] = jnp.zeros((8, D), F32)
            dxc_s[tm:tm + 8, :] = jnp.zeros((8, D), F32)

        first_tile = i == nt - 1
        prev8 = jnp.where(first_tile, 0.0, xh8_ref[...])
        hprev8 = jnp.where(first_tile, 0.0, hh8_ref[...])
        lamv = lam_ref[...]
        sp = _softplus(-lamv)
        hv = h_ref[...]
        g, dg = _gelu_grad(xg_ref[:, D:2 * D])
        dya = dya_ref[...]
        lm_s[...] = dya * g
        drg = ((dya * hv) * dg).astype(BF)
        dpa_ref[:, D:2 * D] = drg
        dpa_ref[:, 2 * D:6 * D] = dpb_ref[...]

        def dpb_block(k):
            return _dot_nt(dpb_ref[:, k * wb - 2 * D:(k + 1) * wb - 2 * D], win_ref[k])
        dh = (_dot_nt(drg[:, 0:2 * wb - D], win_ref[1, :, D - wb:wb])
              + _dot_nt(drg[:, 2 * wb - D:D], win_ref[2, :, 0:2 * D - 2 * wb])
              + _dot_nt(dpb_ref[:, 0:3 * wb - 2 * D], win_ref[2, :, 2 * D - 2 * wb:wb]))
        for k in range(3, 6):
            dh = dh + dpb_block(k)
        dh_s[...] = dh

        c_s[0:1, :] = _scan_bwd(a_s, lm_s, c_s[0:1, :], tm)

        hprev = _shift_back(hprev8, h_ref[...], 1)
        for q in range(NQ):
            cs = slice(q * QW, (q + 1) * QW)
            r = r_s[:, cs].astype(F32)
            ig = ig_s[:, cs].astype(F32)
            a = a_s[:, cs]
            m = m_s[:, cs].astype(F32)
            lm = lm_s[:, cs]
            xq = xc_s[:, cs].astype(F32)
            dixc = lm * m
            dla = (lm * hprev[:, cs]) * a - ((lm * (ig * xq)) * (a * a)) / m
            accs[3, :, cs] += _fold(dla * r)
            dza = (dla * (-RG_C * sp[:, cs])) * (r * (1.0 - r))
            dzx = (dixc * xq) * (ig * (1.0 - ig))
            accs[1, :, cs] += _fold(dza)
            accs[2, :, cs] += _fold(dzx)
            dz = jnp.concatenate([dza, dzx], axis=1).astype(BF)
            dxc_s[0:tm, cs] = dixc * ig + _dot_nt(dz, wax_ref[q])
            dwax_ref[q] += _dot_tn(xc_s[:, cs], dz)

        cur = dxc_s[0:tm, :]
        nx = dxc_s[tm:tm + 8, :]
        drx = cw_ref[3:4, :] * cur
        for j in (1, 2, 3):
            drx = drx + cw_ref[3 - j:4 - j, :] * _shift_fwd(cur, nx, j)
        accs[0] += _fold(cur)
        rx = xg_ref[:, 0:D]
        accs[7] += _fold(cur * rx)
        for j in (1, 2, 3):
            accs[7 - j] += _fold(cur * _shift_back(prev8, rx, j))
        dxc_s[tm:tm + 8, :] = cur[0:8, :]

        drxb = drx.astype(BF)
        dpa_ref[:, 0:D] = drxb
        dh = dh_s[...] + _dot_nt(drxb[:, 0:wb], win_ref[0]) + _dot_nt(drxb[:, wb:D], win_ref[1, :, 0:D - wb])
        for k in range(6, nb):
            dh = dh + dpb_block(k)
        xx = x_ref[...]
        rn = lax.rsqrt(jnp.mean(xx * xx, axis=-1, keepdims=True) + EPS)
        xhn = xx * rn
        accs[8] += _fold(dh * xhn)
        dxh = dh * gmix_ref[...]
        dx_ref[...] = dx1_ref[...] + rn * (dxh - xhn * jnp.mean(dxh * xhn, axis=-1, keepdims=True))

        @pl.when(i == nt - 1)
        def _():
            vec_ref[...] = jnp.zeros((16, D), F32)
            for j in range(9):
                vec_ref[j:j + 1, :] = jnp.sum(accs[j], axis=0, keepdims=True)
            vec_ref[3:4, :] = vec_ref[3:4, :] * (RG_C * _sigmoid(-lamv))
            exchange.finish()

    rev = lambda w: pl.BlockSpec((tm, w), lambda i: (nt - 1 - i, 0))
    halo = pl.BlockSpec((8, D), lambda i: (jnp.maximum((nt - 1 - i) * tb - 1, 0), 0))
    vec = _const_spec((1, D))
    any_spec = pl.BlockSpec(memory_space=pl.ANY)
    outs = pl.pallas_call(
        body, name="mixer_bwd_seq", grid=(nt,),
        in_specs=[rev(4 * D), rev(D), rev(2 * D), halo, rev(D), halo] + [rev(D)] * 7
        + [vec, _const_spec(w_in_all.shape), _const_spec((4, D)), vec, _const_spec(wax.shape)] + [any_spec] * n,
        out_specs=[rev(6 * D), rev(D), pl.BlockSpec((16, D), lambda i: (0, 0)),
                   pl.BlockSpec((NQ, QW, 2 * QW), lambda i: (0, 0, 0))] + [any_spec] * n,
        out_shape=[jax.ShapeDtypeStruct((s, 6 * D), BF), jax.ShapeDtypeStruct((s, D), F32),
                   jax.ShapeDtypeStruct((16, D), F32), jax.ShapeDtypeStruct((NQ, QW, 2 * QW), F32)]
        + [jax.ShapeDtypeStruct(p.shape, p.dtype) for p in parts],
        scratch_shapes=[pltpu.VMEM((tm, D), F32), pltpu.VMEM((tm, D), F32),
                        pltpu.VMEM((tm + 8, D), F32), pltpu.VMEM((8, D), F32), pltpu.VMEM((9, RC, D), F32)]
        + _gather_sems(n),
        compiler_params=_params(("arbitrary",)),
    )(dprojb, dya, proj, proj, h, h, x, dx1, *gates, gmix, w_in_all, cw, lam, wax, *parts)
    return outs[:4], outs[4:]


def _device_of(d):
    return (d // 4, lax.rem(d // 2, 2), lax.rem(d, 2))


_DW_PLAN = (("C", 1), ("C", 2), ("A", 1), ("C", 3), ("A", 2), ("S", 0), ("A", 3), ("O", 0))
_DW_FLIPS = tuple({"C": 2 * j + 1, "A": 2 * j, "S": 1, "O": 0}[kind] for kind, j in _DW_PLAN)


def _dw_exchange_call(name, order, a, b, a_spec, b_spec, k1, n1, s, small):
    ts = min(TS_DW, s)
    ns = s // ts
    nstep = len(_DW_PLAN)

    def slab(i, order_ref):
        return order_ref[i]

    def body(order_ref, a_ref, b_ref, g_ref, own_ref, recv_ref, gsum_ref, acc, sbuf, stage, send_sems, recv_sems,
             st_send, st_recv, local_sem, rbuf, gacc, send1, recv1, send2, recv2):
        i = pl.program_id(0)
        j = pl.program_id(1)
        me = _me()
        mi = _lin(me)
        sibling = _flip(me, 1)
        allreduce = _SmallAllReduce(g_ref, gacc, rbuf, send1, recv1, send2, recv2)
        pl.when((i == 0) & (j == 0))(allreduce.scatter)
        pl.when((i == nstep // 2) & (j == 0))(allreduce.reduce)
        p = _dot_tn(a_ref[...], b_ref[...])

        @pl.when(j == 0)
        def _():
            acc[...] = p

        @pl.when(j > 0)
        def _():
            acc[...] += p

        def send(step):
            kind, jj = _DW_PLAN[step]
            src = sbuf.at[step % 2]
            if kind == "C":
                return pltpu.make_async_remote_copy(
                    src_ref=src, dst_ref=stage.at[jj - 1], send_sem=st_send.at[jj - 1], recv_sem=st_recv.at[jj - 1],
                    device_id=sibling, device_id_type=MESH)
            to = sibling if kind == "S" else _flip(me, 2 * jj)
            return pltpu.make_async_remote_copy(
                src_ref=src, dst_ref=recv_ref.at[mi], send_sem=send_sems.at[jj], recv_sem=recv_sems.at[jj],
                device_id=to, device_id_type=MESH)

        def arrival(jj):
            frm = sibling if jj == 0 else _flip(me, 2 * jj)
            return pltpu.make_async_remote_copy(
                src_ref=sbuf.at[0], dst_ref=recv_ref.at[_lin(frm)], send_sem=send_sems.at[jj],
                recv_sem=recv_sems.at[jj], device_id=frm, device_id_type=MESH)

        def staged(jj):
            return pltpu.make_async_remote_copy(
                src_ref=sbuf.at[0], dst_ref=stage.at[jj - 1], send_sem=st_send.at[jj - 1], recv_sem=st_recv.at[jj - 1],
                device_id=sibling, device_id_type=MESH)

        for step, (kind, jj) in enumerate(_DW_PLAN):
            @pl.when((i == step) & (j == ns - 1))
            def _(step=step, kind=kind, jj=jj):
                if step >= 2:
                    send(step - 2).wait_send()
                if kind == "A":
                    staged(jj).wait_recv()
                    sbuf[step % 2] = (acc[...] + stage[jj - 1].astype(F32)).astype(BF)
                else:
                    sbuf[step % 2] = acc[...].astype(BF)
                if kind != "O":
                    send(step).start()
                else:
                    own_ref[...] = acc[...]
                    mine = pltpu.make_async_copy(sbuf.at[step % 2], recv_ref.at[mi], local_sem)
                    mine.start()
                    send(step - 1).wait_send()
                    for q in range(4):
                        arrival(q).wait_recv()
                    mine.wait()
                    allreduce.finish()
                    gsum_ref[...] = gacc[...]

    vmem = pl.BlockSpec(memory_space=pltpu.VMEM)
    grid_spec = pltpu.PrefetchScalarGridSpec(
        num_scalar_prefetch=1, grid=(nstep, ns),
        in_specs=[a_spec(ts, slab), b_spec(ts, slab), vmem],
        out_specs=[pl.BlockSpec((k1, n1), lambda i, j, o: (0, 0)), pl.BlockSpec(memory_space=pl.ANY), vmem],
        scratch_shapes=[pltpu.VMEM((k1, n1), F32), pltpu.VMEM((2, k1, n1), BF), pltpu.VMEM((3, k1, n1), BF),
                        pltpu.SemaphoreType.DMA((4,)), pltpu.SemaphoreType.DMA((4,)),
                        pltpu.SemaphoreType.DMA((3,)), pltpu.SemaphoreType.DMA((3,)),
                        pltpu.SemaphoreType.DMA(()), pltpu.VMEM((NDEV, SMALL_PER, D), F32),
                        pltpu.VMEM((SMALL_ROWS, D), F32)]
        + [pltpu.SemaphoreType.DMA((NDEV - 1,))] * 4)
    return pl.pallas_call(
        body, name=name, grid_spec=grid_spec,
        out_shape=[jax.ShapeDtypeStruct((k1, n1), F32), jax.ShapeDtypeStruct((NDEV, k1, n1), BF),
                   jax.ShapeDtypeStruct((SMALL_ROWS, D), F32)],
        compiler_params=_params(("arbitrary", "arbitrary")),
    )(order, a, b, small)


def _dw_plain_call(name, me, a, b, a_spec, b_spec, split, k1, n1, s):
    nb = NDEV // split
    r = k1 // split
    ts = min(TS_DW, s)
    ns = s // ts

    def slab(i, me_ref):
        return i

    def body(me_ref, a_ref, b_ref, own_ref, part_ref, acc):
        i = pl.program_id(0)
        j = pl.program_id(1)
        p = _dot_tn(a_ref[...], b_ref[...])

        @pl.when(j == 0)
        def _():
            acc[...] = p

        @pl.when(j > 0)
        def _():
            acc[...] += p

        @pl.when(j == ns - 1)
        def _():
            part_ref[...] = acc[...].astype(BF)

            @pl.when(i == me_ref[0] // split)
            def _():
                off = pl.multiple_of(lax.rem(me_ref[0], split) * r, RC)
                own_ref[...] = acc[pl.ds(off, r), :]

    grid_spec = pltpu.PrefetchScalarGridSpec(
        num_scalar_prefetch=1, grid=(nb, ns),
        in_specs=[a_spec(ts, slab), b_spec(ts, slab)],
        out_specs=[pl.BlockSpec((r, n1), lambda i, j, me_ref: (0, 0)),
                   pl.BlockSpec((None, k1, n1), lambda i, j, me_ref: (i, 0, 0))],
        scratch_shapes=[pltpu.VMEM((k1, n1), F32)])
    own, part = pl.pallas_call(
        body, name=name, grid_spec=grid_spec,
        out_shape=[jax.ShapeDtypeStruct((r, n1), F32), jax.ShapeDtypeStruct((nb, k1, n1), BF)],
        compiler_params=_params(("arbitrary", "arbitrary")),
    )(me, a, b)
    return own, part.reshape(NDEV, r, n1)


def _rows2d(w):
    return lambda ts, slab: pl.BlockSpec((ts, w), lambda i, j, me_ref: (j, 0))


def _cols2d(w):
    return lambda ts, slab: pl.BlockSpec((ts, w), lambda i, j, me_ref: (j, slab(i, me_ref)))


def _blk3d(w):
    return lambda ts, slab: pl.BlockSpec((None, ts, w), lambda i, j, me_ref: (slab(i, me_ref), j, 0))


_BC1 = 1.0 - ADAM_B1 ** ADAM_STEP
_BC2 = 1.0 - ADAM_B2 ** ADAM_STEP


def _adamw_math(w, g, m, v):
    m = ADAM_B1 * m + (1.0 - ADAM_B1) * g
    v = ADAM_B2 * v + (1.0 - ADAM_B2) * (g * g)
    m_hat = m / _BC1
    v_hat = v / _BC2
    delta = -ADAM_LR * (m_hat / (jnp.sqrt(v_hat) + ADAM_EPS) + ADAM_WD * w)
    return delta, m, v


def _row_tile(r):
    for t in (256, 176, 128, 64, 32, 16, 8):
        if r % t == 0:
            return t
    return r


def _reduce_adamw_call(name, sel, own, recv, w, m, v):
    r, c = own.shape
    tr = _row_tile(r)

    def body(sel_ref, own_ref, recv_ref, w_ref, m_ref, v_ref, g_ref, d_ref, nm_ref, nv_ref):
        g = jnp.zeros((tr, c), F32)
        for sdev in range(NDEV):
            part = jnp.where(sel_ref[sdev] == 1, recv_ref[sdev].astype(F32), 0.0)
            g = g + jnp.where(sel_ref[sdev] == 2, own_ref[...], part)
        g_ref[...] = g
        d_ref[...], nm_ref[...], nv_ref[...] = _adamw_math(w_ref[...], g, m_ref[...], v_ref[...])

    tile = pl.BlockSpec((tr, c), lambda i, me_ref: (i, 0))
    grid_spec = pltpu.PrefetchScalarGridSpec(
        num_scalar_prefetch=1, grid=(r // tr,),
        in_specs=[tile, pl.BlockSpec((NDEV, tr, c), lambda i, me_ref: (0, i, 0)), tile, tile, tile],
        out_specs=[tile] * 4)
    return pl.pallas_call(
        body, name=name, grid_spec=grid_spec,
        out_shape=[jax.ShapeDtypeStruct((r, c), F32)] * 4,
        compiler_params=_params(("parallel",)),
    )(sel, own, recv, w, m, v)


def _adamw_call(name, w, g, m, v):
    r, c = w.shape
    tr = _row_tile(r)

    def body(w_ref, g_ref, m_ref, v_ref, d_ref, nm_ref, nv_ref):
        d_ref[...], nm_ref[...], nv_ref[...] = _adamw_math(w_ref[...], g_ref[...], m_ref[...], v_ref[...])

    tile = pl.BlockSpec((tr, c), lambda i: (i, 0))
    return pl.pallas_call(
        body, name=name, grid=(r // tr,), in_specs=[tile] * 4, out_specs=[tile] * 3,
        out_shape=[jax.ShapeDtypeStruct((r, c), F32)] * 3,
        compiler_params=_params(("parallel",)),
    )(w, g, m, v)


def _me():
    return lax.axis_index("x"), lax.axis_index("y"), lax.axis_index("c")


def _flip(pos, r):
    x, y, c = pos
    return (1 - x if r & 4 else x, 1 - y if r & 2 else y, 1 - c if r & 1 else c)


def _lin(pos):
    return pos[0] * 4 + pos[1] * 2 + pos[2]


class _SmallAllReduce:
    def __init__(self, g_ref, out_ref, rbuf, send1, recv1, send2, recv2):
        self.g, self.out, self.rbuf = g_ref, out_ref, rbuf
        self.sems = (send1, recv1, send2, recv2)
        self.me = _me()
        self.mi = _lin(self.me)

    @staticmethod
    def _rows(d):
        return pl.ds(pl.multiple_of(d * SMALL_PER, 8), SMALL_PER)

    def _scatter(self, r, outgoing):
        peer = _flip(self.me, r)
        src, dst = (_lin(peer), self.mi) if outgoing else (self.mi, _lin(peer))
        return pltpu.make_async_remote_copy(
            src_ref=self.g.at[self._rows(src)], dst_ref=self.rbuf.at[dst],
            send_sem=self.sems[0].at[r - 1], recv_sem=self.sems[1].at[r - 1], device_id=peer, device_id_type=MESH)

    def _spread(self, r, outgoing):
        peer = _flip(self.me, r)
        rows = self._rows(self.mi if outgoing else _lin(peer))
        return pltpu.make_async_remote_copy(
            src_ref=self.out.at[rows], dst_ref=self.out.at[rows],
            send_sem=self.sems[2].at[r - 1], recv_sem=self.sems[3].at[r - 1], device_id=peer, device_id_type=MESH)

    def scatter(self):
        for r in range(1, NDEV):
            self._scatter(r, True).start()
        self.rbuf[self.mi] = self.g[self._rows(self.mi), :]

    def reduce(self):
        for r in range(1, NDEV):
            self._scatter(r, False).wait_recv()
        for r in range(1, NDEV):
            self._scatter(r, True).wait_send()
        tot = self.rbuf[0]
        for d in range(1, NDEV):
            tot = tot + self.rbuf[d]
        self.out[self._rows(self.mi), :] = tot
        for r in range(1, NDEV):
            self._spread(r, True).start()

    def finish(self):
        for r in range(1, NDEV):
            self._spread(r, False).wait_recv()
        for r in range(1, NDEV):
            self._spread(r, True).wait_send()


def _head_blocks(w):
    z = jnp.zeros((64, 64), w.dtype)
    groups = []
    for q in range(NQ):
        rows = [jnp.concatenate([w[4 * q + a] if a == b else z for b in range(4)], axis=1) for a in range(4)]
        groups.append(jnp.concatenate(rows, axis=0))
    return jnp.stack(groups)


def _head_unblocks(g):
    return jnp.stack([g[q, 64 * a:64 * a + 64, 64 * a:64 * a + 64] for q in range(NQ) for a in range(4)])


def _local_step(x, tgt, p, me, order, order_dw):
    s = x.shape[0]
    vec = lambda a: a.reshape(1, D)
    gmix, gffn, gfin = vec(p["norm_mix_g"]), vec(p["norm_ffn_g"]), vec(p["norm_final_g"])
    cb, ba, bx, lam = vec(p["conv_b"]), vec(p["rg_ba"]), vec(p["rg_bx"]), vec(p["rg_lambda"])
    lng, lnb = vec(p["sgu_ln_g"]), vec(p["sgu_ln_b"])
    wax = jnp.concatenate([_head_blocks(p["rg_wa"]), _head_blocks(p["rg_wx"])], axis=2).astype(BF)
    tril = jnp.tril(jnp.ones((CHUNK, CHUNK), bool))
    ws = jnp.where(tril[None], p["sgu_ws"], 0.0)
    wtr = ws.astype(BF)
    wtrt = jnp.swapaxes(ws, 1, 2).astype(BF)
    bias = jnp.repeat(p["sgu_bs"].T, GW, axis=1)
    shard = {k: p[k].astype(BF) for k in _BIG}

    proj, h1, w_in, (wpa, wpb, wo, cw) = _proj_gather_call(
        x, gmix, shard["w_in"], order, [shard["w_proj_a"], shard["w_proj_b"], shard["w_out"], p["conv_w"]])
    wpa, wpb, wo = (t.reshape(D, D) for t in (wpa, wpb, wo))
    cw = jnp.swapaxes(cw, 0, 1).reshape(4, D)
    (h, pa, pb, x1, *gates, ya, yb, mg), (wgu, wdn) = _mixer_fwd_call(
        proj, x, cw, cb, ba, bx, lam, lng, lnb, wax, wtr, bias, wpa, wpb, wo, [shard["w_gate_up"], shard["w_down"]])
    wdn = wdn.reshape(NDEV // 2, -1, D)
    nb, _, wb = w_in.shape
    fb = wgu.shape[2]
    nh = wdn.shape[0]
    dx1, dx1b, h2, act, dgu, dx2b, facc = _ffn_call(x1, tgt, gffn, gfin, wgu, wdn)
    assert nb == NDEV and 2 * nh == NDEV
    own_gu, part_gu = _dw_plain_call("dw_gate_up", me, dgu, h2, _blk3d(fb), _rows2d(D), 1, fb, D, s)
    own_dn, part_dn = _dw_plain_call("dw_down", me, act, dx2b, _blk3d(fb), _rows2d(D), 2, fb, D, s)
    (dprojb, dya, dpa, dpb, bvec, dws, dbs), (recv_dn,) = _mixer_bwd_pre_call(
        dx1b, proj, pa, pb, lng, lnb, wtr, wtrt, bias, wpa, wpb, wo, [part_dn])
    own_pa, part_pa = _dw_plain_call("dw_proj_a", me, ya, dpa, _rows2d(D), _rows2d(D), NDEV, D, D, s)
    own_pb, part_pb = _dw_plain_call("dw_proj_b", me, yb, dpb, _rows2d(D), _rows2d(D), NDEV, D, D, s)
    own_wo, part_wo = _dw_plain_call("dw_out", me, mg, dx1b, _rows2d(D), _rows2d(D), NDEV, D, D, s)
    (dproj, dx, svec, dwax), (recv_gu, recv_pa, recv_pb, recv_wo) = _mixer_bwd_seq_call(
        dprojb, dya, proj, h, x, dx1, gates, gmix, w_in, cw, lam, wax, [part_gu, part_pa, part_pb, part_wo])
    small = {
        "norm_mix_g": svec[8], "norm_ffn_g": facc[0], "norm_final_g": facc[1],
        "conv_b": svec[0], "rg_ba": svec[1], "rg_bx": svec[2], "rg_lambda": svec[3],
        "sgu_ln_g": bvec[0], "sgu_ln_b": bvec[1],
        "rg_wa": _head_unblocks(dwax[:, :, 0:QW]), "rg_wx": _head_unblocks(dwax[:, :, QW:2 * QW]),
        "sgu_ws": dws, "sgu_bs": dbs[:, ::GW].T,
    }
    packed = _pack_small(small, svec[4:8], facc[2])
    own_in, recv_in, gsum = _dw_exchange_call("dw_in", order_dw, h1, dproj, _rows2d(D), _cols2d(wb), D, wb, s, packed)
    dw = {
        "w_gate_up": (own_gu, recv_gu), "w_down": (own_dn, recv_dn), "w_proj_a": (own_pa, recv_pa),
        "w_proj_b": (own_pb, recv_pb), "w_out": (own_wo, recv_wo), "w_in": (own_in, recv_in),
    }
    return gsum, dx, dw


_BIG = ("w_in", "w_gate_up", "w_down", "w_proj_a", "w_proj_b", "w_out")
_VEC_ROWS = ("norm_mix_g", "norm_ffn_g", "norm_final_g", "conv_b", "rg_ba", "rg_bx", "rg_lambda",
             "sgu_ln_g", "sgu_ln_b", "sgu_bs")
_WEIGHTS = ("norm_mix_g", "w_in", "conv_w", "conv_b", "rg_wa", "rg_ba", "rg_wx", "rg_bx", "rg_lambda",
            "sgu_ln_g", "sgu_ln_b", "sgu_ws", "sgu_bs", "w_proj_a", "w_proj_b", "w_out", "norm_ffn_g",
            "w_gate_up", "w_down", "norm_final_g")


def _pack_small(t, conv_w, extra=None):
    extra = jnp.zeros((1, D), F32) if extra is None else extra.reshape(1, D)
    head = jnp.concatenate([t[k].reshape(1, D) for k in _VEC_ROWS] + [conv_w, extra, jnp.zeros((1, D), F32)], axis=0)
    return jnp.concatenate([head, t["rg_wa"].reshape(64, D), t["rg_wx"].reshape(64, D), t["sgu_ws"].reshape(128, D),
                            jnp.zeros((SMALL_ROWS - 272, D), F32)], axis=0)


def _unpack_small(a):
    out = {k: a[j] for j, k in enumerate(_VEC_ROWS)}
    out["conv_w"] = a[10:14]
    out["rg_wa"] = a[16:80].reshape(16, 64, 64)
    out["rg_wx"] = a[80:144].reshape(16, 64, 64)
    out["sgu_ws"] = a[144:272].reshape(NGRP, CHUNK, CHUNK)
    return out


def kernel(x, norm_mix_g, w_in, conv_w, conv_b, rg_wa, rg_ba, rg_wx, rg_bx, rg_lambda, sgu_ln_g, sgu_ln_b, sgu_ws, sgu_bs, w_proj_a, w_proj_b, w_out, norm_ffn_g, w_gate_up, w_down, norm_final_g, loss_target, m_norm_mix_g, m_w_in, m_conv_w, m_conv_b, m_rg_wa, m_rg_ba, m_rg_wx, m_rg_bx, m_rg_lambda, m_sgu_ln_g, m_sgu_ln_b, m_sgu_ws, m_sgu_bs, m_w_proj_a, m_w_proj_b, m_w_out, m_norm_ffn_g, m_w_gate_up, m_w_down, m_norm_final_g, v_norm_mix_g, v_w_in, v_conv_w, v_conv_b, v_rg_wa, v_rg_ba, v_rg_wx, v_rg_bx, v_rg_lambda, v_sgu_ln_g, v_sgu_ln_b, v_sgu_ws, v_sgu_bs, v_w_proj_a, v_w_proj_b, v_w_out, v_norm_ffn_g, v_w_gate_up, v_w_down, v_norm_final_g):
    args = dict(locals())
    w = {k: args[k] for k in _WEIGHTS}
    m = {k: args["m_" + k] for k in _WEIGHTS}
    v = {k: args["v_" + k] for k in _WEIGHTS}
    for d in (w, m, v):
        for k in _WEIGHTS:
            if k != "norm_final_g":
                d[k] = d[k][0]
    me = _lin(_me())
    me1 = me.reshape(1).astype(jnp.int32)

    order = jnp.bitwise_xor(me, jnp.array(_PASS_FLIPS, jnp.int32)).astype(jnp.int32)
    order_dw = jnp.bitwise_xor(me, jnp.array(_DW_FLIPS, jnp.int32)).astype(jnp.int32)

    gsum, dx, dw = _local_step(x[0], loss_target[0], w, me1, order, order_dw)

    peer = jnp.bitwise_xor(jnp.arange(NDEV, dtype=jnp.int32), me)
    sel_direct = jnp.where(peer == 0, 2, 1).astype(jnp.int32)
    sel_two_level = jnp.where(peer == 0, 2, jnp.where((peer == 1) | (peer % 2 == 0), 1, 0)).astype(jnp.int32)
    grads, delta, new_m, new_v = {}, {}, {}, {}
    for k in _BIG:
        own, recv = dw[k]
        flip = own.shape != w[k].shape
        wmv = [jnp.swapaxes(t, 0, 1) if flip else t for t in (w[k], m[k], v[k])]
        res = _reduce_adamw_call("adamw_" + k, sel_two_level if k == "w_in" else sel_direct, own, recv, *wmv)
        grads[k], delta[k], new_m[k], new_v[k] = (jnp.swapaxes(t, 0, 1) if flip else t for t in res)

    loss = (0.5 / D) * jnp.sum(gsum[14])
    zc = jnp.zeros((4, D), F32)
    d_s, m_s, v_s = _adamw_call("adamw_small", _pack_small(w, zc), gsum, _pack_small(m, zc), _pack_small(v, zc))
    gs, ds, ms, vs = _unpack_small(gsum), _unpack_small(d_s), _unpack_small(m_s), _unpack_small(v_s)
    g_cw = lax.dynamic_slice(gs["conv_w"], (0, me * 128), (4, 128))
    ds["conv_w"], ms["conv_w"], vs["conv_w"] = _adamw_call("adamw_conv_w", w["conv_w"], g_cw, m["conv_w"], v["conv_w"])
    gs["conv_w"] = g_cw
    for k in _WEIGHTS:
        if k not in _BIG:
            shp = w[k].shape
            grads[k], delta[k], new_m[k], new_v[k] = (t[k].reshape(shp) for t in (gs, ds, ms, vs))

    def lift(t, k):
        return t[k] if k == "norm_final_g" else t[k][None]

    outs = [loss, dx[None]]
    for t in (grads, delta, new_m, new_v):
        outs += [lift(t, k) for k in _WEIGHTS]
    return tuple(outs)
```

```python
import functools

import jax
import jax.numpy as jnp
from jax import lax
from jax.experimental import pallas as pl
from jax.experimental.pallas import tpu as pltpu

F32 = jnp.float32
BF = jnp.bfloat16

D = 1024
NDEV = 8
EPS = 1e-6
RG_C = 8.0
CHUNK = 128
NGRP = 8
GW = 128
NQ = 4
QW = 256
RC = 16
SMALL_ROWS = 320
SMALL_PER = SMALL_ROWS // NDEV

ADAM_LR = 0.001
ADAM_B1 = 0.9
ADAM_B2 = 0.999
ADAM_EPS = 1e-08
ADAM_WD = 0.01
ADAM_STEP = 10

VMEM_LIMIT = 60 * 1024 * 1024

MESH = pl.DeviceIdType.MESH


def _rows(n, fn, unroll=2, rc=RC):
    def body(i, c):
        fn(pl.multiple_of(i * rc, rc))
        return c
    lax.fori_loop(0, n // rc, body, 0, unroll=unroll)


def _fold(v):
    return jnp.sum(v.reshape(v.shape[0] // RC, RC, v.shape[1]), axis=0)


def _dot(a, b):
    return jnp.dot(a, b, preferred_element_type=F32)


def _dot_nt(a, b):
    return lax.dot_general(a, b, (((1,), (1,)), ((), ())), preferred_element_type=F32)


def _dot_tn(a, b):
    return lax.dot_general(a, b, (((0,), (0,)), ((), ())), preferred_element_type=F32)


_GC = 0.7978845608028654
_GK = 0.044715


def _gelu(x):
    t = jnp.tanh(_GC * (x + _GK * (x * x * x)))
    return x * (0.5 * (1.0 + t))


def _gelu_grad(x):
    x2 = x * x
    t = jnp.tanh(_GC * (x + _GK * (x2 * x)))
    cdf = 0.5 * (1.0 + t)
    dg = cdf + (0.5 * x) * (1.0 - t * t) * (_GC * (1.0 + (3.0 * _GK) * x2))
    return x * cdf, dg


def _sigmoid(x):
    return jax.nn.sigmoid(x)


def _log1p(e):
    u = 1.0 + e
    d = u - 1.0
    return jnp.where(d == 0.0, e, jnp.log(u) * (e / jnp.where(d == 0.0, 1.0, d)))


def _softplus(z):
    return jnp.maximum(z, 0.0) + _log1p(jnp.exp(-jnp.abs(z)))


def _neg_expm1(z):
    u = jnp.exp(z)
    lu = jnp.log(u)
    k = (1.0 - u) * (z / jnp.where(lu == 0.0, 1.0, lu))
    small = jnp.where(lu == 0.0, -z, k)
    return jnp.where(z > -0.5, small, 1.0 - u)


def _shift_back(prev8, cur, j):
    cat = jnp.concatenate([prev8, cur], axis=0)
    return pltpu.roll(cat, j, 0)[8:8 + cur.shape[0]]


def _shift_fwd(cur, next8, j):
    cat = jnp.concatenate([cur, next8], axis=0)
    n = cat.shape[0]
    return pltpu.roll(cat, n - j, 0)[0:cur.shape[0]]


def _const_spec(shape):
    nd = len(shape)
    return pl.BlockSpec(shape, lambda *_: (0,) * nd, pipeline_mode=pl.Buffered(1))


def _params(sem):
    return pltpu.CompilerParams(dimension_semantics=sem, vmem_limit_bytes=VMEM_LIMIT)


TM_PROJ = 1024
TM_MIX = 256
TM_FFN = 256
TM_DX = 512
TS_DW = 4096


_CHIPS = (4, 2, 6)
_PASS_FLIPS = (0, 1, 4, 2, 6, 5, 3, 7)


class _Gather:
    def __init__(self, ins, outs, send_sems, recv_sems, local_sems):
        self.ins, self.outs = ins, outs
        self.send_sems, self.recv_sems, self.local_sems = send_sems, recv_sems, local_sems
        self.me = _me()
        self.sibling = _flip(self.me, 1)

    def _copy(self, a, kind, block, to, src=None):
        dst = self.outs[a].at[_lin(block)]
        return pltpu.make_async_remote_copy(
            src_ref=dst if src is None else src, dst_ref=dst,
            send_sem=self.send_sems.at[a, kind], recv_sem=self.recv_sems.at[a, kind],
            device_id=to, device_id_type=MESH)

    def _local(self, a):
        return pltpu.make_async_copy(self.ins[a], self.outs[a].at[_lin(self.me)], self.local_sems.at[a])

    def start(self):
        for a in range(len(self.ins)):
            self._local(a).start()
            self._copy(a, 0, self.me, self.sibling, src=self.ins[a]).start()
            for j, f in enumerate(_CHIPS):
                self._copy(a, 1 + j, self.me, _flip(self.me, f), src=self.ins[a]).start()

    def forward(self):
        for j, f in enumerate(_CHIPS):
            for a in range(len(self.ins)):
                self._copy(a, 1 + j, _flip(self.me, f), self.me).wait_recv()
                self._copy(a, 4 + j, _flip(self.me, f), self.sibling).start()

    def finish(self):
        for a in range(len(self.ins)):
            self._copy(a, 0, self.sibling, self.me).wait_recv()
            for j, f in enumerate(_CHIPS):
                self._copy(a, 4 + j, _flip(self.me, f | 1), self.me).wait_recv()
            self._copy(a, 0, self.me, self.sibling, src=self.ins[a]).wait_send()
            for j, f in enumerate(_CHIPS):
                self._copy(a, 1 + j, self.me, _flip(self.me, f), src=self.ins[a]).wait_send()
                self._copy(a, 4 + j, _flip(self.me, f), self.sibling).wait_send()
            self._local(a).wait()


class _Exchange:
    def __init__(self, ins, outs, send_sems, recv_sems, local_sems):
        self.ins, self.outs = ins, outs
        self.send_sems, self.recv_sems, self.local_sems = send_sems, recv_sems, local_sems
        self.me = _me()

    def _copy(self, a, r, outgoing):
        peer = _flip(self.me, r)
        src, dst = (peer, self.me) if outgoing else (self.me, peer)
        return pltpu.make_async_remote_copy(
            src_ref=self.ins[a].at[_lin(src)], dst_ref=self.outs[a].at[_lin(dst)],
            send_sem=self.send_sems.at[a, r - 1], recv_sem=self.recv_sems.at[a, r - 1],
            device_id=peer, device_id_type=MESH)

    def _local(self, a):
        mi = _lin(self.me)
        return pltpu.make_async_copy(self.ins[a].at[mi], self.outs[a].at[mi], self.local_sems.at[a])

    def start(self):
        for a in range(len(self.ins)):
            self._local(a).start()
        for r in range(1, NDEV):
            for a in range(len(self.ins)):
                self._copy(a, r, True).start()

    def finish(self):
        for r in range(1, NDEV):
            for a in range(len(self.ins)):
                self._copy(a, r, False).wait_recv()
        for r in range(1, NDEV):
            for a in range(len(self.ins)):
                self._copy(a, r, True).wait_send()
        for a in range(len(self.ins)):
            self._local(a).wait()


def _gather_sems(n):
    return [pltpu.SemaphoreType.DMA((n, 7)), pltpu.SemaphoreType.DMA((n, 7)), pltpu.SemaphoreType.DMA((n,))]


def _proj_gather_call(x, gmix, w_shard, order, extras):
    s = x.shape[0]
    tm = min(TM_PROJ, s)
    nt = s // tm
    wb = w_shard.shape[1]
    n = len(extras)

    def body(order_ref, x_ref, g_ref, wsh_ref, *rest):
        ex_in = rest[:n]
        proj_hbm, h1_hbm, wall_hbm = rest[n:n + 3]
        ex_out = rest[n + 3:2 * n + 3]
        wall, hc, pbuf, ws_send, ws_recv, ex_send, ex_recv, ex_local, out_sems, psems = rest[2 * n + 3:]
        k = pl.program_id(0)
        i = pl.program_id(1)
        me = _me()
        sibling = _flip(me, 1)
        gather = _Gather(ex_in, ex_out, ex_send, ex_recv, ex_local)

        def wcopy(kind, block, to):
            ref = wall.at[_lin(block)]
            return pltpu.make_async_remote_copy(
                src_ref=ref, dst_ref=ref, send_sem=ws_send.at[kind], recv_sem=ws_recv.at[kind],
                device_id=to, device_id_type=MESH)

        head = i == 0

        @pl.when(head & (k == 0))
        def _():
            wall[_lin(me)] = wsh_ref[...]
            wcopy(0, me, sibling).start()
            for j, f in enumerate(_CHIPS):
                wcopy(1 + j, me, _flip(me, f)).start()
            gather.start()

        @pl.when(head & (k == 1))
        def _():
            wcopy(0, sibling, me).wait_recv()

        for j, f in enumerate(_CHIPS):
            @pl.when(head & (k == 2 + j))
            def _(j=j, f=f):
                wcopy(1 + j, _flip(me, f), me).wait_recv()
                wcopy(4 + j, _flip(me, f), sibling).start()

            @pl.when(head & (k == 5 + j))
            def _(j=j, f=f):
                wcopy(4 + j, _flip(me, f | 1), me).wait_recv()

        pl.when(head & (k == 5))(gather.forward)

        base = pl.multiple_of(i * tm, tm)

        @pl.when(k == 0)
        def _():
            g = g_ref[...]

            def norm(r0):
                xx = x_ref[pl.ds(r0, RC), :]
                r = lax.rsqrt(jnp.mean(xx * xx, axis=-1, keepdims=True) + EPS)
                hc[pl.ds(base + r0, RC), :] = ((xx * r) * g).astype(BF)
            _rows(tm, norm, unroll=4)

        blk = order_ref[k]
        wk = wall.at[blk]
        step = k * nt + i
        slot = lax.rem(step, 2)
        subs = [(r0, min(TM_MIX, tm - r0)) for r0 in range(0, tm, TM_MIX)]

        def out_copy(q, r0, rs):
            return pltpu.make_async_copy(
                pbuf.at[slot, pl.ds(r0, rs)],
                proj_hbm.at[pl.ds(base + r0, rs), pl.ds(pl.multiple_of(blk * wb, 128), wb)], psems.at[slot, q])

        @pl.when(step >= 2)
        def _():
            for q, (r0, rs) in enumerate(subs):
                out_copy(q, r0, rs).wait()

        for q, (r0, rs) in enumerate(subs):
            pbuf[slot, r0:r0 + rs, :] = _dot(hc[pl.ds(base + r0, rs), :], wk[...])
            out_copy(q, r0, rs).start()

        @pl.when((k == NDEV - 1) & (i == nt - 1))
        def _():
            out_w = pltpu.make_async_copy(wall, wall_hbm, out_sems.at[0])
            out_h = pltpu.make_async_copy(hc, h1_hbm, out_sems.at[1])
            out_w.start()
            out_h.start()
            wcopy(0, me, sibling).wait_send()
            for j, f in enumerate(_CHIPS):
                wcopy(1 + j, me, _flip(me, f)).wait_send()
                wcopy(4 + j, _flip(me, f), sibling).wait_send()
            gather.finish()
            out_w.wait()
            out_h.wait()
            for q, (r0, rs) in enumerate(subs):
                out_copy(q, r0, rs).wait()
                pltpu.make_async_copy(pbuf.at[1 - slot, pl.ds(r0, rs)], proj_hbm.at[pl.ds(r0, rs), pl.ds(0, wb)],
                                      psems.at[1 - slot, q]).wait()

    any_spec = pl.BlockSpec(memory_space=pl.ANY)
    grid_spec = pltpu.PrefetchScalarGridSpec(
        num_scalar_prefetch=1, grid=(NDEV, nt),
        in_specs=[pl.BlockSpec((tm, D), lambda k, i, o: (jnp.where(k == 0, i, nt - 1), 0)),
                  pl.BlockSpec((1, D), lambda k, i, o: (0, 0)),
                  pl.BlockSpec(w_shard.shape, lambda k, i, o: (0, 0))] + [any_spec] * n,
        out_specs=[any_spec, any_spec, any_spec] + [any_spec] * n,
        scratch_shapes=[pltpu.VMEM((NDEV,) + w_shard.shape, BF), pltpu.VMEM((s, D), BF), pltpu.VMEM((2, tm, wb), F32),
                        pltpu.SemaphoreType.DMA((7,)), pltpu.SemaphoreType.DMA((7,))] + _gather_sems(n)
        + [pltpu.SemaphoreType.DMA((2,)), pltpu.SemaphoreType.DMA((2, -(-tm // TM_MIX)))])
    outs = pl.pallas_call(
        body, name="proj_gather", grid_spec=grid_spec,
        out_shape=[jax.ShapeDtypeStruct((s, NDEV * wb), F32), jax.ShapeDtypeStruct((s, D), BF),
                   jax.ShapeDtypeStruct((NDEV,) + w_shard.shape, BF)]
        + [jax.ShapeDtypeStruct((NDEV,) + e.shape, e.dtype) for e in extras],
        compiler_params=_params(("arbitrary", "arbitrary")),
    )(order, x, gmix, w_shard, *extras)
    return outs[0], outs[1], outs[2], outs[3:]


def _conv_tile(rx, prev8, cw_ref, cb):
    xc = cb + cw_ref[3:4, :] * rx
    for j in (1, 2, 3):
        xc = xc + cw_ref[3 - j:4 - j, :] * _shift_back(prev8, rx, j)
    return xc


def _gate_tile(xcb, wax_ref, ba, bx, sp, q):
    cs = slice(q * QW, (q + 1) * QW)
    z = _dot(xcb[:, cs], wax_ref[q])
    r = _sigmoid(z[:, 0:QW] + ba[:, cs])
    ig = _sigmoid(z[:, QW:2 * QW] + bx[:, cs])
    return r, ig, (-RG_C * r) * sp[:, cs]


def _scan_fwd(a_s, b_s, out_ref, h0, n):
    rowi = lax.broadcasted_iota(jnp.int32, (8, D), 0)

    def block(t, h):
        rows = pl.ds(t * 8, 8)
        a = a_s[rows, :]
        b = b_s[rows, :]
        for d in (1, 2, 4):
            m = rowi >= d
            b = jnp.where(m, a * pltpu.roll(b, d, 0) + b, b)
            a = jnp.where(m, a * pltpu.roll(a, d, 0), a)
        hb = b + a * h
        out_ref[rows, :] = hb
        return hb[7:8, :]
    h = h0
    for t in range(n // 8):
        h = block(t, h)
    return h


def _scan_bwd(a_s, lm_s, c0, n):
    rowi = lax.broadcasted_iota(jnp.int32, (8, D), 0)
    nblk = n // 8

    def block(k, cin):
        rows = pl.ds((nblk - 1 - k) * 8, 8)
        a = a_s[rows, :]
        dh = lm_s[rows, :]
        b = a * dh
        for d in (1, 2, 4):
            m = rowi < 8 - d
            b = jnp.where(m, a * pltpu.roll(b, 8 - d, 0) + b, b)
            a = jnp.where(m, a * pltpu.roll(a, 8 - d, 0), a)
        mu = b + a * cin
        lm_s[rows, :] = dh + jnp.where(rowi < 7, pltpu.roll(mu, 7, 0), cin)
        return mu[0:1, :]
    c = c0
    for k in range(nblk):
        c = block(k, c)
    return c


def _mixer_fwd_call(proj, x, cw, cb, ba, bx, lam, lng, lnb, wax, wtr, bias, wpa, wpb, wo, shards):
    s = x.shape[0]
    tm = min(TM_MIX, s)
    nt = s // tm
    pw = proj.shape[1]
    n = len(shards)

    def body(proj_ref, x_ref, cw_ref, cb_ref, ba_ref, bx_ref, lam_ref, lng_ref, lnb_ref, wax_ref, wtr_ref,
             bias_ref, wpa_ref, wpb_ref, wo_ref, *rest):
        sh_in = rest[:n]
        h_ref, pa_ref, pb_ref, x1_ref, xc_ref, r_ref, ig_ref, a_s, m_ref, ya_ref, yb_ref, mg_ref = rest[n:n + 12]
        sh_out = rest[n + 12:2 * n + 12]
        prev_s, b_s, hc_s, vn_s, mx_s, g_send, g_recv, g_local = rest[2 * n + 12:]
        i = pl.program_id(0)
        gather = _Gather(sh_in, sh_out, g_send, g_recv, g_local)

        @pl.when(i == 0)
        def _():
            gather.start()
            prev_s[...] = jnp.zeros((8, D), F32)
            hc_s[...] = jnp.zeros((8, D), F32)

        pl.when(i == nt // 2)(gather.forward)

        rx = proj_ref[:, 0:D]
        xc = _conv_tile(rx, prev_s[...], cw_ref, cb_ref[...])
        prev_s[...] = rx[tm - 8:tm, :]
        xc_ref[...] = xc.astype(BF)
        xcb = xc.astype(BF)
        sp = _softplus(-lam_ref[...])
        ba = ba_ref[...]
        bx = bx_ref[...]
        for q in range(NQ):
            cs = slice(q * QW, (q + 1) * QW)
            r, ig, la = _gate_tile(xcb, wax_ref, ba, bx, sp, q)
            mq = jnp.sqrt(_neg_expm1(2.0 * la))
            r_ref[:, cs] = r.astype(BF)
            ig_ref[:, cs] = ig.astype(BF)
            m_ref[:, cs] = mq.astype(BF)
            a_s[:, cs] = jnp.exp(la)
            b_s[:, cs] = mq * (ig * xc[:, cs])

        gv = _gelu(proj_ref[:, 3 * D:4 * D])
        dv = gv - jnp.mean(gv, axis=-1, keepdims=True)
        var = jnp.mean(dv * dv, axis=-1, keepdims=True)
        vn_s[...] = ((dv * lax.rsqrt(var + EPS)) * lng_ref[...] + lnb_ref[...]).astype(BF)
        nc = tm // CHUNK
        for c in range(nc):
            rs = slice(c * CHUNK, (c + 1) * CHUNK)
            for g in range(NGRP):
                cs = slice(g * GW, (g + 1) * GW)
                mx_s[rs, cs] = _dot(wtr_ref[g], vn_s[rs, cs])
        mixed = mx_s[...] + jnp.concatenate([bias_ref[...]] * nc, axis=0)
        yb = (_gelu(proj_ref[:, 2 * D:3 * D]) * mixed).astype(BF)
        yb_ref[...] = yb
        pb = _dot(yb, wpb_ref[...])
        pb_ref[...] = pb.astype(BF)
        mx_s[...] = pb

        hc_s[0:1, :] = _scan_fwd(a_s, b_s, h_ref, hc_s[0:1, :], tm)

        ya = (_gelu(proj_ref[:, D:2 * D]) * h_ref[...]).astype(BF)
        ya_ref[...] = ya
        pa = _dot(ya, wpa_ref[...])
        pa_ref[...] = pa.astype(BF)
        mg = (_sigmoid(proj_ref[:, 4 * D:5 * D]) * pa + _sigmoid(proj_ref[:, 5 * D:6 * D]) * mx_s[...]).astype(BF)
        mg_ref[...] = mg
        x1_ref[...] = x_ref[...] + _dot(mg, wo_ref[...])

        pl.when(i == nt - 1)(gather.finish)

    tile = lambda w: pl.BlockSpec((tm, w), lambda i: (i, 0))
    vec = _const_spec((1, D))
    any_spec = pl.BlockSpec(memory_space=pl.ANY)
    outs = pl.pallas_call(
        body, name="mixer_fwd", grid=(nt,),
        in_specs=[tile(pw), tile(D), _const_spec((4, D)), vec, vec, vec, vec, vec, vec,
                  _const_spec(wax.shape), _const_spec(wtr.shape), _const_spec(bias.shape),
                  _const_spec((D, D)), _const_spec((D, D)), _const_spec((D, D))] + [any_spec] * n,
        out_specs=[tile(D)] * 12 + [any_spec] * n,
        out_shape=[jax.ShapeDtypeStruct((s, D), dt) for dt in (F32, BF, BF, F32, BF, BF, BF, F32, BF, BF, BF, BF)]
        + [jax.ShapeDtypeStruct((NDEV,) + e.shape, e.dtype) for e in shards],
        scratch_shapes=[pltpu.VMEM((8, D), F32), pltpu.VMEM((tm, D), F32),
                        pltpu.VMEM((8, D), F32), pltpu.VMEM((tm, D), BF), pltpu.VMEM((tm, D), F32)]
        + _gather_sems(n),
        compiler_params=_params(("arbitrary",)),
    )(proj, x, cw, cb, ba, bx, lam, lng, lnb, wax, wtr, bias, wpa, wpb, wo, *shards)
    return outs[:12], outs[12:]


def _ffn_call(x1, tgt, gffn, gfin, wgu, wdn):
    s = x1.shape[0]
    tm = min(TM_FFN, s)
    nt = s // tm
    nh = wdn.shape[0]
    fb = wgu.shape[2]

    def body(x1_ref, tgt_ref, gffn_ref, gfin_ref, wgu_ref, wdn_ref,
             dx1_ref, dx1b_ref, h2_ref, act_ref, dgu_ref, dx2b_ref, acc_ref,
             g_s, u_s, dx2_s, accs):
        i = pl.program_id(0)

        @pl.when(i == 0)
        def _():
            accs[...] = jnp.zeros(accs.shape, F32)

        gffn = gffn_ref[...]
        gfin = gfin_ref[...]

        x1 = x1_ref[...]
        r2 = lax.rsqrt(jnp.mean(x1 * x1, axis=-1, keepdims=True) + EPS)
        xh2 = x1 * r2
        h2 = (xh2 * gffn).astype(BF)
        h2_ref[...] = h2

        for k in range(nh):
            g = _dot(h2, wgu_ref[k])
            u = _dot(h2, wgu_ref[k + nh])
            g_s[k] = g
            u_s[k] = u
            act_ref[k] = ((g * _sigmoid(g)) * u).astype(BF)

        x2 = x1
        for k in range(nh):
            x2 = x2 + _dot(act_ref[k], wdn_ref[k])

        r3 = lax.rsqrt(jnp.mean(x2 * x2, axis=-1, keepdims=True) + EPS)
        xh = x2 * r3
        err = xh * gfin - tgt_ref[...]
        accs[2] += _fold(err * err)
        dy = err * (1.0 / D)
        accs[1] += _fold(dy * xh)
        dxh = dy * gfin
        dx2 = r3 * (dxh - xh * jnp.mean(dxh * xh, axis=-1, keepdims=True))
        dx2_s[...] = dx2
        dx2b = dx2.astype(BF)
        dx2b_ref[...] = dx2b

        for k in range(nh):
            da = _dot_nt(dx2b, wdn_ref[k])
            g = g_s[k]
            sg = _sigmoid(g)
            dgu_ref[k] = ((da * u_s[k]) * (sg * (1.0 + g * (1.0 - sg)))).astype(BF)
            dgu_ref[k + nh] = (da * (g * sg)).astype(BF)

        dh2 = _dot_nt(dgu_ref[0], wgu_ref[0])
        for k in range(1, 2 * nh):
            dh2 = dh2 + _dot_nt(dgu_ref[k], wgu_ref[k])

        accs[0] += _fold(dh2 * xh2)
        dxh = dh2 * gffn
        dx1 = dx2_s[...] + r2 * (dxh - xh2 * jnp.mean(dxh * xh2, axis=-1, keepdims=True))
        dx1_ref[...] = dx1
        dx1b_ref[...] = dx1.astype(BF)

        @pl.when(i == nt - 1)
        def _():
            acc_ref[...] = jnp.zeros((8, D), F32)
            for j in range(3):
                acc_ref[j:j + 1, :] = jnp.sum(accs[j], axis=0, keepdims=True)

    tile = lambda w: pl.BlockSpec((tm, w), lambda i: (i, 0))
    vec = _const_spec((1, D))
    return pl.pallas_call(
        body, name="ffn", grid=(nt,),
        in_specs=[tile(D), tile(D), vec, vec, _const_spec(wgu.shape), _const_spec(wdn.shape)],
        out_specs=[tile(D), tile(D), tile(D),
                   pl.BlockSpec((nh, tm, fb), lambda i: (0, i, 0)),
                   pl.BlockSpec((2 * nh, tm, fb), lambda i: (0, i, 0)),
                   tile(D), pl.BlockSpec((8, D), lambda i: (0, 0))],
        out_shape=[jax.ShapeDtypeStruct((s, D), F32), jax.ShapeDtypeStruct((s, D), BF),
                   jax.ShapeDtypeStruct((s, D), BF), jax.ShapeDtypeStruct((nh, s, fb), BF),
                   jax.ShapeDtypeStruct((2 * nh, s, fb), BF), jax.ShapeDtypeStruct((s, D), BF),
                   jax.ShapeDtypeStruct((8, D), F32)],
        scratch_shapes=[pltpu.VMEM((nh, tm, fb), F32), pltpu.VMEM((nh, tm, fb), F32), pltpu.VMEM((tm, D), F32),
                        pltpu.VMEM((3, RC, D), F32)],
        compiler_params=_params(("arbitrary",)),
    )(x1, tgt, gffn, gfin, wgu, wdn)


def _mixer_bwd_pre_call(dx1b, proj, pa, pb, lng, lnb, wtr, wtrt, bias, wpa, wpb, wo, parts):
    s = dx1b.shape[0]
    tm = min(TM_MIX, s)
    nt = s // tm
    pw = proj.shape[1]
    n = len(parts)

    def body(dx1b_ref, uv_ref, gg_ref, pa_ref, pb_ref, lng_ref, lnb_ref, wtr_ref, wtrt_ref, bias_ref,
             wpa_ref, wpb_ref, wo_ref, *rest):
        ex_in = rest[:n]
        dproj_ref, dya_ref, dpa_ref, dpb_ref, vec_ref, dws_ref, dbs_ref = rest[n:n + 7]
        ex_out = rest[n + 7:2 * n + 7]
        vn_s, mx_s, dmx_s, dvn_s, accs, dbs_s, e_send, e_recv, e_local = rest[2 * n + 7:]
        i = pl.program_id(0)
        exchange = _Exchange(ex_in, ex_out, e_send, e_recv, e_local)

        @pl.when(i == 0)
        def _():
            exchange.start()
            accs[...] = jnp.zeros(accs.shape, F32)
            dbs_s[...] = jnp.zeros(dbs_s.shape, F32)
            dws_ref[...] = jnp.zeros(dws_ref.shape, F32)

        dm = _dot_nt(dx1b_ref[...], wo_ref[...])
        sa = _sigmoid(gg_ref[:, 0:D])
        sb = _sigmoid(gg_ref[:, D:2 * D])
        dpa = dm * sa
        dpb = dm * sb
        dpab = dpa.astype(BF)
        dpbb = dpb.astype(BF)
        dpa_ref[...] = dpab
        dpb_ref[...] = dpbb
        dproj_ref[:, 2 * D:3 * D] = ((dpa * pa_ref[...].astype(F32)) * (1.0 - sa)).astype(BF)
        dproj_ref[:, 3 * D:4 * D] = ((dpb * pb_ref[...].astype(F32)) * (1.0 - sb)).astype(BF)

        dya_ref[...] = _dot_nt(dpab, wpa_ref[...])
        dyb = _dot_nt(dpbb, wpb_ref[...])

        lng = lng_ref[...]
        gv, dgelu_v = _gelu_grad(uv_ref[:, D:2 * D])
        dv = gv - jnp.mean(gv, axis=-1, keepdims=True)
        rstd = lax.rsqrt(jnp.mean(dv * dv, axis=-1, keepdims=True) + EPS)
        xh = dv * rstd
        vn_s[...] = (xh * lng + lnb_ref[...]).astype(BF)

        nc = tm // CHUNK
        for c in range(nc):
            rs = slice(c * CHUNK, (c + 1) * CHUNK)
            for g in range(NGRP):
                cs = slice(g * GW, (g + 1) * GW)
                mx_s[rs, cs] = _dot(wtr_ref[g], vn_s[rs, cs])

        gu, dgelu_u = _gelu_grad(uv_ref[:, 0:D])
        mixed = mx_s[...] + jnp.concatenate([bias_ref[...]] * nc, axis=0)
        dproj_ref[:, 0:D] = ((dyb * mixed) * dgelu_u).astype(BF)
        dmx = dyb * gu
        dmx_s[...] = dmx.astype(BF)
        dbs_s[...] += jnp.sum(dmx.reshape(nc, CHUNK, D), axis=0)

        for c in range(nc):
            rs = slice(c * CHUNK, (c + 1) * CHUNK)
            for g in range(NGRP):
                cs = slice(g * GW, (g + 1) * GW)
                dvn_s[rs, cs] = _dot(wtrt_ref[g], dmx_s[rs, cs])
                dws_ref[g] += _dot_nt(dmx_s[rs, cs], vn_s[rs, cs])

        dvn = dvn_s[...]
        accs[0] += _fold(dvn * xh)
        accs[1] += _fold(dvn)
        dxh = dvn * lng
        m1 = jnp.mean(dxh, axis=-1, keepdims=True)
        m2 = jnp.mean(dxh * xh, axis=-1, keepdims=True)
        dproj_ref[:, D:2 * D] = ((rstd * (dxh - m1 - xh * m2)) * dgelu_v).astype(BF)

        @pl.when(i == nt - 1)
        def _():
            vec_ref[...] = jnp.zeros((8, D), F32)
            for j in range(2):
                vec_ref[j:j + 1, :] = jnp.sum(accs[j], axis=0, keepdims=True)
            row = lax.broadcasted_iota(jnp.int32, (CHUNK, CHUNK), 0)
            col = lax.broadcasted_iota(jnp.int32, (CHUNK, CHUNK), 1)
            for g in range(NGRP):
                dws_ref[g] = jnp.where(row >= col, dws_ref[g], 0.0)
                gs = jnp.sum(dbs_s[:, g * GW:(g + 1) * GW], axis=1, keepdims=True)
                dbs_ref[:, g * GW:(g + 1) * GW] = jnp.broadcast_to(gs, (CHUNK, GW))
            exchange.finish()

    tile = lambda w: pl.BlockSpec((tm, w), lambda i: (i, 0))
    vec = _const_spec((1, D))
    any_spec = pl.BlockSpec(memory_space=pl.ANY)
    outs = pl.pallas_call(
        body, name="mixer_bwd_pre", grid=(nt,),
        in_specs=[tile(D), pl.BlockSpec((tm, 2 * D), lambda i: (i, 1)), pl.BlockSpec((tm, 2 * D), lambda i: (i, 2)),
                  tile(D), tile(D), vec, vec, _const_spec(wtr.shape), _const_spec(wtrt.shape),
                  _const_spec(bias.shape), _const_spec((D, D)), _const_spec((D, D)), _const_spec((D, D))]
        + [any_spec] * n,
        out_specs=[tile(4 * D), tile(D), tile(D), tile(D), pl.BlockSpec((8, D), lambda i: (0, 0)),
                   pl.BlockSpec((NGRP, CHUNK, CHUNK), lambda i: (0, 0, 0)),
                   pl.BlockSpec((CHUNK, D), lambda i: (0, 0))] + [any_spec] * n,
        out_shape=[jax.ShapeDtypeStruct((s, 4 * D), BF), jax.ShapeDtypeStruct((s, D), F32),
                   jax.ShapeDtypeStruct((s, D), BF), jax.ShapeDtypeStruct((s, D), BF),
                   jax.ShapeDtypeStruct((8, D), F32), jax.ShapeDtypeStruct((NGRP, CHUNK, CHUNK), F32),
                   jax.ShapeDtypeStruct((CHUNK, D), F32)]
        + [jax.ShapeDtypeStruct(p.shape, p.dtype) for p in parts],
        scratch_shapes=[pltpu.VMEM((tm, D), BF), pltpu.VMEM((tm, D), F32), pltpu.VMEM((tm, D), BF),
                        pltpu.VMEM((tm, D), F32), pltpu.VMEM((2, RC, D), F32), pltpu.VMEM((CHUNK, D), F32)]
        + _gather_sems(n),
        compiler_params=_params(("arbitrary",)),
    )(dx1b, proj, proj, pa, pb, lng, lnb, wtr, wtrt, bias, wpa, wpb, wo, *parts)
    return outs[:7], outs[7:]


def _mixer_bwd_seq_call(dprojb, dya, proj, h, x, dx1, gates, gmix, w_in_all, cw, lam, wax, parts):
    s = dya.shape[0]
    tm = min(TM_MIX, s)
    nt = s // tm
    tb = tm // 8
    n = len(parts)
    nb, _, wb = w_in_all.shape

    def body(dpb_ref, dya_ref, xg_ref, xh8_ref, h_ref, hh8_ref, x_ref, dx1_ref, xc_s, r_s, ig_s, a_s, m_s,
             gmix_ref, win_ref, cw_ref, lam_ref, wax_ref, *rest):
        ex_in = rest[:n]
        dpa_ref, dx_ref, vec_ref, dwax_ref = rest[n:n + 4]
        ex_out = rest[n + 4:2 * n + 4]
        lm_s, dh_s, dxc_s, c_s, accs, e_send, e_recv, e_local = rest[2 * n + 4:]
        i = pl.program_id(0)
        exchange = _Exchange(ex_in, ex_out, e_send, e_recv, e_local)

        @pl.when(i == 0)
        def _():
            exchange.start()
            accs[...] = jnp.zeros(accs.shape, F32)
            dwax_ref[...] = jnp.zeros(dwax_ref.shape, F32)
            c_s[...] = jnp.zeros((8, D), F32)
            dxc_s[tm:tm + 8, :] = jnp.zeros((8, D), F32)

        first_tile = i == nt - 1
        prev8 = jnp.where(first_tile, 0.0, xh8_ref[...])
        hprev8 = jnp.where(first_tile, 0.0, hh8_ref[...])
        lamv = lam_ref[...]
        sp = _softplus(-lamv)
        hv = h_ref[...]
        g, dg = _gelu_grad(xg_ref[:, D:2 * D])
        dya = dya_ref[...]
        lm_s[...] = dya * g
        drg = ((dya * hv) * dg).astype(BF)
        dpa_ref[:, D:2 * D] = drg
        dpa_ref[:, 2 * D:6 * D] = dpb_ref[...]

        def dpb_block(k):
            return _dot_nt(dpb_ref[:, k * wb - 2 * D:(k + 1) * wb - 2 * D], win_ref[k])
        dh = (_dot_nt(drg[:, 0:2 * wb - D], win_ref[1, :, D - wb:wb])
              + _dot_nt(drg[:, 2 * wb - D:D], win_ref[2, :, 0:2 * D - 2 * wb])
              + _dot_nt(dpb_ref[:, 0:3 * wb - 2 * D], win_ref[2, :, 2 * D - 2 * wb:wb]))
        for k in range(3, 6):
            dh = dh + dpb_block(k)
        dh_s[...] = dh

        c_s[0:1, :] = _scan_bwd(a_s, lm_s, c_s[0:1, :], tm)

        hprev = _shift_back(hprev8, h_ref[...], 1)
        for q in range(NQ):
            cs = slice(q * QW, (q + 1) * QW)
            r = r_s[:, cs].astype(F32)
            ig = ig_s[:, cs].astype(F32)
            a = a_s[:, cs]
            m = m_s[:, cs].astype(F32)
            lm = lm_s[:, cs]
            xq = xc_s[:, cs].astype(F32)
            dixc = lm * m
            dla = (lm * hprev[:, cs]) * a - ((lm * (ig * xq)) * (a * a)) / m
            accs[3, :, cs] += _fold(dla * r)
            dza = (dla * (-RG_C * sp[:, cs])) * (r * (1.0 - r))
            dzx = (dixc * xq) * (ig * (1.0 - ig))
            accs[1, :, cs] += _fold(dza)
            accs[2, :, cs] += _fold(dzx)
            dz = jnp.concatenate([dza, dzx], axis=1).astype(BF)
            dxc_s[0:tm, cs] = dixc * ig + _dot_nt(dz, wax_ref[q])
            dwax_ref[q] += _dot_tn(xc_s[:, cs], dz)

        cur = dxc_s[0:tm, :]
        nx = dxc_s[tm:tm + 8, :]
        drx = cw_ref[3:4, :] * cur
        for j in (1, 2, 3):
            drx = drx + cw_ref[3 - j:4 - j, :] * _shift_fwd(cur, nx, j)
        accs[0] += _fold(cur)
        rx = xg_ref[:, 0:D]
        accs[7] += _fold(cur * rx)
        for j in (1, 2, 3):
            accs[7 - j] += _fold(cur * _shift_back(prev8, rx, j))
        dxc_s[tm:tm + 8, :] = cur[0:8, :]

        drxb = drx.astype(BF)
        dpa_ref[:, 0:D] = drxb
        dh = dh_s[...] + _dot_nt(drxb[:, 0:wb], win_ref[0]) + _dot_nt(drxb[:, wb:D], win_ref[1, :, 0:D - wb])
        for k in range(6, nb):
            dh = dh + dpb_block(k)
        xx = x_ref[...]
        rn = lax.rsqrt(jnp.mean(xx * xx, axis=-1, keepdims=True) + EPS)
        xhn = xx * rn
        accs[8] += _fold(dh * xhn)
        dxh = dh * gmix_ref[...]
        dx_ref[...] = dx1_ref[...] + rn * (dxh - xhn * jnp.mean(dxh * xhn, axis=-1, keepdims=True))

        @pl.when(i == nt - 1)
        def _():
            vec_ref[...] = jnp.zeros((16, D), F32)
            for j in range(9):
                vec_ref[j:j + 1, :] = jnp.sum(accs[j], axis=0, keepdims=True)
            vec_ref[3:4, :] = vec_ref[3:4, :] * (RG_C * _sigmoid(-lamv))
            exchange.finish()

    rev = lambda w: pl.BlockSpec((tm, w), lambda i: (nt - 1 - i, 0))
    halo = pl.BlockSpec((8, D), lambda i: (jnp.maximum((nt - 1 - i) * tb - 1, 0), 0))
    vec = _const_spec((1, D))
    any_spec = pl.BlockSpec(memory_space=pl.ANY)
    outs = pl.pallas_call(
        body, name="mixer_bwd_seq", grid=(nt,),
        in_specs=[rev(4 * D), rev(D), rev(2 * D), halo, rev(D), halo] + [rev(D)] * 7
        + [vec, _const_spec(w_in_all.shape), _const_spec((4, D)), vec, _const_spec(wax.shape)] + [any_spec] * n,
        out_specs=[rev(6 * D), rev(D), pl.BlockSpec((16, D), lambda i: (0, 0)),
                   pl.BlockSpec((NQ, QW, 2 * QW), lambda i: (0, 0, 0))] + [any_spec] * n,
        out_shape=[jax.ShapeDtypeStruct((s, 6 * D), BF), jax.ShapeDtypeStruct((s, D), F32),
                   jax.ShapeDtypeStruct((16, D), F32), jax.ShapeDtypeStruct((NQ, QW, 2 * QW), F32)]
        + [jax.ShapeDtypeStruct(p.shape, p.dtype) for p in parts],
        scratch_shapes=[pltpu.VMEM((tm, D), F32), pltpu.VMEM((tm, D), F32),
                        pltpu.VMEM((tm + 8, D), F32), pltpu.VMEM((8, D), F32), pltpu.VMEM((9, RC, D), F32)]
        + _gather_sems(n),
        compiler_params=_params(("arbitrary",)),
    )(dprojb, dya, proj, proj, h, h, x, dx1, *gates, gmix, w_in_all, cw, lam, wax, *parts)
    return outs[:4], outs[4:]


def _device_of(d):
    return (d // 4, lax.rem(d // 2, 2), lax.rem(d, 2))


_DW_PLAN = (("C", 1), ("C", 2), ("A", 1), ("C", 3), ("A", 2), ("S", 0), ("A", 3), ("O", 0))
_DW_FLIPS = tuple({"C": 2 * j + 1, "A": 2 * j, "S": 1, "O": 0}[kind] for kind, j in _DW_PLAN)


def _dw_exchange_call(name, order, a, b, a_spec, b_spec, k1, n1, s, small):
    ts = min(TS_DW, s)
    ns = s // ts
    nstep = len(_DW_PLAN)

    def slab(i, order_ref):
        return order_ref[i]

    def body(order_ref, a_ref, b_ref, g_ref, own_ref, recv_ref, gsum_ref, acc, sbuf, stage, send_sems, recv_sems,
             st_send, st_recv, local_sem, rbuf, gacc, send1, recv1, send2, recv2):
        i = pl.program_id(0)
        j = pl.program_id(1)
        me = _me()
        mi = _lin(me)
        sibling = _flip(me, 1)
        allreduce = _SmallAllReduce(g_ref, gacc, rbuf, send1, recv1, send2, recv2)
        pl.when((i == 0) & (j == 0))(allreduce.scatter)
        pl.when((i == nstep // 2) & (j == 0))(allreduce.reduce)
        p = _dot_tn(a_ref[...], b_ref[...])

        @pl.when(j == 0)
        def _():
            acc[...] = p

        @pl.when(j > 0)
        def _():
            acc[...] += p

        def send(step):
            kind, jj = _DW_PLAN[step]
            src = sbuf.at[step % 2]
            if kind == "C":
                return pltpu.make_async_remote_copy(
                    src_ref=src, dst_ref=stage.at[jj - 1], send_sem=st_send.at[jj - 1], recv_sem=st_recv.at[jj - 1],
                    device_id=sibling, device_id_type=MESH)
            to = sibling if kind == "S" else _flip(me, 2 * jj)
            return pltpu.make_async_remote_copy(
                src_ref=src, dst_ref=recv_ref.at[mi], send_sem=send_sems.at[jj], recv_sem=recv_sems.at[jj],
                device_id=to, device_id_type=MESH)

        def arrival(jj):
            frm = sibling if jj == 0 else _flip(me, 2 * jj)
            return pltpu.make_async_remote_copy(
                src_ref=sbuf.at[0], dst_ref=recv_ref.at[_lin(frm)], send_sem=send_sems.at[jj],
                recv_sem=recv_sems.at[jj], device_id=frm, device_id_type=MESH)

        def staged(jj):
            return pltpu.make_async_remote_copy(
                src_ref=sbuf.at[0], dst_ref=stage.at[jj - 1], send_sem=st_send.at[jj - 1], recv_sem=st_recv.at[jj - 1],
                device_id=sibling, device_id_type=MESH)

        for step, (kind, jj) in enumerate(_DW_PLAN):
            @pl.when((i == step) & (j == ns - 1))
            def _(step=step, kind=kind, jj=jj):
                if step >= 2:
                    send(step - 2).wait_send()
                if kind == "A":
                    staged(jj).wait_recv()
                    sbuf[step % 2] = (acc[...] + stage[jj - 1].astype(F32)).astype(BF)
                else:
                    sbuf[step % 2] = acc[...].astype(BF)
                if kind != "O":
                    send(step).start()
                else:
                    own_ref[...] = acc[...]
                    mine = pltpu.make_async_copy(sbuf.at[step % 2], recv_ref.at[mi], local_sem)
                    mine.start()
                    send(step - 1).wait_send()
                    for q in range(4):
                        arrival(q).wait_recv()
                    mine.wait()
                    allreduce.finish()
                    gsum_ref[...] = gacc[...]

    vmem = pl.BlockSpec(memory_space=pltpu.VMEM)
    grid_spec = pltpu.PrefetchScalarGridSpec(
        num_scalar_prefetch=1, grid=(nstep, ns),
        in_specs=[a_spec(ts, slab), b_spec(ts, slab), vmem],
        out_specs=[pl.BlockSpec((k1, n1), lambda i, j, o: (0, 0)), pl.BlockSpec(memory_space=pl.ANY), vmem],
        scratch_shapes=[pltpu.VMEM((k1, n1), F32), pltpu.VMEM((2, k1, n1), BF), pltpu.VMEM((3, k1, n1), BF),
                        pltpu.SemaphoreType.DMA((4,)), pltpu.SemaphoreType.DMA((4,)),
                        pltpu.SemaphoreType.DMA((3,)), pltpu.SemaphoreType.DMA((3,)),
                        pltpu.SemaphoreType.DMA(()), pltpu.VMEM((NDEV, SMALL_PER, D), F32),
                        pltpu.VMEM((SMALL_ROWS, D), F32)]
        + [pltpu.SemaphoreType.DMA((NDEV - 1,))] * 4)
    return pl.pallas_call(
        body, name=name, grid_spec=grid_spec,
        out_shape=[jax.ShapeDtypeStruct((k1, n1), F32), jax.ShapeDtypeStruct((NDEV, k1, n1), BF),
                   jax.ShapeDtypeStruct((SMALL_ROWS, D), F32)],
        compiler_params=_params(("arbitrary", "arbitrary")),
    )(order, a, b, small)


def _dw_plain_call(name, me, a, b, a_spec, b_spec, split, k1, n1, s):
    nb = NDEV // split
    r = k1 // split
    ts = min(TS_DW, s)
    ns = s // ts

    def slab(i, me_ref):
        return i

    def body(me_ref, a_ref, b_ref, own_ref, part_ref, acc):
        i = pl.program_id(0)
        j = pl.program_id(1)
        p = _dot_tn(a_ref[...], b_ref[...])

        @pl.when(j == 0)
        def _():
            acc[...] = p

        @pl.when(j > 0)
        def _():
            acc[...] += p

        @pl.when(j == ns - 1)
        def _():
            part_ref[...] = acc[...].astype(BF)

            @pl.when(i == me_ref[0] // split)
            def _():
                off = pl.multiple_of(lax.rem(me_ref[0], split) * r, RC)
                own_ref[...] = acc[pl.ds(off, r), :]

    grid_spec = pltpu.PrefetchScalarGridSpec(
        num_scalar_prefetch=1, grid=(nb, ns),
        in_specs=[a_spec(ts, slab), b_spec(ts, slab)],
        out_specs=[pl.BlockSpec((r, n1), lambda i, j, me_ref: (0, 0)),
                   pl.BlockSpec((None, k1, n1), lambda i, j, me_ref: (i, 0, 0))],
        scratch_shapes=[pltpu.VMEM((k1, n1), F32)])
    own, part = pl.pallas_call(
        body, name=name, grid_spec=grid_spec,
        out_shape=[jax.ShapeDtypeStruct((r, n1), F32), jax.ShapeDtypeStruct((nb, k1, n1), BF)],
        compiler_params=_params(("arbitrary", "arbitrary")),
    )(me, a, b)
    return own, part.reshape(NDEV, r, n1)


def _rows2d(w):
    return lambda ts, slab: pl.BlockSpec((ts, w), lambda i, j, me_ref: (j, 0))


def _cols2d(w):
    return lambda ts, slab: pl.BlockSpec((ts, w), lambda i, j, me_ref: (j, slab(i, me_ref)))


def _blk3d(w):
    return lambda ts, slab: pl.BlockSpec((None, ts, w), lambda i, j, me_ref: (slab(i, me_ref), j, 0))


_BC1 = 1.0 - ADAM_B1 ** ADAM_STEP
_BC2 = 1.0 - ADAM_B2 ** ADAM_STEP


def _adamw_math(w, g, m, v):
    m = ADAM_B1 * m + (1.0 - ADAM_B1) * g
    v = ADAM_B2 * v + (1.0 - ADAM_B2) * (g * g)
    m_hat = m / _BC1
    v_hat = v / _BC2
    delta = -ADAM_LR * (m_hat / (jnp.sqrt(v_hat) + ADAM_EPS) + ADAM_WD * w)
    return delta, m, v


def _row_tile(r):
    for t in (256, 176, 128, 64, 32, 16, 8):
        if r % t == 0:
            return t
    return r


def _reduce_adamw_call(name, sel, own, recv, w, m, v):
    r, c = own.shape
    tr = _row_tile(r)

    def body(sel_ref, own_ref, recv_ref, w_ref, m_ref, v_ref, g_ref, d_ref, nm_ref, nv_ref):
        g = jnp.zeros((tr, c), F32)
        for sdev in range(NDEV):
            part = jnp.where(sel_ref[sdev] == 1, recv_ref[sdev].astype(F32), 0.0)
            g = g + jnp.where(sel_ref[sdev] == 2, own_ref[...], part)
        g_ref[...] = g
        d_ref[...], nm_ref[...], nv_ref[...] = _adamw_math(w_ref[...], g, m_ref[...], v_ref[...])

    tile = pl.BlockSpec((tr, c), lambda i, me_ref: (i, 0))
    grid_spec = pltpu.PrefetchScalarGridSpec(
        num_scalar_prefetch=1, grid=(r // tr,),
        in_specs=[tile, pl.BlockSpec((NDEV, tr, c), lambda i, me_ref: (0, i, 0)), tile, tile, tile],
        out_specs=[tile] * 4)
    return pl.pallas_call(
        body, name=name, grid_spec=grid_spec,
        out_shape=[jax.ShapeDtypeStruct((r, c), F32)] * 4,
        compiler_params=_params(("parallel",)),
    )(sel, own, recv, w, m, v)


def _adamw_call(name, w, g, m, v):
    r, c = w.shape
    tr = _row_tile(r)

    def body(w_ref, g_ref, m_ref, v_ref, d_ref, nm_ref, nv_ref):
        d_ref[...], nm_ref[...], nv_ref[...] = _adamw_math(w_ref[...], g_ref[...], m_ref[...], v_ref[...])

    tile = pl.BlockSpec((tr, c), lambda i: (i, 0))
    return pl.pallas_call(
        body, name=name, grid=(r // tr,), in_specs=[tile] * 4, out_specs=[tile] * 3,
        out_shape=[jax.ShapeDtypeStruct((r, c), F32)] * 3,
        compiler_params=_params(("parallel",)),
    )(w, g, m, v)


def _me():
    return lax.axis_index("x"), lax.axis_index("y"), lax.axis_index("c")


def _flip(pos, r):
    x, y, c = pos
    return (1 - x if r & 4 else x, 1 - y if r & 2 else y, 1 - c if r & 1 else c)


def _lin(pos):
    return pos[0] * 4 + pos[1] * 2 + pos[2]


class _SmallAllReduce:
    def __init__(self, g_ref, out_ref, rbuf, send1, recv1, send2, recv2):
        self.g, self.out, self.rbuf = g_ref, out_ref, rbuf
        self.sems = (send1, recv1, send2, recv2)
        self.me = _me()
        self.mi = _lin(self.me)

    @staticmethod
    def _rows(d):
        return pl.ds(pl.multiple_of(d * SMALL_PER, 8), SMALL_PER)

    def _scatter(self, r, outgoing):
        peer = _flip(self.me, r)
        src, dst = (_lin(peer), self.mi) if outgoing else (self.mi, _lin(peer))
        return pltpu.make_async_remote_copy(
            src_ref=self.g.at[self._rows(src)], dst_ref=self.rbuf.at[dst],
            send_sem=self.sems[0].at[r - 1], recv_sem=self.sems[1].at[r - 1], device_id=peer, device_id_type=MESH)

    def _spread(self, r, outgoing):
        peer = _flip(self.me, r)
        rows = self._rows(self.mi if outgoing else _lin(peer))
        return pltpu.make_async_remote_copy(
            src_ref=self.out.at[rows], dst_ref=self.out.at[rows],
            send_sem=self.sems[2].at[r - 1], recv_sem=self.sems[3].at[r - 1], device_id=peer, device_id_type=MESH)

    def scatter(self):
        for r in range(1, NDEV):
            self._scatter(r, True).start()
        self.rbuf[self.mi] = self.g[self._rows(self.mi), :]

    def reduce(self):
        for r in range(1, NDEV):
            self._scatter(r, False).wait_recv()
        for r in range(1, NDEV):
            self._scatter(r, True).wait_send()
        tot = self.rbuf[0]
        for d in range(1, NDEV):
            tot = tot + self.rbuf[d]
        self.out[self._rows(self.mi), :] = tot
        for r in range(1, NDEV):
            self._spread(r, True).start()

    def finish(self):
        for r in range(1, NDEV):
            self._spread(r, False).wait_recv()
        for r in range(1, NDEV):
            self._spread(r, True).wait_send()


def _head_blocks(w):
    z = jnp.zeros((64, 64), w.dtype)
    groups = []
    for q in range(NQ):
        rows = [jnp.concatenate([w[4 * q + a] if a == b else z for b in range(4)], axis=1) for a in range(4)]
        groups.append(jnp.concatenate(rows, axis=0))
    return jnp.stack(groups)


def _head_unblocks(g):
    return jnp.stack([g[q, 64 * a:64 * a + 64, 64 * a:64 * a + 64] for q in range(NQ) for a in range(4)])


def _local_step(x, tgt, p, me, order, order_dw):
    s = x.shape[0]
    vec = lambda a: a.reshape(1, D)
    gmix, gffn, gfin = vec(p["norm_mix_g"]), vec(p["norm_ffn_g"]), vec(p["norm_final_g"])
    cb, ba, bx, lam = vec(p["conv_b"]), vec(p["rg_ba"]), vec(p["rg_bx"]), vec(p["rg_lambda"])
    lng, lnb = vec(p["sgu_ln_g"]), vec(p["sgu_ln_b"])
    wax = jnp.concatenate([_head_blocks(p["rg_wa"]), _head_blocks(p["rg_wx"])], axis=2).astype(BF)
    tril = jnp.tril(jnp.ones((CHUNK, CHUNK), bool))
    ws = jnp.where(tril[None], p["sgu_ws"], 0.0)
    wtr = ws.astype(BF)
    wtrt = jnp.swapaxes(ws, 1, 2).astype(BF)
    bias = jnp.repeat(p["sgu_bs"].T, GW, axis=1)
    shard = {k: p[k].astype(BF) for k in _BIG}

    proj, h1, w_in, (wpa, wpb, wo, cw) = _proj_gather_call(
        x, gmix, shard["w_in"], order, [shard["w_proj_a"], shard["w_proj_b"], shard["w_out"], p["conv_w"]])
    wpa, wpb, wo = (t.reshape(D, D) for t in (wpa, wpb, wo))
    cw = jnp.swapaxes(cw, 0, 1).reshape(4, D)
    (h, pa, pb, x1, *gates, ya, yb, mg), (wgu, wdn) = _mixer_fwd_call(
        proj, x, cw, cb, ba, bx, lam, lng, lnb, wax, wtr, bias, wpa, wpb, wo, [shard["w_gate_up"], shard["w_down"]])
    wdn = wdn.reshape(NDEV // 2, -1, D)
    nb, _, wb = w_in.shape
    fb = wgu.shape[2]
    nh = wdn.shape[0]
    dx1, dx1b, h2, act, dgu, dx2b, facc = _ffn_call(x1, tgt, gffn, gfin, wgu, wdn)
    assert nb == NDEV and 2 * nh == NDEV
    own_gu, part_gu = _dw_plain_call("dw_gate_up", me, dgu, h2, _blk3d(fb), _rows2d(D), 1, fb, D, s)
    own_dn, part_dn = _dw_plain_call("dw_down", me, act, dx2b, _blk3d(fb), _rows2d(D), 2, fb, D, s)
    (dprojb, dya, dpa, dpb, bvec, dws, dbs), (recv_dn,) = _mixer_bwd_pre_call(
        dx1b, proj, pa, pb, lng, lnb, wtr, wtrt, bias, wpa, wpb, wo, [part_dn])
    own_pa, part_pa = _dw_plain_call("dw_proj_a", me, ya, dpa, _rows2d(D), _rows2d(D), NDEV, D, D, s)
    own_pb, part_pb = _dw_plain_call("dw_proj_b", me, yb, dpb, _rows2d(D), _rows2d(D), NDEV, D, D, s)
    own_wo, part_wo = _dw_plain_call("dw_out", me, mg, dx1b, _rows2d(D), _rows2d(D), NDEV, D, D, s)
    (dproj, dx, svec, dwax), (recv_gu, recv_pa, recv_pb, recv_wo) = _mixer_bwd_seq_call(
        dprojb, dya, proj, h, x, dx1, gates, gmix, w_in, cw, lam, wax, [part_gu, part_pa, part_pb, part_wo])
    small = {
        "norm_mix_g": svec[8], "norm_ffn_g": facc[0], "norm_final_g": facc[1],
        "conv_b": svec[0], "rg_ba": svec[1], "rg_bx": svec[2], "rg_lambda": svec[3],
        "sgu_ln_g": bvec[0], "sgu_ln_b": bvec[1],
        "rg_wa": _head_unblocks(dwax[:, :, 0:QW]), "rg_wx": _head_unblocks(dwax[:, :, QW:2 * QW]),
        "sgu_ws": dws, "sgu_bs": dbs[:, ::GW].T,
    }
    packed = _pack_small(small, svec[4:8], facc[2])
    own_in, recv_in, gsum = _dw_exchange_call("dw_in", order_dw, h1, dproj, _rows2d(D), _cols2d(wb), D, wb, s, packed)
    dw = {
        "w_gate_up": (own_gu, recv_gu), "w_down": (own_dn, recv_dn), "w_proj_a": (own_pa, recv_pa),
        "w_proj_b": (own_pb, recv_pb), "w_out": (own_wo, recv_wo), "w_in": (own_in, recv_in),
    }
    return gsum, dx, dw


_BIG = ("w_in", "w_gate_up", "w_down", "w_proj_a", "w_proj_b", "w_out")
_VEC_ROWS = ("norm_mix_g", "norm_ffn_g", "norm_final_g", "conv_b", "rg_ba", "rg_bx", "rg_lambda",
             "sgu_ln_g", "sgu_ln_b", "sgu_bs")
_WEIGHTS = ("norm_mix_g", "w_in", "conv_w", "conv_b", "rg_wa", "rg_ba", "rg_wx", "rg_bx", "rg_lambda",
            "sgu_ln_g", "sgu_ln_b", "sgu_ws", "sgu_bs", "w_proj_a", "w_proj_b", "w_out", "norm_ffn_g",
            "w_gate_up", "w_down", "norm_final_g")


def _pack_small(t, conv_w, extra=None):
    extra = jnp.zeros((1, D), F32) if extra is None else extra.reshape(1, D)
    head = jnp.concatenate([t[k].reshape(1, D) for k in _VEC_ROWS] + [conv_w, extra, jnp.zeros((1, D), F32)], axis=0)
    return jnp.concatenate([head, t["rg_wa"].reshape(64, D), t["rg_wx"].reshape(64, D), t["sgu_ws"].reshape(128, D),
                            jnp.zeros((SMALL_ROWS - 272, D), F32)], axis=0)


def _unpack_small(a):
    out = {k: a[j] for j, k in enumerate(_VEC_ROWS)}
    out["conv_w"] = a[10:14]
    out["rg_wa"] = a[16:80].reshape(16, 64, 64)
    out["rg_wx"] = a[80:144].reshape(16, 64, 64)
    out["sgu_ws"] = a[144:272].reshape(NGRP, CHUNK, CHUNK)
    return out


def kernel(x, norm_mix_g, w_in, conv_w, conv_b, rg_wa, rg_ba, rg_wx, rg_bx, rg_lambda, sgu_ln_g, sgu_ln_b, sgu_ws, sgu_bs, w_proj_a, w_proj_b, w_out, norm_ffn_g, w_gate_up, w_down, norm_final_g, loss_target, m_norm_mix_g, m_w_in, m_conv_w, m_conv_b, m_rg_wa, m_rg_ba, m_rg_wx, m_rg_bx, m_rg_lambda, m_sgu_ln_g, m_sgu_ln_b, m_sgu_ws, m_sgu_bs, m_w_proj_a, m_w_proj_b, m_w_out, m_norm_ffn_g, m_w_gate_up, m_w_down, m_norm_final_g, v_norm_mix_g, v_w_in, v_conv_w, v_conv_b, v_rg_wa, v_rg_ba, v_rg_wx, v_rg_bx, v_rg_lambda, v_sgu_ln_g, v_sgu_ln_b, v_sgu_ws, v_sgu_bs, v_w_proj_a, v_w_proj_b, v_w_out, v_norm_ffn_g, v_w_gate_up, v_w_down, v_norm_final_g):
    args = dict(locals())
    w = {k: args[k] for k in _WEIGHTS}
    m = {k: args["m_" + k] for k in _WEIGHTS}
    v = {k: args["v_" + k] for k in _WEIGHTS}
    for d in (w, m, v):
        for k in _WEIGHTS:
            if k != "norm_final_g":
                d[k] = d[k][0]
    me = _lin(_me())
    me1 = me.reshape(1).astype(jnp.int32)

    order = jnp.bitwise_xor(me, jnp.array(_PASS_FLIPS, jnp.int32)).astype(jnp.int32)
    order_dw = jnp.bitwise_xor(me, jnp.array(_DW_FLIPS, jnp.int32)).astype(jnp.int32)

    gsum, dx, dw = _local_step(x[0], loss_target[0], w, me1, order, order_dw)

    peer = jnp.bitwise_xor(jnp.arange(NDEV, dtype=jnp.int32), me)
    sel_direct = jnp.where(peer == 0, 2, 1).astype(jnp.int32)
    sel_two_level = jnp.where(peer == 0, 2, jnp.where((peer == 1) | (peer % 2 == 0), 1, 0)).astype(jnp.int32)
    grads, delta, new_m, new_v = {}, {}, {}, {}
    for k in _BIG:
        own, recv = dw[k]
        flip = own.shape != w[k].shape
        wmv = [jnp.swapaxes(t, 0, 1) if flip else t for t in (w[k], m[k], v[k])]
        res = _reduce_adamw_call("adamw_" + k, sel_two_level if k == "w_in" else sel_direct, own, recv, *wmv)
        grads[k], delta[k], new_m[k], new_v[k] = (jnp.swapaxes(t, 0, 1) if flip else t for t in res)

    loss = (0.5 / D) * jnp.sum(gsum[14])
    zc = jnp.zeros((4, D), F32)
    d_s, m_s, v_s = _adamw_call("adamw_small", _pack_small(w, zc), gsum, _pack_small(m, zc), _pack_small(v, zc))
    gs, ds, ms, vs = _unpack_small(gsum), _unpack_small(d_s), _unpack_small(m_s), _unpack_small(v_s)
    g_cw = lax.dynamic_slice(gs["conv_w"], (0, me * 128), (4, 128))
    ds["conv_w"], ms["conv_w"], vs["conv_w"] = _adamw_call("adamw_conv_w", w["conv_w"], g_cw, m["conv_w"], v["conv_w"])
    gs["conv_w"] = g_cw
    for k in _WEIGHTS:
        if k not in _BIG:
            shp = w[k].shape
            grads[k], delta[k], new_m[k], new_v[k] = (t[k].reshape(shp) for t in (gs, ds, ms, vs))

    def lift(t, k):
        return t[k] if k == "norm_final_g" else t[k][None]

    outs = [loss, dx[None]]
    for t in (grads, delta, new_m, new_v):
        outs += [lift(t, k) for k in _WEIGHTS]
    return tuple(outs)
```

```python
import functools

import jax
import jax.numpy as jnp
from jax import lax
from jax.experimental import pallas as pl
from jax.experimental.pallas import tpu as pltpu

F32 = jnp.float32
BF = jnp.bfloat16

D = 1024
NDEV = 8
EPS = 1e-6
RG_C = 8.0
CHUNK = 128
NGRP = 8
GW = 128
NQ = 4
QW = 256
RC = 16
SMALL_ROWS = 320
SMALL_PER = SMALL_ROWS // NDEV

ADAM_LR = 0.001
ADAM_B1 = 0.9
ADAM_B2 = 0.999
ADAM_EPS = 1e-08
ADAM_WD = 0.01
ADAM_STEP = 10

VMEM_LIMIT = 60 * 1024 * 1024

MESH = pl.DeviceIdType.MESH


def _rows(n, fn, unroll=2, rc=RC):
    def body(i, c):
        fn(pl.multiple_of(i * rc, rc))
        return c
    lax.fori_loop(0, n // rc, body, 0, unroll=unroll)


def _fold(v):
    return jnp.sum(v.reshape(v.shape[0] // RC, RC, v.shape[1]), axis=0)


def _dot(a, b):
    return jnp.dot(a, b, preferred_element_type=F32)


def _dot_nt(a, b):
    return lax.dot_general(a, b, (((1,), (1,)), ((), ())), preferred_element_type=F32)


def _dot_tn(a, b):
    return lax.dot_general(a, b, (((0,), (0,)), ((), ())), preferred_element_type=F32)


_GC = 0.7978845608028654
_GK = 0.044715


def _gelu(x):
    t = jnp.tanh(_GC * (x + _GK * (x * x * x)))
    return x * (0.5 * (1.0 + t))


def _gelu_grad(x):
    x2 = x * x
    t = jnp.tanh(_GC * (x + _GK * (x2 * x)))
    cdf = 0.5 * (1.0 + t)
    dg = cdf + (0.5 * x) * (1.0 - t * t) * (_GC * (1.0 + (3.0 * _GK) * x2))
    return x * cdf, dg


def _sigmoid(x):
    return jax.nn.sigmoid(x)


def _log1p(e):
    u = 1.0 + e
    d = u - 1.0
    return jnp.where(d == 0.0, e, jnp.log(u) * (e / jnp.where(d == 0.0, 1.0, d)))


def _softplus(z):
    return jnp.maximum(z, 0.0) + _log1p(jnp.exp(-jnp.abs(z)))


def _neg_expm1(z):
    u = jnp.exp(z)
    lu = jnp.log(u)
    k = (1.0 - u) * (z / jnp.where(lu == 0.0, 1.0, lu))
    small = jnp.where(lu == 0.0, -z, k)
    return jnp.where(z > -0.5, small, 1.0 - u)


def _shift_back(prev8, cur, j):
    cat = jnp.concatenate([prev8, cur], axis=0)
    return pltpu.roll(cat, j, 0)[8:8 + cur.shape[0]]


def _shift_fwd(cur, next8, j):
    cat = jnp.concatenate([cur, next8], axis=0)
    n = cat.shape[0]
    return pltpu.roll(cat, n - j, 0)[0:cur.shape[0]]


def _const_spec(shape):
    nd = len(shape)
    return pl.BlockSpec(shape, lambda *_: (0,) * nd, pipeline_mode=pl.Buffered(1))


def _params(sem):
    return pltpu.CompilerParams(dimension_semantics=sem, vmem_limit_bytes=VMEM_LIMIT)


TM_PROJ = 1024
TM_MIX = 256
TM_FFN = 256
TM_DX = 512
TS_DW = 4096


_CHIPS = (4, 2, 6)
_PASS_FLIPS = (0, 1, 4, 2, 6, 5, 3, 7)


class _Gather:
    def __init__(self, ins, outs, send_sems, recv_sems, local_sems):
        self.ins, self.outs = ins, outs
        self.send_sems, self.recv_sems, self.local_sems = send_sems, recv_sems, local_sems
        self.me = _me()
        self.sibling = _flip(self.me, 1)

    def _copy(self, a, kind, block, to, src=None):
        dst = self.outs[a].at[_lin(block)]
        return pltpu.make_async_remote_copy(
            src_ref=dst if src is None else src, dst_ref=dst,
            send_sem=self.send_sems.at[a, kind], recv_sem=self.recv_sems.at[a, kind],
            device_id=to, device_id_type=MESH)

    def _local(self, a):
        return pltpu.make_async_copy(self.ins[a], self.outs[a].at[_lin(self.me)], self.local_sems.at[a])

    def start(self):
        for a in range(len(self.ins)):
            self._local(a).start()
            self._copy(a, 0, self.me, self.sibling, src=self.ins[a]).start()
            for j, f in enumerate(_CHIPS):
                self._copy(a, 1 + j, self.me, _flip(self.me, f), src=self.ins[a]).start()

    def forward(self):
        for j, f in enumerate(_CHIPS):
            for a in range(len(self.ins)):
                self._copy(a, 1 + j, _flip(self.me, f), self.me).wait_recv()
                self._copy(a, 4 + j, _flip(self.me, f), self.sibling).start()

    def finish(self):
        for a in range(len(self.ins)):
            self._copy(a, 0, self.sibling, self.me).wait_recv()
            for j, f in enumerate(_CHIPS):
                self._copy(a, 4 + j, _flip(self.me, f | 1), self.me).wait_recv()
            self._copy(a, 0, self.me, self.sibling, src=self.ins[a]).wait_send()
            for j, f in enumerate(_CHIPS):
                self._copy(a, 1 + j, self.me, _flip(self.me, f), src=self.ins[a]).wait_send()
                self._copy(a, 4 + j, _flip(self.me, f), self.sibling).wait_send()
            self._local(a).wait()


class _Exchange:
    def __init__(self, ins, outs, send_sems, recv_sems, local_sems):
        self.ins, self.outs = ins, outs
        self.send_sems, self.recv_sems, self.local_sems = send_sems, recv_sems, local_sems
        self.me = _me()

    def _copy(self, a, r, outgoing):
        peer = _flip(self.me, r)
        src, dst = (peer, self.me) if outgoing else (self.me, peer)
        return pltpu.make_async_remote_copy(
            src_ref=self.ins[a].at[_lin(src)], dst_ref=self.outs[a].at[_lin(dst)],
            send_sem=self.send_sems.at[a, r - 1], recv_sem=self.recv_sems.at[a, r - 1],
            device_id=peer, device_id_type=MESH)

    def _local(self, a):
        mi = _lin(self.me)
        return pltpu.make_async_copy(self.ins[a].at[mi], self.outs[a].at[mi], self.local_sems.at[a])

    def start(self):
        for a in range(len(self.ins)):
            self._local(a).start()
        for r in range(1, NDEV):
            for a in range(len(self.ins)):
                self._copy(a, r, True).start()

    def finish(self):
        for r in range(1, NDEV):
            for a in range(len(self.ins)):
                self._copy(a, r, False).wait_recv()
        for r in range(1, NDEV):
            for a in range(len(self.ins)):
                self._copy(a, r, True).wait_send()
        for a in range(len(self.ins)):
            self._local(a).wait()


def _gather_sems(n):
    return [pltpu.SemaphoreType.DMA((n, 7)), pltpu.SemaphoreType.DMA((n, 7)), pltpu.SemaphoreType.DMA((n,))]


def _proj_gather_call(x, gmix, w_shard, order, extras):
    s = x.shape[0]
    tm = min(TM_PROJ, s)
    nt = s // tm
    wb = w_shard.shape[1]
    n = len(extras)

    def body(order_ref, x_ref, g_ref, wsh_ref, *rest):
        ex_in = rest[:n]
        proj_hbm, h1_hbm, wall_hbm = rest[n:n + 3]
        ex_out = rest[n + 3:2 * n + 3]
        wall, hc, pbuf, ws_send, ws_recv, ex_send, ex_recv, ex_local, out_sems, psems = rest[2 * n + 3:]
        k = pl.program_id(0)
        i = pl.program_id(1)
        me = _me()
        sibling = _flip(me, 1)
        gather = _Gather(ex_in, ex_out, ex_send, ex_recv, ex_local)

        def wcopy(kind, block, to):
            ref = wall.at[_lin(block)]
            return pltpu.make_async_remote_copy(
                src_ref=ref, dst_ref=ref, send_sem=ws_send.at[kind], recv_sem=ws_recv.at[kind],
                device_id=to, device_id_type=MESH)

        head = i == 0

        @pl.when(head & (k == 0))
        def _():
            wall[_lin(me)] = wsh_ref[...]
            wcopy(0, me, sibling).start()
            for j, f in enumerate(_CHIPS):
                wcopy(1 + j, me, _flip(me, f)).start()
            gather.start()

        out_w = pltpu.make_async_copy(wall, wall_hbm, out_sems.at[0])
        out_h = pltpu.make_async_copy(hc, h1_hbm, out_sems.at[1])

        @pl.when(head & (k == 1))
        def _():
            wcopy(0, sibling, me).wait_recv()
            out_h.start()

        for j, f in enumerate(_CHIPS):
            @pl.when(head & (k == 2 + j))
            def _(j=j, f=f):
                wcopy(1 + j, _flip(me, f), me).wait_recv()
                wcopy(4 + j, _flip(me, f), sibling).start()

            @pl.when(head & (k == 5 + j))
            def _(j=j, f=f):
                wcopy(4 + j, _flip(me, f | 1), me).wait_recv()
                if 5 + j == NDEV - 1:
                    out_w.start()

        pl.when(head & (k == 5))(gather.forward)

        base = pl.multiple_of(i * tm, tm)

        @pl.when(k == 0)
        def _():
            g = g_ref[...]

            def norm(r0):
                xx = x_ref[pl.ds(r0, RC), :]
                r = lax.rsqrt(jnp.mean(xx * xx, axis=-1, keepdims=True) + EPS)
                hc[pl.ds(base + r0, RC), :] = ((xx * r) * g).astype(BF)
            _rows(tm, norm, unroll=4)

        blk = order_ref[k]
        wk = wall.at[blk]
        step = k * nt + i
        slot = lax.rem(step, 2)
        subs = [(r0, min(TM_MIX, tm - r0)) for r0 in range(0, tm, TM_MIX)]

        def out_copy(q, r0, rs):
            return pltpu.make_async_copy(
                pbuf.at[slot, pl.ds(r0, rs)],
                proj_hbm.at[pl.ds(base + r0, rs), pl.ds(pl.multiple_of(blk * wb, 128), wb)], psems.at[slot, q])

        @pl.when(step >= 2)
        def _():
            for q, (r0, rs) in enumerate(subs):
                out_copy(q, r0, rs).wait()

        for q, (r0, rs) in enumerate(subs):
            pbuf[slot, r0:r0 + rs, :] = _dot(hc[pl.ds(base + r0, rs), :], wk[...])
            out_copy(q, r0, rs).start()

        @pl.when((k == NDEV - 1) & (i == nt - 1))
        def _():
            wcopy(0, me, sibling).wait_send()
            for j, f in enumerate(_CHIPS):
                wcopy(1 + j, me, _flip(me, f)).wait_send()
                wcopy(4 + j, _flip(me, f), sibling).wait_send()
            gather.finish()
            out_w.wait()
            out_h.wait()
            for q, (r0, rs) in enumerate(subs):
                out_copy(q, r0, rs).wait()
                pltpu.make_async_copy(pbuf.at[1 - slot, pl.ds(r0, rs)], proj_hbm.at[pl.ds(r0, rs), pl.ds(0, wb)],
                                      psems.at[1 - slot, q]).wait()

    any_spec = pl.BlockSpec(memory_space=pl.ANY)
    grid_spec = pltpu.PrefetchScalarGridSpec(
        num_scalar_prefetch=1, grid=(NDEV, nt),
        in_specs=[pl.BlockSpec((tm, D), lambda k, i, o: (jnp.where(k == 0, i, nt - 1), 0)),
                  pl.BlockSpec((1, D), lambda k, i, o: (0, 0)),
                  pl.BlockSpec(w_shard.shape, lambda k, i, o: (0, 0))] + [any_spec] * n,
        out_specs=[any_spec, any_spec, any_spec] + [any_spec] * n,
        scratch_shapes=[pltpu.VMEM((NDEV,) + w_shard.shape, BF), pltpu.VMEM((s, D), BF), pltpu.VMEM((2, tm, wb), F32),
                        pltpu.SemaphoreType.DMA((7,)), pltpu.SemaphoreType.DMA((7,))] + _gather_sems(n)
        + [pltpu.SemaphoreType.DMA((2,)), pltpu.SemaphoreType.DMA((2, -(-tm // TM_MIX)))])
    outs = pl.pallas_call(
        body, name="proj_gather", grid_spec=grid_spec,
        out_shape=[jax.ShapeDtypeStruct((s, NDEV * wb), F32), jax.ShapeDtypeStruct((s, D), BF),
                   jax.ShapeDtypeStruct((NDEV,) + w_shard.shape, BF)]
        + [jax.ShapeDtypeStruct((NDEV,) + e.shape, e.dtype) for e in extras],
        compiler_params=_params(("arbitrary", "arbitrary")),
    )(order, x, gmix, w_shard, *extras)
    return outs[0], outs[1], outs[2], outs[3:]


def _conv_tile(rx, prev8, cw_ref, cb):
    xc = cb + cw_ref[3:4, :] * rx
    for j in (1, 2, 3):
        xc = xc + cw_ref[3 - j:4 - j, :] * _shift_back(prev8, rx, j)
    return xc


def _gate_tile(xcb, wax_ref, ba, bx, sp, q):
    cs = slice(q * QW, (q + 1) * QW)
    z = _dot(xcb[:, cs], wax_ref[q])
    r = _sigmoid(z[:, 0:QW] + ba[:, cs])
    ig = _sigmoid(z[:, QW:2 * QW] + bx[:, cs])
    return r, ig, (-RG_C * r) * sp[:, cs]


def _scan_fwd(a_s, b_s, out_ref, h0, n):
    rowi = lax.broadcasted_iota(jnp.int32, (8, D), 0)

    def block(t, h):
        rows = pl.ds(t * 8, 8)
        a = a_s[rows, :]
        b = b_s[rows, :]
        for d in (1, 2, 4):
            m = rowi >= d
            b = jnp.where(m, a * pltpu.roll(b, d, 0) + b, b)
            a = jnp.where(m, a * pltpu.roll(a, d, 0), a)
        hb = b + a * h
        out_ref[rows, :] = hb
        return hb[7:8, :]
    h = h0
    for t in range(n // 8):
        h = block(t, h)
    return h


def _scan_bwd(a_s, lm_s, c0, n):
    rowi = lax.broadcasted_iota(jnp.int32, (8, D), 0)
    nblk = n // 8

    def block(k, cin):
        rows = pl.ds((nblk - 1 - k) * 8, 8)
        a = a_s[rows, :]
        dh = lm_s[rows, :]
        b = a * dh
        for d in (1, 2, 4):
            m = rowi < 8 - d
            b = jnp.where(m, a * pltpu.roll(b, 8 - d, 0) + b, b)
            a = jnp.where(m, a * pltpu.roll(a, 8 - d, 0), a)
        mu = b + a * cin
        lm_s[rows, :] = dh + jnp.where(rowi < 7, pltpu.roll(mu, 7, 0), cin)
        return mu[0:1, :]
    c = c0
    for k in range(nblk):
        c = block(k, c)
    return c


def _mixer_fwd_call(proj, x, cw, cb, ba, bx, lam, lng, lnb, wax, wtr, bias, wpa, wpb, wo, shards):
    s = x.shape[0]
    tm = min(TM_MIX, s)
    nt = s // tm
    pw = proj.shape[1]
    n = len(shards)

    def body(proj_ref, x_ref, cw_ref, cb_ref, ba_ref, bx_ref, lam_ref, lng_ref, lnb_ref, wax_ref, wtr_ref,
             bias_ref, wpa_ref, wpb_ref, wo_ref, *rest):
        sh_in = rest[:n]
        h_ref, pa_ref, pb_ref, x1_ref, xc_ref, r_ref, ig_ref, a_s, m_ref, ya_ref, yb_ref, mg_ref = rest[n:n + 12]
        sh_out = rest[n + 12:2 * n + 12]
        prev_s, b_s, hc_s, vn_s, mx_s, g_send, g_recv, g_local = rest[2 * n + 12:]
        i = pl.program_id(0)
        gather = _Gather(sh_in, sh_out, g_send, g_recv, g_local)

        @pl.when(i == 0)
        def _():
            gather.start()
            prev_s[...] = jnp.zeros((8, D), F32)
            hc_s[...] = jnp.zeros((8, D), F32)

        pl.when(i == nt // 2)(gather.forward)

        rx = proj_ref[:, 0:D]
        xc = _conv_tile(rx, prev_s[...], cw_ref, cb_ref[...])
        prev_s[...] = rx[tm - 8:tm, :]
        xc_ref[...] = xc.astype(BF)
        xcb = xc.astype(BF)
        sp = _softplus(-lam_ref[...])
        ba = ba_ref[...]
        bx = bx_ref[...]
        for q in range(NQ):
            cs = slice(q * QW, (q + 1) * QW)
            r, ig, la = _gate_tile(xcb, wax_ref, ba, bx, sp, q)
            mq = jnp.sqrt(_neg_expm1(2.0 * la))
            r_ref[:, cs] = r.astype(BF)
            ig_ref[:, cs] = ig.astype(BF)
            m_ref[:, cs] = mq.astype(BF)
            a_s[:, cs] = jnp.exp(la)
            b_s[:, cs] = mq * (ig * xc[:, cs])

        gv = _gelu(proj_ref[:, 3 * D:4 * D])
        dv = gv - jnp.mean(gv, axis=-1, keepdims=True)
        var = jnp.mean(dv * dv, axis=-1, keepdims=True)
        vn_s[...] = ((dv * lax.rsqrt(var + EPS)) * lng_ref[...] + lnb_ref[...]).astype(BF)
        nc = tm // CHUNK
        for c in range(nc):
            rs = slice(c * CHUNK, (c + 1) * CHUNK)
            for g in range(NGRP):
                cs = slice(g * GW, (g + 1) * GW)
                mx_s[rs, cs] = _dot(wtr_ref[g], vn_s[rs, cs])
        mixed = mx_s[...] + jnp.concatenate([bias_ref[...]] * nc, axis=0)
        yb = (_gelu(proj_ref[:, 2 * D:3 * D]) * mixed).astype(BF)
        yb_ref[...] = yb
        pb = _dot(yb, wpb_ref[...])
        pb_ref[...] = pb.astype(BF)
        mx_s[...] = pb

        hc_s[0:1, :] = _scan_fwd(a_s, b_s, h_ref, hc_s[0:1, :], tm)

        ya = (_gelu(proj_ref[:, D:2 * D]) * h_ref[...]).astype(BF)
        ya_ref[...] = ya
        pa = _dot(ya, wpa_ref[...])
        pa_ref[...] = pa.astype(BF)
        mg = (_sigmoid(proj_ref[:, 4 * D:5 * D]) * pa + _sigmoid(proj_ref[:, 5 * D:6 * D]) * mx_s[...]).astype(BF)
        mg_ref[...] = mg
        x1_ref[...] = x_ref[...] + _dot(mg, wo_ref[...])

        pl.when(i == nt - 1)(gather.finish)

    tile = lambda w: pl.BlockSpec((tm, w), lambda i: (i, 0))
    vec = _const_spec((1, D))
    any_spec = pl.BlockSpec(memory_space=pl.ANY)
    outs = pl.pallas_call(
        body, name="mixer_fwd", grid=(nt,),
        in_specs=[tile(pw), tile(D), _const_spec((4, D)), vec, vec, vec, vec, vec, vec,
                  _const_spec(wax.shape), _const_spec(wtr.shape), _const_spec(bias.shape),
                  _const_spec((D, D)), _const_spec((D, D)), _const_spec((D, D))] + [any_spec] * n,
        out_specs=[tile(D)] * 12 + [any_spec] * n,
        out_shape=[jax.ShapeDtypeStruct((s, D), dt) for dt in (F32, BF, BF, F32, BF, BF, BF, F32, BF, BF, BF, BF)]
        + [jax.ShapeDtypeStruct((NDEV,) + e.shape, e.dtype) for e in shards],
        scratch_shapes=[pltpu.VMEM((8, D), F32), pltpu.VMEM((tm, D), F32),
                        pltpu.VMEM((8, D), F32), pltpu.VMEM((tm, D), BF), pltpu.VMEM((tm, D), F32)]
        + _gather_sems(n),
        compiler_params=_params(("arbitrary",)),
    )(proj, x, cw, cb, ba, bx, lam, lng, lnb, wax, wtr, bias, wpa, wpb, wo, *shards)
    return outs[:12], outs[12:]


def _ffn_call(x1, tgt, gffn, gfin, wgu, wdn):
    s = x1.shape[0]
    tm = min(TM_FFN, s)
    nt = s // tm
    nh = wdn.shape[0]
    fb = wgu.shape[2]

    def body(x1_ref, tgt_ref, gffn_ref, gfin_ref, wgu_ref, wdn_ref,
             dx1_ref, dx1b_ref, h2_ref, act_ref, dgu_ref, dx2b_ref, acc_ref,
             g_s, u_s, dx2_s, accs):
        i = pl.program_id(0)

        @pl.when(i == 0)
        def _():
            accs[...] = jnp.zeros(accs.shape, F32)

        gffn = gffn_ref[...]
        gfin = gfin_ref[...]

        x1 = x1_ref[...]
        r2 = lax.rsqrt(jnp.mean(x1 * x1, axis=-1, keepdims=True) + EPS)
        xh2 = x1 * r2
        h2 = (xh2 * gffn).astype(BF)
        h2_ref[...] = h2

        for k in range(nh):
            g = _dot(h2, wgu_ref[k])
            u = _dot(h2, wgu_ref[k + nh])
            g_s[k] = g
            u_s[k] = u
            act_ref[k] = ((g * _sigmoid(g)) * u).astype(BF)

        x2 = x1
        for k in range(nh):
            x2 = x2 + _dot(act_ref[k], wdn_ref[k])

        r3 = lax.rsqrt(jnp.mean(x2 * x2, axis=-1, keepdims=True) + EPS)
        xh = x2 * r3
        err = xh * gfin - tgt_ref[...]
        accs[2] += _fold(err * err)
        dy = err * (1.0 / D)
        accs[1] += _fold(dy * xh)
        dxh = dy * gfin
        dx2 = r3 * (dxh - xh * jnp.mean(dxh * xh, axis=-1, keepdims=True))
        dx2_s[...] = dx2
        dx2b = dx2.astype(BF)
        dx2b_ref[...] = dx2b

        for k in range(nh):
            da = _dot_nt(dx2b, wdn_ref[k])
            g = g_s[k]
            sg = _sigmoid(g)
            dgu_ref[k] = ((da * u_s[k]) * (sg * (1.0 + g * (1.0 - sg)))).astype(BF)
            dgu_ref[k + nh] = (da * (g * sg)).astype(BF)

        dh2 = _dot_nt(dgu_ref[0], wgu_ref[0])
        for k in range(1, 2 * nh):
            dh2 = dh2 + _dot_nt(dgu_ref[k], wgu_ref[k])

        accs[0] += _fold(dh2 * xh2)
        dxh = dh2 * gffn
        dx1 = dx2_s[...] + r2 * (dxh - xh2 * jnp.mean(dxh * xh2, axis=-1, keepdims=True))
        dx1_ref[...] = dx1
        dx1b_ref[...] = dx1.astype(BF)

        @pl.when(i == nt - 1)
        def _():
            acc_ref[...] = jnp.zeros((8, D), F32)
            for j in range(3):
                acc_ref[j:j + 1, :] = jnp.sum(accs[j], axis=0, keepdims=True)

    tile = lambda w: pl.BlockSpec((tm, w), lambda i: (i, 0))
    vec = _const_spec((1, D))
    return pl.pallas_call(
        body, name="ffn", grid=(nt,),
        in_specs=[tile(D), tile(D), vec, vec, _const_spec(wgu.shape), _const_spec(wdn.shape)],
        out_specs=[tile(D), tile(D), tile(D),
                   pl.BlockSpec((nh, tm, fb), lambda i: (0, i, 0)),
                   pl.BlockSpec((2 * nh, tm, fb), lambda i: (0, i, 0)),
                   tile(D), pl.BlockSpec((8, D), lambda i: (0, 0))],
        out_shape=[jax.ShapeDtypeStruct((s, D), F32), jax.ShapeDtypeStruct((s, D), BF),
                   jax.ShapeDtypeStruct((s, D), BF), jax.ShapeDtypeStruct((nh, s, fb), BF),
                   jax.ShapeDtypeStruct((2 * nh, s, fb), BF), jax.ShapeDtypeStruct((s, D), BF),
                   jax.ShapeDtypeStruct((8, D), F32)],
        scratch_shapes=[pltpu.VMEM((nh, tm, fb), F32), pltpu.VMEM((nh, tm, fb), F32), pltpu.VMEM((tm, D), F32),
                        pltpu.VMEM((3, RC, D), F32)],
        compiler_params=_params(("arbitrary",)),
    )(x1, tgt, gffn, gfin, wgu, wdn)


def _mixer_bwd_pre_call(dx1b, proj, pa, pb, lng, lnb, wtr, wtrt, bias, wpa, wpb, wo, parts):
    s = dx1b.shape[0]
    tm = min(TM_MIX, s)
    nt = s // tm
    pw = proj.shape[1]
    n = len(parts)

    def body(dx1b_ref, uv_ref, gg_ref, pa_ref, pb_ref, lng_ref, lnb_ref, wtr_ref, wtrt_ref, bias_ref,
             wpa_ref, wpb_ref, wo_ref, *rest):
        ex_in = rest[:n]
        dproj_ref, dya_ref, dpa_ref, dpb_ref, vec_ref, dws_ref, dbs_ref = rest[n:n + 7]
        ex_out = rest[n + 7:2 * n + 7]
        vn_s, mx_s, dmx_s, dvn_s, accs, dbs_s, e_send, e_recv, e_local = rest[2 * n + 7:]
        i = pl.program_id(0)
        exchange = _Exchange(ex_in, ex_out, e_send, e_recv, e_local)

        @pl.when(i == 0)
        def _():
            exchange.start()
            accs[...] = jnp.zeros(accs.shape, F32)
            dbs_s[...] = jnp.zeros(dbs_s.shape, F32)
            dws_ref[...] = jnp.zeros(dws_ref.shape, F32)

        dm = _dot_nt(dx1b_ref[...], wo_ref[...])
        sa = _sigmoid(gg_ref[:, 0:D])
        sb = _sigmoid(gg_ref[:, D:2 * D])
        dpa = dm * sa
        dpb = dm * sb
        dpab = dpa.astype(BF)
        dpbb = dpb.astype(BF)
        dpa_ref[...] = dpab
        dpb_ref[...] = dpbb
        dproj_ref[:, 2 * D:3 * D] = ((dpa * pa_ref[...].astype(F32)) * (1.0 - sa)).astype(BF)
        dproj_ref[:, 3 * D:4 * D] = ((dpb * pb_ref[...].astype(F32)) * (1.0 - sb)).astype(BF)

        dya_ref[...] = _dot_nt(dpab, wpa_ref[...])
        dyb = _dot_nt(dpbb, wpb_ref[...])

        lng = lng_ref[...]
        gv, dgelu_v = _gelu_grad(uv_ref[:, D:2 * D])
        dv = gv - jnp.mean(gv, axis=-1, keepdims=True)
        rstd = lax.rsqrt(jnp.mean(dv * dv, axis=-1, keepdims=True) + EPS)
        xh = dv * rstd
        vn_s[...] = (xh * lng + lnb_ref[...]).astype(BF)

        nc = tm // CHUNK
        for c in range(nc):
            rs = slice(c * CHUNK, (c + 1) * CHUNK)
            for g in range(NGRP):
                cs = slice(g * GW, (g + 1) * GW)
                mx_s[rs, cs] = _dot(wtr_ref[g], vn_s[rs, cs])

        gu, dgelu_u = _gelu_grad(uv_ref[:, 0:D])
        mixed = mx_s[...] + jnp.concatenate([bias_ref[...]] * nc, axis=0)
        dproj_ref[:, 0:D] = ((dyb * mixed) * dgelu_u).astype(BF)
        dmx = dyb * gu
        dmx_s[...] = dmx.astype(BF)
        dbs_s[...] += jnp.sum(dmx.reshape(nc, CHUNK, D), axis=0)

        for c in range(nc):
            rs = slice(c * CHUNK, (c + 1) * CHUNK)
            for g in range(NGRP):
                cs = slice(g * GW, (g + 1) * GW)
                dvn_s[rs, cs] = _dot(wtrt_ref[g], dmx_s[rs, cs])
                dws_ref[g] += _dot_nt(dmx_s[rs, cs], vn_s[rs, cs])

        dvn = dvn_s[...]
        accs[0] += _fold(dvn * xh)
        accs[1] += _fold(dvn)
        dxh = dvn * lng
        m1 = jnp.mean(dxh, axis=-1, keepdims=True)
        m2 = jnp.mean(dxh * xh, axis=-1, keepdims=True)
        dproj_ref[:, D:2 * D] = ((rstd * (dxh - m1 - xh * m2)) * dgelu_v).astype(BF)

        @pl.when(i == nt - 1)
        def _():
            vec_ref[...] = jnp.zeros((8, D), F32)
            for j in range(2):
                vec_ref[j:j + 1, :] = jnp.sum(accs[j], axis=0, keepdims=True)
            row = lax.broadcasted_iota(jnp.int32, (CHUNK, CHUNK), 0)
            col = lax.broadcasted_iota(jnp.int32, (CHUNK, CHUNK), 1)
            for g in range(NGRP):
                dws_ref[g] = jnp.where(row >= col, dws_ref[g], 0.0)
                gs = jnp.sum(dbs_s[:, g * GW:(g + 1) * GW], axis=1, keepdims=True)
                dbs_ref[:, g * GW:(g + 1) * GW] = jnp.broadcast_to(gs, (CHUNK, GW))
            exchange.finish()

    tile = lambda w: pl.BlockSpec((tm, w), lambda i: (i, 0))
    vec = _const_spec((1, D))
    any_spec = pl.BlockSpec(memory_space=pl.ANY)
    outs = pl.pallas_call(
        body, name="mixer_bwd_pre", grid=(nt,),
        in_specs=[tile(D), pl.BlockSpec((tm, 2 * D), lambda i: (i, 1)), pl.BlockSpec((tm, 2 * D), lambda i: (i, 2)),
                  tile(D), tile(D), vec, vec, _const_spec(wtr.shape), _const_spec(wtrt.shape),
                  _const_spec(bias.shape), _const_spec((D, D)), _const_spec((D, D)), _const_spec((D, D))]
        + [any_spec] * n,
        out_specs=[tile(4 * D), tile(D), tile(D), tile(D), pl.BlockSpec((8, D), lambda i: (0, 0)),
                   pl.BlockSpec((NGRP, CHUNK, CHUNK), lambda i: (0, 0, 0)),
                   pl.BlockSpec((CHUNK, D), lambda i: (0, 0))] + [any_spec] * n,
        out_shape=[jax.ShapeDtypeStruct((s, 4 * D), BF), jax.ShapeDtypeStruct((s, D), F32),
                   jax.ShapeDtypeStruct((s, D), BF), jax.ShapeDtypeStruct((s, D), BF),
                   jax.ShapeDtypeStruct((8, D), F32), jax.ShapeDtypeStruct((NGRP, CHUNK, CHUNK), F32),
                   jax.ShapeDtypeStruct((CHUNK, D), F32)]
        + [jax.ShapeDtypeStruct(p.shape, p.dtype) for p in parts],
        scratch_shapes=[pltpu.VMEM((tm, D), BF), pltpu.VMEM((tm, D), F32), pltpu.VMEM((tm, D), BF),
                        pltpu.VMEM((tm, D), F32), pltpu.VMEM((2, RC, D), F32), pltpu.VMEM((CHUNK, D), F32)]
        + _gather_sems(n),
        compiler_params=_params(("arbitrary",)),
    )(dx1b, proj, proj, pa, pb, lng, lnb, wtr, wtrt, bias, wpa, wpb, wo, *parts)
    return outs[:7], outs[7:]


def _mixer_bwd_seq_call(dprojb, dya, proj, h, x, dx1, gates, gmix, w_in_all, cw, lam, wax, parts):
    s = dya.shape[0]
    tm = min(TM_MIX, s)
    nt = s // tm
    tb = tm // 8
    n = len(parts)
    nb, _, wb = w_in_all.shape

    def body(dpb_ref, dya_ref, xg_ref, xh8_ref, h_ref, hh8_ref, x_ref, dx1_ref, xc_s, r_s, ig_s, a_s, m_s,
             gmix_ref, win_ref, cw_ref, lam_ref, wax_ref, *rest):
        ex_in = rest[:n]
        dpa_ref, dx_ref, vec_ref, dwax_ref = rest[n:n + 4]
        ex_out = rest[n + 4:2 * n + 4]
        lm_s, dh_s, dxc_s, c_s, accs, e_send, e_recv, e_local = rest[2 * n + 4:]
        i = pl.program_id(0)
        exchange = _Exchange(ex_in, ex_out, e_send, e_recv, e_local)

        @pl.when(i == 0)
        def _():
            exchange.start()
            accs[...] = jnp.zeros(accs.shape, F32)
            dwax_ref[...] = jnp.zeros(dwax_ref.shape, F32)
            c_s[...] = jnp.zeros((8, D), F32)
            dxc_s[tm:tm + 8, :] = jnp.zeros((8, D), F32)

        first_tile = i == nt - 1
        prev8 = jnp.where(first_tile, 0.0, xh8_ref[...])
        hprev8 = jnp.where(first_tile, 0.0, hh8_ref[...])
        lamv = lam_ref[...]
        sp = _softplus(-lamv)
        hv = h_ref[...]
        g, dg = _gelu_grad(xg_ref[:, D:2 * D])
        dya = dya_ref[...]
        lm_s[...] = dya * g
        drg = ((dya * hv) * dg).astype(BF)
        dpa_ref[:, D:2 * D] = drg
        dpa_ref[:, 2 * D:6 * D] = dpb_ref[...]

        def dpb_block(k):
            return _dot_nt(dpb_ref[:, k * wb - 2 * D:(k + 1) * wb - 2 * D], win_ref[k])
        dh = (_dot_nt(drg[:, 0:2 * wb - D], win_ref[1, :, D - wb:wb])
              + _dot_nt(drg[:, 2 * wb - D:D], win_ref[2, :, 0:2 * D - 2 * wb])
              + _dot_nt(dpb_ref[:, 0:3 * wb - 2 * D], win_ref[2, :, 2 * D - 2 * wb:wb]))
        for k in range(3, 6):
            dh = dh + dpb_block(k)
        dh_s[...] = dh

        c_s[0:1, :] = _scan_bwd(a_s, lm_s, c_s[0:1, :], tm)

        hprev = _shift_back(hprev8, h_ref[...], 1)
        for q in range(NQ):
            cs = slice(q * QW, (q + 1) * QW)
            r = r_s[:, cs].astype(F32)
            ig = ig_s[:, cs].astype(F32)
            a = a_s[:, cs]
            m = m_s[:, cs].astype(F32)
            lm = lm_s[:, cs]
            xq = xc_s[:, cs].astype(F32)
            dixc = lm * m
            dla = (lm * hprev[:, cs]) * a - ((lm * (ig * xq)) * (a * a)) / m
            accs[3, :, cs] += _fold(dla * r)
            dza = (dla * (-RG_C * sp[:, cs])) * (r * (1.0 - r))
            dzx = (dixc * xq) * (ig * (1.0 - ig))
            accs[1, :, cs] += _fold(dza)
            accs[2, :, cs] += _fold(dzx)
            dz = jnp.concatenate([dza, dzx], axis=1).astype(BF)
            dxc_s[0:tm, cs] = dixc * ig + _dot_nt(dz, wax_ref[q])
            dwax_ref[q] += _dot_tn(xc_s[:, cs], dz)

        cur = dxc_s[0:tm, :]
        nx = dxc_s[tm:tm + 8, :]
        drx = cw_ref[3:4, :] * cur
        for j in (1, 2, 3):
            drx = drx + cw_ref[3 - j:4 - j, :] * _shift_fwd(cur, nx, j)
        accs[0] += _fold(cur)
        rx = xg_ref[:, 0:D]
        accs[7] += _fold(cur * rx)
        for j in (1, 2, 3):
            accs[7 - j] += _fold(cur * _shift_back(prev8, rx, j))
        dxc_s[tm:tm + 8, :] = cur[0:8, :]

        drxb = drx.astype(BF)
        dpa_ref[:, 0:D] = drxb
        dh = dh_s[...] + _dot_nt(drxb[:, 0:wb], win_ref[0]) + _dot_nt(drxb[:, wb:D], win_ref[1, :, 0:D - wb])
        for k in range(6, nb):
            dh = dh + dpb_block(k)
        xx = x_ref[...]
        rn = lax.rsqrt(jnp.mean(xx * xx, axis=-1, keepdims=True) + EPS)
        xhn = xx * rn
        accs[8] += _fold(dh * xhn)
        dxh = dh * gmix_ref[...]
        dx_ref[...] = dx1_ref[...] + rn * (dxh - xhn * jnp.mean(dxh * xhn, axis=-1, keepdims=True))

        @pl.when(i == nt - 1)
        def _():
            vec_ref[...] = jnp.zeros((16, D), F32)
            for j in range(9):
                vec_ref[j:j + 1, :] = jnp.sum(accs[j], axis=0, keepdims=True)
            vec_ref[3:4, :] = vec_ref[3:4, :] * (RG_C * _sigmoid(-lamv))
            exchange.finish()

    rev = lambda w: pl.BlockSpec((tm, w), lambda i: (nt - 1 - i, 0))
    halo = pl.BlockSpec((8, D), lambda i: (jnp.maximum((nt - 1 - i) * tb - 1, 0), 0))
    vec = _const_spec((1, D))
    any_spec = pl.BlockSpec(memory_space=pl.ANY)
    outs = pl.pallas_call(
        body, name="mixer_bwd_seq", grid=(nt,),
        in_specs=[rev(4 * D), rev(D), rev(2 * D), halo, rev(D), halo] + [rev(D)] * 7
        + [vec, _const_spec(w_in_all.shape), _const_spec((4, D)), vec, _const_spec(wax.shape)] + [any_spec] * n,
        out_specs=[rev(6 * D), rev(D), pl.BlockSpec((16, D), lambda i: (0, 0)),
                   pl.BlockSpec((NQ, QW, 2 * QW), lambda i: (0, 0, 0))] + [any_spec] * n,
        out_shape=[jax.ShapeDtypeStruct((s, 6 * D), BF), jax.ShapeDtypeStruct((s, D), F32),
                   jax.ShapeDtypeStruct((16, D), F32), jax.ShapeDtypeStruct((NQ, QW, 2 * QW), F32)]
        + [jax.ShapeDtypeStruct(p.shape, p.dtype) for p in parts],
        scratch_shapes=[pltpu.VMEM((tm, D), F32), pltpu.VMEM((tm, D), F32),
                        pltpu.VMEM((tm + 8, D), F32), pltpu.VMEM((8, D), F32), pltpu.VMEM((9, RC, D), F32)]
        + _gather_sems(n),
        compiler_params=_params(("arbitrary",)),
    )(dprojb, dya, proj, proj, h, h, x, dx1, *gates, gmix, w_in_all, cw, lam, wax, *parts)
    return outs[:4], outs[4:]


def _device_of(d):
    return (d // 4, lax.rem(d // 2, 2), lax.rem(d, 2))


_DW_PLAN = (("C", 1), ("C", 2), ("A", 1), ("C", 3), ("A", 2), ("S", 0), ("A", 3), ("O", 0))
_DW_FLIPS = tuple({"C": 2 * j + 1, "A": 2 * j, "S": 1, "O": 0}[kind] for kind, j in _DW_PLAN)


def _dw_exchange_call(name, order, a, b, a_spec, b_spec, k1, n1, s, small):
    ts = min(TS_DW, s)
    ns = s // ts
    nstep = len(_DW_PLAN)

    def slab(i, order_ref):
        return order_ref[i]

    def body(order_ref, a_ref, b_ref, g_ref, own_ref, recv_ref, gsum_ref, acc, sbuf, stage, send_sems, recv_sems,
             st_send, st_recv, local_sem, rbuf, gacc, send1, recv1, send2, recv2):
        i = pl.program_id(0)
        j = pl.program_id(1)
        me = _me()
        mi = _lin(me)
        sibling = _flip(me, 1)
        allreduce = _SmallAllReduce(g_ref, gacc, rbuf, send1, recv1, send2, recv2)
        pl.when((i == 0) & (j == 0))(allreduce.scatter)
        pl.when((i == nstep // 2) & (j == 0))(allreduce.reduce)
        p = _dot_tn(a_ref[...], b_ref[...])

        @pl.when(j == 0)
        def _():
            acc[...] = p

        @pl.when(j > 0)
        def _():
            acc[...] += p

        def send(step):
            kind, jj = _DW_PLAN[step]
            src = sbuf.at[step % 2]
            if kind == "C":
                return pltpu.make_async_remote_copy(
                    src_ref=src, dst_ref=stage.at[jj - 1], send_sem=st_send.at[jj - 1], recv_sem=st_recv.at[jj - 1],
                    device_id=sibling, device_id_type=MESH)
            to = sibling if kind == "S" else _flip(me, 2 * jj)
            return pltpu.make_async_remote_copy(
                src_ref=src, dst_ref=recv_ref.at[mi], send_sem=send_sems.at[jj], recv_sem=recv_sems.at[jj],
                device_id=to, device_id_type=MESH)

        def arrival(jj):
            frm = sibling if jj == 0 else _flip(me, 2 * jj)
            return pltpu.make_async_remote_copy(
                src_ref=sbuf.at[0], dst_ref=recv_ref.at[_lin(frm)], send_sem=send_sems.at[jj],
                recv_sem=recv_sems.at[jj], device_id=frm, device_id_type=MESH)

        def staged(jj):
            return pltpu.make_async_remote_copy(
                src_ref=sbuf.at[0], dst_ref=stage.at[jj - 1], send_sem=st_send.at[jj - 1], recv_sem=st_recv.at[jj - 1],
                device_id=sibling, device_id_type=MESH)

        for step, (kind, jj) in enumerate(_DW_PLAN):
            @pl.when((i == step) & (j == ns - 1))
            def _(step=step, kind=kind, jj=jj):
                if step >= 2:
                    send(step - 2).wait_send()
                if kind == "A":
                    staged(jj).wait_recv()
                    sbuf[step % 2] = (acc[...] + stage[jj - 1].astype(F32)).astype(BF)
                else:
                    sbuf[step % 2] = acc[...].astype(BF)
                if kind != "O":
                    send(step).start()
                else:
                    own_ref[...] = acc[...]
                    mine = pltpu.make_async_copy(sbuf.at[step % 2], recv_ref.at[mi], local_sem)
                    mine.start()
                    send(step - 1).wait_send()
                    for q in range(4):
                        arrival(q).wait_recv()
                    mine.wait()
                    allreduce.finish()
                    gsum_ref[...] = gacc[...]

    vmem = pl.BlockSpec(memory_space=pltpu.VMEM)
    grid_spec = pltpu.PrefetchScalarGridSpec(
        num_scalar_prefetch=1, grid=(nstep, ns),
        in_specs=[a_spec(ts, slab), b_spec(ts, slab), vmem],
        out_specs=[pl.BlockSpec((k1, n1), lambda i, j, o: (0, 0)), pl.BlockSpec(memory_space=pl.ANY), vmem],
        scratch_shapes=[pltpu.VMEM((k1, n1), F32), pltpu.VMEM((2, k1, n1), BF), pltpu.VMEM((3, k1, n1), BF),
                        pltpu.SemaphoreType.DMA((4,)), pltpu.SemaphoreType.DMA((4,)),
                        pltpu.SemaphoreType.DMA((3,)), pltpu.SemaphoreType.DMA((3,)),
                        pltpu.SemaphoreType.DMA(()), pltpu.VMEM((NDEV, SMALL_PER, D), F32),
                        pltpu.VMEM((SMALL_ROWS, D), F32)]
        + [pltpu.SemaphoreType.DMA((NDEV - 1,))] * 4)
    return pl.pallas_call(
        body, name=name, grid_spec=grid_spec,
        out_shape=[jax.ShapeDtypeStruct((k1, n1), F32), jax.ShapeDtypeStruct((NDEV, k1, n1), BF),
                   jax.ShapeDtypeStruct((SMALL_ROWS, D), F32)],
        compiler_params=_params(("arbitrary", "arbitrary")),
    )(order, a, b, small)


def _dw_plain_call(name, me, a, b, a_spec, b_spec, split, k1, n1, s):
    nb = NDEV // split
    r = k1 // split
    ts = min(TS_DW, s)
    ns = s // ts

    def slab(i, me_ref):
        return i

    def body(me_ref, a_ref, b_ref, own_ref, part_ref, acc):
        i = pl.program_id(0)
        j = pl.program_id(1)
        p = _dot_tn(a_ref[...], b_ref[...])

        @pl.when(j == 0)
        def _():
            acc[...] = p

        @pl.when(j > 0)
        def _():
            acc[...] += p

        @pl.when(j == ns - 1)
        def _():
            part_ref[...] = acc[...].astype(BF)

            @pl.when(i == me_ref[0] // split)
            def _():
                off = pl.multiple_of(lax.rem(me_ref[0], split) * r, RC)
                own_ref[...] = acc[pl.ds(off, r), :]

    grid_spec = pltpu.PrefetchScalarGridSpec(
        num_scalar_prefetch=1, grid=(nb, ns),
        in_specs=[a_spec(ts, slab), b_spec(ts, slab)],
        out_specs=[pl.BlockSpec((r, n1), lambda i, j, me_ref: (0, 0)),
                   pl.BlockSpec((None, k1, n1), lambda i, j, me_ref: (i, 0, 0))],
        scratch_shapes=[pltpu.VMEM((k1, n1), F32)])
    own, part = pl.pallas_call(
        body, name=name, grid_spec=grid_spec,
        out_shape=[jax.ShapeDtypeStruct((r, n1), F32), jax.ShapeDtypeStruct((nb, k1, n1), BF)],
        compiler_params=_params(("arbitrary", "arbitrary")),
    )(me, a, b)
    return own, part.reshape(NDEV, r, n1)


def _rows2d(w):
    return lambda ts, slab: pl.BlockSpec((ts, w), lambda i, j, me_ref: (j, 0))


def _cols2d(w):
    return lambda ts, slab: pl.BlockSpec((ts, w), lambda i, j, me_ref: (j, slab(i, me_ref)))


def _blk3d(w):
    return lambda ts, slab: pl.BlockSpec((None, ts, w), lambda i, j, me_ref: (slab(i, me_ref), j, 0))


_BC1 = 1.0 - ADAM_B1 ** ADAM_STEP
_BC2 = 1.0 - ADAM_B2 ** ADAM_STEP


def _adamw_math(w, g, m, v):
    m = ADAM_B1 * m + (1.0 - ADAM_B1) * g
    v = ADAM_B2 * v + (1.0 - ADAM_B2) * (g * g)
    m_hat = m / _BC1
    v_hat = v / _BC2
    delta = -ADAM_LR * (m_hat / (jnp.sqrt(v_hat) + ADAM_EPS) + ADAM_WD * w)
    return delta, m, v


def _row_tile(r):
    for t in (256, 176, 128, 64, 32, 16, 8):
        if r % t == 0:
            return t
    return r


def _reduce_adamw_call(name, sel, own, recv, w, m, v):
    r, c = own.shape
    tr = _row_tile(r)

    def body(sel_ref, own_ref, recv_ref, w_ref, m_ref, v_ref, g_ref, d_ref, nm_ref, nv_ref):
        g = jnp.zeros((tr, c), F32)
        for sdev in range(NDEV):
            part = jnp.where(sel_ref[sdev] == 1, recv_ref[sdev].astype(F32), 0.0)
            g = g + jnp.where(sel_ref[sdev] == 2, own_ref[...], part)
        g_ref[...] = g
        d_ref[...], nm_ref[...], nv_ref[...] = _adamw_math(w_ref[...], g, m_ref[...], v_ref[...])

    tile = pl.BlockSpec((tr, c), lambda i, me_ref: (i, 0))
    grid_spec = pltpu.PrefetchScalarGridSpec(
        num_scalar_prefetch=1, grid=(r // tr,),
        in_specs=[tile, pl.BlockSpec((NDEV, tr, c), lambda i, me_ref: (0, i, 0)), tile, tile, tile],
        out_specs=[tile] * 4)
    return pl.pallas_call(
        body, name=name, grid_spec=grid_spec,
        out_shape=[jax.ShapeDtypeStruct((r, c), F32)] * 4,
        compiler_params=_params(("parallel",)),
    )(sel, own, recv, w, m, v)


def _adamw_call(name, w, g, m, v):
    r, c = w.shape
    tr = _row_tile(r)

    def body(w_ref, g_ref, m_ref, v_ref, d_ref, nm_ref, nv_ref):
        d_ref[...], nm_ref[...], nv_ref[...] = _adamw_math(w_ref[...], g_ref[...], m_ref[...], v_ref[...])

    tile = pl.BlockSpec((tr, c), lambda i: (i, 0))
    return pl.pallas_call(
        body, name=name, grid=(r // tr,), in_specs=[tile] * 4, out_specs=[tile] * 3,
        out_shape=[jax.ShapeDtypeStruct((r, c), F32)] * 3,
        compiler_params=_params(("parallel",)),
    )(w, g, m, v)


def _me():
    return lax.axis_index("x"), lax.axis_index("y"), lax.axis_index("c")


def _flip(pos, r):
    x, y, c = pos
    return (1 - x if r & 4 else x, 1 - y if r & 2 else y, 1 - c if r & 1 else c)


def _lin(pos):
    return pos[0] * 4 + pos[1] * 2 + pos[2]


class _SmallAllReduce:
    def __init__(self, g_ref, out_ref, rbuf, send1, recv1, send2, recv2):
        self.g, self.out, self.rbuf = g_ref, out_ref, rbuf
        self.sems = (send1, recv1, send2, recv2)
        self.me = _me()
        self.mi = _lin(self.me)

    @staticmethod
    def _rows(d):
        return pl.ds(pl.multiple_of(d * SMALL_PER, 8), SMALL_PER)

    def _scatter(self, r, outgoing):
        peer = _flip(self.me, r)
        src, dst = (_lin(peer), self.mi) if outgoing else (self.mi, _lin(peer))
        return pltpu.make_async_remote_copy(
            src_ref=self.g.at[self._rows(src)], dst_ref=self.rbuf.at[dst],
            send_sem=self.sems[0].at[r - 1], recv_sem=self.sems[1].at[r - 1], device_id=peer, device_id_type=MESH)

    def _spread(self, r, outgoing):
        peer = _flip(self.me, r)
        rows = self._rows(self.mi if outgoing else _lin(peer))
        return pltpu.make_async_remote_copy(
            src_ref=self.out.at[rows], dst_ref=self.out.at[rows],
            send_sem=self.sems[2].at[r - 1], recv_sem=self.sems[3].at[r - 1], device_id=peer, device_id_type=MESH)

    def scatter(self):
        for r in range(1, NDEV):
            self._scatter(r, True).start()
        self.rbuf[self.mi] = self.g[self._rows(self.mi), :]

    def reduce(self):
        for r in range(1, NDEV):
            self._scatter(r, False).wait_recv()
        for r in range(1, NDEV):
            self._scatter(r, True).wait_send()
        tot = self.rbuf[0]
        for d in range(1, NDEV):
            tot = tot + self.rbuf[d]
        self.out[self._rows(self.mi), :] = tot
        for r in range(1, NDEV):
            self._spread(r, True).start()

    def finish(self):
        for r in range(1, NDEV):
            self._spread(r, False).wait_recv()
        for r in range(1, NDEV):
            self._spread(r, True).wait_send()


def _head_blocks(w):
    z = jnp.zeros((64, 64), w.dtype)
    groups = []
    for q in range(NQ):
        rows = [jnp.concatenate([w[4 * q + a] if a == b else z for b in range(4)], axis=1) for a in range(4)]
        groups.append(jnp.concatenate(rows, axis=0))
    return jnp.stack(groups)


def _head_unblocks(g):
    return jnp.stack([g[q, 64 * a:64 * a + 64, 64 * a:64 * a + 64] for q in range(NQ) for a in range(4)])


def _local_step(x, tgt, p, me, order, order_dw):
    s = x.shape[0]
    vec = lambda a: a.reshape(1, D)
    gmix, gffn, gfin = vec(p["norm_mix_g"]), vec(p["norm_ffn_g"]), vec(p["norm_final_g"])
    cb, ba, bx, lam = vec(p["conv_b"]), vec(p["rg_ba"]), vec(p["rg_bx"]), vec(p["rg_lambda"])
    lng, lnb = vec(p["sgu_ln_g"]), vec(p["sgu_ln_b"])
    wax = jnp.concatenate([_head_blocks(p["rg_wa"]), _head_blocks(p["rg_wx"])], axis=2).astype(BF)
    tril = jnp.tril(jnp.ones((CHUNK, CHUNK), bool))
    ws = jnp.where(tril[None], p["sgu_ws"], 0.0)
    wtr = ws.astype(BF)
    wtrt = jnp.swapaxes(ws, 1, 2).astype(BF)
    bias = jnp.repeat(p["sgu_bs"].T, GW, axis=1)
    shard = {k: p[k].astype(BF) for k in _BIG}

    proj, h1, w_in, (wpa, wpb, wo, cw) = _proj_gather_call(
        x, gmix, shard["w_in"], order, [shard["w_proj_a"], shard["w_proj_b"], shard["w_out"], p["conv_w"]])
    wpa, wpb, wo = (t.reshape(D, D) for t in (wpa, wpb, wo))
    cw = jnp.swapaxes(cw, 0, 1).reshape(4, D)
    (h, pa, pb, x1, *gates, ya, yb, mg), (wgu, wdn) = _mixer_fwd_call(
        proj, x, cw, cb, ba, bx, lam, lng, lnb, wax, wtr, bias, wpa, wpb, wo, [shard["w_gate_up"], shard["w_down"]])
    wdn = wdn.reshape(NDEV // 2, -1, D)
    nb, _, wb = w_in.shape
    fb = wgu.shape[2]
    nh = wdn.shape[0]
    dx1, dx1b, h2, act, dgu, dx2b, facc = _ffn_call(x1, tgt, gffn, gfin, wgu, wdn)
    assert nb == NDEV and 2 * nh == NDEV
    own_gu, part_gu = _dw_plain_call("dw_gate_up", me, dgu, h2, _blk3d(fb), _rows2d(D), 1, fb, D, s)
    own_dn, part_dn = _dw_plain_call("dw_down", me, act, dx2b, _blk3d(fb), _rows2d(D), 2, fb, D, s)
    (dprojb, dya, dpa, dpb, bvec, dws, dbs), (recv_dn,) = _mixer_bwd_pre_call(
        dx1b, proj, pa, pb, lng, lnb, wtr, wtrt, bias, wpa, wpb, wo, [part_dn])
    own_pa, part_pa = _dw_plain_call("dw_proj_a", me, ya, dpa, _rows2d(D), _rows2d(D), NDEV, D, D, s)
    own_pb, part_pb = _dw_plain_call("dw_proj_b", me, yb, dpb, _rows2d(D), _rows2d(D), NDEV, D, D, s)
    own_wo, part_wo = _dw_plain_call("dw_out", me, mg, dx1b, _rows2d(D), _rows2d(D), NDEV, D, D, s)
    (dproj, dx, svec, dwax), (recv_gu, recv_pa, recv_pb, recv_wo) = _mixer_bwd_seq_call(
        dprojb, dya, proj, h, x, dx1, gates, gmix, w_in, cw, lam, wax, [part_gu, part_pa, part_pb, part_wo])
    small = {
        "norm_mix_g": svec[8], "norm_ffn_g": facc[0], "norm_final_g": facc[1],
        "conv_b": svec[0], "rg_ba": svec[1], "rg_bx": svec[2], "rg_lambda": svec[3],
        "sgu_ln_g": bvec[0], "sgu_ln_b": bvec[1],
        "rg_wa": _head_unblocks(dwax[:, :, 0:QW]), "rg_wx": _head_unblocks(dwax[:, :, QW:2 * QW]),
        "sgu_ws": dws, "sgu_bs": dbs[:, ::GW].T,
    }
    packed = _pack_small(small, svec[4:8], facc[2])
    own_in, recv_in, gsum = _dw_exchange_call("dw_in", order_dw, h1, dproj, _rows2d(D), _cols2d(wb), D, wb, s, packed)
    dw = {
        "w_gate_up": (own_gu, recv_gu), "w_down": (own_dn, recv_dn), "w_proj_a": (own_pa, recv_pa),
        "w_proj_b": (own_pb, recv_pb), "w_out": (own_wo, recv_wo), "w_in": (own_in, recv_in),
    }
    return gsum, dx, dw


_BIG = ("w_in", "w_gate_up", "w_down", "w_proj_a", "w_proj_b", "w_out")
_VEC_ROWS = ("norm_mix_g", "norm_ffn_g", "norm_final_g", "conv_b", "rg_ba", "rg_bx", "rg_lambda",
             "sgu_ln_g", "sgu_ln_b", "sgu_bs")
_WEIGHTS = ("norm_mix_g", "w_in", "conv_w", "conv_b", "rg_wa", "rg_ba", "rg_wx", "rg_bx", "rg_lambda",
            "sgu_ln_g", "sgu_ln_b", "sgu_ws", "sgu_bs", "w_proj_a", "w_proj_b", "w_out", "norm_ffn_g",
            "w_gate_up", "w_down", "norm_final_g")


def _pack_small(t, conv_w, extra=None):
    extra = jnp.zeros((1, D), F32) if extra is None else extra.reshape(1, D)
    head = jnp.concatenate([t[k].reshape(1, D) for k in _VEC_ROWS] + [conv_w, extra, jnp.zeros((1, D), F32)], axis=0)
    return jnp.concatenate([head, t["rg_wa"].reshape(64, D), t["rg_wx"].reshape(64, D), t["sgu_ws"].reshape(128, D),
                            jnp.zeros((SMALL_ROWS - 272, D), F32)], axis=0)


def _unpack_small(a):
    out = {k: a[j] for j, k in enumerate(_VEC_ROWS)}
    out["conv_w"] = a[10:14]
    out["rg_wa"] = a[16:80].reshape(16, 64, 64)
    out["rg_wx"] = a[80:144].reshape(16, 64, 64)
    out["sgu_ws"] = a[144:272].reshape(NGRP, CHUNK, CHUNK)
    return out


def kernel(x, norm_mix_g, w_in, conv_w, conv_b, rg_wa, rg_ba, rg_wx, rg_bx, rg_lambda, sgu_ln_g, sgu_ln_b, sgu_ws, sgu_bs, w_proj_a, w_proj_b, w_out, norm_ffn_g, w_gate_up, w_down, norm_final_g, loss_target, m_norm_mix_g, m_w_in, m_conv_w, m_conv_b, m_rg_wa, m_rg_ba, m_rg_wx, m_rg_bx, m_rg_lambda, m_sgu_ln_g, m_sgu_ln_b, m_sgu_ws, m_sgu_bs, m_w_proj_a, m_w_proj_b, m_w_out, m_norm_ffn_g, m_w_gate_up, m_w_down, m_norm_final_g, v_norm_mix_g, v_w_in, v_conv_w, v_conv_b, v_rg_wa, v_rg_ba, v_rg_wx, v_rg_bx, v_rg_lambda, v_sgu_ln_g, v_sgu_ln_b, v_sgu_ws, v_sgu_bs, v_w_proj_a, v_w_proj_b, v_w_out, v_norm_ffn_g, v_w_gate_up, v_w_down, v_norm_final_g):
    args = dict(locals())
    w = {k: args[k] for k in _WEIGHTS}
    m = {k: args["m_" + k] for k in _WEIGHTS}
    v = {k: args["v_" + k] for k in _WEIGHTS}
    for d in (w, m, v):
        for k in _WEIGHTS:
            if k != "norm_final_g":
                d[k] = d[k][0]
    me = _lin(_me())
    me1 = me.reshape(1).astype(jnp.int32)

    order = jnp.bitwise_xor(me, jnp.array(_PASS_FLIPS, jnp.int32)).astype(jnp.int32)
    order_dw = jnp.bitwise_xor(me, jnp.array(_DW_FLIPS, jnp.int32)).astype(jnp.int32)

    gsum, dx, dw = _local_step(x[0], loss_target[0], w, me1, order, order_dw)

    peer = jnp.bitwise_xor(jnp.arange(NDEV, dtype=jnp.int32), me)
    sel_direct = jnp.where(peer == 0, 2, 1).astype(jnp.int32)
    sel_two_level = jnp.where(peer == 0, 2, jnp.where((peer == 1) | (peer % 2 == 0), 1, 0)).astype(jnp.int32)
    grads, delta, new_m, new_v = {}, {}, {}, {}
    for k in _BIG:
        own, recv = dw[k]
        flip = own.shape != w[k].shape
        wmv = [jnp.swapaxes(t, 0, 1) if flip else t for t in (w[k], m[k], v[k])]
        res = _reduce_adamw_call("adamw_" + k, sel_two_level if k == "w_in" else sel_direct, own, recv, *wmv)
        grads[k], delta[k], new_m[k], new_v[k] = (jnp.swapaxes(t, 0, 1) if flip else t for t in res)

    loss = (0.5 / D) * jnp.sum(gsum[14])
    zc = jnp.zeros((4, D), F32)
    d_s, m_s, v_s = _adamw_call("adamw_small", _pack_small(w, zc), gsum, _pack_small(m, zc), _pack_small(v, zc))
    gs, ds, ms, vs = _unpack_small(gsum), _unpack_small(d_s), _unpack_small(m_s), _unpack_small(v_s)
    g_cw = lax.dynamic_slice(gs["conv_w"], (0, me * 128), (4, 128))
    ds["conv_w"], ms["conv_w"], vs["conv_w"] = _adamw_call("adamw_conv_w", w["conv_w"], g_cw, m["conv_w"], v["conv_w"])
    gs["conv_w"] = g_cw
    for k in _WEIGHTS:
        if k not in _BIG:
            shp = w[k].shape
            grads[k], delta[k], new_m[k], new_v[k] = (t[k].reshape(shp) for t in (gs, ds, ms, vs))

    def lift(t, k):
        return t[k] if k == "norm_final_g" else t[k][None]

    outs = [loss, dx[None]]
    for t in (grads, delta, new_m, new_v):
        outs += [lift(t, k) for k in _WEIGHTS]
    return tuple(outs)
```

```python
import functools

import jax
import jax.numpy as jnp
from jax import lax
from jax.experimental import pallas as pl
from jax.experimental.pallas import tpu as pltpu

F32 = jnp.float32
BF = jnp.bfloat16

D = 1024
NDEV = 8
EPS = 1e-6
RG_C = 8.0
CHUNK = 128
NGRP = 8
GW = 128
NQ = 4
QW = 256
RC = 16
LANES = 128
SMALL_ROWS = 320
SMALL_PER = SMALL_ROWS // NDEV

ADAM_LR = 0.001
ADAM_B1 = 0.9
ADAM_B2 = 0.999
ADAM_EPS = 1e-08
ADAM_WD = 0.01
ADAM_STEP = 10

VMEM_LIMIT = 60 * 1024 * 1024

MESH = pl.DeviceIdType.MESH


def _rows(n, fn, unroll=2, rc=RC):
    def body(i, c):
        fn(pl.multiple_of(i * rc, rc))
        return c
    lax.fori_loop(0, n // rc, body, 0, unroll=unroll)


def _fold(v):
    return jnp.sum(v.reshape(v.shape[0] // RC, RC, v.shape[1]), axis=0)


def _dot(a, b):
    return jnp.dot(a, b, preferred_element_type=F32)


def _dot_nt(a, b):
    return lax.dot_general(a, b, (((1,), (1,)), ((), ())), preferred_element_type=F32)


def _dot_tn(a, b):
    return lax.dot_general(a, b, (((0,), (0,)), ((), ())), preferred_element_type=F32)


_GC = 0.7978845608028654
_GK = 0.044715


def _gelu(x):
    t = jnp.tanh(_GC * (x + _GK * (x * x * x)))
    return x * (0.5 * (1.0 + t))


def _gelu_grad(x):
    x2 = x * x
    t = jnp.tanh(_GC * (x + _GK * (x2 * x)))
    cdf = 0.5 * (1.0 + t)
    dg = cdf + (0.5 * x) * (1.0 - t * t) * (_GC * (1.0 + (3.0 * _GK) * x2))
    return x * cdf, dg


def _sigmoid(x):
    return jax.nn.sigmoid(x)


def _log1p(e):
    u = 1.0 + e
    d = u - 1.0
    return jnp.where(d == 0.0, e, jnp.log(u) * (e / jnp.where(d == 0.0, 1.0, d)))


def _softplus(z):
    return jnp.maximum(z, 0.0) + _log1p(jnp.exp(-jnp.abs(z)))


def _neg_expm1(z):
    u = jnp.exp(z)
    lu = jnp.log(u)
    k = (1.0 - u) * (z / jnp.where(lu == 0.0, 1.0, lu))
    small = jnp.where(lu == 0.0, -z, k)
    return jnp.where(z > -0.5, small, 1.0 - u)


def _shift_back(prev8, cur, j):
    cat = jnp.concatenate([prev8, cur], axis=0)
    return pltpu.roll(cat, j, 0)[8:8 + cur.shape[0]]


def _shift_fwd(cur, next8, j):
    cat = jnp.concatenate([cur, next8], axis=0)
    n = cat.shape[0]
    return pltpu.roll(cat, n - j, 0)[0:cur.shape[0]]


def _const_spec(shape):
    nd = len(shape)
    return pl.BlockSpec(shape, lambda *_: (0,) * nd, pipeline_mode=pl.Buffered(1))


def _params(sem):
    return pltpu.CompilerParams(dimension_semantics=sem, vmem_limit_bytes=VMEM_LIMIT)


TM_PROJ = 1024
TM_MIX = 256
TM_FFN = 256
TM_DX = 512
TS_DW = 4096


_CHIPS = (4, 2, 6)
_PASS_FLIPS = (0, 1, 4, 2, 6, 5, 3, 7)


class _Gather:
    def __init__(self, ins, outs, send_sems, recv_sems, local_sems):
        self.ins, self.outs = ins, outs
        self.send_sems, self.recv_sems, self.local_sems = send_sems, recv_sems, local_sems
        self.me = _me()
        self.sibling = _flip(self.me, 1)

    def _copy(self, a, kind, block, to, src=None):
        dst = self.outs[a].at[_lin(block)]
        return pltpu.make_async_remote_copy(
            src_ref=dst if src is None else src, dst_ref=dst,
            send_sem=self.send_sems.at[a, kind], recv_sem=self.recv_sems.at[a, kind],
            device_id=to, device_id_type=MESH)

    def _local(self, a):
        return pltpu.make_async_copy(self.ins[a], self.outs[a].at[_lin(self.me)], self.local_sems.at[a])

    def start(self):
        for a in range(len(self.ins)):
            self._local(a).start()
            self._copy(a, 0, self.me, self.sibling, src=self.ins[a]).start()
            for j, f in enumerate(_CHIPS):
                self._copy(a, 1 + j, self.me, _flip(self.me, f), src=self.ins[a]).start()

    def forward(self):
        for j, f in enumerate(_CHIPS):
            for a in range(len(self.ins)):
                self._copy(a, 1 + j, _flip(self.me, f), self.me).wait_recv()
                self._copy(a, 4 + j, _flip(self.me, f), self.sibling).start()

    def finish(self):
        for a in range(len(self.ins)):
            self._copy(a, 0, self.sibling, self.me).wait_recv()
            for j, f in enumerate(_CHIPS):
                self._copy(a, 4 + j, _flip(self.me, f | 1), self.me).wait_recv()
            self._copy(a, 0, self.me, self.sibling, src=self.ins[a]).wait_send()
            for j, f in enumerate(_CHIPS):
                self._copy(a, 1 + j, self.me, _flip(self.me, f), src=self.ins[a]).wait_send()
                self._copy(a, 4 + j, _flip(self.me, f), self.sibling).wait_send()
            self._local(a).wait()


class _Exchange:
    def __init__(self, ins, outs, send_sems, recv_sems, local_sems):
        self.ins, self.outs = ins, outs
        self.send_sems, self.recv_sems, self.local_sems = send_sems, recv_sems, local_sems
        self.me = _me()

    def _copy(self, a, r, outgoing):
        peer = _flip(self.me, r)
        src, dst = (peer, self.me) if outgoing else (self.me, peer)
        return pltpu.make_async_remote_copy(
            src_ref=self.ins[a].at[_lin(src)], dst_ref=self.outs[a].at[_lin(dst)],
            send_sem=self.send_sems.at[a, r - 1], recv_sem=self.recv_sems.at[a, r - 1],
            device_id=peer, device_id_type=MESH)

    def _local(self, a):
        mi = _lin(self.me)
        return pltpu.make_async_copy(self.ins[a].at[mi], self.outs[a].at[mi], self.local_sems.at[a])

    def start(self):
        for a in range(len(self.ins)):
            self._local(a).start()
        for r in range(1, NDEV):
            for a in range(len(self.ins)):
                self._copy(a, r, True).start()

    def finish(self):
        for r in range(1, NDEV):
            for a in range(len(self.ins)):
                self._copy(a, r, False).wait_recv()
        for r in range(1, NDEV):
            for a in range(len(self.ins)):
                self._copy(a, r, True).wait_send()
        for a in range(len(self.ins)):
            self._local(a).wait()


def _gather_sems(n):
    return [pltpu.SemaphoreType.DMA((n, 7)), pltpu.SemaphoreType.DMA((n, 7)), pltpu.SemaphoreType.DMA((n,))]


def _proj_gather_call(x, gmix, w_shard, order, extras):
    s = x.shape[0]
    tm = min(TM_PROJ, s)
    nt = s // tm
    wb = w_shard.shape[1]
    n = len(extras)

    def body(order_ref, x_ref, g_ref, wsh_ref, *rest):
        ex_in = rest[:n]
        proj_hbm, h1_hbm, wall_hbm = rest[n:n + 3]
        ex_out = rest[n + 3:2 * n + 3]
        wall, hc, pbuf, ws_send, ws_recv, ex_send, ex_recv, ex_local, out_sems, psems = rest[2 * n + 3:]
        k = pl.program_id(0)
        i = pl.program_id(1)
        me = _me()
        sibling = _flip(me, 1)
        gather = _Gather(ex_in, ex_out, ex_send, ex_recv, ex_local)

        def wcopy(kind, block, to):
            ref = wall.at[_lin(block)]
            return pltpu.make_async_remote_copy(
                src_ref=ref, dst_ref=ref, send_sem=ws_send.at[kind], recv_sem=ws_recv.at[kind],
                device_id=to, device_id_type=MESH)

        head = i == 0

        @pl.when(head & (k == 0))
        def _():
            wall[_lin(me)] = wsh_ref[...]
            wcopy(0, me, sibling).start()
            for j, f in enumerate(_CHIPS):
                wcopy(1 + j, me, _flip(me, f)).start()
            gather.start()

        out_w = pltpu.make_async_copy(wall, wall_hbm, out_sems.at[0])
        out_h = pltpu.make_async_copy(hc, h1_hbm, out_sems.at[1])

        @pl.when(head & (k == 1))
        def _():
            wcopy(0, sibling, me).wait_recv()
            out_h.start()

        for j, f in enumerate(_CHIPS):
            @pl.when(head & (k == 2 + j))
            def _(j=j, f=f):
                wcopy(1 + j, _flip(me, f), me).wait_recv()
                wcopy(4 + j, _flip(me, f), sibling).start()

            @pl.when(head & (k == 5 + j))
            def _(j=j, f=f):
                wcopy(4 + j, _flip(me, f | 1), me).wait_recv()
                if 5 + j == NDEV - 1:
                    out_w.start()

        pl.when(head & (k == 5))(gather.forward)

        base = pl.multiple_of(i * tm, tm)

        @pl.when(k == 0)
        def _():
            g = g_ref[...]

            def norm(r0):
                xx = x_ref[pl.ds(r0, RC), :]
                r = lax.rsqrt(jnp.mean(xx * xx, axis=-1, keepdims=True) + EPS)
                hc[pl.ds(base + r0, RC), :] = ((xx * r) * g).astype(BF)
            _rows(tm, norm, unroll=4)

        blk = order_ref[k]
        wk = wall.at[blk]
        step = k * nt + i
        slot = lax.rem(step, 2)
        subs = [(r0, min(TM_MIX, tm - r0)) for r0 in range(0, tm, TM_MIX)]

        def out_copy(q, r0, rs):
            return pltpu.make_async_copy(
                pbuf.at[slot, pl.ds(r0, rs)],
                proj_hbm.at[pl.ds(base + r0, rs), pl.ds(pl.multiple_of(blk * wb, 128), wb)], psems.at[slot, q])

        @pl.when(step >= 2)
        def _():
            for q, (r0, rs) in enumerate(subs):
                out_copy(q, r0, rs).wait()

        for q, (r0, rs) in enumerate(subs):
            pbuf[slot, r0:r0 + rs, :] = _dot(hc[pl.ds(base + r0, rs), :], wk[...])
            out_copy(q, r0, rs).start()

        @pl.when((k == NDEV - 1) & (i == nt - 1))
        def _():
            wcopy(0, me, sibling).wait_send()
            for j, f in enumerate(_CHIPS):
                wcopy(1 + j, me, _flip(me, f)).wait_send()
                wcopy(4 + j, _flip(me, f), sibling).wait_send()
            gather.finish()
            out_w.wait()
            out_h.wait()
            for q, (r0, rs) in enumerate(subs):
                out_copy(q, r0, rs).wait()
                pltpu.make_async_copy(pbuf.at[1 - slot, pl.ds(r0, rs)], proj_hbm.at[pl.ds(r0, rs), pl.ds(0, wb)],
                                      psems.at[1 - slot, q]).wait()

    any_spec = pl.BlockSpec(memory_space=pl.ANY)
    grid_spec = pltpu.PrefetchScalarGridSpec(
        num_scalar_prefetch=1, grid=(NDEV, nt),
        in_specs=[pl.BlockSpec((tm, D), lambda k, i, o: (jnp.where(k == 0, i, nt - 1), 0)),
                  pl.BlockSpec((1, D), lambda k, i, o: (0, 0)),
                  pl.BlockSpec(w_shard.shape, lambda k, i, o: (0, 0))] + [any_spec] * n,
        out_specs=[any_spec, any_spec, any_spec] + [any_spec] * n,
        scratch_shapes=[pltpu.VMEM((NDEV,) + w_shard.shape, BF), pltpu.VMEM((s, D), BF), pltpu.VMEM((2, tm, wb), F32),
                        pltpu.SemaphoreType.DMA((7,)), pltpu.SemaphoreType.DMA((7,))] + _gather_sems(n)
        + [pltpu.SemaphoreType.DMA((2,)), pltpu.SemaphoreType.DMA((2, -(-tm // TM_MIX)))])
    outs = pl.pallas_call(
        body, name="proj_gather", grid_spec=grid_spec,
        out_shape=[jax.ShapeDtypeStruct((s, NDEV * wb), F32), jax.ShapeDtypeStruct((s, D), BF),
                   jax.ShapeDtypeStruct((NDEV,) + w_shard.shape, BF)]
        + [jax.ShapeDtypeStruct((NDEV,) + e.shape, e.dtype) for e in extras],
        compiler_params=_params(("arbitrary", "arbitrary")),
    )(order, x, gmix, w_shard, *extras)
    return outs[0], outs[1], outs[2], outs[3:]


def _conv_tile(rx, prev8, cw_ref, cb):
    xc = cb + cw_ref[3:4, :] * rx
    for j in (1, 2, 3):
        xc = xc + cw_ref[3 - j:4 - j, :] * _shift_back(prev8, rx, j)
    return xc


def _gate_tile(xcb, wax_ref, ba, bx, sp, q):
    cs = slice(q * QW, (q + 1) * QW)
    z = _dot(xcb[:, cs], wax_ref[q])
    r = _sigmoid(z[:, 0:QW] + ba[:, cs])
    ig = _sigmoid(z[:, QW:2 * QW] + bx[:, cs])
    return r, ig, (-RG_C * r) * sp[:, cs]


def _scan_fwd(a_s, b_s, out_ref, h0, n):
    rowi = lax.broadcasted_iota(jnp.int32, (8, D), 0)

    def block(t, h):
        rows = pl.ds(t * 8, 8)
        a = a_s[rows, :]
        b = b_s[rows, :]
        for d in (1, 2, 4):
            m = rowi >= d
            b = jnp.where(m, a * pltpu.roll(b, d, 0) + b, b)
            a = jnp.where(m, a * pltpu.roll(a, d, 0), a)
        hb = b + a * h
        out_ref[rows, :] = hb
        return hb[7:8, :]
    h = h0
    for t in range(n // 8):
        h = block(t, h)
    return h


def _scan_bwd(a_s, lm_s, c0, n):
    rowi = lax.broadcasted_iota(jnp.int32, (8, D), 0)
    nblk = n // 8

    def block(k, cin):
        rows = pl.ds((nblk - 1 - k) * 8, 8)
        a = a_s[rows, :]
        dh = lm_s[rows, :]
        b = a * dh
        for d in (1, 2, 4):
            m = rowi < 8 - d
            b = jnp.where(m, a * pltpu.roll(b, 8 - d, 0) + b, b)
            a = jnp.where(m, a * pltpu.roll(a, 8 - d, 0), a)
        mu = b + a * cin
        lm_s[rows, :] = dh + jnp.where(rowi < 7, pltpu.roll(mu, 7, 0), cin)
        return mu[0:1, :]
    c = c0
    for k in range(nblk):
        c = block(k, c)
    return c


def _mixer_fwd_call(proj, x, cw, cb, ba, bx, lam, lng, lnb, wax, wtr, bias, wpa, wpb, wo, shards):
    s = x.shape[0]
    tm = min(TM_MIX, s)
    nt = s // tm
    pw = proj.shape[1]
    n = len(shards)

    def body(proj_ref, x_ref, cw_ref, cb_ref, ba_ref, bx_ref, lam_ref, lng_ref, lnb_ref, wax_ref, wtr_ref,
             bias_ref, wpa_ref, wpb_ref, wo_ref, *rest):
        sh_in = rest[:n]
        h_ref, pa_ref, pb_ref, x1_ref, xc_ref, r_ref, ig_ref, a_s, m_ref, ya_ref, yb_ref, mg_ref = rest[n:n + 12]
        sh_out = rest[n + 12:2 * n + 12]
        prev_s, b_s, hc_s, vn_s, mx_s, g_send, g_recv, g_local = rest[2 * n + 12:]
        i = pl.program_id(0)
        gather = _Gather(sh_in, sh_out, g_send, g_recv, g_local)

        @pl.when(i == 0)
        def _():
            gather.start()
            prev_s[...] = jnp.zeros((8, D), F32)
            hc_s[...] = jnp.zeros((8, D), F32)

        pl.when(i == nt // 2)(gather.forward)

        rx = proj_ref[:, 0:D]
        xc = _conv_tile(rx, prev_s[...], cw_ref, cb_ref[...])
        prev_s[...] = rx[tm - 8:tm, :]
        xc_ref[...] = xc.astype(BF)
        xcb = xc.astype(BF)
        sp = _softplus(-lam_ref[...])
        ba = ba_ref[...]
        bx = bx_ref[...]
        for q in range(NQ):
            cs = slice(q * QW, (q + 1) * QW)
            r, ig, la = _gate_tile(xcb, wax_ref, ba, bx, sp, q)
            mq = jnp.sqrt(_neg_expm1(2.0 * la))
            r_ref[:, cs] = r.astype(BF)
            ig_ref[:, cs] = ig.astype(BF)
            m_ref[:, cs] = mq.astype(BF)
            a_s[:, cs] = jnp.exp(la)
            b_s[:, cs] = mq * (ig * xc[:, cs])

        gv = _gelu(proj_ref[:, 3 * D:4 * D])
        dv = gv - jnp.mean(gv, axis=-1, keepdims=True)
        var = jnp.mean(dv * dv, axis=-1, keepdims=True)
        vn_s[...] = ((dv * lax.rsqrt(var + EPS)) * lng_ref[...] + lnb_ref[...]).astype(BF)
        nc = tm // CHUNK
        for c in range(nc):
            rs = slice(c * CHUNK, (c + 1) * CHUNK)
            for g in range(NGRP):
                cs = slice(g * GW, (g + 1) * GW)
                mx_s[rs, cs] = _dot(wtr_ref[g], vn_s[rs, cs])
        mixed = mx_s[...] + jnp.concatenate([bias_ref[...]] * nc, axis=0)
        yb = (_gelu(proj_ref[:, 2 * D:3 * D]) * mixed).astype(BF)
        yb_ref[...] = yb
        pb = _dot(yb, wpb_ref[...])
        pb_ref[...] = pb.astype(BF)
        mx_s[...] = pb

        hc_s[0:1, :] = _scan_fwd(a_s, b_s, h_ref, hc_s[0:1, :], tm)

        ya = (_gelu(proj_ref[:, D:2 * D]) * h_ref[...]).astype(BF)
        ya_ref[...] = ya
        pa = _dot(ya, wpa_ref[...])
        pa_ref[...] = pa.astype(BF)
        mg = (_sigmoid(proj_ref[:, 4 * D:5 * D]) * pa + _sigmoid(proj_ref[:, 5 * D:6 * D]) * mx_s[...]).astype(BF)
        mg_ref[...] = mg
        x1_ref[...] = x_ref[...] + _dot(mg, wo_ref[...])

        pl.when(i == nt - 1)(gather.finish)

    tile = lambda w: pl.BlockSpec((tm, w), lambda i: (i, 0))
    vec = _const_spec((1, D))
    any_spec = pl.BlockSpec(memory_space=pl.ANY)
    outs = pl.pallas_call(
        body, name="mixer_fwd", grid=(nt,),
        in_specs=[tile(pw), tile(D), _const_spec((4, D)), vec, vec, vec, vec, vec, vec,
                  _const_spec(wax.shape), _const_spec(wtr.shape), _const_spec(bias.shape),
                  _const_spec((D, D)), _const_spec((D, D)), _const_spec((D, D))] + [any_spec] * n,
        out_specs=[tile(D)] * 12 + [any_spec] * n,
        out_shape=[jax.ShapeDtypeStruct((s, D), dt) for dt in (F32, BF, BF, F32, BF, BF, BF, F32, BF, BF, BF, BF)]
        + [jax.ShapeDtypeStruct((NDEV,) + e.shape, e.dtype) for e in shards],
        scratch_shapes=[pltpu.VMEM((8, D), F32), pltpu.VMEM((tm, D), F32),
                        pltpu.VMEM((8, D), F32), pltpu.VMEM((tm, D), BF), pltpu.VMEM((tm, D), F32)]
        + _gather_sems(n),
        compiler_params=_params(("arbitrary",)),
    )(proj, x, cw, cb, ba, bx, lam, lng, lnb, wax, wtr, bias, wpa, wpb, wo, *shards)
    return outs[:12], outs[12:]


def _ffn_call(x1, tgt, gffn, gfin, wgu, wdn):
    s = x1.shape[0]
    tm = min(TM_FFN, s)
    nt = s // tm
    nh = wdn.shape[0]
    fb = wgu.shape[2]
    fa = fb // LANES * LANES
    ft = fb - fa
    ff = nh * fb
    tail = lambda k: slice(nh * fa + k * ft, nh * fa + (k + 1) * ft)

    def body(x1_ref, tgt_ref, gffn_ref, gfin_ref, wgu_hbm, wdn_hbm,
             dx1_ref, dx1b_ref, h2_ref, act_ref, dgu_ref, dx2b_ref, acc_ref,
             wgu_s, wdn_s, tail_s, g_s, u_s, dx2_s, xh2_s, r2_s, accs, wsems):
        i = pl.program_id(0)

        def weight_copies():
            out = []
            for k in range(2 * nh):
                half, kk = divmod(k, nh)
                out.append((wgu_hbm.at[k, :, 0:fa], wgu_s.at[half, :, kk * fa:(kk + 1) * fa]))
                out.append((wgu_hbm.at[k, :, fa:fb], tail_s.at[k]))
            for k in range(nh):
                out.append((wdn_hbm.at[k, 0:fa, :], wdn_s.at[k * fa:(k + 1) * fa, :]))
                out.append((wdn_hbm.at[k, fa:fb, :], wdn_s.at[tail(k), :]))
            return [pltpu.make_async_copy(src, dst, wsems.at[n]) for n, (src, dst) in enumerate(out)]

        @pl.when(i == 0)
        def _():
            accs[...] = jnp.zeros(accs.shape, F32)
            for ref in (g_s, u_s, dx2_s, xh2_s, r2_s):
                ref[1] = jnp.zeros(ref.shape[1:], F32)
            copies = weight_copies()
            for c in copies:
                c.start()
            for c in copies:
                c.wait()
            for half in range(2):
                rest = [tail_s[half * nh + k].astype(F32) for k in range(nh)]
                wgu_s[half, :, nh * fa:ff] = jnp.concatenate(rest, axis=1).astype(BF)

        def put_blocks(ref, base, v):
            for k in range(nh):
                ref[base + k, :, 0:fa] = v[:, k * fa:(k + 1) * fa].astype(BF)
                ref[base + k, :, fa:fb] = v[:, tail(k)].astype(BF)

        gffn = gffn_ref[...]
        gfin = gfin_ref[...]
        live = jnp.where(i < nt, 1.0, 0.0)

        def forward(w):
            x1 = x1_ref[...]
            r2 = lax.rsqrt(jnp.mean(x1 * x1, axis=-1, keepdims=True) + EPS)
            xh2 = x1 * r2
            h2 = (xh2 * gffn).astype(BF)
            h2_ref[...] = h2
            r2_s[w] = r2
            xh2_s[w] = xh2

            g = _dot(h2, wgu_s[0])
            u = _dot(h2, wgu_s[1])
            g_s[w] = g
            u_s[w] = u
            act = (g * _sigmoid(g)) * u
            put_blocks(act_ref, 0, act)
            x2 = x1 + _dot(act.astype(BF), wdn_s[...])

            r3 = lax.rsqrt(jnp.mean(x2 * x2, axis=-1, keepdims=True) + EPS)
            xh = x2 * r3
            err = xh * gfin - tgt_ref[...]
            accs[2] += _fold(err * err) * live
            dy = err * (1.0 / D)
            accs[1] += _fold(dy * xh) * live
            dxh = dy * gfin
            dx2 = r3 * (dxh - xh * jnp.mean(dxh * xh, axis=-1, keepdims=True))
            dx2_s[w] = dx2
            dx2b_ref[...] = dx2.astype(BF)

        def backward(r):
            dx2 = dx2_s[r]
            da = _dot_nt(dx2.astype(BF), wdn_s[...])
            g = g_s[r]
            sg = _sigmoid(g)
            dg = (da * u_s[r]) * (sg * (1.0 + g * (1.0 - sg)))
            du = da * (g * sg)
            put_blocks(dgu_ref, 0, dg)
            put_blocks(dgu_ref, nh, du)
            dh2 = _dot_nt(dg.astype(BF), wgu_s[0]) + _dot_nt(du.astype(BF), wgu_s[1])

            xh2 = xh2_s[r]
            accs[0] += _fold(dh2 * xh2)
            dxh = dh2 * gffn
            dx1 = dx2 + r2_s[r] * (dxh - xh2 * jnp.mean(dxh * xh2, axis=-1, keepdims=True))
            dx1_ref[...] = dx1
            dx1b_ref[...] = dx1.astype(BF)

        for p in range(2):
            @pl.when(i % 2 == p)
            def _(p=p):
                forward(p)
                backward(1 - p)

        @pl.when(i == nt)
        def _():
            acc_ref[...] = jnp.zeros((8, D), F32)
            for j in range(3):
                acc_ref[j:j + 1, :] = jnp.sum(accs[j], axis=0, keepdims=True)

    fwd = lambda i: jnp.minimum(i, nt - 1)
    bwd = lambda i: jnp.maximum(i - 1, 0)
    vec = _const_spec((1, D))
    any_spec = pl.BlockSpec(memory_space=pl.ANY)
    return pl.pallas_call(
        body, name="ffn", grid=(nt + 1,),
        in_specs=[pl.BlockSpec((tm, D), lambda i: (fwd(i), 0)), pl.BlockSpec((tm, D), lambda i: (fwd(i), 0)),
                  vec, vec, any_spec, any_spec],
        out_specs=[pl.BlockSpec((tm, D), lambda i: (bwd(i), 0)), pl.BlockSpec((tm, D), lambda i: (bwd(i), 0)),
                   pl.BlockSpec((tm, D), lambda i: (fwd(i), 0)),
                   pl.BlockSpec((nh, tm, fb), lambda i: (0, fwd(i), 0)),
                   pl.BlockSpec((2 * nh, tm, fb), lambda i: (0, bwd(i), 0)),
                   pl.BlockSpec((tm, D), lambda i: (fwd(i), 0)), pl.BlockSpec((8, D), lambda i: (0, 0))],
        out_shape=[jax.ShapeDtypeStruct((s, D), F32), jax.ShapeDtypeStruct((s, D), BF),
                   jax.ShapeDtypeStruct((s, D), BF), jax.ShapeDtypeStruct((nh, s, fb), BF),
                   jax.ShapeDtypeStruct((2 * nh, s, fb), BF), jax.ShapeDtypeStruct((s, D), BF),
                   jax.ShapeDtypeStruct((8, D), F32)],
        scratch_shapes=[pltpu.VMEM((2, D, ff), BF), pltpu.VMEM((ff, D), BF), pltpu.VMEM((2 * nh, D, ft), BF),
                        pltpu.VMEM((2, tm, ff), F32), pltpu.VMEM((2, tm, ff), F32), pltpu.VMEM((2, tm, D), F32),
                        pltpu.VMEM((2, tm, D), F32), pltpu.VMEM((2, tm, 1), F32),
                        pltpu.VMEM((3, RC, D), F32), pltpu.SemaphoreType.DMA((6 * nh,))],
        compiler_params=_params(("arbitrary",)),
    )(x1, tgt, gffn, gfin, wgu, wdn)


def _mixer_bwd_pre_call(dx1b, proj, pa, pb, lng, lnb, wtr, wtrt, bias, wpa, wpb, wo, parts):
    s = dx1b.shape[0]
    tm = min(TM_MIX, s)
    nt = s // tm
    pw = proj.shape[1]
    n = len(parts)

    def body(dx1b_ref, uv_ref, gg_ref, pa_ref, pb_ref, lng_ref, lnb_ref, wtr_ref, wtrt_ref, bias_ref,
             wpa_ref, wpb_ref, wo_ref, *rest):
        ex_in = rest[:n]
        dproj_ref, dya_ref, dpa_ref, dpb_ref, vec_ref, dws_ref, dbs_ref = rest[n:n + 7]
        ex_out = rest[n + 7:2 * n + 7]
        vn_s, mx_s, dmx_s, dvn_s, accs, dbs_s, e_send, e_recv, e_local = rest[2 * n + 7:]
        i = pl.program_id(0)
        exchange = _Exchange(ex_in, ex_out, e_send, e_recv, e_local)

        @pl.when(i == 0)
        def _():
            exchange.start()
            accs[...] = jnp.zeros(accs.shape, F32)
            dbs_s[...] = jnp.zeros(dbs_s.shape, F32)
            dws_ref[...] = jnp.zeros(dws_ref.shape, F32)

        dm = _dot_nt(dx1b_ref[...], wo_ref[...])
        sa = _sigmoid(gg_ref[:, 0:D])
        sb = _sigmoid(gg_ref[:, D:2 * D])
        dpa = dm * sa
        dpb = dm * sb
        dpab = dpa.astype(BF)
        dpbb = dpb.astype(BF)
        dpa_ref[...] = dpab
        dpb_ref[...] = dpbb
        dproj_ref[:, 2 * D:3 * D] = ((dpa * pa_ref[...].astype(F32)) * (1.0 - sa)).astype(BF)
        dproj_ref[:, 3 * D:4 * D] = ((dpb * pb_ref[...].astype(F32)) * (1.0 - sb)).astype(BF)

        dya_ref[...] = _dot_nt(dpab, wpa_ref[...])
        dyb = _dot_nt(dpbb, wpb_ref[...])

        lng = lng_ref[...]
        gv, dgelu_v = _gelu_grad(uv_ref[:, D:2 * D])
        dv = gv - jnp.mean(gv, axis=-1, keepdims=True)
        rstd = lax.rsqrt(jnp.mean(dv * dv, axis=-1, keepdims=True) + EPS)
        xh = dv * rstd
        vn_s[...] = (xh * lng + lnb_ref[...]).astype(BF)

        nc = tm // CHUNK
        for c in range(nc):
            rs = slice(c * CHUNK, (c + 1) * CHUNK)
            for g in range(NGRP):
                cs = slice(g * GW, (g + 1) * GW)
                mx_s[rs, cs] = _dot(wtr_ref[g], vn_s[rs, cs])

        gu, dgelu_u = _gelu_grad(uv_ref[:, 0:D])
        mixed = mx_s[...] + jnp.concatenate([bias_ref[...]] * nc, axis=0)
        dproj_ref[:, 0:D] = ((dyb * mixed) * dgelu_u).astype(BF)
        dmx = dyb * gu
        dmx_s[...] = dmx.astype(BF)
        dbs_s[...] += jnp.sum(dmx.reshape(nc, CHUNK, D), axis=0)

        for c in range(nc):
            rs = slice(c * CHUNK, (c + 1) * CHUNK)
            for g in range(NGRP):
                cs = slice(g * GW, (g + 1) * GW)
                dvn_s[rs, cs] = _dot(wtrt_ref[g], dmx_s[rs, cs])
                dws_ref[g] += _dot_nt(dmx_s[rs, cs], vn_s[rs, cs])

        dvn = dvn_s[...]
        accs[0] += _fold(dvn * xh)
        accs[1] += _fold(dvn)
        dxh = dvn * lng
        m1 = jnp.mean(dxh, axis=-1, keepdims=True)
        m2 = jnp.mean(dxh * xh, axis=-1, keepdims=True)
        dproj_ref[:, D:2 * D] = ((rstd * (dxh - m1 - xh * m2)) * dgelu_v).astype(BF)

        @pl.when(i == nt - 1)
        def _():
            vec_ref[...] = jnp.zeros((8, D), F32)
            for j in range(2):
                vec_ref[j:j + 1, :] = jnp.sum(accs[j], axis=0, keepdims=True)
            row = lax.broadcasted_iota(jnp.int32, (CHUNK, CHUNK), 0)
            col = lax.broadcasted_iota(jnp.int32, (CHUNK, CHUNK), 1)
            for g in range(NGRP):
                dws_ref[g] = jnp.where(row >= col, dws_ref[g], 0.0)
                gs = jnp.sum(dbs_s[:, g * GW:(g + 1) * GW], axis=1, keepdims=True)
                dbs_ref[:, g * GW:(g + 1) * GW] = jnp.broadcast_to(gs, (CHUNK, GW))
            exchange.finish()

    tile = lambda w: pl.BlockSpec((tm, w), lambda i: (i, 0))
    vec = _const_spec((1, D))
    any_spec = pl.BlockSpec(memory_space=pl.ANY)
    outs = pl.pallas_call(
        body, name="mixer_bwd_pre", grid=(nt,),
        in_specs=[tile(D), pl.BlockSpec((tm, 2 * D), lambda i: (i, 1)), pl.BlockSpec((tm, 2 * D), lambda i: (i, 2)),
                  tile(D), tile(D), vec, vec, _const_spec(wtr.shape), _const_spec(wtrt.shape),
                  _const_spec(bias.shape), _const_spec((D, D)), _const_spec((D, D)), _const_spec((D, D))]
        + [any_spec] * n,
        out_specs=[tile(4 * D), tile(D), tile(D), tile(D), pl.BlockSpec((8, D), lambda i: (0, 0)),
                   pl.BlockSpec((NGRP, CHUNK, CHUNK), lambda i: (0, 0, 0)),
                   pl.BlockSpec((CHUNK, D), lambda i: (0, 0))] + [any_spec] * n,
        out_shape=[jax.ShapeDtypeStruct((s, 4 * D), BF), jax.ShapeDtypeStruct((s, D), F32),
                   jax.ShapeDtypeStruct((s, D), BF), jax.ShapeDtypeStruct((s, D), BF),
                   jax.ShapeDtypeStruct((8, D), F32), jax.ShapeDtypeStruct((NGRP, CHUNK, CHUNK), F32),
                   jax.ShapeDtypeStruct((CHUNK, D), F32)]
        + [jax.ShapeDtypeStruct(p.shape, p.dtype) for p in parts],
        scratch_shapes=[pltpu.VMEM((tm, D), BF), pltpu.VMEM((tm, D), F32), pltpu.VMEM((tm, D), BF),
                        pltpu.VMEM((tm, D), F32), pltpu.VMEM((2, RC, D), F32), pltpu.VMEM((CHUNK, D), F32)]
        + _gather_sems(n),
        compiler_params=_params(("arbitrary",)),
    )(dx1b, proj, proj, pa, pb, lng, lnb, wtr, wtrt, bias, wpa, wpb, wo, *parts)
    return outs[:7], outs[7:]


def _mixer_bwd_seq_call(dprojb, dya, proj, h, x, dx1, gates, gmix, w_in_all, cw, lam, wax, parts):
    s = dya.shape[0]
    tm = min(TM_MIX, s)
    nt = s // tm
    tb = tm // 8
    n = len(parts)
    nb, _, wb = w_in_all.shape

    def body(dpb_ref, dya_ref, xg_ref, xh8_ref, h_ref, hh8_ref, x_ref, dx1_ref, xc_s, r_s, ig_s, a_s, m_s,
             gmix_ref, win_ref, cw_ref, lam_ref, wax_ref, *rest):
        ex_in = rest[:n]
        dpa_ref, dx_ref, vec_ref, dwax_ref = rest[n:n + 4]
        ex_out = rest[n + 4:2 * n + 4]
        lm_s, dh_s, dxc_s, c_s, accs, e_send, e_recv, e_local = rest[2 * n + 4:]
        i = pl.program_id(0)
        exchange = _Exchange(ex_in, ex_out, e_send, e_recv, e_local)

        @pl.when(i == 0)
        def _():
            exchange.start()
            accs[...] = jnp.zeros(accs.shape, F32)
            dwax_ref[...] = jnp.zeros(dwax_ref.shape, F32)
            c_s[...] = jnp.zeros((8, D), F32)
            dxc_s[tm:tm + 8, :] = jnp.zeros((8, D), F32)

        first_tile = i == nt - 1
        prev8 = jnp.where(first_tile, 0.0, xh8_ref[...])
        hprev8 = jnp.where(first_tile, 0.0, hh8_ref[...])
        lamv = lam_ref[...]
        sp = _softplus(-lamv)
        hv = h_ref[...]
        g, dg = _gelu_grad(xg_ref[:, D:2 * D])
        dya = dya_ref[...]
        lm_s[...] = dya * g
        drg = ((dya * hv) * dg).astype(BF)
        dpa_ref[:, D:2 * D] = drg
        dpa_ref[:, 2 * D:6 * D] = dpb_ref[...]

        def dpb_block(k):
            return _dot_nt(dpb_ref[:, k * wb - 2 * D:(k + 1) * wb - 2 * D], win_ref[k])
        dh = (_dot_nt(drg[:, 0:2 * wb - D], win_ref[1, :, D - wb:wb])
              + _dot_nt(drg[:, 2 * wb - D:D], win_ref[2, :, 0:2 * D - 2 * wb])
              + _dot_nt(dpb_ref[:, 0:3 * wb - 2 * D], win_ref[2, :, 2 * D - 2 * wb:wb]))
        for k in range(3, 6):
            dh = dh + dpb_block(k)
        dh_s[...] = dh

        c_s[0:1, :] = _scan_bwd(a_s, lm_s, c_s[0:1, :], tm)

        hprev = _shift_back(hprev8, h_ref[...], 1)
        for q in range(NQ):
            cs = slice(q * QW, (q + 1) * QW)
            r = r_s[:, cs].astype(F32)
            ig = ig_s[:, cs].astype(F32)
            a = a_s[:, cs]
            m = m_s[:, cs].astype(F32)
            lm = lm_s[:, cs]
            xq = xc_s[:, cs].astype(F32)
            dixc = lm * m
            dla = (lm * hprev[:, cs]) * a - ((lm * (ig * xq)) * (a * a)) / m
            accs[3, :, cs] += _fold(dla * r)
            dza = (dla * (-RG_C * sp[:, cs])) * (r * (1.0 - r))
            dzx = (dixc * xq) * (ig * (1.0 - ig))
            accs[1, :, cs] += _fold(dza)
            accs[2, :, cs] += _fold(dzx)
            dz = jnp.concatenate([dza, dzx], axis=1).astype(BF)
            dxc_s[0:tm, cs] = dixc * ig + _dot_nt(dz, wax_ref[q])
            dwax_ref[q] += _dot_tn(xc_s[:, cs], dz)

        cur = dxc_s[0:tm, :]
        nx = dxc_s[tm:tm + 8, :]
        drx = cw_ref[3:4, :] * cur
        for j in (1, 2, 3):
            drx = drx + cw_ref[3 - j:4 - j, :] * _shift_fwd(cur, nx, j)
        accs[0] += _fold(cur)
        rx = xg_ref[:, 0:D]
        accs[7] += _fold(cur * rx)
        for j in (1, 2, 3):
            accs[7 - j] += _fold(cur * _shift_back(prev8, rx, j))
        dxc_s[tm:tm + 8, :] = cur[0:8, :]

        drxb = drx.astype(BF)
        dpa_ref[:, 0:D] = drxb
        dh = dh_s[...] + _dot_nt(drxb[:, 0:wb], win_ref[0]) + _dot_nt(drxb[:, wb:D], win_ref[1, :, 0:D - wb])
        for k in range(6, nb):
            dh = dh + dpb_block(k)
        xx = x_ref[...]
        rn = lax.rsqrt(jnp.mean(xx * xx, axis=-1, keepdims=True) + EPS)
        xhn = xx * rn
        accs[8] += _fold(dh * xhn)
        dxh = dh * gmix_ref[...]
        dx_ref[...] = dx1_ref[...] + rn * (dxh - xhn * jnp.mean(dxh * xhn, axis=-1, keepdims=True))

        @pl.when(i == nt - 1)
        def _():
            vec_ref[...] = jnp.zeros((16, D), F32)
            for j in range(9):
                vec_ref[j:j + 1, :] = jnp.sum(accs[j], axis=0, keepdims=True)
            vec_ref[3:4, :] = vec_ref[3:4, :] * (RG_C * _sigmoid(-lamv))
            exchange.finish()

    rev = lambda w: pl.BlockSpec((tm, w), lambda i: (nt - 1 - i, 0))
    halo = pl.BlockSpec((8, D), lambda i: (jnp.maximum((nt - 1 - i) * tb - 1, 0), 0))
    vec = _const_spec((1, D))
    any_spec = pl.BlockSpec(memory_space=pl.ANY)
    outs = pl.pallas_call(
        body, name="mixer_bwd_seq", grid=(nt,),
        in_specs=[rev(4 * D), rev(D), rev(2 * D), halo, rev(D), halo] + [rev(D)] * 7
        + [vec, _const_spec(w_in_all.shape), _const_spec((4, D)), vec, _const_spec(wax.shape)] + [any_spec] * n,
        out_specs=[rev(6 * D), rev(D), pl.BlockSpec((16, D), lambda i: (0, 0)),
                   pl.BlockSpec((NQ, QW, 2 * QW), lambda i: (0, 0, 0))] + [any_spec] * n,
        out_shape=[jax.ShapeDtypeStruct((s, 6 * D), BF), jax.ShapeDtypeStruct((s, D), F32),
                   jax.ShapeDtypeStruct((16, D), F32), jax.ShapeDtypeStruct((NQ, QW, 2 * QW), F32)]
        + [jax.ShapeDtypeStruct(p.shape, p.dtype) for p in parts],
        scratch_shapes=[pltpu.VMEM((tm, D), F32), pltpu.VMEM((tm, D), F32),
                        pltpu.VMEM((tm + 8, D), F32), pltpu.VMEM((8, D), F32), pltpu.VMEM((9, RC, D), F32)]
        + _gather_sems(n),
        compiler_params=_params(("arbitrary",)),
    )(dprojb, dya, proj, proj, h, h, x, dx1, *gates, gmix, w_in_all, cw, lam, wax, *parts)
    return outs[:4], outs[4:]


def _device_of(d):
    return (d // 4, lax.rem(d // 2, 2), lax.rem(d, 2))


_DW_PLAN = (("C", 1), ("C", 2), ("A", 1), ("C", 3), ("A", 2), ("S", 0), ("A", 3), ("O", 0))
_DW_FLIPS = tuple({"C": 2 * j + 1, "A": 2 * j, "S": 1, "O": 0}[kind] for kind, j in _DW_PLAN)


def _dw_exchange_call(name, order, a, b, a_spec, b_spec, k1, n1, s, small):
    ts = min(TS_DW, s)
    ns = s // ts
    nstep = len(_DW_PLAN)

    def slab(i, order_ref):
        return order_ref[i]

    def body(order_ref, a_ref, b_ref, g_ref, own_ref, recv_ref, gsum_ref, acc, sbuf, stage, send_sems, recv_sems,
             st_send, st_recv, local_sem, rbuf, gacc, send1, recv1, send2, recv2):
        i = pl.program_id(0)
        j = pl.program_id(1)
        me = _me()
        mi = _lin(me)
        sibling = _flip(me, 1)
        allreduce = _SmallAllReduce(g_ref, gacc, rbuf, send1, recv1, send2, recv2)
        pl.when((i == 0) & (j == 0))(allreduce.scatter)
        pl.when((i == nstep // 2) & (j == 0))(allreduce.reduce)
        p = _dot_tn(a_ref[...], b_ref[...])

        @pl.when(j == 0)
        def _():
            acc[...] = p

        @pl.when(j > 0)
        def _():
            acc[...] += p

        def send(step):
            kind, jj = _DW_PLAN[step]
            src = sbuf.at[step % 2]
            if kind == "C":
                return pltpu.make_async_remote_copy(
                    src_ref=src, dst_ref=stage.at[jj - 1], send_sem=st_send.at[jj - 1], recv_sem=st_recv.at[jj - 1],
                    device_id=sibling, device_id_type=MESH)
            to = sibling if kind == "S" else _flip(me, 2 * jj)
            return pltpu.make_async_remote_copy(
                src_ref=src, dst_ref=recv_ref.at[mi], send_sem=send_sems.at[jj], recv_sem=recv_sems.at[jj],
                device_id=to, device_id_type=MESH)

        def arrival(jj):
            frm = sibling if jj == 0 else _flip(me, 2 * jj)
            return pltpu.make_async_remote_copy(
                src_ref=sbuf.at[0], dst_ref=recv_ref.at[_lin(frm)], send_sem=send_sems.at[jj],
                recv_sem=recv_sems.at[jj], device_id=frm, device_id_type=MESH)

        def staged(jj):
            return pltpu.make_async_remote_copy(
                src_ref=sbuf.at[0], dst_ref=stage.at[jj - 1], send_sem=st_send.at[jj - 1], recv_sem=st_recv.at[jj - 1],
                device_id=sibling, device_id_type=MESH)

        for step, (kind, jj) in enumerate(_DW_PLAN):
            @pl.when((i == step) & (j == ns - 1))
            def _(step=step, kind=kind, jj=jj):
                if step >= 2:
                    send(step - 2).wait_send()
                if kind == "A":
                    staged(jj).wait_recv()
                    sbuf[step % 2] = (acc[...] + stage[jj - 1].astype(F32)).astype(BF)
                else:
                    sbuf[step % 2] = acc[...].astype(BF)
                if kind != "O":
                    send(step).start()
                else:
                    own_ref[...] = acc[...]
                    mine = pltpu.make_async_copy(sbuf.at[step % 2], recv_ref.at[mi], local_sem)
                    mine.start()
                    send(step - 1).wait_send()
                    for q in range(4):
                        arrival(q).wait_recv()
                    mine.wait()
                    allreduce.finish()
                    gsum_ref[...] = gacc[...]

    vmem = pl.BlockSpec(memory_space=pltpu.VMEM)
    grid_spec = pltpu.PrefetchScalarGridSpec(
        num_scalar_prefetch=1, grid=(nstep, ns),
        in_specs=[a_spec(ts, slab), b_spec(ts, slab), vmem],
        out_specs=[pl.BlockSpec((k1, n1), lambda i, j, o: (0, 0)), pl.BlockSpec(memory_space=pl.ANY), vmem],
        scratch_shapes=[pltpu.VMEM((k1, n1), F32), pltpu.VMEM((2, k1, n1), BF), pltpu.VMEM((3, k1, n1), BF),
                        pltpu.SemaphoreType.DMA((4,)), pltpu.SemaphoreType.DMA((4,)),
                        pltpu.SemaphoreType.DMA((3,)), pltpu.SemaphoreType.DMA((3,)),
                        pltpu.SemaphoreType.DMA(()), pltpu.VMEM((NDEV, SMALL_PER, D), F32),
                        pltpu.VMEM((SMALL_ROWS, D), F32)]
        + [pltpu.SemaphoreType.DMA((NDEV - 1,))] * 4)
    return pl.pallas_call(
        body, name=name, grid_spec=grid_spec,
        out_shape=[jax.ShapeDtypeStruct((k1, n1), F32), jax.ShapeDtypeStruct((NDEV, k1, n1), BF),
                   jax.ShapeDtypeStruct((SMALL_ROWS, D), F32)],
        compiler_params=_params(("arbitrary", "arbitrary")),
    )(order, a, b, small)


def _dw_plain_call(name, me, a, b, a_spec, b_spec, split, k1, n1, s):
    nb = NDEV // split
    r = k1 // split
    ts = min(TS_DW, s)
    ns = s // ts

    def slab(i, me_ref):
        return i

    def body(me_ref, a_ref, b_ref, own_ref, part_ref, acc):
        i = pl.program_id(0)
        j = pl.program_id(1)
        p = _dot_tn(a_ref[...], b_ref[...])

        @pl.when(j == 0)
        def _():
            acc[...] = p

        @pl.when(j > 0)
        def _():
            acc[...] += p

        @pl.when(j == ns - 1)
        def _():
            part_ref[...] = acc[...].astype(BF)

            @pl.when(i == me_ref[0] // split)
            def _():
                off = pl.multiple_of(lax.rem(me_ref[0], split) * r, RC)
                own_ref[...] = acc[pl.ds(off, r), :]

    grid_spec = pltpu.PrefetchScalarGridSpec(
        num_scalar_prefetch=1, grid=(nb, ns),
        in_specs=[a_spec(ts, slab), b_spec(ts, slab)],
        out_specs=[pl.BlockSpec((r, n1), lambda i, j, me_ref: (0, 0)),
                   pl.BlockSpec((None, k1, n1), lambda i, j, me_ref: (i, 0, 0))],
        scratch_shapes=[pltpu.VMEM((k1, n1), F32)])
    own, part = pl.pallas_call(
        body, name=name, grid_spec=grid_spec,
        out_shape=[jax.ShapeDtypeStruct((r, n1), F32), jax.ShapeDtypeStruct((nb, k1, n1), BF)],
        compiler_params=_params(("arbitrary", "arbitrary")),
    )(me, a, b)
    return own, part.reshape(NDEV, r, n1)


def _rows2d(w):
    return lambda ts, slab: pl.BlockSpec((ts, w), lambda i, j, me_ref: (j, 0))


def _cols2d(w):
    return lambda ts, slab: pl.BlockSpec((ts, w), lambda i, j, me_ref: (j, slab(i, me_ref)))


def _blk3d(w):
    return lambda ts, slab: pl.BlockSpec((None, ts, w), lambda i, j, me_ref: (slab(i, me_ref), j, 0))


_BC1 = 1.0 - ADAM_B1 ** ADAM_STEP
_BC2 = 1.0 - ADAM_B2 ** ADAM_STEP


def _adamw_math(w, g, m, v):
    m = ADAM_B1 * m + (1.0 - ADAM_B1) * g
    v = ADAM_B2 * v + (1.0 - ADAM_B2) * (g * g)
    m_hat = m / _BC1
    v_hat = v / _BC2
    delta = -ADAM_LR * (m_hat / (jnp.sqrt(v_hat) + ADAM_EPS) + ADAM_WD * w)
    return delta, m, v


def _row_tile(r):
    for t in (256, 176, 128, 64, 32, 16, 8):
        if r % t == 0:
            return t
    return r


def _reduce_adamw_call(name, sel, own, recv, w, m, v):
    r, c = own.shape
    tr = _row_tile(r)

    def body(sel_ref, own_ref, recv_ref, w_ref, m_ref, v_ref, g_ref, d_ref, nm_ref, nv_ref):
        g = jnp.zeros((tr, c), F32)
        for sdev in range(NDEV):
            part = jnp.where(sel_ref[sdev] == 1, recv_ref[sdev].astype(F32), 0.0)
            g = g + jnp.where(sel_ref[sdev] == 2, own_ref[...], part)
        g_ref[...] = g
        d_ref[...], nm_ref[...], nv_ref[...] = _adamw_math(w_ref[...], g, m_ref[...], v_ref[...])

    tile = pl.BlockSpec((tr, c), lambda i, me_ref: (i, 0))
    grid_spec = pltpu.PrefetchScalarGridSpec(
        num_scalar_prefetch=1, grid=(r // tr,),
        in_specs=[tile, pl.BlockSpec((NDEV, tr, c), lambda i, me_ref: (0, i, 0)), tile, tile, tile],
        out_specs=[tile] * 4)
    return pl.pallas_call(
        body, name=name, grid_spec=grid_spec,
        out_shape=[jax.ShapeDtypeStruct((r, c), F32)] * 4,
        compiler_params=_params(("parallel",)),
    )(sel, own, recv, w, m, v)


def _adamw_call(name, w, g, m, v):
    r, c = w.shape
    tr = _row_tile(r)

    def body(w_ref, g_ref, m_ref, v_ref, d_ref, nm_ref, nv_ref):
        d_ref[...], nm_ref[...], nv_ref[...] = _adamw_math(w_ref[...], g_ref[...], m_ref[...], v_ref[...])

    tile = pl.BlockSpec((tr, c), lambda i: (i, 0))
    return pl.pallas_call(
        body, name=name, grid=(r // tr,), in_specs=[tile] * 4, out_specs=[tile] * 3,
        out_shape=[jax.ShapeDtypeStruct((r, c), F32)] * 3,
        compiler_params=_params(("parallel",)),
    )(w, g, m, v)


def _me():
    return lax.axis_index("x"), lax.axis_index("y"), lax.axis_index("c")


def _flip(pos, r):
    x, y, c = pos
    return (1 - x if r & 4 else x, 1 - y if r & 2 else y, 1 - c if r & 1 else c)


def _lin(pos):
    return pos[0] * 4 + pos[1] * 2 + pos[2]


class _SmallAllReduce:
    def __init__(self, g_ref, out_ref, rbuf, send1, recv1, send2, recv2):
        self.g, self.out, self.rbuf = g_ref, out_ref, rbuf
        self.sems = (send1, recv1, send2, recv2)
        self.me = _me()
        self.mi = _lin(self.me)

    @staticmethod
    def _rows(d):
        return pl.ds(pl.multiple_of(d * SMALL_PER, 8), SMALL_PER)

    def _scatter(self, r, outgoing):
        peer = _flip(self.me, r)
        src, dst = (_lin(peer), self.mi) if outgoing else (self.mi, _lin(peer))
        return pltpu.make_async_remote_copy(
            src_ref=self.g.at[self._rows(src)], dst_ref=self.rbuf.at[dst],
            send_sem=self.sems[0].at[r - 1], recv_sem=self.sems[1].at[r - 1], device_id=peer, device_id_type=MESH)

    def _spread(self, r, outgoing):
        peer = _flip(self.me, r)
        rows = self._rows(self.mi if outgoing else _lin(peer))
        return pltpu.make_async_remote_copy(
            src_ref=self.out.at[rows], dst_ref=self.out.at[rows],
            send_sem=self.sems[2].at[r - 1], recv_sem=self.sems[3].at[r - 1], device_id=peer, device_id_type=MESH)

    def scatter(self):
        for r in range(1, NDEV):
            self._scatter(r, True).start()
        self.rbuf[self.mi] = self.g[self._rows(self.mi), :]

    def reduce(self):
        for r in range(1, NDEV):
            self._scatter(r, False).wait_recv()
        for r in range(1, NDEV):
            self._scatter(r, True).wait_send()
        tot = self.rbuf[0]
        for d in range(1, NDEV):
            tot = tot + self.rbuf[d]
        self.out[self._rows(self.mi), :] = tot
        for r in range(1, NDEV):
            self._spread(r, True).start()

    def finish(self):
        for r in range(1, NDEV):
            self._spread(r, False).wait_recv()
        for r in range(1, NDEV):
            self._spread(r, True).wait_send()


def _head_blocks(w):
    z = jnp.zeros((64, 64), w.dtype)
    groups = []
    for q in range(NQ):
        rows = [jnp.concatenate([w[4 * q + a] if a == b else z for b in range(4)], axis=1) for a in range(4)]
        groups.append(jnp.concatenate(rows, axis=0))
    return jnp.stack(groups)


def _head_unblocks(g):
    return jnp.stack([g[q, 64 * a:64 * a + 64, 64 * a:64 * a + 64] for q in range(NQ) for a in range(4)])


def _local_step(x, tgt, p, me, order, order_dw):
    s = x.shape[0]
    vec = lambda a: a.reshape(1, D)
    gmix, gffn, gfin = vec(p["norm_mix_g"]), vec(p["norm_ffn_g"]), vec(p["norm_final_g"])
    cb, ba, bx, lam = vec(p["conv_b"]), vec(p["rg_ba"]), vec(p["rg_bx"]), vec(p["rg_lambda"])
    lng, lnb = vec(p["sgu_ln_g"]), vec(p["sgu_ln_b"])
    wax = jnp.concatenate([_head_blocks(p["rg_wa"]), _head_blocks(p["rg_wx"])], axis=2).astype(BF)
    tril = jnp.tril(jnp.ones((CHUNK, CHUNK), bool))
    ws = jnp.where(tril[None], p["sgu_ws"], 0.0)
    wtr = ws.astype(BF)
    wtrt = jnp.swapaxes(ws, 1, 2).astype(BF)
    bias = jnp.repeat(p["sgu_bs"].T, GW, axis=1)
    shard = {k: p[k].astype(BF) for k in _BIG}

    proj, h1, w_in, (wpa, wpb, wo, cw) = _proj_gather_call(
        x, gmix, shard["w_in"], order, [shard["w_proj_a"], shard["w_proj_b"], shard["w_out"], p["conv_w"]])
    wpa, wpb, wo = (t.reshape(D, D) for t in (wpa, wpb, wo))
    cw = jnp.swapaxes(cw, 0, 1).reshape(4, D)
    (h, pa, pb, x1, *gates, ya, yb, mg), (wgu, wdn) = _mixer_fwd_call(
        proj, x, cw, cb, ba, bx, lam, lng, lnb, wax, wtr, bias, wpa, wpb, wo, [shard["w_gate_up"], shard["w_down"]])
    wdn = wdn.reshape(NDEV // 2, -1, D)
    nb, _, wb = w_in.shape
    fb = wgu.shape[2]
    nh = wdn.shape[0]
    dx1, dx1b, h2, act, dgu, dx2b, facc = _ffn_call(x1, tgt, gffn, gfin, wgu, wdn)
    assert nb == NDEV and 2 * nh == NDEV
    own_gu, part_gu = _dw_plain_call("dw_gate_up", me, dgu, h2, _blk3d(fb), _rows2d(D), 1, fb, D, s)
    own_dn, part_dn = _dw_plain_call("dw_down", me, act, dx2b, _blk3d(fb), _rows2d(D), 2, fb, D, s)
    (dprojb, dya, dpa, dpb, bvec, dws, dbs), (recv_dn,) = _mixer_bwd_pre_call(
        dx1b, proj, pa, pb, lng, lnb, wtr, wtrt, bias, wpa, wpb, wo, [part_dn])
    own_pa, part_pa = _dw_plain_call("dw_proj_a", me, ya, dpa, _rows2d(D), _rows2d(D), NDEV, D, D, s)
    own_pb, part_pb = _dw_plain_call("dw_proj_b", me, yb, dpb, _rows2d(D), _rows2d(D), NDEV, D, D, s)
    own_wo, part_wo = _dw_plain_call("dw_out", me, mg, dx1b, _rows2d(D), _rows2d(D), NDEV, D, D, s)
    (dproj, dx, svec, dwax), (recv_gu, recv_pa, recv_pb, recv_wo) = _mixer_bwd_seq_call(
        dprojb, dya, proj, h, x, dx1, gates, gmix, w_in, cw, lam, wax, [part_gu, part_pa, part_pb, part_wo])
    small = {
        "norm_mix_g": svec[8], "norm_ffn_g": facc[0], "norm_final_g": facc[1],
        "conv_b": svec[0], "rg_ba": svec[1], "rg_bx": svec[2], "rg_lambda": svec[3],
        "sgu_ln_g": bvec[0], "sgu_ln_b": bvec[1],
        "rg_wa": _head_unblocks(dwax[:, :, 0:QW]), "rg_wx": _head_unblocks(dwax[:, :, QW:2 * QW]),
        "sgu_ws": dws, "sgu_bs": dbs[:, ::GW].T,
    }
    packed = _pack_small(small, svec[4:8], facc[2])
    own_in, recv_in, gsum = _dw_exchange_call("dw_in", order_dw, h1, dproj, _rows2d(D), _cols2d(wb), D, wb, s, packed)
    dw = {
        "w_gate_up": (own_gu, recv_gu), "w_down": (own_dn, recv_dn), "w_proj_a": (own_pa, recv_pa),
        "w_proj_b": (own_pb, recv_pb), "w_out": (own_wo, recv_wo), "w_in": (own_in, recv_in),
    }
    return gsum, dx, dw


_BIG = ("w_in", "w_gate_up", "w_down", "w_proj_a", "w_proj_b", "w_out")
_VEC_ROWS = ("norm_mix_g", "norm_ffn_g", "norm_final_g", "conv_b", "rg_ba", "rg_bx", "rg_lambda",
             "sgu_ln_g", "sgu_ln_b", "sgu_bs")
_WEIGHTS = ("norm_mix_g", "w_in", "conv_w", "conv_b", "rg_wa", "rg_ba", "rg_wx", "rg_bx", "rg_lambda",
            "sgu_ln_g", "sgu_ln_b", "sgu_ws", "sgu_bs", "w_proj_a", "w_proj_b", "w_out", "norm_ffn_g",
            "w_gate_up", "w_down", "norm_final_g")


def _pack_small(t, conv_w, extra=None):
    extra = jnp.zeros((1, D), F32) if extra is None else extra.reshape(1, D)
    head = jnp.concatenate([t[k].reshape(1, D) for k in _VEC_ROWS] + [conv_w, extra, jnp.zeros((1, D), F32)], axis=0)
    return jnp.concatenate([head, t["rg_wa"].reshape(64, D), t["rg_wx"].reshape(64, D), t["sgu_ws"].reshape(128, D),
                            jnp.zeros((SMALL_ROWS - 272, D), F32)], axis=0)


def _unpack_small(a):
    out = {k: a[j] for j, k in enumerate(_VEC_ROWS)}
    out["conv_w"] = a[10:14]
    out["rg_wa"] = a[16:80].reshape(16, 64, 64)
    out["rg_wx"] = a[80:144].reshape(16, 64, 64)
    out["sgu_ws"] = a[144:272].reshape(NGRP, CHUNK, CHUNK)
    return out


def kernel(x, norm_mix_g, w_in, conv_w, conv_b, rg_wa, rg_ba, rg_wx, rg_bx, rg_lambda, sgu_ln_g, sgu_ln_b, sgu_ws, sgu_bs, w_proj_a, w_proj_b, w_out, norm_ffn_g, w_gate_up, w_down, norm_final_g, loss_target, m_norm_mix_g, m_w_in, m_conv_w, m_conv_b, m_rg_wa, m_rg_ba, m_rg_wx, m_rg_bx, m_rg_lambda, m_sgu_ln_g, m_sgu_ln_b, m_sgu_ws, m_sgu_bs, m_w_proj_a, m_w_proj_b, m_w_out, m_norm_ffn_g, m_w_gate_up, m_w_down, m_norm_final_g, v_norm_mix_g, v_w_in, v_conv_w, v_conv_b, v_rg_wa, v_rg_ba, v_rg_wx, v_rg_bx, v_rg_lambda, v_sgu_ln_g, v_sgu_ln_b, v_sgu_ws, v_sgu_bs, v_w_proj_a, v_w_proj_b, v_w_out, v_norm_ffn_g, v_w_gate_up, v_w_down, v_norm_final_g):
    args = dict(locals())
    w = {k: args[k] for k in _WEIGHTS}
    m = {k: args["m_" + k] for k in _WEIGHTS}
    v = {k: args["v_" + k] for k in _WEIGHTS}
    for d in (w, m, v):
        for k in _WEIGHTS:
            if k != "norm_final_g":
                d[k] = d[k][0]
    me = _lin(_me())
    me1 = me.reshape(1).astype(jnp.int32)

    order = jnp.bitwise_xor(me, jnp.array(_PASS_FLIPS, jnp.int32)).astype(jnp.int32)
    order_dw = jnp.bitwise_xor(me, jnp.array(_DW_FLIPS, jnp.int32)).astype(jnp.int32)

    gsum, dx, dw = _local_step(x[0], loss_target[0], w, me1, order, order_dw)

    peer = jnp.bitwise_xor(jnp.arange(NDEV, dtype=jnp.int32), me)
    sel_direct = jnp.where(peer == 0, 2, 1).astype(jnp.int32)
    sel_two_level = jnp.where(peer == 0, 2, jnp.where((peer == 1) | (peer % 2 == 0), 1, 0)).astype(jnp.int32)
    grads, delta, new_m, new_v = {}, {}, {}, {}
    for k in _BIG:
        own, recv = dw[k]
        flip = own.shape != w[k].shape
        wmv = [jnp.swapaxes(t, 0, 1) if flip else t for t in (w[k], m[k], v[k])]
        res = _reduce_adamw_call("adamw_" + k, sel_two_level if k == "w_in" else sel_direct, own, recv, *wmv)
        grads[k], delta[k], new_m[k], new_v[k] = (jnp.swapaxes(t, 0, 1) if flip else t for t in res)

    loss = (0.5 / D) * jnp.sum(gsum[14])
    zc = jnp.zeros((4, D), F32)
    d_s, m_s, v_s = _adamw_call("adamw_small", _pack_small(w, zc), gsum, _pack_small(m, zc), _pack_small(v, zc))
    gs, ds, ms, vs = _unpack_small(gsum), _unpack_small(d_s), _unpack_small(m_s), _unpack_small(v_s)
    g_cw = lax.dynamic_slice(gs["conv_w"], (0, me * 128), (4, 128))
    ds["conv_w"], ms["conv_w"], vs["conv_w"] = _adamw_call("adamw_conv_w", w["conv_w"], g_cw, m["conv_w"], v["conv_w"])
    gs["conv_w"] = g_cw
    for k in _WEIGHTS:
        if k not in _BIG:
            shp = w[k].shape
            grads[k], delta[k], new_m[k], new_v[k] = (t[k].reshape(shp) for t in (gs, ds, ms, vs))

    def lift(t, k):
        return t[k] if k == "norm_final_g" else t[k][None]

    outs = [loss, dx[None]]
    for t in (grads, delta, new_m, new_v):
        outs += [lift(t, k) for k in _WEIGHTS]
    return tuple(outs)
```

```python
import functools

import jax
import jax.numpy as jnp
from jax import lax
from jax.experimental import pallas as pl
from jax.experimental.pallas import tpu as pltpu

F32 = jnp.float32
BF = jnp.bfloat16

D = 1024
NDEV = 8
EPS = 1e-6
RG_C = 8.0
CHUNK = 128
NGRP = 8
GW = 128
NQ = 4
QW = 256
RC = 16
LANES = 128
SMALL_ROWS = 320
SMALL_PER = SMALL_ROWS // NDEV

ADAM_LR = 0.001
ADAM_B1 = 0.9
ADAM_B2 = 0.999
ADAM_EPS = 1e-08
ADAM_WD = 0.01
ADAM_STEP = 10

VMEM_LIMIT = 60 * 1024 * 1024

MESH = pl.DeviceIdType.MESH


def _rows(n, fn, unroll=2, rc=RC):
    def body(i, c):
        fn(pl.multiple_of(i * rc, rc))
        return c
    lax.fori_loop(0, n // rc, body, 0, unroll=unroll)


def _fold(v):
    return jnp.sum(v.reshape(v.shape[0] // RC, RC, v.shape[1]), axis=0)


def _dot(a, b):
    return jnp.dot(a, b, preferred_element_type=F32)


def _dot_nt(a, b):
    return lax.dot_general(a, b, (((1,), (1,)), ((), ())), preferred_element_type=F32)


def _dot_tn(a, b):
    return lax.dot_general(a, b, (((0,), (0,)), ((), ())), preferred_element_type=F32)


_GC = 0.7978845608028654
_GK = 0.044715


def _gelu(x):
    t = jnp.tanh(_GC * (x + _GK * (x * x * x)))
    return x * (0.5 * (1.0 + t))


def _gelu_grad(x):
    x2 = x * x
    t = jnp.tanh(_GC * (x + _GK * (x2 * x)))
    cdf = 0.5 * (1.0 + t)
    dg = cdf + (0.5 * x) * (1.0 - t * t) * (_GC * (1.0 + (3.0 * _GK) * x2))
    return x * cdf, dg


def _sigmoid(x):
    return jax.nn.sigmoid(x)


def _log1p(e):
    u = 1.0 + e
    d = u - 1.0
    return jnp.where(d == 0.0, e, jnp.log(u) * (e / jnp.where(d == 0.0, 1.0, d)))


def _softplus(z):
    return jnp.maximum(z, 0.0) + _log1p(jnp.exp(-jnp.abs(z)))


def _neg_expm1(z):
    u = jnp.exp(z)
    lu = jnp.log(u)
    k = (1.0 - u) * (z / jnp.where(lu == 0.0, 1.0, lu))
    small = jnp.where(lu == 0.0, -z, k)
    return jnp.where(z > -0.5, small, 1.0 - u)


def _shift_back(prev8, cur, j):
    cat = jnp.concatenate([prev8, cur], axis=0)
    return pltpu.roll(cat, j, 0)[8:8 + cur.shape[0]]


def _shift_fwd(cur, next8, j):
    cat = jnp.concatenate([cur, next8], axis=0)
    n = cat.shape[0]
    return pltpu.roll(cat, n - j, 0)[0:cur.shape[0]]


def _const_spec(shape):
    nd = len(shape)
    return pl.BlockSpec(shape, lambda *_: (0,) * nd, pipeline_mode=pl.Buffered(1))


def _params(sem):
    return pltpu.CompilerParams(dimension_semantics=sem, vmem_limit_bytes=VMEM_LIMIT)


TM_PROJ = 1024
TM_MIX = 256
TM_FFN = 256
TM_DX = 512
TS_DW = 4096


_CHIPS = (4, 2, 6)
_PASS_FLIPS = (0, 1, 4, 2, 6, 5, 3, 7)


class _Gather:
    def __init__(self, ins, outs, send_sems, recv_sems, local_sems):
        self.ins, self.outs = ins, outs
        self.send_sems, self.recv_sems, self.local_sems = send_sems, recv_sems, local_sems
        self.me = _me()
        self.sibling = _flip(self.me, 1)

    def _copy(self, a, kind, block, to, src=None):
        dst = self.outs[a].at[_lin(block)]
        return pltpu.make_async_remote_copy(
            src_ref=dst if src is None else src, dst_ref=dst,
            send_sem=self.send_sems.at[a, kind], recv_sem=self.recv_sems.at[a, kind],
            device_id=to, device_id_type=MESH)

    def _local(self, a):
        return pltpu.make_async_copy(self.ins[a], self.outs[a].at[_lin(self.me)], self.local_sems.at[a])

    def start(self):
        for a in range(len(self.ins)):
            self._local(a).start()
            self._copy(a, 0, self.me, self.sibling, src=self.ins[a]).start()
            for j, f in enumerate(_CHIPS):
                self._copy(a, 1 + j, self.me, _flip(self.me, f), src=self.ins[a]).start()

    def forward(self):
        for j, f in enumerate(_CHIPS):
            for a in range(len(self.ins)):
                self._copy(a, 1 + j, _flip(self.me, f), self.me).wait_recv()
                self._copy(a, 4 + j, _flip(self.me, f), self.sibling).start()

    def finish(self):
        for a in range(len(self.ins)):
            self._copy(a, 0, self.sibling, self.me).wait_recv()
            for j, f in enumerate(_CHIPS):
                self._copy(a, 4 + j, _flip(self.me, f | 1), self.me).wait_recv()
            self._copy(a, 0, self.me, self.sibling, src=self.ins[a]).wait_send()
            for j, f in enumerate(_CHIPS):
                self._copy(a, 1 + j, self.me, _flip(self.me, f), src=self.ins[a]).wait_send()
                self._copy(a, 4 + j, _flip(self.me, f), self.sibling).wait_send()
            self._local(a).wait()


class _Exchange:
    def __init__(self, ins, outs, send_sems, recv_sems, local_sems):
        self.ins, self.outs = ins, outs
        self.send_sems, self.recv_sems, self.local_sems = send_sems, recv_sems, local_sems
        self.me = _me()

    def _copy(self, a, r, outgoing):
        peer = _flip(self.me, r)
        src, dst = (peer, self.me) if outgoing else (self.me, peer)
        return pltpu.make_async_remote_copy(
            src_ref=self.ins[a].at[_lin(src)], dst_ref=self.outs[a].at[_lin(dst)],
            send_sem=self.send_sems.at[a, r - 1], recv_sem=self.recv_sems.at[a, r - 1],
            device_id=peer, device_id_type=MESH)

    def _local(self, a):
        mi = _lin(self.me)
        return pltpu.make_async_copy(self.ins[a].at[mi], self.outs[a].at[mi], self.local_sems.at[a])

    def start(self):
        for a in range(len(self.ins)):
            self._local(a).start()
        for r in range(1, NDEV):
            for a in range(len(self.ins)):
                self._copy(a, r, True).start()

    def finish(self):
        for r in range(1, NDEV):
            for a in range(len(self.ins)):
                self._copy(a, r, False).wait_recv()
        for r in range(1, NDEV):
            for a in range(len(self.ins)):
                self._copy(a, r, True).wait_send()
        for a in range(len(self.ins)):
            self._local(a).wait()


def _gather_sems(n):
    return [pltpu.SemaphoreType.DMA((n, 7)), pltpu.SemaphoreType.DMA((n, 7)), pltpu.SemaphoreType.DMA((n,))]


def _proj_gather_call(x, gmix, w_shard, order, extras):
    s = x.shape[0]
    tm = min(TM_PROJ, s)
    nt = s // tm
    wb = w_shard.shape[1]
    n = len(extras)

    def body(order_ref, x_ref, g_ref, wsh_ref, *rest):
        ex_in = rest[:n]
        proj_hbm, h1_hbm, wall_hbm = rest[n:n + 3]
        ex_out = rest[n + 3:2 * n + 3]
        wall, hc, pbuf, ws_send, ws_recv, ex_send, ex_recv, ex_local, out_sems, psems = rest[2 * n + 3:]
        k = pl.program_id(0)
        i = pl.program_id(1)
        me = _me()
        sibling = _flip(me, 1)
        gather = _Gather(ex_in, ex_out, ex_send, ex_recv, ex_local)

        def wcopy(kind, block, to):
            ref = wall.at[_lin(block)]
            return pltpu.make_async_remote_copy(
                src_ref=ref, dst_ref=ref, send_sem=ws_send.at[kind], recv_sem=ws_recv.at[kind],
                device_id=to, device_id_type=MESH)

        head = i == 0

        @pl.when(head & (k == 0))
        def _():
            wall[_lin(me)] = wsh_ref[...]
            wcopy(0, me, sibling).start()
            for j, f in enumerate(_CHIPS):
                wcopy(1 + j, me, _flip(me, f)).start()
            gather.start()

        out_w = pltpu.make_async_copy(wall, wall_hbm, out_sems.at[0])
        out_h = pltpu.make_async_copy(hc, h1_hbm, out_sems.at[1])

        @pl.when(head & (k == 1))
        def _():
            wcopy(0, sibling, me).wait_recv()
            out_h.start()

        for j, f in enumerate(_CHIPS):
            @pl.when(head & (k == 2 + j))
            def _(j=j, f=f):
                wcopy(1 + j, _flip(me, f), me).wait_recv()
                wcopy(4 + j, _flip(me, f), sibling).start()

            @pl.when(head & (k == 5 + j))
            def _(j=j, f=f):
                wcopy(4 + j, _flip(me, f | 1), me).wait_recv()
                if 5 + j == NDEV - 1:
                    out_w.start()

        pl.when(head & (k == 5))(gather.forward)

        base = pl.multiple_of(i * tm, tm)

        @pl.when(k == 0)
        def _():
            g = g_ref[...]

            def norm(r0):
                xx = x_ref[pl.ds(r0, RC), :]
                r = lax.rsqrt(jnp.mean(xx * xx, axis=-1, keepdims=True) + EPS)
                hc[pl.ds(base + r0, RC), :] = ((xx * r) * g).astype(BF)
            _rows(tm, norm, unroll=4)

        blk = order_ref[k]
        wk = wall.at[blk]
        step = k * nt + i
        slot = lax.rem(step, 2)
        subs = [(r0, min(TM_MIX, tm - r0)) for r0 in range(0, tm, TM_MIX)]

        def out_copy(q, r0, rs):
            return pltpu.make_async_copy(
                pbuf.at[slot, pl.ds(r0, rs)],
                proj_hbm.at[pl.ds(base + r0, rs), pl.ds(pl.multiple_of(blk * wb, 128), wb)], psems.at[slot, q])

        @pl.when(step >= 2)
        def _():
            for q, (r0, rs) in enumerate(subs):
                out_copy(q, r0, rs).wait()

        for q, (r0, rs) in enumerate(subs):
            pbuf[slot, r0:r0 + rs, :] = _dot(hc[pl.ds(base + r0, rs), :], wk[...])
            out_copy(q, r0, rs).start()

        @pl.when((k == NDEV - 1) & (i == nt - 1))
        def _():
            wcopy(0, me, sibling).wait_send()
            for j, f in enumerate(_CHIPS):
                wcopy(1 + j, me, _flip(me, f)).wait_send()
                wcopy(4 + j, _flip(me, f), sibling).wait_send()
            gather.finish()
            out_w.wait()
            out_h.wait()
            for q, (r0, rs) in enumerate(subs):
                out_copy(q, r0, rs).wait()
                pltpu.make_async_copy(pbuf.at[1 - slot, pl.ds(r0, rs)], proj_hbm.at[pl.ds(r0, rs), pl.ds(0, wb)],
                                      psems.at[1 - slot, q]).wait()

    any_spec = pl.BlockSpec(memory_space=pl.ANY)
    grid_spec = pltpu.PrefetchScalarGridSpec(
        num_scalar_prefetch=1, grid=(NDEV, nt),
        in_specs=[pl.BlockSpec((tm, D), lambda k, i, o: (jnp.where(k == 0, i, nt - 1), 0)),
                  pl.BlockSpec((1, D), lambda k, i, o: (0, 0)),
                  pl.BlockSpec(w_shard.shape, lambda k, i, o: (0, 0))] + [any_spec] * n,
        out_specs=[any_spec, any_spec, any_spec] + [any_spec] * n,
        scratch_shapes=[pltpu.VMEM((NDEV,) + w_shard.shape, BF), pltpu.VMEM((s, D), BF), pltpu.VMEM((2, tm, wb), F32),
                        pltpu.SemaphoreType.DMA((7,)), pltpu.SemaphoreType.DMA((7,))] + _gather_sems(n)
        + [pltpu.SemaphoreType.DMA((2,)), pltpu.SemaphoreType.DMA((2, -(-tm // TM_MIX)))])
    outs = pl.pallas_call(
        body, name="proj_gather", grid_spec=grid_spec,
        out_shape=[jax.ShapeDtypeStruct((s, NDEV * wb), F32), jax.ShapeDtypeStruct((s, D), BF),
                   jax.ShapeDtypeStruct((NDEV,) + w_shard.shape, BF)]
        + [jax.ShapeDtypeStruct((NDEV,) + e.shape, e.dtype) for e in extras],
        compiler_params=_params(("arbitrary", "arbitrary")),
    )(order, x, gmix, w_shard, *extras)
    return outs[0], outs[1], outs[2], outs[3:]


def _conv_tile(rx, prev8, cw_ref, cb):
    xc = cb + cw_ref[3:4, :] * rx
    for j in (1, 2, 3):
        xc = xc + cw_ref[3 - j:4 - j, :] * _shift_back(prev8, rx, j)
    return xc


def _gate_tile(xcb, wax_ref, ba, bx, sp, q):
    cs = slice(q * QW, (q + 1) * QW)
    z = _dot(xcb[:, cs], wax_ref[q])
    r = _sigmoid(z[:, 0:QW] + ba[:, cs])
    ig = _sigmoid(z[:, QW:2 * QW] + bx[:, cs])
    return r, ig, (-RG_C * r) * sp[:, cs]


def _scan_fwd(a_s, b_s, out_ref, h0, n):
    rowi = lax.broadcasted_iota(jnp.int32, (8, D), 0)

    def block(t, h):
        rows = pl.ds(t * 8, 8)
        a = a_s[rows, :]
        b = b_s[rows, :]
        for d in (1, 2, 4):
            m = rowi >= d
            b = jnp.where(m, a * pltpu.roll(b, d, 0) + b, b)
            a = jnp.where(m, a * pltpu.roll(a, d, 0), a)
        hb = b + a * h
        out_ref[rows, :] = hb
        return hb[7:8, :]
    h = h0
    for t in range(n // 8):
        h = block(t, h)
    return h


def _scan_bwd(a_s, lm_s, c0, n):
    rowi = lax.broadcasted_iota(jnp.int32, (8, D), 0)
    nblk = n // 8

    def block(k, cin):
        rows = pl.ds((nblk - 1 - k) * 8, 8)
        a = a_s[rows, :]
        dh = lm_s[rows, :]
        b = a * dh
        for d in (1, 2, 4):
            m = rowi < 8 - d
            b = jnp.where(m, a * pltpu.roll(b, 8 - d, 0) + b, b)
            a = jnp.where(m, a * pltpu.roll(a, 8 - d, 0), a)
        mu = b + a * cin
        lm_s[rows, :] = dh + jnp.where(rowi < 7, pltpu.roll(mu, 7, 0), cin)
        return mu[0:1, :]
    c = c0
    for k in range(nblk):
        c = block(k, c)
    return c


def _mixer_fwd_call(proj, x, cw, cb, ba, bx, lam, lng, lnb, wax, wtr, bias, wpa, wpb, wo, shards):
    s = x.shape[0]
    tm = min(TM_MIX, s)
    nt = s // tm
    pw = proj.shape[1]
    n = len(shards)

    def body(proj_ref, x_ref, cw_ref, cb_ref, ba_ref, bx_ref, lam_ref, lng_ref, lnb_ref, wax_ref, wtr_ref,
             bias_ref, wpa_ref, wpb_ref, wo_ref, *rest):
        sh_in = rest[:n]
        h_ref, pa_ref, pb_ref, x1_ref, xc_ref, r_ref, ig_ref, a_s, m_ref, ya_ref, yb_ref, mg_ref = rest[n:n + 12]
        sh_out = rest[n + 12:2 * n + 12]
        prev_s, b_s, hc_s, vn_s, mx_s, g_send, g_recv, g_local = rest[2 * n + 12:]
        i = pl.program_id(0)
        gather = _Gather(sh_in, sh_out, g_send, g_recv, g_local)

        @pl.when(i == 0)
        def _():
            gather.start()
            prev_s[...] = jnp.zeros((8, D), F32)
            hc_s[...] = jnp.zeros((8, D), F32)

        pl.when(i == nt // 2)(gather.forward)

        rx = proj_ref[:, 0:D]
        xc = _conv_tile(rx, prev_s[...], cw_ref, cb_ref[...])
        prev_s[...] = rx[tm - 8:tm, :]
        xc_ref[...] = xc.astype(BF)
        xcb = xc.astype(BF)
        sp = _softplus(-lam_ref[...])
        ba = ba_ref[...]
        bx = bx_ref[...]
        for q in range(NQ):
            cs = slice(q * QW, (q + 1) * QW)
            r, ig, la = _gate_tile(xcb, wax_ref, ba, bx, sp, q)
            mq = jnp.sqrt(_neg_expm1(2.0 * la))
            r_ref[:, cs] = r.astype(BF)
            ig_ref[:, cs] = ig.astype(BF)
            m_ref[:, cs] = mq.astype(BF)
            a_s[:, cs] = jnp.exp(la)
            b_s[:, cs] = mq * (ig * xc[:, cs])

        gv = _gelu(proj_ref[:, 3 * D:4 * D])
        dv = gv - jnp.mean(gv, axis=-1, keepdims=True)
        var = jnp.mean(dv * dv, axis=-1, keepdims=True)
        vn_s[...] = ((dv * lax.rsqrt(var + EPS)) * lng_ref[...] + lnb_ref[...]).astype(BF)
        nc = tm // CHUNK
        for c in range(nc):
            rs = slice(c * CHUNK, (c + 1) * CHUNK)
            for g in range(NGRP):
                cs = slice(g * GW, (g + 1) * GW)
                mx_s[rs, cs] = _dot(wtr_ref[g], vn_s[rs, cs])
        mixed = mx_s[...] + jnp.concatenate([bias_ref[...]] * nc, axis=0)
        yb = (_gelu(proj_ref[:, 2 * D:3 * D]) * mixed).astype(BF)
        yb_ref[...] = yb
        pb = _dot(yb, wpb_ref[...])
        pb_ref[...] = pb.astype(BF)
        mx_s[...] = pb

        hc_s[0:1, :] = _scan_fwd(a_s, b_s, h_ref, hc_s[0:1, :], tm)

        ya = (_gelu(proj_ref[:, D:2 * D]) * h_ref[...]).astype(BF)
        ya_ref[...] = ya
        pa = _dot(ya, wpa_ref[...])
        pa_ref[...] = pa.astype(BF)
        mg = (_sigmoid(proj_ref[:, 4 * D:5 * D]) * pa + _sigmoid(proj_ref[:, 5 * D:6 * D]) * mx_s[...]).astype(BF)
        mg_ref[...] = mg
        x1_ref[...] = x_ref[...] + _dot(mg, wo_ref[...])

        pl.when(i == nt - 1)(gather.finish)

    tile = lambda w: pl.BlockSpec((tm, w), lambda i: (i, 0))
    vec = _const_spec((1, D))
    any_spec = pl.BlockSpec(memory_space=pl.ANY)
    outs = pl.pallas_call(
        body, name="mixer_fwd", grid=(nt,),
        in_specs=[tile(pw), tile(D), _const_spec((4, D)), vec, vec, vec, vec, vec, vec,
                  _const_spec(wax.shape), _const_spec(wtr.shape), _const_spec(bias.shape),
                  _const_spec((D, D)), _const_spec((D, D)), _const_spec((D, D))] + [any_spec] * n,
        out_specs=[tile(D)] * 12 + [any_spec] * n,
        out_shape=[jax.ShapeDtypeStruct((s, D), dt) for dt in (F32, BF, BF, F32, BF, BF, BF, F32, BF, BF, BF, BF)]
        + [jax.ShapeDtypeStruct((NDEV,) + e.shape, e.dtype) for e in shards],
        scratch_shapes=[pltpu.VMEM((8, D), F32), pltpu.VMEM((tm, D), F32),
                        pltpu.VMEM((8, D), F32), pltpu.VMEM((tm, D), BF), pltpu.VMEM((tm, D), F32)]
        + _gather_sems(n),
        compiler_params=_params(("arbitrary",)),
    )(proj, x, cw, cb, ba, bx, lam, lng, lnb, wax, wtr, bias, wpa, wpb, wo, *shards)
    return outs[:12], outs[12:]


def _ffn_call(x1, tgt, gffn, gfin, wgu, wdn):
    s = x1.shape[0]
    tm = min(TM_FFN, s)
    nt = s // tm
    nh = wdn.shape[0]
    fb = wgu.shape[2]
    fa = fb // LANES * LANES
    ft = fb - fa
    fbp = fa + LANES
    assert 2 * ft == LANES and nh % 2 == 0
    ff = nh * fb
    tail = lambda k: slice(nh * fa + k * ft, nh * fa + (k + 1) * ft)

    def body(x1_ref, tgt_ref, gffn_ref, gfin_ref, wgu_hbm, wdn_hbm,
             dx1_ref, dx1b_ref, h2_ref, act_ref, dgu_ref, dx2b_ref, acc_ref,
             wgu_s, wdn_s, tail_s, g_s, u_s, dx2_s, xh2_s, r2_s, accs, wsems):
        i = pl.program_id(0)

        def weight_copies():
            out = []
            for k in range(2 * nh):
                half, kk = divmod(k, nh)
                out.append((wgu_hbm.at[k, :, 0:fa], wgu_s.at[half, :, kk * fa:(kk + 1) * fa]))
                out.append((wgu_hbm.at[k, :, fa:fb], tail_s.at[k]))
            for k in range(nh):
                out.append((wdn_hbm.at[k, 0:fa, :], wdn_s.at[k * fa:(k + 1) * fa, :]))
                out.append((wdn_hbm.at[k, fa:fb, :], wdn_s.at[tail(k), :]))
            return [pltpu.make_async_copy(src, dst, wsems.at[n]) for n, (src, dst) in enumerate(out)]

        @pl.when(i == 0)
        def _():
            accs[...] = jnp.zeros(accs.shape, F32)
            for ref in (g_s, u_s, dx2_s, xh2_s, r2_s):
                ref[1] = jnp.zeros(ref.shape[1:], F32)
            copies = weight_copies()
            for c in copies:
                c.start()
            for c in copies:
                c.wait()
            for half in range(2):
                rest = [tail_s[half * nh + k].astype(F32) for k in range(nh)]
                wgu_s[half, :, nh * fa:ff] = jnp.concatenate(rest, axis=1).astype(BF)

        def put_blocks(ref, base, v):
            for k in range(nh):
                ref[base + k, :, 0:fa] = v[:, k * fa:(k + 1) * fa].astype(BF)
                pair = v[:, nh * fa + (k // 2) * LANES:nh * fa + (k // 2 + 1) * LANES]
                ref[base + k, :, fa:fbp] = (pltpu.roll(pair, ft, axis=1) if k % 2 else pair).astype(BF)

        gffn = gffn_ref[...]
        gfin = gfin_ref[...]
        live = jnp.where(i < nt, 1.0, 0.0)

        def forward(w):
            x1 = x1_ref[...]
            r2 = lax.rsqrt(jnp.mean(x1 * x1, axis=-1, keepdims=True) + EPS)
            xh2 = x1 * r2
            h2 = (xh2 * gffn).astype(BF)
            h2_ref[...] = h2
            r2_s[w] = r2
            xh2_s[w] = xh2

            g = _dot(h2, wgu_s[0])
            u = _dot(h2, wgu_s[1])
            g_s[w] = g
            u_s[w] = u
            act = (g * _sigmoid(g)) * u
            put_blocks(act_ref, 0, act)
            x2 = x1 + _dot(act.astype(BF), wdn_s[...])

            r3 = lax.rsqrt(jnp.mean(x2 * x2, axis=-1, keepdims=True) + EPS)
            xh = x2 * r3
            err = xh * gfin - tgt_ref[...]
            accs[2] += _fold(err * err) * live
            dy = err * (1.0 / D)
            accs[1] += _fold(dy * xh) * live
            dxh = dy * gfin
            dx2 = r3 * (dxh - xh * jnp.mean(dxh * xh, axis=-1, keepdims=True))
            dx2_s[w] = dx2
            dx2b_ref[...] = dx2.astype(BF)

        def backward(r):
            dx2 = dx2_s[r]
            da = _dot_nt(dx2.astype(BF), wdn_s[...])
            g = g_s[r]
            sg = _sigmoid(g)
            dg = (da * u_s[r]) * (sg * (1.0 + g * (1.0 - sg)))
            du = da * (g * sg)
            put_blocks(dgu_ref, 0, dg)
            put_blocks(dgu_ref, nh, du)
            dh2 = _dot_nt(dg.astype(BF), wgu_s[0]) + _dot_nt(du.astype(BF), wgu_s[1])

            xh2 = xh2_s[r]
            accs[0] += _fold(dh2 * xh2)
            dxh = dh2 * gffn
            dx1 = dx2 + r2_s[r] * (dxh - xh2 * jnp.mean(dxh * xh2, axis=-1, keepdims=True))
            dx1_ref[...] = dx1
            dx1b_ref[...] = dx1.astype(BF)

        for p in range(2):
            @pl.when(i % 2 == p)
            def _(p=p):
                forward(p)
                backward(1 - p)

        @pl.when(i == nt)
        def _():
            acc_ref[...] = jnp.zeros((8, D), F32)
            for j in range(3):
                acc_ref[j:j + 1, :] = jnp.sum(accs[j], axis=0, keepdims=True)

    fwd = lambda i: jnp.minimum(i, nt - 1)
    bwd = lambda i: jnp.maximum(i - 1, 0)
    vec = _const_spec((1, D))
    any_spec = pl.BlockSpec(memory_space=pl.ANY)
    return pl.pallas_call(
        body, name="ffn", grid=(nt + 1,),
        in_specs=[pl.BlockSpec((tm, D), lambda i: (fwd(i), 0)), pl.BlockSpec((tm, D), lambda i: (fwd(i), 0)),
                  vec, vec, any_spec, any_spec],
        out_specs=[pl.BlockSpec((tm, D), lambda i: (bwd(i), 0)), pl.BlockSpec((tm, D), lambda i: (bwd(i), 0)),
                   pl.BlockSpec((tm, D), lambda i: (fwd(i), 0)),
                   pl.BlockSpec((nh, tm, fbp), lambda i: (0, fwd(i), 0)),
                   pl.BlockSpec((2 * nh, tm, fbp), lambda i: (0, bwd(i), 0)),
                   pl.BlockSpec((tm, D), lambda i: (fwd(i), 0)), pl.BlockSpec((8, D), lambda i: (0, 0))],
        out_shape=[jax.ShapeDtypeStruct((s, D), F32), jax.ShapeDtypeStruct((s, D), BF),
                   jax.ShapeDtypeStruct((s, D), BF), jax.ShapeDtypeStruct((nh, s, fbp), BF),
                   jax.ShapeDtypeStruct((2 * nh, s, fbp), BF), jax.ShapeDtypeStruct((s, D), BF),
                   jax.ShapeDtypeStruct((8, D), F32)],
        scratch_shapes=[pltpu.VMEM((2, D, ff), BF), pltpu.VMEM((ff, D), BF), pltpu.VMEM((2 * nh, D, ft), BF),
                        pltpu.VMEM((2, tm, ff), F32), pltpu.VMEM((2, tm, ff), F32), pltpu.VMEM((2, tm, D), F32),
                        pltpu.VMEM((2, tm, D), F32), pltpu.VMEM((2, tm, 1), F32),
                        pltpu.VMEM((3, RC, D), F32), pltpu.SemaphoreType.DMA((6 * nh,))],
        compiler_params=_params(("arbitrary",)),
    )(x1, tgt, gffn, gfin, wgu, wdn)


def _mixer_bwd_pre_call(dx1b, proj, pa, pb, lng, lnb, wtr, wtrt, bias, wpa, wpb, wo, parts):
    s = dx1b.shape[0]
    tm = min(TM_MIX, s)
    nt = s // tm
    pw = proj.shape[1]
    n = len(parts)

    def body(dx1b_ref, uv_ref, gg_ref, pa_ref, pb_ref, lng_ref, lnb_ref, wtr_ref, wtrt_ref, bias_ref,
             wpa_ref, wpb_ref, wo_ref, *rest):
        ex_in = rest[:n]
        dproj_ref, dya_ref, dpa_ref, dpb_ref, vec_ref, dws_ref, dbs_ref = rest[n:n + 7]
        ex_out = rest[n + 7:2 * n + 7]
        vn_s, mx_s, dmx_s, dvn_s, accs, dbs_s, e_send, e_recv, e_local = rest[2 * n + 7:]
        i = pl.program_id(0)
        exchange = _Exchange(ex_in, ex_out, e_send, e_recv, e_local)

        @pl.when(i == 0)
        def _():
            exchange.start()
            accs[...] = jnp.zeros(accs.shape, F32)
            dbs_s[...] = jnp.zeros(dbs_s.shape, F32)
            dws_ref[...] = jnp.zeros(dws_ref.shape, F32)

        dm = _dot_nt(dx1b_ref[...], wo_ref[...])
        sa = _sigmoid(gg_ref[:, 0:D])
        sb = _sigmoid(gg_ref[:, D:2 * D])
        dpa = dm * sa
        dpb = dm * sb
        dpab = dpa.astype(BF)
        dpbb = dpb.astype(BF)
        dpa_ref[...] = dpab
        dpb_ref[...] = dpbb
        dproj_ref[:, 2 * D:3 * D] = ((dpa * pa_ref[...].astype(F32)) * (1.0 - sa)).astype(BF)
        dproj_ref[:, 3 * D:4 * D] = ((dpb * pb_ref[...].astype(F32)) * (1.0 - sb)).astype(BF)

        dya_ref[...] = _dot_nt(dpab, wpa_ref[...])
        dyb = _dot_nt(dpbb, wpb_ref[...])

        lng = lng_ref[...]
        gv, dgelu_v = _gelu_grad(uv_ref[:, D:2 * D])
        dv = gv - jnp.mean(gv, axis=-1, keepdims=True)
        rstd = lax.rsqrt(jnp.mean(dv * dv, axis=-1, keepdims=True) + EPS)
        xh = dv * rstd
        vn_s[...] = (xh * lng + lnb_ref[...]).astype(BF)

        nc = tm // CHUNK
        for c in range(nc):
            rs = slice(c * CHUNK, (c + 1) * CHUNK)
            for g in range(NGRP):
                cs = slice(g * GW, (g + 1) * GW)
                mx_s[rs, cs] = _dot(wtr_ref[g], vn_s[rs, cs])

        gu, dgelu_u = _gelu_grad(uv_ref[:, 0:D])
        mixed = mx_s[...] + jnp.concatenate([bias_ref[...]] * nc, axis=0)
        dproj_ref[:, 0:D] = ((dyb * mixed) * dgelu_u).astype(BF)
        dmx = dyb * gu
        dmx_s[...] = dmx.astype(BF)
        dbs_s[...] += jnp.sum(dmx.reshape(nc, CHUNK, D), axis=0)

        for c in range(nc):
            rs = slice(c * CHUNK, (c + 1) * CHUNK)
            for g in range(NGRP):
                cs = slice(g * GW, (g + 1) * GW)
                dvn_s[rs, cs] = _dot(wtrt_ref[g], dmx_s[rs, cs])
                dws_ref[g] += _dot_nt(dmx_s[rs, cs], vn_s[rs, cs])

        dvn = dvn_s[...]
        accs[0] += _fold(dvn * xh)
        accs[1] += _fold(dvn)
        dxh = dvn * lng
        m1 = jnp.mean(dxh, axis=-1, keepdims=True)
        m2 = jnp.mean(dxh * xh, axis=-1, keepdims=True)
        dproj_ref[:, D:2 * D] = ((rstd * (dxh - m1 - xh * m2)) * dgelu_v).astype(BF)

        @pl.when(i == nt - 1)
        def _():
            vec_ref[...] = jnp.zeros((8, D), F32)
            for j in range(2):
                vec_ref[j:j + 1, :] = jnp.sum(accs[j], axis=0, keepdims=True)
            row = lax.broadcasted_iota(jnp.int32, (CHUNK, CHUNK), 0)
            col = lax.broadcasted_iota(jnp.int32, (CHUNK, CHUNK), 1)
            for g in range(NGRP):
                dws_ref[g] = jnp.where(row >= col, dws_ref[g], 0.0)
                gs = jnp.sum(dbs_s[:, g * GW:(g + 1) * GW], axis=1, keepdims=True)
                dbs_ref[:, g * GW:(g + 1) * GW] = jnp.broadcast_to(gs, (CHUNK, GW))
            exchange.finish()

    tile = lambda w: pl.BlockSpec((tm, w), lambda i: (i, 0))
    vec = _const_spec((1, D))
    any_spec = pl.BlockSpec(memory_space=pl.ANY)
    outs = pl.pallas_call(
        body, name="mixer_bwd_pre", grid=(nt,),
        in_specs=[tile(D), pl.BlockSpec((tm, 2 * D), lambda i: (i, 1)), pl.BlockSpec((tm, 2 * D), lambda i: (i, 2)),
                  tile(D), tile(D), vec, vec, _const_spec(wtr.shape), _const_spec(wtrt.shape),
                  _const_spec(bias.shape), _const_spec((D, D)), _const_spec((D, D)), _const_spec((D, D))]
        + [any_spec] * n,
        out_specs=[tile(4 * D), tile(D), tile(D), tile(D), pl.BlockSpec((8, D), lambda i: (0, 0)),
                   pl.BlockSpec((NGRP, CHUNK, CHUNK), lambda i: (0, 0, 0)),
                   pl.BlockSpec((CHUNK, D), lambda i: (0, 0))] + [any_spec] * n,
        out_shape=[jax.ShapeDtypeStruct((s, 4 * D), BF), jax.ShapeDtypeStruct((s, D), F32),
                   jax.ShapeDtypeStruct((s, D), BF), jax.ShapeDtypeStruct((s, D), BF),
                   jax.ShapeDtypeStruct((8, D), F32), jax.ShapeDtypeStruct((NGRP, CHUNK, CHUNK), F32),
                   jax.ShapeDtypeStruct((CHUNK, D), F32)]
        + [jax.ShapeDtypeStruct(p.shape, p.dtype) for p in parts],
        scratch_shapes=[pltpu.VMEM((tm, D), BF), pltpu.VMEM((tm, D), F32), pltpu.VMEM((tm, D), BF),
                        pltpu.VMEM((tm, D), F32), pltpu.VMEM((2, RC, D), F32), pltpu.VMEM((CHUNK, D), F32)]
        + _gather_sems(n),
        compiler_params=_params(("arbitrary",)),
    )(dx1b, proj, proj, pa, pb, lng, lnb, wtr, wtrt, bias, wpa, wpb, wo, *parts)
    return outs[:7], outs[7:]


def _mixer_bwd_seq_call(dprojb, dya, proj, h, x, dx1, gates, gmix, w_in_all, cw, lam, wax, parts):
    s = dya.shape[0]
    tm = min(TM_MIX, s)
    nt = s // tm
    tb = tm // 8
    n = len(parts)
    nb, _, wb = w_in_all.shape

    def body(dpb_ref, dya_ref, xg_ref, xh8_ref, h_ref, hh8_ref, x_ref, dx1_ref, xc_s, r_s, ig_s, a_s, m_s,
             gmix_ref, win_ref, cw_ref, lam_ref, wax_ref, *rest):
        ex_in = rest[:n]
        dpa_ref, dx_ref, vec_ref, dwax_ref = rest[n:n + 4]
        ex_out = rest[n + 4:2 * n + 4]
        lm_s, dh_s, dxc_s, c_s, accs, e_send, e_recv, e_local = rest[2 * n + 4:]
        i = pl.program_id(0)
        exchange = _Exchange(ex_in, ex_out, e_send, e_recv, e_local)

        @pl.when(i == 0)
        def _():
            exchange.start()
            accs[...] = jnp.zeros(accs.shape, F32)
            dwax_ref[...] = jnp.zeros(dwax_ref.shape, F32)
            c_s[...] = jnp.zeros((8, D), F32)
            dxc_s[tm:tm + 8, :] = jnp.zeros((8, D), F32)

        first_tile = i == nt - 1
        prev8 = jnp.where(first_tile, 0.0, xh8_ref[...])
        hprev8 = jnp.where(first_tile, 0.0, hh8_ref[...])
        lamv = lam_ref[...]
        sp = _softplus(-lamv)
        hv = h_ref[...]
        g, dg = _gelu_grad(xg_ref[:, D:2 * D])
        dya = dya_ref[...]
        lm_s[...] = dya * g
        drg = ((dya * hv) * dg).astype(BF)
        dpa_ref[:, D:2 * D] = drg
        dpa_ref[:, 2 * D:6 * D] = dpb_ref[...]

        def dpb_block(k):
            return _dot_nt(dpb_ref[:, k * wb - 2 * D:(k + 1) * wb - 2 * D], win_ref[k])
        dh = (_dot_nt(drg[:, 0:2 * wb - D], win_ref[1, :, D - wb:wb])
              + _dot_nt(drg[:, 2 * wb - D:D], win_ref[2, :, 0:2 * D - 2 * wb])
              + _dot_nt(dpb_ref[:, 0:3 * wb - 2 * D], win_ref[2, :, 2 * D - 2 * wb:wb]))
        for k in range(3, 6):
            dh = dh + dpb_block(k)
        dh_s[...] = dh

        c_s[0:1, :] = _scan_bwd(a_s, lm_s, c_s[0:1, :], tm)

        hprev = _shift_back(hprev8, h_ref[...], 1)
        for q in range(NQ):
            cs = slice(q * QW, (q + 1) * QW)
            r = r_s[:, cs].astype(F32)
            ig = ig_s[:, cs].astype(F32)
            a = a_s[:, cs]
            m = m_s[:, cs].astype(F32)
            lm = lm_s[:, cs]
            xq = xc_s[:, cs].astype(F32)
            dixc = lm * m
            dla = (lm * hprev[:, cs]) * a - ((lm * (ig * xq)) * (a * a)) / m
            accs[3, :, cs] += _fold(dla * r)
            dza = (dla * (-RG_C * sp[:, cs])) * (r * (1.0 - r))
            dzx = (dixc * xq) * (ig * (1.0 - ig))
            accs[1, :, cs] += _fold(dza)
            accs[2, :, cs] += _fold(dzx)
            dz = jnp.concatenate([dza, dzx], axis=1).astype(BF)
            dxc_s[0:tm, cs] = dixc * ig + _dot_nt(dz, wax_ref[q])
            dwax_ref[q] += _dot_tn(xc_s[:, cs], dz)

        cur = dxc_s[0:tm, :]
        nx = dxc_s[tm:tm + 8, :]
        drx = cw_ref[3:4, :] * cur
        for j in (1, 2, 3):
            drx = drx + cw_ref[3 - j:4 - j, :] * _shift_fwd(cur, nx, j)
        accs[0] += _fold(cur)
        rx = xg_ref[:, 0:D]
        accs[7] += _fold(cur * rx)
        for j in (1, 2, 3):
            accs[7 - j] += _fold(cur * _shift_back(prev8, rx, j))
        dxc_s[tm:tm + 8, :] = cur[0:8, :]

        drxb = drx.astype(BF)
        dpa_ref[:, 0:D] = drxb
        dh = dh_s[...] + _dot_nt(drxb[:, 0:wb], win_ref[0]) + _dot_nt(drxb[:, wb:D], win_ref[1, :, 0:D - wb])
        for k in range(6, nb):
            dh = dh + dpb_block(k)
        xx = x_ref[...]
        rn = lax.rsqrt(jnp.mean(xx * xx, axis=-1, keepdims=True) + EPS)
        xhn = xx * rn
        accs[8] += _fold(dh * xhn)
        dxh = dh * gmix_ref[...]
        dx_ref[...] = dx1_ref[...] + rn * (dxh - xhn * jnp.mean(dxh * xhn, axis=-1, keepdims=True))

        @pl.when(i == nt - 1)
        def _():
            vec_ref[...] = jnp.zeros((16, D), F32)
            for j in range(9):
                vec_ref[j:j + 1, :] = jnp.sum(accs[j], axis=0, keepdims=True)
            vec_ref[3:4, :] = vec_ref[3:4, :] * (RG_C * _sigmoid(-lamv))
            exchange.finish()

    rev = lambda w: pl.BlockSpec((tm, w), lambda i: (nt - 1 - i, 0))
    halo = pl.BlockSpec((8, D), lambda i: (jnp.maximum((nt - 1 - i) * tb - 1, 0), 0))
    vec = _const_spec((1, D))
    any_spec = pl.BlockSpec(memory_space=pl.ANY)
    outs = pl.pallas_call(
        body, name="mixer_bwd_seq", grid=(nt,),
        in_specs=[rev(4 * D), rev(D), rev(2 * D), halo, rev(D), halo] + [rev(D)] * 7
        + [vec, _const_spec(w_in_all.shape), _const_spec((4, D)), vec, _const_spec(wax.shape)] + [any_spec] * n,
        out_specs=[rev(6 * D), rev(D), pl.BlockSpec((16, D), lambda i: (0, 0)),
                   pl.BlockSpec((NQ, QW, 2 * QW), lambda i: (0, 0, 0))] + [any_spec] * n,
        out_shape=[jax.ShapeDtypeStruct((s, 6 * D), BF), jax.ShapeDtypeStruct((s, D), F32),
                   jax.ShapeDtypeStruct((16, D), F32), jax.ShapeDtypeStruct((NQ, QW, 2 * QW), F32)]
        + [jax.ShapeDtypeStruct(p.shape, p.dtype) for p in parts],
        scratch_shapes=[pltpu.VMEM((tm, D), F32), pltpu.VMEM((tm, D), F32),
                        pltpu.VMEM((tm + 8, D), F32), pltpu.VMEM((8, D), F32), pltpu.VMEM((9, RC, D), F32)]
        + _gather_sems(n),
        compiler_params=_params(("arbitrary",)),
    )(dprojb, dya, proj, proj, h, h, x, dx1, *gates, gmix, w_in_all, cw, lam, wax, *parts)
    return outs[:4], outs[4:]


def _device_of(d):
    return (d // 4, lax.rem(d // 2, 2), lax.rem(d, 2))


_DW_PLAN = (("C", 1), ("C", 2), ("A", 1), ("C", 3), ("A", 2), ("S", 0), ("A", 3), ("O", 0))
_DW_FLIPS = tuple({"C": 2 * j + 1, "A": 2 * j, "S": 1, "O": 0}[kind] for kind, j in _DW_PLAN)


def _dw_exchange_call(name, order, a, b, a_spec, b_spec, k1, n1, s, small):
    ts = min(TS_DW, s)
    ns = s // ts
    nstep = len(_DW_PLAN)

    def slab(i, order_ref):
        return order_ref[i]

    def body(order_ref, a_ref, b_ref, g_ref, own_ref, recv_ref, gsum_ref, acc, sbuf, stage, send_sems, recv_sems,
             st_send, st_recv, local_sem, rbuf, gacc, send1, recv1, send2, recv2):
        i = pl.program_id(0)
        j = pl.program_id(1)
        me = _me()
        mi = _lin(me)
        sibling = _flip(me, 1)
        allreduce = _SmallAllReduce(g_ref, gacc, rbuf, send1, recv1, send2, recv2)
        pl.when((i == 0) & (j == 0))(allreduce.scatter)
        pl.when((i == nstep // 2) & (j == 0))(allreduce.reduce)
        p = _dot_tn(a_ref[...], b_ref[...])

        @pl.when(j == 0)
        def _():
            acc[...] = p

        @pl.when(j > 0)
        def _():
            acc[...] += p

        def send(step):
            kind, jj = _DW_PLAN[step]
            src = sbuf.at[step % 2]
            if kind == "C":
                return pltpu.make_async_remote_copy(
                    src_ref=src, dst_ref=stage.at[jj - 1], send_sem=st_send.at[jj - 1], recv_sem=st_recv.at[jj - 1],
                    device_id=sibling, device_id_type=MESH)
            to = sibling if kind == "S" else _flip(me, 2 * jj)
            return pltpu.make_async_remote_copy(
                src_ref=src, dst_ref=recv_ref.at[mi], send_sem=send_sems.at[jj], recv_sem=recv_sems.at[jj],
                device_id=to, device_id_type=MESH)

        def arrival(jj):
            frm = sibling if jj == 0 else _flip(me, 2 * jj)
            return pltpu.make_async_remote_copy(
                src_ref=sbuf.at[0], dst_ref=recv_ref.at[_lin(frm)], send_sem=send_sems.at[jj],
                recv_sem=recv_sems.at[jj], device_id=frm, device_id_type=MESH)

        def staged(jj):
            return pltpu.make_async_remote_copy(
                src_ref=sbuf.at[0], dst_ref=stage.at[jj - 1], send_sem=st_send.at[jj - 1], recv_sem=st_recv.at[jj - 1],
                device_id=sibling, device_id_type=MESH)

        for step, (kind, jj) in enumerate(_DW_PLAN):
            @pl.when((i == step) & (j == ns - 1))
            def _(step=step, kind=kind, jj=jj):
                if step >= 2:
                    send(step - 2).wait_send()
                if kind == "A":
                    staged(jj).wait_recv()
                    sbuf[step % 2] = (acc[...] + stage[jj - 1].astype(F32)).astype(BF)
                else:
                    sbuf[step % 2] = acc[...].astype(BF)
                if kind != "O":
                    send(step).start()
                else:
                    own_ref[...] = acc[...]
                    mine = pltpu.make_async_copy(sbuf.at[step % 2], recv_ref.at[mi], local_sem)
                    mine.start()
                    send(step - 1).wait_send()
                    for q in range(4):
                        arrival(q).wait_recv()
                    mine.wait()
                    allreduce.finish()
                    gsum_ref[...] = gacc[...]

    vmem = pl.BlockSpec(memory_space=pltpu.VMEM)
    grid_spec = pltpu.PrefetchScalarGridSpec(
        num_scalar_prefetch=1, grid=(nstep, ns),
        in_specs=[a_spec(ts, slab), b_spec(ts, slab), vmem],
        out_specs=[pl.BlockSpec((k1, n1), lambda i, j, o: (0, 0)), pl.BlockSpec(memory_space=pl.ANY), vmem],
        scratch_shapes=[pltpu.VMEM((k1, n1), F32), pltpu.VMEM((2, k1, n1), BF), pltpu.VMEM((3, k1, n1), BF),
                        pltpu.SemaphoreType.DMA((4,)), pltpu.SemaphoreType.DMA((4,)),
                        pltpu.SemaphoreType.DMA((3,)), pltpu.SemaphoreType.DMA((3,)),
                        pltpu.SemaphoreType.DMA(()), pltpu.VMEM((NDEV, SMALL_PER, D), F32),
                        pltpu.VMEM((SMALL_ROWS, D), F32)]
        + [pltpu.SemaphoreType.DMA((NDEV - 1,))] * 4)
    return pl.pallas_call(
        body, name=name, grid_spec=grid_spec,
        out_shape=[jax.ShapeDtypeStruct((k1, n1), F32), jax.ShapeDtypeStruct((NDEV, k1, n1), BF),
                   jax.ShapeDtypeStruct((SMALL_ROWS, D), F32)],
        compiler_params=_params(("arbitrary", "arbitrary")),
    )(order, a, b, small)


def _dw_plain_call(name, me, a, b, a_spec, b_spec, split, k1, n1, s):
    nb = NDEV // split
    r = k1 // split
    ts = min(TS_DW, s)
    ns = s // ts

    def slab(i, me_ref):
        return i

    def body(me_ref, a_ref, b_ref, own_ref, part_ref, acc):
        i = pl.program_id(0)
        j = pl.program_id(1)
        p = _dot_tn(a_ref[:, 0:k1], b_ref[...])

        @pl.when(j == 0)
        def _():
            acc[...] = p

        @pl.when(j > 0)
        def _():
            acc[...] += p

        @pl.when(j == ns - 1)
        def _():
            part_ref[...] = acc[...].astype(BF)

            @pl.when(i == me_ref[0] // split)
            def _():
                off = pl.multiple_of(lax.rem(me_ref[0], split) * r, RC)
                own_ref[...] = acc[pl.ds(off, r), :]

    grid_spec = pltpu.PrefetchScalarGridSpec(
        num_scalar_prefetch=1, grid=(nb, ns),
        in_specs=[a_spec(ts, slab), b_spec(ts, slab)],
        out_specs=[pl.BlockSpec((r, n1), lambda i, j, me_ref: (0, 0)),
                   pl.BlockSpec((None, k1, n1), lambda i, j, me_ref: (i, 0, 0))],
        scratch_shapes=[pltpu.VMEM((k1, n1), F32)])
    own, part = pl.pallas_call(
        body, name=name, grid_spec=grid_spec,
        out_shape=[jax.ShapeDtypeStruct((r, n1), F32), jax.ShapeDtypeStruct((nb, k1, n1), BF)],
        compiler_params=_params(("arbitrary", "arbitrary")),
    )(me, a, b)
    return own, part.reshape(NDEV, r, n1)


def _rows2d(w):
    return lambda ts, slab: pl.BlockSpec((ts, w), lambda i, j, me_ref: (j, 0))


def _cols2d(w):
    return lambda ts, slab: pl.BlockSpec((ts, w), lambda i, j, me_ref: (j, slab(i, me_ref)))


def _blk3d(w):
    return lambda ts, slab: pl.BlockSpec((None, ts, w), lambda i, j, me_ref: (slab(i, me_ref), j, 0))


_BC1 = 1.0 - ADAM_B1 ** ADAM_STEP
_BC2 = 1.0 - ADAM_B2 ** ADAM_STEP


def _adamw_math(w, g, m, v):
    m = ADAM_B1 * m + (1.0 - ADAM_B1) * g
    v = ADAM_B2 * v + (1.0 - ADAM_B2) * (g * g)
    m_hat = m / _BC1
    v_hat = v / _BC2
    delta = -ADAM_LR * (m_hat / (jnp.sqrt(v_hat) + ADAM_EPS) + ADAM_WD * w)
    return delta, m, v


def _row_tile(r):
    for t in (256, 176, 128, 64, 32, 16, 8):
        if r % t == 0:
            return t
    return r


def _reduce_adamw_call(name, sel, own, recv, w, m, v):
    r, c = own.shape
    tr = _row_tile(r)

    def body(sel_ref, own_ref, recv_ref, w_ref, m_ref, v_ref, g_ref, d_ref, nm_ref, nv_ref):
        g = jnp.zeros((tr, c), F32)
        for sdev in range(NDEV):
            part = jnp.where(sel_ref[sdev] == 1, recv_ref[sdev].astype(F32), 0.0)
            g = g + jnp.where(sel_ref[sdev] == 2, own_ref[...], part)
        g_ref[...] = g
        d_ref[...], nm_ref[...], nv_ref[...] = _adamw_math(w_ref[...], g, m_ref[...], v_ref[...])

    tile = pl.BlockSpec((tr, c), lambda i, me_ref: (i, 0))
    grid_spec = pltpu.PrefetchScalarGridSpec(
        num_scalar_prefetch=1, grid=(r // tr,),
        in_specs=[tile, pl.BlockSpec((NDEV, tr, c), lambda i, me_ref: (0, i, 0)), tile, tile, tile],
        out_specs=[tile] * 4)
    return pl.pallas_call(
        body, name=name, grid_spec=grid_spec,
        out_shape=[jax.ShapeDtypeStruct((r, c), F32)] * 4,
        compiler_params=_params(("parallel",)),
    )(sel, own, recv, w, m, v)


def _adamw_call(name, w, g, m, v):
    r, c = w.shape
    tr = _row_tile(r)

    def body(w_ref, g_ref, m_ref, v_ref, d_ref, nm_ref, nv_ref):
        d_ref[...], nm_ref[...], nv_ref[...] = _adamw_math(w_ref[...], g_ref[...], m_ref[...], v_ref[...])

    tile = pl.BlockSpec((tr, c), lambda i: (i, 0))
    return pl.pallas_call(
        body, name=name, grid=(r // tr,), in_specs=[tile] * 4, out_specs=[tile] * 3,
        out_shape=[jax.ShapeDtypeStruct((r, c), F32)] * 3,
        compiler_params=_params(("parallel",)),
    )(w, g, m, v)


def _me():
    return lax.axis_index("x"), lax.axis_index("y"), lax.axis_index("c")


def _flip(pos, r):
    x, y, c = pos
    return (1 - x if r & 4 else x, 1 - y if r & 2 else y, 1 - c if r & 1 else c)


def _lin(pos):
    return pos[0] * 4 + pos[1] * 2 + pos[2]


class _SmallAllReduce:
    def __init__(self, g_ref, out_ref, rbuf, send1, recv1, send2, recv2):
        self.g, self.out, self.rbuf = g_ref, out_ref, rbuf
        self.sems = (send1, recv1, send2, recv2)
        self.me = _me()
        self.mi = _lin(self.me)

    @staticmethod
    def _rows(d):
        return pl.ds(pl.multiple_of(d * SMALL_PER, 8), SMALL_PER)

    def _scatter(self, r, outgoing):
        peer = _flip(self.me, r)
        src, dst = (_lin(peer), self.mi) if outgoing else (self.mi, _lin(peer))
        return pltpu.make_async_remote_copy(
            src_ref=self.g.at[self._rows(src)], dst_ref=self.rbuf.at[dst],
            send_sem=self.sems[0].at[r - 1], recv_sem=self.sems[1].at[r - 1], device_id=peer, device_id_type=MESH)

    def _spread(self, r, outgoing):
        peer = _flip(self.me, r)
        rows = self._rows(self.mi if outgoing else _lin(peer))
        return pltpu.make_async_remote_copy(
            src_ref=self.out.at[rows], dst_ref=self.out.at[rows],
            send_sem=self.sems[2].at[r - 1], recv_sem=self.sems[3].at[r - 1], device_id=peer, device_id_type=MESH)

    def scatter(self):
        for r in range(1, NDEV):
            self._scatter(r, True).start()
        self.rbuf[self.mi] = self.g[self._rows(self.mi), :]

    def reduce(self):
        for r in range(1, NDEV):
            self._scatter(r, False).wait_recv()
        for r in range(1, NDEV):
            self._scatter(r, True).wait_send()
        tot = self.rbuf[0]
        for d in range(1, NDEV):
            tot = tot + self.rbuf[d]
        self.out[self._rows(self.mi), :] = tot
        for r in range(1, NDEV):
            self._spread(r, True).start()

    def finish(self):
        for r in range(1, NDEV):
            self._spread(r, False).wait_recv()
        for r in range(1, NDEV):
            self._spread(r, True).wait_send()


def _head_blocks(w):
    z = jnp.zeros((64, 64), w.dtype)
    groups = []
    for q in range(NQ):
        rows = [jnp.concatenate([w[4 * q + a] if a == b else z for b in range(4)], axis=1) for a in range(4)]
        groups.append(jnp.concatenate(rows, axis=0))
    return jnp.stack(groups)


def _head_unblocks(g):
    return jnp.stack([g[q, 64 * a:64 * a + 64, 64 * a:64 * a + 64] for q in range(NQ) for a in range(4)])


def _local_step(x, tgt, p, me, order, order_dw):
    s = x.shape[0]
    vec = lambda a: a.reshape(1, D)
    gmix, gffn, gfin = vec(p["norm_mix_g"]), vec(p["norm_ffn_g"]), vec(p["norm_final_g"])
    cb, ba, bx, lam = vec(p["conv_b"]), vec(p["rg_ba"]), vec(p["rg_bx"]), vec(p["rg_lambda"])
    lng, lnb = vec(p["sgu_ln_g"]), vec(p["sgu_ln_b"])
    wax = jnp.concatenate([_head_blocks(p["rg_wa"]), _head_blocks(p["rg_wx"])], axis=2).astype(BF)
    tril = jnp.tril(jnp.ones((CHUNK, CHUNK), bool))
    ws = jnp.where(tril[None], p["sgu_ws"], 0.0)
    wtr = ws.astype(BF)
    wtrt = jnp.swapaxes(ws, 1, 2).astype(BF)
    bias = jnp.repeat(p["sgu_bs"].T, GW, axis=1)
    shard = {k: p[k].astype(BF) for k in _BIG}

    proj, h1, w_in, (wpa, wpb, wo, cw) = _proj_gather_call(
        x, gmix, shard["w_in"], order, [shard["w_proj_a"], shard["w_proj_b"], shard["w_out"], p["conv_w"]])
    wpa, wpb, wo = (t.reshape(D, D) for t in (wpa, wpb, wo))
    cw = jnp.swapaxes(cw, 0, 1).reshape(4, D)
    (h, pa, pb, x1, *gates, ya, yb, mg), (wgu, wdn) = _mixer_fwd_call(
        proj, x, cw, cb, ba, bx, lam, lng, lnb, wax, wtr, bias, wpa, wpb, wo, [shard["w_gate_up"], shard["w_down"]])
    wdn = wdn.reshape(NDEV // 2, -1, D)
    nb, _, wb = w_in.shape
    fb = wgu.shape[2]
    nh = wdn.shape[0]
    dx1, dx1b, h2, act, dgu, dx2b, facc = _ffn_call(x1, tgt, gffn, gfin, wgu, wdn)
    assert nb == NDEV and 2 * nh == NDEV
    own_gu, part_gu = _dw_plain_call("dw_gate_up", me, dgu, h2, _blk3d(dgu.shape[2]), _rows2d(D), 1, fb, D, s)
    own_dn, part_dn = _dw_plain_call("dw_down", me, act, dx2b, _blk3d(act.shape[2]), _rows2d(D), 2, fb, D, s)
    (dprojb, dya, dpa, dpb, bvec, dws, dbs), (recv_dn,) = _mixer_bwd_pre_call(
        dx1b, proj, pa, pb, lng, lnb, wtr, wtrt, bias, wpa, wpb, wo, [part_dn])
    own_pa, part_pa = _dw_plain_call("dw_proj_a", me, ya, dpa, _rows2d(D), _rows2d(D), NDEV, D, D, s)
    own_pb, part_pb = _dw_plain_call("dw_proj_b", me, yb, dpb, _rows2d(D), _rows2d(D), NDEV, D, D, s)
    own_wo, part_wo = _dw_plain_call("dw_out", me, mg, dx1b, _rows2d(D), _rows2d(D), NDEV, D, D, s)
    (dproj, dx, svec, dwax), (recv_gu, recv_pa, recv_pb, recv_wo) = _mixer_bwd_seq_call(
        dprojb, dya, proj, h, x, dx1, gates, gmix, w_in, cw, lam, wax, [part_gu, part_pa, part_pb, part_wo])
    small = {
        "norm_mix_g": svec[8], "norm_ffn_g": facc[0], "norm_final_g": facc[1],
        "conv_b": svec[0], "rg_ba": svec[1], "rg_bx": svec[2], "rg_lambda": svec[3],
        "sgu_ln_g": bvec[0], "sgu_ln_b": bvec[1],
        "rg_wa": _head_unblocks(dwax[:, :, 0:QW]), "rg_wx": _head_unblocks(dwax[:, :, QW:2 * QW]),
        "sgu_ws": dws, "sgu_bs": dbs[:, ::GW].T,
    }
    packed = _pack_small(small, svec[4:8], facc[2])
    own_in, recv_in, gsum = _dw_exchange_call("dw_in", order_dw, h1, dproj, _rows2d(D), _cols2d(wb), D, wb, s, packed)
    dw = {
        "w_gate_up": (own_gu, recv_gu), "w_down": (own_dn, recv_dn), "w_proj_a": (own_pa, recv_pa),
        "w_proj_b": (own_pb, recv_pb), "w_out": (own_wo, recv_wo), "w_in": (own_in, recv_in),
    }
    return gsum, dx, dw


_BIG = ("w_in", "w_gate_up", "w_down", "w_proj_a", "w_proj_b", "w_out")
_VEC_ROWS = ("norm_mix_g", "norm_ffn_g", "norm_final_g", "conv_b", "rg_ba", "rg_bx", "rg_lambda",
             "sgu_ln_g", "sgu_ln_b", "sgu_bs")
_WEIGHTS = ("norm_mix_g", "w_in", "conv_w", "conv_b", "rg_wa", "rg_ba", "rg_wx", "rg_bx", "rg_lambda",
            "sgu_ln_g", "sgu_ln_b", "sgu_ws", "sgu_bs", "w_proj_a", "w_proj_b", "w_out", "norm_ffn_g",
            "w_gate_up", "w_down", "norm_final_g")


def _pack_small(t, conv_w, extra=None):
    extra = jnp.zeros((1, D), F32) if extra is None else extra.reshape(1, D)
    head = jnp.concatenate([t[k].reshape(1, D) for k in _VEC_ROWS] + [conv_w, extra, jnp.zeros((1, D), F32)], axis=0)
    return jnp.concatenate([head, t["rg_wa"].reshape(64, D), t["rg_wx"].reshape(64, D), t["sgu_ws"].reshape(128, D),
                            jnp.zeros((SMALL_ROWS - 272, D), F32)], axis=0)


def _unpack_small(a):
    out = {k: a[j] for j, k in enumerate(_VEC_ROWS)}
    out["conv_w"] = a[10:14]
    out["rg_wa"] = a[16:80].reshape(16, 64, 64)
    out["rg_wx"] = a[80:144].reshape(16, 64, 64)
    out["sgu_ws"] = a[144:272].reshape(NGRP, CHUNK, CHUNK)
    return out


def kernel(x, norm_mix_g, w_in, conv_w, conv_b, rg_wa, rg_ba, rg_wx, rg_bx, rg_lambda, sgu_ln_g, sgu_ln_b, sgu_ws, sgu_bs, w_proj_a, w_proj_b, w_out, norm_ffn_g, w_gate_up, w_down, norm_final_g, loss_target, m_norm_mix_g, m_w_in, m_conv_w, m_conv_b, m_rg_wa, m_rg_ba, m_rg_wx, m_rg_bx, m_rg_lambda, m_sgu_ln_g, m_sgu_ln_b, m_sgu_ws, m_sgu_bs, m_w_proj_a, m_w_proj_b, m_w_out, m_norm_ffn_g, m_w_gate_up, m_w_down, m_norm_final_g, v_norm_mix_g, v_w_in, v_conv_w, v_conv_b, v_rg_wa, v_rg_ba, v_rg_wx, v_rg_bx, v_rg_lambda, v_sgu_ln_g, v_sgu_ln_b, v_sgu_ws, v_sgu_bs, v_w_proj_a, v_w_proj_b, v_w_out, v_norm_ffn_g, v_w_gate_up, v_w_down, v_norm_final_g):
    args = dict(locals())
    w = {k: args[k] for k in _WEIGHTS}
    m = {k: args["m_" + k] for k in _WEIGHTS}
    v = {k: args["v_" + k] for k in _WEIGHTS}
    for d in (w, m, v):
        for k in _WEIGHTS:
            if k != "norm_final_g":
                d[k] = d[k][0]
    me = _lin(_me())
    me1 = me.reshape(1).astype(jnp.int32)

    order = jnp.bitwise_xor(me, jnp.array(_PASS_FLIPS, jnp.int32)).astype(jnp.int32)
    order_dw = jnp.bitwise_xor(me, jnp.array(_DW_FLIPS, jnp.int32)).astype(jnp.int32)

    gsum, dx, dw = _local_step(x[0], loss_target[0], w, me1, order, order_dw)

    peer = jnp.bitwise_xor(jnp.arange(NDEV, dtype=jnp.int32), me)
    sel_direct = jnp.where(peer == 0, 2, 1).astype(jnp.int32)
    sel_two_level = jnp.where(peer == 0, 2, jnp.where((peer == 1) | (peer % 2 == 0), 1, 0)).astype(jnp.int32)
    grads, delta, new_m, new_v = {}, {}, {}, {}
    for k in _BIG:
        own, recv = dw[k]
        flip = own.shape != w[k].shape
        wmv = [jnp.swapaxes(t, 0, 1) if flip else t for t in (w[k], m[k], v[k])]
        res = _reduce_adamw_call("adamw_" + k, sel_two_level if k == "w_in" else sel_direct, own, recv, *wmv)
        grads[k], delta[k], new_m[k], new_v[k] = (jnp.swapaxes(t, 0, 1) if flip else t for t in res)

    loss = (0.5 / D) * jnp.sum(gsum[14])
    zc = jnp.zeros((4, D), F32)
    d_s, m_s, v_s = _adamw_call("adamw_small", _pack_small(w, zc), gsum, _pack_small(m, zc), _pack_small(v, zc))
    gs, ds, ms, vs = _unpack_small(gsum), _unpack_small(d_s), _unpack_small(m_s), _unpack_small(v_s)
    g_cw = lax.dynamic_slice(gs["conv_w"], (0, me * 128), (4, 128))
    ds["conv_w"], ms["conv_w"], vs["conv_w"] = _adamw_call("adamw_conv_w", w["conv_w"], g_cw, m["conv_w"], v["conv_w"])
    gs["conv_w"] = g_cw
    for k in _WEIGHTS:
        if k not in _BIG:
            shp = w[k].shape
            grads[k], delta[k], new_m[k], new_v[k] = (t[k].reshape(shp) for t in (gs, ds, ms, vs))

    def lift(t, k):
        return t[k] if k == "norm_final_g" else t[k][None]

    outs = [loss, dx[None]]
    for t in (grads, delta, new_m, new_v):
        outs += [lift(t, k) for k in _WEIGHTS]
    return tuple(outs)
```

```python
import functools

import jax
import jax.numpy as jnp
from jax import lax
from jax.experimental import pallas as pl
from jax.experimental.pallas import tpu as pltpu

F32 = jnp.float32
BF = jnp.bfloat16

D = 1024
NDEV = 8
EPS = 1e-6
RG_C = 8.0
CHUNK = 128
NGRP = 8
GW = 128
NQ = 4
QW = 256
RC = 16
SMALL_ROWS = 320
SMALL_PER = SMALL_ROWS // NDEV

ADAM_LR = 0.001
ADAM_B1 = 0.9
ADAM_B2 = 0.999
ADAM_EPS = 1e-08
ADAM_WD = 0.01
ADAM_STEP = 10

VMEM_LIMIT = 60 * 1024 * 1024

MESH = pl.DeviceIdType.MESH


def _rows(n, fn, unroll=2, rc=RC):
    def body(i, c):
        fn(pl.multiple_of(i * rc, rc))
        return c
    lax.fori_loop(0, n // rc, body, 0, unroll=unroll)


def _fold(v):
    return jnp.sum(v.reshape(v.shape[0] // RC, RC, v.shape[1]), axis=0)


def _dot(a, b):
    return jnp.dot(a, b, preferred_element_type=F32)


def _dot_nt(a, b):
    return lax.dot_general(a, b, (((1,), (1,)), ((), ())), preferred_element_type=F32)


def _dot_tn(a, b):
    return lax.dot_general(a, b, (((0,), (0,)), ((), ())), preferred_element_type=F32)


_GC = 0.7978845608028654
_GK = 0.044715


def _gelu(x):
    t = jnp.tanh(_GC * (x + _GK * (x * x * x)))
    return x * (0.5 * (1.0 + t))


def _gelu_grad(x):
    x2 = x * x
    t = jnp.tanh(_GC * (x + _GK * (x2 * x)))
    cdf = 0.5 * (1.0 + t)
    dg = cdf + (0.5 * x) * (1.0 - t * t) * (_GC * (1.0 + (3.0 * _GK) * x2))
    return x * cdf, dg


def _sigmoid(x):
    return jax.nn.sigmoid(x)


def _log1p(e):
    u = 1.0 + e
    d = u - 1.0
    return jnp.where(d == 0.0, e, jnp.log(u) * (e / jnp.where(d == 0.0, 1.0, d)))


def _softplus(z):
    return jnp.maximum(z, 0.0) + _log1p(jnp.exp(-jnp.abs(z)))


def _neg_expm1(z):
    u = jnp.exp(z)
    lu = jnp.log(u)
    k = (1.0 - u) * (z / jnp.where(lu == 0.0, 1.0, lu))
    small = jnp.where(lu == 0.0, -z, k)
    return jnp.where(z > -0.5, small, 1.0 - u)


def _shift_back(prev8, cur, j):
    cat = jnp.concatenate([prev8, cur], axis=0)
    return pltpu.roll(cat, j, 0)[8:8 + cur.shape[0]]


def _shift_fwd(cur, next8, j):
    cat = jnp.concatenate([cur, next8], axis=0)
    n = cat.shape[0]
    return pltpu.roll(cat, n - j, 0)[0:cur.shape[0]]


def _const_spec(shape):
    nd = len(shape)
    return pl.BlockSpec(shape, lambda *_: (0,) * nd, pipeline_mode=pl.Buffered(1))


def _params(sem):
    return pltpu.CompilerParams(dimension_semantics=sem, vmem_limit_bytes=VMEM_LIMIT)


TM_PROJ = 1024
TM_MIX = 256
TM_FFN = 256
TM_DX = 512
TS_DW = 4096


_CHIPS = (4, 2, 6)
_PASS_FLIPS = (0, 1, 4, 2, 6, 5, 3, 7)


class _Gather:
    def __init__(self, ins, outs, send_sems, recv_sems, local_sems):
        self.ins, self.outs = ins, outs
        self.send_sems, self.recv_sems, self.local_sems = send_sems, recv_sems, local_sems
        self.me = _me()
        self.sibling = _flip(self.me, 1)

    def _copy(self, a, kind, block, to, src=None):
        dst = self.outs[a].at[_lin(block)]
        return pltpu.make_async_remote_copy(
            src_ref=dst if src is None else src, dst_ref=dst,
            send_sem=self.send_sems.at[a, kind], recv_sem=self.recv_sems.at[a, kind],
            device_id=to, device_id_type=MESH)

    def _local(self, a):
        return pltpu.make_async_copy(self.ins[a], self.outs[a].at[_lin(self.me)], self.local_sems.at[a])

    def start(self):
        for a in range(len(self.ins)):
            self._local(a).start()
            self._copy(a, 0, self.me, self.sibling, src=self.ins[a]).start()
            for j, f in enumerate(_CHIPS):
                self._copy(a, 1 + j, self.me, _flip(self.me, f), src=self.ins[a]).start()

    def forward(self):
        for j, f in enumerate(_CHIPS):
            for a in range(len(self.ins)):
                self._copy(a, 1 + j, _flip(self.me, f), self.me).wait_recv()
                self._copy(a, 4 + j, _flip(self.me, f), self.sibling).start()

    def finish(self):
        for a in range(len(self.ins)):
            self._copy(a, 0, self.sibling, self.me).wait_recv()
            for j, f in enumerate(_CHIPS):
                self._copy(a, 4 + j, _flip(self.me, f | 1), self.me).wait_recv()
            self._copy(a, 0, self.me, self.sibling, src=self.ins[a]).wait_send()
            for j, f in enumerate(_CHIPS):
                self._copy(a, 1 + j, self.me, _flip(self.me, f), src=self.ins[a]).wait_send()
                self._copy(a, 4 + j, _flip(self.me, f), self.sibling).wait_send()
            self._local(a).wait()


class _Exchange:
    def __init__(self, ins, outs, send_sems, recv_sems, local_sems):
        self.ins, self.outs = ins, outs
        self.send_sems, self.recv_sems, self.local_sems = send_sems, recv_sems, local_sems
        self.me = _me()

    def _copy(self, a, r, outgoing):
        peer = _flip(self.me, r)
        src, dst = (peer, self.me) if outgoing else (self.me, peer)
        return pltpu.make_async_remote_copy(
            src_ref=self.ins[a].at[_lin(src)], dst_ref=self.outs[a].at[_lin(dst)],
            send_sem=self.send_sems.at[a, r - 1], recv_sem=self.recv_sems.at[a, r - 1],
            device_id=peer, device_id_type=MESH)

    def _local(self, a):
        mi = _lin(self.me)
        return pltpu.make_async_copy(self.ins[a].at[mi], self.outs[a].at[mi], self.local_sems.at[a])

    def start(self):
        for a in range(len(self.ins)):
            self._local(a).start()
        for r in range(1, NDEV):
            for a in range(len(self.ins)):
                self._copy(a, r, True).start()

    def finish(self):
        for r in range(1, NDEV):
            for a in range(len(self.ins)):
                self._copy(a, r, False).wait_recv()
        for r in range(1, NDEV):
            for a in range(len(self.ins)):
                self._copy(a, r, True).wait_send()
        for a in range(len(self.ins)):
            self._local(a).wait()


def _gather_sems(n):
    return [pltpu.SemaphoreType.DMA((n, 7)), pltpu.SemaphoreType.DMA((n, 7)), pltpu.SemaphoreType.DMA((n,))]


def _proj_gather_call(x, gmix, w_shard, order, extras):
    s = x.shape[0]
    tm = min(TM_PROJ, s)
    nt = s // tm
    wb = w_shard.shape[1]
    n = len(extras)

    def body(order_ref, x_ref, g_ref, wsh_ref, *rest):
        ex_in = rest[:n]
        proj_hbm, h1_hbm, wall_hbm = rest[n:n + 3]
        ex_out = rest[n + 3:2 * n + 3]
        wall, hc, pbuf, ws_send, ws_recv, ex_send, ex_recv, ex_local, out_sems, psems = rest[2 * n + 3:]
        k = pl.program_id(0)
        i = pl.program_id(1)
        me = _me()
        sibling = _flip(me, 1)
        gather = _Gather(ex_in, ex_out, ex_send, ex_recv, ex_local)

        def wcopy(kind, block, to):
            ref = wall.at[_lin(block)]
            return pltpu.make_async_remote_copy(
                src_ref=ref, dst_ref=ref, send_sem=ws_send.at[kind], recv_sem=ws_recv.at[kind],
                device_id=to, device_id_type=MESH)

        head = i == 0

        @pl.when(head & (k == 0))
        def _():
            wall[_lin(me)] = wsh_ref[...]
            wcopy(0, me, sibling).start()
            for j, f in enumerate(_CHIPS):
                wcopy(1 + j, me, _flip(me, f)).start()
            gather.start()

        out_w = pltpu.make_async_copy(wall, wall_hbm, out_sems.at[0])
        out_h = pltpu.make_async_copy(hc, h1_hbm, out_sems.at[1])

        @pl.when(head & (k == 1))
        def _():
            wcopy(0, sibling, me).wait_recv()
            out_h.start()

        for j, f in enumerate(_CHIPS):
            @pl.when(head & (k == 2 + j))
            def _(j=j, f=f):
                wcopy(1 + j, _flip(me, f), me).wait_recv()
                wcopy(4 + j, _flip(me, f), sibling).start()

            @pl.when(head & (k == 5 + j))
            def _(j=j, f=f):
                wcopy(4 + j, _flip(me, f | 1), me).wait_recv()
                if 5 + j == NDEV - 1:
                    out_w.start()

        pl.when(head & (k == 5))(gather.forward)

        base = pl.multiple_of(i * tm, tm)

        @pl.when(k == 0)
        def _():
            g = g_ref[...]

            def norm(r0):
                xx = x_ref[pl.ds(r0, RC), :]
                r = lax.rsqrt(jnp.mean(xx * xx, axis=-1, keepdims=True) + EPS)
                hc[pl.ds(base + r0, RC), :] = ((xx * r) * g).astype(BF)
            _rows(tm, norm, unroll=4)

        blk = order_ref[k]
        wk = wall.at[blk]
        step = k * nt + i
        slot = lax.rem(step, 2)
        subs = [(r0, min(TM_MIX, tm - r0)) for r0 in range(0, tm, TM_MIX)]

        def out_copy(q, r0, rs):
            return pltpu.make_async_copy(
                pbuf.at[slot, pl.ds(r0, rs)],
                proj_hbm.at[pl.ds(base + r0, rs), pl.ds(pl.multiple_of(blk * wb, 128), wb)], psems.at[slot, q])

        @pl.when(step >= 2)
        def _():
            for q, (r0, rs) in enumerate(subs):
                out_copy(q, r0, rs).wait()

        for q, (r0, rs) in enumerate(subs):
            pbuf[slot, r0:r0 + rs, :] = _dot(hc[pl.ds(base + r0, rs), :], wk[...])
            out_copy(q, r0, rs).start()

        @pl.when((k == NDEV - 1) & (i == nt - 1))
        def _():
            wcopy(0, me, sibling).wait_send()
            for j, f in enumerate(_CHIPS):
                wcopy(1 + j, me, _flip(me, f)).wait_send()
                wcopy(4 + j, _flip(me, f), sibling).wait_send()
            gather.finish()
            out_w.wait()
            out_h.wait()
            for q, (r0, rs) in enumerate(subs):
                out_copy(q, r0, rs).wait()
                pltpu.make_async_copy(pbuf.at[1 - slot, pl.ds(r0, rs)], proj_hbm.at[pl.ds(r0, rs), pl.ds(0, wb)],
                                      psems.at[1 - slot, q]).wait()

    any_spec = pl.BlockSpec(memory_space=pl.ANY)
    grid_spec = pltpu.PrefetchScalarGridSpec(
        num_scalar_prefetch=1, grid=(NDEV, nt),
        in_specs=[pl.BlockSpec((tm, D), lambda k, i, o: (jnp.where(k == 0, i, nt - 1), 0)),
                  pl.BlockSpec((1, D), lambda k, i, o: (0, 0)),
                  pl.BlockSpec(w_shard.shape, lambda k, i, o: (0, 0))] + [any_spec] * n,
        out_specs=[any_spec, any_spec, any_spec] + [any_spec] * n,
        scratch_shapes=[pltpu.VMEM((NDEV,) + w_shard.shape, BF), pltpu.VMEM((s, D), BF), pltpu.VMEM((2, tm, wb), F32),
                        pltpu.SemaphoreType.DMA((7,)), pltpu.SemaphoreType.DMA((7,))] + _gather_sems(n)
        + [pltpu.SemaphoreType.DMA((2,)), pltpu.SemaphoreType.DMA((2, -(-tm // TM_MIX)))])
    outs = pl.pallas_call(
        body, name="proj_gather", grid_spec=grid_spec,
        out_shape=[jax.ShapeDtypeStruct((s, NDEV * wb), F32), jax.ShapeDtypeStruct((s, D), BF),
                   jax.ShapeDtypeStruct((NDEV,) + w_shard.shape, BF)]
        + [jax.ShapeDtypeStruct((NDEV,) + e.shape, e.dtype) for e in extras],
        compiler_params=_params(("arbitrary", "arbitrary")),
    )(order, x, gmix, w_shard, *extras)
    return outs[0], outs[1], outs[2], outs[3:]


def _conv_tile(rx, prev8, cw_ref, cb):
    xc = cb + cw_ref[3:4, :] * rx
    for j in (1, 2, 3):
        xc = xc + cw_ref[3 - j:4 - j, :] * _shift_back(prev8, rx, j)
    return xc


def _gate_tile(xcb, wax_ref, ba, bx, sp, q):
    cs = slice(q * QW, (q + 1) * QW)
    z = _dot(xcb[:, cs], wax_ref[q])
    r = _sigmoid(z[:, 0:QW] + ba[:, cs])
    ig = _sigmoid(z[:, QW:2 * QW] + bx[:, cs])
    return r, ig, (-RG_C * r) * sp[:, cs]


def _scan_fwd(a_s, b_s, out_ref, h0, n):
    rowi = lax.broadcasted_iota(jnp.int32, (8, D), 0)

    def block(t, h):
        rows = pl.ds(t * 8, 8)
        a = a_s[rows, :]
        b = b_s[rows, :]
        for d in (1, 2, 4):
            m = rowi >= d
            b = jnp.where(m, a * pltpu.roll(b, d, 0) + b, b)
            a = jnp.where(m, a * pltpu.roll(a, d, 0), a)
        hb = b + a * h
        out_ref[rows, :] = hb
        return hb[7:8, :]
    h = h0
    for t in range(n // 8):
        h = block(t, h)
    return h


def _scan_bwd(a_s, lm_s, c0, n):
    rowi = lax.broadcasted_iota(jnp.int32, (8, D), 0)
    nblk = n // 8

    def block(k, cin):
        rows = pl.ds((nblk - 1 - k) * 8, 8)
        a = a_s[rows, :]
        dh = lm_s[rows, :]
        b = a * dh
        for d in (1, 2, 4):
            m = rowi < 8 - d
            b = jnp.where(m, a * pltpu.roll(b, 8 - d, 0) + b, b)
            a = jnp.where(m, a * pltpu.roll(a, 8 - d, 0), a)
        mu = b + a * cin
        lm_s[rows, :] = dh + jnp.where(rowi < 7, pltpu.roll(mu, 7, 0), cin)
        return mu[0:1, :]
    c = c0
    for k in range(nblk):
        c = block(k, c)
    return c


def _mixer_fwd_call(proj, x, cw, cb, ba, bx, lam, lng, lnb, wax, wtr, bias, wpa, wpb, wo, shards):
    s = x.shape[0]
    tm = min(TM_MIX, s)
    nt = s // tm
    pw = proj.shape[1]
    n = len(shards)

    def body(proj_ref, x_ref, cw_ref, cb_ref, ba_ref, bx_ref, lam_ref, lng_ref, lnb_ref, wax_ref, wtr_ref,
             bias_ref, wpa_ref, wpb_ref, wo_ref, *rest):
        sh_in = rest[:n]
        h_ref, pa_ref, pb_ref, x1_ref, xc_ref, r_ref, ig_ref, a_s, m_ref, ya_ref, yb_ref, mg_ref = rest[n:n + 12]
        sh_out = rest[n + 12:2 * n + 12]
        prev_s, b_s, hc_s, vn_s, mx_s, g_send, g_recv, g_local = rest[2 * n + 12:]
        i = pl.program_id(0)
        gather = _Gather(sh_in, sh_out, g_send, g_recv, g_local)

        @pl.when(i == 0)
        def _():
            gather.start()
            prev_s[...] = jnp.zeros((8, D), F32)
            hc_s[...] = jnp.zeros((8, D), F32)

        pl.when(i == nt // 2)(gather.forward)

        rx = proj_ref[:, 0:D]
        xc = _conv_tile(rx, prev_s[...], cw_ref, cb_ref[...])
        prev_s[...] = rx[tm - 8:tm, :]
        xc_ref[...] = xc.astype(BF)
        xcb = xc.astype(BF)
        sp = _softplus(-lam_ref[...])
        ba = ba_ref[...]
        bx = bx_ref[...]
        for q in range(NQ):
            cs = slice(q * QW, (q + 1) * QW)
            r, ig, la = _gate_tile(xcb, wax_ref, ba, bx, sp, q)
            mq = jnp.sqrt(_neg_expm1(2.0 * la))
            r_ref[:, cs] = r.astype(BF)
            ig_ref[:, cs] = ig.astype(BF)
            m_ref[:, cs] = mq.astype(BF)
            a_s[:, cs] = jnp.exp(la)
            b_s[:, cs] = mq * (ig * xc[:, cs])

        gv = _gelu(proj_ref[:, 3 * D:4 * D])
        dv = gv - jnp.mean(gv, axis=-1, keepdims=True)
        var = jnp.mean(dv * dv, axis=-1, keepdims=True)
        vn_s[...] = ((dv * lax.rsqrt(var + EPS)) * lng_ref[...] + lnb_ref[...]).astype(BF)
        nc = tm // CHUNK
        for c in range(nc):
            rs = slice(c * CHUNK, (c + 1) * CHUNK)
            for g in range(NGRP):
                cs = slice(g * GW, (g + 1) * GW)
                mx_s[rs, cs] = _dot(wtr_ref[g], vn_s[rs, cs])
        mixed = mx_s[...] + jnp.concatenate([bias_ref[...]] * nc, axis=0)
        yb = (_gelu(proj_ref[:, 2 * D:3 * D]) * mixed).astype(BF)
        yb_ref[...] = yb
        pb = _dot(yb, wpb_ref[...])
        pb_ref[...] = pb.astype(BF)
        mx_s[...] = pb

        hc_s[0:1, :] = _scan_fwd(a_s, b_s, h_ref, hc_s[0:1, :], tm)

        ya = (_gelu(proj_ref[:, D:2 * D]) * h_ref[...]).astype(BF)
        ya_ref[...] = ya
        pa = _dot(ya, wpa_ref[...])
        pa_ref[...] = pa.astype(BF)
        mg = (_sigmoid(proj_ref[:, 4 * D:5 * D]) * pa + _sigmoid(proj_ref[:, 5 * D:6 * D]) * mx_s[...]).astype(BF)
        mg_ref[...] = mg
        x1_ref[...] = x_ref[...] + _dot(mg, wo_ref[...])

        pl.when(i == nt - 1)(gather.finish)

    tile = lambda w: pl.BlockSpec((tm, w), lambda i: (i, 0))
    vec = _const_spec((1, D))
    any_spec = pl.BlockSpec(memory_space=pl.ANY)
    outs = pl.pallas_call(
        body, name="mixer_fwd", grid=(nt,),
        in_specs=[tile(pw), tile(D), _const_spec((4, D)), vec, vec, vec, vec, vec, vec,
                  _const_spec(wax.shape), _const_spec(wtr.shape), _const_spec(bias.shape),
                  _const_spec((D, D)), _const_spec((D, D)), _const_spec((D, D))] + [any_spec] * n,
        out_specs=[tile(D)] * 12 + [any_spec] * n,
        out_shape=[jax.ShapeDtypeStruct((s, D), dt) for dt in (F32, BF, BF, F32, BF, BF, BF, F32, BF, BF, BF, BF)]
        + [jax.ShapeDtypeStruct((NDEV,) + e.shape, e.dtype) for e in shards],
        scratch_shapes=[pltpu.VMEM((8, D), F32), pltpu.VMEM((tm, D), F32),
                        pltpu.VMEM((8, D), F32), pltpu.VMEM((tm, D), BF), pltpu.VMEM((tm, D), F32)]
        + _gather_sems(n),
        compiler_params=_params(("arbitrary",)),
    )(proj, x, cw, cb, ba, bx, lam, lng, lnb, wax, wtr, bias, wpa, wpb, wo, *shards)
    return outs[:12], outs[12:]


def _ffn_call(x1, tgt, gffn, gfin, wgu, wdn):
    s = x1.shape[0]
    tm = min(TM_FFN, s)
    nt = s // tm
    nh = wdn.shape[0]
    fb = wgu.shape[2]

    def body(x1_ref, tgt_ref, gffn_ref, gfin_ref, wgu_ref, wdn_ref,
             dx1_ref, dx1b_ref, h2_ref, act_ref, dgu_ref, dx2b_ref, acc_ref,
             g_s, u_s, dx2_s, accs):
        i = pl.program_id(0)

        @pl.when(i == 0)
        def _():
            accs[...] = jnp.zeros(accs.shape, F32)

        gffn = gffn_ref[...]
        gfin = gfin_ref[...]

        x1 = x1_ref[...]
        r2 = lax.rsqrt(jnp.mean(x1 * x1, axis=-1, keepdims=True) + EPS)
        xh2 = x1 * r2
        h2 = (xh2 * gffn).astype(BF)
        h2_ref[...] = h2

        for k in range(nh):
            g = _dot(h2, wgu_ref[k])
            u = _dot(h2, wgu_ref[k + nh])
            g_s[k] = g
            u_s[k] = u
            act_ref[k] = ((g * _sigmoid(g)) * u).astype(BF)

        x2 = x1
        for k in range(nh):
            x2 = x2 + _dot(act_ref[k], wdn_ref[k])

        r3 = lax.rsqrt(jnp.mean(x2 * x2, axis=-1, keepdims=True) + EPS)
        xh = x2 * r3
        err = xh * gfin - tgt_ref[...]
        accs[2] += _fold(err * err)
        dy = err * (1.0 / D)
        accs[1] += _fold(dy * xh)
        dxh = dy * gfin
        dx2 = r3 * (dxh - xh * jnp.mean(dxh * xh, axis=-1, keepdims=True))
        dx2_s[...] = dx2
        dx2b = dx2.astype(BF)
        dx2b_ref[...] = dx2b

        for k in range(nh):
            da = _dot_nt(dx2b, wdn_ref[k])
            g = g_s[k]
            sg = _sigmoid(g)
            dgu_ref[k] = ((da * u_s[k]) * (sg * (1.0 + g * (1.0 - sg)))).astype(BF)
            dgu_ref[k + nh] = (da * (g * sg)).astype(BF)

        dh2 = _dot_nt(dgu_ref[0], wgu_ref[0])
        for k in range(1, 2 * nh):
            dh2 = dh2 + _dot_nt(dgu_ref[k], wgu_ref[k])

        accs[0] += _fold(dh2 * xh2)
        dxh = dh2 * gffn
        dx1 = dx2_s[...] + r2 * (dxh - xh2 * jnp.mean(dxh * xh2, axis=-1, keepdims=True))
        dx1_ref[...] = dx1
        dx1b_ref[...] = dx1.astype(BF)

        @pl.when(i == nt - 1)
        def _():
            acc_ref[...] = jnp.zeros((8, D), F32)
            for j in range(3):
                acc_ref[j:j + 1, :] = jnp.sum(accs[j], axis=0, keepdims=True)

    tile = lambda w: pl.BlockSpec((tm, w), lambda i: (i, 0))
    vec = _const_spec((1, D))
    return pl.pallas_call(
        body, name="ffn", grid=(nt,),
        in_specs=[tile(D), tile(D), vec, vec, _const_spec(wgu.shape), _const_spec(wdn.shape)],
        out_specs=[tile(D), tile(D), tile(D),
                   pl.BlockSpec((nh, tm, fb), lambda i: (0, i, 0)),
                   pl.BlockSpec((2 * nh, tm, fb), lambda i: (0, i, 0)),
                   tile(D), pl.BlockSpec((8, D), lambda i: (0, 0))],
        out_shape=[jax.ShapeDtypeStruct((s, D), F32), jax.ShapeDtypeStruct((s, D), BF),
                   jax.ShapeDtypeStruct((s, D), BF), jax.ShapeDtypeStruct((nh, s, fb), BF),
                   jax.ShapeDtypeStruct((2 * nh, s, fb), BF), jax.ShapeDtypeStruct((s, D), BF),
                   jax.ShapeDtypeStruct((8, D), F32)],
        scratch_shapes=[pltpu.VMEM((nh, tm, fb), F32), pltpu.VMEM((nh, tm, fb), F32), pltpu.VMEM((tm, D), F32),
                        pltpu.VMEM((3, RC, D), F32)],
        compiler_params=_params(("arbitrary",)),
    )(x1, tgt, gffn, gfin, wgu, wdn)


def _mixer_bwd_pre_call(dx1b, proj, pa, pb, lng, lnb, wtr, wtrt, bias, wpa, wpb, wo, parts):
    s = dx1b.shape[0]
    tm = min(TM_MIX, s)
    nt = s // tm
    pw = proj.shape[1]
    n = len(parts)

    def body(dx1b_ref, uv_ref, gg_ref, pa_ref, pb_ref, lng_ref, lnb_ref, wtr_ref, wtrt_ref, bias_ref,
             wpa_ref, wpb_ref, wo_ref, *rest):
        ex_in = rest[:n]
        dproj_ref, dya_ref, dpa_ref, dpb_ref, vec_ref, dws_ref, dbs_ref = rest[n:n + 7]
        ex_out = rest[n + 7:2 * n + 7]
        vn_s, mx_s, dmx_s, dvn_s, accs, dbs_s, e_send, e_recv, e_local = rest[2 * n + 7:]
        i = pl.program_id(0)
        exchange = _Exchange(ex_in, ex_out, e_send, e_recv, e_local)

        @pl.when(i == 0)
        def _():
            exchange.start()
            accs[...] = jnp.zeros(accs.shape, F32)
            dbs_s[...] = jnp.zeros(dbs_s.shape, F32)
            dws_ref[...] = jnp.zeros(dws_ref.shape, F32)

        dm = _dot_nt(dx1b_ref[...], wo_ref[...])
        sa = _sigmoid(gg_ref[:, 0:D])
        sb = _sigmoid(gg_ref[:, D:2 * D])
        dpa = dm * sa
        dpb = dm * sb
        dpab = dpa.astype(BF)
        dpbb = dpb.astype(BF)
        dpa_ref[...] = dpab
        dpb_ref[...] = dpbb
        dproj_ref[:, 2 * D:3 * D] = ((dpa * pa_ref[...].astype(F32)) * (1.0 - sa)).astype(BF)
        dproj_ref[:, 3 * D:4 * D] = ((dpb * pb_ref[...].astype(F32)) * (1.0 - sb)).astype(BF)

        dya_ref[...] = _dot_nt(dpab, wpa_ref[...])
        dyb = _dot_nt(dpbb, wpb_ref[...])

        lng = lng_ref[...]
        gv, dgelu_v = _gelu_grad(uv_ref[:, D:2 * D])
        dv = gv - jnp.mean(gv, axis=-1, keepdims=True)
        rstd = lax.rsqrt(jnp.mean(dv * dv, axis=-1, keepdims=True) + EPS)
        xh = dv * rstd
        vn_s[...] = (xh * lng + lnb_ref[...]).astype(BF)

        nc = tm // CHUNK
        for c in range(nc):
            rs = slice(c * CHUNK, (c + 1) * CHUNK)
            for g in range(NGRP):
                cs = slice(g * GW, (g + 1) * GW)
                mx_s[rs, cs] = _dot(wtr_ref[g], vn_s[rs, cs])

        gu, dgelu_u = _gelu_grad(uv_ref[:, 0:D])
        mixed = mx_s[...] + jnp.concatenate([bias_ref[...]] * nc, axis=0)
        dproj_ref[:, 0:D] = ((dyb * mixed) * dgelu_u).astype(BF)
        dmx = dyb * gu
        dmx_s[...] = dmx.astype(BF)
        dbs_s[...] += jnp.sum(dmx.reshape(nc, CHUNK, D), axis=0)

        for c in range(nc):
            rs = slice(c * CHUNK, (c + 1) * CHUNK)
            for g in range(NGRP):
                cs = slice(g * GW, (g + 1) * GW)
                dvn_s[rs, cs] = _dot(wtrt_ref[g], dmx_s[rs, cs])
                dws_ref[g] += _dot_nt(dmx_s[rs, cs], vn_s[rs, cs])

        dvn = dvn_s[...]
        accs[0] += _fold(dvn * xh)
        accs[1] += _fold(dvn)
        dxh = dvn * lng
        m1 = jnp.mean(dxh, axis=-1, keepdims=True)
        m2 = jnp.mean(dxh * xh, axis=-1, keepdims=True)
        dproj_ref[:, D:2 * D] = ((rstd * (dxh - m1 - xh * m2)) * dgelu_v).astype(BF)

        @pl.when(i == nt - 1)
        def _():
            vec_ref[...] = jnp.zeros((8, D), F32)
            for j in range(2):
                vec_ref[j:j + 1, :] = jnp.sum(accs[j], axis=0, keepdims=True)
            row = lax.broadcasted_iota(jnp.int32, (CHUNK, CHUNK), 0)
            col = lax.broadcasted_iota(jnp.int32, (CHUNK, CHUNK), 1)
            for g in range(NGRP):
                dws_ref[g] = jnp.where(row >= col, dws_ref[g], 0.0)
                gs = jnp.sum(dbs_s[:, g * GW:(g + 1) * GW], axis=1, keepdims=True)
                dbs_ref[:, g * GW:(g + 1) * GW] = jnp.broadcast_to(gs, (CHUNK, GW))
            exchange.finish()

    tile = lambda w: pl.BlockSpec((tm, w), lambda i: (i, 0))
    vec = _const_spec((1, D))
    any_spec = pl.BlockSpec(memory_space=pl.ANY)
    outs = pl.pallas_call(
        body, name="mixer_bwd_pre", grid=(nt,),
        in_specs=[tile(D), pl.BlockSpec((tm, 2 * D), lambda i: (i, 1)), pl.BlockSpec((tm, 2 * D), lambda i: (i, 2)),
                  tile(D), tile(D), vec, vec, _const_spec(wtr.shape), _const_spec(wtrt.shape),
                  _const_spec(bias.shape), _const_spec((D, D)), _const_spec((D, D)), _const_spec((D, D))]
        + [any_spec] * n,
        out_specs=[tile(4 * D), tile(D), tile(D), tile(D), pl.BlockSpec((8, D), lambda i: (0, 0)),
                   pl.BlockSpec((NGRP, CHUNK, CHUNK), lambda i: (0, 0, 0)),
                   pl.BlockSpec((CHUNK, D), lambda i: (0, 0))] + [any_spec] * n,
        out_shape=[jax.ShapeDtypeStruct((s, 4 * D), BF), jax.ShapeDtypeStruct((s, D), F32),
                   jax.ShapeDtypeStruct((s, D), BF), jax.ShapeDtypeStruct((s, D), BF),
                   jax.ShapeDtypeStruct((8, D), F32), jax.ShapeDtypeStruct((NGRP, CHUNK, CHUNK), F32),
                   jax.ShapeDtypeStruct((CHUNK, D), F32)]
        + [jax.ShapeDtypeStruct(p.shape, p.dtype) for p in parts],
        scratch_shapes=[pltpu.VMEM((tm, D), BF), pltpu.VMEM((tm, D), F32), pltpu.VMEM((tm, D), BF),
                        pltpu.VMEM((tm, D), F32), pltpu.VMEM((2, RC, D), F32), pltpu.VMEM((CHUNK, D), F32)]
        + _gather_sems(n),
        compiler_params=_params(("arbitrary",)),
    )(dx1b, proj, proj, pa, pb, lng, lnb, wtr, wtrt, bias, wpa, wpb, wo, *parts)
    return outs[:7], outs[7:]


def _mixer_bwd_seq_call(dprojb, dya, proj, h, x, dx1, gates, gmix, w_in_all, cw, lam, wax, parts):
    s = dya.shape[0]
    tm = min(TM_MIX, s)
    nt = s // tm
    tb = tm // 8
    n = len(parts)
    nb, _, wb = w_in_all.shape

    def body(dpb_ref, dya_ref, xg_ref, xh8_ref, h_ref, hh8_ref, x_ref, dx1_ref, xc_s, r_s, ig_s, a_s, m_s,
             gmix_ref, win_ref, cw_ref, lam_ref, wax_ref, *rest):
        ex_in = rest[:n]
        dpa_ref, dx_ref, vec_ref, dwax_ref = rest[n:n + 4]
        ex_out = rest[n + 4:2 * n + 4]
        lm_s, dh_s, dxc_s, c_s, accs, e_send, e_recv, e_local = rest[2 * n + 4:]
        i = pl.program_id(0)
        exchange = _Exchange(ex_in, ex_out, e_send, e_recv, e_local)

        @pl.when(i == 0)
        def _():
            exchange.start()
            accs[...] = jnp.zeros(accs.shape, F32)
            dwax_ref[...] = jnp.zeros(dwax_ref.shape, F32)
            c_s[...] = jnp.zeros((8, D), F32)
            dxc_s[tm:tm + 8, :] = jnp.zeros((8, D), F32)

        first_tile = i == nt - 1
        prev8 = jnp.where(first_tile, 0.0, xh8_ref[...])
        hprev8 = jnp.where(first_tile, 0.0, hh8_ref[...])
        lamv = lam_ref[...]
        sp = _softplus(-lamv)
        hv = h_ref[...]
        g, dg = _gelu_grad(xg_ref[:, D:2 * D])
        dya = dya_ref[...]
        lm_s[...] = dya * g
        drg = ((dya * hv) * dg).astype(BF)
        dpa_ref[:, D:2 * D] = drg
        dpa_ref[:, 2 * D:6 * D] = dpb_ref[...]

        def dpb_block(k):
            return _dot_nt(dpb_ref[:, k * wb - 2 * D:(k + 1) * wb - 2 * D], win_ref[k])
        dh = (_dot_nt(drg[:, 0:2 * wb - D], win_ref[1, :, D - wb:wb])
              + _dot_nt(drg[:, 2 * wb - D:D], win_ref[2, :, 0:2 * D - 2 * wb])
              + _dot_nt(dpb_ref[:, 0:3 * wb - 2 * D], win_ref[2, :, 2 * D - 2 * wb:wb]))
        for k in range(3, 6):
            dh = dh + dpb_block(k)
        dh_s[...] = dh

        c_s[0:1, :] = _scan_bwd(a_s, lm_s, c_s[0:1, :], tm)

        hprev = _shift_back(hprev8, h_ref[...], 1)
        for q in range(NQ):
            cs = slice(q * QW, (q + 1) * QW)
            r = r_s[:, cs].astype(F32)
            ig = ig_s[:, cs].astype(F32)
            a = a_s[:, cs]
            m = m_s[:, cs].astype(F32)
            lm = lm_s[:, cs]
            xq = xc_s[:, cs].astype(F32)
            dixc = lm * m
            dla = (lm * hprev[:, cs]) * a - ((lm * (ig * xq)) * (a * a)) / m
            accs[3, :, cs] += _fold(dla * r)
            dza = (dla * (-RG_C * sp[:, cs])) * (r * (1.0 - r))
            dzx = (dixc * xq) * (ig * (1.0 - ig))
            accs[1, :, cs] += _fold(dza)
            accs[2, :, cs] += _fold(dzx)
            dz = jnp.concatenate([dza, dzx], axis=1).astype(BF)
            dxc_s[0:tm, cs] = dixc * ig + _dot_nt(dz, wax_ref[q])
            dwax_ref[q] += _dot_tn(xc_s[:, cs], dz)

        cur = dxc_s[0:tm, :]
        nx = dxc_s[tm:tm + 8, :]
        drx = cw_ref[3:4, :] * cur
        for j in (1, 2, 3):
            drx = drx + cw_ref[3 - j:4 - j, :] * _shift_fwd(cur, nx, j)
        accs[0] += _fold(cur)
        rx = xg_ref[:, 0:D]
        accs[7] += _fold(cur * rx)
        for j in (1, 2, 3):
            accs[7 - j] += _fold(cur * _shift_back(prev8, rx, j))
        dxc_s[tm:tm + 8, :] = cur[0:8, :]

        drxb = drx.astype(BF)
        dpa_ref[:, 0:D] = drxb
        dh = dh_s[...] + _dot_nt(drxb[:, 0:wb], win_ref[0]) + _dot_nt(drxb[:, wb:D], win_ref[1, :, 0:D - wb])
        for k in range(6, nb):
            dh = dh + dpb_block(k)
        xx = x_ref[...]
        rn = lax.rsqrt(jnp.mean(xx * xx, axis=-1, keepdims=True) + EPS)
        xhn = xx * rn
        accs[8] += _fold(dh * xhn)
        dxh = dh * gmix_ref[...]
        dx_ref[...] = dx1_ref[...] + rn * (dxh - xhn * jnp.mean(dxh * xhn, axis=-1, keepdims=True))

        @pl.when(i == nt - 1)
        def _():
            vec_ref[...] = jnp.zeros((16, D), F32)
            for j in range(9):
                vec_ref[j:j + 1, :] = jnp.sum(accs[j], axis=0, keepdims=True)
            vec_ref[3:4, :] = vec_ref[3:4, :] * (RG_C * _sigmoid(-lamv))
            exchange.finish()

    rev = lambda w: pl.BlockSpec((tm, w), lambda i: (nt - 1 - i, 0))
    halo = pl.BlockSpec((8, D), lambda i: (jnp.maximum((nt - 1 - i) * tb - 1, 0), 0))
    vec = _const_spec((1, D))
    any_spec = pl.BlockSpec(memory_space=pl.ANY)
    outs = pl.pallas_call(
        body, name="mixer_bwd_seq", grid=(nt,),
        in_specs=[rev(4 * D), rev(D), rev(2 * D), halo, rev(D), halo] + [rev(D)] * 7
        + [vec, _const_spec(w_in_all.shape), _const_spec((4, D)), vec, _const_spec(wax.shape)] + [any_spec] * n,
        out_specs=[rev(6 * D), rev(D), pl.BlockSpec((16, D), lambda i: (0, 0)),
                   pl.BlockSpec((NQ, QW, 2 * QW), lambda i: (0, 0, 0))] + [any_spec] * n,
        out_shape=[jax.ShapeDtypeStruct((s, 6 * D), BF), jax.ShapeDtypeStruct((s, D), F32),
                   jax.ShapeDtypeStruct((16, D), F32), jax.ShapeDtypeStruct((NQ, QW, 2 * QW), F32)]
        + [jax.ShapeDtypeStruct(p.shape, p.dtype) for p in parts],
        scratch_shapes=[pltpu.VMEM((tm, D), F32), pltpu.VMEM((tm, D), F32),
                        pltpu.VMEM((tm + 8, D), F32), pltpu.VMEM((8, D), F32), pltpu.VMEM((9, RC, D), F32)]
        + _gather_sems(n),
        compiler_params=_params(("arbitrary",)),
    )(dprojb, dya, proj, proj, h, h, x, dx1, *gates, gmix, w_in_all, cw, lam, wax, *parts)
    return outs[:4], outs[4:]


def _device_of(d):
    return (d // 4, lax.rem(d // 2, 2), lax.rem(d, 2))


_DW_PLAN = (("C", 1), ("C", 2), ("A", 1), ("C", 3), ("A", 2), ("S", 0), ("A", 3), ("O", 0))
_DW_FLIPS = tuple({"C": 2 * j + 1, "A": 2 * j, "S": 1, "O": 0}[kind] for kind, j in _DW_PLAN)


def _dw_exchange_call(name, order, a, b, a_spec, b_spec, k1, n1, s, small):
    ts = min(TS_DW, s)
    ns = s // ts
    nstep = len(_DW_PLAN)

    def slab(i, order_ref):
        return order_ref[i]

    def body(order_ref, a_ref, b_ref, g_ref, own_ref, recv_ref, gsum_ref, acc, sbuf, stage, send_sems, recv_sems,
             st_send, st_recv, local_sem, rbuf, gacc, send1, recv1, send2, recv2):
        i = pl.program_id(0)
        j = pl.program_id(1)
        me = _me()
        mi = _lin(me)
        sibling = _flip(me, 1)
        allreduce = _SmallAllReduce(g_ref, gacc, rbuf, send1, recv1, send2, recv2)
        pl.when((i == 0) & (j == 0))(allreduce.scatter)
        pl.when((i == nstep // 2) & (j == 0))(allreduce.reduce)
        p = _dot_tn(a_ref[...], b_ref[...])

        @pl.when(j == 0)
        def _():
            acc[...] = p

        @pl.when(j > 0)
        def _():
            acc[...] += p

        def send(step):
            kind, jj = _DW_PLAN[step]
            src = sbuf.at[step % 2]
            if kind == "C":
                return pltpu.make_async_remote_copy(
                    src_ref=src, dst_ref=stage.at[jj - 1], send_sem=st_send.at[jj - 1], recv_sem=st_recv.at[jj - 1],
                    device_id=sibling, device_id_type=MESH)
            to = sibling if kind == "S" else _flip(me, 2 * jj)
            return pltpu.make_async_remote_copy(
                src_ref=src, dst_ref=recv_ref.at[mi], send_sem=send_sems.at[jj], recv_sem=recv_sems.at[jj],
                device_id=to, device_id_type=MESH)

        def arrival(jj):
            frm = sibling if jj == 0 else _flip(me, 2 * jj)
            return pltpu.make_async_remote_copy(
                src_ref=sbuf.at[0], dst_ref=recv_ref.at[_lin(frm)], send_sem=send_sems.at[jj],
                recv_sem=recv_sems.at[jj], device_id=frm, device_id_type=MESH)

        def staged(jj):
            return pltpu.make_async_remote_copy(
                src_ref=sbuf.at[0], dst_ref=stage.at[jj - 1], send_sem=st_send.at[jj - 1], recv_sem=st_recv.at[jj - 1],
                device_id=sibling, device_id_type=MESH)

        for step, (kind, jj) in enumerate(_DW_PLAN):
            @pl.when((i == step) & (j == ns - 1))
            def _(step=step, kind=kind, jj=jj):
                if step >= 2:
                    send(step - 2).wait_send()
                if kind == "A":
                    staged(jj).wait_recv()
                    sbuf[step % 2] = (acc[...] + stage[jj - 1].astype(F32)).astype(BF)
                else:
                    sbuf[step % 2] = acc[...].astype(BF)
                if kind != "O":
                    send(step).start()
                else:
                    own_ref[...] = acc[...]
                    mine = pltpu.make_async_copy(sbuf.at[step % 2], recv_ref.at[mi], local_sem)
                    mine.start()
                    send(step - 1).wait_send()
                    for q in range(4):
                        arrival(q).wait_recv()
                    mine.wait()
                    allreduce.finish()
                    gsum_ref[...] = gacc[...]

    vmem = pl.BlockSpec(memory_space=pltpu.VMEM)
    grid_spec = pltpu.PrefetchScalarGridSpec(
        num_scalar_prefetch=1, grid=(nstep, ns),
        in_specs=[a_spec(ts, slab), b_spec(ts, slab), vmem],
        out_specs=[pl.BlockSpec((k1, n1), lambda i, j, o: (0, 0)), pl.BlockSpec(memory_space=pl.ANY), vmem],
        scratch_shapes=[pltpu.VMEM((k1, n1), F32), pltpu.VMEM((2, k1, n1), BF), pltpu.VMEM((3, k1, n1), BF),
                        pltpu.SemaphoreType.DMA((4,)), pltpu.SemaphoreType.DMA((4,)),
                        pltpu.SemaphoreType.DMA((3,)), pltpu.SemaphoreType.DMA((3,)),
                        pltpu.SemaphoreType.DMA(()), pltpu.VMEM((NDEV, SMALL_PER, D), F32),
                        pltpu.VMEM((SMALL_ROWS, D), F32)]
        + [pltpu.SemaphoreType.DMA((NDEV - 1,))] * 4)
    return pl.pallas_call(
        body, name=name, grid_spec=grid_spec,
        out_shape=[jax.ShapeDtypeStruct((k1, n1), F32), jax.ShapeDtypeStruct((NDEV, k1, n1), BF),
                   jax.ShapeDtypeStruct((SMALL_ROWS, D), F32)],
        compiler_params=_params(("arbitrary", "arbitrary")),
    )(order, a, b, small)


def _dw_plain_call(name, me, a, b, a_spec, b_spec, split, k1, n1, s):
    nb = NDEV // split
    r = k1 // split
    ts = min(TS_DW, s)
    ns = s // ts

    def slab(i, me_ref):
        return i

    def body(me_ref, a_ref, b_ref, own_ref, part_ref, acc):
        i = pl.program_id(0)
        j = pl.program_id(1)
        p = _dot_tn(a_ref[...], b_ref[...])

        @pl.when(j == 0)
        def _():
            acc[...] = p

        @pl.when(j > 0)
        def _():
            acc[...] += p

        @pl.when(j == ns - 1)
        def _():
            part_ref[...] = acc[...].astype(BF)

            @pl.when(i == me_ref[0] // split)
            def _():
                off = pl.multiple_of(lax.rem(me_ref[0], split) * r, RC)
                own_ref[...] = acc[pl.ds(off, r), :]

    grid_spec = pltpu.PrefetchScalarGridSpec(
        num_scalar_prefetch=1, grid=(nb, ns),
        in_specs=[a_spec(ts, slab), b_spec(ts, slab)],
        out_specs=[pl.BlockSpec((r, n1), lambda i, j, me_ref: (0, 0)),
                   pl.BlockSpec((None, k1, n1), lambda i, j, me_ref: (i, 0, 0))],
        scratch_shapes=[pltpu.VMEM((k1, n1), F32)])
    own, part = pl.pallas_call(
        body, name=name, grid_spec=grid_spec,
        out_shape=[jax.ShapeDtypeStruct((r, n1), F32), jax.ShapeDtypeStruct((nb, k1, n1), BF)],
        compiler_params=_params(("arbitrary", "arbitrary")),
    )(me, a, b)
    return own, part.reshape(NDEV, r, n1)


def _rows2d(w):
    return lambda ts, slab: pl.BlockSpec((ts, w), lambda i, j, me_ref: (j, 0))


def _cols2d(w):
    return lambda ts, slab: pl.BlockSpec((ts, w), lambda i, j, me_ref: (j, slab(i, me_ref)))


def _blk3d(w):
    return lambda ts, slab: pl.BlockSpec((None, ts, w), lambda i, j, me_ref: (slab(i, me_ref), j, 0))


_BC1 = 1.0 - ADAM_B1 ** ADAM_STEP
_BC2 = 1.0 - ADAM_B2 ** ADAM_STEP


def _adamw_math(w, g, m, v):
    m = ADAM_B1 * m + (1.0 - ADAM_B1) * g
    v = ADAM_B2 * v + (1.0 - ADAM_B2) * (g * g)
    m_hat = m / _BC1
    v_hat = v / _BC2
    delta = -ADAM_LR * (m_hat / (jnp.sqrt(v_hat) + ADAM_EPS) + ADAM_WD * w)
    return delta, m, v


def _row_tile(r):
    for t in (256, 176, 128, 64, 32, 16, 8):
        if r % t == 0:
            return t
    return r


def _reduce_adamw_call(name, sel, own, recv, w, m, v):
    r, c = own.shape
    tr = _row_tile(r)

    def body(sel_ref, own_ref, recv_ref, w_ref, m_ref, v_ref, g_ref, d_ref, nm_ref, nv_ref):
        g = jnp.zeros((tr, c), F32)
        for sdev in range(NDEV):
            part = jnp.where(sel_ref[sdev] == 1, recv_ref[sdev].astype(F32), 0.0)
            g = g + jnp.where(sel_ref[sdev] == 2, own_ref[...], part)
        g_ref[...] = g
        d_ref[...], nm_ref[...], nv_ref[...] = _adamw_math(w_ref[...], g, m_ref[...], v_ref[...])

    tile = pl.BlockSpec((tr, c), lambda i, me_ref: (i, 0))
    grid_spec = pltpu.PrefetchScalarGridSpec(
        num_scalar_prefetch=1, grid=(r // tr,),
        in_specs=[tile, pl.BlockSpec((NDEV, tr, c), lambda i, me_ref: (0, i, 0)), tile, tile, tile],
        out_specs=[tile] * 4)
    return pl.pallas_call(
        body, name=name, grid_spec=grid_spec,
        out_shape=[jax.ShapeDtypeStruct((r, c), F32)] * 4,
        compiler_params=_params(("parallel",)),
    )(sel, own, recv, w, m, v)


def _adamw_call(name, w, g, m, v):
    r, c = w.shape
    tr = _row_tile(r)

    def body(w_ref, g_ref, m_ref, v_ref, d_ref, nm_ref, nv_ref):
        d_ref[...], nm_ref[...], nv_ref[...] = _adamw_math(w_ref[...], g_ref[...], m_ref[...], v_ref[...])

    tile = pl.BlockSpec((tr, c), lambda i: (i, 0))
    return pl.pallas_call(
        body, name=name, grid=(r // tr,), in_specs=[tile] * 4, out_specs=[tile] * 3,
        out_shape=[jax.ShapeDtypeStruct((r, c), F32)] * 3,
        compiler_params=_params(("parallel",)),
    )(w, g, m, v)


def _me():
    return lax.axis_index("x"), lax.axis_index("y"), lax.axis_index("c")


def _flip(pos, r):
    x, y, c = pos
    return (1 - x if r & 4 else x, 1 - y if r & 2 else y, 1 - c if r & 1 else c)


def _lin(pos):
    return pos[0] * 4 + pos[1] * 2 + pos[2]


class _SmallAllReduce:
    def __init__(self, g_ref, out_ref, rbuf, send1, recv1, send2, recv2):
        self.g, self.out, self.rbuf = g_ref, out_ref, rbuf
        self.sems = (send1, recv1, send2, recv2)
        self.me = _me()
        self.mi = _lin(self.me)

    @staticmethod
    def _rows(d):
        return pl.ds(pl.multiple_of(d * SMALL_PER, 8), SMALL_PER)

    def _scatter(self, r, outgoing):
        peer = _flip(self.me, r)
        src, dst = (_lin(peer), self.mi) if outgoing else (self.mi, _lin(peer))
        return pltpu.make_async_remote_copy(
            src_ref=self.g.at[self._rows(src)], dst_ref=self.rbuf.at[dst],
            send_sem=self.sems[0].at[r - 1], recv_sem=self.sems[1].at[r - 1], device_id=peer, device_id_type=MESH)

    def _spread(self, r, outgoing):
        peer = _flip(self.me, r)
        rows = self._rows(self.mi if outgoing else _lin(peer))
        return pltpu.make_async_remote_copy(
            src_ref=self.out.at[rows], dst_ref=self.out.at[rows],
            send_sem=self.sems[2].at[r - 1], recv_sem=self.sems[3].at[r - 1], device_id=peer, device_id_type=MESH)

    def scatter(self):
        for r in range(1, NDEV):
            self._scatter(r, True).start()
        self.rbuf[self.mi] = self.g[self._rows(self.mi), :]

    def reduce(self):
        for r in range(1, NDEV):
            self._scatter(r, False).wait_recv()
        for r in range(1, NDEV):
            self._scatter(r, True).wait_send()
        tot = self.rbuf[0]
        for d in range(1, NDEV):
            tot = tot + self.rbuf[d]
        self.out[self._rows(self.mi), :] = tot
        for r in range(1, NDEV):
            self._spread(r, True).start()

    def finish(self):
        for r in range(1, NDEV):
            self._spread(r, False).wait_recv()
        for r in range(1, NDEV):
            self._spread(r, True).wait_send()


def _head_blocks(w):
    z = jnp.zeros((64, 64), w.dtype)
    groups = []
    for q in range(NQ):
        rows = [jnp.concatenate([w[4 * q + a] if a == b else z for b in range(4)], axis=1) for a in range(4)]
        groups.append(jnp.concatenate(rows, axis=0))
    return jnp.stack(groups)


def _head_unblocks(g):
    return jnp.stack([g[q, 64 * a:64 * a + 64, 64 * a:64 * a + 64] for q in range(NQ) for a in range(4)])


def _local_step(x, tgt, p, me, order, order_dw):
    s = x.shape[0]
    vec = lambda a: a.reshape(1, D)
    gmix, gffn, gfin = vec(p["norm_mix_g"]), vec(p["norm_ffn_g"]), vec(p["norm_final_g"])
    cb, ba, bx, lam = vec(p["conv_b"]), vec(p["rg_ba"]), vec(p["rg_bx"]), vec(p["rg_lambda"])
    lng, lnb = vec(p["sgu_ln_g"]), vec(p["sgu_ln_b"])
    wax = jnp.concatenate([_head_blocks(p["rg_wa"]), _head_blocks(p["rg_wx"])], axis=2).astype(BF)
    tril = jnp.tril(jnp.ones((CHUNK, CHUNK), bool))
    ws = jnp.where(tril[None], p["sgu_ws"], 0.0)
    wtr = ws.astype(BF)
    wtrt = jnp.swapaxes(ws, 1, 2).astype(BF)
    bias = jnp.repeat(p["sgu_bs"].T, GW, axis=1)
    shard = {k: p[k].astype(BF) for k in _BIG}

    proj, h1, w_in, (wpa, wpb, wo, cw) = _proj_gather_call(
        x, gmix, shard["w_in"], order, [shard["w_proj_a"], shard["w_proj_b"], shard["w_out"], p["conv_w"]])
    wpa, wpb, wo = (t.reshape(D, D) for t in (wpa, wpb, wo))
    cw = jnp.swapaxes(cw, 0, 1).reshape(4, D)
    (h, pa, pb, x1, *gates, ya, yb, mg), (wgu, wdn) = _mixer_fwd_call(
        proj, x, cw, cb, ba, bx, lam, lng, lnb, wax, wtr, bias, wpa, wpb, wo, [shard["w_gate_up"], shard["w_down"]])
    wdn = wdn.reshape(NDEV // 2, -1, D)
    nb, _, wb = w_in.shape
    fb = wgu.shape[2]
    nh = wdn.shape[0]
    dx1, dx1b, h2, act, dgu, dx2b, facc = _ffn_call(x1, tgt, gffn, gfin, wgu, wdn)
    assert nb == NDEV and 2 * nh == NDEV
    own_gu, part_gu = _dw_plain_call("dw_gate_up", me, dgu, h2, _blk3d(fb), _rows2d(D), 1, fb, D, s)
    own_dn, part_dn = _dw_plain_call("dw_down", me, act, dx2b, _blk3d(fb), _rows2d(D), 2, fb, D, s)
    (dprojb, dya, dpa, dpb, bvec, dws, dbs), (recv_dn,) = _mixer_bwd_pre_call(
        dx1b, proj, pa, pb, lng, lnb, wtr, wtrt, bias, wpa, wpb, wo, [part_dn])
    own_pa, part_pa = _dw_plain_call("dw_proj_a", me, ya, dpa, _rows2d(D), _rows2d(D), NDEV, D, D, s)
    own_pb, part_pb = _dw_plain_call("dw_proj_b", me, yb, dpb, _rows2d(D), _rows2d(D), NDEV, D, D, s)
    own_wo, part_wo = _dw_plain_call("dw_out", me, mg, dx1b, _rows2d(D), _rows2d(D), NDEV, D, D, s)
    (dproj, dx, svec, dwax), (recv_gu, recv_pa, recv_pb, recv_wo) = _mixer_bwd_seq_call(
        dprojb, dya, proj, h, x, dx1, gates, gmix, w_in, cw, lam, wax, [part_gu, part_pa, part_pb, part_wo])
    small = {
        "norm_mix_g": svec[8], "norm_ffn_g": facc[0], "norm_final_g": facc[1],
        "conv_b": svec[0], "rg_ba": svec[1], "rg_bx": svec[2], "rg_lambda": svec[3],
        "sgu_ln_g": bvec[0], "sgu_ln_b": bvec[1],
        "rg_wa": _head_unblocks(dwax[:, :, 0:QW]), "rg_wx": _head_unblocks(dwax[:, :, QW:2 * QW]),
        "sgu_ws": dws, "sgu_bs": dbs[:, ::GW].T,
    }
    packed = _pack_small(small, svec[4:8], facc[2])
    own_in, recv_in, gsum = _dw_exchange_call("dw_in", order_dw, h1, dproj, _rows2d(D), _cols2d(wb), D, wb, s, packed)
    dw = {
        "w_gate_up": (own_gu, recv_gu), "w_down": (own_dn, recv_dn), "w_proj_a": (own_pa, recv_pa),
        "w_proj_b": (own_pb, recv_pb), "w_out": (own_wo, recv_wo), "w_in": (own_in, recv_in),
    }
    return gsum, dx, dw


_BIG = ("w_in", "w_gate_up", "w_down", "w_proj_a", "w_proj_b", "w_out")
_VEC_ROWS = ("norm_mix_g", "norm_ffn_g", "norm_final_g", "conv_b", "rg_ba", "rg_bx", "rg_lambda",
             "sgu_ln_g", "sgu_ln_b", "sgu_bs")
_WEIGHTS = ("norm_mix_g", "w_in", "conv_w", "conv_b", "rg_wa", "rg_ba", "rg_wx", "rg_bx", "rg_lambda",
            "sgu_ln_g", "sgu_ln_b", "sgu_ws", "sgu_bs", "w_proj_a", "w_proj_b", "w_out", "norm_ffn_g",
            "w_gate_up", "w_down", "norm_final_g")


_MAPS = ("rg_wa", "rg_wx", "sgu_ws")


def _pack_rows(t, conv_w, extra=None):
    extra = jnp.zeros((1, D), F32) if extra is None else extra.reshape(1, D)
    return jnp.concatenate([t[k].reshape(1, D) for k in _VEC_ROWS] + [conv_w, extra, jnp.zeros((1, D), F32)], axis=0)


def _pack_small(t, conv_w, extra=None):
    return jnp.concatenate([_pack_rows(t, conv_w, extra), t["rg_wa"].reshape(64, D), t["rg_wx"].reshape(64, D),
                            t["sgu_ws"].reshape(128, D), jnp.zeros((SMALL_ROWS - 272, D), F32)], axis=0)


def _unpack_rows(a):
    return {k: a[j] for j, k in enumerate(_VEC_ROWS)}


def _unpack_small(a):
    out = _unpack_rows(a)
    out["conv_w"] = a[10:14]
    out["rg_wa"] = a[16:80].reshape(16, 64, 64)
    out["rg_wx"] = a[80:144].reshape(16, 64, 64)
    out["sgu_ws"] = a[144:272].reshape(NGRP, CHUNK, CHUNK)
    return out


def kernel(x, norm_mix_g, w_in, conv_w, conv_b, rg_wa, rg_ba, rg_wx, rg_bx, rg_lambda, sgu_ln_g, sgu_ln_b, sgu_ws, sgu_bs, w_proj_a, w_proj_b, w_out, norm_ffn_g, w_gate_up, w_down, norm_final_g, loss_target, m_norm_mix_g, m_w_in, m_conv_w, m_conv_b, m_rg_wa, m_rg_ba, m_rg_wx, m_rg_bx, m_rg_lambda, m_sgu_ln_g, m_sgu_ln_b, m_sgu_ws, m_sgu_bs, m_w_proj_a, m_w_proj_b, m_w_out, m_norm_ffn_g, m_w_gate_up, m_w_down, m_norm_final_g, v_norm_mix_g, v_w_in, v_conv_w, v_conv_b, v_rg_wa, v_rg_ba, v_rg_wx, v_rg_bx, v_rg_lambda, v_sgu_ln_g, v_sgu_ln_b, v_sgu_ws, v_sgu_bs, v_w_proj_a, v_w_proj_b, v_w_out, v_norm_ffn_g, v_w_gate_up, v_w_down, v_norm_final_g):
    args = dict(locals())
    w = {k: args[k] for k in _WEIGHTS}
    m = {k: args["m_" + k] for k in _WEIGHTS}
    v = {k: args["v_" + k] for k in _WEIGHTS}
    for d in (w, m, v):
        for k in _WEIGHTS:
            if k != "norm_final_g":
                d[k] = d[k][0]
    me = _lin(_me())
    me1 = me.reshape(1).astype(jnp.int32)

    order = jnp.bitwise_xor(me, jnp.array(_PASS_FLIPS, jnp.int32)).astype(jnp.int32)
    order_dw = jnp.bitwise_xor(me, jnp.array(_DW_FLIPS, jnp.int32)).astype(jnp.int32)

    gsum, dx, dw = _local_step(x[0], loss_target[0], w, me1, order, order_dw)

    peer = jnp.bitwise_xor(jnp.arange(NDEV, dtype=jnp.int32), me)
    sel_direct = jnp.where(peer == 0, 2, 1).astype(jnp.int32)
    sel_two_level = jnp.where(peer == 0, 2, jnp.where((peer == 1) | (peer % 2 == 0), 1, 0)).astype(jnp.int32)
    grads, delta, new_m, new_v = {}, {}, {}, {}
    for k in _BIG:
        own, recv = dw[k]
        flip = own.shape != w[k].shape
        wmv = [jnp.swapaxes(t, 0, 1) if flip else t for t in (w[k], m[k], v[k])]
        res = _reduce_adamw_call("adamw_" + k, sel_two_level if k == "w_in" else sel_direct, own, recv, *wmv)
        grads[k], delta[k], new_m[k], new_v[k] = (jnp.swapaxes(t, 0, 1) if flip else t for t in res)

    loss = (0.5 / D) * jnp.sum(gsum[14])
    zc = jnp.zeros((4, D), F32)
    d_s, m_s, v_s = _adamw_call("adamw_small", _pack_rows(w, zc), gsum, _pack_rows(m, zc), _pack_rows(v, zc))
    gs, ds, ms, vs = _unpack_small(gsum), _unpack_rows(d_s), _unpack_rows(m_s), _unpack_rows(v_s)
    rows = lambda t: t.reshape(-1, t.shape[-1])
    for k in _MAPS:
        ds[k], ms[k], vs[k] = _adamw_call("adamw_" + k, rows(w[k]), rows(gs[k]), rows(m[k]), rows(v[k]))
    g_cw = lax.dynamic_slice(gs["conv_w"], (0, me * 128), (4, 128))
    ds["conv_w"], ms["conv_w"], vs["conv_w"] = _adamw_call("adamw_conv_w", w["conv_w"], g_cw, m["conv_w"], v["conv_w"])
    gs["conv_w"] = g_cw
    for k in _WEIGHTS:
        if k not in _BIG:
            shp = w[k].shape
            grads[k], delta[k], new_m[k], new_v[k] = (t[k].reshape(shp) for t in (gs, ds, ms, vs))

    def lift(t, k):
        return t[k] if k == "norm_final_g" else t[k][None]

    outs = [loss, dx[None]]
    for t in (grads, delta, new_m, new_v):
        outs += [lift(t, k) for k in _WEIGHTS]
    return tuple(outs)
```

```python
import functools

import jax
import jax.numpy as jnp
from jax import lax
from jax.experimental import pallas as pl
from jax.experimental.pallas import tpu as pltpu

F32 = jnp.float32
BF = jnp.bfloat16

D = 1024
NDEV = 8
EPS = 1e-6
RG_C = 8.0
CHUNK = 128
NGRP = 8
GW = 128
NQ = 4
QW = 256
RC = 16
SMALL_ROWS = 320
SMALL_PER = SMALL_ROWS // NDEV

ADAM_LR = 0.001
ADAM_B1 = 0.9
ADAM_B2 = 0.999
ADAM_EPS = 1e-08
ADAM_WD = 0.01
ADAM_STEP = 10

VMEM_LIMIT = 60 * 1024 * 1024

MESH = pl.DeviceIdType.MESH


def _rows(n, fn, unroll=2, rc=RC):
    def body(i, c):
        fn(pl.multiple_of(i * rc, rc))
        return c
    lax.fori_loop(0, n // rc, body, 0, unroll=unroll)


def _fold(v):
    return jnp.sum(v.reshape(v.shape[0] // RC, RC, v.shape[1]), axis=0)


def _dot(a, b):
    return jnp.dot(a, b, preferred_element_type=F32)


def _dot_nt(a, b):
    return lax.dot_general(a, b, (((1,), (1,)), ((), ())), preferred_element_type=F32)


def _dot_tn(a, b):
    return lax.dot_general(a, b, (((0,), (0,)), ((), ())), preferred_element_type=F32)


_GC = 0.7978845608028654
_GK = 0.044715


def _gelu(x):
    t = jnp.tanh(_GC * (x + _GK * (x * x * x)))
    return x * (0.5 * (1.0 + t))


def _gelu_grad(x):
    x2 = x * x
    t = jnp.tanh(_GC * (x + _GK * (x2 * x)))
    cdf = 0.5 * (1.0 + t)
    dg = cdf + (0.5 * x) * (1.0 - t * t) * (_GC * (1.0 + (3.0 * _GK) * x2))
    return x * cdf, dg


def _sigmoid(x):
    return jax.nn.sigmoid(x)


def _log1p(e):
    u = 1.0 + e
    d = u - 1.0
    return jnp.where(d == 0.0, e, jnp.log(u) * (e / jnp.where(d == 0.0, 1.0, d)))


def _softplus(z):
    return jnp.maximum(z, 0.0) + _log1p(jnp.exp(-jnp.abs(z)))


def _neg_expm1(z):
    u = jnp.exp(z)
    lu = jnp.log(u)
    k = (1.0 - u) * (z / jnp.where(lu == 0.0, 1.0, lu))
    small = jnp.where(lu == 0.0, -z, k)
    return jnp.where(z > -0.5, small, 1.0 - u)


def _shift_back(prev8, cur, j):
    cat = jnp.concatenate([prev8, cur], axis=0)
    return pltpu.roll(cat, j, 0)[8:8 + cur.shape[0]]


def _shift_fwd(cur, next8, j):
    cat = jnp.concatenate([cur, next8], axis=0)
    n = cat.shape[0]
    return pltpu.roll(cat, n - j, 0)[0:cur.shape[0]]


def _const_spec(shape):
    nd = len(shape)
    return pl.BlockSpec(shape, lambda *_: (0,) * nd, pipeline_mode=pl.Buffered(1))


def _params(sem):
    return pltpu.CompilerParams(dimension_semantics=sem, vmem_limit_bytes=VMEM_LIMIT)


TM_PROJ = 1024
TM_MIX = 256
TM_FFN = 256
TM_DX = 512
TS_DW = 4096


_CHIPS = (4, 2, 6)
_PASS_FLIPS = (0, 1, 4, 2, 6, 5, 3, 7)


class _Gather:
    def __init__(self, ins, outs, send_sems, recv_sems, local_sems):
        self.ins, self.outs = ins, outs
        self.send_sems, self.recv_sems, self.local_sems = send_sems, recv_sems, local_sems
        self.me = _me()
        self.sibling = _flip(self.me, 1)

    def _copy(self, a, kind, block, to, src=None):
        dst = self.outs[a].at[_lin(block)]
        return pltpu.make_async_remote_copy(
            src_ref=dst if src is None else src, dst_ref=dst,
            send_sem=self.send_sems.at[a, kind], recv_sem=self.recv_sems.at[a, kind],
            device_id=to, device_id_type=MESH)

    def _local(self, a):
        return pltpu.make_async_copy(self.ins[a], self.outs[a].at[_lin(self.me)], self.local_sems.at[a])

    def start(self):
        for a in range(len(self.ins)):
            self._local(a).start()
            self._copy(a, 0, self.me, self.sibling, src=self.ins[a]).start()
            for j, f in enumerate(_CHIPS):
                self._copy(a, 1 + j, self.me, _flip(self.me, f), src=self.ins[a]).start()

    def forward(self):
        for j, f in enumerate(_CHIPS):
            for a in range(len(self.ins)):
                self._copy(a, 1 + j, _flip(self.me, f), self.me).wait_recv()
                self._copy(a, 4 + j, _flip(self.me, f), self.sibling).start()

    def finish(self):
        for a in range(len(self.ins)):
            self._copy(a, 0, self.sibling, self.me).wait_recv()
            for j, f in enumerate(_CHIPS):
                self._copy(a, 4 + j, _flip(self.me, f | 1), self.me).wait_recv()
            self._copy(a, 0, self.me, self.sibling, src=self.ins[a]).wait_send()
            for j, f in enumerate(_CHIPS):
                self._copy(a, 1 + j, self.me, _flip(self.me, f), src=self.ins[a]).wait_send()
                self._copy(a, 4 + j, _flip(self.me, f), self.sibling).wait_send()
            self._local(a).wait()


class _Exchange:
    def __init__(self, ins, outs, send_sems, recv_sems, local_sems):
        self.ins, self.outs = ins, outs
        self.send_sems, self.recv_sems, self.local_sems = send_sems, recv_sems, local_sems
        self.me = _me()

    def _copy(self, a, r, outgoing):
        peer = _flip(self.me, r)
        src, dst = (peer, self.me) if outgoing else (self.me, peer)
        return pltpu.make_async_remote_copy(
            src_ref=self.ins[a].at[_lin(src)], dst_ref=self.outs[a].at[_lin(dst)],
            send_sem=self.send_sems.at[a, r - 1], recv_sem=self.recv_sems.at[a, r - 1],
            device_id=peer, device_id_type=MESH)

    def _local(self, a):
        mi = _lin(self.me)
        return pltpu.make_async_copy(self.ins[a].at[mi], self.outs[a].at[mi], self.local_sems.at[a])

    def start(self):
        for a in range(len(self.ins)):
            self._local(a).start()
        for r in range(1, NDEV):
            for a in range(len(self.ins)):
                self._copy(a, r, True).start()

    def finish(self):
        for r in range(1, NDEV):
            for a in range(len(self.ins)):
                self._copy(a, r, False).wait_recv()
        for r in range(1, NDEV):
            for a in range(len(self.ins)):
                self._copy(a, r, True).wait_send()
        for a in range(len(self.ins)):
            self._local(a).wait()


def _gather_sems(n):
    return [pltpu.SemaphoreType.DMA((n, 7)), pltpu.SemaphoreType.DMA((n, 7)), pltpu.SemaphoreType.DMA((n,))]


def _proj_gather_call(x, gmix, w_shard, order, extras):
    s = x.shape[0]
    tm = min(TM_PROJ, s)
    nt = s // tm
    wb = w_shard.shape[1]
    n = len(extras)

    def body(order_ref, x_ref, g_ref, wsh_ref, *rest):
        ex_in = rest[:n]
        proj_hbm, h1_hbm, wall_hbm = rest[n:n + 3]
        ex_out = rest[n + 3:2 * n + 3]
        wall, hc, pbuf, ws_send, ws_recv, ex_send, ex_recv, ex_local, out_sems, psems = rest[2 * n + 3:]
        k = pl.program_id(0)
        i = pl.program_id(1)
        me = _me()
        sibling = _flip(me, 1)
        gather = _Gather(ex_in, ex_out, ex_send, ex_recv, ex_local)

        def wcopy(kind, block, to):
            ref = wall.at[_lin(block)]
            return pltpu.make_async_remote_copy(
                src_ref=ref, dst_ref=ref, send_sem=ws_send.at[kind], recv_sem=ws_recv.at[kind],
                device_id=to, device_id_type=MESH)

        head = i == 0

        @pl.when(head & (k == 0))
        def _():
            wall[_lin(me)] = wsh_ref[...]
            wcopy(0, me, sibling).start()
            for j, f in enumerate(_CHIPS):
                wcopy(1 + j, me, _flip(me, f)).start()
            gather.start()

        out_w = pltpu.make_async_copy(wall, wall_hbm, out_sems.at[0])
        out_h = pltpu.make_async_copy(hc, h1_hbm, out_sems.at[1])

        @pl.when(head & (k == 1))
        def _():
            wcopy(0, sibling, me).wait_recv()
            out_h.start()

        for j, f in enumerate(_CHIPS):
            @pl.when(head & (k == 2 + j))
            def _(j=j, f=f):
                wcopy(1 + j, _flip(me, f), me).wait_recv()
                wcopy(4 + j, _flip(me, f), sibling).start()

            @pl.when(head & (k == 5 + j))
            def _(j=j, f=f):
                wcopy(4 + j, _flip(me, f | 1), me).wait_recv()
                if 5 + j == NDEV - 1:
                    out_w.start()

        pl.when(head & (k == 5))(gather.forward)

        base = pl.multiple_of(i * tm, tm)

        @pl.when(k == 0)
        def _():
            g = g_ref[...]

            def norm(r0):
                xx = x_ref[pl.ds(r0, RC), :]
                r = lax.rsqrt(jnp.mean(xx * xx, axis=-1, keepdims=True) + EPS)
                hc[pl.ds(base + r0, RC), :] = ((xx * r) * g).astype(BF)
            _rows(tm, norm, unroll=4)

        blk = order_ref[k]
        wk = wall.at[blk]
        step = k * nt + i
        slot = lax.rem(step, 2)
        subs = [(r0, min(TM_MIX, tm - r0)) for r0 in range(0, tm, TM_MIX)]

        def out_copy(q, r0, rs):
            return pltpu.make_async_copy(
                pbuf.at[slot, pl.ds(r0, rs)],
                proj_hbm.at[pl.ds(base + r0, rs), pl.ds(pl.multiple_of(blk * wb, 128), wb)], psems.at[slot, q])

        @pl.when(step >= 2)
        def _():
            for q, (r0, rs) in enumerate(subs):
                out_copy(q, r0, rs).wait()

        for q, (r0, rs) in enumerate(subs):
            pbuf[slot, r0:r0 + rs, :] = _dot(hc[pl.ds(base + r0, rs), :], wk[...]).astype(BF)
            out_copy(q, r0, rs).start()

        @pl.when((k == NDEV - 1) & (i == nt - 1))
        def _():
            wcopy(0, me, sibling).wait_send()
            for j, f in enumerate(_CHIPS):
                wcopy(1 + j, me, _flip(me, f)).wait_send()
                wcopy(4 + j, _flip(me, f), sibling).wait_send()
            gather.finish()
            out_w.wait()
            out_h.wait()
            for q, (r0, rs) in enumerate(subs):
                out_copy(q, r0, rs).wait()
                pltpu.make_async_copy(pbuf.at[1 - slot, pl.ds(r0, rs)], proj_hbm.at[pl.ds(r0, rs), pl.ds(0, wb)],
                                      psems.at[1 - slot, q]).wait()

    any_spec = pl.BlockSpec(memory_space=pl.ANY)
    grid_spec = pltpu.PrefetchScalarGridSpec(
        num_scalar_prefetch=1, grid=(NDEV, nt),
        in_specs=[pl.BlockSpec((tm, D), lambda k, i, o: (jnp.where(k == 0, i, nt - 1), 0)),
                  pl.BlockSpec((1, D), lambda k, i, o: (0, 0)),
                  pl.BlockSpec(w_shard.shape, lambda k, i, o: (0, 0))] + [any_spec] * n,
        out_specs=[any_spec, any_spec, any_spec] + [any_spec] * n,
        scratch_shapes=[pltpu.VMEM((NDEV,) + w_shard.shape, BF), pltpu.VMEM((s, D), BF), pltpu.VMEM((2, tm, wb), BF),
                        pltpu.SemaphoreType.DMA((7,)), pltpu.SemaphoreType.DMA((7,))] + _gather_sems(n)
        + [pltpu.SemaphoreType.DMA((2,)), pltpu.SemaphoreType.DMA((2, -(-tm // TM_MIX)))])
    outs = pl.pallas_call(
        body, name="proj_gather", grid_spec=grid_spec,
        out_shape=[jax.ShapeDtypeStruct((s, NDEV * wb), BF), jax.ShapeDtypeStruct((s, D), BF),
                   jax.ShapeDtypeStruct((NDEV,) + w_shard.shape, BF)]
        + [jax.ShapeDtypeStruct((NDEV,) + e.shape, e.dtype) for e in extras],
        compiler_params=_params(("arbitrary", "arbitrary")),
    )(order, x, gmix, w_shard, *extras)
    return outs[0], outs[1], outs[2], outs[3:]


def _conv_tile(rx, prev8, cw_ref, cb):
    xc = cb + cw_ref[3:4, :] * rx
    for j in (1, 2, 3):
        xc = xc + cw_ref[3 - j:4 - j, :] * _shift_back(prev8, rx, j)
    return xc


def _gate_tile(xcb, wax_ref, ba, bx, sp, q):
    cs = slice(q * QW, (q + 1) * QW)
    z = _dot(xcb[:, cs], wax_ref[q])
    r = _sigmoid(z[:, 0:QW] + ba[:, cs])
    ig = _sigmoid(z[:, QW:2 * QW] + bx[:, cs])
    return r, ig, (-RG_C * r) * sp[:, cs]


def _scan_fwd(a_s, b_s, out_ref, h0, n):
    rowi = lax.broadcasted_iota(jnp.int32, (8, D), 0)

    def block(t, h):
        rows = pl.ds(t * 8, 8)
        a = a_s[rows, :]
        b = b_s[rows, :]
        for d in (1, 2, 4):
            m = rowi >= d
            b = jnp.where(m, a * pltpu.roll(b, d, 0) + b, b)
            a = jnp.where(m, a * pltpu.roll(a, d, 0), a)
        hb = b + a * h
        out_ref[rows, :] = hb
        return hb[7:8, :]
    h = h0
    for t in range(n // 8):
        h = block(t, h)
    return h


def _scan_bwd(a_s, lm_s, c0, n):
    rowi = lax.broadcasted_iota(jnp.int32, (8, D), 0)
    nblk = n // 8

    def block(k, cin):
        rows = pl.ds((nblk - 1 - k) * 8, 8)
        a = a_s[rows, :]
        dh = lm_s[rows, :]
        b = a * dh
        for d in (1, 2, 4):
            m = rowi < 8 - d
            b = jnp.where(m, a * pltpu.roll(b, 8 - d, 0) + b, b)
            a = jnp.where(m, a * pltpu.roll(a, 8 - d, 0), a)
        mu = b + a * cin
        lm_s[rows, :] = dh + jnp.where(rowi < 7, pltpu.roll(mu, 7, 0), cin)
        return mu[0:1, :]
    c = c0
    for k in range(nblk):
        c = block(k, c)
    return c


def _mixer_fwd_call(proj, x, cw, cb, ba, bx, lam, lng, lnb, wax, wtr, bias, wpa, wpb, wo, shards):
    s = x.shape[0]
    tm = min(TM_MIX, s)
    nt = s // tm
    pw = proj.shape[1]
    n = len(shards)

    def body(proj_ref, x_ref, cw_ref, cb_ref, ba_ref, bx_ref, lam_ref, lng_ref, lnb_ref, wax_ref, wtr_ref,
             bias_ref, wpa_ref, wpb_ref, wo_ref, *rest):
        sh_in = rest[:n]
        h_ref, pa_ref, pb_ref, x1_ref, xc_ref, r_ref, ig_ref, a_s, m_ref, ya_ref, yb_ref, mg_ref = rest[n:n + 12]
        sh_out = rest[n + 12:2 * n + 12]
        prev_s, b_s, hc_s, vn_s, mx_s, g_send, g_recv, g_local = rest[2 * n + 12:]
        i = pl.program_id(0)
        gather = _Gather(sh_in, sh_out, g_send, g_recv, g_local)

        @pl.when(i == 0)
        def _():
            gather.start()
            prev_s[...] = jnp.zeros((8, D), F32)
            hc_s[...] = jnp.zeros((8, D), F32)

        pl.when(i == nt // 2)(gather.forward)

        rx = proj_ref[:, 0:D].astype(F32)
        xc = _conv_tile(rx, prev_s[...], cw_ref, cb_ref[...])
        prev_s[...] = rx[tm - 8:tm, :]
        xc_ref[...] = xc.astype(BF)
        xcb = xc.astype(BF)
        sp = _softplus(-lam_ref[...])
        ba = ba_ref[...]
        bx = bx_ref[...]
        for q in range(NQ):
            cs = slice(q * QW, (q + 1) * QW)
            r, ig, la = _gate_tile(xcb, wax_ref, ba, bx, sp, q)
            mq = jnp.sqrt(_neg_expm1(2.0 * la))
            r_ref[:, cs] = r.astype(BF)
            ig_ref[:, cs] = ig.astype(BF)
            m_ref[:, cs] = mq.astype(BF)
            a_s[:, cs] = jnp.exp(la)
            b_s[:, cs] = mq * (ig * xc[:, cs])

        gv = _gelu(proj_ref[:, 3 * D:4 * D].astype(F32))
        dv = gv - jnp.mean(gv, axis=-1, keepdims=True)
        var = jnp.mean(dv * dv, axis=-1, keepdims=True)
        vn_s[...] = ((dv * lax.rsqrt(var + EPS)) * lng_ref[...] + lnb_ref[...]).astype(BF)
        nc = tm // CHUNK
        for c in range(nc):
            rs = slice(c * CHUNK, (c + 1) * CHUNK)
            for g in range(NGRP):
                cs = slice(g * GW, (g + 1) * GW)
                mx_s[rs, cs] = _dot(wtr_ref[g], vn_s[rs, cs])
        mixed = mx_s[...] + jnp.concatenate([bias_ref[...]] * nc, axis=0)
        yb = (_gelu(proj_ref[:, 2 * D:3 * D].astype(F32)) * mixed).astype(BF)
        yb_ref[...] = yb
        pb = _dot(yb, wpb_ref[...])
        pb_ref[...] = pb.astype(BF)
        mx_s[...] = pb

        hc_s[0:1, :] = _scan_fwd(a_s, b_s, h_ref, hc_s[0:1, :], tm)

        ya = (_gelu(proj_ref[:, D:2 * D].astype(F32)) * h_ref[...]).astype(BF)
        ya_ref[...] = ya
        pa = _dot(ya, wpa_ref[...])
        pa_ref[...] = pa.astype(BF)
        mg = (_sigmoid(proj_ref[:, 4 * D:5 * D].astype(F32)) * pa
              + _sigmoid(proj_ref[:, 5 * D:6 * D].astype(F32)) * mx_s[...]).astype(BF)
        mg_ref[...] = mg
        x1_ref[...] = x_ref[...] + _dot(mg, wo_ref[...])

        pl.when(i == nt - 1)(gather.finish)

    tile = lambda w: pl.BlockSpec((tm, w), lambda i: (i, 0))
    vec = _const_spec((1, D))
    any_spec = pl.BlockSpec(memory_space=pl.ANY)
    outs = pl.pallas_call(
        body, name="mixer_fwd", grid=(nt,),
        in_specs=[tile(pw), tile(D), _const_spec((4, D)), vec, vec, vec, vec, vec, vec,
                  _const_spec(wax.shape), _const_spec(wtr.shape), _const_spec(bias.shape),
                  _const_spec((D, D)), _const_spec((D, D)), _const_spec((D, D))] + [any_spec] * n,
        out_specs=[tile(D)] * 12 + [any_spec] * n,
        out_shape=[jax.ShapeDtypeStruct((s, D), dt) for dt in (F32, BF, BF, F32, BF, BF, BF, F32, BF, BF, BF, BF)]
        + [jax.ShapeDtypeStruct((NDEV,) + e.shape, e.dtype) for e in shards],
        scratch_shapes=[pltpu.VMEM((8, D), F32), pltpu.VMEM((tm, D), F32),
                        pltpu.VMEM((8, D), F32), pltpu.VMEM((tm, D), BF), pltpu.VMEM((tm, D), F32)]
        + _gather_sems(n),
        compiler_params=_params(("arbitrary",)),
    )(proj, x, cw, cb, ba, bx, lam, lng, lnb, wax, wtr, bias, wpa, wpb, wo, *shards)
    return outs[:12], outs[12:]


def _ffn_call(x1, tgt, gffn, gfin, wgu, wdn):
    s = x1.shape[0]
    tm = min(TM_FFN, s)
    nt = s // tm
    nh = wdn.shape[0]
    fb = wgu.shape[2]

    def body(x1_ref, tgt_ref, gffn_ref, gfin_ref, wgu_ref, wdn_ref,
             dx1_ref, dx1b_ref, h2_ref, act_ref, dgu_ref, dx2b_ref, acc_ref,
             g_s, u_s, dx2_s, accs):
        i = pl.program_id(0)

        @pl.when(i == 0)
        def _():
            accs[...] = jnp.zeros(accs.shape, F32)

        gffn = gffn_ref[...]
        gfin = gfin_ref[...]

        x1 = x1_ref[...]
        r2 = lax.rsqrt(jnp.mean(x1 * x1, axis=-1, keepdims=True) + EPS)
        xh2 = x1 * r2
        h2 = (xh2 * gffn).astype(BF)
        h2_ref[...] = h2

        for k in range(nh):
            g = _dot(h2, wgu_ref[k])
            u = _dot(h2, wgu_ref[k + nh])
            g_s[k] = g
            u_s[k] = u
            act_ref[k] = ((g * _sigmoid(g)) * u).astype(BF)

        x2 = x1
        for k in range(nh):
            x2 = x2 + _dot(act_ref[k], wdn_ref[k])

        r3 = lax.rsqrt(jnp.mean(x2 * x2, axis=-1, keepdims=True) + EPS)
        xh = x2 * r3
        err = xh * gfin - tgt_ref[...]
        accs[2] += _fold(err * err)
        dy = err * (1.0 / D)
        accs[1] += _fold(dy * xh)
        dxh = dy * gfin
        dx2 = r3 * (dxh - xh * jnp.mean(dxh * xh, axis=-1, keepdims=True))
        dx2_s[...] = dx2
        dx2b = dx2.astype(BF)
        dx2b_ref[...] = dx2b

        for k in range(nh):
            da = _dot_nt(dx2b, wdn_ref[k])
            g = g_s[k]
            sg = _sigmoid(g)
            dgu_ref[k] = ((da * u_s[k]) * (sg * (1.0 + g * (1.0 - sg)))).astype(BF)
            dgu_ref[k + nh] = (da * (g * sg)).astype(BF)

        dh2 = _dot_nt(dgu_ref[0], wgu_ref[0])
        for k in range(1, 2 * nh):
            dh2 = dh2 + _dot_nt(dgu_ref[k], wgu_ref[k])

        accs[0] += _fold(dh2 * xh2)
        dxh = dh2 * gffn
        dx1 = dx2_s[...] + r2 * (dxh - xh2 * jnp.mean(dxh * xh2, axis=-1, keepdims=True))
        dx1_ref[...] = dx1
        dx1b_ref[...] = dx1.astype(BF)

        @pl.when(i == nt - 1)
        def _():
            acc_ref[...] = jnp.zeros((8, D), F32)
            for j in range(3):
                acc_ref[j:j + 1, :] = jnp.sum(accs[j], axis=0, keepdims=True)

    tile = lambda w: pl.BlockSpec((tm, w), lambda i: (i, 0))
    vec = _const_spec((1, D))
    return pl.pallas_call(
        body, name="ffn", grid=(nt,),
        in_specs=[tile(D), tile(D), vec, vec, _const_spec(wgu.shape), _const_spec(wdn.shape)],
        out_specs=[tile(D), tile(D), tile(D),
                   pl.BlockSpec((nh, tm, fb), lambda i: (0, i, 0)),
                   pl.BlockSpec((2 * nh, tm, fb), lambda i: (0, i, 0)),
                   tile(D), pl.BlockSpec((8, D), lambda i: (0, 0))],
        out_shape=[jax.ShapeDtypeStruct((s, D), F32), jax.ShapeDtypeStruct((s, D), BF),
                   jax.ShapeDtypeStruct((s, D), BF), jax.ShapeDtypeStruct((nh, s, fb), BF),
                   jax.ShapeDtypeStruct((2 * nh, s, fb), BF), jax.ShapeDtypeStruct((s, D), BF),
                   jax.ShapeDtypeStruct((8, D), F32)],
        scratch_shapes=[pltpu.VMEM((nh, tm, fb), F32), pltpu.VMEM((nh, tm, fb), F32), pltpu.VMEM((tm, D), F32),
                        pltpu.VMEM((3, RC, D), F32)],
        compiler_params=_params(("arbitrary",)),
    )(x1, tgt, gffn, gfin, wgu, wdn)


def _mixer_bwd_pre_call(dx1b, proj, pa, pb, lng, lnb, wtr, wtrt, bias, wpa, wpb, wo, parts):
    s = dx1b.shape[0]
    tm = min(TM_MIX, s)
    nt = s // tm
    pw = proj.shape[1]
    n = len(parts)

    def body(dx1b_ref, uv_ref, gg_ref, pa_ref, pb_ref, lng_ref, lnb_ref, wtr_ref, wtrt_ref, bias_ref,
             wpa_ref, wpb_ref, wo_ref, *rest):
        ex_in = rest[:n]
        dproj_ref, dya_ref, dpa_ref, dpb_ref, vec_ref, dws_ref, dbs_ref = rest[n:n + 7]
        ex_out = rest[n + 7:2 * n + 7]
        vn_s, mx_s, dmx_s, dvn_s, accs, dbs_s, e_send, e_recv, e_local = rest[2 * n + 7:]
        i = pl.program_id(0)
        exchange = _Exchange(ex_in, ex_out, e_send, e_recv, e_local)

        @pl.when(i == 0)
        def _():
            exchange.start()
            accs[...] = jnp.zeros(accs.shape, F32)
            dbs_s[...] = jnp.zeros(dbs_s.shape, F32)
            dws_ref[...] = jnp.zeros(dws_ref.shape, F32)

        dm = _dot_nt(dx1b_ref[...], wo_ref[...])
        sa = _sigmoid(gg_ref[:, 0:D].astype(F32))
        sb = _sigmoid(gg_ref[:, D:2 * D].astype(F32))
        dpa = dm * sa
        dpb = dm * sb
        dpab = dpa.astype(BF)
        dpbb = dpb.astype(BF)
        dpa_ref[...] = dpab
        dpb_ref[...] = dpbb
        dproj_ref[:, 2 * D:3 * D] = ((dpa * pa_ref[...].astype(F32)) * (1.0 - sa)).astype(BF)
        dproj_ref[:, 3 * D:4 * D] = ((dpb * pb_ref[...].astype(F32)) * (1.0 - sb)).astype(BF)

        dya_ref[...] = _dot_nt(dpab, wpa_ref[...])
        dyb = _dot_nt(dpbb, wpb_ref[...])

        lng = lng_ref[...]
        gv, dgelu_v = _gelu_grad(uv_ref[:, D:2 * D].astype(F32))
        dv = gv - jnp.mean(gv, axis=-1, keepdims=True)
        rstd = lax.rsqrt(jnp.mean(dv * dv, axis=-1, keepdims=True) + EPS)
        xh = dv * rstd
        vn_s[...] = (xh * lng + lnb_ref[...]).astype(BF)

        nc = tm // CHUNK
        for c in range(nc):
            rs = slice(c * CHUNK, (c + 1) * CHUNK)
            for g in range(NGRP):
                cs = slice(g * GW, (g + 1) * GW)
                mx_s[rs, cs] = _dot(wtr_ref[g], vn_s[rs, cs])

        gu, dgelu_u = _gelu_grad(uv_ref[:, 0:D].astype(F32))
        mixed = mx_s[...] + jnp.concatenate([bias_ref[...]] * nc, axis=0)
        dproj_ref[:, 0:D] = ((dyb * mixed) * dgelu_u).astype(BF)
        dmx = dyb * gu
        dmx_s[...] = dmx.astype(BF)
        dbs_s[...] += jnp.sum(dmx.reshape(nc, CHUNK, D), axis=0)

        for c in range(nc):
            rs = slice(c * CHUNK, (c + 1) * CHUNK)
            for g in range(NGRP):
                cs = slice(g * GW, (g + 1) * GW)
                dvn_s[rs, cs] = _dot(wtrt_ref[g], dmx_s[rs, cs])
                dws_ref[g] += _dot_nt(dmx_s[rs, cs], vn_s[rs, cs])

        dvn = dvn_s[...]
        accs[0] += _fold(dvn * xh)
        accs[1] += _fold(dvn)
        dxh = dvn * lng
        m1 = jnp.mean(dxh, axis=-1, keepdims=True)
        m2 = jnp.mean(dxh * xh, axis=-1, keepdims=True)
        dproj_ref[:, D:2 * D] = ((rstd * (dxh - m1 - xh * m2)) * dgelu_v).astype(BF)

        @pl.when(i == nt - 1)
        def _():
            vec_ref[...] = jnp.zeros((8, D), F32)
            for j in range(2):
                vec_ref[j:j + 1, :] = jnp.sum(accs[j], axis=0, keepdims=True)
            row = lax.broadcasted_iota(jnp.int32, (CHUNK, CHUNK), 0)
            col = lax.broadcasted_iota(jnp.int32, (CHUNK, CHUNK), 1)
            for g in range(NGRP):
                dws_ref[g] = jnp.where(row >= col, dws_ref[g], 0.0)
                gs = jnp.sum(dbs_s[:, g * GW:(g + 1) * GW], axis=1, keepdims=True)
                dbs_ref[:, g * GW:(g + 1) * GW] = jnp.broadcast_to(gs, (CHUNK, GW))
            exchange.finish()

    tile = lambda w: pl.BlockSpec((tm, w), lambda i: (i, 0))
    vec = _const_spec((1, D))
    any_spec = pl.BlockSpec(memory_space=pl.ANY)
    outs = pl.pallas_call(
        body, name="mixer_bwd_pre", grid=(nt,),
        in_specs=[tile(D), pl.BlockSpec((tm, 2 * D), lambda i: (i, 1)), pl.BlockSpec((tm, 2 * D), lambda i: (i, 2)),
                  tile(D), tile(D), vec, vec, _const_spec(wtr.shape), _const_spec(wtrt.shape),
                  _const_spec(bias.shape), _const_spec((D, D)), _const_spec((D, D)), _const_spec((D, D))]
        + [any_spec] * n,
        out_specs=[tile(4 * D), tile(D), tile(D), tile(D), pl.BlockSpec((8, D), lambda i: (0, 0)),
                   pl.BlockSpec((NGRP, CHUNK, CHUNK), lambda i: (0, 0, 0)),
                   pl.BlockSpec((CHUNK, D), lambda i: (0, 0))] + [any_spec] * n,
        out_shape=[jax.ShapeDtypeStruct((s, 4 * D), BF), jax.ShapeDtypeStruct((s, D), F32),
                   jax.ShapeDtypeStruct((s, D), BF), jax.ShapeDtypeStruct((s, D), BF),
                   jax.ShapeDtypeStruct((8, D), F32), jax.ShapeDtypeStruct((NGRP, CHUNK, CHUNK), F32),
                   jax.ShapeDtypeStruct((CHUNK, D), F32)]
        + [jax.ShapeDtypeStruct(p.shape, p.dtype) for p in parts],
        scratch_shapes=[pltpu.VMEM((tm, D), BF), pltpu.VMEM((tm, D), F32), pltpu.VMEM((tm, D), BF),
                        pltpu.VMEM((tm, D), F32), pltpu.VMEM((2, RC, D), F32), pltpu.VMEM((CHUNK, D), F32)]
        + _gather_sems(n),
        compiler_params=_params(("arbitrary",)),
    )(dx1b, proj, proj, pa, pb, lng, lnb, wtr, wtrt, bias, wpa, wpb, wo, *parts)
    return outs[:7], outs[7:]


def _mixer_bwd_seq_call(dprojb, dya, proj, h, x, dx1, gates, gmix, w_in_all, cw, lam, wax, parts):
    s = dya.shape[0]
    tm = min(TM_MIX, s)
    nt = s // tm
    tb = tm // 8
    n = len(parts)
    nb, _, wb = w_in_all.shape

    def body(dpb_ref, dya_ref, xg_ref, xh8_ref, h_ref, hh8_ref, x_ref, dx1_ref, xc_s, r_s, ig_s, a_s, m_s,
             gmix_ref, win_ref, cw_ref, lam_ref, wax_ref, *rest):
        ex_in = rest[:n]
        dpa_ref, dx_ref, vec_ref, dwax_ref = rest[n:n + 4]
        ex_out = rest[n + 4:2 * n + 4]
        lm_s, dh_s, dxc_s, c_s, accs, e_send, e_recv, e_local = rest[2 * n + 4:]
        i = pl.program_id(0)
        exchange = _Exchange(ex_in, ex_out, e_send, e_recv, e_local)

        @pl.when(i == 0)
        def _():
            exchange.start()
            accs[...] = jnp.zeros(accs.shape, F32)
            dwax_ref[...] = jnp.zeros(dwax_ref.shape, F32)
            c_s[...] = jnp.zeros((8, D), F32)
            dxc_s[tm:tm + 8, :] = jnp.zeros((8, D), F32)

        first_tile = i == nt - 1
        prev8 = jnp.where(first_tile, 0.0, xh8_ref[RC - 8:RC, :].astype(F32))
        hprev8 = jnp.where(first_tile, 0.0, hh8_ref[...])
        lamv = lam_ref[...]
        sp = _softplus(-lamv)
        hv = h_ref[...]
        g, dg = _gelu_grad(xg_ref[:, D:2 * D].astype(F32))
        dya = dya_ref[...]
        lm_s[...] = dya * g
        drg = ((dya * hv) * dg).astype(BF)
        dpa_ref[:, D:2 * D] = drg
        dpa_ref[:, 2 * D:6 * D] = dpb_ref[...]

        def dpb_block(k):
            return _dot_nt(dpb_ref[:, k * wb - 2 * D:(k + 1) * wb - 2 * D], win_ref[k])
        dh = (_dot_nt(drg[:, 0:2 * wb - D], win_ref[1, :, D - wb:wb])
              + _dot_nt(drg[:, 2 * wb - D:D], win_ref[2, :, 0:2 * D - 2 * wb])
              + _dot_nt(dpb_ref[:, 0:3 * wb - 2 * D], win_ref[2, :, 2 * D - 2 * wb:wb]))
        for k in range(3, 6):
            dh = dh + dpb_block(k)
        dh_s[...] = dh

        c_s[0:1, :] = _scan_bwd(a_s, lm_s, c_s[0:1, :], tm)

        hprev = _shift_back(hprev8, h_ref[...], 1)
        for q in range(NQ):
            cs = slice(q * QW, (q + 1) * QW)
            r = r_s[:, cs].astype(F32)
            ig = ig_s[:, cs].astype(F32)
            a = a_s[:, cs]
            m = m_s[:, cs].astype(F32)
            lm = lm_s[:, cs]
            xq = xc_s[:, cs].astype(F32)
            dixc = lm * m
            dla = (lm * hprev[:, cs]) * a - ((lm * (ig * xq)) * (a * a)) / m
            accs[3, :, cs] += _fold(dla * r)
            dza = (dla * (-RG_C * sp[:, cs])) * (r * (1.0 - r))
            dzx = (dixc * xq) * (ig * (1.0 - ig))
            accs[1, :, cs] += _fold(dza)
            accs[2, :, cs] += _fold(dzx)
            dz = jnp.concatenate([dza, dzx], axis=1).astype(BF)
            dxc_s[0:tm, cs] = dixc * ig + _dot_nt(dz, wax_ref[q])
            dwax_ref[q] += _dot_tn(xc_s[:, cs], dz)

        cur = dxc_s[0:tm, :]
        nx = dxc_s[tm:tm + 8, :]
        drx = cw_ref[3:4, :] * cur
        for j in (1, 2, 3):
            drx = drx + cw_ref[3 - j:4 - j, :] * _shift_fwd(cur, nx, j)
        accs[0] += _fold(cur)
        rx = xg_ref[:, 0:D].astype(F32)
        accs[7] += _fold(cur * rx)
        for j in (1, 2, 3):
            accs[7 - j] += _fold(cur * _shift_back(prev8, rx, j))
        dxc_s[tm:tm + 8, :] = cur[0:8, :]

        drxb = drx.astype(BF)
        dpa_ref[:, 0:D] = drxb
        dh = dh_s[...] + _dot_nt(drxb[:, 0:wb], win_ref[0]) + _dot_nt(drxb[:, wb:D], win_ref[1, :, 0:D - wb])
        for k in range(6, nb):
            dh = dh + dpb_block(k)
        xx = x_ref[...]
        rn = lax.rsqrt(jnp.mean(xx * xx, axis=-1, keepdims=True) + EPS)
        xhn = xx * rn
        accs[8] += _fold(dh * xhn)
        dxh = dh * gmix_ref[...]
        dx_ref[...] = dx1_ref[...] + rn * (dxh - xhn * jnp.mean(dxh * xhn, axis=-1, keepdims=True))

        @pl.when(i == nt - 1)
        def _():
            vec_ref[...] = jnp.zeros((16, D), F32)
            for j in range(9):
                vec_ref[j:j + 1, :] = jnp.sum(accs[j], axis=0, keepdims=True)
            vec_ref[3:4, :] = vec_ref[3:4, :] * (RG_C * _sigmoid(-lamv))
            exchange.finish()

    rev = lambda w: pl.BlockSpec((tm, w), lambda i: (nt - 1 - i, 0))
    halo = pl.BlockSpec((8, D), lambda i: (jnp.maximum((nt - 1 - i) * tb - 1, 0), 0))
    halo_bf = pl.BlockSpec((RC, D), lambda i: (jnp.maximum((nt - 1 - i) * (tm // RC) - 1, 0), 0))
    vec = _const_spec((1, D))
    any_spec = pl.BlockSpec(memory_space=pl.ANY)
    outs = pl.pallas_call(
        body, name="mixer_bwd_seq", grid=(nt,),
        in_specs=[rev(4 * D), rev(D), rev(2 * D), halo_bf, rev(D), halo] + [rev(D)] * 7
        + [vec, _const_spec(w_in_all.shape), _const_spec((4, D)), vec, _const_spec(wax.shape)] + [any_spec] * n,
        out_specs=[rev(6 * D), rev(D), pl.BlockSpec((16, D), lambda i: (0, 0)),
                   pl.BlockSpec((NQ, QW, 2 * QW), lambda i: (0, 0, 0))] + [any_spec] * n,
        out_shape=[jax.ShapeDtypeStruct((s, 6 * D), BF), jax.ShapeDtypeStruct((s, D), F32),
                   jax.ShapeDtypeStruct((16, D), F32), jax.ShapeDtypeStruct((NQ, QW, 2 * QW), F32)]
        + [jax.ShapeDtypeStruct(p.shape, p.dtype) for p in parts],
        scratch_shapes=[pltpu.VMEM((tm, D), F32), pltpu.VMEM((tm, D), F32),
                        pltpu.VMEM((tm + 8, D), F32), pltpu.VMEM((8, D), F32), pltpu.VMEM((9, RC, D), F32)]
        + _gather_sems(n),
        compiler_params=_params(("arbitrary",)),
    )(dprojb, dya, proj, proj, h, h, x, dx1, *gates, gmix, w_in_all, cw, lam, wax, *parts)
    return outs[:4], outs[4:]


def _device_of(d):
    return (d // 4, lax.rem(d // 2, 2), lax.rem(d, 2))


_DW_PLAN = (("C", 1), ("C", 2), ("A", 1), ("C", 3), ("A", 2), ("S", 0), ("A", 3), ("O", 0))
_DW_FLIPS = tuple({"C": 2 * j + 1, "A": 2 * j, "S": 1, "O": 0}[kind] for kind, j in _DW_PLAN)


def _dw_exchange_call(name, order, a, b, a_spec, b_spec, k1, n1, s, small):
    ts = min(TS_DW, s)
    ns = s // ts
    nstep = len(_DW_PLAN)

    def slab(i, order_ref):
        return order_ref[i]

    def body(order_ref, a_ref, b_ref, g_ref, own_ref, recv_ref, gsum_ref, acc, sbuf, stage, send_sems, recv_sems,
             st_send, st_recv, local_sem, rbuf, gacc, send1, recv1, send2, recv2):
        i = pl.program_id(0)
        j = pl.program_id(1)
        me = _me()
        mi = _lin(me)
        sibling = _flip(me, 1)
        allreduce = _SmallAllReduce(g_ref, gacc, rbuf, send1, recv1, send2, recv2)
        pl.when((i == 0) & (j == 0))(allreduce.scatter)
        pl.when((i == nstep // 2) & (j == 0))(allreduce.reduce)
        p = _dot_tn(a_ref[...], b_ref[...])

        @pl.when(j == 0)
        def _():
            acc[...] = p

        @pl.when(j > 0)
        def _():
            acc[...] += p

        def send(step):
            kind, jj = _DW_PLAN[step]
            src = sbuf.at[step % 2]
            if kind == "C":
                return pltpu.make_async_remote_copy(
                    src_ref=src, dst_ref=stage.at[jj - 1], send_sem=st_send.at[jj - 1], recv_sem=st_recv.at[jj - 1],
                    device_id=sibling, device_id_type=MESH)
            to = sibling if kind == "S" else _flip(me, 2 * jj)
            return pltpu.make_async_remote_copy(
                src_ref=src, dst_ref=recv_ref.at[mi], send_sem=send_sems.at[jj], recv_sem=recv_sems.at[jj],
                device_id=to, device_id_type=MESH)

        def arrival(jj):
            frm = sibling if jj == 0 else _flip(me, 2 * jj)
            return pltpu.make_async_remote_copy(
                src_ref=sbuf.at[0], dst_ref=recv_ref.at[_lin(frm)], send_sem=send_sems.at[jj],
                recv_sem=recv_sems.at[jj], device_id=frm, device_id_type=MESH)

        def staged(jj):
            return pltpu.make_async_remote_copy(
                src_ref=sbuf.at[0], dst_ref=stage.at[jj - 1], send_sem=st_send.at[jj - 1], recv_sem=st_recv.at[jj - 1],
                device_id=sibling, device_id_type=MESH)

        for step, (kind, jj) in enumerate(_DW_PLAN):
            @pl.when((i == step) & (j == ns - 1))
            def _(step=step, kind=kind, jj=jj):
                if step >= 2:
                    send(step - 2).wait_send()
                if kind == "A":
                    staged(jj).wait_recv()
                    sbuf[step % 2] = (acc[...] + stage[jj - 1].astype(F32)).astype(BF)
                else:
                    sbuf[step % 2] = acc[...].astype(BF)
                if kind != "O":
                    send(step).start()
                else:
                    own_ref[...] = acc[...]
                    mine = pltpu.make_async_copy(sbuf.at[step % 2], recv_ref.at[mi], local_sem)
                    mine.start()
                    send(step - 1).wait_send()
                    for q in range(4):
                        arrival(q).wait_recv()
                    mine.wait()
                    allreduce.finish()
                    gsum_ref[...] = gacc[...]

    vmem = pl.BlockSpec(memory_space=pltpu.VMEM)
    grid_spec = pltpu.PrefetchScalarGridSpec(
        num_scalar_prefetch=1, grid=(nstep, ns),
        in_specs=[a_spec(ts, slab), b_spec(ts, slab), vmem],
        out_specs=[pl.BlockSpec((k1, n1), lambda i, j, o: (0, 0)), pl.BlockSpec(memory_space=pl.ANY), vmem],
        scratch_shapes=[pltpu.VMEM((k1, n1), F32), pltpu.VMEM((2, k1, n1), BF), pltpu.VMEM((3, k1, n1), BF),
                        pltpu.SemaphoreType.DMA((4,)), pltpu.SemaphoreType.DMA((4,)),
                        pltpu.SemaphoreType.DMA((3,)), pltpu.SemaphoreType.DMA((3,)),
                        pltpu.SemaphoreType.DMA(()), pltpu.VMEM((NDEV, SMALL_PER, D), F32),
                        pltpu.VMEM((SMALL_ROWS, D), F32)]
        + [pltpu.SemaphoreType.DMA((NDEV - 1,))] * 4)
    return pl.pallas_call(
        body, name=name, grid_spec=grid_spec,
        out_shape=[jax.ShapeDtypeStruct((k1, n1), F32), jax.ShapeDtypeStruct((NDEV, k1, n1), BF),
                   jax.ShapeDtypeStruct((SMALL_ROWS, D), F32)],
        compiler_params=_params(("arbitrary", "arbitrary")),
    )(order, a, b, small)


def _dw_plain_call(name, me, a, b, a_spec, b_spec, split, k1, n1, s):
    nb = NDEV // split
    r = k1 // split
    ts = min(TS_DW, s)
    ns = s // ts

    def slab(i, me_ref):
        return i

    def body(me_ref, a_ref, b_ref, own_ref, part_ref, acc):
        i = pl.program_id(0)
        j = pl.program_id(1)
        p = _dot_tn(a_ref[...], b_ref[...])

        @pl.when(j == 0)
        def _():
            acc[...] = p

        @pl.when(j > 0)
        def _():
            acc[...] += p

        @pl.when(j == ns - 1)
        def _():
            part_ref[...] = acc[...].astype(BF)

            @pl.when(i == me_ref[0] // split)
            def _():
                off = pl.multiple_of(lax.rem(me_ref[0], split) * r, RC)
                own_ref[...] = acc[pl.ds(off, r), :]

    grid_spec = pltpu.PrefetchScalarGridSpec(
        num_scalar_prefetch=1, grid=(nb, ns),
        in_specs=[a_spec(ts, slab), b_spec(ts, slab)],
        out_specs=[pl.BlockSpec((r, n1), lambda i, j, me_ref: (0, 0)),
                   pl.BlockSpec((None, k1, n1), lambda i, j, me_ref: (i, 0, 0))],
        scratch_shapes=[pltpu.VMEM((k1, n1), F32)])
    own, part = pl.pallas_call(
        body, name=name, grid_spec=grid_spec,
        out_shape=[jax.ShapeDtypeStruct((r, n1), F32), jax.ShapeDtypeStruct((nb, k1, n1), BF)],
        compiler_params=_params(("arbitrary", "arbitrary")),
    )(me, a, b)
    return own, part.reshape(NDEV, r, n1)


def _rows2d(w):
    return lambda ts, slab: pl.BlockSpec((ts, w), lambda i, j, me_ref: (j, 0))


def _cols2d(w):
    return lambda ts, slab: pl.BlockSpec((ts, w), lambda i, j, me_ref: (j, slab(i, me_ref)))


def _blk3d(w):
    return lambda ts, slab: pl.BlockSpec((None, ts, w), lambda i, j, me_ref: (slab(i, me_ref), j, 0))


_BC1 = 1.0 - ADAM_B1 ** ADAM_STEP
_BC2 = 1.0 - ADAM_B2 ** ADAM_STEP


def _adamw_math(w, g, m, v):
    m = ADAM_B1 * m + (1.0 - ADAM_B1) * g
    v = ADAM_B2 * v + (1.0 - ADAM_B2) * (g * g)
    m_hat = m / _BC1
    v_hat = v / _BC2
    delta = -ADAM_LR * (m_hat / (jnp.sqrt(v_hat) + ADAM_EPS) + ADAM_WD * w)
    return delta, m, v


def _row_tile(r):
    for t in (256, 176, 128, 64, 32, 16, 8):
        if r % t == 0:
            return t
    return r


def _reduce_adamw_call(name, sel, own, recv, w, m, v):
    r, c = own.shape
    tr = _row_tile(r)

    def body(sel_ref, own_ref, recv_ref, w_ref, m_ref, v_ref, g_ref, d_ref, nm_ref, nv_ref):
        g = jnp.zeros((tr, c), F32)
        for sdev in range(NDEV):
            part = jnp.where(sel_ref[sdev] == 1, recv_ref[sdev].astype(F32), 0.0)
            g = g + jnp.where(sel_ref[sdev] == 2, own_ref[...], part)
        g_ref[...] = g
        d_ref[...], nm_ref[...], nv_ref[...] = _adamw_math(w_ref[...], g, m_ref[...], v_ref[...])

    tile = pl.BlockSpec((tr, c), lambda i, me_ref: (i, 0))
    grid_spec = pltpu.PrefetchScalarGridSpec(
        num_scalar_prefetch=1, grid=(r // tr,),
        in_specs=[tile, pl.BlockSpec((NDEV, tr, c), lambda i, me_ref: (0, i, 0)), tile, tile, tile],
        out_specs=[tile] * 4)
    return pl.pallas_call(
        body, name=name, grid_spec=grid_spec,
        out_shape=[jax.ShapeDtypeStruct((r, c), F32)] * 4,
        compiler_params=_params(("parallel",)),
    )(sel, own, recv, w, m, v)


def _adamw_call(name, w, g, m, v):
    r, c = w.shape
    tr = _row_tile(r)

    def body(w_ref, g_ref, m_ref, v_ref, d_ref, nm_ref, nv_ref):
        d_ref[...], nm_ref[...], nv_ref[...] = _adamw_math(w_ref[...], g_ref[...], m_ref[...], v_ref[...])

    tile = pl.BlockSpec((tr, c), lambda i: (i, 0))
    return pl.pallas_call(
        body, name=name, grid=(r // tr,), in_specs=[tile] * 4, out_specs=[tile] * 3,
        out_shape=[jax.ShapeDtypeStruct((r, c), F32)] * 3,
        compiler_params=_params(("parallel",)),
    )(w, g, m, v)


def _me():
    return lax.axis_index("x"), lax.axis_index("y"), lax.axis_index("c")


def _flip(pos, r):
    x, y, c = pos
    return (1 - x if r & 4 else x, 1 - y if r & 2 else y, 1 - c if r & 1 else c)


def _lin(pos):
    return pos[0] * 4 + pos[1] * 2 + pos[2]


class _SmallAllReduce:
    def __init__(self, g_ref, out_ref, rbuf, send1, recv1, send2, recv2):
        self.g, self.out, self.rbuf = g_ref, out_ref, rbuf
        self.sems = (send1, recv1, send2, recv2)
        self.me = _me()
        self.mi = _lin(self.me)

    @staticmethod
    def _rows(d):
        return pl.ds(pl.multiple_of(d * SMALL_PER, 8), SMALL_PER)

    def _scatter(self, r, outgoing):
        peer = _flip(self.me, r)
        src, dst = (_lin(peer), self.mi) if outgoing else (self.mi, _lin(peer))
        return pltpu.make_async_remote_copy(
            src_ref=self.g.at[self._rows(src)], dst_ref=self.rbuf.at[dst],
            send_sem=self.sems[0].at[r - 1], recv_sem=self.sems[1].at[r - 1], device_id=peer, device_id_type=MESH)

    def _spread(self, r, outgoing):
        peer = _flip(self.me, r)
        rows = self._rows(self.mi if outgoing else _lin(peer))
        return pltpu.make_async_remote_copy(
            src_ref=self.out.at[rows], dst_ref=self.out.at[rows],
            send_sem=self.sems[2].at[r - 1], recv_sem=self.sems[3].at[r - 1], device_id=peer, device_id_type=MESH)

    def scatter(self):
        for r in range(1, NDEV):
            self._scatter(r, True).start()
        self.rbuf[self.mi] = self.g[self._rows(self.mi), :]

    def reduce(self):
        for r in range(1, NDEV):
            self._scatter(r, False).wait_recv()
        for r in range(1, NDEV):
            self._scatter(r, True).wait_send()
        tot = self.rbuf[0]
        for d in range(1, NDEV):
            tot = tot + self.rbuf[d]
        self.out[self._rows(self.mi), :] = tot
        for r in range(1, NDEV):
            self._spread(r, True).start()

    def finish(self):
        for r in range(1, NDEV):
            self._spread(r, False).wait_recv()
        for r in range(1, NDEV):
            self._spread(r, True).wait_send()


def _head_blocks(w):
    z = jnp.zeros((64, 64), w.dtype)
    groups = []
    for q in range(NQ):
        rows = [jnp.concatenate([w[4 * q + a] if a == b else z for b in range(4)], axis=1) for a in range(4)]
        groups.append(jnp.concatenate(rows, axis=0))
    return jnp.stack(groups)


def _head_unblocks(g):
    return jnp.stack([g[q, 64 * a:64 * a + 64, 64 * a:64 * a + 64] for q in range(NQ) for a in range(4)])


def _local_step(x, tgt, p, me, order, order_dw):
    s = x.shape[0]
    vec = lambda a: a.reshape(1, D)
    gmix, gffn, gfin = vec(p["norm_mix_g"]), vec(p["norm_ffn_g"]), vec(p["norm_final_g"])
    cb, ba, bx, lam = vec(p["conv_b"]), vec(p["rg_ba"]), vec(p["rg_bx"]), vec(p["rg_lambda"])
    lng, lnb = vec(p["sgu_ln_g"]), vec(p["sgu_ln_b"])
    wax = jnp.concatenate([_head_blocks(p["rg_wa"]), _head_blocks(p["rg_wx"])], axis=2).astype(BF)
    tril = jnp.tril(jnp.ones((CHUNK, CHUNK), bool))
    ws = jnp.where(tril[None], p["sgu_ws"], 0.0)
    wtr = ws.astype(BF)
    wtrt = jnp.swapaxes(ws, 1, 2).astype(BF)
    bias = jnp.repeat(p["sgu_bs"].T, GW, axis=1)
    shard = {k: p[k].astype(BF) for k in _BIG}

    proj, h1, w_in, (wpa, wpb, wo, cw) = _proj_gather_call(
        x, gmix, shard["w_in"], order, [shard["w_proj_a"], shard["w_proj_b"], shard["w_out"], p["conv_w"]])
    wpa, wpb, wo = (t.reshape(D, D) for t in (wpa, wpb, wo))
    cw = jnp.swapaxes(cw, 0, 1).reshape(4, D)
    (h, pa, pb, x1, *gates, ya, yb, mg), (wgu, wdn) = _mixer_fwd_call(
        proj, x, cw, cb, ba, bx, lam, lng, lnb, wax, wtr, bias, wpa, wpb, wo, [shard["w_gate_up"], shard["w_down"]])
    wdn = wdn.reshape(NDEV // 2, -1, D)
    nb, _, wb = w_in.shape
    fb = wgu.shape[2]
    nh = wdn.shape[0]
    dx1, dx1b, h2, act, dgu, dx2b, facc = _ffn_call(x1, tgt, gffn, gfin, wgu, wdn)
    assert nb == NDEV and 2 * nh == NDEV
    own_gu, part_gu = _dw_plain_call("dw_gate_up", me, dgu, h2, _blk3d(fb), _rows2d(D), 1, fb, D, s)
    own_dn, part_dn = _dw_plain_call("dw_down", me, act, dx2b, _blk3d(fb), _rows2d(D), 2, fb, D, s)
    (dprojb, dya, dpa, dpb, bvec, dws, dbs), (recv_dn,) = _mixer_bwd_pre_call(
        dx1b, proj, pa, pb, lng, lnb, wtr, wtrt, bias, wpa, wpb, wo, [part_dn])
    own_pa, part_pa = _dw_plain_call("dw_proj_a", me, ya, dpa, _rows2d(D), _rows2d(D), NDEV, D, D, s)
    own_pb, part_pb = _dw_plain_call("dw_proj_b", me, yb, dpb, _rows2d(D), _rows2d(D), NDEV, D, D, s)
    own_wo, part_wo = _dw_plain_call("dw_out", me, mg, dx1b, _rows2d(D), _rows2d(D), NDEV, D, D, s)
    (dproj, dx, svec, dwax), (recv_gu, recv_pa, recv_pb, recv_wo) = _mixer_bwd_seq_call(
        dprojb, dya, proj, h, x, dx1, gates, gmix, w_in, cw, lam, wax, [part_gu, part_pa, part_pb, part_wo])
    small = {
        "norm_mix_g": svec[8], "norm_ffn_g": facc[0], "norm_final_g": facc[1],
        "conv_b": svec[0], "rg_ba": svec[1], "rg_bx": svec[2], "rg_lambda": svec[3],
        "sgu_ln_g": bvec[0], "sgu_ln_b": bvec[1],
        "rg_wa": _head_unblocks(dwax[:, :, 0:QW]), "rg_wx": _head_unblocks(dwax[:, :, QW:2 * QW]),
        "sgu_ws": dws, "sgu_bs": dbs[:, ::GW].T,
    }
    packed = _pack_small(small, svec[4:8], facc[2])
    own_in, recv_in, gsum = _dw_exchange_call("dw_in", order_dw, h1, dproj, _rows2d(D), _cols2d(wb), D, wb, s, packed)
    dw = {
        "w_gate_up": (own_gu, recv_gu), "w_down": (own_dn, recv_dn), "w_proj_a": (own_pa, recv_pa),
        "w_proj_b": (own_pb, recv_pb), "w_out": (own_wo, recv_wo), "w_in": (own_in, recv_in),
    }
    return gsum, dx, dw


_BIG = ("w_in", "w_gate_up", "w_down", "w_proj_a", "w_proj_b", "w_out")
_VEC_ROWS = ("norm_mix_g", "norm_ffn_g", "norm_final_g", "conv_b", "rg_ba", "rg_bx", "rg_lambda",
             "sgu_ln_g", "sgu_ln_b", "sgu_bs")
_WEIGHTS = ("norm_mix_g", "w_in", "conv_w", "conv_b", "rg_wa", "rg_ba", "rg_wx", "rg_bx", "rg_lambda",
            "sgu_ln_g", "sgu_ln_b", "sgu_ws", "sgu_bs", "w_proj_a", "w_proj_b", "w_out", "norm_ffn_g",
            "w_gate_up", "w_down", "norm_final_g")


def _pack_small(t, conv_w, extra=None):
    extra = jnp.zeros((1, D), F32) if extra is None else extra.reshape(1, D)
    head = jnp.concatenate([t[k].reshape(1, D) for k in _VEC_ROWS] + [conv_w, extra, jnp.zeros((1, D), F32)], axis=0)
    return jnp.concatenate([head, t["rg_wa"].reshape(64, D), t["rg_wx"].reshape(64, D), t["sgu_ws"].reshape(128, D),
                            jnp.zeros((SMALL_ROWS - 272, D), F32)], axis=0)


def _unpack_small(a):
    out = {k: a[j] for j, k in enumerate(_VEC_ROWS)}
    out["conv_w"] = a[10:14]
    out["rg_wa"] = a[16:80].reshape(16, 64, 64)
    out["rg_wx"] = a[80:144].reshape(16, 64, 64)
    out["sgu_ws"] = a[144:272].reshape(NGRP, CHUNK, CHUNK)
    return out


def kernel(x, norm_mix_g, w_in, conv_w, conv_b, rg_wa, rg_ba, rg_wx, rg_bx, rg_lambda, sgu_ln_g, sgu_ln_b, sgu_ws, sgu_bs, w_proj_a, w_proj_b, w_out, norm_ffn_g, w_gate_up, w_down, norm_final_g, loss_target, m_norm_mix_g, m_w_in, m_conv_w, m_conv_b, m_rg_wa, m_rg_ba, m_rg_wx, m_rg_bx, m_rg_lambda, m_sgu_ln_g, m_sgu_ln_b, m_sgu_ws, m_sgu_bs, m_w_proj_a, m_w_proj_b, m_w_out, m_norm_ffn_g, m_w_gate_up, m_w_down, m_norm_final_g, v_norm_mix_g, v_w_in, v_conv_w, v_conv_b, v_rg_wa, v_rg_ba, v_rg_wx, v_rg_bx, v_rg_lambda, v_sgu_ln_g, v_sgu_ln_b, v_sgu_ws, v_sgu_bs, v_w_proj_a, v_w_proj_b, v_w_out, v_norm_ffn_g, v_w_gate_up, v_w_down, v_norm_final_g):
    args = dict(locals())
    w = {k: args[k] for k in _WEIGHTS}
    m = {k: args["m_" + k] for k in _WEIGHTS}
    v = {k: args["v_" + k] for k in _WEIGHTS}
    for d in (w, m, v):
        for k in _WEIGHTS:
            if k != "norm_final_g":
                d[k] = d[k][0]
    me = _lin(_me())
    me1 = me.reshape(1).astype(jnp.int32)

    order = jnp.bitwise_xor(me, jnp.array(_PASS_FLIPS, jnp.int32)).astype(jnp.int32)
    order_dw = jnp.bitwise_xor(me, jnp.array(_DW_FLIPS, jnp.int32)).astype(jnp.int32)

    gsum, dx, dw = _local_step(x[0], loss_target[0], w, me1, order, order_dw)

    peer = jnp.bitwise_xor(jnp.arange(NDEV, dtype=jnp.int32), me)
    sel_direct = jnp.where(peer == 0, 2, 1).astype(jnp.int32)
    sel_two_level = jnp.where(peer == 0, 2, jnp.where((peer == 1) | (peer % 2 == 0), 1, 0)).astype(jnp.int32)
    grads, delta, new_m, new_v = {}, {}, {}, {}
    for k in _BIG:
        own, recv = dw[k]
        flip = own.shape != w[k].shape
        wmv = [jnp.swapaxes(t, 0, 1) if flip else t for t in (w[k], m[k], v[k])]
        res = _reduce_adamw_call("adamw_" + k, sel_two_level if k == "w_in" else sel_direct, own, recv, *wmv)
        grads[k], delta[k], new_m[k], new_v[k] = (jnp.swapaxes(t, 0, 1) if flip else t for t in res)

    loss = (0.5 / D) * jnp.sum(gsum[14])
    zc = jnp.zeros((4, D), F32)
    d_s, m_s, v_s = _adamw_call("adamw_small", _pack_small(w, zc), gsum, _pack_small(m, zc), _pack_small(v, zc))
    gs, ds, ms, vs = _unpack_small(gsum), _unpack_small(d_s), _unpack_small(m_s), _unpack_small(v_s)
    g_cw = lax.dynamic_slice(gs["conv_w"], (0, me * 128), (4, 128))
    ds["conv_w"], ms["conv_w"], vs["conv_w"] = _adamw_call("adamw_conv_w", w["conv_w"], g_cw, m["conv_w"], v["conv_w"])
    gs["conv_w"] = g_cw
    for k in _WEIGHTS:
        if k not in _BIG:
            shp = w[k].shape
            grads[k], delta[k], new_m[k], new_v[k] = (t[k].reshape(shp) for t in (gs, ds, ms, vs))

    def lift(t, k):
        return t[k] if k == "norm_final_g" else t[k][None]

    outs = [loss, dx[None]]
    for t in (grads, delta, new_m, new_v):
        outs += [lift(t, k) for k in _WEIGHTS]
    return tuple(outs)
```

```python
import functools

import jax
import jax.numpy as jnp
from jax import lax
from jax.experimental import pallas as pl
from jax.experimental.pallas import tpu as pltpu

F32 = jnp.float32
BF = jnp.bfloat16

D = 1024
NDEV = 8
EPS = 1e-6
RG_C = 8.0
CHUNK = 128
NGRP = 8
GW = 128
NQ = 4
QW = 256
RC = 16
SMALL_ROWS = 320
SMALL_PER = SMALL_ROWS // NDEV

ADAM_LR = 0.001
ADAM_B1 = 0.9
ADAM_B2 = 0.999
ADAM_EPS = 1e-08
ADAM_WD = 0.01
ADAM_STEP = 10

VMEM_LIMIT = 60 * 1024 * 1024

MESH = pl.DeviceIdType.MESH


def _rows(n, fn, unroll=2, rc=RC):
    def body(i, c):
        fn(pl.multiple_of(i * rc, rc))
        return c
    lax.fori_loop(0, n // rc, body, 0, unroll=unroll)


def _fold(v):
    return jnp.sum(v.reshape(v.shape[0] // RC, RC, v.shape[1]), axis=0)


def _dot(a, b):
    return jnp.dot(a, b, preferred_element_type=F32)


def _dot_nt(a, b):
    return lax.dot_general(a, b, (((1,), (1,)), ((), ())), preferred_element_type=F32)


def _dot_tn(a, b):
    return lax.dot_general(a, b, (((0,), (0,)), ((), ())), preferred_element_type=F32)


_GC = 0.7978845608028654
_GK = 0.044715


def _gelu(x):
    t = jnp.tanh(_GC * (x + _GK * (x * x * x)))
    return x * (0.5 * (1.0 + t))


def _gelu_grad(x):
    x2 = x * x
    t = jnp.tanh(_GC * (x + _GK * (x2 * x)))
    cdf = 0.5 * (1.0 + t)
    dg = cdf + (0.5 * x) * (1.0 - t * t) * (_GC * (1.0 + (3.0 * _GK) * x2))
    return x * cdf, dg


def _sigmoid(x):
    return jax.nn.sigmoid(x)


def _log1p(e):
    u = 1.0 + e
    d = u - 1.0
    return jnp.where(d == 0.0, e, jnp.log(u) * (e / jnp.where(d == 0.0, 1.0, d)))


def _softplus(z):
    return jnp.maximum(z, 0.0) + _log1p(jnp.exp(-jnp.abs(z)))


def _neg_expm1(z):
    u = jnp.exp(z)
    lu = jnp.log(u)
    k = (1.0 - u) * (z / jnp.where(lu == 0.0, 1.0, lu))
    small = jnp.where(lu == 0.0, -z, k)
    return jnp.where(z > -0.5, small, 1.0 - u)


def _shift_back(prev8, cur, j):
    cat = jnp.concatenate([prev8, cur], axis=0)
    return pltpu.roll(cat, j, 0)[8:8 + cur.shape[0]]


def _shift_fwd(cur, next8, j):
    cat = jnp.concatenate([cur, next8], axis=0)
    n = cat.shape[0]
    return pltpu.roll(cat, n - j, 0)[0:cur.shape[0]]


def _const_spec(shape):
    nd = len(shape)
    return pl.BlockSpec(shape, lambda *_: (0,) * nd, pipeline_mode=pl.Buffered(1))


def _params(sem):
    return pltpu.CompilerParams(dimension_semantics=sem, vmem_limit_bytes=VMEM_LIMIT)


TM_PROJ = 1024
TM_MIX = 256
TM_FFN = 256
TM_DX = 512
TS_DW = 4096


_CHIPS = (4, 2, 6)
_PASS_FLIPS = (0, 1, 4, 2, 6, 5, 3, 7)


class _Gather:
    def __init__(self, ins, outs, send_sems, recv_sems, local_sems):
        self.ins, self.outs = ins, outs
        self.send_sems, self.recv_sems, self.local_sems = send_sems, recv_sems, local_sems
        self.me = _me()
        self.sibling = _flip(self.me, 1)

    def _copy(self, a, kind, block, to, src=None):
        dst = self.outs[a].at[_lin(block)]
        return pltpu.make_async_remote_copy(
            src_ref=dst if src is None else src, dst_ref=dst,
            send_sem=self.send_sems.at[a, kind], recv_sem=self.recv_sems.at[a, kind],
            device_id=to, device_id_type=MESH)

    def _local(self, a):
        return pltpu.make_async_copy(self.ins[a], self.outs[a].at[_lin(self.me)], self.local_sems.at[a])

    def start(self):
        for a in range(len(self.ins)):
            self._local(a).start()
            self._copy(a, 0, self.me, self.sibling, src=self.ins[a]).start()
            for j, f in enumerate(_CHIPS):
                self._copy(a, 1 + j, self.me, _flip(self.me, f), src=self.ins[a]).start()

    def forward(self):
        for j, f in enumerate(_CHIPS):
            for a in range(len(self.ins)):
                self._copy(a, 1 + j, _flip(self.me, f), self.me).wait_recv()
                self._copy(a, 4 + j, _flip(self.me, f), self.sibling).start()

    def finish(self):
        for a in range(len(self.ins)):
            self._copy(a, 0, self.sibling, self.me).wait_recv()
            for j, f in enumerate(_CHIPS):
                self._copy(a, 4 + j, _flip(self.me, f | 1), self.me).wait_recv()
            self._copy(a, 0, self.me, self.sibling, src=self.ins[a]).wait_send()
            for j, f in enumerate(_CHIPS):
                self._copy(a, 1 + j, self.me, _flip(self.me, f), src=self.ins[a]).wait_send()
                self._copy(a, 4 + j, _flip(self.me, f), self.sibling).wait_send()
            self._local(a).wait()


class _Exchange:
    def __init__(self, ins, outs, send_sems, recv_sems, local_sems):
        self.ins, self.outs = ins, outs
        self.send_sems, self.recv_sems, self.local_sems = send_sems, recv_sems, local_sems
        self.me = _me()

    def _copy(self, a, r, outgoing):
        peer = _flip(self.me, r)
        src, dst = (peer, self.me) if outgoing else (self.me, peer)
        return pltpu.make_async_remote_copy(
            src_ref=self.ins[a].at[_lin(src)], dst_ref=self.outs[a].at[_lin(dst)],
            send_sem=self.send_sems.at[a, r - 1], recv_sem=self.recv_sems.at[a, r - 1],
            device_id=peer, device_id_type=MESH)

    def _local(self, a):
        mi = _lin(self.me)
        return pltpu.make_async_copy(self.ins[a].at[mi], self.outs[a].at[mi], self.local_sems.at[a])

    def start(self):
        for a in range(len(self.ins)):
            self._local(a).start()
        for r in range(1, NDEV):
            for a in range(len(self.ins)):
                self._copy(a, r, True).start()

    def finish(self):
        for r in range(1, NDEV):
            for a in range(len(self.ins)):
                self._copy(a, r, False).wait_recv()
        for r in range(1, NDEV):
            for a in range(len(self.ins)):
                self._copy(a, r, True).wait_send()
        for a in range(len(self.ins)):
            self._local(a).wait()


def _gather_sems(n):
    return [pltpu.SemaphoreType.DMA((n, 7)), pltpu.SemaphoreType.DMA((n, 7)), pltpu.SemaphoreType.DMA((n,))]


def _proj_gather_call(x, gmix, w_shard, order, extras):
    s = x.shape[0]
    tm = min(TM_PROJ, s)
    nt = s // tm
    wb = w_shard.shape[1]
    n = len(extras)

    def body(order_ref, x_ref, g_ref, wsh_ref, *rest):
        ex_in = rest[:n]
        proj_hbm, h1_hbm, wall_hbm = rest[n:n + 3]
        ex_out = rest[n + 3:2 * n + 3]
        wall, hc, pbuf, ws_send, ws_recv, ex_send, ex_recv, ex_local, out_sems, psems = rest[2 * n + 3:]
        k = pl.program_id(0)
        i = pl.program_id(1)
        me = _me()
        sibling = _flip(me, 1)
        gather = _Gather(ex_in, ex_out, ex_send, ex_recv, ex_local)

        def wcopy(kind, block, to):
            ref = wall.at[_lin(block)]
            return pltpu.make_async_remote_copy(
                src_ref=ref, dst_ref=ref, send_sem=ws_send.at[kind], recv_sem=ws_recv.at[kind],
                device_id=to, device_id_type=MESH)

        head = i == 0

        @pl.when(head & (k == 0))
        def _():
            wall[_lin(me)] = wsh_ref[...]
            wcopy(0, me, sibling).start()
            for j, f in enumerate(_CHIPS):
                wcopy(1 + j, me, _flip(me, f)).start()
            gather.start()

        out_w = pltpu.make_async_copy(wall, wall_hbm, out_sems.at[0])
        out_h = pltpu.make_async_copy(hc, h1_hbm, out_sems.at[1])

        @pl.when(head & (k == 1))
        def _():
            wcopy(0, sibling, me).wait_recv()
            out_h.start()

        for j, f in enumerate(_CHIPS):
            @pl.when(head & (k == 2 + j))
            def _(j=j, f=f):
                wcopy(1 + j, _flip(me, f), me).wait_recv()
                wcopy(4 + j, _flip(me, f), sibling).start()

            @pl.when(head & (k == 5 + j))
            def _(j=j, f=f):
                wcopy(4 + j, _flip(me, f | 1), me).wait_recv()
                if 5 + j == NDEV - 1:
                    out_w.start()

        pl.when(head & (k == 5))(gather.forward)

        base = pl.multiple_of(i * tm, tm)

        @pl.when(k == 0)
        def _():
            g = g_ref[...]

            def norm(r0):
                xx = x_ref[pl.ds(r0, RC), :]
                r = lax.rsqrt(jnp.mean(xx * xx, axis=-1, keepdims=True) + EPS)
                hc[pl.ds(base + r0, RC), :] = ((xx * r) * g).astype(BF)
            _rows(tm, norm, unroll=4)

        blk = order_ref[k]
        wk = wall.at[blk]
        step = k * nt + i
        slot = lax.rem(step, 2)
        subs = [(r0, min(TM_MIX, tm - r0)) for r0 in range(0, tm, TM_MIX)]

        def out_copy(q, r0, rs):
            return pltpu.make_async_copy(
                pbuf.at[slot, pl.ds(r0, rs)],
                proj_hbm.at[pl.ds(base + r0, rs), pl.ds(pl.multiple_of(blk * wb, 128), wb)], psems.at[slot, q])

        @pl.when(step >= 2)
        def _():
            for q, (r0, rs) in enumerate(subs):
                out_copy(q, r0, rs).wait()

        for q, (r0, rs) in enumerate(subs):
            pbuf[slot, r0:r0 + rs, :] = _dot(hc[pl.ds(base + r0, rs), :], wk[...])
            out_copy(q, r0, rs).start()

        @pl.when((k == NDEV - 1) & (i == nt - 1))
        def _():
            wcopy(0, me, sibling).wait_send()
            for j, f in enumerate(_CHIPS):
                wcopy(1 + j, me, _flip(me, f)).wait_send()
                wcopy(4 + j, _flip(me, f), sibling).wait_send()
            gather.finish()
            out_w.wait()
            out_h.wait()
            for q, (r0, rs) in enumerate(subs):
                out_copy(q, r0, rs).wait()
                pltpu.make_async_copy(pbuf.at[1 - slot, pl.ds(r0, rs)], proj_hbm.at[pl.ds(r0, rs), pl.ds(0, wb)],
                                      psems.at[1 - slot, q]).wait()

    any_spec = pl.BlockSpec(memory_space=pl.ANY)
    grid_spec = pltpu.PrefetchScalarGridSpec(
        num_scalar_prefetch=1, grid=(NDEV, nt),
        in_specs=[pl.BlockSpec((tm, D), lambda k, i, o: (jnp.where(k == 0, i, nt - 1), 0)),
                  pl.BlockSpec((1, D), lambda k, i, o: (0, 0)),
                  pl.BlockSpec(w_shard.shape, lambda k, i, o: (0, 0))] + [any_spec] * n,
        out_specs=[any_spec, any_spec, any_spec] + [any_spec] * n,
        scratch_shapes=[pltpu.VMEM((NDEV,) + w_shard.shape, BF), pltpu.VMEM((s, D), BF), pltpu.VMEM((2, tm, wb), F32),
                        pltpu.SemaphoreType.DMA((7,)), pltpu.SemaphoreType.DMA((7,))] + _gather_sems(n)
        + [pltpu.SemaphoreType.DMA((2,)), pltpu.SemaphoreType.DMA((2, -(-tm // TM_MIX)))])
    outs = pl.pallas_call(
        body, name="proj_gather", grid_spec=grid_spec,
        out_shape=[jax.ShapeDtypeStruct((s, NDEV * wb), F32), jax.ShapeDtypeStruct((s, D), BF),
                   jax.ShapeDtypeStruct((NDEV,) + w_shard.shape, BF)]
        + [jax.ShapeDtypeStruct((NDEV,) + e.shape, e.dtype) for e in extras],
        compiler_params=_params(("arbitrary", "arbitrary")),
    )(order, x, gmix, w_shard, *extras)
    return outs[0], outs[1], outs[2], outs[3:]


def _conv_tile(rx, prev8, cw_ref, cb):
    xc = cb + cw_ref[3:4, :] * rx
    for j in (1, 2, 3):
        xc = xc + cw_ref[3 - j:4 - j, :] * _shift_back(prev8, rx, j)
    return xc


def _gate_tile(xcb, wax_ref, ba, bx, sp, q):
    cs = slice(q * QW, (q + 1) * QW)
    z = _dot(xcb[:, cs], wax_ref[q])
    r = _sigmoid(z[:, 0:QW] + ba[:, cs])
    ig = _sigmoid(z[:, QW:2 * QW] + bx[:, cs])
    return r, ig, (-RG_C * r) * sp[:, cs]


def _scan_fwd(a_s, b_s, out_ref, h0, n):
    rowi = lax.broadcasted_iota(jnp.int32, (8, D), 0)

    def block(t, h):
        rows = pl.ds(t * 8, 8)
        a = a_s[rows, :]
        b = b_s[rows, :]
        for d in (1, 2, 4):
            m = rowi >= d
            b = jnp.where(m, a * pltpu.roll(b, d, 0) + b, b)
            a = jnp.where(m, a * pltpu.roll(a, d, 0), a)
        hb = b + a * h
        out_ref[rows, :] = hb
        return hb[7:8, :]
    h = h0
    for t in range(n // 8):
        h = block(t, h)
    return h


def _scan_bwd(a_s, lm_s, c0, n):
    rowi = lax.broadcasted_iota(jnp.int32, (8, D), 0)
    nblk = n // 8

    def block(k, cin):
        rows = pl.ds((nblk - 1 - k) * 8, 8)
        a = a_s[rows, :]
        dh = lm_s[rows, :]
        b = a * dh
        for d in (1, 2, 4):
            m = rowi < 8 - d
            b = jnp.where(m, a * pltpu.roll(b, 8 - d, 0) + b, b)
            a = jnp.where(m, a * pltpu.roll(a, 8 - d, 0), a)
        mu = b + a * cin
        lm_s[rows, :] = dh + jnp.where(rowi < 7, pltpu.roll(mu, 7, 0), cin)
        return mu[0:1, :]
    c = c0
    for k in range(nblk):
        c = block(k, c)
    return c


def _mixer_fwd_call(proj, x, cw, cb, ba, bx, lam, lng, lnb, wax, wtr, bias, wpa, wpb, wo, shards):
    s = x.shape[0]
    tm = min(TM_MIX, s)
    nt = s // tm
    pw = proj.shape[1]
    n = len(shards)

    def body(proj_ref, x_ref, cw_ref, cb_ref, ba_ref, bx_ref, lam_ref, lng_ref, lnb_ref, wax_ref, wtr_ref,
             bias_ref, wpa_ref, wpb_ref, wo_ref, *rest):
        sh_in = rest[:n]
        h_ref, pa_ref, pb_ref, x1_ref, xc_ref, r_ref, ig_ref, a_s, m_ref, ya_ref, yb_ref, mg_ref = rest[n:n + 12]
        sh_out = rest[n + 12:2 * n + 12]
        prev_s, b_s, hc_s, vn_s, mx_s, g_send, g_recv, g_local = rest[2 * n + 12:]
        i = pl.program_id(0)
        gather = _Gather(sh_in, sh_out, g_send, g_recv, g_local)

        @pl.when(i == 0)
        def _():
            gather.start()
            prev_s[...] = jnp.zeros((8, D), F32)
            hc_s[...] = jnp.zeros((8, D), F32)

        pl.when(i == nt // 2)(gather.forward)

        rx = proj_ref[:, 0:D]
        xc = _conv_tile(rx, prev_s[...], cw_ref, cb_ref[...])
        prev_s[...] = rx[tm - 8:tm, :]
        xc_ref[...] = xc.astype(BF)
        xcb = xc.astype(BF)
        sp = _softplus(-lam_ref[...])
        ba = ba_ref[...]
        bx = bx_ref[...]
        for q in range(NQ):
            cs = slice(q * QW, (q + 1) * QW)
            r, ig, la = _gate_tile(xcb, wax_ref, ba, bx, sp, q)
            mq = jnp.sqrt(_neg_expm1(2.0 * la))
            r_ref[:, cs] = r.astype(BF)
            ig_ref[:, cs] = ig.astype(BF)
            m_ref[:, cs] = mq.astype(BF)
            a_s[:, cs] = jnp.exp(la)
            b_s[:, cs] = mq * (ig * xc[:, cs])

        gv = _gelu(proj_ref[:, 3 * D:4 * D])
        dv = gv - jnp.mean(gv, axis=-1, keepdims=True)
        var = jnp.mean(dv * dv, axis=-1, keepdims=True)
        vn_s[...] = ((dv * lax.rsqrt(var + EPS)) * lng_ref[...] + lnb_ref[...]).astype(BF)
        nc = tm // CHUNK
        for c in range(nc):
            rs = slice(c * CHUNK, (c + 1) * CHUNK)
            for g in range(NGRP):
                cs = slice(g * GW, (g + 1) * GW)
                mx_s[rs, cs] = _dot(wtr_ref[g], vn_s[rs, cs])
        mixed = mx_s[...] + jnp.concatenate([bias_ref[...]] * nc, axis=0)
        yb = (_gelu(proj_ref[:, 2 * D:3 * D]) * mixed).astype(BF)
        yb_ref[...] = yb
        pb = _dot(yb, wpb_ref[...])
        pb_ref[...] = pb.astype(BF)
        mx_s[...] = pb

        hc_s[0:1, :] = _scan_fwd(a_s, b_s, h_ref, hc_s[0:1, :], tm)

        ya = (_gelu(proj_ref[:, D:2 * D]) * h_ref[...]).astype(BF)
        ya_ref[...] = ya
        pa = _dot(ya, wpa_ref[...])
        pa_ref[...] = pa.astype(BF)
        mg = (_sigmoid(proj_ref[:, 4 * D:5 * D]) * pa + _sigmoid(proj_ref[:, 5 * D:6 * D]) * mx_s[...]).astype(BF)
        mg_ref[...] = mg
        x1_ref[...] = x_ref[...] + _dot(mg, wo_ref[...])

        pl.when(i == nt - 1)(gather.finish)

    tile = lambda w: pl.BlockSpec((tm, w), lambda i: (i, 0))
    vec = _const_spec((1, D))
    any_spec = pl.BlockSpec(memory_space=pl.ANY)
    outs = pl.pallas_call(
        body, name="mixer_fwd", grid=(nt,),
        in_specs=[tile(pw), tile(D), _const_spec((4, D)), vec, vec, vec, vec, vec, vec,
                  _const_spec(wax.shape), _const_spec(wtr.shape), _const_spec(bias.shape),
                  _const_spec((D, D)), _const_spec((D, D)), _const_spec((D, D))] + [any_spec] * n,
        out_specs=[tile(D)] * 12 + [any_spec] * n,
        out_shape=[jax.ShapeDtypeStruct((s, D), dt) for dt in (F32, BF, BF, F32, BF, BF, BF, F32, BF, BF, BF, BF)]
        + [jax.ShapeDtypeStruct((NDEV,) + e.shape, e.dtype) for e in shards],
        scratch_shapes=[pltpu.VMEM((8, D), F32), pltpu.VMEM((tm, D), F32),
                        pltpu.VMEM((8, D), F32), pltpu.VMEM((tm, D), BF), pltpu.VMEM((tm, D), F32)]
        + _gather_sems(n),
        compiler_params=_params(("arbitrary",)),
    )(proj, x, cw, cb, ba, bx, lam, lng, lnb, wax, wtr, bias, wpa, wpb, wo, *shards)
    return outs[:12], outs[12:]


def _ffn_call(x1, tgt, gffn, gfin, wgu, wdn):
    s = x1.shape[0]
    tm = min(TM_FFN, s)
    nt = s // tm
    nh = wdn.shape[0]
    fb = wgu.shape[2]

    def body(x1_ref, tgt_ref, gffn_ref, gfin_ref, wgu_ref, wdn_ref,
             dx1_ref, dx1b_ref, h2_ref, act_ref, dgu_ref, dx2b_ref, acc_ref,
             g_s, u_s, dx2_s, accs):
        i = pl.program_id(0)

        @pl.when(i == 0)
        def _():
            accs[...] = jnp.zeros(accs.shape, F32)

        gffn = gffn_ref[...]
        gfin = gfin_ref[...]

        x1 = x1_ref[...]
        r2 = lax.rsqrt(jnp.mean(x1 * x1, axis=-1, keepdims=True) + EPS)
        xh2 = x1 * r2
        h2 = (xh2 * gffn).astype(BF)
        h2_ref[...] = h2

        for k in range(nh):
            g = _dot(h2, wgu_ref[k])
            u = _dot(h2, wgu_ref[k + nh])
            g_s[k] = g
            u_s[k] = u
            act_ref[k] = ((g * _sigmoid(g)) * u).astype(BF)

        x2 = x1
        for k in range(nh):
            x2 = x2 + _dot(act_ref[k], wdn_ref[k])

        r3 = lax.rsqrt(jnp.mean(x2 * x2, axis=-1, keepdims=True) + EPS)
        xh = x2 * r3
        err = xh * gfin - tgt_ref[...]
        accs[2] += _fold(err * err)
        dy = err * (1.0 / D)
        accs[1] += _fold(dy * xh)
        dxh = dy * gfin
        dx2 = r3 * (dxh - xh * jnp.mean(dxh * xh, axis=-1, keepdims=True))
        dx2_s[...] = dx2
        dx2b = dx2.astype(BF)
        dx2b_ref[...] = dx2b

        for k in range(nh):
            da = _dot_nt(dx2b, wdn_ref[k])
            g = g_s[k]
            sg = _sigmoid(g)
            dgu_ref[k] = ((da * u_s[k]) * (sg * (1.0 + g * (1.0 - sg)))).astype(BF)
            dgu_ref[k + nh] = (da * (g * sg)).astype(BF)

        dh2 = _dot_nt(dgu_ref[0], wgu_ref[0])
        for k in range(1, 2 * nh):
            dh2 = dh2 + _dot_nt(dgu_ref[k], wgu_ref[k])

        accs[0] += _fold(dh2 * xh2)
        dxh = dh2 * gffn
        dx1 = dx2_s[...] + r2 * (dxh - xh2 * jnp.mean(dxh * xh2, axis=-1, keepdims=True))
        dx1_ref[...] = dx1
        dx1b_ref[...] = dx1.astype(BF)

        @pl.when(i == nt - 1)
        def _():
            acc_ref[...] = jnp.zeros((8, D), F32)
            for j in range(3):
                acc_ref[j:j + 1, :] = jnp.sum(accs[j], axis=0, keepdims=True)

    tile = lambda w: pl.BlockSpec((tm, w), lambda i: (i, 0))
    vec = _const_spec((1, D))
    return pl.pallas_call(
        body, name="ffn", grid=(nt,),
        in_specs=[tile(D), tile(D), vec, vec, _const_spec(wgu.shape), _const_spec(wdn.shape)],
        out_specs=[tile(D), tile(D), tile(D),
                   pl.BlockSpec((nh, tm, fb), lambda i: (0, i, 0)),
                   pl.BlockSpec((2 * nh, tm, fb), lambda i: (0, i, 0)),
                   tile(D), pl.BlockSpec((8, D), lambda i: (0, 0))],
        out_shape=[jax.ShapeDtypeStruct((s, D), F32), jax.ShapeDtypeStruct((s, D), BF),
                   jax.ShapeDtypeStruct((s, D), BF), jax.ShapeDtypeStruct((nh, s, fb), BF),
                   jax.ShapeDtypeStruct((2 * nh, s, fb), BF), jax.ShapeDtypeStruct((s, D), BF),
                   jax.ShapeDtypeStruct((8, D), F32)],
        scratch_shapes=[pltpu.VMEM((nh, tm, fb), F32), pltpu.VMEM((nh, tm, fb), F32), pltpu.VMEM((tm, D), F32),
                        pltpu.VMEM((3, RC, D), F32)],
        compiler_params=_params(("arbitrary",)),
    )(x1, tgt, gffn, gfin, wgu, wdn)


def _mixer_bwd_pre_call(dx1b, proj, pa, pb, lng, lnb, wtr, wtrt, bias, wpa, wpb, wo, parts):
    s = dx1b.shape[0]
    tm = min(TM_MIX, s)
    nt = s // tm
    pw = proj.shape[1]
    n = len(parts)

    def body(dx1b_ref, uv_ref, gg_ref, pa_ref, pb_ref, lng_ref, lnb_ref, wtr_ref, wtrt_ref, bias_ref,
             wpa_ref, wpb_ref, wo_ref, *rest):
        ex_in = rest[:n]
        dproj_ref, dya_ref, dpa_ref, dpb_ref, vec_ref, dws_ref, dbs_ref = rest[n:n + 7]
        ex_out = rest[n + 7:2 * n + 7]
        vn_s, mx_s, dmx_s, dvn_s, accs, dbs_s, e_send, e_recv, e_local = rest[2 * n + 7:]
        i = pl.program_id(0)
        exchange = _Exchange(ex_in, ex_out, e_send, e_recv, e_local)

        @pl.when(i == 0)
        def _():
            exchange.start()
            accs[...] = jnp.zeros(accs.shape, F32)
            dbs_s[...] = jnp.zeros(dbs_s.shape, F32)
            dws_ref[...] = jnp.zeros(dws_ref.shape, F32)

        dm = _dot_nt(dx1b_ref[...], wo_ref[...])
        sa = _sigmoid(gg_ref[:, 0:D])
        sb = _sigmoid(gg_ref[:, D:2 * D])
        dpa = dm * sa
        dpb = dm * sb
        dpab = dpa.astype(BF)
        dpbb = dpb.astype(BF)
        dpa_ref[...] = dpab
        dpb_ref[...] = dpbb
        dproj_ref[:, 2 * D:3 * D] = ((dpa * pa_ref[...].astype(F32)) * (1.0 - sa)).astype(BF)
        dproj_ref[:, 3 * D:4 * D] = ((dpb * pb_ref[...].astype(F32)) * (1.0 - sb)).astype(BF)

        dya_ref[...] = _dot_nt(dpab, wpa_ref[...])
        dyb = _dot_nt(dpbb, wpb_ref[...])

        lng = lng_ref[...]
        gv, dgelu_v = _gelu_grad(uv_ref[:, D:2 * D])
        dv = gv - jnp.mean(gv, axis=-1, keepdims=True)
        rstd = lax.rsqrt(jnp.mean(dv * dv, axis=-1, keepdims=True) + EPS)
        xh = dv * rstd
        vn_s[...] = (xh * lng + lnb_ref[...]).astype(BF)

        nc = tm // CHUNK
        for c in range(nc):
            rs = slice(c * CHUNK, (c + 1) * CHUNK)
            for g in range(NGRP):
                cs = slice(g * GW, (g + 1) * GW)
                mx_s[rs, cs] = _dot(wtr_ref[g], vn_s[rs, cs])

        gu, dgelu_u = _gelu_grad(uv_ref[:, 0:D])
        mixed = mx_s[...] + jnp.concatenate([bias_ref[...]] * nc, axis=0)
        dproj_ref[:, 0:D] = ((dyb * mixed) * dgelu_u).astype(BF)
        dmx = dyb * gu
        dmx_s[...] = dmx.astype(BF)
        dbs_s[...] += jnp.sum(dmx.reshape(nc, CHUNK, D), axis=0)

        for c in range(nc):
            rs = slice(c * CHUNK, (c + 1) * CHUNK)
            for g in range(NGRP):
                cs = slice(g * GW, (g + 1) * GW)
                dvn_s[rs, cs] = _dot(wtrt_ref[g], dmx_s[rs, cs])
                dws_ref[g] += _dot_nt(dmx_s[rs, cs], vn_s[rs, cs])

        dvn = dvn_s[...]
        accs[0] += _fold(dvn * xh)
        accs[1] += _fold(dvn)
        dxh = dvn * lng
        m1 = jnp.mean(dxh, axis=-1, keepdims=True)
        m2 = jnp.mean(dxh * xh, axis=-1, keepdims=True)
        dproj_ref[:, D:2 * D] = ((rstd * (dxh - m1 - xh * m2)) * dgelu_v).astype(BF)

        @pl.when(i == nt - 1)
        def _():
            vec_ref[...] = jnp.zeros((8, D), F32)
            for j in range(2):
                vec_ref[j:j + 1, :] = jnp.sum(accs[j], axis=0, keepdims=True)
            row = lax.broadcasted_iota(jnp.int32, (CHUNK, CHUNK), 0)
            col = lax.broadcasted_iota(jnp.int32, (CHUNK, CHUNK), 1)
            for g in range(NGRP):
                dws_ref[g] = jnp.where(row >= col, dws_ref[g], 0.0)
                gs = jnp.sum(dbs_s[:, g * GW:(g + 1) * GW], axis=1, keepdims=True)
                dbs_ref[:, g * GW:(g + 1) * GW] = jnp.broadcast_to(gs, (CHUNK, GW))
            exchange.finish()

    tile = lambda w: pl.BlockSpec((tm, w), lambda i: (i, 0))
    vec = _const_spec((1, D))
    any_spec = pl.BlockSpec(memory_space=pl.ANY)
    outs = pl.pallas_call(
        body, name="mixer_bwd_pre", grid=(nt,),
        in_specs=[tile(D), pl.BlockSpec((tm, 2 * D), lambda i: (i, 1)), pl.BlockSpec((tm, 2 * D), lambda i: (i, 2)),
                  tile(D), tile(D), vec, vec, _const_spec(wtr.shape), _const_spec(wtrt.shape),
                  _const_spec(bias.shape), _const_spec((D, D)), _const_spec((D, D)), _const_spec((D, D))]
        + [any_spec] * n,
        out_specs=[tile(4 * D), tile(D), tile(D), tile(D), pl.BlockSpec((8, D), lambda i: (0, 0)),
                   pl.BlockSpec((NGRP, CHUNK, CHUNK), lambda i: (0, 0, 0)),
                   pl.BlockSpec((CHUNK, D), lambda i: (0, 0))] + [any_spec] * n,
        out_shape=[jax.ShapeDtypeStruct((s, 4 * D), BF), jax.ShapeDtypeStruct((s, D), F32),
                   jax.ShapeDtypeStruct((s, D), BF), jax.ShapeDtypeStruct((s, D), BF),
                   jax.ShapeDtypeStruct((8, D), F32), jax.ShapeDtypeStruct((NGRP, CHUNK, CHUNK), F32),
                   jax.ShapeDtypeStruct((CHUNK, D), F32)]
        + [jax.ShapeDtypeStruct(p.shape, p.dtype) for p in parts],
        scratch_shapes=[pltpu.VMEM((tm, D), BF), pltpu.VMEM((tm, D), F32), pltpu.VMEM((tm, D), BF),
                        pltpu.VMEM((tm, D), F32), pltpu.VMEM((2, RC, D), F32), pltpu.VMEM((CHUNK, D), F32)]
        + _gather_sems(n),
        compiler_params=_params(("arbitrary",)),
    )(dx1b, proj, proj, pa, pb, lng, lnb, wtr, wtrt, bias, wpa, wpb, wo, *parts)
    return outs[:7], outs[7:]


def _mixer_bwd_seq_call(dprojb, dya, proj, h, x, dx1, gates, gmix, w_in_all, cw, lam, wax, parts):
    s = dya.shape[0]
    tm = min(TM_MIX, s)
    nt = s // tm
    tb = tm // 8
    n = len(parts)
    nb, _, wb = w_in_all.shape

    def body(dpb_ref, dya_ref, xg_ref, xh8_ref, h_ref, hh8_ref, x_ref, dx1_ref, xc_s, r_s, ig_s, a_s, m_s,
             gmix_ref, win_ref, cw_ref, lam_ref, wax_ref, *rest):
        ex_in = rest[:n]
        dpa_ref, dx_ref, vec_ref, dwax_ref = rest[n:n + 4]
        ex_out = rest[n + 4:2 * n + 4]
        lm_s, dh_s, dxc_s, c_s, accs, e_send, e_recv, e_local = rest[2 * n + 4:]
        i = pl.program_id(0)
        exchange = _Exchange(ex_in, ex_out, e_send, e_recv, e_local)

        @pl.when(i == 0)
        def _():
            exchange.start()
            accs[...] = jnp.zeros(accs.shape, F32)
            dwax_ref[...] = jnp.zeros(dwax_ref.shape, F32)
            c_s[...] = jnp.zeros((8, D), F32)
            dxc_s[tm:tm + 8, :] = jnp.zeros((8, D), F32)

        first_tile = i == nt - 1
        prev8 = jnp.where(first_tile, 0.0, xh8_ref[...])
        hprev8 = jnp.where(first_tile, 0.0, hh8_ref[...])
        lamv = lam_ref[...]
        sp = _softplus(-lamv)
        hv = h_ref[...]
        g, dg = _gelu_grad(xg_ref[:, D:2 * D])
        dya = dya_ref[...]
        lm_s[...] = dya * g
        drg = ((dya * hv) * dg).astype(BF)
        dpa_ref[:, D:2 * D] = drg
        dpa_ref[:, 2 * D:6 * D] = dpb_ref[...]

        def dpb_block(k):
            return _dot_nt(dpb_ref[:, k * wb - 2 * D:(k + 1) * wb - 2 * D], win_ref[k])
        dh = (_dot_nt(drg[:, 0:2 * wb - D], win_ref[1, :, D - wb:wb])
              + _dot_nt(drg[:, 2 * wb - D:D], win_ref[2, :, 0:2 * D - 2 * wb])
              + _dot_nt(dpb_ref[:, 0:3 * wb - 2 * D], win_ref[2, :, 2 * D - 2 * wb:wb]))
        for k in range(3, 6):
            dh = dh + dpb_block(k)
        dh_s[...] = dh

        c_s[0:1, :] = _scan_bwd(a_s, lm_s, c_s[0:1, :], tm)

        hprev = _shift_back(hprev8, h_ref[...], 1)
        for q in range(NQ):
            cs = slice(q * QW, (q + 1) * QW)
            r = r_s[:, cs].astype(F32)
            ig = ig_s[:, cs].astype(F32)
            a = a_s[:, cs]
            m = m_s[:, cs].astype(F32)
            lm = lm_s[:, cs]
            xq = xc_s[:, cs].astype(F32)
            dixc = lm * m
            dla = (lm * hprev[:, cs]) * a - ((lm * (ig * xq)) * (a * a)) / m
            accs[3, :, cs] += _fold(dla * r)
            dza = (dla * (-RG_C * sp[:, cs])) * (r * (1.0 - r))
            dzx = (dixc * xq) * (ig * (1.0 - ig))
            accs[1, :, cs] += _fold(dza)
            accs[2, :, cs] += _fold(dzx)
            dz = jnp.concatenate([dza, dzx], axis=1).astype(BF)
            dxc_s[0:tm, cs] = dixc * ig + _dot_nt(dz, wax_ref[q])
            dwax_ref[q] += _dot_tn(xc_s[:, cs], dz)

        cur = dxc_s[0:tm, :]
        nx = dxc_s[tm:tm + 8, :]
        drx = cw_ref[3:4, :] * cur
        for j in (1, 2, 3):
            drx = drx + cw_ref[3 - j:4 - j, :] * _shift_fwd(cur, nx, j)
        accs[0] += _fold(cur)
        rx = xg_ref[:, 0:D]
        accs[7] += _fold(cur * rx)
        for j in (1, 2, 3):
            accs[7 - j] += _fold(cur * _shift_back(prev8, rx, j))
        dxc_s[tm:tm + 8, :] = cur[0:8, :]

        drxb = drx.astype(BF)
        dpa_ref[:, 0:D] = drxb
        dh = dh_s[...] + _dot_nt(drxb[:, 0:wb], win_ref[0]) + _dot_nt(drxb[:, wb:D], win_ref[1, :, 0:D - wb])
        for k in range(6, nb):
            dh = dh + dpb_block(k)
        xx = x_ref[...]
        rn = lax.rsqrt(jnp.mean(xx * xx, axis=-1, keepdims=True) + EPS)
        xhn = xx * rn
        accs[8] += _fold(dh * xhn)
        dxh = dh * gmix_ref[...]
        dx_ref[...] = dx1_ref[...] + rn * (dxh - xhn * jnp.mean(dxh * xhn, axis=-1, keepdims=True))

        @pl.when(i == nt - 1)
        def _():
            vec_ref[...] = jnp.zeros((16, D), F32)
            for j in range(9):
                vec_ref[j:j + 1, :] = jnp.sum(accs[j], axis=0, keepdims=True)
            vec_ref[3:4, :] = vec_ref[3:4, :] * (RG_C * _sigmoid(-lamv))
            exchange.finish()

    rev = lambda w: pl.BlockSpec((tm, w), lambda i: (nt - 1 - i, 0))
    halo = pl.BlockSpec((8, D), lambda i: (jnp.maximum((nt - 1 - i) * tb - 1, 0), 0))
    vec = _const_spec((1, D))
    any_spec = pl.BlockSpec(memory_space=pl.ANY)
    outs = pl.pallas_call(
        body, name="mixer_bwd_seq", grid=(nt,),
        in_specs=[rev(4 * D), rev(D), rev(2 * D), halo, rev(D), halo] + [rev(D)] * 7
        + [vec, _const_spec(w_in_all.shape), _const_spec((4, D)), vec, _const_spec(wax.shape)] + [any_spec] * n,
        out_specs=[rev(6 * D), rev(D), pl.BlockSpec((16, D), lambda i: (0, 0)),
                   pl.BlockSpec((NQ, QW, 2 * QW), lambda i: (0, 0, 0))] + [any_spec] * n,
        out_shape=[jax.ShapeDtypeStruct((s, 6 * D), BF), jax.ShapeDtypeStruct((s, D), F32),
                   jax.ShapeDtypeStruct((16, D), F32), jax.ShapeDtypeStruct((NQ, QW, 2 * QW), F32)]
        + [jax.ShapeDtypeStruct(p.shape, p.dtype) for p in parts],
        scratch_shapes=[pltpu.VMEM((tm, D), F32), pltpu.VMEM((tm, D), F32),
                        pltpu.VMEM((tm + 8, D), F32), pltpu.VMEM((8, D), F32), pltpu.VMEM((9, RC, D), F32)]
        + _gather_sems(n),
        compiler_params=_params(("arbitrary",)),
    )(dprojb, dya, proj, proj, h, h, x, dx1, *gates, gmix, w_in_all, cw, lam, wax, *parts)
    return outs[:4], outs[4:]


def _device_of(d):
    return (d // 4, lax.rem(d // 2, 2), lax.rem(d, 2))


_DW_PLAN = (("C", 1), ("A", 1), ("C", 2), ("A", 2), ("C", 3), ("A", 3), ("S", 0), ("O", 0))
_DW_FLIPS = tuple({"C": 2 * j + 1, "A": 2 * j, "S": 1, "O": 0}[kind] for kind, j in _DW_PLAN)


def _dw_exchange_call(name, order, a, b, a_spec, b_spec, k1, n1, s, small):
    ts = min(TS_DW, s)
    ns = s // ts
    nstep = len(_DW_PLAN)

    def slab(i, order_ref):
        return order_ref[i]

    def body(order_ref, a_ref, b_ref, g_ref, own_ref, recv_ref, gsum_ref, acc, sbuf, stage, send_sems, recv_sems,
             st_send, st_recv, local_sem, rbuf, gacc, send1, recv1, send2, recv2):
        i = pl.program_id(0)
        j = pl.program_id(1)
        me = _me()
        mi = _lin(me)
        sibling = _flip(me, 1)
        allreduce = _SmallAllReduce(g_ref, gacc, rbuf, send1, recv1, send2, recv2)
        pl.when((i == 0) & (j == 0))(allreduce.scatter)
        pl.when((i == nstep // 2) & (j == 0))(allreduce.reduce)
        p = _dot_tn(a_ref[...], b_ref[...])

        @pl.when(j == 0)
        def _():
            acc[...] = p

        @pl.when(j > 0)
        def _():
            acc[...] += p

        def send(step):
            kind, jj = _DW_PLAN[step]
            src = sbuf.at[step % 2]
            if kind == "C":
                return pltpu.make_async_remote_copy(
                    src_ref=src, dst_ref=stage.at[jj - 1], send_sem=st_send.at[jj - 1], recv_sem=st_recv.at[jj - 1],
                    device_id=sibling, device_id_type=MESH)
            to = sibling if kind == "S" else _flip(me, 2 * jj)
            return pltpu.make_async_remote_copy(
                src_ref=src, dst_ref=recv_ref.at[mi], send_sem=send_sems.at[jj], recv_sem=recv_sems.at[jj],
                device_id=to, device_id_type=MESH)

        def arrival(jj):
            frm = sibling if jj == 0 else _flip(me, 2 * jj)
            return pltpu.make_async_remote_copy(
                src_ref=sbuf.at[0], dst_ref=recv_ref.at[_lin(frm)], send_sem=send_sems.at[jj],
                recv_sem=recv_sems.at[jj], device_id=frm, device_id_type=MESH)

        def staged(jj):
            return pltpu.make_async_remote_copy(
                src_ref=sbuf.at[0], dst_ref=stage.at[jj - 1], send_sem=st_send.at[jj - 1], recv_sem=st_recv.at[jj - 1],
                device_id=sibling, device_id_type=MESH)

        for step, (kind, jj) in enumerate(_DW_PLAN):
            @pl.when((i == step) & (j == ns - 1))
            def _(step=step, kind=kind, jj=jj):
                if step >= 2:
                    send(step - 2).wait_send()
                if kind == "A":
                    staged(jj).wait_recv()
                    sbuf[step % 2] = (acc[...] + stage[jj - 1].astype(F32)).astype(BF)
                else:
                    sbuf[step % 2] = acc[...].astype(BF)
                if kind != "O":
                    send(step).start()
                else:
                    own_ref[...] = acc[...]
                    mine = pltpu.make_async_copy(sbuf.at[step % 2], recv_ref.at[mi], local_sem)
                    mine.start()
                    send(step - 1).wait_send()
                    for q in range(4):
                        arrival(q).wait_recv()
                    mine.wait()
                    allreduce.finish()
                    gsum_ref[...] = gacc[...]

    vmem = pl.BlockSpec(memory_space=pltpu.VMEM)
    grid_spec = pltpu.PrefetchScalarGridSpec(
        num_scalar_prefetch=1, grid=(nstep, ns),
        in_specs=[a_spec(ts, slab), b_spec(ts, slab), vmem],
        out_specs=[pl.BlockSpec((k1, n1), lambda i, j, o: (0, 0)), pl.BlockSpec(memory_space=pl.ANY), vmem],
        scratch_shapes=[pltpu.VMEM((k1, n1), F32), pltpu.VMEM((2, k1, n1), BF), pltpu.VMEM((3, k1, n1), BF),
                        pltpu.SemaphoreType.DMA((4,)), pltpu.SemaphoreType.DMA((4,)),
                        pltpu.SemaphoreType.DMA((3,)), pltpu.SemaphoreType.DMA((3,)),
                        pltpu.SemaphoreType.DMA(()), pltpu.VMEM((NDEV, SMALL_PER, D), F32),
                        pltpu.VMEM((SMALL_ROWS, D), F32)]
        + [pltpu.SemaphoreType.DMA((NDEV - 1,))] * 4)
    return pl.pallas_call(
        body, name=name, grid_spec=grid_spec,
        out_shape=[jax.ShapeDtypeStruct((k1, n1), F32), jax.ShapeDtypeStruct((NDEV, k1, n1), BF),
                   jax.ShapeDtypeStruct((SMALL_ROWS, D), F32)],
        compiler_params=_params(("arbitrary", "arbitrary")),
    )(order, a, b, small)


def _dw_plain_call(name, me, a, b, a_spec, b_spec, split, k1, n1, s):
    nb = NDEV // split
    r = k1 // split
    ts = min(TS_DW, s)
    ns = s // ts

    def slab(i, me_ref):
        return i

    def body(me_ref, a_ref, b_ref, own_ref, part_ref, acc):
        i = pl.program_id(0)
        j = pl.program_id(1)
        p = _dot_tn(a_ref[...], b_ref[...])

        @pl.when(j == 0)
        def _():
            acc[...] = p

        @pl.when(j > 0)
        def _():
            acc[...] += p

        @pl.when(j == ns - 1)
        def _():
            part_ref[...] = acc[...].astype(BF)

            @pl.when(i == me_ref[0] // split)
            def _():
                off = pl.multiple_of(lax.rem(me_ref[0], split) * r, RC)
                own_ref[...] = acc[pl.ds(off, r), :]

    grid_spec = pltpu.PrefetchScalarGridSpec(
        num_scalar_prefetch=1, grid=(nb, ns),
        in_specs=[a_spec(ts, slab), b_spec(ts, slab)],
        out_specs=[pl.BlockSpec((r, n1), lambda i, j, me_ref: (0, 0)),
                   pl.BlockSpec((None, k1, n1), lambda i, j, me_ref: (i, 0, 0))],
        scratch_shapes=[pltpu.VMEM((k1, n1), F32)])
    own, part = pl.pallas_call(
        body, name=name, grid_spec=grid_spec,
        out_shape=[jax.ShapeDtypeStruct((r, n1), F32), jax.ShapeDtypeStruct((nb, k1, n1), BF)],
        compiler_params=_params(("arbitrary", "arbitrary")),
    )(me, a, b)
    return own, part.reshape(NDEV, r, n1)


def _rows2d(w):
    return lambda ts, slab: pl.BlockSpec((ts, w), lambda i, j, me_ref: (j, 0))


def _cols2d(w):
    return lambda ts, slab: pl.BlockSpec((ts, w), lambda i, j, me_ref: (j, slab(i, me_ref)))


def _blk3d(w):
    return lambda ts, slab: pl.BlockSpec((None, ts, w), lambda i, j, me_ref: (slab(i, me_ref), j, 0))


_BC1 = 1.0 - ADAM_B1 ** ADAM_STEP
_BC2 = 1.0 - ADAM_B2 ** ADAM_STEP


def _adamw_math(w, g, m, v):
    m = ADAM_B1 * m + (1.0 - ADAM_B1) * g
    v = ADAM_B2 * v + (1.0 - ADAM_B2) * (g * g)
    m_hat = m / _BC1
    v_hat = v / _BC2
    delta = -ADAM_LR * (m_hat / (jnp.sqrt(v_hat) + ADAM_EPS) + ADAM_WD * w)
    return delta, m, v


def _row_tile(r):
    for t in (256, 176, 128, 64, 32, 16, 8):
        if r % t == 0:
            return t
    return r


def _reduce_adamw_call(name, sel, own, recv, w, m, v):
    r, c = own.shape
    tr = _row_tile(r)

    def body(sel_ref, own_ref, recv_ref, w_ref, m_ref, v_ref, g_ref, d_ref, nm_ref, nv_ref):
        g = jnp.zeros((tr, c), F32)
        for sdev in range(NDEV):
            part = jnp.where(sel_ref[sdev] == 1, recv_ref[sdev].astype(F32), 0.0)
            g = g + jnp.where(sel_ref[sdev] == 2, own_ref[...], part)
        g_ref[...] = g
        d_ref[...], nm_ref[...], nv_ref[...] = _adamw_math(w_ref[...], g, m_ref[...], v_ref[...])

    tile = pl.BlockSpec((tr, c), lambda i, me_ref: (i, 0))
    grid_spec = pltpu.PrefetchScalarGridSpec(
        num_scalar_prefetch=1, grid=(r // tr,),
        in_specs=[tile, pl.BlockSpec((NDEV, tr, c), lambda i, me_ref: (0, i, 0)), tile, tile, tile],
        out_specs=[tile] * 4)
    return pl.pallas_call(
        body, name=name, grid_spec=grid_spec,
        out_shape=[jax.ShapeDtypeStruct((r, c), F32)] * 4,
        compiler_params=_params(("parallel",)),
    )(sel, own, recv, w, m, v)


def _adamw_call(name, w, g, m, v):
    r, c = w.shape
    tr = _row_tile(r)

    def body(w_ref, g_ref, m_ref, v_ref, d_ref, nm_ref, nv_ref):
        d_ref[...], nm_ref[...], nv_ref[...] = _adamw_math(w_ref[...], g_ref[...], m_ref[...], v_ref[...])

    tile = pl.BlockSpec((tr, c), lambda i: (i, 0))
    return pl.pallas_call(
        body, name=name, grid=(r // tr,), in_specs=[tile] * 4, out_specs=[tile] * 3,
        out_shape=[jax.ShapeDtypeStruct((r, c), F32)] * 3,
        compiler_params=_params(("parallel",)),
    )(w, g, m, v)


def _me():
    return lax.axis_index("x"), lax.axis_index("y"), lax.axis_index("c")


def _flip(pos, r):
    x, y, c = pos
    return (1 - x if r & 4 else x, 1 - y if r & 2 else y, 1 - c if r & 1 else c)


def _lin(pos):
    return pos[0] * 4 + pos[1] * 2 + pos[2]


class _SmallAllReduce:
    def __init__(self, g_ref, out_ref, rbuf, send1, recv1, send2, recv2):
        self.g, self.out, self.rbuf = g_ref, out_ref, rbuf
        self.sems = (send1, recv1, send2, recv2)
        self.me = _me()
        self.mi = _lin(self.me)

    @staticmethod
    def _rows(d):
        return pl.ds(pl.multiple_of(d * SMALL_PER, 8), SMALL_PER)

    def _scatter(self, r, outgoing):
        peer = _flip(self.me, r)
        src, dst = (_lin(peer), self.mi) if outgoing else (self.mi, _lin(peer))
        return pltpu.make_async_remote_copy(
            src_ref=self.g.at[self._rows(src)], dst_ref=self.rbuf.at[dst],
            send_sem=self.sems[0].at[r - 1], recv_sem=self.sems[1].at[r - 1], device_id=peer, device_id_type=MESH)

    def _spread(self, r, outgoing):
        peer = _flip(self.me, r)
        rows = self._rows(self.mi if outgoing else _lin(peer))
        return pltpu.make_async_remote_copy(
            src_ref=self.out.at[rows], dst_ref=self.out.at[rows],
            send_sem=self.sems[2].at[r - 1], recv_sem=self.sems[3].at[r - 1], device_id=peer, device_id_type=MESH)

    def scatter(self):
        for r in range(1, NDEV):
            self._scatter(r, True).start()
        self.rbuf[self.mi] = self.g[self._rows(self.mi), :]

    def reduce(self):
        for r in range(1, NDEV):
            self._scatter(r, False).wait_recv()
        for r in range(1, NDEV):
            self._scatter(r, True).wait_send()
        tot = self.rbuf[0]
        for d in range(1, NDEV):
            tot = tot + self.rbuf[d]
        self.out[self._rows(self.mi), :] = tot
        for r in range(1, NDEV):
            self._spread(r, True).start()

    def finish(self):
        for r in range(1, NDEV):
            self._spread(r, False).wait_recv()
        for r in range(1, NDEV):
            self._spread(r, True).wait_send()


def _head_blocks(w):
    z = jnp.zeros((64, 64), w.dtype)
    groups = []
    for q in range(NQ):
        rows = [jnp.concatenate([w[4 * q + a] if a == b else z for b in range(4)], axis=1) for a in range(4)]
        groups.append(jnp.concatenate(rows, axis=0))
    return jnp.stack(groups)


def _head_unblocks(g):
    return jnp.stack([g[q, 64 * a:64 * a + 64, 64 * a:64 * a + 64] for q in range(NQ) for a in range(4)])


def _local_step(x, tgt, p, me, order, order_dw):
    s = x.shape[0]
    vec = lambda a: a.reshape(1, D)
    gmix, gffn, gfin = vec(p["norm_mix_g"]), vec(p["norm_ffn_g"]), vec(p["norm_final_g"])
    cb, ba, bx, lam = vec(p["conv_b"]), vec(p["rg_ba"]), vec(p["rg_bx"]), vec(p["rg_lambda"])
    lng, lnb = vec(p["sgu_ln_g"]), vec(p["sgu_ln_b"])
    wax = jnp.concatenate([_head_blocks(p["rg_wa"]), _head_blocks(p["rg_wx"])], axis=2).astype(BF)
    tril = jnp.tril(jnp.ones((CHUNK, CHUNK), bool))
    ws = jnp.where(tril[None], p["sgu_ws"], 0.0)
    wtr = ws.astype(BF)
    wtrt = jnp.swapaxes(ws, 1, 2).astype(BF)
    bias = jnp.repeat(p["sgu_bs"].T, GW, axis=1)
    shard = {k: p[k].astype(BF) for k in _BIG}

    proj, h1, w_in, (wpa, wpb, wo, cw) = _proj_gather_call(
        x, gmix, shard["w_in"], order, [shard["w_proj_a"], shard["w_proj_b"], shard["w_out"], p["conv_w"]])
    wpa, wpb, wo = (t.reshape(D, D) for t in (wpa, wpb, wo))
    cw = jnp.swapaxes(cw, 0, 1).reshape(4, D)
    (h, pa, pb, x1, *gates, ya, yb, mg), (wgu, wdn) = _mixer_fwd_call(
        proj, x, cw, cb, ba, bx, lam, lng, lnb, wax, wtr, bias, wpa, wpb, wo, [shard["w_gate_up"], shard["w_down"]])
    wdn = wdn.reshape(NDEV // 2, -1, D)
    nb, _, wb = w_in.shape
    fb = wgu.shape[2]
    nh = wdn.shape[0]
    dx1, dx1b, h2, act, dgu, dx2b, facc = _ffn_call(x1, tgt, gffn, gfin, wgu, wdn)
    assert nb == NDEV and 2 * nh == NDEV
    own_gu, part_gu = _dw_plain_call("dw_gate_up", me, dgu, h2, _blk3d(fb), _rows2d(D), 1, fb, D, s)
    own_dn, part_dn = _dw_plain_call("dw_down", me, act, dx2b, _blk3d(fb), _rows2d(D), 2, fb, D, s)
    (dprojb, dya, dpa, dpb, bvec, dws, dbs), (recv_dn,) = _mixer_bwd_pre_call(
        dx1b, proj, pa, pb, lng, lnb, wtr, wtrt, bias, wpa, wpb, wo, [part_dn])
    own_pa, part_pa = _dw_plain_call("dw_proj_a", me, ya, dpa, _rows2d(D), _rows2d(D), NDEV, D, D, s)
    own_pb, part_pb = _dw_plain_call("dw_proj_b", me, yb, dpb, _rows2d(D), _rows2d(D), NDEV, D, D, s)
    own_wo, part_wo = _dw_plain_call("dw_out", me, mg, dx1b, _rows2d(D), _rows2d(D), NDEV, D, D, s)
    (dproj, dx, svec, dwax), (recv_gu, recv_pa, recv_pb, recv_wo) = _mixer_bwd_seq_call(
        dprojb, dya, proj, h, x, dx1, gates, gmix, w_in, cw, lam, wax, [part_gu, part_pa, part_pb, part_wo])
    small = {
        "norm_mix_g": svec[8], "norm_ffn_g": facc[0], "norm_final_g": facc[1],
        "conv_b": svec[0], "rg_ba": svec[1], "rg_bx": svec[2], "rg_lambda": svec[3],
        "sgu_ln_g": bvec[0], "sgu_ln_b": bvec[1],
        "rg_wa": _head_unblocks(dwax[:, :, 0:QW]), "rg_wx": _head_unblocks(dwax[:, :, QW:2 * QW]),
        "sgu_ws": dws, "sgu_bs": dbs[:, ::GW].T,
    }
    packed = _pack_small(small, svec[4:8], facc[2])
    own_in, recv_in, gsum = _dw_exchange_call("dw_in", order_dw, h1, dproj, _rows2d(D), _cols2d(wb), D, wb, s, packed)
    dw = {
        "w_gate_up": (own_gu, recv_gu), "w_down": (own_dn, recv_dn), "w_proj_a": (own_pa, recv_pa),
        "w_proj_b": (own_pb, recv_pb), "w_out": (own_wo, recv_wo), "w_in": (own_in, recv_in),
    }
    return gsum, dx, dw


_BIG = ("w_in", "w_gate_up", "w_down", "w_proj_a", "w_proj_b", "w_out")
_VEC_ROWS = ("norm_mix_g", "norm_ffn_g", "norm_final_g", "conv_b", "rg_ba", "rg_bx", "rg_lambda",
             "sgu_ln_g", "sgu_ln_b", "sgu_bs")
_WEIGHTS = ("norm_mix_g", "w_in", "conv_w", "conv_b", "rg_wa", "rg_ba", "rg_wx", "rg_bx", "rg_lambda",
            "sgu_ln_g", "sgu_ln_b", "sgu_ws", "sgu_bs", "w_proj_a", "w_proj_b", "w_out", "norm_ffn_g",
            "w_gate_up", "w_down", "norm_final_g")


def _pack_small(t, conv_w, extra=None):
    extra = jnp.zeros((1, D), F32) if extra is None else extra.reshape(1, D)
    head = jnp.concatenate([t[k].reshape(1, D) for k in _VEC_ROWS] + [conv_w, extra, jnp.zeros((1, D), F32)], axis=0)
    return jnp.concatenate([head, t["rg_wa"].reshape(64, D), t["rg_wx"].reshape(64, D), t["sgu_ws"].reshape(128, D),
                            jnp.zeros((SMALL_ROWS - 272, D), F32)], axis=0)


def _unpack_small(a):
    out = {k: a[j] for j, k in enumerate(_VEC_ROWS)}
    out["conv_w"] = a[10:14]
    out["rg_wa"] = a[16:80].reshape(16, 64, 64)
    out["rg_wx"] = a[80:144].reshape(16, 64, 64)
    out["sgu_ws"] = a[144:272].reshape(NGRP, CHUNK, CHUNK)
    return out


def kernel(x, norm_mix_g, w_in, conv_w, conv_b, rg_wa, rg_ba, rg_wx, rg_bx, rg_lambda, sgu_ln_g, sgu_ln_b, sgu_ws, sgu_bs, w_proj_a, w_proj_b, w_out, norm_ffn_g, w_gate_up, w_down, norm_final_g, loss_target, m_norm_mix_g, m_w_in, m_conv_w, m_conv_b, m_rg_wa, m_rg_ba, m_rg_wx, m_rg_bx, m_rg_lambda, m_sgu_ln_g, m_sgu_ln_b, m_sgu_ws, m_sgu_bs, m_w_proj_a, m_w_proj_b, m_w_out, m_norm_ffn_g, m_w_gate_up, m_w_down, m_norm_final_g, v_norm_mix_g, v_w_in, v_conv_w, v_conv_b, v_rg_wa, v_rg_ba, v_rg_wx, v_rg_bx, v_rg_lambda, v_sgu_ln_g, v_sgu_ln_b, v_sgu_ws, v_sgu_bs, v_w_proj_a, v_w_proj_b, v_w_out, v_norm_ffn_g, v_w_gate_up, v_w_down, v_norm_final_g):
    args = dict(locals())
    w = {k: args[k] for k in _WEIGHTS}
    m = {k: args["m_" + k] for k in _WEIGHTS}
    v = {k: args["v_" + k] for k in _WEIGHTS}
    for d in (w, m, v):
        for k in _WEIGHTS:
            if k != "norm_final_g":
                d[k] = d[k][0]
    me = _lin(_me())
    me1 = me.reshape(1).astype(jnp.int32)

    order = jnp.bitwise_xor(me, jnp.array(_PASS_FLIPS, jnp.int32)).astype(jnp.int32)
    order_dw = jnp.bitwise_xor(me, jnp.array(_DW_FLIPS, jnp.int32)).astype(jnp.int32)

    gsum, dx, dw = _local_step(x[0], loss_target[0], w, me1, order, order_dw)

    peer = jnp.bitwise_xor(jnp.arange(NDEV, dtype=jnp.int32), me)
    sel_direct = jnp.where(peer == 0, 2, 1).astype(jnp.int32)
    sel_two_level = jnp.where(peer == 0, 2, jnp.where((peer == 1) | (peer % 2 == 0), 1, 0)).astype(jnp.int32)
    grads, delta, new_m, new_v = {}, {}, {}, {}
    for k in _BIG:
        own, recv = dw[k]
        flip = own.shape != w[k].shape
        wmv = [jnp.swapaxes(t, 0, 1) if flip else t for t in (w[k], m[k], v[k])]
        res = _reduce_adamw_call("adamw_" + k, sel_two_level if k == "w_in" else sel_direct, own, recv, *wmv)
        grads[k], delta[k], new_m[k], new_v[k] = (jnp.swapaxes(t, 0, 1) if flip else t for t in res)

    loss = (0.5 / D) * jnp.sum(gsum[14])
    zc = jnp.zeros((4, D), F32)
    d_s, m_s, v_s = _adamw_call("adamw_small", _pack_small(w, zc), gsum, _pack_small(m, zc), _pack_small(v, zc))
    gs, ds, ms, vs = _unpack_small(gsum), _unpack_small(d_s), _unpack_small(m_s), _unpack_small(v_s)
    g_cw = lax.dynamic_slice(gs["conv_w"], (0, me * 128), (4, 128))
    ds["conv_w"], ms["conv_w"], vs["conv_w"] = _adamw_call("adamw_conv_w", w["conv_w"], g_cw, m["conv_w"], v["conv_w"])
    gs["conv_w"] = g_cw
    for k in _WEIGHTS:
        if k not in _BIG:
            shp = w[k].shape
            grads[k], delta[k], new_m[k], new_v[k] = (t[k].reshape(shp) for t in (gs, ds, ms, vs))

    def lift(t, k):
        return t[k] if k == "norm_final_g" else t[k][None]

    outs = [loss, dx[None]]
    for t in (grads, delta, new_m, new_v):
        outs += [lift(t, k) for k in _WEIGHTS]
    return tuple(outs)
```

```python
import functools

import jax
import jax.numpy as jnp
from jax import lax
from jax.experimental import pallas as pl
from jax.experimental.pallas import tpu as pltpu

F32 = jnp.float32
BF = jnp.bfloat16

D = 1024
NDEV = 8
EPS = 1e-6
RG_C = 8.0
CHUNK = 128
NGRP = 8
GW = 128
NQ = 4
QW = 256
RC = 16
SMALL_ROWS = 320
SMALL_PER = SMALL_ROWS // NDEV

ADAM_LR = 0.001
ADAM_B1 = 0.9
ADAM_B2 = 0.999
ADAM_EPS = 1e-08
ADAM_WD = 0.01
ADAM_STEP = 10

VMEM_LIMIT = 60 * 1024 * 1024

MESH = pl.DeviceIdType.MESH


def _rows(n, fn, unroll=2, rc=RC):
    def body(i, c):
        fn(pl.multiple_of(i * rc, rc))
        return c
    lax.fori_loop(0, n // rc, body, 0, unroll=unroll)


def _fold(v):
    return jnp.sum(v.reshape(v.shape[0] // RC, RC, v.shape[1]), axis=0)


def _dot(a, b):
    return jnp.dot(a, b, preferred_element_type=F32)


def _dot_nt(a, b):
    return lax.dot_general(a, b, (((1,), (1,)), ((), ())), preferred_element_type=F32)


def _dot_tn(a, b):
    return lax.dot_general(a, b, (((0,), (0,)), ((), ())), preferred_element_type=F32)


_GC = 0.7978845608028654
_GK = 0.044715


def _gelu(x):
    t = jnp.tanh(_GC * (x + _GK * (x * x * x)))
    return x * (0.5 * (1.0 + t))


def _gelu_grad(x):
    x2 = x * x
    t = jnp.tanh(_GC * (x + _GK * (x2 * x)))
    cdf = 0.5 * (1.0 + t)
    dg = cdf + (0.5 * x) * (1.0 - t * t) * (_GC * (1.0 + (3.0 * _GK) * x2))
    return x * cdf, dg


def _sigmoid(x):
    return jax.nn.sigmoid(x)


def _log1p(e):
    u = 1.0 + e
    d = u - 1.0
    return jnp.where(d == 0.0, e, jnp.log(u) * (e / jnp.where(d == 0.0, 1.0, d)))


def _softplus(z):
    return jnp.maximum(z, 0.0) + _log1p(jnp.exp(-jnp.abs(z)))


def _neg_expm1(z):
    u = jnp.exp(z)
    lu = jnp.log(u)
    k = (1.0 - u) * (z / jnp.where(lu == 0.0, 1.0, lu))
    small = jnp.where(lu == 0.0, -z, k)
    return jnp.where(z > -0.5, small, 1.0 - u)


def _shift_back(prev8, cur, j):
    cat = jnp.concatenate([prev8, cur], axis=0)
    return pltpu.roll(cat, j, 0)[8:8 + cur.shape[0]]


def _shift_fwd(cur, next8, j):
    cat = jnp.concatenate([cur, next8], axis=0)
    n = cat.shape[0]
    return pltpu.roll(cat, n - j, 0)[0:cur.shape[0]]


def _const_spec(shape):
    nd = len(shape)
    return pl.BlockSpec(shape, lambda *_: (0,) * nd, pipeline_mode=pl.Buffered(1))


def _params(sem):
    return pltpu.CompilerParams(dimension_semantics=sem, vmem_limit_bytes=VMEM_LIMIT)


TM_PROJ = 1024
TM_MIX = 256
TM_FFN = 256
TM_DX = 512
TS_DW = 4096


_CHIPS = (4, 2, 6)
_PASS_FLIPS = (0, 1, 4, 2, 6, 5, 3, 7)


class _Gather:
    def __init__(self, ins, outs, send_sems, recv_sems, local_sems):
        self.ins, self.outs = ins, outs
        self.send_sems, self.recv_sems, self.local_sems = send_sems, recv_sems, local_sems
        self.me = _me()
        self.sibling = _flip(self.me, 1)

    def _copy(self, a, kind, block, to, src=None):
        dst = self.outs[a].at[_lin(block)]
        return pltpu.make_async_remote_copy(
            src_ref=dst if src is None else src, dst_ref=dst,
            send_sem=self.send_sems.at[a, kind], recv_sem=self.recv_sems.at[a, kind],
            device_id=to, device_id_type=MESH)

    def _local(self, a):
        return pltpu.make_async_copy(self.ins[a], self.outs[a].at[_lin(self.me)], self.local_sems.at[a])

    def start(self):
        for a in range(len(self.ins)):
            self._local(a).start()
            self._copy(a, 0, self.me, self.sibling, src=self.ins[a]).start()
            for j, f in enumerate(_CHIPS):
                self._copy(a, 1 + j, self.me, _flip(self.me, f), src=self.ins[a]).start()

    def forward(self):
        for j, f in enumerate(_CHIPS):
            for a in range(len(self.ins)):
                self._copy(a, 1 + j, _flip(self.me, f), self.me).wait_recv()
                self._copy(a, 4 + j, _flip(self.me, f), self.sibling).start()

    def finish(self):
        for a in range(len(self.ins)):
            self._copy(a, 0, self.sibling, self.me).wait_recv()
            for j, f in enumerate(_CHIPS):
                self._copy(a, 4 + j, _flip(self.me, f | 1), self.me).wait_recv()
            self._copy(a, 0, self.me, self.sibling, src=self.ins[a]).wait_send()
            for j, f in enumerate(_CHIPS):
                self._copy(a, 1 + j, self.me, _flip(self.me, f), src=self.ins[a]).wait_send()
                self._copy(a, 4 + j, _flip(self.me, f), self.sibling).wait_send()
            self._local(a).wait()


class _Exchange:
    def __init__(self, ins, outs, send_sems, recv_sems, local_sems):
        self.ins, self.outs = ins, outs
        self.send_sems, self.recv_sems, self.local_sems = send_sems, recv_sems, local_sems
        self.me = _me()

    def _copy(self, a, r, outgoing):
        peer = _flip(self.me, r)
        src, dst = (peer, self.me) if outgoing else (self.me, peer)
        return pltpu.make_async_remote_copy(
            src_ref=self.ins[a].at[_lin(src)], dst_ref=self.outs[a].at[_lin(dst)],
            send_sem=self.send_sems.at[a, r - 1], recv_sem=self.recv_sems.at[a, r - 1],
            device_id=peer, device_id_type=MESH)

    def _local(self, a):
        mi = _lin(self.me)
        return pltpu.make_async_copy(self.ins[a].at[mi], self.outs[a].at[mi], self.local_sems.at[a])

    def start(self):
        for a in range(len(self.ins)):
            self._local(a).start()
        for r in range(1, NDEV):
            for a in range(len(self.ins)):
                self._copy(a, r, True).start()

    def finish(self):
        for r in range(1, NDEV):
            for a in range(len(self.ins)):
                self._copy(a, r, False).wait_recv()
        for r in range(1, NDEV):
            for a in range(len(self.ins)):
                self._copy(a, r, True).wait_send()
        for a in range(len(self.ins)):
            self._local(a).wait()


def _gather_sems(n):
    return [pltpu.SemaphoreType.DMA((n, 7)), pltpu.SemaphoreType.DMA((n, 7)), pltpu.SemaphoreType.DMA((n,))]


def _proj_gather_call(x, gmix, w_shard, order, extras):
    s = x.shape[0]
    tm = min(TM_PROJ, s)
    nt = s // tm
    wb = w_shard.shape[1]
    n = len(extras)

    def body(order_ref, x_ref, g_ref, wsh_ref, *rest):
        ex_in = rest[:n]
        proj_hbm, h1_hbm, wall_hbm = rest[n:n + 3]
        ex_out = rest[n + 3:2 * n + 3]
        wall, hc, pbuf, ws_send, ws_recv, ex_send, ex_recv, ex_local, out_sems, psems = rest[2 * n + 3:]
        k = pl.program_id(0)
        i = pl.program_id(1)
        me = _me()
        sibling = _flip(me, 1)
        gather = _Gather(ex_in, ex_out, ex_send, ex_recv, ex_local)

        def wcopy(kind, block, to):
            ref = wall.at[_lin(block)]
            return pltpu.make_async_remote_copy(
                src_ref=ref, dst_ref=ref, send_sem=ws_send.at[kind], recv_sem=ws_recv.at[kind],
                device_id=to, device_id_type=MESH)

        head = i == 0

        @pl.when(head & (k == 0))
        def _():
            wall[_lin(me)] = wsh_ref[...]
            wcopy(0, me, sibling).start()
            for j, f in enumerate(_CHIPS):
                wcopy(1 + j, me, _flip(me, f)).start()
            gather.start()

        out_w = pltpu.make_async_copy(wall, wall_hbm, out_sems.at[0])
        out_h = pltpu.make_async_copy(hc, h1_hbm, out_sems.at[1])

        @pl.when(head & (k == 1))
        def _():
            wcopy(0, sibling, me).wait_recv()
            out_h.start()

        for j, f in enumerate(_CHIPS):
            @pl.when(head & (k == 2 + j))
            def _(j=j, f=f):
                wcopy(1 + j, _flip(me, f), me).wait_recv()
                wcopy(4 + j, _flip(me, f), sibling).start()

            @pl.when(head & (k == 5 + j))
            def _(j=j, f=f):
                wcopy(4 + j, _flip(me, f | 1), me).wait_recv()
                if 5 + j == NDEV - 1:
                    out_w.start()

        pl.when(head & (k == 5))(gather.forward)

        base = pl.multiple_of(i * tm, tm)

        @pl.when(k == 0)
        def _():
            g = g_ref[...]

            def norm(r0):
                xx = x_ref[pl.ds(r0, RC), :]
                r = lax.rsqrt(jnp.mean(xx * xx, axis=-1, keepdims=True) + EPS)
                hc[pl.ds(base + r0, RC), :] = ((xx * r) * g).astype(BF)
            _rows(tm, norm, unroll=4)

        blk = order_ref[k]
        wk = wall.at[blk]
        step = k * nt + i
        slot = lax.rem(step, 2)
        subs = [(r0, min(TM_MIX, tm - r0)) for r0 in range(0, tm, TM_MIX)]

        def out_copy(q, r0, rs):
            return pltpu.make_async_copy(
                pbuf.at[slot, pl.ds(r0, rs)],
                proj_hbm.at[pl.ds(base + r0, rs), pl.ds(pl.multiple_of(blk * wb, 128), wb)], psems.at[slot, q])

        @pl.when(step >= 2)
        def _():
            for q, (r0, rs) in enumerate(subs):
                out_copy(q, r0, rs).wait()

        for q, (r0, rs) in enumerate(subs):
            pbuf[slot, r0:r0 + rs, :] = _dot(hc[pl.ds(base + r0, rs), :], wk[...])
            out_copy(q, r0, rs).start()

        @pl.when((k == NDEV - 1) & (i == nt - 1))
        def _():
            wcopy(0, me, sibling).wait_send()
            for j, f in enumerate(_CHIPS):
                wcopy(1 + j, me, _flip(me, f)).wait_send()
                wcopy(4 + j, _flip(me, f), sibling).wait_send()
            gather.finish()
            out_w.wait()
            out_h.wait()
            for q, (r0, rs) in enumerate(subs):
                out_copy(q, r0, rs).wait()
                pltpu.make_async_copy(pbuf.at[1 - slot, pl.ds(r0, rs)], proj_hbm.at[pl.ds(r0, rs), pl.ds(0, wb)],
                                      psems.at[1 - slot, q]).wait()

    any_spec = pl.BlockSpec(memory_space=pl.ANY)
    grid_spec = pltpu.PrefetchScalarGridSpec(
        num_scalar_prefetch=1, grid=(NDEV, nt),
        in_specs=[pl.BlockSpec((tm, D), lambda k, i, o: (jnp.where(k == 0, i, nt - 1), 0)),
                  pl.BlockSpec((1, D), lambda k, i, o: (0, 0)),
                  pl.BlockSpec(w_shard.shape, lambda k, i, o: (0, 0))] + [any_spec] * n,
        out_specs=[any_spec, any_spec, any_spec] + [any_spec] * n,
        scratch_shapes=[pltpu.VMEM((NDEV,) + w_shard.shape, BF), pltpu.VMEM((s, D), BF), pltpu.VMEM((2, tm, wb), F32),
                        pltpu.SemaphoreType.DMA((7,)), pltpu.SemaphoreType.DMA((7,))] + _gather_sems(n)
        + [pltpu.SemaphoreType.DMA((2,)), pltpu.SemaphoreType.DMA((2, -(-tm // TM_MIX)))])
    outs = pl.pallas_call(
        body, name="proj_gather", grid_spec=grid_spec,
        out_shape=[jax.ShapeDtypeStruct((s, NDEV * wb), F32), jax.ShapeDtypeStruct((s, D), BF),
                   jax.ShapeDtypeStruct((NDEV,) + w_shard.shape, BF)]
        + [jax.ShapeDtypeStruct((NDEV,) + e.shape, e.dtype) for e in extras],
        compiler_params=_params(("arbitrary", "arbitrary")),
    )(order, x, gmix, w_shard, *extras)
    return outs[0], outs[1], outs[2], outs[3:]


def _conv_tile(rx, prev8, cw_ref, cb):
    xc = cb + cw_ref[3:4, :] * rx
    for j in (1, 2, 3):
        xc = xc + cw_ref[3 - j:4 - j, :] * _shift_back(prev8, rx, j)
    return xc


def _gate_tile(xcb, wax_ref, ba, bx, sp, q):
    cs = slice(q * QW, (q + 1) * QW)
    z = _dot(xcb[:, cs], wax_ref[q])
    r = _sigmoid(z[:, 0:QW] + ba[:, cs])
    ig = _sigmoid(z[:, QW:2 * QW] + bx[:, cs])
    return r, ig, (-RG_C * r) * sp[:, cs]


def _scan_fwd(a_s, b_s, out_ref, h0, n):
    rowi = lax.broadcasted_iota(jnp.int32, (8, D), 0)

    def block(t, h):
        rows = pl.ds(t * 8, 8)
        a = a_s[rows, :]
        b = b_s[rows, :]
        for d in (1, 2, 4):
            m = rowi >= d
            b = jnp.where(m, a * pltpu.roll(b, d, 0) + b, b)
            a = jnp.where(m, a * pltpu.roll(a, d, 0), a)
        hb = b + a * h
        out_ref[rows, :] = hb
        return hb[7:8, :]
    h = h0
    for t in range(n // 8):
        h = block(t, h)
    return h


def _scan_bwd(a_s, lm_s, c0, n):
    rowi = lax.broadcasted_iota(jnp.int32, (8, D), 0)
    nblk = n // 8

    def block(k, cin):
        rows = pl.ds((nblk - 1 - k) * 8, 8)
        a = a_s[rows, :]
        dh = lm_s[rows, :]
        b = a * dh
        for d in (1, 2, 4):
            m = rowi < 8 - d
            b = jnp.where(m, a * pltpu.roll(b, 8 - d, 0) + b, b)
            a = jnp.where(m, a * pltpu.roll(a, 8 - d, 0), a)
        mu = b + a * cin
        lm_s[rows, :] = dh + jnp.where(rowi < 7, pltpu.roll(mu, 7, 0), cin)
        return mu[0:1, :]
    c = c0
    for k in range(nblk):
        c = block(k, c)
    return c


def _mixer_fwd_call(proj, x, cw, cb, ba, bx, lam, lng, lnb, wax, wtr, bias, wpa, wpb, wo, shards):
    s = x.shape[0]
    tm = min(TM_MIX, s)
    nt = s // tm
    pw = proj.shape[1]
    n = len(shards)

    def body(proj_ref, x_ref, cw_ref, cb_ref, ba_ref, bx_ref, lam_ref, lng_ref, lnb_ref, wax_ref, wtr_ref,
             bias_ref, wpa_ref, wpb_ref, wo_ref, *rest):
        sh_in = rest[:n]
        h_ref, pa_ref, pb_ref, x1_ref, xc_ref, r_ref, ig_ref, a_s, m_ref, ya_ref, yb_ref, mg_ref = rest[n:n + 12]
        sh_out = rest[n + 12:2 * n + 12]
        prev_s, b_s, hc_s, vn_s, mx_s, g_send, g_recv, g_local = rest[2 * n + 12:]
        i = pl.program_id(0)
        gather = _Gather(sh_in, sh_out, g_send, g_recv, g_local)

        @pl.when(i == 0)
        def _():
            gather.start()
            prev_s[...] = jnp.zeros((8, D), F32)
            hc_s[...] = jnp.zeros((8, D), F32)

        pl.when(i == nt // 2)(gather.forward)

        rx = proj_ref[:, 0:D]
        xc = _conv_tile(rx, prev_s[...], cw_ref, cb_ref[...])
        prev_s[...] = rx[tm - 8:tm, :]
        xc_ref[...] = xc.astype(BF)
        xcb = xc.astype(BF)
        sp = _softplus(-lam_ref[...])
        ba = ba_ref[...]
        bx = bx_ref[...]
        for q in range(NQ):
            cs = slice(q * QW, (q + 1) * QW)
            r, ig, la = _gate_tile(xcb, wax_ref, ba, bx, sp, q)
            mq = jnp.sqrt(_neg_expm1(2.0 * la))
            r_ref[:, cs] = r.astype(BF)
            ig_ref[:, cs] = ig.astype(BF)
            m_ref[:, cs] = mq.astype(BF)
            a_s[:, cs] = jnp.exp(la)
            b_s[:, cs] = mq * (ig * xc[:, cs])

        gv = _gelu(proj_ref[:, 3 * D:4 * D])
        dv = gv - jnp.mean(gv, axis=-1, keepdims=True)
        var = jnp.mean(dv * dv, axis=-1, keepdims=True)
        vn_s[...] = ((dv * lax.rsqrt(var + EPS)) * lng_ref[...] + lnb_ref[...]).astype(BF)
        nc = tm // CHUNK
        for c in range(nc):
            rs = slice(c * CHUNK, (c + 1) * CHUNK)
            for g in range(NGRP):
                cs = slice(g * GW, (g + 1) * GW)
                mx_s[rs, cs] = _dot(wtr_ref[g], vn_s[rs, cs])
        mixed = mx_s[...] + jnp.concatenate([bias_ref[...]] * nc, axis=0)
        yb = (_gelu(proj_ref[:, 2 * D:3 * D]) * mixed).astype(BF)
        yb_ref[...] = yb
        pb = _dot(yb, wpb_ref[...])
        pb_ref[...] = pb.astype(BF)
        mx_s[...] = pb

        hc_s[0:1, :] = _scan_fwd(a_s, b_s, h_ref, hc_s[0:1, :], tm)

        ya = (_gelu(proj_ref[:, D:2 * D]) * h_ref[...]).astype(BF)
        ya_ref[...] = ya
        pa = _dot(ya, wpa_ref[...])
        pa_ref[...] = pa.astype(BF)
        mg = (_sigmoid(proj_ref[:, 4 * D:5 * D]) * pa + _sigmoid(proj_ref[:, 5 * D:6 * D]) * mx_s[...]).astype(BF)
        mg_ref[...] = mg
        x1_ref[...] = x_ref[...] + _dot(mg, wo_ref[...])

        pl.when(i == nt - 1)(gather.finish)

    tile = lambda w: pl.BlockSpec((tm, w), lambda i: (i, 0))
    vec = _const_spec((1, D))
    any_spec = pl.BlockSpec(memory_space=pl.ANY)
    outs = pl.pallas_call(
        body, name="mixer_fwd", grid=(nt,),
        in_specs=[tile(pw), tile(D), _const_spec((4, D)), vec, vec, vec, vec, vec, vec,
                  _const_spec(wax.shape), _const_spec(wtr.shape), _const_spec(bias.shape),
                  _const_spec((D, D)), _const_spec((D, D)), _const_spec((D, D))] + [any_spec] * n,
        out_specs=[tile(D)] * 12 + [any_spec] * n,
        out_shape=[jax.ShapeDtypeStruct((s, D), dt) for dt in (F32, BF, BF, F32, BF, BF, BF, F32, BF, BF, BF, BF)]
        + [jax.ShapeDtypeStruct((NDEV,) + e.shape, e.dtype) for e in shards],
        scratch_shapes=[pltpu.VMEM((8, D), F32), pltpu.VMEM((tm, D), F32),
                        pltpu.VMEM((8, D), F32), pltpu.VMEM((tm, D), BF), pltpu.VMEM((tm, D), F32)]
        + _gather_sems(n),
        compiler_params=_params(("arbitrary",)),
    )(proj, x, cw, cb, ba, bx, lam, lng, lnb, wax, wtr, bias, wpa, wpb, wo, *shards)
    return outs[:12], outs[12:]


def _ffn_call(x1, tgt, gffn, gfin, wgu, wdn):
    s = x1.shape[0]
    tm = min(TM_FFN, s)
    nt = s // tm
    nh = wdn.shape[0]
    fb = wgu.shape[2]

    def body(x1_ref, tgt_ref, gffn_ref, gfin_ref, wgu_ref, wdn_ref,
             dx1_ref, dx1b_ref, h2_ref, act_ref, dgu_ref, dx2b_ref, acc_ref,
             g_s, u_s, dx2_s, accs):
        i = pl.program_id(0)

        @pl.when(i == 0)
        def _():
            accs[...] = jnp.zeros(accs.shape, F32)

        gffn = gffn_ref[...]
        gfin = gfin_ref[...]

        x1 = x1_ref[...]
        r2 = lax.rsqrt(jnp.mean(x1 * x1, axis=-1, keepdims=True) + EPS)
        xh2 = x1 * r2
        h2 = (xh2 * gffn).astype(BF)
        h2_ref[...] = h2

        for k in range(nh):
            g = _dot(h2, wgu_ref[k])
            u = _dot(h2, wgu_ref[k + nh])
            g_s[k] = g
            u_s[k] = u
            act_ref[k] = ((g * _sigmoid(g)) * u).astype(BF)

        x2 = x1
        for k in range(nh):
            x2 = x2 + _dot(act_ref[k], wdn_ref[k])

        r3 = lax.rsqrt(jnp.mean(x2 * x2, axis=-1, keepdims=True) + EPS)
        xh = x2 * r3
        err = xh * gfin - tgt_ref[...]
        accs[2] += _fold(err * err)
        dy = err * (1.0 / D)
        accs[1] += _fold(dy * xh)
        dxh = dy * gfin
        dx2 = r3 * (dxh - xh * jnp.mean(dxh * xh, axis=-1, keepdims=True))
        dx2_s[...] = dx2
        dx2b = dx2.astype(BF)
        dx2b_ref[...] = dx2b

        for k in range(nh):
            da = _dot_nt(dx2b, wdn_ref[k])
            g = g_s[k]
            sg = _sigmoid(g)
            dgu_ref[k] = ((da * u_s[k]) * (sg * (1.0 + g * (1.0 - sg)))).astype(BF)
            dgu_ref[k + nh] = (da * (g * sg)).astype(BF)

        dh2 = _dot_nt(dgu_ref[0], wgu_ref[0])
        for k in range(1, 2 * nh):
            dh2 = dh2 + _dot_nt(dgu_ref[k], wgu_ref[k])

        accs[0] += _fold(dh2 * xh2)
        dxh = dh2 * gffn
        dx1 = dx2_s[...] + r2 * (dxh - xh2 * jnp.mean(dxh * xh2, axis=-1, keepdims=True))
        dx1_ref[...] = dx1
        dx1b_ref[...] = dx1.astype(BF)

        @pl.when(i == nt - 1)
        def _():
            acc_ref[...] = jnp.zeros((8, D), F32)
            for j in range(3):
                acc_ref[j:j + 1, :] = jnp.sum(accs[j], axis=0, keepdims=True)

    tile = lambda w: pl.BlockSpec((tm, w), lambda i: (i, 0))
    vec = _const_spec((1, D))
    return pl.pallas_call(
        body, name="ffn", grid=(nt,),
        in_specs=[tile(D), tile(D), vec, vec, _const_spec(wgu.shape), _const_spec(wdn.shape)],
        out_specs=[tile(D), tile(D), tile(D),
                   pl.BlockSpec((nh, tm, fb), lambda i: (0, i, 0)),
                   pl.BlockSpec((2 * nh, tm, fb), lambda i: (0, i, 0)),
                   tile(D), pl.BlockSpec((8, D), lambda i: (0, 0))],
        out_shape=[jax.ShapeDtypeStruct((s, D), F32), jax.ShapeDtypeStruct((s, D), BF),
                   jax.ShapeDtypeStruct((s, D), BF), jax.ShapeDtypeStruct((nh, s, fb), BF),
                   jax.ShapeDtypeStruct((2 * nh, s, fb), BF), jax.ShapeDtypeStruct((s, D), BF),
                   jax.ShapeDtypeStruct((8, D), F32)],
        scratch_shapes=[pltpu.VMEM((nh, tm, fb), F32), pltpu.VMEM((nh, tm, fb), F32), pltpu.VMEM((tm, D), F32),
                        pltpu.VMEM((3, RC, D), F32)],
        compiler_params=_params(("arbitrary",)),
    )(x1, tgt, gffn, gfin, wgu, wdn)


def _mixer_bwd_pre_call(dx1b, proj, pa, pb, lng, lnb, wtr, wtrt, bias, wpa, wpb, wo, parts):
    s = dx1b.shape[0]
    tm = min(TM_MIX, s)
    nt = s // tm
    pw = proj.shape[1]
    n = len(parts)

    def body(dx1b_ref, uv_ref, gg_ref, pa_ref, pb_ref, lng_ref, lnb_ref, wtr_ref, wtrt_ref, bias_ref,
             wpa_ref, wpb_ref, wo_ref, *rest):
        ex_in = rest[:n]
        dproj_ref, dya_ref, dpa_ref, dpb_ref, vec_ref, dws_ref, dbs_ref = rest[n:n + 7]
        ex_out = rest[n + 7:2 * n + 7]
        vn_s, mx_s, dmx_s, dvn_s, accs, dbs_s, e_send, e_recv, e_local = rest[2 * n + 7:]
        i = pl.program_id(0)
        exchange = _Exchange(ex_in, ex_out, e_send, e_recv, e_local)

        @pl.when(i == 0)
        def _():
            exchange.start()
            accs[...] = jnp.zeros(accs.shape, F32)
            dbs_s[...] = jnp.zeros(dbs_s.shape, F32)
            dws_ref[...] = jnp.zeros(dws_ref.shape, F32)

        dm = _dot_nt(dx1b_ref[...], wo_ref[...])
        sa = _sigmoid(gg_ref[:, 0:D])
        sb = _sigmoid(gg_ref[:, D:2 * D])
        dpa = dm * sa
        dpb = dm * sb
        dpab = dpa.astype(BF)
        dpbb = dpb.astype(BF)
        dpa_ref[...] = dpab
        dpb_ref[...] = dpbb
        dproj_ref[:, 2 * D:3 * D] = ((dpa * pa_ref[...].astype(F32)) * (1.0 - sa)).astype(BF)
        dproj_ref[:, 3 * D:4 * D] = ((dpb * pb_ref[...].astype(F32)) * (1.0 - sb)).astype(BF)

        dya_ref[...] = _dot_nt(dpab, wpa_ref[...])
        dyb = _dot_nt(dpbb, wpb_ref[...])

        lng = lng_ref[...]
        gv, dgelu_v = _gelu_grad(uv_ref[:, D:2 * D])
        dv = gv - jnp.mean(gv, axis=-1, keepdims=True)
        rstd = lax.rsqrt(jnp.mean(dv * dv, axis=-1, keepdims=True) + EPS)
        xh = dv * rstd
        vn_s[...] = (xh * lng + lnb_ref[...]).astype(BF)

        nc = tm // CHUNK
        for c in range(nc):
            rs = slice(c * CHUNK, (c + 1) * CHUNK)
            for g in range(NGRP):
                cs = slice(g * GW, (g + 1) * GW)
                mx_s[rs, cs] = _dot(wtr_ref[g], vn_s[rs, cs])

        gu, dgelu_u = _gelu_grad(uv_ref[:, 0:D])
        mixed = mx_s[...] + jnp.concatenate([bias_ref[...]] * nc, axis=0)
        dproj_ref[:, 0:D] = ((dyb * mixed) * dgelu_u).astype(BF)
        dmx = dyb * gu
        dmx_s[...] = dmx.astype(BF)
        dbs_s[...] += jnp.sum(dmx.reshape(nc, CHUNK, D), axis=0)

        for c in range(nc):
            rs = slice(c * CHUNK, (c + 1) * CHUNK)
            for g in range(NGRP):
                cs = slice(g * GW, (g + 1) * GW)
                dvn_s[rs, cs] = _dot(wtrt_ref[g], dmx_s[rs, cs])
                dws_ref[g] += _dot_nt(dmx_s[rs, cs], vn_s[rs, cs])

        dvn = dvn_s[...]
        accs[0] += _fold(dvn * xh)
        accs[1] += _fold(dvn)
        dxh = dvn * lng
        m1 = jnp.mean(dxh, axis=-1, keepdims=True)
        m2 = jnp.mean(dxh * xh, axis=-1, keepdims=True)
        dproj_ref[:, D:2 * D] = ((rstd * (dxh - m1 - xh * m2)) * dgelu_v).astype(BF)

        @pl.when(i == nt - 1)
        def _():
            vec_ref[...] = jnp.zeros((8, D), F32)
            for j in range(2):
                vec_ref[j:j + 1, :] = jnp.sum(accs[j], axis=0, keepdims=True)
            row = lax.broadcasted_iota(jnp.int32, (CHUNK, CHUNK), 0)
            col = lax.broadcasted_iota(jnp.int32, (CHUNK, CHUNK), 1)
            for g in range(NGRP):
                dws_ref[g] = jnp.where(row >= col, dws_ref[g], 0.0)
                gs = jnp.sum(dbs_s[:, g * GW:(g + 1) * GW], axis=1, keepdims=True)
                dbs_ref[:, g * GW:(g + 1) * GW] = jnp.broadcast_to(gs, (CHUNK, GW))
            exchange.finish()

    tile = lambda w: pl.BlockSpec((tm, w), lambda i: (i, 0))
    vec = _const_spec((1, D))
    any_spec = pl.BlockSpec(memory_space=pl.ANY)
    outs = pl.pallas_call(
        body, name="mixer_bwd_pre", grid=(nt,),
        in_specs=[tile(D), pl.BlockSpec((tm, 2 * D), lambda i: (i, 1)), pl.BlockSpec((tm, 2 * D), lambda i: (i, 2)),
                  tile(D), tile(D), vec, vec, _const_spec(wtr.shape), _const_spec(wtrt.shape),
                  _const_spec(bias.shape), _const_spec((D, D)), _const_spec((D, D)), _const_spec((D, D))]
        + [any_spec] * n,
        out_specs=[tile(4 * D), tile(D), tile(D), tile(D), pl.BlockSpec((8, D), lambda i: (0, 0)),
                   pl.BlockSpec((NGRP, CHUNK, CHUNK), lambda i: (0, 0, 0)),
                   pl.BlockSpec((CHUNK, D), lambda i: (0, 0))] + [any_spec] * n,
        out_shape=[jax.ShapeDtypeStruct((s, 4 * D), BF), jax.ShapeDtypeStruct((s, D), F32),
                   jax.ShapeDtypeStruct((s, D), BF), jax.ShapeDtypeStruct((s, D), BF),
                   jax.ShapeDtypeStruct((8, D), F32), jax.ShapeDtypeStruct((NGRP, CHUNK, CHUNK), F32),
                   jax.ShapeDtypeStruct((CHUNK, D), F32)]
        + [jax.ShapeDtypeStruct(p.shape, p.dtype) for p in parts],
        scratch_shapes=[pltpu.VMEM((tm, D), BF), pltpu.VMEM((tm, D), F32), pltpu.VMEM((tm, D), BF),
                        pltpu.VMEM((tm, D), F32), pltpu.VMEM((2, RC, D), F32), pltpu.VMEM((CHUNK, D), F32)]
        + _gather_sems(n),
        compiler_params=_params(("arbitrary",)),
    )(dx1b, proj, proj, pa, pb, lng, lnb, wtr, wtrt, bias, wpa, wpb, wo, *parts)
    return outs[:7], outs[7:]


def _mixer_bwd_seq_call(dprojb, dya, proj, h, x, dx1, gates, gmix, w_in_all, cw, lam, wax, parts):
    s = dya.shape[0]
    tm = min(TM_MIX, s)
    nt = s // tm
    tb = tm // 8
    n = len(parts)
    nb, _, wb = w_in_all.shape

    def body(dpb_ref, dya_ref, xg_ref, xh8_ref, h_ref, hh8_ref, x_ref, dx1_ref, xc_s, r_s, ig_s, a_s, m_s,
             gmix_ref, win_ref, cw_ref, lam_ref, wax_ref, *rest):
        ex_in = rest[:n]
        dpa_ref, dx_ref, vec_ref, dwax_ref = rest[n:n + 4]
        ex_out = rest[n + 4:2 * n + 4]
        lm_s, dh_s, dxc_s, c_s, accs, e_send, e_recv, e_local = rest[2 * n + 4:]
        i = pl.program_id(0)
        exchange = _Exchange(ex_in, ex_out, e_send, e_recv, e_local)

        @pl.when(i == 0)
        def _():
            exchange.start()
            accs[...] = jnp.zeros(accs.shape, F32)
            dwax_ref[...] = jnp.zeros(dwax_ref.shape, F32)
            c_s[...] = jnp.zeros((8, D), F32)
            dxc_s[tm:tm + 8, :] = jnp.zeros((8, D), F32)

        first_tile = i == nt - 1
        prev8 = jnp.where(first_tile, 0.0, xh8_ref[...])
        hprev8 = jnp.where(first_tile, 0.0, hh8_ref[...])
        lamv = lam_ref[...]
        sp = _softplus(-lamv)
        hv = h_ref[...]
        g, dg = _gelu_grad(xg_ref[:, D:2 * D])
        dya = dya_ref[...]
        lm_s[...] = dya * g
        drg = ((dya * hv) * dg).astype(BF)
        dpa_ref[:, D:2 * D] = drg
        dpa_ref[:, 2 * D:6 * D] = dpb_ref[...]

        def dpb_block(k):
            return _dot_nt(dpb_ref[:, k * wb - 2 * D:(k + 1) * wb - 2 * D], win_ref[k])
        dh = (_dot_nt(drg[:, 0:2 * wb - D], win_ref[1, :, D - wb:wb])
              + _dot_nt(drg[:, 2 * wb - D:D], win_ref[2, :, 0:2 * D - 2 * wb])
              + _dot_nt(dpb_ref[:, 0:3 * wb - 2 * D], win_ref[2, :, 2 * D - 2 * wb:wb]))
        for k in range(3, 6):
            dh = dh + dpb_block(k)
        dh_s[...] = dh

        c_s[0:1, :] = _scan_bwd(a_s, lm_s, c_s[0:1, :], tm)

        hprev = _shift_back(hprev8, h_ref[...], 1)
        for q in range(NQ):
            cs = slice(q * QW, (q + 1) * QW)
            r = r_s[:, cs].astype(F32)
            ig = ig_s[:, cs].astype(F32)
            a = a_s[:, cs]
            m = m_s[:, cs].astype(F32)
            lm = lm_s[:, cs]
            xq = xc_s[:, cs].astype(F32)
            dixc = lm * m
            dla = (lm * hprev[:, cs]) * a - ((lm * (ig * xq)) * (a * a)) / m
            accs[3, :, cs] += _fold(dla * r)
            dza = (dla * (-RG_C * sp[:, cs])) * (r * (1.0 - r))
            dzx = (dixc * xq) * (ig * (1.0 - ig))
            accs[1, :, cs] += _fold(dza)
            accs[2, :, cs] += _fold(dzx)
            dz = jnp.concatenate([dza, dzx], axis=1).astype(BF)
            dxc_s[0:tm, cs] = dixc * ig + _dot_nt(dz, wax_ref[q])
            dwax_ref[q] += _dot_tn(xc_s[:, cs], dz)

        cur = dxc_s[0:tm, :]
        nx = dxc_s[tm:tm + 8, :]
        drx = cw_ref[3:4, :] * cur
        for j in (1, 2, 3):
            drx = drx + cw_ref[3 - j:4 - j, :] * _shift_fwd(cur, nx, j)
        accs[0] += _fold(cur)
        rx = xg_ref[:, 0:D]
        accs[7] += _fold(cur * rx)
        for j in (1, 2, 3):
            accs[7 - j] += _fold(cur * _shift_back(prev8, rx, j))
        dxc_s[tm:tm + 8, :] = cur[0:8, :]

        drxb = drx.astype(BF)
        dpa_ref[:, 0:D] = drxb
        dh = dh_s[...] + _dot_nt(drxb[:, 0:wb], win_ref[0]) + _dot_nt(drxb[:, wb:D], win_ref[1, :, 0:D - wb])
        for k in range(6, nb):
            dh = dh + dpb_block(k)
        xx = x_ref[...]
        rn = lax.rsqrt(jnp.mean(xx * xx, axis=-1, keepdims=True) + EPS)
        xhn = xx * rn
        accs[8] += _fold(dh * xhn)
        dxh = dh * gmix_ref[...]
        dx_ref[...] = dx1_ref[...] + rn * (dxh - xhn * jnp.mean(dxh * xhn, axis=-1, keepdims=True))

        @pl.when(i == nt - 1)
        def _():
            vec_ref[...] = jnp.zeros((16, D), F32)
            for j in range(9):
                vec_ref[j:j + 1, :] = jnp.sum(accs[j], axis=0, keepdims=True)
            vec_ref[3:4, :] = vec_ref[3:4, :] * (RG_C * _sigmoid(-lamv))
            exchange.finish()

    rev = lambda w: pl.BlockSpec((tm, w), lambda i: (nt - 1 - i, 0))
    halo = pl.BlockSpec((8, D), lambda i: (jnp.maximum((nt - 1 - i) * tb - 1, 0), 0))
    vec = _const_spec((1, D))
    any_spec = pl.BlockSpec(memory_space=pl.ANY)
    outs = pl.pallas_call(
        body, name="mixer_bwd_seq", grid=(nt,),
        in_specs=[rev(4 * D), rev(D), rev(2 * D), halo, rev(D), halo] + [rev(D)] * 7
        + [vec, _const_spec(w_in_all.shape), _const_spec((4, D)), vec, _const_spec(wax.shape)] + [any_spec] * n,
        out_specs=[rev(6 * D), rev(D), pl.BlockSpec((16, D), lambda i: (0, 0)),
                   pl.BlockSpec((NQ, QW, 2 * QW), lambda i: (0, 0, 0))] + [any_spec] * n,
        out_shape=[jax.ShapeDtypeStruct((s, 6 * D), BF), jax.ShapeDtypeStruct((s, D), F32),
                   jax.ShapeDtypeStruct((16, D), F32), jax.ShapeDtypeStruct((NQ, QW, 2 * QW), F32)]
        + [jax.ShapeDtypeStruct(p.shape, p.dtype) for p in parts],
        scratch_shapes=[pltpu.VMEM((tm, D), F32), pltpu.VMEM((tm, D), F32),
                        pltpu.VMEM((tm + 8, D), F32), pltpu.VMEM((8, D), F32), pltpu.VMEM((9, RC, D), F32)]
        + _gather_sems(n),
        compiler_params=_params(("arbitrary",)),
    )(dprojb, dya, proj, proj, h, h, x, dx1, *gates, gmix, w_in_all, cw, lam, wax, *parts)
    return outs[:4], outs[4:]


def _device_of(d):
    return (d // 4, lax.rem(d // 2, 2), lax.rem(d, 2))


_DW_PLAN = (("C", 3), ("A", 3), ("C", 1), ("A", 1), ("C", 2), ("A", 2), ("S", 0), ("O", 0))
_DW_FLIPS = tuple({"C": 2 * j + 1, "A": 2 * j, "S": 1, "O": 0}[kind] for kind, j in _DW_PLAN)


def _dw_exchange_call(name, order, a, b, a_spec, b_spec, k1, n1, s, small):
    ts = min(TS_DW, s)
    ns = s // ts
    nstep = len(_DW_PLAN)

    def slab(i, order_ref):
        return order_ref[i]

    def body(order_ref, a_ref, b_ref, g_ref, own_ref, recv_ref, gsum_ref, acc, sbuf, stage, send_sems, recv_sems,
             st_send, st_recv, local_sem, rbuf, gacc, send1, recv1, send2, recv2):
        i = pl.program_id(0)
        j = pl.program_id(1)
        me = _me()
        mi = _lin(me)
        sibling = _flip(me, 1)
        allreduce = _SmallAllReduce(g_ref, gacc, rbuf, send1, recv1, send2, recv2)
        pl.when((i == 0) & (j == 0))(allreduce.scatter)
        pl.when((i == nstep // 2) & (j == 0))(allreduce.reduce)
        p = _dot_tn(a_ref[...], b_ref[...])

        @pl.when(j == 0)
        def _():
            acc[...] = p

        @pl.when(j > 0)
        def _():
            acc[...] += p

        def send(step):
            kind, jj = _DW_PLAN[step]
            src = sbuf.at[step % 2]
            if kind == "C":
                return pltpu.make_async_remote_copy(
                    src_ref=src, dst_ref=stage.at[jj - 1], send_sem=st_send.at[jj - 1], recv_sem=st_recv.at[jj - 1],
                    device_id=sibling, device_id_type=MESH)
            to = sibling if kind == "S" else _flip(me, 2 * jj)
            return pltpu.make_async_remote_copy(
                src_ref=src, dst_ref=recv_ref.at[mi], send_sem=send_sems.at[jj], recv_sem=recv_sems.at[jj],
                device_id=to, device_id_type=MESH)

        def arrival(jj):
            frm = sibling if jj == 0 else _flip(me, 2 * jj)
            return pltpu.make_async_remote_copy(
                src_ref=sbuf.at[0], dst_ref=recv_ref.at[_lin(frm)], send_sem=send_sems.at[jj],
                recv_sem=recv_sems.at[jj], device_id=frm, device_id_type=MESH)

        def staged(jj):
            return pltpu.make_async_remote_copy(
                src_ref=sbuf.at[0], dst_ref=stage.at[jj - 1], send_sem=st_send.at[jj - 1], recv_sem=st_recv.at[jj - 1],
                device_id=sibling, device_id_type=MESH)

        for step, (kind, jj) in enumerate(_DW_PLAN):
            @pl.when((i == step) & (j == ns - 1))
            def _(step=step, kind=kind, jj=jj):
                if step >= 2:
                    send(step - 2).wait_send()
                if kind == "A":
                    staged(jj).wait_recv()
                    sbuf[step % 2] = (acc[...] + stage[jj - 1].astype(F32)).astype(BF)
                else:
                    sbuf[step % 2] = acc[...].astype(BF)
                if kind != "O":
                    send(step).start()
                else:
                    own_ref[...] = acc[...]
                    mine = pltpu.make_async_copy(sbuf.at[step % 2], recv_ref.at[mi], local_sem)
                    mine.start()
                    send(step - 1).wait_send()
                    for q in range(4):
                        arrival(q).wait_recv()
                    mine.wait()
                    allreduce.finish()
                    gsum_ref[...] = gacc[...]

    vmem = pl.BlockSpec(memory_space=pltpu.VMEM)
    grid_spec = pltpu.PrefetchScalarGridSpec(
        num_scalar_prefetch=1, grid=(nstep, ns),
        in_specs=[a_spec(ts, slab), b_spec(ts, slab), vmem],
        out_specs=[pl.BlockSpec((k1, n1), lambda i, j, o: (0, 0)), pl.BlockSpec(memory_space=pl.ANY), vmem],
        scratch_shapes=[pltpu.VMEM((k1, n1), F32), pltpu.VMEM((2, k1, n1), BF), pltpu.VMEM((3, k1, n1), BF),
                        pltpu.SemaphoreType.DMA((4,)), pltpu.SemaphoreType.DMA((4,)),
                        pltpu.SemaphoreType.DMA((3,)), pltpu.SemaphoreType.DMA((3,)),
                        pltpu.SemaphoreType.DMA(()), pltpu.VMEM((NDEV, SMALL_PER, D), F32),
                        pltpu.VMEM((SMALL_ROWS, D), F32)]
        + [pltpu.SemaphoreType.DMA((NDEV - 1,))] * 4)
    return pl.pallas_call(
        body, name=name, grid_spec=grid_spec,
        out_shape=[jax.ShapeDtypeStruct((k1, n1), F32), jax.ShapeDtypeStruct((NDEV, k1, n1), BF),
                   jax.ShapeDtypeStruct((SMALL_ROWS, D), F32)],
        compiler_params=_params(("arbitrary", "arbitrary")),
    )(order, a, b, small)


def _dw_plain_call(name, me, a, b, a_spec, b_spec, split, k1, n1, s):
    nb = NDEV // split
    r = k1 // split
    ts = min(TS_DW, s)
    ns = s // ts

    def slab(i, me_ref):
        return i

    def body(me_ref, a_ref, b_ref, own_ref, part_ref, acc):
        i = pl.program_id(0)
        j = pl.program_id(1)
        p = _dot_tn(a_ref[...], b_ref[...])

        @pl.when(j == 0)
        def _():
            acc[...] = p

        @pl.when(j > 0)
        def _():
            acc[...] += p

        @pl.when(j == ns - 1)
        def _():
            part_ref[...] = acc[...].astype(BF)

            @pl.when(i == me_ref[0] // split)
            def _():
                off = pl.multiple_of(lax.rem(me_ref[0], split) * r, RC)
                own_ref[...] = acc[pl.ds(off, r), :]

    grid_spec = pltpu.PrefetchScalarGridSpec(
        num_scalar_prefetch=1, grid=(nb, ns),
        in_specs=[a_spec(ts, slab), b_spec(ts, slab)],
        out_specs=[pl.BlockSpec((r, n1), lambda i, j, me_ref: (0, 0)),
                   pl.BlockSpec((None, k1, n1), lambda i, j, me_ref: (i, 0, 0))],
        scratch_shapes=[pltpu.VMEM((k1, n1), F32)])
    own, part = pl.pallas_call(
        body, name=name, grid_spec=grid_spec,
        out_shape=[jax.ShapeDtypeStruct((r, n1), F32), jax.ShapeDtypeStruct((nb, k1, n1), BF)],
        compiler_params=_params(("arbitrary", "arbitrary")),
    )(me, a, b)
    return own, part.reshape(NDEV, r, n1)


def _rows2d(w):
    return lambda ts, slab: pl.BlockSpec((ts, w), lambda i, j, me_ref: (j, 0))


def _cols2d(w):
    return lambda ts, slab: pl.BlockSpec((ts, w), lambda i, j, me_ref: (j, slab(i, me_ref)))


def _blk3d(w):
    return lambda ts, slab: pl.BlockSpec((None, ts, w), lambda i, j, me_ref: (slab(i, me_ref), j, 0))


_BC1 = 1.0 - ADAM_B1 ** ADAM_STEP
_BC2 = 1.0 - ADAM_B2 ** ADAM_STEP


def _adamw_math(w, g, m, v):
    m = ADAM_B1 * m + (1.0 - ADAM_B1) * g
    v = ADAM_B2 * v + (1.0 - ADAM_B2) * (g * g)
    m_hat = m / _BC1
    v_hat = v / _BC2
    delta = -ADAM_LR * (m_hat / (jnp.sqrt(v_hat) + ADAM_EPS) + ADAM_WD * w)
    return delta, m, v


def _row_tile(r):
    for t in (256, 176, 128, 64, 32, 16, 8):
        if r % t == 0:
            return t
    return r


def _reduce_adamw_call(name, sel, own, recv, w, m, v):
    r, c = own.shape
    tr = _row_tile(r)

    def body(sel_ref, own_ref, recv_ref, w_ref, m_ref, v_ref, g_ref, d_ref, nm_ref, nv_ref):
        g = jnp.zeros((tr, c), F32)
        for sdev in range(NDEV):
            part = jnp.where(sel_ref[sdev] == 1, recv_ref[sdev].astype(F32), 0.0)
            g = g + jnp.where(sel_ref[sdev] == 2, own_ref[...], part)
        g_ref[...] = g
        d_ref[...], nm_ref[...], nv_ref[...] = _adamw_math(w_ref[...], g, m_ref[...], v_ref[...])

    tile = pl.BlockSpec((tr, c), lambda i, me_ref: (i, 0))
    grid_spec = pltpu.PrefetchScalarGridSpec(
        num_scalar_prefetch=1, grid=(r // tr,),
        in_specs=[tile, pl.BlockSpec((NDEV, tr, c), lambda i, me_ref: (0, i, 0)), tile, tile, tile],
        out_specs=[tile] * 4)
    return pl.pallas_call(
        body, name=name, grid_spec=grid_spec,
        out_shape=[jax.ShapeDtypeStruct((r, c), F32)] * 4,
        compiler_params=_params(("parallel",)),
    )(sel, own, recv, w, m, v)


def _adamw_call(name, w, g, m, v):
    r, c = w.shape
    tr = _row_tile(r)

    def body(w_ref, g_ref, m_ref, v_ref, d_ref, nm_ref, nv_ref):
        d_ref[...], nm_ref[...], nv_ref[...] = _adamw_math(w_ref[...], g_ref[...], m_ref[...], v_ref[...])

    tile = pl.BlockSpec((tr, c), lambda i: (i, 0))
    return pl.pallas_call(
        body, name=name, grid=(r // tr,), in_specs=[tile] * 4, out_specs=[tile] * 3,
        out_shape=[jax.ShapeDtypeStruct((r, c), F32)] * 3,
        compiler_params=_params(("parallel",)),
    )(w, g, m, v)


def _me():
    return lax.axis_index("x"), lax.axis_index("y"), lax.axis_index("c")


def _flip(pos, r):
    x, y, c = pos
    return (1 - x if r & 4 else x, 1 - y if r & 2 else y, 1 - c if r & 1 else c)


def _lin(pos):
    return pos[0] * 4 + pos[1] * 2 + pos[2]


class _SmallAllReduce:
    def __init__(self, g_ref, out_ref, rbuf, send1, recv1, send2, recv2):
        self.g, self.out, self.rbuf = g_ref, out_ref, rbuf
        self.sems = (send1, recv1, send2, recv2)
        self.me = _me()
        self.mi = _lin(self.me)

    @staticmethod
    def _rows(d):
        return pl.ds(pl.multiple_of(d * SMALL_PER, 8), SMALL_PER)

    def _scatter(self, r, outgoing):
        peer = _flip(self.me, r)
        src, dst = (_lin(peer), self.mi) if outgoing else (self.mi, _lin(peer))
        return pltpu.make_async_remote_copy(
            src_ref=self.g.at[self._rows(src)], dst_ref=self.rbuf.at[dst],
            send_sem=self.sems[0].at[r - 1], recv_sem=self.sems[1].at[r - 1], device_id=peer, device_id_type=MESH)

    def _spread(self, r, outgoing):
        peer = _flip(self.me, r)
        rows = self._rows(self.mi if outgoing else _lin(peer))
        return pltpu.make_async_remote_copy(
            src_ref=self.out.at[rows], dst_ref=self.out.at[rows],
            send_sem=self.sems[2].at[r - 1], recv_sem=self.sems[3].at[r - 1], device_id=peer, device_id_type=MESH)

    def scatter(self):
        for r in range(1, NDEV):
            self._scatter(r, True).start()
        self.rbuf[self.mi] = self.g[self._rows(self.mi), :]

    def reduce(self):
        for r in range(1, NDEV):
            self._scatter(r, False).wait_recv()
        for r in range(1, NDEV):
            self._scatter(r, True).wait_send()
        tot = self.rbuf[0]
        for d in range(1, NDEV):
            tot = tot + self.rbuf[d]
        self.out[self._rows(self.mi), :] = tot
        for r in range(1, NDEV):
            self._spread(r, True).start()

    def finish(self):
        for r in range(1, NDEV):
            self._spread(r, False).wait_recv()
        for r in range(1, NDEV):
            self._spread(r, True).wait_send()


def _head_blocks(w):
    z = jnp.zeros((64, 64), w.dtype)
    groups = []
    for q in range(NQ):
        rows = [jnp.concatenate([w[4 * q + a] if a == b else z for b in range(4)], axis=1) for a in range(4)]
        groups.append(jnp.concatenate(rows, axis=0))
    return jnp.stack(groups)


def _head_unblocks(g):
    return jnp.stack([g[q, 64 * a:64 * a + 64, 64 * a:64 * a + 64] for q in range(NQ) for a in range(4)])


def _local_step(x, tgt, p, me, order, order_dw):
    s = x.shape[0]
    vec = lambda a: a.reshape(1, D)
    gmix, gffn, gfin = vec(p["norm_mix_g"]), vec(p["norm_ffn_g"]), vec(p["norm_final_g"])
    cb, ba, bx, lam = vec(p["conv_b"]), vec(p["rg_ba"]), vec(p["rg_bx"]), vec(p["rg_lambda"])
    lng, lnb = vec(p["sgu_ln_g"]), vec(p["sgu_ln_b"])
    wax = jnp.concatenate([_head_blocks(p["rg_wa"]), _head_blocks(p["rg_wx"])], axis=2).astype(BF)
    tril = jnp.tril(jnp.ones((CHUNK, CHUNK), bool))
    ws = jnp.where(tril[None], p["sgu_ws"], 0.0)
    wtr = ws.astype(BF)
    wtrt = jnp.swapaxes(ws, 1, 2).astype(BF)
    bias = jnp.repeat(p["sgu_bs"].T, GW, axis=1)
    shard = {k: p[k].astype(BF) for k in _BIG}

    proj, h1, w_in, (wpa, wpb, wo, cw) = _proj_gather_call(
        x, gmix, shard["w_in"], order, [shard["w_proj_a"], shard["w_proj_b"], shard["w_out"], p["conv_w"]])
    wpa, wpb, wo = (t.reshape(D, D) for t in (wpa, wpb, wo))
    cw = jnp.swapaxes(cw, 0, 1).reshape(4, D)
    (h, pa, pb, x1, *gates, ya, yb, mg), (wgu, wdn) = _mixer_fwd_call(
        proj, x, cw, cb, ba, bx, lam, lng, lnb, wax, wtr, bias, wpa, wpb, wo, [shard["w_gate_up"], shard["w_down"]])
    wdn = wdn.reshape(NDEV // 2, -1, D)
    nb, _, wb = w_in.shape
    fb = wgu.shape[2]
    nh = wdn.shape[0]
    dx1, dx1b, h2, act, dgu, dx2b, facc = _ffn_call(x1, tgt, gffn, gfin, wgu, wdn)
    assert nb == NDEV and 2 * nh == NDEV
    own_gu, part_gu = _dw_plain_call("dw_gate_up", me, dgu, h2, _blk3d(fb), _rows2d(D), 1, fb, D, s)
    own_dn, part_dn = _dw_plain_call("dw_down", me, act, dx2b, _blk3d(fb), _rows2d(D), 2, fb, D, s)
    (dprojb, dya, dpa, dpb, bvec, dws, dbs), (recv_dn,) = _mixer_bwd_pre_call(
        dx1b, proj, pa, pb, lng, lnb, wtr, wtrt, bias, wpa, wpb, wo, [part_dn])
    own_pa, part_pa = _dw_plain_call("dw_proj_a", me, ya, dpa, _rows2d(D), _rows2d(D), NDEV, D, D, s)
    own_pb, part_pb = _dw_plain_call("dw_proj_b", me, yb, dpb, _rows2d(D), _rows2d(D), NDEV, D, D, s)
    own_wo, part_wo = _dw_plain_call("dw_out", me, mg, dx1b, _rows2d(D), _rows2d(D), NDEV, D, D, s)
    (dproj, dx, svec, dwax), (recv_gu, recv_pa, recv_pb, recv_wo) = _mixer_bwd_seq_call(
        dprojb, dya, proj, h, x, dx1, gates, gmix, w_in, cw, lam, wax, [part_gu, part_pa, part_pb, part_wo])
    small = {
        "norm_mix_g": svec[8], "norm_ffn_g": facc[0], "norm_final_g": facc[1],
        "conv_b": svec[0], "rg_ba": svec[1], "rg_bx": svec[2], "rg_lambda": svec[3],
        "sgu_ln_g": bvec[0], "sgu_ln_b": bvec[1],
        "rg_wa": _head_unblocks(dwax[:, :, 0:QW]), "rg_wx": _head_unblocks(dwax[:, :, QW:2 * QW]),
        "sgu_ws": dws, "sgu_bs": dbs[:, ::GW].T,
    }
    packed = _pack_small(small, svec[4:8], facc[2])
    own_in, recv_in, gsum = _dw_exchange_call("dw_in", order_dw, h1, dproj, _rows2d(D), _cols2d(wb), D, wb, s, packed)
    dw = {
        "w_gate_up": (own_gu, recv_gu), "w_down": (own_dn, recv_dn), "w_proj_a": (own_pa, recv_pa),
        "w_proj_b": (own_pb, recv_pb), "w_out": (own_wo, recv_wo), "w_in": (own_in, recv_in),
    }
    return gsum, dx, dw


_BIG = ("w_in", "w_gate_up", "w_down", "w_proj_a", "w_proj_b", "w_out")
_VEC_ROWS = ("norm_mix_g", "norm_ffn_g", "norm_final_g", "conv_b", "rg_ba", "rg_bx", "rg_lambda",
             "sgu_ln_g", "sgu_ln_b", "sgu_bs")
_WEIGHTS = ("norm_mix_g", "w_in", "conv_w", "conv_b", "rg_wa", "rg_ba", "rg_wx", "rg_bx", "rg_lambda",
            "sgu_ln_g", "sgu_ln_b", "sgu_ws", "sgu_bs", "w_proj_a", "w_proj_b", "w_out", "norm_ffn_g",
            "w_gate_up", "w_down", "norm_final_g")


def _pack_small(t, conv_w, extra=None):
    extra = jnp.zeros((1, D), F32) if extra is None else extra.reshape(1, D)
    head = jnp.concatenate([t[k].reshape(1, D) for k in _VEC_ROWS] + [conv_w, extra, jnp.zeros((1, D), F32)], axis=0)
    return jnp.concatenate([head, t["rg_wa"].reshape(64, D), t["rg_wx"].reshape(64, D), t["sgu_ws"].reshape(128, D),
                            jnp.zeros((SMALL_ROWS - 272, D), F32)], axis=0)


def _unpack_small(a):
    out = {k: a[j] for j, k in enumerate(_VEC_ROWS)}
    out["conv_w"] = a[10:14]
    out["rg_wa"] = a[16:80].reshape(16, 64, 64)
    out["rg_wx"] = a[80:144].reshape(16, 64, 64)
    out["sgu_ws"] = a[144:272].reshape(NGRP, CHUNK, CHUNK)
    return out


def kernel(x, norm_mix_g, w_in, conv_w, conv_b, rg_wa, rg_ba, rg_wx, rg_bx, rg_lambda, sgu_ln_g, sgu_ln_b, sgu_ws, sgu_bs, w_proj_a, w_proj_b, w_out, norm_ffn_g, w_gate_up, w_down, norm_final_g, loss_target, m_norm_mix_g, m_w_in, m_conv_w, m_conv_b, m_rg_wa, m_rg_ba, m_rg_wx, m_rg_bx, m_rg_lambda, m_sgu_ln_g, m_sgu_ln_b, m_sgu_ws, m_sgu_bs, m_w_proj_a, m_w_proj_b, m_w_out, m_norm_ffn_g, m_w_gate_up, m_w_down, m_norm_final_g, v_norm_mix_g, v_w_in, v_conv_w, v_conv_b, v_rg_wa, v_rg_ba, v_rg_wx, v_rg_bx, v_rg_lambda, v_sgu_ln_g, v_sgu_ln_b, v_sgu_ws, v_sgu_bs, v_w_proj_a, v_w_proj_b, v_w_out, v_norm_ffn_g, v_w_gate_up, v_w_down, v_norm_final_g):
    args = dict(locals())
    w = {k: args[k] for k in _WEIGHTS}
    m = {k: args["m_" + k] for k in _WEIGHTS}
    v = {k: args["v_" + k] for k in _WEIGHTS}
    for d in (w, m, v):
        for k in _WEIGHTS:
            if k != "norm_final_g":
                d[k] = d[k][0]
    me = _lin(_me())
    me1 = me.reshape(1).astype(jnp.int32)

    order = jnp.bitwise_xor(me, jnp.array(_PASS_FLIPS, jnp.int32)).astype(jnp.int32)
    order_dw = jnp.bitwise_xor(me, jnp.array(_DW_FLIPS, jnp.int32)).astype(jnp.int32)

    gsum, dx, dw = _local_step(x[0], loss_target[0], w, me1, order, order_dw)

    peer = jnp.bitwise_xor(jnp.arange(NDEV, dtype=jnp.int32), me)
    sel_direct = jnp.where(peer == 0, 2, 1).astype(jnp.int32)
    sel_two_level = jnp.where(peer == 0, 2, jnp.where((peer == 1) | (peer % 2 == 0), 1, 0)).astype(jnp.int32)
    grads, delta, new_m, new_v = {}, {}, {}, {}
    for k in _BIG:
        own, recv = dw[k]
        flip = own.shape != w[k].shape
        wmv = [jnp.swapaxes(t, 0, 1) if flip else t for t in (w[k], m[k], v[k])]
        res = _reduce_adamw_call("adamw_" + k, sel_two_level if k == "w_in" else sel_direct, own, recv, *wmv)
        grads[k], delta[k], new_m[k], new_v[k] = (jnp.swapaxes(t, 0, 1) if flip else t for t in res)

    loss = (0.5 / D) * jnp.sum(gsum[14])
    zc = jnp.zeros((4, D), F32)
    d_s, m_s, v_s = _adamw_call("adamw_small", _pack_small(w, zc), gsum, _pack_small(m, zc), _pack_small(v, zc))
    gs, ds, ms, vs = _unpack_small(gsum), _unpack_small(d_s), _unpack_small(m_s), _unpack_small(v_s)
    g_cw = lax.dynamic_slice(gs["conv_w"], (0, me * 128), (4, 128))
    ds["conv_w"], ms["conv_w"], vs["conv_w"] = _adamw_call("adamw_conv_w", w["conv_w"], g_cw, m["conv_w"], v["conv_w"])
    gs["conv_w"] = g_cw
    for k in _WEIGHTS:
        if k not in _BIG:
            shp = w[k].shape
            grads[k], delta[k], new_m[k], new_v[k] = (t[k].reshape(shp) for t in (gs, ds, ms, vs))

    def lift(t, k):
        return t[k] if k == "norm_final_g" else t[k][None]

    outs = [loss, dx[None]]
    for t in (grads, delta, new_m, new_v):
        outs += [lift(t, k) for k in _WEIGHTS]
    return tuple(outs)
```

```python
import functools

import jax
import jax.numpy as jnp
from jax import lax
from jax.experimental import pallas as pl
from jax.experimental.pallas import tpu as pltpu

F32 = jnp.float32
BF = jnp.bfloat16

D = 1024
NDEV = 8
EPS = 1e-6
RG_C = 8.0
CHUNK = 128
NGRP = 8
GW = 128
NQ = 4
QW = 256
RC = 16
SMALL_ROWS = 320
SMALL_PER = SMALL_ROWS // NDEV

ADAM_LR = 0.001
ADAM_B1 = 0.9
ADAM_B2 = 0.999
ADAM_EPS = 1e-08
ADAM_WD = 0.01
ADAM_STEP = 10

VMEM_LIMIT = 60 * 1024 * 1024

MESH = pl.DeviceIdType.MESH


def _rows(n, fn, unroll=2, rc=RC):
    def body(i, c):
        fn(pl.multiple_of(i * rc, rc))
        return c
    lax.fori_loop(0, n // rc, body, 0, unroll=unroll)


def _fold(v):
    return jnp.sum(v.reshape(v.shape[0] // RC, RC, v.shape[1]), axis=0)


def _dot(a, b):
    return jnp.dot(a, b, preferred_element_type=F32)


def _dot_nt(a, b):
    return lax.dot_general(a, b, (((1,), (1,)), ((), ())), preferred_element_type=F32)


def _dot_tn(a, b):
    return lax.dot_general(a, b, (((0,), (0,)), ((), ())), preferred_element_type=F32)


_GC = 0.7978845608028654
_GK = 0.044715


def _gelu(x):
    t = jnp.tanh(_GC * (x + _GK * (x * x * x)))
    return x * (0.5 * (1.0 + t))


def _gelu_grad(x):
    x2 = x * x
    t = jnp.tanh(_GC * (x + _GK * (x2 * x)))
    cdf = 0.5 * (1.0 + t)
    dg = cdf + (0.5 * x) * (1.0 - t * t) * (_GC * (1.0 + (3.0 * _GK) * x2))
    return x * cdf, dg


def _sigmoid(x):
    return jax.nn.sigmoid(x)


def _log1p(e):
    u = 1.0 + e
    d = u - 1.0
    return jnp.where(d == 0.0, e, jnp.log(u) * (e / jnp.where(d == 0.0, 1.0, d)))


def _softplus(z):
    return jnp.maximum(z, 0.0) + _log1p(jnp.exp(-jnp.abs(z)))


def _neg_expm1(z):
    u = jnp.exp(z)
    lu = jnp.log(u)
    k = (1.0 - u) * (z / jnp.where(lu == 0.0, 1.0, lu))
    small = jnp.where(lu == 0.0, -z, k)
    return jnp.where(z > -0.5, small, 1.0 - u)


def _shift_back(prev8, cur, j):
    cat = jnp.concatenate([prev8, cur], axis=0)
    return pltpu.roll(cat, j, 0)[8:8 + cur.shape[0]]


def _shift_fwd(cur, next8, j):
    cat = jnp.concatenate([cur, next8], axis=0)
    n = cat.shape[0]
    return pltpu.roll(cat, n - j, 0)[0:cur.shape[0]]


def _const_spec(shape):
    nd = len(shape)
    return pl.BlockSpec(shape, lambda *_: (0,) * nd, pipeline_mode=pl.Buffered(1))


def _params(sem):
    return pltpu.CompilerParams(dimension_semantics=sem, vmem_limit_bytes=VMEM_LIMIT)


TM_PROJ = 1024
TM_MIX = 256
TM_FFN = 256
TM_DX = 512
TS_DW = 4096


_CHIPS = (4, 2, 6)
_PASS_FLIPS = (0, 1, 4, 2, 6, 5, 3, 7)


class _Gather:
    def __init__(self, ins, outs, send_sems, recv_sems, local_sems):
        self.ins, self.outs = ins, outs
        self.send_sems, self.recv_sems, self.local_sems = send_sems, recv_sems, local_sems
        self.me = _me()
        self.sibling = _flip(self.me, 1)

    def _copy(self, a, kind, block, to, src=None):
        dst = self.outs[a].at[_lin(block)]
        return pltpu.make_async_remote_copy(
            src_ref=dst if src is None else src, dst_ref=dst,
            send_sem=self.send_sems.at[a, kind], recv_sem=self.recv_sems.at[a, kind],
            device_id=to, device_id_type=MESH)

    def _local(self, a):
        return pltpu.make_async_copy(self.ins[a], self.outs[a].at[_lin(self.me)], self.local_sems.at[a])

    def start(self):
        for a in range(len(self.ins)):
            self._local(a).start()
            self._copy(a, 0, self.me, self.sibling, src=self.ins[a]).start()
            for j, f in enumerate(_CHIPS):
                self._copy(a, 1 + j, self.me, _flip(self.me, f), src=self.ins[a]).start()

    def forward(self):
        for j, f in enumerate(_CHIPS):
            for a in range(len(self.ins)):
                self._copy(a, 1 + j, _flip(self.me, f), self.me).wait_recv()
                self._copy(a, 4 + j, _flip(self.me, f), self.sibling).start()

    def finish(self):
        for a in range(len(self.ins)):
            self._copy(a, 0, self.sibling, self.me).wait_recv()
            for j, f in enumerate(_CHIPS):
                self._copy(a, 4 + j, _flip(self.me, f | 1), self.me).wait_recv()
            self._copy(a, 0, self.me, self.sibling, src=self.ins[a]).wait_send()
            for j, f in enumerate(_CHIPS):
                self._copy(a, 1 + j, self.me, _flip(self.me, f), src=self.ins[a]).wait_send()
                self._copy(a, 4 + j, _flip(self.me, f), self.sibling).wait_send()
            self._local(a).wait()


class _Exchange:
    def __init__(self, ins, outs, send_sems, recv_sems, local_sems):
        self.ins, self.outs = ins, outs
        self.send_sems, self.recv_sems, self.local_sems = send_sems, recv_sems, local_sems
        self.me = _me()

    def _copy(self, a, r, outgoing):
        peer = _flip(self.me, r)
        src, dst = (peer, self.me) if outgoing else (self.me, peer)
        return pltpu.make_async_remote_copy(
            src_ref=self.ins[a].at[_lin(src)], dst_ref=self.outs[a].at[_lin(dst)],
            send_sem=self.send_sems.at[a, r - 1], recv_sem=self.recv_sems.at[a, r - 1],
            device_id=peer, device_id_type=MESH)

    def _local(self, a):
        mi = _lin(self.me)
        return pltpu.make_async_copy(self.ins[a].at[mi], self.outs[a].at[mi], self.local_sems.at[a])

    def start(self):
        for a in range(len(self.ins)):
            self._local(a).start()
        for r in range(1, NDEV):
            for a in range(len(self.ins)):
                self._copy(a, r, True).start()

    def finish(self):
        for r in range(1, NDEV):
            for a in range(len(self.ins)):
                self._copy(a, r, False).wait_recv()
        for r in range(1, NDEV):
            for a in range(len(self.ins)):
                self._copy(a, r, True).wait_send()
        for a in range(len(self.ins)):
            self._local(a).wait()


def _gather_sems(n):
    return [pltpu.SemaphoreType.DMA((n, 7)), pltpu.SemaphoreType.DMA((n, 7)), pltpu.SemaphoreType.DMA((n,))]


def _proj_gather_call(x, gmix, w_shard, order, extras):
    s = x.shape[0]
    tm = min(TM_PROJ, s)
    nt = s // tm
    wb = w_shard.shape[1]
    n = len(extras)

    def body(order_ref, x_ref, g_ref, wsh_ref, *rest):
        ex_in = rest[:n]
        proj_hbm, h1_hbm, wall_hbm = rest[n:n + 3]
        ex_out = rest[n + 3:2 * n + 3]
        wall, hc, pbuf, ws_send, ws_recv, ex_send, ex_recv, ex_local, out_sems, psems = rest[2 * n + 3:]
        k = pl.program_id(0)
        i = pl.program_id(1)
        me = _me()
        sibling = _flip(me, 1)
        gather = _Gather(ex_in, ex_out, ex_send, ex_recv, ex_local)

        def wcopy(kind, block, to):
            ref = wall.at[_lin(block)]
            return pltpu.make_async_remote_copy(
                src_ref=ref, dst_ref=ref, send_sem=ws_send.at[kind], recv_sem=ws_recv.at[kind],
                device_id=to, device_id_type=MESH)

        head = i == 0

        @pl.when(head & (k == 0))
        def _():
            wall[_lin(me)] = wsh_ref[...]
            wcopy(0, me, sibling).start()
            for j, f in enumerate(_CHIPS):
                wcopy(1 + j, me, _flip(me, f)).start()
            gather.start()

        out_w = pltpu.make_async_copy(wall, wall_hbm, out_sems.at[0])
        out_h = pltpu.make_async_copy(hc, h1_hbm, out_sems.at[1])

        @pl.when(head & (k == 1))
        def _():
            wcopy(0, sibling, me).wait_recv()
            out_h.start()

        for j, f in enumerate(_CHIPS):
            @pl.when(head & (k == 2 + j))
            def _(j=j, f=f):
                wcopy(1 + j, _flip(me, f), me).wait_recv()
                wcopy(4 + j, _flip(me, f), sibling).start()

            @pl.when(head & (k == 5 + j))
            def _(j=j, f=f):
                wcopy(4 + j, _flip(me, f | 1), me).wait_recv()
                if 5 + j == NDEV - 1:
                    out_w.start()

        pl.when(head & (k == 5))(gather.forward)

        base = pl.multiple_of(i * tm, tm)

        @pl.when(k == 0)
        def _():
            g = g_ref[...]

            def norm(r0):
                xx = x_ref[pl.ds(r0, RC), :]
                r = lax.rsqrt(jnp.mean(xx * xx, axis=-1, keepdims=True) + EPS)
                hc[pl.ds(base + r0, RC), :] = ((xx * r) * g).astype(BF)
            _rows(tm, norm, unroll=4)

        blk = order_ref[k]
        wk = wall.at[blk]
        step = k * nt + i
        slot = lax.rem(step, 2)
        subs = [(r0, min(TM_MIX, tm - r0)) for r0 in range(0, tm, TM_MIX)]

        def out_copy(q, r0, rs):
            return pltpu.make_async_copy(
                pbuf.at[slot, pl.ds(r0, rs)],
                proj_hbm.at[pl.ds(base + r0, rs), pl.ds(pl.multiple_of(blk * wb, 128), wb)], psems.at[slot, q])

        @pl.when(step >= 2)
        def _():
            for q, (r0, rs) in enumerate(subs):
                out_copy(q, r0, rs).wait()

        for q, (r0, rs) in enumerate(subs):
            pbuf[slot, r0:r0 + rs, :] = _dot(hc[pl.ds(base + r0, rs), :], wk[...])
            out_copy(q, r0, rs).start()

        @pl.when((k == NDEV - 1) & (i == nt - 1))
        def _():
            wcopy(0, me, sibling).wait_send()
            for j, f in enumerate(_CHIPS):
                wcopy(1 + j, me, _flip(me, f)).wait_send()
                wcopy(4 + j, _flip(me, f), sibling).wait_send()
            gather.finish()
            out_w.wait()
            out_h.wait()
            for q, (r0, rs) in enumerate(subs):
                out_copy(q, r0, rs).wait()
                pltpu.make_async_copy(pbuf.at[1 - slot, pl.ds(r0, rs)], proj_hbm.at[pl.ds(r0, rs), pl.ds(0, wb)],
                                      psems.at[1 - slot, q]).wait()

    any_spec = pl.BlockSpec(memory_space=pl.ANY)
    grid_spec = pltpu.PrefetchScalarGridSpec(
        num_scalar_prefetch=1, grid=(NDEV, nt),
        in_specs=[pl.BlockSpec((tm, D), lambda k, i, o: (jnp.where(k == 0, i, nt - 1), 0)),
                  pl.BlockSpec((1, D), lambda k, i, o: (0, 0)),
                  pl.BlockSpec(w_shard.shape, lambda k, i, o: (0, 0))] + [any_spec] * n,
        out_specs=[any_spec, any_spec, any_spec] + [any_spec] * n,
        scratch_shapes=[pltpu.VMEM((NDEV,) + w_shard.shape, BF), pltpu.VMEM((s, D), BF), pltpu.VMEM((2, tm, wb), F32),
                        pltpu.SemaphoreType.DMA((7,)), pltpu.SemaphoreType.DMA((7,))] + _gather_sems(n)
        + [pltpu.SemaphoreType.DMA((2,)), pltpu.SemaphoreType.DMA((2, -(-tm // TM_MIX)))])
    outs = pl.pallas_call(
        body, name="proj_gather", grid_spec=grid_spec,
        out_shape=[jax.ShapeDtypeStruct((s, NDEV * wb), F32), jax.ShapeDtypeStruct((s, D), BF),
                   jax.ShapeDtypeStruct((NDEV,) + w_shard.shape, BF)]
        + [jax.ShapeDtypeStruct((NDEV,) + e.shape, e.dtype) for e in extras],
        compiler_params=_params(("arbitrary", "arbitrary")),
    )(order, x, gmix, w_shard, *extras)
    return outs[0], outs[1], outs[2], outs[3:]


def _conv_tile(rx, prev8, cw_ref, cb):
    xc = cb + cw_ref[3:4, :] * rx
    for j in (1, 2, 3):
        xc = xc + cw_ref[3 - j:4 - j, :] * _shift_back(prev8, rx, j)
    return xc


def _gate_tile(xcb, wax_ref, ba, bx, sp, q):
    cs = slice(q * QW, (q + 1) * QW)
    z = _dot(xcb[:, cs], wax_ref[q])
    r = _sigmoid(z[:, 0:QW] + ba[:, cs])
    ig = _sigmoid(z[:, QW:2 * QW] + bx[:, cs])
    return r, ig, (-RG_C * r) * sp[:, cs]


def _scan_fwd(a_s, b_s, out_ref, h0, n):
    rowi = lax.broadcasted_iota(jnp.int32, (8, D), 0)

    def block(t, h):
        rows = pl.ds(t * 8, 8)
        a = a_s[rows, :]
        b = b_s[rows, :]
        for d in (1, 2, 4):
            m = rowi >= d
            b = jnp.where(m, a * pltpu.roll(b, d, 0) + b, b)
            a = jnp.where(m, a * pltpu.roll(a, d, 0), a)
        hb = b + a * h
        out_ref[rows, :] = hb
        return hb[7:8, :]
    h = h0
    for t in range(n // 8):
        h = block(t, h)
    return h


def _scan_bwd(a_s, lm_s, c0, n):
    rowi = lax.broadcasted_iota(jnp.int32, (8, D), 0)
    nblk = n // 8

    def block(k, cin):
        rows = pl.ds((nblk - 1 - k) * 8, 8)
        a = a_s[rows, :]
        dh = lm_s[rows, :]
        b = a * dh
        for d in (1, 2, 4):
            m = rowi < 8 - d
            b = jnp.where(m, a * pltpu.roll(b, 8 - d, 0) + b, b)
            a = jnp.where(m, a * pltpu.roll(a, 8 - d, 0), a)
        mu = b + a * cin
        lm_s[rows, :] = dh + jnp.where(rowi < 7, pltpu.roll(mu, 7, 0), cin)
        return mu[0:1, :]
    c = c0
    for k in range(nblk):
        c = block(k, c)
    return c


def _mixer_fwd_call(proj, x, cw, cb, ba, bx, lam, lng, lnb, wax, wtr, bias, wpa, wpb, wo, shards):
    s = x.shape[0]
    tm = min(TM_MIX, s)
    nt = s // tm
    pw = proj.shape[1]
    n = len(shards)

    def body(proj_ref, x_ref, cw_ref, cb_ref, ba_ref, bx_ref, lam_ref, lng_ref, lnb_ref, wax_ref, wtr_ref,
             bias_ref, wpa_ref, wpb_ref, wo_ref, *rest):
        sh_in = rest[:n]
        h_ref, pa_ref, pb_ref, x1_ref, xc_ref, r_ref, ig_ref, a_s, m_ref, ya_ref, yb_ref, mg_ref = rest[n:n + 12]
        sh_out = rest[n + 12:2 * n + 12]
        prev_s, b_s, hc_s, vn_s, mx_s, g_send, g_recv, g_local = rest[2 * n + 12:]
        i = pl.program_id(0)
        gather = _Gather(sh_in, sh_out, g_send, g_recv, g_local)

        @pl.when(i == 0)
        def _():
            gather.start()
            prev_s[...] = jnp.zeros((8, D), F32)
            hc_s[...] = jnp.zeros((8, D), F32)

        pl.when(i == nt // 2)(gather.forward)

        rx = proj_ref[:, 0:D]
        xc = _conv_tile(rx, prev_s[...], cw_ref, cb_ref[...])
        prev_s[...] = rx[tm - 8:tm, :]
        xc_ref[...] = xc.astype(BF)
        xcb = xc.astype(BF)
        sp = _softplus(-lam_ref[...])
        ba = ba_ref[...]
        bx = bx_ref[...]
        for q in range(NQ):
            cs = slice(q * QW, (q + 1) * QW)
            r, ig, la = _gate_tile(xcb, wax_ref, ba, bx, sp, q)
            mq = jnp.sqrt(_neg_expm1(2.0 * la))
            r_ref[:, cs] = r.astype(BF)
            ig_ref[:, cs] = ig.astype(BF)
            m_ref[:, cs] = mq.astype(BF)
            a_s[:, cs] = jnp.exp(la)
            b_s[:, cs] = mq * (ig * xc[:, cs])

        gv = _gelu(proj_ref[:, 3 * D:4 * D])
        dv = gv - jnp.mean(gv, axis=-1, keepdims=True)
        var = jnp.mean(dv * dv, axis=-1, keepdims=True)
        vn_s[...] = ((dv * lax.rsqrt(var + EPS)) * lng_ref[...] + lnb_ref[...]).astype(BF)
        nc = tm // CHUNK
        for c in range(nc):
            rs = slice(c * CHUNK, (c + 1) * CHUNK)
            for g in range(NGRP):
                cs = slice(g * GW, (g + 1) * GW)
                mx_s[rs, cs] = _dot(wtr_ref[g], vn_s[rs, cs])
        mixed = mx_s[...] + jnp.concatenate([bias_ref[...]] * nc, axis=0)
        yb = (_gelu(proj_ref[:, 2 * D:3 * D]) * mixed).astype(BF)
        yb_ref[...] = yb
        pb = _dot(yb, wpb_ref[...])
        pb_ref[...] = pb.astype(BF)
        mx_s[...] = pb

        hc_s[0:1, :] = _scan_fwd(a_s, b_s, h_ref, hc_s[0:1, :], tm)

        ya = (_gelu(proj_ref[:, D:2 * D]) * h_ref[...]).astype(BF)
        ya_ref[...] = ya
        pa = _dot(ya, wpa_ref[...])
        pa_ref[...] = pa.astype(BF)
        mg = (_sigmoid(proj_ref[:, 4 * D:5 * D]) * pa + _sigmoid(proj_ref[:, 5 * D:6 * D]) * mx_s[...]).astype(BF)
        mg_ref[...] = mg
        x1_ref[...] = x_ref[...] + _dot(mg, wo_ref[...])

        pl.when(i == nt - 1)(gather.finish)

    tile = lambda w: pl.BlockSpec((tm, w), lambda i: (i, 0))
    vec = _const_spec((1, D))
    any_spec = pl.BlockSpec(memory_space=pl.ANY)
    outs = pl.pallas_call(
        body, name="mixer_fwd", grid=(nt,),
        in_specs=[tile(pw), tile(D), _const_spec((4, D)), vec, vec, vec, vec, vec, vec,
                  _const_spec(wax.shape), _const_spec(wtr.shape), _const_spec(bias.shape),
                  _const_spec((D, D)), _const_spec((D, D)), _const_spec((D, D))] + [any_spec] * n,
        out_specs=[tile(D)] * 12 + [any_spec] * n,
        out_shape=[jax.ShapeDtypeStruct((s, D), dt) for dt in (F32, BF, BF, F32, BF, BF, BF, F32, BF, BF, BF, BF)]
        + [jax.ShapeDtypeStruct((NDEV,) + e.shape, e.dtype) for e in shards],
        scratch_shapes=[pltpu.VMEM((8, D), F32), pltpu.VMEM((tm, D), F32),
                        pltpu.VMEM((8, D), F32), pltpu.VMEM((tm, D), BF), pltpu.VMEM((tm, D), F32)]
        + _gather_sems(n),
        compiler_params=_params(("arbitrary",)),
    )(proj, x, cw, cb, ba, bx, lam, lng, lnb, wax, wtr, bias, wpa, wpb, wo, *shards)
    return outs[:12], outs[12:]


def _ffn_call(x1, tgt, gffn, gfin, wgu, wdn):
    s = x1.shape[0]
    tm = min(TM_FFN, s)
    nt = s // tm
    nh = wdn.shape[0]
    fb = wgu.shape[2]

    def body(x1_ref, tgt_ref, gffn_ref, gfin_ref, wgu_ref, wdn_ref,
             dx1_ref, dx1b_ref, h2_ref, act_ref, dgu_ref, dx2b_ref, acc_ref,
             g_s, u_s, dx2_s, accs):
        i = pl.program_id(0)

        @pl.when(i == 0)
        def _():
            accs[...] = jnp.zeros(accs.shape, F32)

        gffn = gffn_ref[...]
        gfin = gfin_ref[...]

        x1 = x1_ref[...]
        r2 = lax.rsqrt(jnp.mean(x1 * x1, axis=-1, keepdims=True) + EPS)
        xh2 = x1 * r2
        h2 = (xh2 * gffn).astype(BF)
        h2_ref[...] = h2

        for k in range(nh):
            g = _dot(h2, wgu_ref[k])
            u = _dot(h2, wgu_ref[k + nh])
            g_s[k] = g
            u_s[k] = u
            act_ref[k] = ((g * _sigmoid(g)) * u).astype(BF)

        x2 = x1
        for k in range(nh):
            x2 = x2 + _dot(act_ref[k], wdn_ref[k])

        r3 = lax.rsqrt(jnp.mean(x2 * x2, axis=-1, keepdims=True) + EPS)
        xh = x2 * r3
        err = xh * gfin - tgt_ref[...]
        accs[2] += _fold(err * err)
        dy = err * (1.0 / D)
        accs[1] += _fold(dy * xh)
        dxh = dy * gfin
        dx2 = r3 * (dxh - xh * jnp.mean(dxh * xh, axis=-1, keepdims=True))
        dx2_s[...] = dx2
        dx2b = dx2.astype(BF)
        dx2b_ref[...] = dx2b

        for k in range(nh):
            da = _dot_nt(dx2b, wdn_ref[k])
            g = g_s[k]
            sg = _sigmoid(g)
            dgu_ref[k] = ((da * u_s[k]) * (sg * (1.0 + g * (1.0 - sg)))).astype(BF)
            dgu_ref[k + nh] = (da * (g * sg)).astype(BF)

        dh2 = _dot_nt(dgu_ref[0], wgu_ref[0])
        for k in range(1, 2 * nh):
            dh2 = dh2 + _dot_nt(dgu_ref[k], wgu_ref[k])

        accs[0] += _fold(dh2 * xh2)
        dxh = dh2 * gffn
        dx1 = dx2_s[...] + r2 * (dxh - xh2 * jnp.mean(dxh * xh2, axis=-1, keepdims=True))
        dx1_ref[...] = dx1
        dx1b_ref[...] = dx1.astype(BF)

        @pl.when(i == nt - 1)
        def _():
            acc_ref[...] = jnp.zeros((8, D), F32)
            for j in range(3):
                acc_ref[j:j + 1, :] = jnp.sum(accs[j], axis=0, keepdims=True)

    tile = lambda w: pl.BlockSpec((tm, w), lambda i: (i, 0))
    vec = _const_spec((1, D))
    return pl.pallas_call(
        body, name="ffn", grid=(nt,),
        in_specs=[tile(D), tile(D), vec, vec, _const_spec(wgu.shape), _const_spec(wdn.shape)],
        out_specs=[tile(D), tile(D), tile(D),
                   pl.BlockSpec((nh, tm, fb), lambda i: (0, i, 0)),
                   pl.BlockSpec((2 * nh, tm, fb), lambda i: (0, i, 0)),
                   tile(D), pl.BlockSpec((8, D), lambda i: (0, 0))],
        out_shape=[jax.ShapeDtypeStruct((s, D), F32), jax.ShapeDtypeStruct((s, D), BF),
                   jax.ShapeDtypeStruct((s, D), BF), jax.ShapeDtypeStruct((nh, s, fb), BF),
                   jax.ShapeDtypeStruct((2 * nh, s, fb), BF), jax.ShapeDtypeStruct((s, D), BF),
                   jax.ShapeDtypeStruct((8, D), F32)],
        scratch_shapes=[pltpu.VMEM((nh, tm, fb), F32), pltpu.VMEM((nh, tm, fb), F32), pltpu.VMEM((tm, D), F32),
                        pltpu.VMEM((3, RC, D), F32)],
        compiler_params=_params(("arbitrary",)),
    )(x1, tgt, gffn, gfin, wgu, wdn)


def _mixer_bwd_pre_call(dx1b, proj, pa, pb, lng, lnb, wtr, wtrt, bias, wpa, wpb, wo, parts):
    s = dx1b.shape[0]
    tm = min(TM_MIX, s)
    nt = s // tm
    pw = proj.shape[1]
    n = len(parts)

    def body(dx1b_ref, uv_ref, gg_ref, pa_ref, pb_ref, lng_ref, lnb_ref, wtr_ref, wtrt_ref, bias_ref,
             wpa_ref, wpb_ref, wo_ref, *rest):
        ex_in = rest[:n]
        dproj_ref, dya_ref, dpa_ref, dpb_ref, vec_ref, dws_ref, dbs_ref = rest[n:n + 7]
        ex_out = rest[n + 7:2 * n + 7]
        vn_s, mx_s, dmx_s, dvn_s, accs, dbs_s, e_send, e_recv, e_local = rest[2 * n + 7:]
        i = pl.program_id(0)
        exchange = _Exchange(ex_in, ex_out, e_send, e_recv, e_local)

        @pl.when(i == 0)
        def _():
            exchange.start()
            accs[...] = jnp.zeros(accs.shape, F32)
            dbs_s[...] = jnp.zeros(dbs_s.shape, F32)
            dws_ref[...] = jnp.zeros(dws_ref.shape, F32)

        dm = _dot_nt(dx1b_ref[...], wo_ref[...])
        sa = _sigmoid(gg_ref[:, 0:D])
        sb = _sigmoid(gg_ref[:, D:2 * D])
        dpa = dm * sa
        dpb = dm * sb
        dpab = dpa.astype(BF)
        dpbb = dpb.astype(BF)
        dpa_ref[...] = dpab
        dpb_ref[...] = dpbb
        dproj_ref[:, 2 * D:3 * D] = ((dpa * pa_ref[...].astype(F32)) * (1.0 - sa)).astype(BF)
        dproj_ref[:, 3 * D:4 * D] = ((dpb * pb_ref[...].astype(F32)) * (1.0 - sb)).astype(BF)

        dya_ref[...] = _dot_nt(dpab, wpa_ref[...])
        dyb = _dot_nt(dpbb, wpb_ref[...])

        lng = lng_ref[...]
        gv, dgelu_v = _gelu_grad(uv_ref[:, D:2 * D])
        dv = gv - jnp.mean(gv, axis=-1, keepdims=True)
        rstd = lax.rsqrt(jnp.mean(dv * dv, axis=-1, keepdims=True) + EPS)
        xh = dv * rstd
        vn_s[...] = (xh * lng + lnb_ref[...]).astype(BF)

        nc = tm // CHUNK
        for c in range(nc):
            rs = slice(c * CHUNK, (c + 1) * CHUNK)
            for g in range(NGRP):
                cs = slice(g * GW, (g + 1) * GW)
                mx_s[rs, cs] = _dot(wtr_ref[g], vn_s[rs, cs])

        gu, dgelu_u = _gelu_grad(uv_ref[:, 0:D])
        mixed = mx_s[...] + jnp.concatenate([bias_ref[...]] * nc, axis=0)
        dproj_ref[:, 0:D] = ((dyb * mixed) * dgelu_u).astype(BF)
        dmx = dyb * gu
        dmx_s[...] = dmx.astype(BF)
        dbs_s[...] += jnp.sum(dmx.reshape(nc, CHUNK, D), axis=0)

        for c in range(nc):
            rs = slice(c * CHUNK, (c + 1) * CHUNK)
            for g in range(NGRP):
                cs = slice(g * GW, (g + 1) * GW)
                dvn_s[rs, cs] = _dot(wtrt_ref[g], dmx_s[rs, cs])
                dws_ref[g] += _dot_nt(dmx_s[rs, cs], vn_s[rs, cs])

        dvn = dvn_s[...]
        accs[0] += _fold(dvn * xh)
        accs[1] += _fold(dvn)
        dxh = dvn * lng
        m1 = jnp.mean(dxh, axis=-1, keepdims=True)
        m2 = jnp.mean(dxh * xh, axis=-1, keepdims=True)
        dproj_ref[:, D:2 * D] = ((rstd * (dxh - m1 - xh * m2)) * dgelu_v).astype(BF)

        @pl.when(i == nt - 1)
        def _():
            vec_ref[...] = jnp.zeros((8, D), F32)
            for j in range(2):
                vec_ref[j:j + 1, :] = jnp.sum(accs[j], axis=0, keepdims=True)
            row = lax.broadcasted_iota(jnp.int32, (CHUNK, CHUNK), 0)
            col = lax.broadcasted_iota(jnp.int32, (CHUNK, CHUNK), 1)
            for g in range(NGRP):
                dws_ref[g] = jnp.where(row >= col, dws_ref[g], 0.0)
                gs = jnp.sum(dbs_s[:, g * GW:(g + 1) * GW], axis=1, keepdims=True)
                dbs_ref[:, g * GW:(g + 1) * GW] = jnp.broadcast_to(gs, (CHUNK, GW))
            exchange.finish()

    tile = lambda w: pl.BlockSpec((tm, w), lambda i: (i, 0))
    vec = _const_spec((1, D))
    any_spec = pl.BlockSpec(memory_space=pl.ANY)
    outs = pl.pallas_call(
        body, name="mixer_bwd_pre", grid=(nt,),
        in_specs=[tile(D), pl.BlockSpec((tm, 2 * D), lambda i: (i, 1)), pl.BlockSpec((tm, 2 * D), lambda i: (i, 2)),
                  tile(D), tile(D), vec, vec, _const_spec(wtr.shape), _const_spec(wtrt.shape),
                  _const_spec(bias.shape), _const_spec((D, D)), _const_spec((D, D)), _const_spec((D, D))]
        + [any_spec] * n,
        out_specs=[tile(4 * D), tile(D), tile(D), tile(D), pl.BlockSpec((8, D), lambda i: (0, 0)),
                   pl.BlockSpec((NGRP, CHUNK, CHUNK), lambda i: (0, 0, 0)),
                   pl.BlockSpec((CHUNK, D), lambda i: (0, 0))] + [any_spec] * n,
        out_shape=[jax.ShapeDtypeStruct((s, 4 * D), BF), jax.ShapeDtypeStruct((s, D), F32),
                   jax.ShapeDtypeStruct((s, D), BF), jax.ShapeDtypeStruct((s, D), BF),
                   jax.ShapeDtypeStruct((8, D), F32), jax.ShapeDtypeStruct((NGRP, CHUNK, CHUNK), F32),
                   jax.ShapeDtypeStruct((CHUNK, D), F32)]
        + [jax.ShapeDtypeStruct(p.shape, p.dtype) for p in parts],
        scratch_shapes=[pltpu.VMEM((tm, D), BF), pltpu.VMEM((tm, D), F32), pltpu.VMEM((tm, D), BF),
                        pltpu.VMEM((tm, D), F32), pltpu.VMEM((2, RC, D), F32), pltpu.VMEM((CHUNK, D), F32)]
        + _gather_sems(n),
        compiler_params=_params(("arbitrary",)),
    )(dx1b, proj, proj, pa, pb, lng, lnb, wtr, wtrt, bias, wpa, wpb, wo, *parts)
    return outs[:7], outs[7:]


def _mixer_bwd_seq_call(dprojb, dya, proj, h, x, dx1, gates, gmix, w_in_all, cw, lam, wax, parts):
    s = dya.shape[0]
    tm = min(TM_MIX, s)
    nt = s // tm
    tb = tm // 8
    n = len(parts)
    nb, _, wb = w_in_all.shape

    def body(dpb_ref, dya_ref, xg_ref, xh8_ref, h_ref, hh8_ref, x_ref, dx1_ref, xc_s, r_s, ig_s, a_s, m_s,
             gmix_ref, win_ref, cw_ref, lam_ref, wax_ref, *rest):
        ex_in = rest[:n]
        dpa_ref, dx_ref, vec_ref, dwax_ref = rest[n:n + 4]
        ex_out = rest[n + 4:2 * n + 4]
        lm_s, dh_s, dxc_s, c_s, accs, e_send, e_recv, e_local = rest[2 * n + 4:]
        i = pl.program_id(0)
        exchange = _Exchange(ex_in, ex_out, e_send, e_recv, e_local)

        @pl.when(i == 0)
        def _():
            exchange.start()
            accs[...] = jnp.zeros(accs.shape, F32)
            dwax_ref[...] = jnp.zeros(dwax_ref.shape, F32)
            c_s[...] = jnp.zeros((8, D), F32)
            dxc_s[tm:tm + 8, :] = jnp.zeros((8, D), F32)

        first_tile = i == nt - 1
        prev8 = jnp.where(first_tile, 0.0, xh8_ref[...])
        hprev8 = jnp.where(first_tile, 0.0, hh8_ref[...])
        lamv = lam_ref[...]
        sp = _softplus(-lamv)
        hv = h_ref[...]
        g, dg = _gelu_grad(xg_ref[:, D:2 * D])
        dya = dya_ref[...]
        lm_s[...] = dya * g
        drg = ((dya * hv) * dg).astype(BF)
        dpa_ref[:, D:2 * D] = drg
        dpa_ref[:, 2 * D:6 * D] = dpb_ref[...]

        def dpb_block(k):
            return _dot_nt(dpb_ref[:, k * wb - 2 * D:(k + 1) * wb - 2 * D], win_ref[k])
        dh = (_dot_nt(drg[:, 0:2 * wb - D], win_ref[1, :, D - wb:wb])
              + _dot_nt(drg[:, 2 * wb - D:D], win_ref[2, :, 0:2 * D - 2 * wb])
              + _dot_nt(dpb_ref[:, 0:3 * wb - 2 * D], win_ref[2, :, 2 * D - 2 * wb:wb]))
        for k in range(3, 6):
            dh = dh + dpb_block(k)
        dh_s[...] = dh

        c_s[0:1, :] = _scan_bwd(a_s, lm_s, c_s[0:1, :], tm)

        hprev = _shift_back(hprev8, h_ref[...], 1)
        for q in range(NQ):
            cs = slice(q * QW, (q + 1) * QW)
            r = r_s[:, cs].astype(F32)
            ig = ig_s[:, cs].astype(F32)
            a = a_s[:, cs]
            m = m_s[:, cs].astype(F32)
            lm = lm_s[:, cs]
            xq = xc_s[:, cs].astype(F32)
            dixc = lm * m
            dla = (lm * hprev[:, cs]) * a - ((lm * (ig * xq)) * (a * a)) / m
            accs[3, :, cs] += _fold(dla * r)
            dza = (dla * (-RG_C * sp[:, cs])) * (r * (1.0 - r))
            dzx = (dixc * xq) * (ig * (1.0 - ig))
            accs[1, :, cs] += _fold(dza)
            accs[2, :, cs] += _fold(dzx)
            dz = jnp.concatenate([dza, dzx], axis=1).astype(BF)
            dxc_s[0:tm, cs] = dixc * ig + _dot_nt(dz, wax_ref[q])
            dwax_ref[q] += _dot_tn(xc_s[:, cs], dz)

        cur = dxc_s[0:tm, :]
        nx = dxc_s[tm:tm + 8, :]
        drx = cw_ref[3:4, :] * cur
        for j in (1, 2, 3):
            drx = drx + cw_ref[3 - j:4 - j, :] * _shift_fwd(cur, nx, j)
        accs[0] += _fold(cur)
        rx = xg_ref[:, 0:D]
        accs[7] += _fold(cur * rx)
        for j in (1, 2, 3):
            accs[7 - j] += _fold(cur * _shift_back(prev8, rx, j))
        dxc_s[tm:tm + 8, :] = cur[0:8, :]

        drxb = drx.astype(BF)
        dpa_ref[:, 0:D] = drxb
        dh = dh_s[...] + _dot_nt(drxb[:, 0:wb], win_ref[0]) + _dot_nt(drxb[:, wb:D], win_ref[1, :, 0:D - wb])
        for k in range(6, nb):
            dh = dh + dpb_block(k)
        xx = x_ref[...]
        rn = lax.rsqrt(jnp.mean(xx * xx, axis=-1, keepdims=True) + EPS)
        xhn = xx * rn
        accs[8] += _fold(dh * xhn)
        dxh = dh * gmix_ref[...]
        dx_ref[...] = dx1_ref[...] + rn * (dxh - xhn * jnp.mean(dxh * xhn, axis=-1, keepdims=True))

        @pl.when(i == nt - 1)
        def _():
            vec_ref[...] = jnp.zeros((16, D), F32)
            for j in range(9):
                vec_ref[j:j + 1, :] = jnp.sum(accs[j], axis=0, keepdims=True)
            vec_ref[3:4, :] = vec_ref[3:4, :] * (RG_C * _sigmoid(-lamv))
            exchange.finish()

    rev = lambda w: pl.BlockSpec((tm, w), lambda i: (nt - 1 - i, 0))
    halo = pl.BlockSpec((8, D), lambda i: (jnp.maximum((nt - 1 - i) * tb - 1, 0), 0))
    vec = _const_spec((1, D))
    any_spec = pl.BlockSpec(memory_space=pl.ANY)
    outs = pl.pallas_call(
        body, name="mixer_bwd_seq", grid=(nt,),
        in_specs=[rev(4 * D), rev(D), rev(2 * D), halo, rev(D), halo] + [rev(D)] * 7
        + [vec, _const_spec(w_in_all.shape), _const_spec((4, D)), vec, _const_spec(wax.shape)] + [any_spec] * n,
        out_specs=[rev(6 * D), rev(D), pl.BlockSpec((16, D), lambda i: (0, 0)),
                   pl.BlockSpec((NQ, QW, 2 * QW), lambda i: (0, 0, 0))] + [any_spec] * n,
        out_shape=[jax.ShapeDtypeStruct((s, 6 * D), BF), jax.ShapeDtypeStruct((s, D), F32),
                   jax.ShapeDtypeStruct((16, D), F32), jax.ShapeDtypeStruct((NQ, QW, 2 * QW), F32)]
        + [jax.ShapeDtypeStruct(p.shape, p.dtype) for p in parts],
        scratch_shapes=[pltpu.VMEM((tm, D), F32), pltpu.VMEM((tm, D), F32),
                        pltpu.VMEM((tm + 8, D), F32), pltpu.VMEM((8, D), F32), pltpu.VMEM((9, RC, D), F32)]
        + _gather_sems(n),
        compiler_params=_params(("arbitrary",)),
    )(dprojb, dya, proj, proj, h, h, x, dx1, *gates, gmix, w_in_all, cw, lam, wax, *parts)
    return outs[:4], outs[4:]


def _device_of(d):
    return (d // 4, lax.rem(d // 2, 2), lax.rem(d, 2))


_DW_PLAN = (("C", 3), ("A", 3), ("C", 1), ("A", 1), ("C", 2), ("A", 2), ("S", 0), ("O", 0))
_DW_FLIPS = tuple({"C": 2 * j + 1, "A": 2 * j, "S": 1, "O": 0}[kind] for kind, j in _DW_PLAN)


def _dw_exchange_call(name, order, a, b, a_spec, b_spec, k1, n1, s, small):
    ts = min(TS_DW, s)
    ns = s // ts
    nstep = len(_DW_PLAN)

    def slab(i, order_ref):
        return order_ref[i]

    def body(order_ref, a_ref, b_ref, g_ref, own_ref, recv_ref, gsum_ref, acc, sbuf, stage, send_sems, recv_sems,
             st_send, st_recv, local_sem, rbuf, gacc, send1, recv1, send2, recv2):
        i = pl.program_id(0)
        j = pl.program_id(1)
        me = _me()
        mi = _lin(me)
        sibling = _flip(me, 1)
        allreduce = _SmallAllReduce(g_ref, gacc, rbuf, send1, recv1, send2, recv2)
        pl.when((i == 0) & (j == 0))(allreduce.scatter)
        pl.when((i == nstep // 2) & (j == 0))(allreduce.reduce)
        p = _dot(a_ref[...], b_ref[...])

        @pl.when(j == 0)
        def _():
            acc[...] = p

        @pl.when(j > 0)
        def _():
            acc[...] += p

        def send(step):
            kind, jj = _DW_PLAN[step]
            src = sbuf.at[step % 2]
            if kind == "C":
                return pltpu.make_async_remote_copy(
                    src_ref=src, dst_ref=stage.at[jj - 1], send_sem=st_send.at[jj - 1], recv_sem=st_recv.at[jj - 1],
                    device_id=sibling, device_id_type=MESH)
            to = sibling if kind == "S" else _flip(me, 2 * jj)
            return pltpu.make_async_remote_copy(
                src_ref=src, dst_ref=recv_ref.at[mi], send_sem=send_sems.at[jj], recv_sem=recv_sems.at[jj],
                device_id=to, device_id_type=MESH)

        def arrival(jj):
            frm = sibling if jj == 0 else _flip(me, 2 * jj)
            return pltpu.make_async_remote_copy(
                src_ref=sbuf.at[0], dst_ref=recv_ref.at[_lin(frm)], send_sem=send_sems.at[jj],
                recv_sem=recv_sems.at[jj], device_id=frm, device_id_type=MESH)

        def staged(jj):
            return pltpu.make_async_remote_copy(
                src_ref=sbuf.at[0], dst_ref=stage.at[jj - 1], send_sem=st_send.at[jj - 1], recv_sem=st_recv.at[jj - 1],
                device_id=sibling, device_id_type=MESH)

        for step, (kind, jj) in enumerate(_DW_PLAN):
            @pl.when((i == step) & (j == ns - 1))
            def _(step=step, kind=kind, jj=jj):
                if step >= 2:
                    send(step - 2).wait_send()
                if kind == "A":
                    staged(jj).wait_recv()
                    sbuf[step % 2] = (acc[...] + stage[jj - 1].astype(F32)).astype(BF)
                else:
                    sbuf[step % 2] = acc[...].astype(BF)
                if kind != "O":
                    send(step).start()
                else:
                    own_ref[...] = acc[...]
                    mine = pltpu.make_async_copy(sbuf.at[step % 2], recv_ref.at[mi], local_sem)
                    mine.start()
                    send(step - 1).wait_send()
                    for q in range(4):
                        arrival(q).wait_recv()
                    mine.wait()
                    allreduce.finish()
                    gsum_ref[...] = gacc[...]

    vmem = pl.BlockSpec(memory_space=pltpu.VMEM)
    grid_spec = pltpu.PrefetchScalarGridSpec(
        num_scalar_prefetch=1, grid=(nstep, ns),
        in_specs=[a_spec(ts, slab), b_spec(ts, slab), vmem],
        out_specs=[pl.BlockSpec((k1, n1), lambda i, j, o: (0, 0)), pl.BlockSpec(memory_space=pl.ANY), vmem],
        scratch_shapes=[pltpu.VMEM((k1, n1), F32), pltpu.VMEM((2, k1, n1), BF), pltpu.VMEM((3, k1, n1), BF),
                        pltpu.SemaphoreType.DMA((4,)), pltpu.SemaphoreType.DMA((4,)),
                        pltpu.SemaphoreType.DMA((3,)), pltpu.SemaphoreType.DMA((3,)),
                        pltpu.SemaphoreType.DMA(()), pltpu.VMEM((NDEV, SMALL_PER, D), F32),
                        pltpu.VMEM((SMALL_ROWS, D), F32)]
        + [pltpu.SemaphoreType.DMA((NDEV - 1,))] * 4)
    return pl.pallas_call(
        body, name=name, grid_spec=grid_spec,
        out_shape=[jax.ShapeDtypeStruct((k1, n1), F32), jax.ShapeDtypeStruct((NDEV, k1, n1), BF),
                   jax.ShapeDtypeStruct((SMALL_ROWS, D), F32)],
        compiler_params=_params(("arbitrary", "arbitrary")),
    )(order, a, b, small)


def _dw_plain_call(name, me, a, b, a_spec, b_spec, split, k1, n1, s):
    nb = NDEV // split
    r = k1 // split
    ts = min(TS_DW, s)
    ns = s // ts

    def slab(i, me_ref):
        return i

    def body(me_ref, a_ref, b_ref, own_ref, part_ref, acc):
        i = pl.program_id(0)
        j = pl.program_id(1)
        p = _dot_tn(a_ref[...], b_ref[...])

        @pl.when(j == 0)
        def _():
            acc[...] = p

        @pl.when(j > 0)
        def _():
            acc[...] += p

        @pl.when(j == ns - 1)
        def _():
            part_ref[...] = acc[...].astype(BF)

            @pl.when(i == me_ref[0] // split)
            def _():
                off = pl.multiple_of(lax.rem(me_ref[0], split) * r, RC)
                own_ref[...] = acc[pl.ds(off, r), :]

    grid_spec = pltpu.PrefetchScalarGridSpec(
        num_scalar_prefetch=1, grid=(nb, ns),
        in_specs=[a_spec(ts, slab), b_spec(ts, slab)],
        out_specs=[pl.BlockSpec((r, n1), lambda i, j, me_ref: (0, 0)),
                   pl.BlockSpec((None, k1, n1), lambda i, j, me_ref: (i, 0, 0))],
        scratch_shapes=[pltpu.VMEM((k1, n1), F32)])
    own, part = pl.pallas_call(
        body, name=name, grid_spec=grid_spec,
        out_shape=[jax.ShapeDtypeStruct((r, n1), F32), jax.ShapeDtypeStruct((nb, k1, n1), BF)],
        compiler_params=_params(("arbitrary", "arbitrary")),
    )(me, a, b)
    return own, part.reshape(NDEV, r, n1)


def _rows2d(w):
    return lambda ts, slab: pl.BlockSpec((ts, w), lambda i, j, me_ref: (j, 0))


def _rows2d_t(w):
    return lambda ts, slab: pl.BlockSpec((w, ts), lambda i, j, me_ref: (0, j))


def _transpose_call(name, a):
    s, w = a.shape
    ts = min(TM_PROJ, s)

    def body(a_ref, o_ref):
        o_ref[...] = a_ref[...].T

    return pl.pallas_call(
        body, name=name, grid=(s // ts,),
        in_specs=[pl.BlockSpec((ts, w), lambda i: (i, 0))], out_specs=pl.BlockSpec((w, ts), lambda i: (0, i)),
        out_shape=jax.ShapeDtypeStruct((w, s), a.dtype), compiler_params=_params(("parallel",)),
    )(a)


def _cols2d(w):
    return lambda ts, slab: pl.BlockSpec((ts, w), lambda i, j, me_ref: (j, slab(i, me_ref)))


def _blk3d(w):
    return lambda ts, slab: pl.BlockSpec((None, ts, w), lambda i, j, me_ref: (slab(i, me_ref), j, 0))


_BC1 = 1.0 - ADAM_B1 ** ADAM_STEP
_BC2 = 1.0 - ADAM_B2 ** ADAM_STEP


def _adamw_math(w, g, m, v):
    m = ADAM_B1 * m + (1.0 - ADAM_B1) * g
    v = ADAM_B2 * v + (1.0 - ADAM_B2) * (g * g)
    m_hat = m / _BC1
    v_hat = v / _BC2
    delta = -ADAM_LR * (m_hat / (jnp.sqrt(v_hat) + ADAM_EPS) + ADAM_WD * w)
    return delta, m, v


def _row_tile(r):
    for t in (256, 176, 128, 64, 32, 16, 8):
        if r % t == 0:
            return t
    return r


def _reduce_adamw_call(name, sel, own, recv, w, m, v):
    r, c = own.shape
    tr = _row_tile(r)

    def body(sel_ref, own_ref, recv_ref, w_ref, m_ref, v_ref, g_ref, d_ref, nm_ref, nv_ref):
        g = jnp.zeros((tr, c), F32)
        for sdev in range(NDEV):
            part = jnp.where(sel_ref[sdev] == 1, recv_ref[sdev].astype(F32), 0.0)
            g = g + jnp.where(sel_ref[sdev] == 2, own_ref[...], part)
        g_ref[...] = g
        d_ref[...], nm_ref[...], nv_ref[...] = _adamw_math(w_ref[...], g, m_ref[...], v_ref[...])

    tile = pl.BlockSpec((tr, c), lambda i, me_ref: (i, 0))
    grid_spec = pltpu.PrefetchScalarGridSpec(
        num_scalar_prefetch=1, grid=(r // tr,),
        in_specs=[tile, pl.BlockSpec((NDEV, tr, c), lambda i, me_ref: (0, i, 0)), tile, tile, tile],
        out_specs=[tile] * 4)
    return pl.pallas_call(
        body, name=name, grid_spec=grid_spec,
        out_shape=[jax.ShapeDtypeStruct((r, c), F32)] * 4,
        compiler_params=_params(("parallel",)),
    )(sel, own, recv, w, m, v)


def _adamw_call(name, w, g, m, v):
    r, c = w.shape
    tr = _row_tile(r)

    def body(w_ref, g_ref, m_ref, v_ref, d_ref, nm_ref, nv_ref):
        d_ref[...], nm_ref[...], nv_ref[...] = _adamw_math(w_ref[...], g_ref[...], m_ref[...], v_ref[...])

    tile = pl.BlockSpec((tr, c), lambda i: (i, 0))
    return pl.pallas_call(
        body, name=name, grid=(r // tr,), in_specs=[tile] * 4, out_specs=[tile] * 3,
        out_shape=[jax.ShapeDtypeStruct((r, c), F32)] * 3,
        compiler_params=_params(("parallel",)),
    )(w, g, m, v)


def _me():
    return lax.axis_index("x"), lax.axis_index("y"), lax.axis_index("c")


def _flip(pos, r):
    x, y, c = pos
    return (1 - x if r & 4 else x, 1 - y if r & 2 else y, 1 - c if r & 1 else c)


def _lin(pos):
    return pos[0] * 4 + pos[1] * 2 + pos[2]


class _SmallAllReduce:
    def __init__(self, g_ref, out_ref, rbuf, send1, recv1, send2, recv2):
        self.g, self.out, self.rbuf = g_ref, out_ref, rbuf
        self.sems = (send1, recv1, send2, recv2)
        self.me = _me()
        self.mi = _lin(self.me)

    @staticmethod
    def _rows(d):
        return pl.ds(pl.multiple_of(d * SMALL_PER, 8), SMALL_PER)

    def _scatter(self, r, outgoing):
        peer = _flip(self.me, r)
        src, dst = (_lin(peer), self.mi) if outgoing else (self.mi, _lin(peer))
        return pltpu.make_async_remote_copy(
            src_ref=self.g.at[self._rows(src)], dst_ref=self.rbuf.at[dst],
            send_sem=self.sems[0].at[r - 1], recv_sem=self.sems[1].at[r - 1], device_id=peer, device_id_type=MESH)

    def _spread(self, r, outgoing):
        peer = _flip(self.me, r)
        rows = self._rows(self.mi if outgoing else _lin(peer))
        return pltpu.make_async_remote_copy(
            src_ref=self.out.at[rows], dst_ref=self.out.at[rows],
            send_sem=self.sems[2].at[r - 1], recv_sem=self.sems[3].at[r - 1], device_id=peer, device_id_type=MESH)

    def scatter(self):
        for r in range(1, NDEV):
            self._scatter(r, True).start()
        self.rbuf[self.mi] = self.g[self._rows(self.mi), :]

    def reduce(self):
        for r in range(1, NDEV):
            self._scatter(r, False).wait_recv()
        for r in range(1, NDEV):
            self._scatter(r, True).wait_send()
        tot = self.rbuf[0]
        for d in range(1, NDEV):
            tot = tot + self.rbuf[d]
        self.out[self._rows(self.mi), :] = tot
        for r in range(1, NDEV):
            self._spread(r, True).start()

    def finish(self):
        for r in range(1, NDEV):
            self._spread(r, False).wait_recv()
        for r in range(1, NDEV):
            self._spread(r, True).wait_send()


def _head_blocks(w):
    z = jnp.zeros((64, 64), w.dtype)
    groups = []
    for q in range(NQ):
        rows = [jnp.concatenate([w[4 * q + a] if a == b else z for b in range(4)], axis=1) for a in range(4)]
        groups.append(jnp.concatenate(rows, axis=0))
    return jnp.stack(groups)


def _head_unblocks(g):
    return jnp.stack([g[q, 64 * a:64 * a + 64, 64 * a:64 * a + 64] for q in range(NQ) for a in range(4)])


def _local_step(x, tgt, p, me, order, order_dw):
    s = x.shape[0]
    vec = lambda a: a.reshape(1, D)
    gmix, gffn, gfin = vec(p["norm_mix_g"]), vec(p["norm_ffn_g"]), vec(p["norm_final_g"])
    cb, ba, bx, lam = vec(p["conv_b"]), vec(p["rg_ba"]), vec(p["rg_bx"]), vec(p["rg_lambda"])
    lng, lnb = vec(p["sgu_ln_g"]), vec(p["sgu_ln_b"])
    wax = jnp.concatenate([_head_blocks(p["rg_wa"]), _head_blocks(p["rg_wx"])], axis=2).astype(BF)
    tril = jnp.tril(jnp.ones((CHUNK, CHUNK), bool))
    ws = jnp.where(tril[None], p["sgu_ws"], 0.0)
    wtr = ws.astype(BF)
    wtrt = jnp.swapaxes(ws, 1, 2).astype(BF)
    bias = jnp.repeat(p["sgu_bs"].T, GW, axis=1)
    shard = {k: p[k].astype(BF) for k in _BIG}

    proj, h1, w_in, (wpa, wpb, wo, cw) = _proj_gather_call(
        x, gmix, shard["w_in"], order, [shard["w_proj_a"], shard["w_proj_b"], shard["w_out"], p["conv_w"]])
    wpa, wpb, wo = (t.reshape(D, D) for t in (wpa, wpb, wo))
    cw = jnp.swapaxes(cw, 0, 1).reshape(4, D)
    (h, pa, pb, x1, *gates, ya, yb, mg), (wgu, wdn) = _mixer_fwd_call(
        proj, x, cw, cb, ba, bx, lam, lng, lnb, wax, wtr, bias, wpa, wpb, wo, [shard["w_gate_up"], shard["w_down"]])
    wdn = wdn.reshape(NDEV // 2, -1, D)
    nb, _, wb = w_in.shape
    fb = wgu.shape[2]
    nh = wdn.shape[0]
    dx1, dx1b, h2, act, dgu, dx2b, facc = _ffn_call(x1, tgt, gffn, gfin, wgu, wdn)
    assert nb == NDEV and 2 * nh == NDEV
    own_gu, part_gu = _dw_plain_call("dw_gate_up", me, dgu, h2, _blk3d(fb), _rows2d(D), 1, fb, D, s)
    own_dn, part_dn = _dw_plain_call("dw_down", me, act, dx2b, _blk3d(fb), _rows2d(D), 2, fb, D, s)
    (dprojb, dya, dpa, dpb, bvec, dws, dbs), (recv_dn,) = _mixer_bwd_pre_call(
        dx1b, proj, pa, pb, lng, lnb, wtr, wtrt, bias, wpa, wpb, wo, [part_dn])
    own_pa, part_pa = _dw_plain_call("dw_proj_a", me, ya, dpa, _rows2d(D), _rows2d(D), NDEV, D, D, s)
    own_pb, part_pb = _dw_plain_call("dw_proj_b", me, yb, dpb, _rows2d(D), _rows2d(D), NDEV, D, D, s)
    own_wo, part_wo = _dw_plain_call("dw_out", me, mg, dx1b, _rows2d(D), _rows2d(D), NDEV, D, D, s)
    (dproj, dx, svec, dwax), (recv_gu, recv_pa, recv_pb, recv_wo) = _mixer_bwd_seq_call(
        dprojb, dya, proj, h, x, dx1, gates, gmix, w_in, cw, lam, wax, [part_gu, part_pa, part_pb, part_wo])
    small = {
        "norm_mix_g": svec[8], "norm_ffn_g": facc[0], "norm_final_g": facc[1],
        "conv_b": svec[0], "rg_ba": svec[1], "rg_bx": svec[2], "rg_lambda": svec[3],
        "sgu_ln_g": bvec[0], "sgu_ln_b": bvec[1],
        "rg_wa": _head_unblocks(dwax[:, :, 0:QW]), "rg_wx": _head_unblocks(dwax[:, :, QW:2 * QW]),
        "sgu_ws": dws, "sgu_bs": dbs[:, ::GW].T,
    }
    packed = _pack_small(small, svec[4:8], facc[2])
    h1t = _transpose_call("h1_transpose", h1)
    own_in, recv_in, gsum = _dw_exchange_call("dw_in", order_dw, h1t, dproj, _rows2d_t(D), _cols2d(wb), D, wb, s, packed)
    dw = {
        "w_gate_up": (own_gu, recv_gu), "w_down": (own_dn, recv_dn), "w_proj_a": (own_pa, recv_pa),
        "w_proj_b": (own_pb, recv_pb), "w_out": (own_wo, recv_wo), "w_in": (own_in, recv_in),
    }
    return gsum, dx, dw


_BIG = ("w_in", "w_gate_up", "w_down", "w_proj_a", "w_proj_b", "w_out")
_VEC_ROWS = ("norm_mix_g", "norm_ffn_g", "norm_final_g", "conv_b", "rg_ba", "rg_bx", "rg_lambda",
             "sgu_ln_g", "sgu_ln_b", "sgu_bs")
_WEIGHTS = ("norm_mix_g", "w_in", "conv_w", "conv_b", "rg_wa", "rg_ba", "rg_wx", "rg_bx", "rg_lambda",
            "sgu_ln_g", "sgu_ln_b", "sgu_ws", "sgu_bs", "w_proj_a", "w_proj_b", "w_out", "norm_ffn_g",
            "w_gate_up", "w_down", "norm_final_g")


def _pack_small(t, conv_w, extra=None):
    extra = jnp.zeros((1, D), F32) if extra is None else extra.reshape(1, D)
    head = jnp.concatenate([t[k].reshape(1, D) for k in _VEC_ROWS] + [conv_w, extra, jnp.zeros((1, D), F32)], axis=0)
    return jnp.concatenate([head, t["rg_wa"].reshape(64, D), t["rg_wx"].reshape(64, D), t["sgu_ws"].reshape(128, D),
                            jnp.zeros((SMALL_ROWS - 272, D), F32)], axis=0)


def _unpack_small(a):
    out = {k: a[j] for j, k in enumerate(_VEC_ROWS)}
    out["conv_w"] = a[10:14]
    out["rg_wa"] = a[16:80].reshape(16, 64, 64)
    out["rg_wx"] = a[80:144].reshape(16, 64, 64)
    out["sgu_ws"] = a[144:272].reshape(NGRP, CHUNK, CHUNK)
    return out


def kernel(x, norm_mix_g, w_in, conv_w, conv_b, rg_wa, rg_ba, rg_wx, rg_bx, rg_lambda, sgu_ln_g, sgu_ln_b, sgu_ws, sgu_bs, w_proj_a, w_proj_b, w_out, norm_ffn_g, w_gate_up, w_down, norm_final_g, loss_target, m_norm_mix_g, m_w_in, m_conv_w, m_conv_b, m_rg_wa, m_rg_ba, m_rg_wx, m_rg_bx, m_rg_lambda, m_sgu_ln_g, m_sgu_ln_b, m_sgu_ws, m_sgu_bs, m_w_proj_a, m_w_proj_b, m_w_out, m_norm_ffn_g, m_w_gate_up, m_w_down, m_norm_final_g, v_norm_mix_g, v_w_in, v_conv_w, v_conv_b, v_rg_wa, v_rg_ba, v_rg_wx, v_rg_bx, v_rg_lambda, v_sgu_ln_g, v_sgu_ln_b, v_sgu_ws, v_sgu_bs, v_w_proj_a, v_w_proj_b, v_w_out, v_norm_ffn_g, v_w_gate_up, v_w_down, v_norm_final_g):
    args = dict(locals())
    w = {k: args[k] for k in _WEIGHTS}
    m = {k: args["m_" + k] for k in _WEIGHTS}
    v = {k: args["v_" + k] for k in _WEIGHTS}
    for d in (w, m, v):
        for k in _WEIGHTS:
            if k != "norm_final_g":
                d[k] = d[k][0]
    me = _lin(_me())
    me1 = me.reshape(1).astype(jnp.int32)

    order = jnp.bitwise_xor(me, jnp.array(_PASS_FLIPS, jnp.int32)).astype(jnp.int32)
    order_dw = jnp.bitwise_xor(me, jnp.array(_DW_FLIPS, jnp.int32)).astype(jnp.int32)

    gsum, dx, dw = _local_step(x[0], loss_target[0], w, me1, order, order_dw)

    peer = jnp.bitwise_xor(jnp.arange(NDEV, dtype=jnp.int32), me)
    sel_direct = jnp.where(peer == 0, 2, 1).astype(jnp.int32)
    sel_two_level = jnp.where(peer == 0, 2, jnp.where((peer == 1) | (peer % 2 == 0), 1, 0)).astype(jnp.int32)
    grads, delta, new_m, new_v = {}, {}, {}, {}
    for k in _BIG:
        own, recv = dw[k]
        flip = own.shape != w[k].shape
        wmv = [jnp.swapaxes(t, 0, 1) if flip else t for t in (w[k], m[k], v[k])]
        res = _reduce_adamw_call("adamw_" + k, sel_two_level if k == "w_in" else sel_direct, own, recv, *wmv)
        grads[k], delta[k], new_m[k], new_v[k] = (jnp.swapaxes(t, 0, 1) if flip else t for t in res)

    loss = (0.5 / D) * jnp.sum(gsum[14])
    zc = jnp.zeros((4, D), F32)
    d_s, m_s, v_s = _adamw_call("adamw_small", _pack_small(w, zc), gsum, _pack_small(m, zc), _pack_small(v, zc))
    gs, ds, ms, vs = _unpack_small(gsum), _unpack_small(d_s), _unpack_small(m_s), _unpack_small(v_s)
    g_cw = lax.dynamic_slice(gs["conv_w"], (0, me * 128), (4, 128))
    ds["conv_w"], ms["conv_w"], vs["conv_w"] = _adamw_call("adamw_conv_w", w["conv_w"], g_cw, m["conv_w"], v["conv_w"])
    gs["conv_w"] = g_cw
    for k in _WEIGHTS:
        if k not in _BIG:
            shp = w[k].shape
            grads[k], delta[k], new_m[k], new_v[k] = (t[k].reshape(shp) for t in (gs, ds, ms, vs))

    def lift(t, k):
        return t[k] if k == "norm_final_g" else t[k][None]

    outs = [loss, dx[None]]
    for t in (grads, delta, new_m, new_v):
        outs += [lift(t, k) for k in _WEIGHTS]
    return tuple(outs)
```

```python
import functools

import jax
import jax.numpy as jnp
from jax import lax
from jax.experimental import pallas as pl
from jax.experimental.pallas import tpu as pltpu

F32 = jnp.float32
BF = jnp.bfloat16

D = 1024
NDEV = 8
EPS = 1e-6
RG_C = 8.0
CHUNK = 128
NGRP = 8
GW = 128
NQ = 4
QW = 256
RC = 16
SMALL_ROWS = 320
SMALL_PER = SMALL_ROWS // NDEV

ADAM_LR = 0.001
ADAM_B1 = 0.9
ADAM_B2 = 0.999
ADAM_EPS = 1e-08
ADAM_WD = 0.01
ADAM_STEP = 10

VMEM_LIMIT = 60 * 1024 * 1024

MESH = pl.DeviceIdType.MESH


def _rows(n, fn, unroll=2, rc=RC):
    def body(i, c):
        fn(pl.multiple_of(i * rc, rc))
        return c
    lax.fori_loop(0, n // rc, body, 0, unroll=unroll)


def _fold(v):
    return jnp.sum(v.reshape(v.shape[0] // RC, RC, v.shape[1]), axis=0)


def _dot(a, b):
    return jnp.dot(a, b, preferred_element_type=F32)


def _dot_nt(a, b):
    return lax.dot_general(a, b, (((1,), (1,)), ((), ())), preferred_element_type=F32)


def _dot_tn(a, b):
    return lax.dot_general(a, b, (((0,), (0,)), ((), ())), preferred_element_type=F32)


_GC = 0.7978845608028654
_GK = 0.044715


def _gelu(x):
    t = jnp.tanh(_GC * (x + _GK * (x * x * x)))
    return x * (0.5 * (1.0 + t))


def _gelu_grad(x):
    x2 = x * x
    t = jnp.tanh(_GC * (x + _GK * (x2 * x)))
    cdf = 0.5 * (1.0 + t)
    dg = cdf + (0.5 * x) * (1.0 - t * t) * (_GC * (1.0 + (3.0 * _GK) * x2))
    return x * cdf, dg


def _sigmoid(x):
    return jax.nn.sigmoid(x)


def _log1p(e):
    u = 1.0 + e
    d = u - 1.0
    return jnp.where(d == 0.0, e, jnp.log(u) * (e / jnp.where(d == 0.0, 1.0, d)))


def _softplus(z):
    return jnp.maximum(z, 0.0) + _log1p(jnp.exp(-jnp.abs(z)))


def _neg_expm1(z):
    u = jnp.exp(z)
    lu = jnp.log(u)
    k = (1.0 - u) * (z / jnp.where(lu == 0.0, 1.0, lu))
    small = jnp.where(lu == 0.0, -z, k)
    return jnp.where(z > -0.5, small, 1.0 - u)


def _shift_back(prev8, cur, j):
    cat = jnp.concatenate([prev8, cur], axis=0)
    return pltpu.roll(cat, j, 0)[8:8 + cur.shape[0]]


def _shift_fwd(cur, next8, j):
    cat = jnp.concatenate([cur, next8], axis=0)
    n = cat.shape[0]
    return pltpu.roll(cat, n - j, 0)[0:cur.shape[0]]


def _const_spec(shape):
    nd = len(shape)
    return pl.BlockSpec(shape, lambda *_: (0,) * nd, pipeline_mode=pl.Buffered(1))


def _params(sem):
    return pltpu.CompilerParams(dimension_semantics=sem, vmem_limit_bytes=VMEM_LIMIT)


TM_PROJ = 1024
TM_MIX = 256
TM_FFN = 256
TM_DX = 512
TS_DW = 4096


_CHIPS = (4, 2, 6)
_PASS_FLIPS = (0, 1, 4, 2, 6, 5, 3, 7)


class _Gather:
    def __init__(self, ins, outs, send_sems, recv_sems, local_sems):
        self.ins, self.outs = ins, outs
        self.send_sems, self.recv_sems, self.local_sems = send_sems, recv_sems, local_sems
        self.me = _me()
        self.sibling = _flip(self.me, 1)

    def _copy(self, a, kind, block, to, src=None):
        dst = self.outs[a].at[_lin(block)]
        return pltpu.make_async_remote_copy(
            src_ref=dst if src is None else src, dst_ref=dst,
            send_sem=self.send_sems.at[a, kind], recv_sem=self.recv_sems.at[a, kind],
            device_id=to, device_id_type=MESH)

    def _local(self, a):
        return pltpu.make_async_copy(self.ins[a], self.outs[a].at[_lin(self.me)], self.local_sems.at[a])

    def start(self):
        for a in range(len(self.ins)):
            self._local(a).start()
            self._copy(a, 0, self.me, self.sibling, src=self.ins[a]).start()
            for j, f in enumerate(_CHIPS):
                self._copy(a, 1 + j, self.me, _flip(self.me, f), src=self.ins[a]).start()

    def forward(self):
        for j, f in enumerate(_CHIPS):
            for a in range(len(self.ins)):
                self._copy(a, 1 + j, _flip(self.me, f), self.me).wait_recv()
                self._copy(a, 4 + j, _flip(self.me, f), self.sibling).start()

    def finish(self):
        for a in range(len(self.ins)):
            self._copy(a, 0, self.sibling, self.me).wait_recv()
            for j, f in enumerate(_CHIPS):
                self._copy(a, 4 + j, _flip(self.me, f | 1), self.me).wait_recv()
            self._copy(a, 0, self.me, self.sibling, src=self.ins[a]).wait_send()
            for j, f in enumerate(_CHIPS):
                self._copy(a, 1 + j, self.me, _flip(self.me, f), src=self.ins[a]).wait_send()
                self._copy(a, 4 + j, _flip(self.me, f), self.sibling).wait_send()
            self._local(a).wait()


class _Exchange:
    def __init__(self, ins, outs, send_sems, recv_sems, local_sems):
        self.ins, self.outs = ins, outs
        self.send_sems, self.recv_sems, self.local_sems = send_sems, recv_sems, local_sems
        self.me = _me()

    def _copy(self, a, r, outgoing):
        peer = _flip(self.me, r)
        src, dst = (peer, self.me) if outgoing else (self.me, peer)
        return pltpu.make_async_remote_copy(
            src_ref=self.ins[a].at[_lin(src)], dst_ref=self.outs[a].at[_lin(dst)],
            send_sem=self.send_sems.at[a, r - 1], recv_sem=self.recv_sems.at[a, r - 1],
            device_id=peer, device_id_type=MESH)

    def _local(self, a):
        mi = _lin(self.me)
        return pltpu.make_async_copy(self.ins[a].at[mi], self.outs[a].at[mi], self.local_sems.at[a])

    def start(self):
        for a in range(len(self.ins)):
            self._local(a).start()
        for r in range(1, NDEV):
            for a in range(len(self.ins)):
                self._copy(a, r, True).start()

    def finish(self):
        for r in range(1, NDEV):
            for a in range(len(self.ins)):
                self._copy(a, r, False).wait_recv()
        for r in range(1, NDEV):
            for a in range(len(self.ins)):
                self._copy(a, r, True).wait_send()
        for a in range(len(self.ins)):
            self._local(a).wait()


def _gather_sems(n):
    return [pltpu.SemaphoreType.DMA((n, 7)), pltpu.SemaphoreType.DMA((n, 7)), pltpu.SemaphoreType.DMA((n,))]


def _proj_gather_call(x, gmix, w_shard, order, extras):
    s = x.shape[0]
    tm = min(TM_PROJ, s)
    nt = s // tm
    wb = w_shard.shape[1]
    n = len(extras)

    def body(order_ref, x_ref, g_ref, wsh_ref, *rest):
        ex_in = rest[:n]
        proj_hbm, h1_hbm, wall_hbm = rest[n:n + 3]
        ex_out = rest[n + 3:2 * n + 3]
        wall, hc, pbuf, ws_send, ws_recv, ex_send, ex_recv, ex_local, out_sems, psems = rest[2 * n + 3:]
        k = pl.program_id(0)
        i = pl.program_id(1)
        me = _me()
        sibling = _flip(me, 1)
        gather = _Gather(ex_in, ex_out, ex_send, ex_recv, ex_local)

        def wcopy(kind, block, to):
            ref = wall.at[_lin(block)]
            return pltpu.make_async_remote_copy(
                src_ref=ref, dst_ref=ref, send_sem=ws_send.at[kind], recv_sem=ws_recv.at[kind],
                device_id=to, device_id_type=MESH)

        head = i == 0

        @pl.when(head & (k == 0))
        def _():
            wall[_lin(me)] = wsh_ref[...]
            wcopy(0, me, sibling).start()
            for j, f in enumerate(_CHIPS):
                wcopy(1 + j, me, _flip(me, f)).start()
            gather.start()

        out_w = pltpu.make_async_copy(wall, wall_hbm, out_sems.at[0])
        out_h = pltpu.make_async_copy(hc, h1_hbm, out_sems.at[1])

        @pl.when(head & (k == 1))
        def _():
            wcopy(0, sibling, me).wait_recv()
            out_h.start()

        for j, f in enumerate(_CHIPS):
            @pl.when(head & (k == 2 + j))
            def _(j=j, f=f):
                wcopy(1 + j, _flip(me, f), me).wait_recv()
                wcopy(4 + j, _flip(me, f), sibling).start()

            @pl.when(head & (k == 5 + j))
            def _(j=j, f=f):
                wcopy(4 + j, _flip(me, f | 1), me).wait_recv()
                if 5 + j == NDEV - 1:
                    out_w.start()

        pl.when(head & (k == 5))(gather.forward)

        base = pl.multiple_of(i * tm, tm)

        @pl.when(k == 0)
        def _():
            g = g_ref[...]

            def norm(r0):
                xx = x_ref[pl.ds(r0, RC), :]
                r = lax.rsqrt(jnp.mean(xx * xx, axis=-1, keepdims=True) + EPS)
                hc[pl.ds(base + r0, RC), :] = ((xx * r) * g).astype(BF)
            _rows(tm, norm, unroll=4)

        blk = order_ref[k]
        wk = wall.at[blk]
        step = k * nt + i
        slot = lax.rem(step, 2)
        subs = [(r0, min(TM_MIX, tm - r0)) for r0 in range(0, tm, TM_MIX)]

        def out_copy(q, r0, rs):
            return pltpu.make_async_copy(
                pbuf.at[slot, pl.ds(r0, rs)],
                proj_hbm.at[pl.ds(base + r0, rs), pl.ds(pl.multiple_of(blk * wb, 128), wb)], psems.at[slot, q])

        @pl.when(step >= 2)
        def _():
            for q, (r0, rs) in enumerate(subs):
                out_copy(q, r0, rs).wait()

        for q, (r0, rs) in enumerate(subs):
            pbuf[slot, r0:r0 + rs, :] = _dot(hc[pl.ds(base + r0, rs), :], wk[...])
            out_copy(q, r0, rs).start()

        @pl.when((k == NDEV - 1) & (i == nt - 1))
        def _():
            wcopy(0, me, sibling).wait_send()
            for j, f in enumerate(_CHIPS):
                wcopy(1 + j, me, _flip(me, f)).wait_send()
                wcopy(4 + j, _flip(me, f), sibling).wait_send()
            gather.finish()
            out_w.wait()
            out_h.wait()
            for q, (r0, rs) in enumerate(subs):
                out_copy(q, r0, rs).wait()
                pltpu.make_async_copy(pbuf.at[1 - slot, pl.ds(r0, rs)], proj_hbm.at[pl.ds(r0, rs), pl.ds(0, wb)],
                                      psems.at[1 - slot, q]).wait()

    any_spec = pl.BlockSpec(memory_space=pl.ANY)
    grid_spec = pltpu.PrefetchScalarGridSpec(
        num_scalar_prefetch=1, grid=(NDEV, nt),
        in_specs=[pl.BlockSpec((tm, D), lambda k, i, o: (jnp.where(k == 0, i, nt - 1), 0)),
                  pl.BlockSpec((1, D), lambda k, i, o: (0, 0)),
                  pl.BlockSpec(w_shard.shape, lambda k, i, o: (0, 0))] + [any_spec] * n,
        out_specs=[any_spec, any_spec, any_spec] + [any_spec] * n,
        scratch_shapes=[pltpu.VMEM((NDEV,) + w_shard.shape, BF), pltpu.VMEM((s, D), BF), pltpu.VMEM((2, tm, wb), F32),
                        pltpu.SemaphoreType.DMA((7,)), pltpu.SemaphoreType.DMA((7,))] + _gather_sems(n)
        + [pltpu.SemaphoreType.DMA((2,)), pltpu.SemaphoreType.DMA((2, -(-tm // TM_MIX)))])
    outs = pl.pallas_call(
        body, name="proj_gather", grid_spec=grid_spec,
        out_shape=[jax.ShapeDtypeStruct((s, NDEV * wb), F32), jax.ShapeDtypeStruct((s, D), BF),
                   jax.ShapeDtypeStruct((NDEV,) + w_shard.shape, BF)]
        + [jax.ShapeDtypeStruct((NDEV,) + e.shape, e.dtype) for e in extras],
        compiler_params=_params(("arbitrary", "arbitrary")),
    )(order, x, gmix, w_shard, *extras)
    return outs[0], outs[1], outs[2], outs[3:]


def _conv_tile(rx, prev8, cw_ref, cb):
    xc = cb + cw_ref[3:4, :] * rx
    for j in (1, 2, 3):
        xc = xc + cw_ref[3 - j:4 - j, :] * _shift_back(prev8, rx, j)
    return xc


def _gate_tile(xcb, wax_ref, ba, bx, sp, q):
    cs = slice(q * QW, (q + 1) * QW)
    z = _dot(xcb[:, cs], wax_ref[q])
    r = _sigmoid(z[:, 0:QW] + ba[:, cs])
    ig = _sigmoid(z[:, QW:2 * QW] + bx[:, cs])
    return r, ig, (-RG_C * r) * sp[:, cs]


def _scan_fwd(a_s, b_s, out_ref, h0, n):
    rowi = lax.broadcasted_iota(jnp.int32, (8, D), 0)

    def block(t, h):
        rows = pl.ds(t * 8, 8)
        a = a_s[rows, :]
        b = b_s[rows, :]
        for d in (1, 2, 4):
            m = rowi >= d
            b = jnp.where(m, a * pltpu.roll(b, d, 0) + b, b)
            a = jnp.where(m, a * pltpu.roll(a, d, 0), a)
        hb = b + a * h
        out_ref[rows, :] = hb
        return hb[7:8, :]
    h = h0
    for t in range(n // 8):
        h = block(t, h)
    return h


def _scan_bwd(a_s, lm_s, c0, n):
    rowi = lax.broadcasted_iota(jnp.int32, (8, D), 0)
    nblk = n // 8

    def block(k, cin):
        rows = pl.ds((nblk - 1 - k) * 8, 8)
        a = a_s[rows, :]
        dh = lm_s[rows, :]
        b = a * dh
        for d in (1, 2, 4):
            m = rowi < 8 - d
            b = jnp.where(m, a * pltpu.roll(b, 8 - d, 0) + b, b)
            a = jnp.where(m, a * pltpu.roll(a, 8 - d, 0), a)
        mu = b + a * cin
        lm_s[rows, :] = dh + jnp.where(rowi < 7, pltpu.roll(mu, 7, 0), cin)
        return mu[0:1, :]
    c = c0
    for k in range(nblk):
        c = block(k, c)
    return c


def _mixer_fwd_call(proj, x, cw, cb, ba, bx, lam, lng, lnb, wax, wtr, bias, wpa, wpb, wo, shards):
    s = x.shape[0]
    tm = min(TM_MIX, s)
    nt = s // tm
    pw = proj.shape[1]
    n = len(shards)

    def body(proj_ref, x_ref, cw_ref, cb_ref, ba_ref, bx_ref, lam_ref, lng_ref, lnb_ref, wax_ref, wtr_ref,
             bias_ref, wpa_ref, wpb_ref, wo_ref, *rest):
        sh_in = rest[:n]
        h_ref, pa_ref, pb_ref, x1_ref, xc_ref, r_ref, ig_ref, a_s, m_ref, ya_ref, yb_ref, mg_ref = rest[n:n + 12]
        sh_out = rest[n + 12:2 * n + 12]
        prev_s, b_s, hc_s, vn_s, mx_s, g_send, g_recv, g_local = rest[2 * n + 12:]
        i = pl.program_id(0)
        gather = _Gather(sh_in, sh_out, g_send, g_recv, g_local)

        @pl.when(i == 0)
        def _():
            gather.start()
            prev_s[...] = jnp.zeros((8, D), F32)
            hc_s[...] = jnp.zeros((8, D), F32)

        pl.when(i == nt // 2)(gather.forward)

        rx = proj_ref[:, 0:D]
        xc = _conv_tile(rx, prev_s[...], cw_ref, cb_ref[...])
        prev_s[...] = rx[tm - 8:tm, :]
        xc_ref[...] = xc.astype(BF)
        xcb = xc.astype(BF)
        sp = _softplus(-lam_ref[...])
        ba = ba_ref[...]
        bx = bx_ref[...]
        for q in range(NQ):
            cs = slice(q * QW, (q + 1) * QW)
            r, ig, la = _gate_tile(xcb, wax_ref, ba, bx, sp, q)
            mq = jnp.sqrt(_neg_expm1(2.0 * la))
            r_ref[:, cs] = r.astype(BF)
            ig_ref[:, cs] = ig.astype(BF)
            m_ref[:, cs] = mq.astype(BF)
            a_s[:, cs] = jnp.exp(la)
            b_s[:, cs] = mq * (ig * xc[:, cs])

        gv = _gelu(proj_ref[:, 3 * D:4 * D])
        dv = gv - jnp.mean(gv, axis=-1, keepdims=True)
        var = jnp.mean(dv * dv, axis=-1, keepdims=True)
        vn_s[...] = ((dv * lax.rsqrt(var + EPS)) * lng_ref[...] + lnb_ref[...]).astype(BF)
        nc = tm // CHUNK
        for c in range(nc):
            rs = slice(c * CHUNK, (c + 1) * CHUNK)
            for g in range(NGRP):
                cs = slice(g * GW, (g + 1) * GW)
                mx_s[rs, cs] = _dot(wtr_ref[g], vn_s[rs, cs])
        mixed = mx_s[...] + jnp.concatenate([bias_ref[...]] * nc, axis=0)
        yb = (_gelu(proj_ref[:, 2 * D:3 * D]) * mixed).astype(BF)
        yb_ref[...] = yb
        pb = _dot(yb, wpb_ref[...])
        pb_ref[...] = pb.astype(BF)
        mx_s[...] = pb

        hc_s[0:1, :] = _scan_fwd(a_s, b_s, h_ref, hc_s[0:1, :], tm)

        ya = (_gelu(proj_ref[:, D:2 * D]) * h_ref[...]).astype(BF)
        ya_ref[...] = ya
        pa = _dot(ya, wpa_ref[...])
        pa_ref[...] = pa.astype(BF)
        mg = (_sigmoid(proj_ref[:, 4 * D:5 * D]) * pa + _sigmoid(proj_ref[:, 5 * D:6 * D]) * mx_s[...]).astype(BF)
        mg_ref[...] = mg
        x1_ref[...] = x_ref[...] + _dot(mg, wo_ref[...])

        pl.when(i == nt - 1)(gather.finish)

    tile = lambda w: pl.BlockSpec((tm, w), lambda i: (i, 0))
    vec = _const_spec((1, D))
    any_spec = pl.BlockSpec(memory_space=pl.ANY)
    outs = pl.pallas_call(
        body, name="mixer_fwd", grid=(nt,),
        in_specs=[tile(pw), tile(D), _const_spec((4, D)), vec, vec, vec, vec, vec, vec,
                  _const_spec(wax.shape), _const_spec(wtr.shape), _const_spec(bias.shape),
                  _const_spec((D, D)), _const_spec((D, D)), _const_spec((D, D))] + [any_spec] * n,
        out_specs=[tile(D)] * 12 + [any_spec] * n,
        out_shape=[jax.ShapeDtypeStruct((s, D), dt) for dt in (F32, BF, BF, F32, BF, BF, BF, F32, BF, BF, BF, BF)]
        + [jax.ShapeDtypeStruct((NDEV,) + e.shape, e.dtype) for e in shards],
        scratch_shapes=[pltpu.VMEM((8, D), F32), pltpu.VMEM((tm, D), F32),
                        pltpu.VMEM((8, D), F32), pltpu.VMEM((tm, D), BF), pltpu.VMEM((tm, D), F32)]
        + _gather_sems(n),
        compiler_params=_params(("arbitrary",)),
    )(proj, x, cw, cb, ba, bx, lam, lng, lnb, wax, wtr, bias, wpa, wpb, wo, *shards)
    return outs[:12], outs[12:]


def _ffn_call(x1, tgt, gffn, gfin, wgu, wdn):
    s = x1.shape[0]
    tm = min(TM_FFN, s)
    nt = s // tm
    nh = wdn.shape[0]
    fb = wgu.shape[2]

    def body(x1_ref, tgt_ref, gffn_ref, gfin_ref, wgu_ref, wdn_ref,
             dx1_ref, dx1b_ref, h2_ref, act_ref, dgu_ref, dx2b_ref, acc_ref,
             g_s, u_s, dx2_s, accs):
        i = pl.program_id(0)

        @pl.when(i == 0)
        def _():
            accs[...] = jnp.zeros(accs.shape, F32)

        gffn = gffn_ref[...]
        gfin = gfin_ref[...]

        x1 = x1_ref[...]
        r2 = lax.rsqrt(jnp.mean(x1 * x1, axis=-1, keepdims=True) + EPS)
        xh2 = x1 * r2
        h2 = (xh2 * gffn).astype(BF)
        h2_ref[...] = h2

        for k in range(nh):
            g = _dot(h2, wgu_ref[k])
            u = _dot(h2, wgu_ref[k + nh])
            g_s[k] = g
            u_s[k] = u
            act_ref[k] = ((g * _sigmoid(g)) * u).astype(BF)

        x2 = x1
        for k in range(nh):
            x2 = x2 + _dot(act_ref[k], wdn_ref[k])

        r3 = lax.rsqrt(jnp.mean(x2 * x2, axis=-1, keepdims=True) + EPS)
        xh = x2 * r3
        err = xh * gfin - tgt_ref[...]
        accs[2] += _fold(err * err)
        dy = err * (1.0 / D)
        accs[1] += _fold(dy * xh)
        dxh = dy * gfin
        dx2 = r3 * (dxh - xh * jnp.mean(dxh * xh, axis=-1, keepdims=True))
        dx2_s[...] = dx2
        dx2b = dx2.astype(BF)
        dx2b_ref[...] = dx2b

        for k in range(nh):
            da = _dot_nt(dx2b, wdn_ref[k])
            g = g_s[k]
            sg = _sigmoid(g)
            dgu_ref[k] = ((da * u_s[k]) * (sg * (1.0 + g * (1.0 - sg)))).astype(BF)
            dgu_ref[k + nh] = (da * (g * sg)).astype(BF)

        dh2 = _dot_nt(dgu_ref[0], wgu_ref[0])
        for k in range(1, 2 * nh):
            dh2 = dh2 + _dot_nt(dgu_ref[k], wgu_ref[k])

        accs[0] += _fold(dh2 * xh2)
        dxh = dh2 * gffn
        dx1 = dx2_s[...] + r2 * (dxh - xh2 * jnp.mean(dxh * xh2, axis=-1, keepdims=True))
        dx1_ref[...] = dx1
        dx1b_ref[...] = dx1.astype(BF)

        @pl.when(i == nt - 1)
        def _():
            acc_ref[...] = jnp.zeros((8, D), F32)
            for j in range(3):
                acc_ref[j:j + 1, :] = jnp.sum(accs[j], axis=0, keepdims=True)

    tile = lambda w: pl.BlockSpec((tm, w), lambda i: (i, 0))
    vec = _const_spec((1, D))
    return pl.pallas_call(
        body, name="ffn", grid=(nt,),
        in_specs=[tile(D), tile(D), vec, vec, _const_spec(wgu.shape), _const_spec(wdn.shape)],
        out_specs=[tile(D), tile(D), tile(D),
                   pl.BlockSpec((nh, tm, fb), lambda i: (0, i, 0)),
                   pl.BlockSpec((2 * nh, tm, fb), lambda i: (0, i, 0)),
                   tile(D), pl.BlockSpec((8, D), lambda i: (0, 0))],
        out_shape=[jax.ShapeDtypeStruct((s, D), F32), jax.ShapeDtypeStruct((s, D), BF),
                   jax.ShapeDtypeStruct((s, D), BF), jax.ShapeDtypeStruct((nh, s, fb), BF),
                   jax.ShapeDtypeStruct((2 * nh, s, fb), BF), jax.ShapeDtypeStruct((s, D), BF),
                   jax.ShapeDtypeStruct((8, D), F32)],
        scratch_shapes=[pltpu.VMEM((nh, tm, fb), F32), pltpu.VMEM((nh, tm, fb), F32), pltpu.VMEM((tm, D), F32),
                        pltpu.VMEM((3, RC, D), F32)],
        compiler_params=_params(("arbitrary",)),
    )(x1, tgt, gffn, gfin, wgu, wdn)


def _mixer_bwd_pre_call(dx1b, proj, pa, pb, lng, lnb, wtr, wtrt, bias, wpa, wpb, wo, parts):
    s = dx1b.shape[0]
    tm = min(TM_MIX, s)
    nt = s // tm
    pw = proj.shape[1]
    n = len(parts)

    def body(dx1b_ref, uv_ref, gg_ref, pa_ref, pb_ref, lng_ref, lnb_ref, wtr_ref, wtrt_ref, bias_ref,
             wpa_ref, wpb_ref, wo_ref, *rest):
        ex_in = rest[:n]
        dproj_ref, dya_ref, dpa_ref, dpb_ref, vec_ref, dws_ref, dbs_ref = rest[n:n + 7]
        ex_out = rest[n + 7:2 * n + 7]
        vn_s, mx_s, dmx_s, dvn_s, accs, dbs_s, e_send, e_recv, e_local = rest[2 * n + 7:]
        i = pl.program_id(0)
        exchange = _Exchange(ex_in, ex_out, e_send, e_recv, e_local)

        @pl.when(i == 0)
        def _():
            exchange.start()
            accs[...] = jnp.zeros(accs.shape, F32)
            dbs_s[...] = jnp.zeros(dbs_s.shape, F32)
            dws_ref[...] = jnp.zeros(dws_ref.shape, F32)

        dm = _dot_nt(dx1b_ref[...], wo_ref[...])
        sa = _sigmoid(gg_ref[:, 0:D])
        sb = _sigmoid(gg_ref[:, D:2 * D])
        dpa = dm * sa
        dpb = dm * sb
        dpab = dpa.astype(BF)
        dpbb = dpb.astype(BF)
        dpa_ref[...] = dpab
        dpb_ref[...] = dpbb
        dproj_ref[:, 2 * D:3 * D] = ((dpa * pa_ref[...].astype(F32)) * (1.0 - sa)).astype(BF)
        dproj_ref[:, 3 * D:4 * D] = ((dpb * pb_ref[...].astype(F32)) * (1.0 - sb)).astype(BF)

        dya_ref[...] = _dot_nt(dpab, wpa_ref[...])
        dyb = _dot_nt(dpbb, wpb_ref[...])

        lng = lng_ref[...]
        gv, dgelu_v = _gelu_grad(uv_ref[:, D:2 * D])
        dv = gv - jnp.mean(gv, axis=-1, keepdims=True)
        rstd = lax.rsqrt(jnp.mean(dv * dv, axis=-1, keepdims=True) + EPS)
        xh = dv * rstd
        vn_s[...] = (xh * lng + lnb_ref[...]).astype(BF)

        nc = tm // CHUNK
        for c in range(nc):
            rs = slice(c * CHUNK, (c + 1) * CHUNK)
            for g in range(NGRP):
                cs = slice(g * GW, (g + 1) * GW)
                mx_s[rs, cs] = _dot(wtr_ref[g], vn_s[rs, cs])

        gu, dgelu_u = _gelu_grad(uv_ref[:, 0:D])
        mixed = mx_s[...] + jnp.concatenate([bias_ref[...]] * nc, axis=0)
        dproj_ref[:, 0:D] = ((dyb * mixed) * dgelu_u).astype(BF)
        dmx = dyb * gu
        dmx_s[...] = dmx.astype(BF)
        dbs_s[...] += jnp.sum(dmx.reshape(nc, CHUNK, D), axis=0)

        for c in range(nc):
            rs = slice(c * CHUNK, (c + 1) * CHUNK)
            for g in range(NGRP):
                cs = slice(g * GW, (g + 1) * GW)
                dvn_s[rs, cs] = _dot(wtrt_ref[g], dmx_s[rs, cs])
                dws_ref[g] += _dot_nt(dmx_s[rs, cs], vn_s[rs, cs])

        dvn = dvn_s[...]
        accs[0] += _fold(dvn * xh)
        accs[1] += _fold(dvn)
        dxh = dvn * lng
        m1 = jnp.mean(dxh, axis=-1, keepdims=True)
        m2 = jnp.mean(dxh * xh, axis=-1, keepdims=True)
        dproj_ref[:, D:2 * D] = ((rstd * (dxh - m1 - xh * m2)) * dgelu_v).astype(BF)

        @pl.when(i == nt - 1)
        def _():
            vec_ref[...] = jnp.zeros((8, D), F32)
            for j in range(2):
                vec_ref[j:j + 1, :] = jnp.sum(accs[j], axis=0, keepdims=True)
            row = lax.broadcasted_iota(jnp.int32, (CHUNK, CHUNK), 0)
            col = lax.broadcasted_iota(jnp.int32, (CHUNK, CHUNK), 1)
            for g in range(NGRP):
                dws_ref[g] = jnp.where(row >= col, dws_ref[g], 0.0)
                gs = jnp.sum(dbs_s[:, g * GW:(g + 1) * GW], axis=1, keepdims=True)
                dbs_ref[:, g * GW:(g + 1) * GW] = jnp.broadcast_to(gs, (CHUNK, GW))
            exchange.finish()

    tile = lambda w: pl.BlockSpec((tm, w), lambda i: (i, 0))
    vec = _const_spec((1, D))
    any_spec = pl.BlockSpec(memory_space=pl.ANY)
    outs = pl.pallas_call(
        body, name="mixer_bwd_pre", grid=(nt,),
        in_specs=[tile(D), pl.BlockSpec((tm, 2 * D), lambda i: (i, 1)), pl.BlockSpec((tm, 2 * D), lambda i: (i, 2)),
                  tile(D), tile(D), vec, vec, _const_spec(wtr.shape), _const_spec(wtrt.shape),
                  _const_spec(bias.shape), _const_spec((D, D)), _const_spec((D, D)), _const_spec((D, D))]
        + [any_spec] * n,
        out_specs=[tile(4 * D), tile(D), tile(D), tile(D), pl.BlockSpec((8, D), lambda i: (0, 0)),
                   pl.BlockSpec((NGRP, CHUNK, CHUNK), lambda i: (0, 0, 0)),
                   pl.BlockSpec((CHUNK, D), lambda i: (0, 0))] + [any_spec] * n,
        out_shape=[jax.ShapeDtypeStruct((s, 4 * D), BF), jax.ShapeDtypeStruct((s, D), F32),
                   jax.ShapeDtypeStruct((s, D), BF), jax.ShapeDtypeStruct((s, D), BF),
                   jax.ShapeDtypeStruct((8, D), F32), jax.ShapeDtypeStruct((NGRP, CHUNK, CHUNK), F32),
                   jax.ShapeDtypeStruct((CHUNK, D), F32)]
        + [jax.ShapeDtypeStruct(p.shape, p.dtype) for p in parts],
        scratch_shapes=[pltpu.VMEM((tm, D), BF), pltpu.VMEM((tm, D), F32), pltpu.VMEM((tm, D), BF),
                        pltpu.VMEM((tm, D), F32), pltpu.VMEM((2, RC, D), F32), pltpu.VMEM((CHUNK, D), F32)]
        + _gather_sems(n),
        compiler_params=_params(("arbitrary",)),
    )(dx1b, proj, proj, pa, pb, lng, lnb, wtr, wtrt, bias, wpa, wpb, wo, *parts)
    return outs[:7], outs[7:]


def _mixer_bwd_seq_call(dprojb, dya, proj, h, x, dx1, gates, gmix, w_in_all, cw, lam, wax, parts):
    s = dya.shape[0]
    tm = min(TM_MIX, s)
    nt = s // tm
    tb = tm // 8
    n = len(parts)
    nb, _, wb = w_in_all.shape

    def body(dpb_ref, dya_ref, xg_ref, xh8_ref, h_ref, hh8_ref, x_ref, dx1_ref, xc_s, r_s, ig_s, a_s, m_s,
             gmix_ref, win_ref, cw_ref, lam_ref, wax_ref, *rest):
        ex_in = rest[:n]
        dpa_ref, dx_ref, vec_ref, dwax_ref = rest[n:n + 4]
        ex_out = rest[n + 4:2 * n + 4]
        lm_s, dh_s, dxc_s, c_s, accs, e_send, e_recv, e_local = rest[2 * n + 4:]
        i = pl.program_id(0)
        exchange = _Exchange(ex_in, ex_out, e_send, e_recv, e_local)

        @pl.when(i == 0)
        def _():
            exchange.start()
            accs[...] = jnp.zeros(accs.shape, F32)
            dwax_ref[...] = jnp.zeros(dwax_ref.shape, F32)
            c_s[...] = jnp.zeros((8, D), F32)
            dxc_s[tm:tm + 8, :] = jnp.zeros((8, D), F32)

        first_tile = i == nt - 1
        prev8 = jnp.where(first_tile, 0.0, xh8_ref[...])
        hprev8 = jnp.where(first_tile, 0.0, hh8_ref[...])
        lamv = lam_ref[...]
        sp = _softplus(-lamv)
        hv = h_ref[...]
        g, dg = _gelu_grad(xg_ref[:, D:2 * D])
        dya = dya_ref[...]
        lm_s[...] = dya * g
        drg = ((dya * hv) * dg).astype(BF)
        dpa_ref[:, D:2 * D] = drg
        dpa_ref[:, 2 * D:6 * D] = dpb_ref[...]

        def dpb_block(k):
            return _dot_nt(dpb_ref[:, k * wb - 2 * D:(k + 1) * wb - 2 * D], win_ref[k])
        dh = (_dot_nt(drg[:, 0:2 * wb - D], win_ref[1, :, D - wb:wb])
              + _dot_nt(drg[:, 2 * wb - D:D], win_ref[2, :, 0:2 * D - 2 * wb])
              + _dot_nt(dpb_ref[:, 0:3 * wb - 2 * D], win_ref[2, :, 2 * D - 2 * wb:wb]))
        for k in range(3, 6):
            dh = dh + dpb_block(k)
        dh_s[...] = dh

        c_s[0:1, :] = _scan_bwd(a_s, lm_s, c_s[0:1, :], tm)

        hprev = _shift_back(hprev8, h_ref[...], 1)
        for q in range(NQ):
            cs = slice(q * QW, (q + 1) * QW)
            r = r_s[:, cs].astype(F32)
            ig = ig_s[:, cs].astype(F32)
            a = a_s[:, cs]
            m = m_s[:, cs].astype(F32)
            lm = lm_s[:, cs]
            xq = xc_s[:, cs].astype(F32)
            dixc = lm * m
            dla = (lm * hprev[:, cs]) * a - ((lm * (ig * xq)) * (a * a)) / m
            accs[3, :, cs] += _fold(dla * r)
            dza = (dla * (-RG_C * sp[:, cs])) * (r * (1.0 - r))
            dzx = (dixc * xq) * (ig * (1.0 - ig))
            accs[1, :, cs] += _fold(dza)
            accs[2, :, cs] += _fold(dzx)
            dz = jnp.concatenate([dza, dzx], axis=1).astype(BF)
            dxc_s[0:tm, cs] = dixc * ig + _dot_nt(dz, wax_ref[q])
            dwax_ref[q] += _dot_tn(xc_s[:, cs], dz)

        cur = dxc_s[0:tm, :]
        nx = dxc_s[tm:tm + 8, :]
        drx = cw_ref[3:4, :] * cur
        for j in (1, 2, 3):
            drx = drx + cw_ref[3 - j:4 - j, :] * _shift_fwd(cur, nx, j)
        accs[0] += _fold(cur)
        rx = xg_ref[:, 0:D]
        accs[7] += _fold(cur * rx)
        for j in (1, 2, 3):
            accs[7 - j] += _fold(cur * _shift_back(prev8, rx, j))
        dxc_s[tm:tm + 8, :] = cur[0:8, :]

        drxb = drx.astype(BF)
        dpa_ref[:, 0:D] = drxb
        dh = dh_s[...] + _dot_nt(drxb[:, 0:wb], win_ref[0]) + _dot_nt(drxb[:, wb:D], win_ref[1, :, 0:D - wb])
        for k in range(6, nb):
            dh = dh + dpb_block(k)
        xx = x_ref[...]
        rn = lax.rsqrt(jnp.mean(xx * xx, axis=-1, keepdims=True) + EPS)
        xhn = xx * rn
        accs[8] += _fold(dh * xhn)
        dxh = dh * gmix_ref[...]
        dx_ref[...] = dx1_ref[...] + rn * (dxh - xhn * jnp.mean(dxh * xhn, axis=-1, keepdims=True))

        @pl.when(i == nt - 1)
        def _():
            vec_ref[...] = jnp.zeros((16, D), F32)
            for j in range(9):
                vec_ref[j:j + 1, :] = jnp.sum(accs[j], axis=0, keepdims=True)
            vec_ref[3:4, :] = vec_ref[3:4, :] * (RG_C * _sigmoid(-lamv))
            exchange.finish()

    rev = lambda w: pl.BlockSpec((tm, w), lambda i: (nt - 1 - i, 0))
    halo = pl.BlockSpec((8, D), lambda i: (jnp.maximum((nt - 1 - i) * tb - 1, 0), 0))
    vec = _const_spec((1, D))
    any_spec = pl.BlockSpec(memory_space=pl.ANY)
    outs = pl.pallas_call(
        body, name="mixer_bwd_seq", grid=(nt,),
        in_specs=[rev(4 * D), rev(D), rev(2 * D), halo, rev(D), halo] + [rev(D)] * 7
        + [vec, _const_spec(w_in_all.shape), _const_spec((4, D)), vec, _const_spec(wax.shape)] + [any_spec] * n,
        out_specs=[rev(6 * D), rev(D), pl.BlockSpec((16, D), lambda i: (0, 0)),
                   pl.BlockSpec((NQ, QW, 2 * QW), lambda i: (0, 0, 0))] + [any_spec] * n,
        out_shape=[jax.ShapeDtypeStruct((s, 6 * D), BF), jax.ShapeDtypeStruct((s, D), F32),
                   jax.ShapeDtypeStruct((16, D), F32), jax.ShapeDtypeStruct((NQ, QW, 2 * QW), F32)]
        + [jax.ShapeDtypeStruct(p.shape, p.dtype) for p in parts],
        scratch_shapes=[pltpu.VMEM((tm, D), F32), pltpu.VMEM((tm, D), F32),
                        pltpu.VMEM((tm + 8, D), F32), pltpu.VMEM((8, D), F32), pltpu.VMEM((9, RC, D), F32)]
        + _gather_sems(n),
        compiler_params=_params(("arbitrary",)),
    )(dprojb, dya, proj, proj, h, h, x, dx1, *gates, gmix, w_in_all, cw, lam, wax, *parts)
    return outs[:4], outs[4:]


def _device_of(d):
    return (d // 4, lax.rem(d // 2, 2), lax.rem(d, 2))


_DW_PLAN = (("C", 3), ("A", 3), ("C", 1), ("A", 1), ("C", 2), ("A", 2), ("S", 0), ("O", 0))
_DW_FLIPS = tuple({"C": 2 * j + 1, "A": 2 * j, "S": 1, "O": 0}[kind] for kind, j in _DW_PLAN)


def _dw_exchange_call(name, order, a, b, a_spec, b_spec, k1, n1, s, small):
    ts = min(TS_DW, s)
    ns = s // ts
    nstep = len(_DW_PLAN)

    def slab(i, order_ref):
        return order_ref[i]

    def body(order_ref, a_ref, b_ref, g_ref, own_ref, recv_ref, gsum_ref, acc, sbuf, stage, send_sems, recv_sems,
             st_send, st_recv, local_sem, rbuf, gacc, send1, recv1, send2, recv2):
        i = pl.program_id(0)
        j = pl.program_id(1)
        me = _me()
        mi = _lin(me)
        sibling = _flip(me, 1)
        allreduce = _SmallAllReduce(g_ref, gacc, rbuf, send1, recv1, send2, recv2)
        pl.when((i == 0) & (j == 0))(allreduce.scatter)
        pl.when((i == nstep // 2) & (j == 0))(allreduce.reduce)
        @pl.when((i == 0) & (j == 0))
        def _():
            acc[...] = jnp.zeros(acc.shape, F32)

        acc[...] = _dot_tn(a_ref[...], b_ref[...]) + jnp.where(j == 0, 0.0, acc[...])

        def send(step):
            kind, jj = _DW_PLAN[step]
            src = sbuf.at[step % 2]
            if kind == "C":
                return pltpu.make_async_remote_copy(
                    src_ref=src, dst_ref=stage.at[jj - 1], send_sem=st_send.at[jj - 1], recv_sem=st_recv.at[jj - 1],
                    device_id=sibling, device_id_type=MESH)
            to = sibling if kind == "S" else _flip(me, 2 * jj)
            return pltpu.make_async_remote_copy(
                src_ref=src, dst_ref=recv_ref.at[mi], send_sem=send_sems.at[jj], recv_sem=recv_sems.at[jj],
                device_id=to, device_id_type=MESH)

        def arrival(jj):
            frm = sibling if jj == 0 else _flip(me, 2 * jj)
            return pltpu.make_async_remote_copy(
                src_ref=sbuf.at[0], dst_ref=recv_ref.at[_lin(frm)], send_sem=send_sems.at[jj],
                recv_sem=recv_sems.at[jj], device_id=frm, device_id_type=MESH)

        def staged(jj):
            return pltpu.make_async_remote_copy(
                src_ref=sbuf.at[0], dst_ref=stage.at[jj - 1], send_sem=st_send.at[jj - 1], recv_sem=st_recv.at[jj - 1],
                device_id=sibling, device_id_type=MESH)

        for step, (kind, jj) in enumerate(_DW_PLAN):
            @pl.when((i == step) & (j == ns - 1))
            def _(step=step, kind=kind, jj=jj):
                if step >= 2:
                    send(step - 2).wait_send()
                if kind == "A":
                    staged(jj).wait_recv()
                    sbuf[step % 2] = (acc[...] + stage[jj - 1].astype(F32)).astype(BF)
                else:
                    sbuf[step % 2] = acc[...].astype(BF)
                if kind != "O":
                    send(step).start()
                else:
                    own_ref[...] = acc[...]
                    mine = pltpu.make_async_copy(sbuf.at[step % 2], recv_ref.at[mi], local_sem)
                    mine.start()
                    send(step - 1).wait_send()
                    for q in range(4):
                        arrival(q).wait_recv()
                    mine.wait()
                    allreduce.finish()
                    gsum_ref[...] = gacc[...]

    vmem = pl.BlockSpec(memory_space=pltpu.VMEM)
    grid_spec = pltpu.PrefetchScalarGridSpec(
        num_scalar_prefetch=1, grid=(nstep, ns),
        in_specs=[a_spec(ts, slab), b_spec(ts, slab), vmem],
        out_specs=[pl.BlockSpec((k1, n1), lambda i, j, o: (0, 0)), pl.BlockSpec(memory_space=pl.ANY), vmem],
        scratch_shapes=[pltpu.VMEM((k1, n1), F32), pltpu.VMEM((2, k1, n1), BF), pltpu.VMEM((3, k1, n1), BF),
                        pltpu.SemaphoreType.DMA((4,)), pltpu.SemaphoreType.DMA((4,)),
                        pltpu.SemaphoreType.DMA((3,)), pltpu.SemaphoreType.DMA((3,)),
                        pltpu.SemaphoreType.DMA(()), pltpu.VMEM((NDEV, SMALL_PER, D), F32),
                        pltpu.VMEM((SMALL_ROWS, D), F32)]
        + [pltpu.SemaphoreType.DMA((NDEV - 1,))] * 4)
    return pl.pallas_call(
        body, name=name, grid_spec=grid_spec,
        out_shape=[jax.ShapeDtypeStruct((k1, n1), F32), jax.ShapeDtypeStruct((NDEV, k1, n1), BF),
                   jax.ShapeDtypeStruct((SMALL_ROWS, D), F32)],
        compiler_params=_params(("arbitrary", "arbitrary")),
    )(order, a, b, small)


def _dw_plain_call(name, me, a, b, a_spec, b_spec, split, k1, n1, s):
    nb = NDEV // split
    r = k1 // split
    ts = min(TS_DW, s)
    ns = s // ts

    def slab(i, me_ref):
        return i

    def body(me_ref, a_ref, b_ref, own_ref, part_ref, acc):
        i = pl.program_id(0)
        j = pl.program_id(1)
        @pl.when((i == 0) & (j == 0))
        def _():
            acc[...] = jnp.zeros(acc.shape, F32)

        acc[...] = _dot_tn(a_ref[...], b_ref[...]) + jnp.where(j == 0, 0.0, acc[...])

        @pl.when(j == ns - 1)
        def _():
            part_ref[...] = acc[...].astype(BF)

            @pl.when(i == me_ref[0] // split)
            def _():
                off = pl.multiple_of(lax.rem(me_ref[0], split) * r, RC)
                own_ref[...] = acc[pl.ds(off, r), :]

    grid_spec = pltpu.PrefetchScalarGridSpec(
        num_scalar_prefetch=1, grid=(nb, ns),
        in_specs=[a_spec(ts, slab), b_spec(ts, slab)],
        out_specs=[pl.BlockSpec((r, n1), lambda i, j, me_ref: (0, 0)),
                   pl.BlockSpec((None, k1, n1), lambda i, j, me_ref: (i, 0, 0))],
        scratch_shapes=[pltpu.VMEM((k1, n1), F32)])
    own, part = pl.pallas_call(
        body, name=name, grid_spec=grid_spec,
        out_shape=[jax.ShapeDtypeStruct((r, n1), F32), jax.ShapeDtypeStruct((nb, k1, n1), BF)],
        compiler_params=_params(("arbitrary", "arbitrary")),
    )(me, a, b)
    return own, part.reshape(NDEV, r, n1)


def _rows2d(w):
    return lambda ts, slab: pl.BlockSpec((ts, w), lambda i, j, me_ref: (j, 0))


def _cols2d(w):
    return lambda ts, slab: pl.BlockSpec((ts, w), lambda i, j, me_ref: (j, slab(i, me_ref)))


def _blk3d(w):
    return lambda ts, slab: pl.BlockSpec((None, ts, w), lambda i, j, me_ref: (slab(i, me_ref), j, 0))


_BC1 = 1.0 - ADAM_B1 ** ADAM_STEP
_BC2 = 1.0 - ADAM_B2 ** ADAM_STEP


def _adamw_math(w, g, m, v):
    m = ADAM_B1 * m + (1.0 - ADAM_B1) * g
    v = ADAM_B2 * v + (1.0 - ADAM_B2) * (g * g)
    m_hat = m / _BC1
    v_hat = v / _BC2
    delta = -ADAM_LR * (m_hat / (jnp.sqrt(v_hat) + ADAM_EPS) + ADAM_WD * w)
    return delta, m, v


def _row_tile(r):
    for t in (256, 176, 128, 64, 32, 16, 8):
        if r % t == 0:
            return t
    return r


def _reduce_adamw_call(name, sel, own, recv, w, m, v):
    r, c = own.shape
    tr = _row_tile(r)

    def body(sel_ref, own_ref, recv_ref, w_ref, m_ref, v_ref, g_ref, d_ref, nm_ref, nv_ref):
        g = jnp.zeros((tr, c), F32)
        for sdev in range(NDEV):
            part = jnp.where(sel_ref[sdev] == 1, recv_ref[sdev].astype(F32), 0.0)
            g = g + jnp.where(sel_ref[sdev] == 2, own_ref[...], part)
        g_ref[...] = g
        d_ref[...], nm_ref[...], nv_ref[...] = _adamw_math(w_ref[...], g, m_ref[...], v_ref[...])

    tile = pl.BlockSpec((tr, c), lambda i, me_ref: (i, 0))
    grid_spec = pltpu.PrefetchScalarGridSpec(
        num_scalar_prefetch=1, grid=(r // tr,),
        in_specs=[tile, pl.BlockSpec((NDEV, tr, c), lambda i, me_ref: (0, i, 0)), tile, tile, tile],
        out_specs=[tile] * 4)
    return pl.pallas_call(
        body, name=name, grid_spec=grid_spec,
        out_shape=[jax.ShapeDtypeStruct((r, c), F32)] * 4,
        compiler_params=_params(("parallel",)),
    )(sel, own, recv, w, m, v)


def _adamw_call(name, w, g, m, v):
    r, c = w.shape
    tr = _row_tile(r)

    def body(w_ref, g_ref, m_ref, v_ref, d_ref, nm_ref, nv_ref):
        d_ref[...], nm_ref[...], nv_ref[...] = _adamw_math(w_ref[...], g_ref[...], m_ref[...], v_ref[...])

    tile = pl.BlockSpec((tr, c), lambda i: (i, 0))
    return pl.pallas_call(
        body, name=name, grid=(r // tr,), in_specs=[tile] * 4, out_specs=[tile] * 3,
        out_shape=[jax.ShapeDtypeStruct((r, c), F32)] * 3,
        compiler_params=_params(("parallel",)),
    )(w, g, m, v)


def _me():
    return lax.axis_index("x"), lax.axis_index("y"), lax.axis_index("c")


def _flip(pos, r):
    x, y, c = pos
    return (1 - x if r & 4 else x, 1 - y if r & 2 else y, 1 - c if r & 1 else c)


def _lin(pos):
    return pos[0] * 4 + pos[1] * 2 + pos[2]


class _SmallAllReduce:
    def __init__(self, g_ref, out_ref, rbuf, send1, recv1, send2, recv2):
        self.g, self.out, self.rbuf = g_ref, out_ref, rbuf
        self.sems = (send1, recv1, send2, recv2)
        self.me = _me()
        self.mi = _lin(self.me)

    @staticmethod
    def _rows(d):
        return pl.ds(pl.multiple_of(d * SMALL_PER, 8), SMALL_PER)

    def _scatter(self, r, outgoing):
        peer = _flip(self.me, r)
        src, dst = (_lin(peer), self.mi) if outgoing else (self.mi, _lin(peer))
        return pltpu.make_async_remote_copy(
            src_ref=self.g.at[self._rows(src)], dst_ref=self.rbuf.at[dst],
            send_sem=self.sems[0].at[r - 1], recv_sem=self.sems[1].at[r - 1], device_id=peer, device_id_type=MESH)

    def _spread(self, r, outgoing):
        peer = _flip(self.me, r)
        rows = self._rows(self.mi if outgoing else _lin(peer))
        return pltpu.make_async_remote_copy(
            src_ref=self.out.at[rows], dst_ref=self.out.at[rows],
            send_sem=self.sems[2].at[r - 1], recv_sem=self.sems[3].at[r - 1], device_id=peer, device_id_type=MESH)

    def scatter(self):
        for r in range(1, NDEV):
            self._scatter(r, True).start()
        self.rbuf[self.mi] = self.g[self._rows(self.mi), :]

    def reduce(self):
        for r in range(1, NDEV):
            self._scatter(r, False).wait_recv()
        for r in range(1, NDEV):
            self._scatter(r, True).wait_send()
        tot = self.rbuf[0]
        for d in range(1, NDEV):
            tot = tot + self.rbuf[d]
        self.out[self._rows(self.mi), :] = tot
        for r in range(1, NDEV):
            self._spread(r, True).start()

    def finish(self):
        for r in range(1, NDEV):
            self._spread(r, False).wait_recv()
        for r in range(1, NDEV):
            self._spread(r, True).wait_send()


def _head_blocks(w):
    z = jnp.zeros((64, 64), w.dtype)
    groups = []
    for q in range(NQ):
        rows = [jnp.concatenate([w[4 * q + a] if a == b else z for b in range(4)], axis=1) for a in range(4)]
        groups.append(jnp.concatenate(rows, axis=0))
    return jnp.stack(groups)


def _head_unblocks(g):
    return jnp.stack([g[q, 64 * a:64 * a + 64, 64 * a:64 * a + 64] for q in range(NQ) for a in range(4)])


def _local_step(x, tgt, p, me, order, order_dw):
    s = x.shape[0]
    vec = lambda a: a.reshape(1, D)
    gmix, gffn, gfin = vec(p["norm_mix_g"]), vec(p["norm_ffn_g"]), vec(p["norm_final_g"])
    cb, ba, bx, lam = vec(p["conv_b"]), vec(p["rg_ba"]), vec(p["rg_bx"]), vec(p["rg_lambda"])
    lng, lnb = vec(p["sgu_ln_g"]), vec(p["sgu_ln_b"])
    wax = jnp.concatenate([_head_blocks(p["rg_wa"]), _head_blocks(p["rg_wx"])], axis=2).astype(BF)
    tril = jnp.tril(jnp.ones((CHUNK, CHUNK), bool))
    ws = jnp.where(tril[None], p["sgu_ws"], 0.0)
    wtr = ws.astype(BF)
    wtrt = jnp.swapaxes(ws, 1, 2).astype(BF)
    bias = jnp.repeat(p["sgu_bs"].T, GW, axis=1)
    shard = {k: p[k].astype(BF) for k in _BIG}

    proj, h1, w_in, (wpa, wpb, wo, cw) = _proj_gather_call(
        x, gmix, shard["w_in"], order, [shard["w_proj_a"], shard["w_proj_b"], shard["w_out"], p["conv_w"]])
    wpa, wpb, wo = (t.reshape(D, D) for t in (wpa, wpb, wo))
    cw = jnp.swapaxes(cw, 0, 1).reshape(4, D)
    (h, pa, pb, x1, *gates, ya, yb, mg), (wgu, wdn) = _mixer_fwd_call(
        proj, x, cw, cb, ba, bx, lam, lng, lnb, wax, wtr, bias, wpa, wpb, wo, [shard["w_gate_up"], shard["w_down"]])
    wdn = wdn.reshape(NDEV // 2, -1, D)
    nb, _, wb = w_in.shape
    fb = wgu.shape[2]
    nh = wdn.shape[0]
    dx1, dx1b, h2, act, dgu, dx2b, facc = _ffn_call(x1, tgt, gffn, gfin, wgu, wdn)
    assert nb == NDEV and 2 * nh == NDEV
    own_gu, part_gu = _dw_plain_call("dw_gate_up", me, dgu, h2, _blk3d(fb), _rows2d(D), 1, fb, D, s)
    own_dn, part_dn = _dw_plain_call("dw_down", me, act, dx2b, _blk3d(fb), _rows2d(D), 2, fb, D, s)
    (dprojb, dya, dpa, dpb, bvec, dws, dbs), (recv_dn,) = _mixer_bwd_pre_call(
        dx1b, proj, pa, pb, lng, lnb, wtr, wtrt, bias, wpa, wpb, wo, [part_dn])
    own_pa, part_pa = _dw_plain_call("dw_proj_a", me, ya, dpa, _rows2d(D), _rows2d(D), NDEV, D, D, s)
    own_pb, part_pb = _dw_plain_call("dw_proj_b", me, yb, dpb, _rows2d(D), _rows2d(D), NDEV, D, D, s)
    own_wo, part_wo = _dw_plain_call("dw_out", me, mg, dx1b, _rows2d(D), _rows2d(D), NDEV, D, D, s)
    (dproj, dx, svec, dwax), (recv_gu, recv_pa, recv_pb, recv_wo) = _mixer_bwd_seq_call(
        dprojb, dya, proj, h, x, dx1, gates, gmix, w_in, cw, lam, wax, [part_gu, part_pa, part_pb, part_wo])
    small = {
        "norm_mix_g": svec[8], "norm_ffn_g": facc[0], "norm_final_g": facc[1],
        "conv_b": svec[0], "rg_ba": svec[1], "rg_bx": svec[2], "rg_lambda": svec[3],
        "sgu_ln_g": bvec[0], "sgu_ln_b": bvec[1],
        "rg_wa": _head_unblocks(dwax[:, :, 0:QW]), "rg_wx": _head_unblocks(dwax[:, :, QW:2 * QW]),
        "sgu_ws": dws, "sgu_bs": dbs[:, ::GW].T,
    }
    packed = _pack_small(small, svec[4:8], facc[2])
    own_in, recv_in, gsum = _dw_exchange_call("dw_in", order_dw, h1, dproj, _rows2d(D), _cols2d(wb), D, wb, s, packed)
    dw = {
        "w_gate_up": (own_gu, recv_gu), "w_down": (own_dn, recv_dn), "w_proj_a": (own_pa, recv_pa),
        "w_proj_b": (own_pb, recv_pb), "w_out": (own_wo, recv_wo), "w_in": (own_in, recv_in),
    }
    return gsum, dx, dw


_BIG = ("w_in", "w_gate_up", "w_down", "w_proj_a", "w_proj_b", "w_out")
_VEC_ROWS = ("norm_mix_g", "norm_ffn_g", "norm_final_g", "conv_b", "rg_ba", "rg_bx", "rg_lambda",
             "sgu_ln_g", "sgu_ln_b", "sgu_bs")
_WEIGHTS = ("norm_mix_g", "w_in", "conv_w", "conv_b", "rg_wa", "rg_ba", "rg_wx", "rg_bx", "rg_lambda",
            "sgu_ln_g", "sgu_ln_b", "sgu_ws", "sgu_bs", "w_proj_a", "w_proj_b", "w_out", "norm_ffn_g",
            "w_gate_up", "w_down", "norm_final_g")


def _pack_small(t, conv_w, extra=None):
    extra = jnp.zeros((1, D), F32) if extra is None else extra.reshape(1, D)
    head = jnp.concatenate([t[k].reshape(1, D) for k in _VEC_ROWS] + [conv_w, extra, jnp.zeros((1, D), F32)], axis=0)
    return jnp.concatenate([head, t["rg_wa"].reshape(64, D), t["rg_wx"].reshape(64, D), t["sgu_ws"].reshape(128, D),
                            jnp.zeros((SMALL_ROWS - 272, D), F32)], axis=0)


def _unpack_small(a):
    out = {k: a[j] for j, k in enumerate(_VEC_ROWS)}
    out["conv_w"] = a[10:14]
    out["rg_wa"] = a[16:80].reshape(16, 64, 64)
    out["rg_wx"] = a[80:144].reshape(16, 64, 64)
    out["sgu_ws"] = a[144:272].reshape(NGRP, CHUNK, CHUNK)
    return out


def kernel(x, norm_mix_g, w_in, conv_w, conv_b, rg_wa, rg_ba, rg_wx, rg_bx, rg_lambda, sgu_ln_g, sgu_ln_b, sgu_ws, sgu_bs, w_proj_a, w_proj_b, w_out, norm_ffn_g, w_gate_up, w_down, norm_final_g, loss_target, m_norm_mix_g, m_w_in, m_conv_w, m_conv_b, m_rg_wa, m_rg_ba, m_rg_wx, m_rg_bx, m_rg_lambda, m_sgu_ln_g, m_sgu_ln_b, m_sgu_ws, m_sgu_bs, m_w_proj_a, m_w_proj_b, m_w_out, m_norm_ffn_g, m_w_gate_up, m_w_down, m_norm_final_g, v_norm_mix_g, v_w_in, v_conv_w, v_conv_b, v_rg_wa, v_rg_ba, v_rg_wx, v_rg_bx, v_rg_lambda, v_sgu_ln_g, v_sgu_ln_b, v_sgu_ws, v_sgu_bs, v_w_proj_a, v_w_proj_b, v_w_out, v_norm_ffn_g, v_w_gate_up, v_w_down, v_norm_final_g):
    args = dict(locals())
    w = {k: args[k] for k in _WEIGHTS}
    m = {k: args["m_" + k] for k in _WEIGHTS}
    v = {k: args["v_" + k] for k in _WEIGHTS}
    for d in (w, m, v):
        for k in _WEIGHTS:
            if k != "norm_final_g":
                d[k] = d[k][0]
    me = _lin(_me())
    me1 = me.reshape(1).astype(jnp.int32)

    order = jnp.bitwise_xor(me, jnp.array(_PASS_FLIPS, jnp.int32)).astype(jnp.int32)
    order_dw = jnp.bitwise_xor(me, jnp.array(_DW_FLIPS, jnp.int32)).astype(jnp.int32)

    gsum, dx, dw = _local_step(x[0], loss_target[0], w, me1, order, order_dw)

    peer = jnp.bitwise_xor(jnp.arange(NDEV, dtype=jnp.int32), me)
    sel_direct = jnp.where(peer == 0, 2, 1).astype(jnp.int32)
    sel_two_level = jnp.where(peer == 0, 2, jnp.where((peer == 1) | (peer % 2 == 0), 1, 0)).astype(jnp.int32)
    grads, delta, new_m, new_v = {}, {}, {}, {}
    for k in _BIG:
        own, recv = dw[k]
        flip = own.shape != w[k].shape
        wmv = [jnp.swapaxes(t, 0, 1) if flip else t for t in (w[k], m[k], v[k])]
        res = _reduce_adamw_call("adamw_" + k, sel_two_level if k == "w_in" else sel_direct, own, recv, *wmv)
        grads[k], delta[k], new_m[k], new_v[k] = (jnp.swapaxes(t, 0, 1) if flip else t for t in res)

    loss = (0.5 / D) * jnp.sum(gsum[14])
    zc = jnp.zeros((4, D), F32)
    d_s, m_s, v_s = _adamw_call("adamw_small", _pack_small(w, zc), gsum, _pack_small(m, zc), _pack_small(v, zc))
    gs, ds, ms, vs = _unpack_small(gsum), _unpack_small(d_s), _unpack_small(m_s), _unpack_small(v_s)
    g_cw = lax.dynamic_slice(gs["conv_w"], (0, me * 128), (4, 128))
    ds["conv_w"], ms["conv_w"], vs["conv_w"] = _adamw_call("adamw_conv_w", w["conv_w"], g_cw, m["conv_w"], v["conv_w"])
    gs["conv_w"] = g_cw
    for k in _WEIGHTS:
        if k not in _BIG:
            shp = w[k].shape
            grads[k], delta[k], new_m[k], new_v[k] = (t[k].reshape(shp) for t in (gs, ds, ms, vs))

    def lift(t, k):
        return t[k] if k == "norm_final_g" else t[k][None]

    outs = [loss, dx[None]]
    for t in (grads, delta, new_m, new_v):
        outs += [lift(t, k) for k in _WEIGHTS]
    return tuple(outs)
```

```python
import functools

import jax
import jax.numpy as jnp
from jax import lax
from jax.experimental import pallas as pl
from jax.experimental.pallas import tpu as pltpu

F32 = jnp.float32
BF = jnp.bfloat16

D = 1024
NDEV = 8
EPS = 1e-6
RG_C = 8.0
CHUNK = 128
NGRP = 8
GW = 128
NQ = 4
QW = 256
RC = 16
SMALL_ROWS = 320
SMALL_PER = SMALL_ROWS // NDEV

ADAM_LR = 0.001
ADAM_B1 = 0.9
ADAM_B2 = 0.999
ADAM_EPS = 1e-08
ADAM_WD = 0.01
ADAM_STEP = 10

VMEM_LIMIT = 60 * 1024 * 1024

MESH = pl.DeviceIdType.MESH


def _rows(n, fn, unroll=2, rc=RC):
    def body(i, c):
        fn(pl.multiple_of(i * rc, rc))
        return c
    lax.fori_loop(0, n // rc, body, 0, unroll=unroll)


def _fold(v):
    return jnp.sum(v.reshape(v.shape[0] // RC, RC, v.shape[1]), axis=0)


def _dot(a, b):
    return jnp.dot(a, b, preferred_element_type=F32)


def _dot_nt(a, b):
    return lax.dot_general(a, b, (((1,), (1,)), ((), ())), preferred_element_type=F32)


def _dot_tn(a, b):
    return lax.dot_general(a, b, (((0,), (0,)), ((), ())), preferred_element_type=F32)


_GC = 0.7978845608028654
_GK = 0.044715


def _gelu(x):
    t = jnp.tanh(_GC * (x + _GK * (x * x * x)))
    return x * (0.5 * (1.0 + t))


def _gelu_grad(x):
    x2 = x * x
    t = jnp.tanh(_GC * (x + _GK * (x2 * x)))
    cdf = 0.5 * (1.0 + t)
    dg = cdf + (0.5 * x) * (1.0 - t * t) * (_GC * (1.0 + (3.0 * _GK) * x2))
    return x * cdf, dg


def _sigmoid(x):
    return jax.nn.sigmoid(x)


def _log1p(e):
    u = 1.0 + e
    d = u - 1.0
    return jnp.where(d == 0.0, e, jnp.log(u) * (e / jnp.where(d == 0.0, 1.0, d)))


def _softplus(z):
    return jnp.maximum(z, 0.0) + _log1p(jnp.exp(-jnp.abs(z)))


def _neg_expm1(z):
    u = jnp.exp(z)
    lu = jnp.log(u)
    k = (1.0 - u) * (z / jnp.where(lu == 0.0, 1.0, lu))
    small = jnp.where(lu == 0.0, -z, k)
    return jnp.where(z > -0.5, small, 1.0 - u)


def _shift_back(prev8, cur, j):
    cat = jnp.concatenate([prev8, cur], axis=0)
    return pltpu.roll(cat, j, 0)[8:8 + cur.shape[0]]


def _shift_fwd(cur, next8, j):
    cat = jnp.concatenate([cur, next8], axis=0)
    n = cat.shape[0]
    return pltpu.roll(cat, n - j, 0)[0:cur.shape[0]]


def _const_spec(shape):
    nd = len(shape)
    return pl.BlockSpec(shape, lambda *_: (0,) * nd, pipeline_mode=pl.Buffered(1))


def _params(sem):
    return pltpu.CompilerParams(dimension_semantics=sem, vmem_limit_bytes=VMEM_LIMIT)


TM_PROJ = 1024
TM_MIX = 256
TM_FFN = 256
TM_DX = 512
TS_DW = 4096


_CHIPS = (4, 2, 6)
_PASS_FLIPS = (0, 1, 4, 2, 6, 5, 3, 7)


class _Gather:
    def __init__(self, ins, outs, send_sems, recv_sems, local_sems):
        self.ins, self.outs = ins, outs
        self.send_sems, self.recv_sems, self.local_sems = send_sems, recv_sems, local_sems
        self.me = _me()
        self.sibling = _flip(self.me, 1)

    def _copy(self, a, kind, block, to, src=None):
        dst = self.outs[a].at[_lin(block)]
        return pltpu.make_async_remote_copy(
            src_ref=dst if src is None else src, dst_ref=dst,
            send_sem=self.send_sems.at[a, kind], recv_sem=self.recv_sems.at[a, kind],
            device_id=to, device_id_type=MESH)

    def _local(self, a):
        return pltpu.make_async_copy(self.ins[a], self.outs[a].at[_lin(self.me)], self.local_sems.at[a])

    def start(self):
        for a in range(len(self.ins)):
            self._local(a).start()
            self._copy(a, 0, self.me, self.sibling, src=self.ins[a]).start()
            for j, f in enumerate(_CHIPS):
                self._copy(a, 1 + j, self.me, _flip(self.me, f), src=self.ins[a]).start()

    def forward(self):
        for j, f in enumerate(_CHIPS):
            for a in range(len(self.ins)):
                self._copy(a, 1 + j, _flip(self.me, f), self.me).wait_recv()
                self._copy(a, 4 + j, _flip(self.me, f), self.sibling).start()

    def finish(self):
        for a in range(len(self.ins)):
            self._copy(a, 0, self.sibling, self.me).wait_recv()
            for j, f in enumerate(_CHIPS):
                self._copy(a, 4 + j, _flip(self.me, f | 1), self.me).wait_recv()
            self._copy(a, 0, self.me, self.sibling, src=self.ins[a]).wait_send()
            for j, f in enumerate(_CHIPS):
                self._copy(a, 1 + j, self.me, _flip(self.me, f), src=self.ins[a]).wait_send()
                self._copy(a, 4 + j, _flip(self.me, f), self.sibling).wait_send()
            self._local(a).wait()


class _Exchange:
    def __init__(self, ins, outs, send_sems, recv_sems, local_sems):
        self.ins, self.outs = ins, outs
        self.send_sems, self.recv_sems, self.local_sems = send_sems, recv_sems, local_sems
        self.me = _me()

    def _copy(self, a, r, outgoing):
        peer = _flip(self.me, r)
        src, dst = (peer, self.me) if outgoing else (self.me, peer)
        return pltpu.make_async_remote_copy(
            src_ref=self.ins[a].at[_lin(src)], dst_ref=self.outs[a].at[_lin(dst)],
            send_sem=self.send_sems.at[a, r - 1], recv_sem=self.recv_sems.at[a, r - 1],
            device_id=peer, device_id_type=MESH)

    def _local(self, a):
        mi = _lin(self.me)
        return pltpu.make_async_copy(self.ins[a].at[mi], self.outs[a].at[mi], self.local_sems.at[a])

    def start(self):
        for a in range(len(self.ins)):
            self._local(a).start()
        for r in range(1, NDEV):
            for a in range(len(self.ins)):
                self._copy(a, r, True).start()

    def finish(self):
        for r in range(1, NDEV):
            for a in range(len(self.ins)):
                self._copy(a, r, False).wait_recv()
        for r in range(1, NDEV):
            for a in range(len(self.ins)):
                self._copy(a, r, True).wait_send()
        for a in range(len(self.ins)):
            self._local(a).wait()


def _gather_sems(n):
    return [pltpu.SemaphoreType.DMA((n, 7)), pltpu.SemaphoreType.DMA((n, 7)), pltpu.SemaphoreType.DMA((n,))]


def _proj_gather_call(x, gmix, w_shard, order, extras):
    s = x.shape[0]
    tm = min(TM_PROJ, s)
    nt = s // tm
    wb = w_shard.shape[1]
    n = len(extras)

    def body(order_ref, x_ref, g_ref, wsh_ref, *rest):
        ex_in = rest[:n]
        proj_hbm, h1_hbm, wall_hbm = rest[n:n + 3]
        ex_out = rest[n + 3:2 * n + 3]
        wall, hc, pbuf, ws_send, ws_recv, ex_send, ex_recv, ex_local, out_sems, psems = rest[2 * n + 3:]
        k = pl.program_id(0)
        i = pl.program_id(1)
        me = _me()
        sibling = _flip(me, 1)
        gather = _Gather(ex_in, ex_out, ex_send, ex_recv, ex_local)

        def wcopy(kind, block, to):
            ref = wall.at[_lin(block)]
            return pltpu.make_async_remote_copy(
                src_ref=ref, dst_ref=ref, send_sem=ws_send.at[kind], recv_sem=ws_recv.at[kind],
                device_id=to, device_id_type=MESH)

        head = i == 0

        @pl.when(head & (k == 0))
        def _():
            wall[_lin(me)] = wsh_ref[...]
            wcopy(0, me, sibling).start()
            for j, f in enumerate(_CHIPS):
                wcopy(1 + j, me, _flip(me, f)).start()
            gather.start()

        out_w = pltpu.make_async_copy(wall, wall_hbm, out_sems.at[0])
        out_h = pltpu.make_async_copy(hc, h1_hbm, out_sems.at[1])

        @pl.when(head & (k == 1))
        def _():
            wcopy(0, sibling, me).wait_recv()
            out_h.start()

        for j, f in enumerate(_CHIPS):
            @pl.when(head & (k == 2 + j))
            def _(j=j, f=f):
                wcopy(1 + j, _flip(me, f), me).wait_recv()
                wcopy(4 + j, _flip(me, f), sibling).start()

            @pl.when(head & (k == 5 + j))
            def _(j=j, f=f):
                wcopy(4 + j, _flip(me, f | 1), me).wait_recv()
                if 5 + j == NDEV - 1:
                    out_w.start()

        pl.when(head & (k == 5))(gather.forward)

        base = pl.multiple_of(i * tm, tm)

        @pl.when(k == 0)
        def _():
            g = g_ref[...]

            def norm(r0):
                xx = x_ref[pl.ds(r0, RC), :]
                r = lax.rsqrt(jnp.mean(xx * xx, axis=-1, keepdims=True) + EPS)
                hc[pl.ds(base + r0, RC), :] = ((xx * r) * g).astype(BF)
            _rows(tm, norm, unroll=4)

        blk = order_ref[k]
        wk = wall.at[blk]
        step = k * nt + i
        slot = lax.rem(step, 2)
        subs = [(r0, min(TM_MIX, tm - r0)) for r0 in range(0, tm, TM_MIX)]

        def out_copy(q, r0, rs):
            return pltpu.make_async_copy(
                pbuf.at[slot, pl.ds(r0, rs)],
                proj_hbm.at[pl.ds(base + r0, rs), pl.ds(pl.multiple_of(blk * wb, 128), wb)], psems.at[slot, q])

        @pl.when(step >= 2)
        def _():
            for q, (r0, rs) in enumerate(subs):
                out_copy(q, r0, rs).wait()

        for q, (r0, rs) in enumerate(subs):
            pbuf[slot, r0:r0 + rs, :] = _dot(hc[pl.ds(base + r0, rs), :], wk[...])
            out_copy(q, r0, rs).start()

        @pl.when((k == NDEV - 1) & (i == nt - 1))
        def _():
            wcopy(0, me, sibling).wait_send()
            for j, f in enumerate(_CHIPS):
                wcopy(1 + j, me, _flip(me, f)).wait_send()
                wcopy(4 + j, _flip(me, f), sibling).wait_send()
            gather.finish()
            out_w.wait()
            out_h.wait()
            for q, (r0, rs) in enumerate(subs):
                out_copy(q, r0, rs).wait()
                pltpu.make_async_copy(pbuf.at[1 - slot, pl.ds(r0, rs)], proj_hbm.at[pl.ds(r0, rs), pl.ds(0, wb)],
                                      psems.at[1 - slot, q]).wait()

    any_spec = pl.BlockSpec(memory_space=pl.ANY)
    grid_spec = pltpu.PrefetchScalarGridSpec(
        num_scalar_prefetch=1, grid=(NDEV, nt),
        in_specs=[pl.BlockSpec((tm, D), lambda k, i, o: (jnp.where(k == 0, i, nt - 1), 0)),
                  pl.BlockSpec((1, D), lambda k, i, o: (0, 0)),
                  pl.BlockSpec(w_shard.shape, lambda k, i, o: (0, 0))] + [any_spec] * n,
        out_specs=[any_spec, any_spec, any_spec] + [any_spec] * n,
        scratch_shapes=[pltpu.VMEM((NDEV,) + w_shard.shape, BF), pltpu.VMEM((s, D), BF), pltpu.VMEM((2, tm, wb), F32),
                        pltpu.SemaphoreType.DMA((7,)), pltpu.SemaphoreType.DMA((7,))] + _gather_sems(n)
        + [pltpu.SemaphoreType.DMA((2,)), pltpu.SemaphoreType.DMA((2, -(-tm // TM_MIX)))])
    outs = pl.pallas_call(
        body, name="proj_gather", grid_spec=grid_spec,
        out_shape=[jax.ShapeDtypeStruct((s, NDEV * wb), F32), jax.ShapeDtypeStruct((s, D), BF),
                   jax.ShapeDtypeStruct((NDEV,) + w_shard.shape, BF)]
        + [jax.ShapeDtypeStruct((NDEV,) + e.shape, e.dtype) for e in extras],
        compiler_params=_params(("arbitrary", "arbitrary")),
    )(order, x, gmix, w_shard, *extras)
    return outs[0], outs[1], outs[2], outs[3:]


def _conv_tile(rx, prev8, cw_ref, cb):
    xc = cb + cw_ref[3:4, :] * rx
    for j in (1, 2, 3):
        xc = xc + cw_ref[3 - j:4 - j, :] * _shift_back(prev8, rx, j)
    return xc


def _gate_tile(xcb, wax_ref, ba, bx, sp, q):
    cs = slice(q * QW, (q + 1) * QW)
    z = _dot(xcb[:, cs], wax_ref[q])
    r = _sigmoid(z[:, 0:QW] + ba[:, cs])
    ig = _sigmoid(z[:, QW:2 * QW] + bx[:, cs])
    return r, ig, (-RG_C * r) * sp[:, cs]


def _scan_fwd(a_s, b_s, out_ref, h0, n):
    rowi = lax.broadcasted_iota(jnp.int32, (8, D), 0)

    def block(t, h):
        rows = pl.ds(t * 8, 8)
        a = a_s[rows, :]
        b = b_s[rows, :]
        for d in (1, 2, 4):
            m = rowi >= d
            b = jnp.where(m, a * pltpu.roll(b, d, 0) + b, b)
            a = jnp.where(m, a * pltpu.roll(a, d, 0), a)
        hb = b + a * h
        out_ref[rows, :] = hb
        return hb[7:8, :]
    h = h0
    for t in range(n // 8):
        h = block(t, h)
    return h


def _scan_bwd(a_s, lm_s, c0, n):
    rowi = lax.broadcasted_iota(jnp.int32, (8, D), 0)
    nblk = n // 8

    def block(k, cin):
        rows = pl.ds((nblk - 1 - k) * 8, 8)
        a = a_s[rows, :]
        dh = lm_s[rows, :]
        b = a * dh
        for d in (1, 2, 4):
            m = rowi < 8 - d
            b = jnp.where(m, a * pltpu.roll(b, 8 - d, 0) + b, b)
            a = jnp.where(m, a * pltpu.roll(a, 8 - d, 0), a)
        mu = b + a * cin
        lm_s[rows, :] = dh + jnp.where(rowi < 7, pltpu.roll(mu, 7, 0), cin)
        return mu[0:1, :]
    c = c0
    for k in range(nblk):
        c = block(k, c)
    return c


def _mixer_fwd_call(proj, x, cw, cb, ba, bx, lam, lng, lnb, wax, wtr, bias, wpa, wpb, wo, shards):
    s = x.shape[0]
    tm = min(TM_MIX, s)
    nt = s // tm
    pw = proj.shape[1]
    n = len(shards)

    def body(proj_ref, x_ref, cw_ref, cb_ref, ba_ref, bx_ref, lam_ref, lng_ref, lnb_ref, wax_ref, wtr_ref,
             bias_ref, wpa_ref, wpb_ref, wo_ref, *rest):
        sh_in = rest[:n]
        h_ref, pa_ref, pb_ref, x1_ref, xc_ref, r_ref, ig_ref, a_s, m_ref, ya_ref, yb_ref, mg_ref = rest[n:n + 12]
        sh_out = rest[n + 12:2 * n + 12]
        prev_s, b_s, hc_s, vn_s, mx_s, g_send, g_recv, g_local = rest[2 * n + 12:]
        i = pl.program_id(0)
        gather = _Gather(sh_in, sh_out, g_send, g_recv, g_local)

        @pl.when(i == 0)
        def _():
            gather.start()
            prev_s[...] = jnp.zeros((8, D), F32)
            hc_s[...] = jnp.zeros((8, D), F32)

        pl.when(i == nt // 2)(gather.forward)

        rx = proj_ref[:, 0:D]
        xc = _conv_tile(rx, prev_s[...], cw_ref, cb_ref[...])
        prev_s[...] = rx[tm - 8:tm, :]
        xc_ref[...] = xc.astype(BF)
        xcb = xc.astype(BF)
        sp = _softplus(-lam_ref[...])
        ba = ba_ref[...]
        bx = bx_ref[...]
        for q in range(NQ):
            cs = slice(q * QW, (q + 1) * QW)
            r, ig, la = _gate_tile(xcb, wax_ref, ba, bx, sp, q)
            mq = jnp.sqrt(_neg_expm1(2.0 * la))
            r_ref[:, cs] = r.astype(BF)
            ig_ref[:, cs] = ig.astype(BF)
            m_ref[:, cs] = mq.astype(BF)
            a_s[:, cs] = jnp.exp(la)
            b_s[:, cs] = mq * (ig * xc[:, cs])

        gv = _gelu(proj_ref[:, 3 * D:4 * D])
        dv = gv - jnp.mean(gv, axis=-1, keepdims=True)
        var = jnp.mean(dv * dv, axis=-1, keepdims=True)
        vn_s[...] = ((dv * lax.rsqrt(var + EPS)) * lng_ref[...] + lnb_ref[...]).astype(BF)
        nc = tm // CHUNK
        for c in range(nc):
            rs = slice(c * CHUNK, (c + 1) * CHUNK)
            for g in range(NGRP):
                cs = slice(g * GW, (g + 1) * GW)
                mx_s[rs, cs] = _dot(wtr_ref[g], vn_s[rs, cs])
        mixed = mx_s[...] + jnp.concatenate([bias_ref[...]] * nc, axis=0)
        yb = (_gelu(proj_ref[:, 2 * D:3 * D]) * mixed).astype(BF)
        yb_ref[...] = yb
        pb = _dot(yb, wpb_ref[...])
        pb_ref[...] = pb.astype(BF)
        mx_s[...] = pb

        hc_s[0:1, :] = _scan_fwd(a_s, b_s, h_ref, hc_s[0:1, :], tm)

        ya = (_gelu(proj_ref[:, D:2 * D]) * h_ref[...]).astype(BF)
        ya_ref[...] = ya
        pa = _dot(ya, wpa_ref[...])
        pa_ref[...] = pa.astype(BF)
        mg = (_sigmoid(proj_ref[:, 4 * D:5 * D]) * pa + _sigmoid(proj_ref[:, 5 * D:6 * D]) * mx_s[...]).astype(BF)
        mg_ref[...] = mg
        x1_ref[...] = x_ref[...] + _dot(mg, wo_ref[...])

        pl.when(i == nt - 1)(gather.finish)

    tile = lambda w: pl.BlockSpec((tm, w), lambda i: (i, 0))
    vec = _const_spec((1, D))
    any_spec = pl.BlockSpec(memory_space=pl.ANY)
    outs = pl.pallas_call(
        body, name="mixer_fwd", grid=(nt,),
        in_specs=[tile(pw), tile(D), _const_spec((4, D)), vec, vec, vec, vec, vec, vec,
                  _const_spec(wax.shape), _const_spec(wtr.shape), _const_spec(bias.shape),
                  _const_spec((D, D)), _const_spec((D, D)), _const_spec((D, D))] + [any_spec] * n,
        out_specs=[tile(D)] * 12 + [any_spec] * n,
        out_shape=[jax.ShapeDtypeStruct((s, D), dt) for dt in (F32, BF, BF, F32, BF, BF, BF, F32, BF, BF, BF, BF)]
        + [jax.ShapeDtypeStruct((NDEV,) + e.shape, e.dtype) for e in shards],
        scratch_shapes=[pltpu.VMEM((8, D), F32), pltpu.VMEM((tm, D), F32),
                        pltpu.VMEM((8, D), F32), pltpu.VMEM((tm, D), BF), pltpu.VMEM((tm, D), F32)]
        + _gather_sems(n),
        compiler_params=_params(("arbitrary",)),
    )(proj, x, cw, cb, ba, bx, lam, lng, lnb, wax, wtr, bias, wpa, wpb, wo, *shards)
    return outs[:12], outs[12:]


def _ffn_call(x1, tgt, gffn, gfin, wgu, wdn):
    s = x1.shape[0]
    tm = min(TM_FFN, s)
    nt = s // tm
    nh = wdn.shape[0]
    fb = wgu.shape[2]

    def body(x1_ref, tgt_ref, gffn_ref, gfin_ref, wgu_ref, wdn_ref,
             dx1_ref, dx1b_ref, h2_ref, act_ref, dgu_ref, dx2b_ref, acc_ref,
             g_s, u_s, dx2_s, accs):
        i = pl.program_id(0)

        @pl.when(i == 0)
        def _():
            accs[...] = jnp.zeros(accs.shape, F32)

        gffn = gffn_ref[...]
        gfin = gfin_ref[...]

        x1 = x1_ref[...]
        r2 = lax.rsqrt(jnp.mean(x1 * x1, axis=-1, keepdims=True) + EPS)
        xh2 = x1 * r2
        h2 = (xh2 * gffn).astype(BF)
        h2_ref[...] = h2

        for k in range(nh):
            g = _dot(h2, wgu_ref[k])
            u = _dot(h2, wgu_ref[k + nh])
            g_s[k] = g
            u_s[k] = u
            act_ref[k] = ((g * _sigmoid(g)) * u).astype(BF)

        x2 = x1
        for k in range(nh):
            x2 = x2 + _dot(act_ref[k], wdn_ref[k])

        r3 = lax.rsqrt(jnp.mean(x2 * x2, axis=-1, keepdims=True) + EPS)
        xh = x2 * r3
        err = xh * gfin - tgt_ref[...]
        accs[2] += _fold(err * err)
        dy = err * (1.0 / D)
        accs[1] += _fold(dy * xh)
        dxh = dy * gfin
        dx2 = r3 * (dxh - xh * jnp.mean(dxh * xh, axis=-1, keepdims=True))
        dx2_s[...] = dx2
        dx2b = dx2.astype(BF)
        dx2b_ref[...] = dx2b

        for k in range(nh):
            da = _dot_nt(dx2b, wdn_ref[k])
            g = g_s[k]
            sg = _sigmoid(g)
            dgu_ref[k] = ((da * u_s[k]) * (sg * (1.0 + g * (1.0 - sg)))).astype(BF)
            dgu_ref[k + nh] = (da * (g * sg)).astype(BF)

        dh2 = _dot_nt(dgu_ref[0], wgu_ref[0])
        for k in range(1, 2 * nh):
            dh2 = dh2 + _dot_nt(dgu_ref[k], wgu_ref[k])

        accs[0] += _fold(dh2 * xh2)
        dxh = dh2 * gffn
        dx1 = dx2_s[...] + r2 * (dxh - xh2 * jnp.mean(dxh * xh2, axis=-1, keepdims=True))
        dx1_ref[...] = dx1
        dx1b_ref[...] = dx1.astype(BF)

        @pl.when(i == nt - 1)
        def _():
            acc_ref[...] = jnp.zeros((8, D), F32)
            for j in range(3):
                acc_ref[j:j + 1, :] = jnp.sum(accs[j], axis=0, keepdims=True)

    tile = lambda w: pl.BlockSpec((tm, w), lambda i: (i, 0))
    vec = _const_spec((1, D))
    return pl.pallas_call(
        body, name="ffn", grid=(nt,),
        in_specs=[tile(D), tile(D), vec, vec, _const_spec(wgu.shape), _const_spec(wdn.shape)],
        out_specs=[tile(D), tile(D), tile(D),
                   pl.BlockSpec((nh, tm, fb), lambda i: (0, i, 0)),
                   pl.BlockSpec((2 * nh, tm, fb), lambda i: (0, i, 0)),
                   tile(D), pl.BlockSpec((8, D), lambda i: (0, 0))],
        out_shape=[jax.ShapeDtypeStruct((s, D), F32), jax.ShapeDtypeStruct((s, D), BF),
                   jax.ShapeDtypeStruct((s, D), BF), jax.ShapeDtypeStruct((nh, s, fb), BF),
                   jax.ShapeDtypeStruct((2 * nh, s, fb), BF), jax.ShapeDtypeStruct((s, D), BF),
                   jax.ShapeDtypeStruct((8, D), F32)],
        scratch_shapes=[pltpu.VMEM((nh, tm, fb), F32), pltpu.VMEM((nh, tm, fb), F32), pltpu.VMEM((tm, D), F32),
                        pltpu.VMEM((3, RC, D), F32)],
        compiler_params=_params(("arbitrary",)),
    )(x1, tgt, gffn, gfin, wgu, wdn)


def _mixer_bwd_pre_call(dx1b, proj, pa, pb, lng, lnb, wtr, wtrt, bias, wpa, wpb, wo, parts):
    s = dx1b.shape[0]
    tm = min(TM_MIX, s)
    nt = s // tm
    pw = proj.shape[1]
    n = len(parts)

    def body(dx1b_ref, proj_hbm, pa_ref, pb_ref, lng_ref, lnb_ref, wtr_ref, wtrt_ref, bias_ref,
             wpa_ref, wpb_ref, wo_ref, *rest):
        ex_in = rest[:n]
        dproj_ref, dya_ref, dpa_ref, dpb_ref, vec_ref, dws_ref, dbs_ref = rest[n:n + 7]
        ex_out = rest[n + 7:2 * n + 7]
        vn_s, mx_s, dmx_s, dvn_s, accs, dbs_s, e_send, e_recv, e_local, pbuf, psem = rest[2 * n + 7:]
        i = pl.program_id(0)
        exchange = _Exchange(ex_in, ex_out, e_send, e_recv, e_local)

        def fetch(t):
            k = lax.rem(t, 3)
            return pltpu.make_async_copy(
                proj_hbm.at[pl.ds(pl.multiple_of(t * tm, tm), tm), pl.ds(2 * D, 4 * D)], pbuf.at[k], psem.at[k])

        @pl.when(i == 0)
        def _():
            for t in range(min(2, nt)):
                fetch(t).start()

        @pl.when(i + 2 < nt)
        def _():
            fetch(i + 2).start()

        fetch(i).wait()
        slot = lax.rem(i, 3)
        uv_ref = pbuf.at[slot, :, 0:2 * D]
        gg_ref = pbuf.at[slot, :, 2 * D:4 * D]

        @pl.when(i == 0)
        def _():
            exchange.start()
            accs[...] = jnp.zeros(accs.shape, F32)
            dbs_s[...] = jnp.zeros(dbs_s.shape, F32)
            dws_ref[...] = jnp.zeros(dws_ref.shape, F32)

        dm = _dot_nt(dx1b_ref[...], wo_ref[...])
        sa = _sigmoid(gg_ref[:, 0:D])
        sb = _sigmoid(gg_ref[:, D:2 * D])
        dpa = dm * sa
        dpb = dm * sb
        dpab = dpa.astype(BF)
        dpbb = dpb.astype(BF)
        dpa_ref[...] = dpab
        dpb_ref[...] = dpbb
        dproj_ref[:, 2 * D:3 * D] = ((dpa * pa_ref[...].astype(F32)) * (1.0 - sa)).astype(BF)
        dproj_ref[:, 3 * D:4 * D] = ((dpb * pb_ref[...].astype(F32)) * (1.0 - sb)).astype(BF)

        dya_ref[...] = _dot_nt(dpab, wpa_ref[...])
        dyb = _dot_nt(dpbb, wpb_ref[...])

        lng = lng_ref[...]
        gv, dgelu_v = _gelu_grad(uv_ref[:, D:2 * D])
        dv = gv - jnp.mean(gv, axis=-1, keepdims=True)
        rstd = lax.rsqrt(jnp.mean(dv * dv, axis=-1, keepdims=True) + EPS)
        xh = dv * rstd
        vn_s[...] = (xh * lng + lnb_ref[...]).astype(BF)

        nc = tm // CHUNK
        for c in range(nc):
            rs = slice(c * CHUNK, (c + 1) * CHUNK)
            for g in range(NGRP):
                cs = slice(g * GW, (g + 1) * GW)
                mx_s[rs, cs] = _dot(wtr_ref[g], vn_s[rs, cs])

        gu, dgelu_u = _gelu_grad(uv_ref[:, 0:D])
        mixed = mx_s[...] + jnp.concatenate([bias_ref[...]] * nc, axis=0)
        dproj_ref[:, 0:D] = ((dyb * mixed) * dgelu_u).astype(BF)
        dmx = dyb * gu
        dmx_s[...] = dmx.astype(BF)
        dbs_s[...] += jnp.sum(dmx.reshape(nc, CHUNK, D), axis=0)

        for c in range(nc):
            rs = slice(c * CHUNK, (c + 1) * CHUNK)
            for g in range(NGRP):
                cs = slice(g * GW, (g + 1) * GW)
                dvn_s[rs, cs] = _dot(wtrt_ref[g], dmx_s[rs, cs])
                dws_ref[g] += _dot_nt(dmx_s[rs, cs], vn_s[rs, cs])

        dvn = dvn_s[...]
        accs[0] += _fold(dvn * xh)
        accs[1] += _fold(dvn)
        dxh = dvn * lng
        m1 = jnp.mean(dxh, axis=-1, keepdims=True)
        m2 = jnp.mean(dxh * xh, axis=-1, keepdims=True)
        dproj_ref[:, D:2 * D] = ((rstd * (dxh - m1 - xh * m2)) * dgelu_v).astype(BF)

        @pl.when(i == nt - 1)
        def _():
            vec_ref[...] = jnp.zeros((8, D), F32)
            for j in range(2):
                vec_ref[j:j + 1, :] = jnp.sum(accs[j], axis=0, keepdims=True)
            row = lax.broadcasted_iota(jnp.int32, (CHUNK, CHUNK), 0)
            col = lax.broadcasted_iota(jnp.int32, (CHUNK, CHUNK), 1)
            for g in range(NGRP):
                dws_ref[g] = jnp.where(row >= col, dws_ref[g], 0.0)
                gs = jnp.sum(dbs_s[:, g * GW:(g + 1) * GW], axis=1, keepdims=True)
                dbs_ref[:, g * GW:(g + 1) * GW] = jnp.broadcast_to(gs, (CHUNK, GW))
            exchange.finish()

    tile = lambda w: pl.BlockSpec((tm, w), lambda i: (i, 0))
    vec = _const_spec((1, D))
    any_spec = pl.BlockSpec(memory_space=pl.ANY)
    outs = pl.pallas_call(
        body, name="mixer_bwd_pre", grid=(nt,),
        in_specs=[tile(D), any_spec,
                  tile(D), tile(D), vec, vec, _const_spec(wtr.shape), _const_spec(wtrt.shape),
                  _const_spec(bias.shape), _const_spec((D, D)), _const_spec((D, D)), _const_spec((D, D))]
        + [any_spec] * n,
        out_specs=[tile(4 * D), tile(D), tile(D), tile(D), pl.BlockSpec((8, D), lambda i: (0, 0)),
                   pl.BlockSpec((NGRP, CHUNK, CHUNK), lambda i: (0, 0, 0)),
                   pl.BlockSpec((CHUNK, D), lambda i: (0, 0))] + [any_spec] * n,
        out_shape=[jax.ShapeDtypeStruct((s, 4 * D), BF), jax.ShapeDtypeStruct((s, D), F32),
                   jax.ShapeDtypeStruct((s, D), BF), jax.ShapeDtypeStruct((s, D), BF),
                   jax.ShapeDtypeStruct((8, D), F32), jax.ShapeDtypeStruct((NGRP, CHUNK, CHUNK), F32),
                   jax.ShapeDtypeStruct((CHUNK, D), F32)]
        + [jax.ShapeDtypeStruct(p.shape, p.dtype) for p in parts],
        scratch_shapes=[pltpu.VMEM((tm, D), BF), pltpu.VMEM((tm, D), F32), pltpu.VMEM((tm, D), BF),
                        pltpu.VMEM((tm, D), F32), pltpu.VMEM((2, RC, D), F32), pltpu.VMEM((CHUNK, D), F32)]
        + _gather_sems(n) + [pltpu.VMEM((3, tm, 4 * D), F32), pltpu.SemaphoreType.DMA((3,))],
        compiler_params=_params(("arbitrary",)),
    )(dx1b, proj, pa, pb, lng, lnb, wtr, wtrt, bias, wpa, wpb, wo, *parts)
    return outs[:7], outs[7:]


def _mixer_bwd_seq_call(dprojb, dya, proj, h, x, dx1, gates, gmix, w_in_all, cw, lam, wax, parts):
    s = dya.shape[0]
    tm = min(TM_MIX, s)
    nt = s // tm
    tb = tm // 8
    n = len(parts)
    nb, _, wb = w_in_all.shape

    def body(dpb_ref, dya_ref, xg_ref, xh8_ref, h_ref, hh8_ref, x_ref, dx1_ref, xc_s, r_s, ig_s, a_s, m_s,
             gmix_ref, win_ref, cw_ref, lam_ref, wax_ref, *rest):
        ex_in = rest[:n]
        dpa_ref, dx_ref, vec_ref, dwax_ref = rest[n:n + 4]
        ex_out = rest[n + 4:2 * n + 4]
        lm_s, dh_s, dxc_s, c_s, accs, e_send, e_recv, e_local = rest[2 * n + 4:]
        i = pl.program_id(0)
        exchange = _Exchange(ex_in, ex_out, e_send, e_recv, e_local)

        @pl.when(i == 0)
        def _():
            exchange.start()
            accs[...] = jnp.zeros(accs.shape, F32)
            dwax_ref[...] = jnp.zeros(dwax_ref.shape, F32)
            c_s[...] = jnp.zeros((8, D), F32)
            dxc_s[tm:tm + 8, :] = jnp.zeros((8, D), F32)

        first_tile = i == nt - 1
        prev8 = jnp.where(first_tile, 0.0, xh8_ref[...])
        hprev8 = jnp.where(first_tile, 0.0, hh8_ref[...])
        lamv = lam_ref[...]
        sp = _softplus(-lamv)
        hv = h_ref[...]
        g, dg = _gelu_grad(xg_ref[:, D:2 * D])
        dya = dya_ref[...]
        lm_s[...] = dya * g
        drg = ((dya * hv) * dg).astype(BF)
        dpa_ref[:, D:2 * D] = drg
        dpa_ref[:, 2 * D:6 * D] = dpb_ref[...]

        def dpb_block(k):
            return _dot_nt(dpb_ref[:, k * wb - 2 * D:(k + 1) * wb - 2 * D], win_ref[k])
        dh = (_dot_nt(drg[:, 0:2 * wb - D], win_ref[1, :, D - wb:wb])
              + _dot_nt(drg[:, 2 * wb - D:D], win_ref[2, :, 0:2 * D - 2 * wb])
              + _dot_nt(dpb_ref[:, 0:3 * wb - 2 * D], win_ref[2, :, 2 * D - 2 * wb:wb]))
        for k in range(3, 6):
            dh = dh + dpb_block(k)
        dh_s[...] = dh

        c_s[0:1, :] = _scan_bwd(a_s, lm_s, c_s[0:1, :], tm)

        hprev = _shift_back(hprev8, h_ref[...], 1)
        for q in range(NQ):
            cs = slice(q * QW, (q + 1) * QW)
            r = r_s[:, cs].astype(F32)
            ig = ig_s[:, cs].astype(F32)
            a = a_s[:, cs]
            m = m_s[:, cs].astype(F32)
            lm = lm_s[:, cs]
            xq = xc_s[:, cs].astype(F32)
            dixc = lm * m
            dla = (lm * hprev[:, cs]) * a - ((lm * (ig * xq)) * (a * a)) / m
            accs[3, :, cs] += _fold(dla * r)
            dza = (dla * (-RG_C * sp[:, cs])) * (r * (1.0 - r))
            dzx = (dixc * xq) * (ig * (1.0 - ig))
            accs[1, :, cs] += _fold(dza)
            accs[2, :, cs] += _fold(dzx)
            dz = jnp.concatenate([dza, dzx], axis=1).astype(BF)
            dxc_s[0:tm, cs] = dixc * ig + _dot_nt(dz, wax_ref[q])
            dwax_ref[q] += _dot_tn(xc_s[:, cs], dz)

        cur = dxc_s[0:tm, :]
        nx = dxc_s[tm:tm + 8, :]
        drx = cw_ref[3:4, :] * cur
        for j in (1, 2, 3):
            drx = drx + cw_ref[3 - j:4 - j, :] * _shift_fwd(cur, nx, j)
        accs[0] += _fold(cur)
        rx = xg_ref[:, 0:D]
        accs[7] += _fold(cur * rx)
        for j in (1, 2, 3):
            accs[7 - j] += _fold(cur * _shift_back(prev8, rx, j))
        dxc_s[tm:tm + 8, :] = cur[0:8, :]

        drxb = drx.astype(BF)
        dpa_ref[:, 0:D] = drxb
        dh = dh_s[...] + _dot_nt(drxb[:, 0:wb], win_ref[0]) + _dot_nt(drxb[:, wb:D], win_ref[1, :, 0:D - wb])
        for k in range(6, nb):
            dh = dh + dpb_block(k)
        xx = x_ref[...]
        rn = lax.rsqrt(jnp.mean(xx * xx, axis=-1, keepdims=True) + EPS)
        xhn = xx * rn
        accs[8] += _fold(dh * xhn)
        dxh = dh * gmix_ref[...]
        dx_ref[...] = dx1_ref[...] + rn * (dxh - xhn * jnp.mean(dxh * xhn, axis=-1, keepdims=True))

        @pl.when(i == nt - 1)
        def _():
            vec_ref[...] = jnp.zeros((16, D), F32)
            for j in range(9):
                vec_ref[j:j + 1, :] = jnp.sum(accs[j], axis=0, keepdims=True)
            vec_ref[3:4, :] = vec_ref[3:4, :] * (RG_C * _sigmoid(-lamv))
            exchange.finish()

    rev = lambda w: pl.BlockSpec((tm, w), lambda i: (nt - 1 - i, 0))
    halo = pl.BlockSpec((8, D), lambda i: (jnp.maximum((nt - 1 - i) * tb - 1, 0), 0))
    vec = _const_spec((1, D))
    any_spec = pl.BlockSpec(memory_space=pl.ANY)
    outs = pl.pallas_call(
        body, name="mixer_bwd_seq", grid=(nt,),
        in_specs=[rev(4 * D), rev(D), rev(2 * D), halo, rev(D), halo] + [rev(D)] * 7
        + [vec, _const_spec(w_in_all.shape), _const_spec((4, D)), vec, _const_spec(wax.shape)] + [any_spec] * n,
        out_specs=[rev(6 * D), rev(D), pl.BlockSpec((16, D), lambda i: (0, 0)),
                   pl.BlockSpec((NQ, QW, 2 * QW), lambda i: (0, 0, 0))] + [any_spec] * n,
        out_shape=[jax.ShapeDtypeStruct((s, 6 * D), BF), jax.ShapeDtypeStruct((s, D), F32),
                   jax.ShapeDtypeStruct((16, D), F32), jax.ShapeDtypeStruct((NQ, QW, 2 * QW), F32)]
        + [jax.ShapeDtypeStruct(p.shape, p.dtype) for p in parts],
        scratch_shapes=[pltpu.VMEM((tm, D), F32), pltpu.VMEM((tm, D), F32),
                        pltpu.VMEM((tm + 8, D), F32), pltpu.VMEM((8, D), F32), pltpu.VMEM((9, RC, D), F32)]
        + _gather_sems(n),
        compiler_params=_params(("arbitrary",)),
    )(dprojb, dya, proj, proj, h, h, x, dx1, *gates, gmix, w_in_all, cw, lam, wax, *parts)
    return outs[:4], outs[4:]


def _device_of(d):
    return (d // 4, lax.rem(d // 2, 2), lax.rem(d, 2))


_DW_PLAN = (("C", 3), ("A", 3), ("C", 1), ("A", 1), ("C", 2), ("A", 2), ("S", 0), ("O", 0))
_DW_FLIPS = tuple({"C": 2 * j + 1, "A": 2 * j, "S": 1, "O": 0}[kind] for kind, j in _DW_PLAN)


def _dw_exchange_call(name, order, a, b, a_spec, b_spec, k1, n1, s, small):
    ts = min(TS_DW, s)
    ns = s // ts
    nstep = len(_DW_PLAN)

    def slab(i, order_ref):
        return order_ref[i]

    def body(order_ref, a_ref, b_ref, g_ref, own_ref, recv_ref, gsum_ref, acc, sbuf, stage, send_sems, recv_sems,
             st_send, st_recv, local_sem, rbuf, gacc, send1, recv1, send2, recv2):
        i = pl.program_id(0)
        j = pl.program_id(1)
        me = _me()
        mi = _lin(me)
        sibling = _flip(me, 1)
        allreduce = _SmallAllReduce(g_ref, gacc, rbuf, send1, recv1, send2, recv2)
        pl.when((i == 0) & (j == 0))(allreduce.scatter)
        pl.when((i == nstep // 2) & (j == 0))(allreduce.reduce)
        @pl.when((i == 0) & (j == 0))
        def _():
            acc[...] = jnp.zeros(acc.shape, F32)

        acc[...] = _dot_tn(a_ref[...], b_ref[...]) + jnp.where(j == 0, 0.0, acc[...])

        def send(step):
            kind, jj = _DW_PLAN[step]
            src = sbuf.at[step % 2]
            if kind == "C":
                return pltpu.make_async_remote_copy(
                    src_ref=src, dst_ref=stage.at[jj - 1], send_sem=st_send.at[jj - 1], recv_sem=st_recv.at[jj - 1],
                    device_id=sibling, device_id_type=MESH)
            to = sibling if kind == "S" else _flip(me, 2 * jj)
            return pltpu.make_async_remote_copy(
                src_ref=src, dst_ref=recv_ref.at[mi], send_sem=send_sems.at[jj], recv_sem=recv_sems.at[jj],
                device_id=to, device_id_type=MESH)

        def arrival(jj):
            frm = sibling if jj == 0 else _flip(me, 2 * jj)
            return pltpu.make_async_remote_copy(
                src_ref=sbuf.at[0], dst_ref=recv_ref.at[_lin(frm)], send_sem=send_sems.at[jj],
                recv_sem=recv_sems.at[jj], device_id=frm, device_id_type=MESH)

        def staged(jj):
            return pltpu.make_async_remote_copy(
                src_ref=sbuf.at[0], dst_ref=stage.at[jj - 1], send_sem=st_send.at[jj - 1], recv_sem=st_recv.at[jj - 1],
                device_id=sibling, device_id_type=MESH)

        for step, (kind, jj) in enumerate(_DW_PLAN):
            @pl.when((i == step) & (j == ns - 1))
            def _(step=step, kind=kind, jj=jj):
                if step >= 2:
                    send(step - 2).wait_send()
                if kind == "A":
                    staged(jj).wait_recv()
                    sbuf[step % 2] = (acc[...] + stage[jj - 1].astype(F32)).astype(BF)
                else:
                    sbuf[step % 2] = acc[...].astype(BF)
                if kind != "O":
                    send(step).start()
                else:
                    own_ref[...] = acc[...]
                    mine = pltpu.make_async_copy(sbuf.at[step % 2], recv_ref.at[mi], local_sem)
                    mine.start()
                    send(step - 1).wait_send()
                    for q in range(4):
                        arrival(q).wait_recv()
                    mine.wait()
                    allreduce.finish()
                    gsum_ref[...] = gacc[...]

    vmem = pl.BlockSpec(memory_space=pltpu.VMEM)
    grid_spec = pltpu.PrefetchScalarGridSpec(
        num_scalar_prefetch=1, grid=(nstep, ns),
        in_specs=[a_spec(ts, slab), b_spec(ts, slab), vmem],
        out_specs=[pl.BlockSpec((k1, n1), lambda i, j, o: (0, 0)), pl.BlockSpec(memory_space=pl.ANY), vmem],
        scratch_shapes=[pltpu.VMEM((k1, n1), F32), pltpu.VMEM((2, k1, n1), BF), pltpu.VMEM((3, k1, n1), BF),
                        pltpu.SemaphoreType.DMA((4,)), pltpu.SemaphoreType.DMA((4,)),
                        pltpu.SemaphoreType.DMA((3,)), pltpu.SemaphoreType.DMA((3,)),
                        pltpu.SemaphoreType.DMA(()), pltpu.VMEM((NDEV, SMALL_PER, D), F32),
                        pltpu.VMEM((SMALL_ROWS, D), F32)]
        + [pltpu.SemaphoreType.DMA((NDEV - 1,))] * 4)
    return pl.pallas_call(
        body, name=name, grid_spec=grid_spec,
        out_shape=[jax.ShapeDtypeStruct((k1, n1), F32), jax.ShapeDtypeStruct((NDEV, k1, n1), BF),
                   jax.ShapeDtypeStruct((SMALL_ROWS, D), F32)],
        compiler_params=_params(("arbitrary", "arbitrary")),
    )(order, a, b, small)


def _dw_plain_call(name, me, a, b, a_spec, b_spec, split, k1, n1, s):
    nb = NDEV // split
    r = k1 // split
    ts = min(TS_DW, s)
    ns = s // ts

    def slab(i, me_ref):
        return i

    def body(me_ref, a_ref, b_ref, own_ref, part_ref, acc):
        i = pl.program_id(0)
        j = pl.program_id(1)
        @pl.when((i == 0) & (j == 0))
        def _():
            acc[...] = jnp.zeros(acc.shape, F32)

        acc[...] = _dot_tn(a_ref[...], b_ref[...]) + jnp.where(j == 0, 0.0, acc[...])

        @pl.when(j == ns - 1)
        def _():
            part_ref[...] = acc[...].astype(BF)

            @pl.when(i == me_ref[0] // split)
            def _():
                off = pl.multiple_of(lax.rem(me_ref[0], split) * r, RC)
                own_ref[...] = acc[pl.ds(off, r), :]

    grid_spec = pltpu.PrefetchScalarGridSpec(
        num_scalar_prefetch=1, grid=(nb, ns),
        in_specs=[a_spec(ts, slab), b_spec(ts, slab)],
        out_specs=[pl.BlockSpec((r, n1), lambda i, j, me_ref: (0, 0)),
                   pl.BlockSpec((None, k1, n1), lambda i, j, me_ref: (i, 0, 0))],
        scratch_shapes=[pltpu.VMEM((k1, n1), F32)])
    own, part = pl.pallas_call(
        body, name=name, grid_spec=grid_spec,
        out_shape=[jax.ShapeDtypeStruct((r, n1), F32), jax.ShapeDtypeStruct((nb, k1, n1), BF)],
        compiler_params=_params(("arbitrary", "arbitrary")),
    )(me, a, b)
    return own, part.reshape(NDEV, r, n1)


def _rows2d(w):
    return lambda ts, slab: pl.BlockSpec((ts, w), lambda i, j, me_ref: (j, 0))


def _cols2d(w):
    return lambda ts, slab: pl.BlockSpec((ts, w), lambda i, j, me_ref: (j, slab(i, me_ref)))


def _blk3d(w):
    return lambda ts, slab: pl.BlockSpec((None, ts, w), lambda i, j, me_ref: (slab(i, me_ref), j, 0))


_BC1 = 1.0 - ADAM_B1 ** ADAM_STEP
_BC2 = 1.0 - ADAM_B2 ** ADAM_STEP


def _adamw_math(w, g, m, v):
    m = ADAM_B1 * m + (1.0 - ADAM_B1) * g
    v = ADAM_B2 * v + (1.0 - ADAM_B2) * (g * g)
    m_hat = m / _BC1
    v_hat = v / _BC2
    delta = -ADAM_LR * (m_hat / (jnp.sqrt(v_hat) + ADAM_EPS) + ADAM_WD * w)
    return delta, m, v


def _row_tile(r):
    for t in (256, 176, 128, 64, 32, 16, 8):
        if r % t == 0:
            return t
    return r


def _reduce_adamw_call(name, sel, own, recv, w, m, v):
    r, c = own.shape
    tr = _row_tile(r)

    def body(sel_ref, own_ref, recv_ref, w_ref, m_ref, v_ref, g_ref, d_ref, nm_ref, nv_ref):
        g = jnp.zeros((tr, c), F32)
        for sdev in range(NDEV):
            part = jnp.where(sel_ref[sdev] == 1, recv_ref[sdev].astype(F32), 0.0)
            g = g + jnp.where(sel_ref[sdev] == 2, own_ref[...], part)
        g_ref[...] = g
        d_ref[...], nm_ref[...], nv_ref[...] = _adamw_math(w_ref[...], g, m_ref[...], v_ref[...])

    tile = pl.BlockSpec((tr, c), lambda i, me_ref: (i, 0))
    grid_spec = pltpu.PrefetchScalarGridSpec(
        num_scalar_prefetch=1, grid=(r // tr,),
        in_specs=[tile, pl.BlockSpec((NDEV, tr, c), lambda i, me_ref: (0, i, 0)), tile, tile, tile],
        out_specs=[tile] * 4)
    return pl.pallas_call(
        body, name=name, grid_spec=grid_spec,
        out_shape=[jax.ShapeDtypeStruct((r, c), F32)] * 4,
        compiler_params=_params(("parallel",)),
    )(sel, own, recv, w, m, v)


def _adamw_call(name, w, g, m, v):
    r, c = w.shape
    tr = _row_tile(r)

    def body(w_ref, g_ref, m_ref, v_ref, d_ref, nm_ref, nv_ref):
        d_ref[...], nm_ref[...], nv_ref[...] = _adamw_math(w_ref[...], g_ref[...], m_ref[...], v_ref[...])

    tile = pl.BlockSpec((tr, c), lambda i: (i, 0))
    return pl.pallas_call(
        body, name=name, grid=(r // tr,), in_specs=[tile] * 4, out_specs=[tile] * 3,
        out_shape=[jax.ShapeDtypeStruct((r, c), F32)] * 3,
        compiler_params=_params(("parallel",)),
    )(w, g, m, v)


def _me():
    return lax.axis_index("x"), lax.axis_index("y"), lax.axis_index("c")


def _flip(pos, r):
    x, y, c = pos
    return (1 - x if r & 4 else x, 1 - y if r & 2 else y, 1 - c if r & 1 else c)


def _lin(pos):
    return pos[0] * 4 + pos[1] * 2 + pos[2]


class _SmallAllReduce:
    def __init__(self, g_ref, out_ref, rbuf, send1, recv1, send2, recv2):
        self.g, self.out, self.rbuf = g_ref, out_ref, rbuf
        self.sems = (send1, recv1, send2, recv2)
        self.me = _me()
        self.mi = _lin(self.me)

    @staticmethod
    def _rows(d):
        return pl.ds(pl.multiple_of(d * SMALL_PER, 8), SMALL_PER)

    def _scatter(self, r, outgoing):
        peer = _flip(self.me, r)
        src, dst = (_lin(peer), self.mi) if outgoing else (self.mi, _lin(peer))
        return pltpu.make_async_remote_copy(
            src_ref=self.g.at[self._rows(src)], dst_ref=self.rbuf.at[dst],
            send_sem=self.sems[0].at[r - 1], recv_sem=self.sems[1].at[r - 1], device_id=peer, device_id_type=MESH)

    def _spread(self, r, outgoing):
        peer = _flip(self.me, r)
        rows = self._rows(self.mi if outgoing else _lin(peer))
        return pltpu.make_async_remote_copy(
            src_ref=self.out.at[rows], dst_ref=self.out.at[rows],
            send_sem=self.sems[2].at[r - 1], recv_sem=self.sems[3].at[r - 1], device_id=peer, device_id_type=MESH)

    def scatter(self):
        for r in range(1, NDEV):
            self._scatter(r, True).start()
        self.rbuf[self.mi] = self.g[self._rows(self.mi), :]

    def reduce(self):
        for r in range(1, NDEV):
            self._scatter(r, False).wait_recv()
        for r in range(1, NDEV):
            self._scatter(r, True).wait_send()
        tot = self.rbuf[0]
        for d in range(1, NDEV):
            tot = tot + self.rbuf[d]
        self.out[self._rows(self.mi), :] = tot
        for r in range(1, NDEV):
            self._spread(r, True).start()

    def finish(self):
        for r in range(1, NDEV):
            self._spread(r, False).wait_recv()
        for r in range(1, NDEV):
            self._spread(r, True).wait_send()


def _head_blocks(w):
    z = jnp.zeros((64, 64), w.dtype)
    groups = []
    for q in range(NQ):
        rows = [jnp.concatenate([w[4 * q + a] if a == b else z for b in range(4)], axis=1) for a in range(4)]
        groups.append(jnp.concatenate(rows, axis=0))
    return jnp.stack(groups)


def _head_unblocks(g):
    return jnp.stack([g[q, 64 * a:64 * a + 64, 64 * a:64 * a + 64] for q in range(NQ) for a in range(4)])


def _local_step(x, tgt, p, me, order, order_dw):
    s = x.shape[0]
    vec = lambda a: a.reshape(1, D)
    gmix, gffn, gfin = vec(p["norm_mix_g"]), vec(p["norm_ffn_g"]), vec(p["norm_final_g"])
    cb, ba, bx, lam = vec(p["conv_b"]), vec(p["rg_ba"]), vec(p["rg_bx"]), vec(p["rg_lambda"])
    lng, lnb = vec(p["sgu_ln_g"]), vec(p["sgu_ln_b"])
    wax = jnp.concatenate([_head_blocks(p["rg_wa"]), _head_blocks(p["rg_wx"])], axis=2).astype(BF)
    tril = jnp.tril(jnp.ones((CHUNK, CHUNK), bool))
    ws = jnp.where(tril[None], p["sgu_ws"], 0.0)
    wtr = ws.astype(BF)
    wtrt = jnp.swapaxes(ws, 1, 2).astype(BF)
    bias = jnp.repeat(p["sgu_bs"].T, GW, axis=1)
    shard = {k: p[k].astype(BF) for k in _BIG}

    proj, h1, w_in, (wpa, wpb, wo, cw) = _proj_gather_call(
        x, gmix, shard["w_in"], order, [shard["w_proj_a"], shard["w_proj_b"], shard["w_out"], p["conv_w"]])
    wpa, wpb, wo = (t.reshape(D, D) for t in (wpa, wpb, wo))
    cw = jnp.swapaxes(cw, 0, 1).reshape(4, D)
    (h, pa, pb, x1, *gates, ya, yb, mg), (wgu, wdn) = _mixer_fwd_call(
        proj, x, cw, cb, ba, bx, lam, lng, lnb, wax, wtr, bias, wpa, wpb, wo, [shard["w_gate_up"], shard["w_down"]])
    wdn = wdn.reshape(NDEV // 2, -1, D)
    nb, _, wb = w_in.shape
    fb = wgu.shape[2]
    nh = wdn.shape[0]
    dx1, dx1b, h2, act, dgu, dx2b, facc = _ffn_call(x1, tgt, gffn, gfin, wgu, wdn)
    assert nb == NDEV and 2 * nh == NDEV
    own_gu, part_gu = _dw_plain_call("dw_gate_up", me, dgu, h2, _blk3d(fb), _rows2d(D), 1, fb, D, s)
    own_dn, part_dn = _dw_plain_call("dw_down", me, act, dx2b, _blk3d(fb), _rows2d(D), 2, fb, D, s)
    (dprojb, dya, dpa, dpb, bvec, dws, dbs), (recv_dn,) = _mixer_bwd_pre_call(
        dx1b, proj, pa, pb, lng, lnb, wtr, wtrt, bias, wpa, wpb, wo, [part_dn])
    own_pa, part_pa = _dw_plain_call("dw_proj_a", me, ya, dpa, _rows2d(D), _rows2d(D), NDEV, D, D, s)
    own_pb, part_pb = _dw_plain_call("dw_proj_b", me, yb, dpb, _rows2d(D), _rows2d(D), NDEV, D, D, s)
    own_wo, part_wo = _dw_plain_call("dw_out", me, mg, dx1b, _rows2d(D), _rows2d(D), NDEV, D, D, s)
    (dproj, dx, svec, dwax), (recv_gu, recv_pa, recv_pb, recv_wo) = _mixer_bwd_seq_call(
        dprojb, dya, proj, h, x, dx1, gates, gmix, w_in, cw, lam, wax, [part_gu, part_pa, part_pb, part_wo])
    small = {
        "norm_mix_g": svec[8], "norm_ffn_g": facc[0], "norm_final_g": facc[1],
        "conv_b": svec[0], "rg_ba": svec[1], "rg_bx": svec[2], "rg_lambda": svec[3],
        "sgu_ln_g": bvec[0], "sgu_ln_b": bvec[1],
        "rg_wa": _head_unblocks(dwax[:, :, 0:QW]), "rg_wx": _head_unblocks(dwax[:, :, QW:2 * QW]),
        "sgu_ws": dws, "sgu_bs": dbs[:, ::GW].T,
    }
    packed = _pack_small(small, svec[4:8], facc[2])
    own_in, recv_in, gsum = _dw_exchange_call("dw_in", order_dw, h1, dproj, _rows2d(D), _cols2d(wb), D, wb, s, packed)
    dw = {
        "w_gate_up": (own_gu, recv_gu), "w_down": (own_dn, recv_dn), "w_proj_a": (own_pa, recv_pa),
        "w_proj_b": (own_pb, recv_pb), "w_out": (own_wo, recv_wo), "w_in": (own_in, recv_in),
    }
    return gsum, dx, dw


_BIG = ("w_in", "w_gate_up", "w_down", "w_proj_a", "w_proj_b", "w_out")
_VEC_ROWS = ("norm_mix_g", "norm_ffn_g", "norm_final_g", "conv_b", "rg_ba", "rg_bx", "rg_lambda",
             "sgu_ln_g", "sgu_ln_b", "sgu_bs")
_WEIGHTS = ("norm_mix_g", "w_in", "conv_w", "conv_b", "rg_wa", "rg_ba", "rg_wx", "rg_bx", "rg_lambda",
            "sgu_ln_g", "sgu_ln_b", "sgu_ws", "sgu_bs", "w_proj_a", "w_proj_b", "w_out", "norm_ffn_g",
            "w_gate_up", "w_down", "norm_final_g")


def _pack_small(t, conv_w, extra=None):
    extra = jnp.zeros((1, D), F32) if extra is None else extra.reshape(1, D)
    head = jnp.concatenate([t[k].reshape(1, D) for k in _VEC_ROWS] + [conv_w, extra, jnp.zeros((1, D), F32)], axis=0)
    return jnp.concatenate([head, t["rg_wa"].reshape(64, D), t["rg_wx"].reshape(64, D), t["sgu_ws"].reshape(128, D),
                            jnp.zeros((SMALL_ROWS - 272, D), F32)], axis=0)


def _unpack_small(a):
    out = {k: a[j] for j, k in enumerate(_VEC_ROWS)}
    out["conv_w"] = a[10:14]
    out["rg_wa"] = a[16:80].reshape(16, 64, 64)
    out["rg_wx"] = a[80:144].reshape(16, 64, 64)
    out["sgu_ws"] = a[144:272].reshape(NGRP, CHUNK, CHUNK)
    return out


def kernel(x, norm_mix_g, w_in, conv_w, conv_b, rg_wa, rg_ba, rg_wx, rg_bx, rg_lambda, sgu_ln_g, sgu_ln_b, sgu_ws, sgu_bs, w_proj_a, w_proj_b, w_out, norm_ffn_g, w_gate_up, w_down, norm_final_g, loss_target, m_norm_mix_g, m_w_in, m_conv_w, m_conv_b, m_rg_wa, m_rg_ba, m_rg_wx, m_rg_bx, m_rg_lambda, m_sgu_ln_g, m_sgu_ln_b, m_sgu_ws, m_sgu_bs, m_w_proj_a, m_w_proj_b, m_w_out, m_norm_ffn_g, m_w_gate_up, m_w_down, m_norm_final_g, v_norm_mix_g, v_w_in, v_conv_w, v_conv_b, v_rg_wa, v_rg_ba, v_rg_wx, v_rg_bx, v_rg_lambda, v_sgu_ln_g, v_sgu_ln_b, v_sgu_ws, v_sgu_bs, v_w_proj_a, v_w_proj_b, v_w_out, v_norm_ffn_g, v_w_gate_up, v_w_down, v_norm_final_g):
    args = dict(locals())
    w = {k: args[k] for k in _WEIGHTS}
    m = {k: args["m_" + k] for k in _WEIGHTS}
    v = {k: args["v_" + k] for k in _WEIGHTS}
    for d in (w, m, v):
        for k in _WEIGHTS:
            if k != "norm_final_g":
                d[k] = d[k][0]
    me = _lin(_me())
    me1 = me.reshape(1).astype(jnp.int32)

    order = jnp.bitwise_xor(me, jnp.array(_PASS_FLIPS, jnp.int32)).astype(jnp.int32)
    order_dw = jnp.bitwise_xor(me, jnp.array(_DW_FLIPS, jnp.int32)).astype(jnp.int32)

    gsum, dx, dw = _local_step(x[0], loss_target[0], w, me1, order, order_dw)

    peer = jnp.bitwise_xor(jnp.arange(NDEV, dtype=jnp.int32), me)
    sel_direct = jnp.where(peer == 0, 2, 1).astype(jnp.int32)
    sel_two_level = jnp.where(peer == 0, 2, jnp.where((peer == 1) | (peer % 2 == 0), 1, 0)).astype(jnp.int32)
    grads, delta, new_m, new_v = {}, {}, {}, {}
    for k in _BIG:
        own, recv = dw[k]
        flip = own.shape != w[k].shape
        wmv = [jnp.swapaxes(t, 0, 1) if flip else t for t in (w[k], m[k], v[k])]
        res = _reduce_adamw_call("adamw_" + k, sel_two_level if k == "w_in" else sel_direct, own, recv, *wmv)
        grads[k], delta[k], new_m[k], new_v[k] = (jnp.swapaxes(t, 0, 1) if flip else t for t in res)

    loss = (0.5 / D) * jnp.sum(gsum[14])
    zc = jnp.zeros((4, D), F32)
    d_s, m_s, v_s = _adamw_call("adamw_small", _pack_small(w, zc), gsum, _pack_small(m, zc), _pack_small(v, zc))
    gs, ds, ms, vs = _unpack_small(gsum), _unpack_small(d_s), _unpack_small(m_s), _unpack_small(v_s)
    g_cw = lax.dynamic_slice(gs["conv_w"], (0, me * 128), (4, 128))
    ds["conv_w"], ms["conv_w"], vs["conv_w"] = _adamw_call("adamw_conv_w", w["conv_w"], g_cw, m["conv_w"], v["conv_w"])
    gs["conv_w"] = g_cw
    for k in _WEIGHTS:
        if k not in _BIG:
            shp = w[k].shape
            grads[k], delta[k], new_m[k], new_v[k] = (t[k].reshape(shp) for t in (gs, ds, ms, vs))

    def lift(t, k):
        return t[k] if k == "norm_final_g" else t[k][None]

    outs = [loss, dx[None]]
    for t in (grads, delta, new_m, new_v):
        outs += [lift(t, k) for k in _WEIGHTS]
    return tuple(outs)
```
